```python
import math
import jax
import jax.numpy as jnp
from jax import lax
import numpy as np

D_MODEL = 1024
BATCH = 8
SEQ = 4096
DEPTH = 4

DN_HEADS = 8
DN_HEAD_DIM = 128
DN_WIDTH = DN_HEADS * DN_HEAD_DIM
DN_CONV = 5
DN_CHUNK = 64
SW_HEADS = 16
SW_KV_HEADS = 4
SW_HEAD_DIM = 64
SW_WINDOW = 128
SW_BLOCK = 128
ROPE_THETA = 10000.0
FFN_HIDDEN = ((8 * D_MODEL // 3 + 255) // 256) * 256
DN_ALPHA = (2.0 * DEPTH) ** 0.25
DN_BETA = (8.0 * DEPTH) ** -0.25
LN_EPS = 1e-5
RMS_EPS = 1e-6

COL_SIZES = (
    3 * DN_WIDTH,
    DN_WIDTH,
    2 * DN_HEADS,
    2 * DN_HEADS,
    SW_HEADS * SW_HEAD_DIM,
    SW_KV_HEADS * SW_HEAD_DIM,
    SW_KV_HEADS * SW_HEAD_DIM,
    2 * D_MODEL,
)
IN_COLS = sum(COL_SIZES)

kernel_name = "hybrid_deltanet_swa_deepnorm_encoder"


def layer_norm(x, g, b):
    xf = x.astype(jnp.float32)
    mu = jnp.mean(xf, axis=-1, keepdims=True)
    var = jnp.mean(jnp.square(xf - mu), axis=-1, keepdims=True)
    y = (xf - mu) * lax.rsqrt(var + LN_EPS) * g.astype(jnp.float32) + b.astype(jnp.float32)
    return y.astype(x.dtype)


def l2_normalize(x):
    xf = x.astype(jnp.float32)
    return xf * lax.rsqrt(jnp.sum(xf * xf, axis=-1, keepdims=True) + RMS_EPS)


def depthwise_conv_centred(x, w):
    c = x.shape[-1]
    half = DN_CONV // 2
    return lax.conv_general_dilated(
        x, w[:, None, :].astype(x.dtype), window_strides=(1,),
        padding=[(half, half)], dimension_numbers=("NWC", "WIO", "NWC"),
        feature_group_count=c)


def rope_tables(seq_len):
    half = SW_HEAD_DIM // 2
    inv_freq = ROPE_THETA ** (-jnp.arange(half, dtype=jnp.float32) / half)
    pos = jnp.arange(seq_len, dtype=jnp.float32)
    ang = pos[:, None] * inv_freq[None, :]
    return jnp.cos(ang), jnp.sin(ang)


def apply_rope(x, cos, sin):
    x1, x2 = jnp.split(x, 2, axis=-1)
    c = cos[None, :, None, :].astype(x.dtype)
    s = sin[None, :, None, :].astype(x.dtype)
    return jnp.concatenate([x1 * c - x2 * s, x2 * c + x1 * s], axis=-1)


def chunk_gated_delta_rule(q, k, v, g, beta):
    b_, t_, h_, dk = q.shape
    dv = v.shape[-1]
    n_chunks = t_ // DN_CHUNK
    f32 = jnp.float32

    def chunks(a):
        a = a.astype(f32).reshape((b_, n_chunks, DN_CHUNK, h_) + a.shape[3:])
        return jnp.moveaxis(a, 3, 1)

    q, k, v, beta = chunks(q), chunks(k), chunks(v), chunks(beta)
    g = jnp.cumsum(chunks(g), axis=-1)
    idx = jnp.arange(DN_CHUNK)
    lower_incl = idx[:, None] >= idx[None, :]
    strict = idx[:, None] > idx[None, :]
    decay = jnp.exp(jnp.where(lower_incl, g[..., :, None] - g[..., None, :], -jnp.inf))

    kk = jnp.einsum("bhnid,bhnjd->bhnij", k, k)
    eye = jnp.eye(DN_CHUNK, dtype=f32)
    lmat = jnp.where(strict, beta[..., :, None] * kk * decay, 0.0) + eye
    t_inv = lax.linalg.triangular_solve(
        lmat, jnp.broadcast_to(eye, lmat.shape), left_side=True, lower=True,
        unit_diagonal=True)
    u = jnp.einsum("bhnij,bhnjd->bhnid", t_inv, v * beta[..., None])
    w = jnp.einsum("bhnij,bhnjd->bhnid", t_inv, k * (beta * jnp.exp(g))[..., None])
    qk = jnp.einsum("bhnid,bhnjd->bhnij", q, k) * decay
    q_dec = q * jnp.exp(g)[..., None]
    g_last = g[..., -1]
    k_dec = k * jnp.exp(g_last[..., None] - g)[..., None]

    xs = tuple(jnp.moveaxis(a, 2, 0) for a in (u, w, qk, q_dec, k_dec, g_last))

    def step(state, inp):
        u_n, w_n, qk_n, qd_n, kd_n, gl_n = inp
        v_new = u_n - jnp.einsum("bhcd,bhde->bhce", w_n, state)
        o_n = (jnp.einsum("bhcd,bhde->bhce", qd_n, state)
               + jnp.einsum("bhij,bhje->bhie", qk_n, v_new))
        state = (state * jnp.exp(gl_n)[..., None, None]
                 + jnp.einsum("bhcd,bhce->bhde", kd_n, v_new))
        return state, o_n

    s0 = jnp.zeros((b_, h_, dk, dv), f32)
    _, o = lax.scan(step, s0, xs)
    return jnp.transpose(o, (1, 0, 3, 2, 4)).reshape(b_, t_, h_, dv)


def windowed_gqa_sink(q, k, v, sinks):
    b_, t_, hq, d = q.shape
    hkv = k.shape[2]
    grp = hq // hkv
    nb = t_ // SW_BLOCK
    pad = ((0, 0), (SW_BLOCK, SW_BLOCK), (0, 0), (0, 0))

    def bands(a):
        ab = jnp.pad(a, pad).reshape(b_, nb + 2, SW_BLOCK, hkv, d)
        return jnp.concatenate([ab[:, :-2], ab[:, 1:-1], ab[:, 2:]], axis=2)

    kb, vb = bands(k), bands(v)
    qb = q.reshape(b_, nb, SW_BLOCK, hkv, grp, d)
    s = jnp.einsum("bnqhgd,bnkhd->bnhgqk", qb, kb).astype(jnp.float32) * (d ** -0.5)
    blk = jnp.arange(nb)[:, None, None] * SW_BLOCK
    qpos = blk + jnp.arange(SW_BLOCK)[None, :, None]
    kpos = blk - SW_BLOCK + jnp.arange(3 * SW_BLOCK)[None, None, :]
    valid = (jnp.abs(qpos - kpos) <= SW_WINDOW) & (kpos >= 0) & (kpos < t_)
    s = jnp.where(valid[None, :, None, None], s, -jnp.inf)
    sink = jnp.broadcast_to(
        sinks.astype(jnp.float32).reshape(hkv, grp)[None, None, :, :, None, None],
        s.shape[:-1] + (1,))
    p = jax.nn.softmax(jnp.concatenate([s, sink], axis=-1), axis=-1)[..., :-1]
    o = jnp.einsum("bnhgqk,bnkhd->bnqhgd", p.astype(v.dtype), vb)
    return o.reshape(b_, t_, hq * d)


def token_mixing(x, w_in, conv_w, a_log, dt_bias, dn_norm_w, sinks,
                 w_branch_a, w_branch_b, w_out, cos, sin):
    b_, t_, _ = x.shape
    proj = jnp.einsum("btd,de->bte", x, w_in)
    offs = [sum(COL_SIZES[:i + 1]) for i in range(len(COL_SIZES) - 1)]
    qkv_dn, z, b_dn, a_dn, q_sw, k_sw, v_sw, gates = jnp.split(proj, offs, axis=-1)

    qkv_dn = jax.nn.silu(depthwise_conv_centred(qkv_dn, conv_w))
    q_dn, k_dn, v_dn = [a.reshape(b_, t_, DN_HEADS, DN_HEAD_DIM)
                        for a in jnp.split(qkv_dn, 3, axis=-1)]
    q_dn = l2_normalize(q_dn) * (DN_HEAD_DIM ** -0.5)
    k_dn = l2_normalize(k_dn)
    beta = jax.nn.sigmoid(b_dn.astype(jnp.float32)).reshape(b_, t_, 2, DN_HEADS)
    g = -jnp.exp(a_log.astype(jnp.float32)) * jax.nn.softplus(
        a_dn.astype(jnp.float32).reshape(b_, t_, 2, DN_HEADS) + dt_bias.astype(jnp.float32))
    o_fwd = chunk_gated_delta_rule(q_dn, k_dn, v_dn, g[:, :, 0], beta[:, :, 0])
    flip = lambda a: jnp.flip(a, axis=1)
    o_bwd = flip(chunk_gated_delta_rule(flip(q_dn), flip(k_dn), flip(v_dn),
                                        flip(g[:, :, 1]), flip(beta[:, :, 1])))
    o_dn = o_fwd + o_bwd
    o_dn = (o_dn * lax.rsqrt(jnp.mean(o_dn * o_dn, axis=-1, keepdims=True) + RMS_EPS)
            * dn_norm_w.astype(jnp.float32)
            * jax.nn.silu(z.astype(jnp.float32)).reshape(b_, t_, DN_HEADS, DN_HEAD_DIM))
    o_dn = o_dn.reshape(b_, t_, DN_WIDTH).astype(x.dtype)

    q_sw = apply_rope(q_sw.reshape(b_, t_, SW_HEADS, SW_HEAD_DIM), cos, sin)
    k_sw = apply_rope(k_sw.reshape(b_, t_, SW_KV_HEADS, SW_HEAD_DIM), cos, sin)
    v_sw = v_sw.reshape(b_, t_, SW_KV_HEADS, SW_HEAD_DIM)
    o_sw = windowed_gqa_sink(q_sw, k_sw, v_sw, sinks)

    gate_a, gate_b = jnp.split(jax.nn.sigmoid(gates), 2, axis=-1)
    merged = (gate_a * jnp.einsum("bte,ed->btd", o_dn, w_branch_a)
              + gate_b * jnp.einsum("bte,ed->btd", o_sw, w_branch_b))
    return jnp.einsum("btd,de->bte", merged, w_out)


def swiglu_ffn(x, w_gate_up, w_down):
    gu = jnp.einsum("btd,df->btf", x, w_gate_up)
    gate, up = jnp.split(gu, 2, axis=-1)
    return jnp.einsum("btf,fd->btd", jax.nn.silu(gate) * up, w_down)


def _fwd_setup_inputs(seed: int = 0) -> dict:
    key = jax.random.key(seed)
    ks = jax.random.split(key, 20)
    f32 = jnp.float32
    nrm = lambda k, shape, scale: jax.random.normal(k, shape, f32) * scale
    x = jax.random.normal(ks[0], (BATCH, SEQ, D_MODEL), f32)
    w_in = nrm(ks[1], (DEPTH, D_MODEL, IN_COLS), D_MODEL ** -0.5)
    conv_w = nrm(ks[2], (DEPTH, DN_CONV, 3 * DN_WIDTH), DN_CONV ** -0.5)
    a_log = jnp.log(jax.random.uniform(ks[3], (DEPTH, 2, DN_HEADS), f32, 1.0, 16.0))
    dt = jnp.exp(jax.random.uniform(ks[4], (DEPTH, 2, DN_HEADS), f32,
                                    math.log(1e-3), math.log(1e-1)))
    dt_bias = dt + jnp.log(-jnp.expm1(-dt))
    dn_norm_w = 1.0 + nrm(ks[5], (DEPTH, DN_HEAD_DIM), 0.02)
    sinks = nrm(ks[6], (DEPTH, SW_HEADS), 0.5)
    w_branch_a = nrm(ks[7], (DEPTH, DN_WIDTH, D_MODEL), DN_WIDTH ** -0.5)
    w_branch_b = nrm(ks[8], (DEPTH, SW_HEADS * SW_HEAD_DIM, D_MODEL),
                     (SW_HEADS * SW_HEAD_DIM) ** -0.5)
    w_out = nrm(ks[9], (DEPTH, D_MODEL, D_MODEL), D_MODEL ** -0.5 * DN_BETA)
    ln1_g = 1.0 + nrm(ks[10], (DEPTH, D_MODEL), 0.02)
    ln1_b = nrm(ks[11], (DEPTH, D_MODEL), 0.02)
    w_gate_up = nrm(ks[12], (DEPTH, D_MODEL, 2 * FFN_HIDDEN), D_MODEL ** -0.5)
    w_down = nrm(ks[13], (DEPTH, FFN_HIDDEN, D_MODEL), FFN_HIDDEN ** -0.5 * DN_BETA)
    ln2_g = 1.0 + nrm(ks[14], (DEPTH, D_MODEL), 0.02)
    ln2_b = nrm(ks[15], (DEPTH, D_MODEL), 0.02)
    return {"x": x, "w_in": w_in, "conv_w": conv_w, "a_log": a_log,
            "dt_bias": dt_bias, "dn_norm_w": dn_norm_w, "sinks": sinks,
            "w_branch_a": w_branch_a, "w_branch_b": w_branch_b, "w_out": w_out,
            "ln1_g": ln1_g, "ln1_b": ln1_b, "w_gate_up": w_gate_up,
            "w_down": w_down, "ln2_g": ln2_g, "ln2_b": ln2_b}


def _fwd_reference(x, w_in, conv_w, a_log, dt_bias, dn_norm_w, sinks, w_branch_a,
              w_branch_b, w_out, ln1_g, ln1_b, w_gate_up, w_down, ln2_g, ln2_b):
    cos, sin = rope_tables(x.shape[1])
    for l in range(DEPTH):
        mix = token_mixing(x, w_in[l], conv_w[l], a_log[l], dt_bias[l], dn_norm_w[l],
                           sinks[l], w_branch_a[l], w_branch_b[l], w_out[l], cos, sin)
        x = layer_norm(DN_ALPHA * x + mix, ln1_g[l], ln1_b[l])
        x = layer_norm(DN_ALPHA * x + swiglu_ffn(x, w_gate_up[l], w_down[l]), ln2_g[l], ln2_b[l])
    return x


import jax as _jax
import jax.numpy as _jnp

TWIN_FORMAT = 'train_step'
FWD_PARAMS = ['x', 'w_in', 'conv_w', 'a_log', 'dt_bias', 'dn_norm_w', 'sinks', 'w_branch_a', 'w_branch_b', 'w_out', 'ln1_g', 'ln1_b', 'w_gate_up', 'w_down', 'ln2_g', 'ln2_b']
TWIN_WEIGHTS = ['w_in', 'conv_w', 'a_log', 'dt_bias', 'dn_norm_w', 'sinks', 'w_branch_a', 'w_branch_b', 'w_out', 'ln1_g', 'ln1_b', 'w_gate_up', 'w_down', 'ln2_g', 'ln2_b']
TWIN_DIFF_INPUT = 'x'
TWIN_INPUTS = ['x', 'w_in', 'conv_w', 'a_log', 'dt_bias', 'dn_norm_w', 'sinks', 'w_branch_a', 'w_branch_b', 'w_out', 'ln1_g', 'ln1_b', 'w_gate_up', 'w_down', 'ln2_g', 'ln2_b', 'loss_target', 'm_w_in', 'm_conv_w', 'm_a_log', 'm_dt_bias', 'm_dn_norm_w', 'm_sinks', 'm_w_branch_a', 'm_w_branch_b', 'm_w_out', 'm_ln1_g', 'm_ln1_b', 'm_w_gate_up', 'm_w_down', 'm_ln2_g', 'm_ln2_b', 'v_w_in', 'v_conv_w', 'v_a_log', 'v_dt_bias', 'v_dn_norm_w', 'v_sinks', 'v_w_branch_a', 'v_w_branch_b', 'v_w_out', 'v_ln1_g', 'v_ln1_b', 'v_w_gate_up', 'v_w_down', 'v_ln2_g', 'v_ln2_b']
TWIN_OUTPUTS = ['loss', 'grad_x', 'grad_w_in', 'grad_conv_w', 'grad_a_log', 'grad_dt_bias', 'grad_dn_norm_w', 'grad_sinks', 'grad_w_branch_a', 'grad_w_branch_b', 'grad_w_out', 'grad_ln1_g', 'grad_ln1_b', 'grad_w_gate_up', 'grad_w_down', 'grad_ln2_g', 'grad_ln2_b', 'delta_w_in', 'delta_conv_w', 'delta_a_log', 'delta_dt_bias', 'delta_dn_norm_w', 'delta_sinks', 'delta_w_branch_a', 'delta_w_branch_b', 'delta_w_out', 'delta_ln1_g', 'delta_ln1_b', 'delta_w_gate_up', 'delta_w_down', 'delta_ln2_g', 'delta_ln2_b', 'new_m_w_in', 'new_m_conv_w', 'new_m_a_log', 'new_m_dt_bias', 'new_m_dn_norm_w', 'new_m_sinks', 'new_m_w_branch_a', 'new_m_w_branch_b', 'new_m_w_out', 'new_m_ln1_g', 'new_m_ln1_b', 'new_m_w_gate_up', 'new_m_w_down', 'new_m_ln2_g', 'new_m_ln2_b', 'new_v_w_in', 'new_v_conv_w', 'new_v_a_log', 'new_v_dt_bias', 'new_v_dn_norm_w', 'new_v_sinks', 'new_v_w_branch_a', 'new_v_w_branch_b', 'new_v_w_out', 'new_v_ln1_g', 'new_v_ln1_b', 'new_v_w_gate_up', 'new_v_w_down', 'new_v_ln2_g', 'new_v_ln2_b']
TWIN_LEAF_KINDS = {'loss': 'loss', 'grad_x': 'grad_x', 'grad_w_in': 'grad_w', 'grad_conv_w': 'grad_w', 'grad_a_log': 'grad_w', 'grad_dt_bias': 'grad_w', 'grad_dn_norm_w': 'grad_w', 'grad_sinks': 'grad_w', 'grad_w_branch_a': 'grad_w', 'grad_w_branch_b': 'grad_w', 'grad_w_out': 'grad_w', 'grad_ln1_g': 'grad_w', 'grad_ln1_b': 'grad_w', 'grad_w_gate_up': 'grad_w', 'grad_w_down': 'grad_w', 'grad_ln2_g': 'grad_w', 'grad_ln2_b': 'grad_w', 'delta_w_in': 'delta_w', 'delta_conv_w': 'delta_w', 'delta_a_log': 'delta_w', 'delta_dt_bias': 'delta_w', 'delta_dn_norm_w': 'delta_w', 'delta_sinks': 'delta_w', 'delta_w_branch_a': 'delta_w', 'delta_w_branch_b': 'delta_w', 'delta_w_out': 'delta_w', 'delta_ln1_g': 'delta_w', 'delta_ln1_b': 'delta_w', 'delta_w_gate_up': 'delta_w', 'delta_w_down': 'delta_w', 'delta_ln2_g': 'delta_w', 'delta_ln2_b': 'delta_w', 'new_m_w_in': 'new_m', 'new_m_conv_w': 'new_m', 'new_m_a_log': 'new_m', 'new_m_dt_bias': 'new_m', 'new_m_dn_norm_w': 'new_m', 'new_m_sinks': 'new_m', 'new_m_w_branch_a': 'new_m', 'new_m_w_branch_b': 'new_m', 'new_m_w_out': 'new_m', 'new_m_ln1_g': 'new_m', 'new_m_ln1_b': 'new_m', 'new_m_w_gate_up': 'new_m', 'new_m_w_down': 'new_m', 'new_m_ln2_g': 'new_m', 'new_m_ln2_b': 'new_m', 'new_v_w_in': 'new_v', 'new_v_conv_w': 'new_v', 'new_v_a_log': 'new_v', 'new_v_dt_bias': 'new_v', 'new_v_dn_norm_w': 'new_v', 'new_v_sinks': 'new_v', 'new_v_w_branch_a': 'new_v', 'new_v_w_branch_b': 'new_v', 'new_v_w_out': 'new_v', 'new_v_ln1_g': 'new_v', 'new_v_ln1_b': 'new_v', 'new_v_w_gate_up': 'new_v', 'new_v_w_down': 'new_v', 'new_v_ln2_g': 'new_v', 'new_v_ln2_b': 'new_v'}


def _forward(args):
    return _fwd_reference(*[args[k] for k in FWD_PARAMS])


def _output_shape():
    def fwd():
        inp = _fwd_setup_inputs(0)
        return _fwd_reference(*[inp[k] for k in FWD_PARAMS])
    out = _jax.eval_shape(fwd)
    return out.shape, out.dtype

N_MICROBATCH = 1
ADAM_LR = 0.001
ADAM_B1 = 0.9
ADAM_B2 = 0.999
ADAM_EPS = 1e-08
ADAM_WD = 0.01
ADAM_STEP = 10
PER_EXAMPLE_BATCH_AXIS = {'x': 0, 'loss_target': 0}
SHARED_INPUTS = []
_WEIGHT_DTYPES = {'w_in': _jnp.float32, 'conv_w': _jnp.float32, 'a_log': _jnp.float32, 'dt_bias': _jnp.float32, 'dn_norm_w': _jnp.float32, 'sinks': _jnp.float32, 'w_branch_a': _jnp.float32, 'w_branch_b': _jnp.float32, 'w_out': _jnp.float32, 'ln1_g': _jnp.float32, 'ln1_b': _jnp.float32, 'w_gate_up': _jnp.float32, 'w_down': _jnp.float32, 'ln2_g': _jnp.float32, 'ln2_b': _jnp.float32}
MOMENT_SCALE = {'w_in': 8.986173e-03, 'conv_w': 1.099976e-02, 'a_log': 3.351904e-02, 'dt_bias': 3.306855e-02, 'dn_norm_w': 4.592956e-02, 'sinks': 1.136121e-04, 'w_branch_a': 1.681422e-02, 'w_branch_b': 3.366571e-03, 'w_out': 4.072385e-02, 'ln1_g': 1.163568e+00, 'ln1_b': 5.563163e-01, 'w_gate_up': 1.646203e-02, 'w_down': 6.391515e-02, 'ln2_g': 1.611109e+01, 'ln2_b': 1.120423e+00}


def _to_microbatches(a, axis):
    t = _jnp.moveaxis(a, axis, 0)
    t = t.reshape((N_MICROBATCH, t.shape[0] // N_MICROBATCH) + t.shape[1:])
    return _jnp.moveaxis(t, 1, axis + 1)


def setup_inputs(seed: int = 0) -> dict:
    inp = _fwd_setup_inputs(seed)
    key = _jax.random.fold_in(_jax.random.key(seed), 7919)
    shape, _ = _output_shape()
    out = dict(inp)
    out["loss_target"] = _jax.random.normal(_jax.random.fold_in(key, 0), shape, _jnp.float32)
    for i, name in enumerate(TWIN_WEIGHTS):
        w = inp[name].astype(_jnp.float32)
        if MOMENT_SCALE is None:
            s = _jnp.sqrt(_jnp.mean(_jnp.square(w)) + 1e-30)
        else:
            s = MOMENT_SCALE[name]
        km, kv = _jax.random.split(_jax.random.fold_in(key, i + 1))
        out[name] = w
        out["m_" + name] = s * _jax.random.normal(km, w.shape, _jnp.float32)
        out["v_" + name] = (s * s) * _jax.random.uniform(kv, w.shape, _jnp.float32, 0.5, 1.5)
    if N_MICROBATCH > 1:
        for name, axis in PER_EXAMPLE_BATCH_AXIS.items():
            out[name] = _to_microbatches(out[name], axis)
    return {'x': out['x'], 'w_in': out['w_in'], 'conv_w': out['conv_w'], 'a_log': out['a_log'], 'dt_bias': out['dt_bias'], 'dn_norm_w': out['dn_norm_w'], 'sinks': out['sinks'], 'w_branch_a': out['w_branch_a'], 'w_branch_b': out['w_branch_b'], 'w_out': out['w_out'], 'ln1_g': out['ln1_g'], 'ln1_b': out['ln1_b'], 'w_gate_up': out['w_gate_up'], 'w_down': out['w_down'], 'ln2_g': out['ln2_g'], 'ln2_b': out['ln2_b'], 'loss_target': out['loss_target'], 'm_w_in': out['m_w_in'], 'm_conv_w': out['m_conv_w'], 'm_a_log': out['m_a_log'], 'm_dt_bias': out['m_dt_bias'], 'm_dn_norm_w': out['m_dn_norm_w'], 'm_sinks': out['m_sinks'], 'm_w_branch_a': out['m_w_branch_a'], 'm_w_branch_b': out['m_w_branch_b'], 'm_w_out': out['m_w_out'], 'm_ln1_g': out['m_ln1_g'], 'm_ln1_b': out['m_ln1_b'], 'm_w_gate_up': out['m_w_gate_up'], 'm_w_down': out['m_w_down'], 'm_ln2_g': out['m_ln2_g'], 'm_ln2_b': out['m_ln2_b'], 'v_w_in': out['v_w_in'], 'v_conv_w': out['v_conv_w'], 'v_a_log': out['v_a_log'], 'v_dt_bias': out['v_dt_bias'], 'v_dn_norm_w': out['v_dn_norm_w'], 'v_sinks': out['v_sinks'], 'v_w_branch_a': out['v_w_branch_a'], 'v_w_branch_b': out['v_w_branch_b'], 'v_w_out': out['v_w_out'], 'v_ln1_g': out['v_ln1_g'], 'v_ln1_b': out['v_ln1_b'], 'v_w_gate_up': out['v_w_gate_up'], 'v_w_down': out['v_w_down'], 'v_ln2_g': out['v_ln2_g'], 'v_ln2_b': out['v_ln2_b']}


def _loss(weights, diff, rest, loss_target):
    with _jax.named_scope("forward"):
        args = {**rest, TWIN_DIFF_INPUT: diff, **{k: w.astype(_WEIGHT_DTYPES[k]) for k, w in weights.items()}}
        y = _forward(args)
    with _jax.named_scope("loss_head"):
        err = _jnp.square(y.astype(_jnp.float32) - loss_target)
        return 0.5 * _jnp.sum(_jnp.mean(err, axis=-1)) if err.ndim else 0.5 * err


def _adamw(w, g, m, v):
    m = ADAM_B1 * m + (1.0 - ADAM_B1) * g
    v = ADAM_B2 * v + (1.0 - ADAM_B2) * _jnp.square(g)
    m_hat = m / (1.0 - ADAM_B1 ** ADAM_STEP)
    v_hat = v / (1.0 - ADAM_B2 ** ADAM_STEP)
    delta = -ADAM_LR * (m_hat / (_jnp.sqrt(v_hat) + ADAM_EPS) + ADAM_WD * w)
    return delta, m, v


def reference(x, w_in, conv_w, a_log, dt_bias, dn_norm_w, sinks, w_branch_a, w_branch_b, w_out, ln1_g, ln1_b, w_gate_up, w_down, ln2_g, ln2_b, loss_target, m_w_in, m_conv_w, m_a_log, m_dt_bias, m_dn_norm_w, m_sinks, m_w_branch_a, m_w_branch_b, m_w_out, m_ln1_g, m_ln1_b, m_w_gate_up, m_w_down, m_ln2_g, m_ln2_b, v_w_in, v_conv_w, v_a_log, v_dt_bias, v_dn_norm_w, v_sinks, v_w_branch_a, v_w_branch_b, v_w_out, v_ln1_g, v_ln1_b, v_w_gate_up, v_w_down, v_ln2_g, v_ln2_b):
    given = dict(x=x, w_in=w_in, conv_w=conv_w, a_log=a_log, dt_bias=dt_bias, dn_norm_w=dn_norm_w, sinks=sinks, w_branch_a=w_branch_a, w_branch_b=w_branch_b, w_out=w_out, ln1_g=ln1_g, ln1_b=ln1_b, w_gate_up=w_gate_up, w_down=w_down, ln2_g=ln2_g, ln2_b=ln2_b, loss_target=loss_target, m_w_in=m_w_in, m_conv_w=m_conv_w, m_a_log=m_a_log, m_dt_bias=m_dt_bias, m_dn_norm_w=m_dn_norm_w, m_sinks=m_sinks, m_w_branch_a=m_w_branch_a, m_w_branch_b=m_w_branch_b, m_w_out=m_w_out, m_ln1_g=m_ln1_g, m_ln1_b=m_ln1_b, m_w_gate_up=m_w_gate_up, m_w_down=m_w_down, m_ln2_g=m_ln2_g, m_ln2_b=m_ln2_b, v_w_in=v_w_in, v_conv_w=v_conv_w, v_a_log=v_a_log, v_dt_bias=v_dt_bias, v_dn_norm_w=v_dn_norm_w, v_sinks=v_sinks, v_w_branch_a=v_w_branch_a, v_w_branch_b=v_w_branch_b, v_w_out=v_w_out, v_ln1_g=v_ln1_g, v_ln1_b=v_ln1_b, v_w_gate_up=v_w_gate_up, v_w_down=v_w_down, v_ln2_g=v_ln2_g, v_ln2_b=v_ln2_b)
    weights = {n: given[n] for n in TWIN_WEIGHTS}
    shared = {n: given[n] for n in SHARED_INPUTS}
    per_example = {n: given[n] for n in ['x']}
    grad_fn = _jax.value_and_grad(_loss, argnums=(0, 1))

    def one_microbatch(ex, loss_target):
        ex = dict(ex)
        diff = ex.pop(TWIN_DIFF_INPUT)
        return grad_fn(weights, diff, {**shared, **ex}, loss_target)

    if N_MICROBATCH == 1:
        loss, (grad_w, grad_x) = one_microbatch(per_example, given["loss_target"])
    else:
        def body(carry, xs):
            loss_sum, grad_sum = carry
            l_k, (gw_k, gx_k) = one_microbatch(xs[0], xs[1])
            with _jax.named_scope("update"):
                return (loss_sum + l_k, _jax.tree.map(_jnp.add, grad_sum, gw_k)), gx_k

        init = (_jnp.zeros((), _jnp.float32), _jax.tree.map(_jnp.zeros_like, weights))
        (loss, grad_w), grad_x = _jax.lax.scan(body, init, (per_example, given["loss_target"]))
    with _jax.named_scope("update"):
        delta_w, new_m, new_v = {}, {}, {}
        for n in TWIN_WEIGHTS:
            delta_w[n], new_m[n], new_v[n] = _adamw(weights[n], grad_w[n], given["m_" + n], given["v_" + n])
    return (loss, grad_x, *[grad_w[n] for n in TWIN_WEIGHTS], *[delta_w[n] for n in TWIN_WEIGHTS],
            *[new_m[n] for n in TWIN_WEIGHTS], *[new_v[n] for n in TWIN_WEIGHTS])
```

```python
import functools

import jax
import jax.numpy as jnp
from jax import lax
from jax.experimental import pallas as pl
from jax.experimental.pallas import tpu as pltpu

F32 = jnp.float32
_MXU = jnp.bfloat16
_HI = lax.Precision.HIGHEST

N_DEV = 8
D_MODEL = 1024
DEPTH = 4
DN_HEADS = 8
DN_HEAD_DIM = 128
DN_WIDTH = DN_HEADS * DN_HEAD_DIM
DN_CONV = 5
DN_CHUNK = 64
SW_HEADS = 16
SW_KV_HEADS = 4
SW_HEAD_DIM = 64
SW_GROUP = SW_HEADS // SW_KV_HEADS
SW_BLOCK = 128
SW_KV_WIDTH = SW_KV_HEADS * SW_HEAD_DIM
ROPE_THETA = 10000.0
FFN_HIDDEN = 2816
DN_ALPHA = (2.0 * DEPTH) ** 0.25
LN_EPS = 1e-5
RMS_EPS = 1e-6
ADAM_LR = 0.001
ADAM_B1 = 0.9
ADAM_B2 = 0.999
ADAM_EPS = 1e-08
ADAM_WD = 0.01
ADAM_STEP = 10

LANES = 128
N_HD = 2 * DN_HEADS
LOCAL_ROWS = 256
REC_ROWS = 512
ROW_TILE = 256
VMEM_LIMIT = 48 << 20

C_QKV, C_Z, C_QSW, C_GA, C_GB, C_KSW, C_VSW = 0, 3072, 4096, 5120, 6144, 7168, 7424
MAIN_COLS = 7680
R_QKV, R_Z, R_BA, R_QSW, R_KSW, R_VSW, R_G = 0, 3072, 4096, 4128, 5152, 5408, 5664
IN_COLS = 7712


_NN = ((1,), (0,))
_NT = ((1,), (1,))
_TN = ((0,), (0,))


def _dg(a, b, dims, hi):
    if hi:
        return lax.dot_general(a, b, (dims, ((), ())), precision=_HI, preferred_element_type=F32)
    return lax.dot_general(a.astype(_MXU), b.astype(_MXU), (dims, ((), ())), preferred_element_type=F32)


def _make_dots(hi):
    @jax.custom_vjp
    def nn(a, b):
        return _dg(a, b, _NN, hi)

    @jax.custom_vjp
    def nt(a, b):
        return _dg(a, b, _NT, hi)

    @jax.custom_vjp
    def tn(a, b):
        return _dg(a, b, _TN, hi)

    nn.defvjp(lambda a, b: (nn(a, b), (a, b)), lambda r, g: (nt(g, r[1]), tn(r[0], g)))
    nt.defvjp(lambda a, b: (nt(a, b), (a, b)), lambda r, g: (nn(g, r[1]), tn(g, r[0])))
    tn.defvjp(lambda a, b: (tn(a, b), (a, b)), lambda r, g: (nt(r[1], g), nn(r[0], g)))
    return nn, nt, tn


_bnn, _bnt, _btn = _make_dots(False)
_hnn, _hnt, _htn = _make_dots(True)


@jax.custom_vjp
def _inv_unit(a):
    n = a.shape[0]
    eye = (lax.broadcasted_iota(jnp.int32, (n, n), 0) == lax.broadcasted_iota(jnp.int32, (n, n), 1)).astype(F32)
    inv = eye - a
    p = a
    span = 2
    while span < n:
        p = _hnn(p, p)
        inv = inv + _hnn(inv, p)
        span *= 2
    return inv


def _inv_unit_fwd(a):
    t = _inv_unit(a)
    return t, t


def _inv_unit_bwd(t, g):
    return (-_hnt(_htn(t, g), t),)


_inv_unit.defvjp(_inv_unit_fwd, _inv_unit_bwd)


def _silu(x):
    return x * jax.nn.sigmoid(x)


def _softplus(x):
    return jnp.maximum(x, 0.0) + jnp.log1p(jnp.exp(-jnp.abs(x)))


def _params(sem=None):
    kw = {"vmem_limit_bytes": VMEM_LIMIT}
    if sem is not None:
        kw["dimension_semantics"] = sem
    return pltpu.CompilerParams(**kw)


def _tile(dim, pref):
    if dim <= pref:
        return dim
    t = (pref // LANES) * LANES
    while t > LANES and dim % t:
        t -= LANES
    assert dim % t == 0, (dim, pref)
    return t


def _full(shape):
    zeros = (0,) * len(shape)
    return pl.BlockSpec(shape, lambda *_: zeros)


def _sds(shape, dtype=F32):
    return jax.ShapeDtypeStruct(shape, dtype)


def _mm(a, b, mode, *, name, add=None, tm=512, tn=512, tk=1536):
    if mode == "nn":
        (m, k), (k2, n) = a.shape, b.shape
    elif mode == "nt":
        (m, k), (n, k2) = a.shape, b.shape
    else:
        (k, m), (k2, n) = a.shape, b.shape
    assert k == k2, (a.shape, b.shape, mode)
    tm, tn, tk = _tile(m, tm), _tile(n, tn), _tile(k, tk)
    nk = k // tk
    dims = {"nn": _NN, "nt": _NT, "tn": _TN}[mode]

    def body(*refs):
        if add is None:
            a_ref, b_ref, o_ref, acc = refs
        else:
            a_ref, b_ref, add_ref, o_ref, acc = refs
        kk = pl.program_id(2)

        @pl.when(kk == 0)
        def _():
            acc[...] = jnp.zeros_like(acc)

        acc[...] += _dg(a_ref[...], b_ref[...], dims, False)

        @pl.when(kk == nk - 1)
        def _():
            o_ref[...] = acc[...] if add is None else acc[...] + add_ref[...]

    a_spec = pl.BlockSpec((tk, tm), lambda i, j, kk: (kk, i)) if mode == "tn" else pl.BlockSpec((tm, tk), lambda i, j, kk: (i, kk))
    b_spec = pl.BlockSpec((tn, tk), lambda i, j, kk: (j, kk)) if mode == "nt" else pl.BlockSpec((tk, tn), lambda i, j, kk: (kk, j))
    o_spec = pl.BlockSpec((tm, tn), lambda i, j, kk: (i, j))
    ins, specs = [a, b], [a_spec, b_spec]
    if add is not None:
        ins.append(add)
        specs.append(o_spec)
    return pl.pallas_call(
        body, name=name, grid=(m // tm, n // tn, nk), in_specs=specs, out_specs=o_spec,
        out_shape=_sds((m, n)), scratch_shapes=[pltpu.VMEM((tm, tn), F32)],
        compiler_params=_params(("parallel", "parallel", "arbitrary")),
    )(*ins)


def _cols(width, start):
    assert start % width == 0
    return pl.BlockSpec((ROW_TILE, width), lambda i: (i, start // width))


def _rows(width):
    return pl.BlockSpec((ROW_TILE, width), lambda i: (i, 0))


def _accumulate(ref, value, step):
    @pl.when(step == 0)
    def _():
        ref[...] = value

    @pl.when(step != 0)
    def _():
        ref[...] += value


def _ln_fn(x, r, g, b):
    u = DN_ALPHA * x + r
    mu = jnp.mean(u, axis=-1, keepdims=True)
    var = jnp.mean(jnp.square(u - mu), axis=-1, keepdims=True)
    return (u - mu) * lax.rsqrt(var + LN_EPS) * g + b


def _ln_fwd(x, r, g, b, name):
    t, d = x.shape

    def body(x_ref, r_ref, g_ref, b_ref, o_ref):
        o_ref[...] = _ln_fn(x_ref[...], r_ref[...], g_ref[...], b_ref[...])

    return pl.pallas_call(
        body, name=name, grid=(t // ROW_TILE,), in_specs=[_rows(d), _rows(d), _full((1, d)), _full((1, d))],
        out_specs=_rows(d), out_shape=_sds((t, d)), compiler_params=_params(("parallel",)),
    )(x, r, g, b)


def _ln_bwd(x, r, g, b, dy, name):
    t, d = x.shape

    def body(x_ref, r_ref, g_ref, b_ref, dy_ref, dx_ref, dr_ref, dg_ref, db_ref):
        _, vjp = jax.vjp(_ln_fn, x_ref[...], r_ref[...], g_ref[...], b_ref[...])
        dx, dr, dg, db = vjp(dy_ref[...])
        dx_ref[...] = dx
        dr_ref[...] = dr
        _accumulate(dg_ref, dg, pl.program_id(0))
        _accumulate(db_ref, db, pl.program_id(0))

    return pl.pallas_call(
        body, name=name, grid=(t // ROW_TILE,),
        in_specs=[_rows(d), _rows(d), _full((1, d)), _full((1, d)), _rows(d)],
        out_specs=[_rows(d), _rows(d), _full((1, d)), _full((1, d))],
        out_shape=[_sds((t, d)), _sds((t, d)), _sds((1, d)), _sds((1, d))],
        compiler_params=_params(("arbitrary",)),
    )(x, r, g, b, dy)


def _merge_fn(ga, gb, ya, yb):
    return jax.nn.sigmoid(ga) * ya + jax.nn.sigmoid(gb) * yb


def _merge_fwd(proj, ya, yb):
    t, d = ya.shape

    def body(ga_ref, gb_ref, ya_ref, yb_ref, o_ref):
        o_ref[...] = _merge_fn(ga_ref[...], gb_ref[...], ya_ref[...], yb_ref[...])

    return pl.pallas_call(
        body, name="merge_fwd", grid=(t // ROW_TILE,), in_specs=[_cols(d, C_GA), _cols(d, C_GB), _rows(d), _rows(d)],
        out_specs=_rows(d), out_shape=_sds((t, d)), compiler_params=_params(("parallel",)),
    )(proj, proj, ya, yb)


def _merge_bwd(proj, ya, yb, dm):
    t, d = ya.shape

    def body(ga_ref, gb_ref, ya_ref, yb_ref, dm_ref, dga_ref, dgb_ref, dya_ref, dyb_ref):
        _, vjp = jax.vjp(_merge_fn, ga_ref[...], gb_ref[...], ya_ref[...], yb_ref[...])
        dga_ref[...], dgb_ref[...], dya_ref[...], dyb_ref[...] = vjp(dm_ref[...])

    return pl.pallas_call(
        body, name="merge_bwd", grid=(t // ROW_TILE,),
        in_specs=[_cols(d, C_GA), _cols(d, C_GB), _rows(d), _rows(d), _rows(d)],
        out_specs=[_rows(d)] * 4, out_shape=[_sds((t, d))] * 4, compiler_params=_params(("parallel",)),
    )(proj, proj, ya, yb, dm)


def _swiglu_fn(gate, up):
    return _silu(gate) * up


def _swiglu_fwd(gu):
    t = gu.shape[0]
    f = FFN_HIDDEN
    rows = 128

    def body(gu_ref, o_ref):
        o_ref[...] = _swiglu_fn(gu_ref[:, :f], gu_ref[:, f:])

    return pl.pallas_call(
        body, name="swiglu_fwd", grid=(t // rows,), in_specs=[pl.BlockSpec((rows, 2 * f), lambda i: (i, 0))],
        out_specs=pl.BlockSpec((rows, f), lambda i: (i, 0)), out_shape=_sds((t, f)), compiler_params=_params(("parallel",)),
    )(gu)


def _swiglu_bwd(gu, dh):
    t = gu.shape[0]
    f = FFN_HIDDEN
    rows = 128

    def body(gu_ref, dh_ref, o_ref):
        _, vjp = jax.vjp(_swiglu_fn, gu_ref[:, :f], gu_ref[:, f:])
        o_ref[:, :f], o_ref[:, f:] = vjp(dh_ref[...])

    return pl.pallas_call(
        body, name="swiglu_bwd", grid=(t // rows,),
        in_specs=[pl.BlockSpec((rows, 2 * f), lambda i: (i, 0)), pl.BlockSpec((rows, f), lambda i: (i, 0))],
        out_specs=pl.BlockSpec((rows, 2 * f), lambda i: (i, 0)), out_shape=_sds((t, 2 * f)),
        compiler_params=_params(("parallel",)),
    )(gu, dh)


def _loss_head(y, target):
    t, d = y.shape

    def body(y_ref, t_ref, s_ref, dy_ref):
        err = y_ref[...] - t_ref[...]
        dy_ref[...] = err / d
        _accumulate(s_ref, jnp.broadcast_to(jnp.sum(jnp.square(err)), (1, LANES)), pl.program_id(0))

    return pl.pallas_call(
        body, name="loss_head", grid=(t // ROW_TILE,), in_specs=[_rows(d), _rows(d)],
        out_specs=[_full((1, LANES)), _rows(d)], out_shape=[_sds((1, LANES)), _sds((t, d))],
        compiler_params=_params(("arbitrary",)),
    )(y, target)


def _shift_rows(x, s):
    if s == 0:
        return x
    t = x.shape[0]
    rolled = pltpu.roll(x, (-s) % t, 0)
    row = lax.broadcasted_iota(jnp.int32, x.shape, 0)
    return jnp.where((row + s >= 0) & (row + s < t), rolled, 0.0)


def _conv(x, w):
    half = DN_CONV // 2
    acc = None
    for k in range(DN_CONV):
        term = _shift_rows(x, k - half) * w[k:k + 1, :]
        acc = term if acc is None else acc + term
    return acc


def _act_norm(c, do_norm, scale):
    a = _silu(c)
    if not do_norm:
        return a
    return a * lax.rsqrt(jnp.sum(a * a, axis=-1, keepdims=True) + RMS_EPS) * scale


PREP_ROWS = 512
HALO = 8
_KINDS = ((True, DN_HEAD_DIM ** -0.5), (True, 1.0), (False, 1.0))


def _halo_rows(read, i, pr, t):
    lo, hi = i * pr - HALO, (i + 1) * pr + HALO
    parts = []
    if lo < 0:
        parts.append(jnp.zeros((HALO, LANES), F32))
    parts.append(read(max(lo, 0), min(hi, t)))
    if hi > t:
        parts.append(jnp.zeros((HALO, LANES), F32))
    return jnp.concatenate(parts, axis=0) if len(parts) > 1 else parts[0]


def _prep_fwd(proj, conv_w, kind):
    t = proj.shape[0]
    pr = min(PREP_ROWS, t)
    do_norm, scale = _KINDS[kind]
    blk = pl.BlockSpec((t, LANES), lambda j: (0, kind * DN_HEADS + j))

    def body(x_ref, w_ref, o_ref):
        w = w_ref[...]
        for i in range(t // pr):
            xx = _halo_rows(lambda lo, hi: x_ref[lo:hi, :], i, pr, t)
            c = _conv(xx, w)[HALO:HALO + pr, :]
            o_ref[i * pr:(i + 1) * pr, :] = _act_norm(c, do_norm, scale)

    return pl.pallas_call(
        body, name=f"prep_fwd_{kind}", grid=(DN_HEADS,),
        in_specs=[blk, pl.BlockSpec((8, LANES), lambda j: (0, kind * DN_HEADS + j))],
        out_specs=pl.BlockSpec((t, LANES), lambda j: (0, j)), out_shape=_sds((t, DN_WIDTH)),
        compiler_params=_params(("parallel",)),
    )(proj, conv_w)


def _prep_bwd(proj, conv_w, d2, kind):
    t = proj.shape[0]
    pr = min(PREP_ROWS, t)
    do_norm, scale = _KINDS[kind]
    half = DN_CONV // 2
    blk = pl.BlockSpec((t, LANES), lambda j: (0, kind * DN_HEADS + j))
    oblk = pl.BlockSpec((t, LANES), lambda j: (0, j))

    def body(x_ref, w_ref, d_ref, dx_ref, dw_ref):
        w = w_ref[...]
        own = slice(HALO, HALO + pr)
        dw = jnp.zeros((8, LANES), F32)
        for i in range(t // pr):
            xx = _halo_rows(lambda lo, hi: x_ref[lo:hi, :], i, pr, t)
            dn = _halo_rows(lambda lo, hi: d_ref[0, lo:hi, :] + d_ref[1, lo:hi, :], i, pr, t)
            _, vjp = jax.vjp(lambda c: _act_norm(c, do_norm, scale), _conv(xx, w))
            (dc,) = vjp(dn)
            dx = None
            rows = []
            for k in range(DN_CONV):
                term = _shift_rows(dc, half - k) * w[k:k + 1, :]
                dx = term if dx is None else dx + term
                rows.append(jnp.sum(dc[own, :] * _shift_rows(xx, k - half)[own, :], axis=0, keepdims=True))
            dx_ref[i * pr:(i + 1) * pr, :] = dx[own, :]
            dw = dw + jnp.concatenate(rows + [jnp.zeros((8 - DN_CONV, LANES), F32)], axis=0)
        dw_ref[...] = dw

    return pl.pallas_call(
        body, name=f"prep_bwd_{kind}", grid=(DN_HEADS,),
        in_specs=[blk, pl.BlockSpec((8, LANES), lambda j: (0, kind * DN_HEADS + j)), pl.BlockSpec((2, t, LANES), lambda j: (0, 0, j))],
        out_specs=[oblk, pl.BlockSpec((8, LANES), lambda j: (0, j))], out_shape=[_sds((t, DN_WIDTH)), _sds((8, DN_WIDTH))],
        compiler_params=_params(("parallel",)),
    )(proj, conv_w, d2)


def _gb_fn(ba, alog_row, dtb_row):
    c = DN_CHUNK
    lane = lax.broadcasted_iota(jnp.int32, (c, LANES), 1)
    ii = lax.broadcasted_iota(jnp.int32, (c, c), 0)
    jj = lax.broadcasted_iota(jnp.int32, (c, c), 1)
    beta = jax.nn.sigmoid(ba)
    g = -jnp.exp(alog_row) * _softplus(ba + dtb_row)
    g = jnp.where((lane >= N_HD) & (lane < 2 * N_HD), g, 0.0)
    gc_fwd = _hnn((ii >= jj).astype(F32), g)
    gc_rev = _hnn((ii <= jj).astype(F32), g)
    gc = jnp.where(lane < N_HD + DN_HEADS, gc_fwd, gc_rev)
    return jnp.where(lane < N_HD, beta, jnp.where(lane < 2 * N_HD, gc, 0.0))


def _gb_fwd(ba, alog_row, dtb_row):
    t = ba.shape[0]
    n = ROW_TILE // DN_CHUNK

    def body(ba_ref, a_ref, d_ref, o_ref):
        for c in range(n):
            rows = slice(c * DN_CHUNK, (c + 1) * DN_CHUNK)
            o_ref[rows, :] = _gb_fn(ba_ref[rows, :], a_ref[...], d_ref[...])

    return pl.pallas_call(
        body, name="gates_fwd", grid=(t // ROW_TILE,), in_specs=[_rows(LANES), _full((1, LANES)), _full((1, LANES))],
        out_specs=_rows(LANES), out_shape=_sds((t, LANES)), compiler_params=_params(("parallel",)),
    )(ba, alog_row, dtb_row)


def _gb_bwd(ba, alog_row, dtb_row, d_a, d_b):
    t = ba.shape[0]
    n = ROW_TILE // DN_CHUNK

    def body(ba_ref, a_ref, d_ref, da_ref, db_ref, dba_ref, dal_ref, ddt_ref):
        dal = jnp.zeros((1, LANES), F32)
        ddt = jnp.zeros((1, LANES), F32)
        for c in range(n):
            rows = slice(c * DN_CHUNK, (c + 1) * DN_CHUNK)
            _, vjp = jax.vjp(_gb_fn, ba_ref[rows, :], a_ref[...], d_ref[...])
            dba, da, dd = vjp(da_ref[rows, :] + db_ref[rows, :])
            dba_ref[rows, :] = dba
            dal = dal + da
            ddt = ddt + dd
        _accumulate(dal_ref, dal, pl.program_id(0))
        _accumulate(ddt_ref, ddt, pl.program_id(0))

    return pl.pallas_call(
        body, name="gates_bwd", grid=(t // ROW_TILE,),
        in_specs=[_rows(LANES), _full((1, LANES)), _full((1, LANES)), _rows(LANES), _rows(LANES)],
        out_specs=[_rows(LANES), _full((1, LANES)), _full((1, LANES))],
        out_shape=[_sds((t, LANES)), _sds((1, LANES)), _sds((1, LANES))], compiler_params=_params(("arbitrary",)),
    )(ba, alog_row, dtb_row, d_a, d_b)


def _dn_local(q, k, v, beta, gcc, gcr, sgn):
    c = DN_CHUNK
    ii = lax.broadcasted_iota(jnp.int32, (c, c), 0)
    jj = lax.broadcasted_iota(jnp.int32, (c, c), 1)
    d = (ii - jj) * sgn
    lower = d >= 0
    strict = d > 0
    decay = jnp.where(lower, jnp.exp(jnp.where(lower, gcc - gcr, 0.0)), 0.0)
    a = jnp.where(strict, beta * _bnt(k, k) * decay, 0.0)
    t_inv = _inv_unit(a)
    eg = jnp.exp(gcc)
    u = _bnn(t_inv, v * beta)
    w = _bnn(t_inv, k * (beta * eg))
    qk = _bnt(q, k) * decay
    qd = q * eg
    last = jnp.where(sgn > 0, c - 1, 0)
    onehot = (lax.broadcasted_iota(jnp.int32, (c, 1), 0) == last).astype(F32)
    gl = jnp.sum(gcc * onehot, axis=0, keepdims=True)
    kd = k * jnp.exp(gl - gcc)
    egl = jnp.broadcast_to(jnp.exp(gl), (1, LANES))
    return u, w, qk, qd, kd, egl


def _hd_sign(hd):
    return jnp.where(hd < DN_HEADS, 1, -1).astype(jnp.int32)


def _head_of(hd):
    return jnp.where(hd < DN_HEADS, hd, hd - DN_HEADS)


def _dir_of(hd):
    return jnp.where(hd < DN_HEADS, 0, 1)


def _dn_specs(t):
    nl = LOCAL_ROWS // DN_CHUNK
    wide = pl.BlockSpec((1, LOCAL_ROWS, LANES), lambda hd, i: (hd, i, 0))
    half = pl.BlockSpec((1, LOCAL_ROWS, DN_CHUNK), lambda hd, i: (hd, i, 0))
    col = pl.BlockSpec((1, LOCAL_ROWS, 1), lambda hd, i: (hd, i, 0))
    row = pl.BlockSpec((1, nl, 1, DN_CHUNK), lambda hd, i: (hd, i, 0, 0))
    egl = pl.BlockSpec((1, nl, 1, LANES), lambda hd, i: (hd, i, 0, 0))
    return wide, half, col, row, egl


def _qkv_specs():
    return [pl.BlockSpec((LOCAL_ROWS, LANES), lambda hd, i: (i, _head_of(hd)))] * 3


def _dn_local_fwd(q, k, v, beta_c, gc_c, gc_r):
    t = q.shape[0]
    nc = t // DN_CHUNK
    nl = LOCAL_ROWS // DN_CHUNK
    wide, half, col, row, egl = _dn_specs(t)

    def body(q_ref, k_ref, v_ref, b_ref, gc_ref, gr_ref, u_ref, w_ref, qk_ref, qd_ref, kd_ref, egl_ref):
        sgn = _hd_sign(pl.program_id(0))
        for c in range(nl):
            rows = slice(c * DN_CHUNK, (c + 1) * DN_CHUNK)
            outs = _dn_local(q_ref[rows, :], k_ref[rows, :], v_ref[rows, :], b_ref[0, rows, :], gc_ref[0, rows, :],
                             gr_ref[0, c], sgn)
            u_ref[0, rows, :], w_ref[0, rows, :], qk_ref[0, rows, :], qd_ref[0, rows, :], kd_ref[0, rows, :], egl_ref[0, c] = outs

    big = _sds((N_HD, t, LANES))
    return pl.pallas_call(
        body, name="dn_local_fwd", grid=(N_HD, t // LOCAL_ROWS), in_specs=_qkv_specs() + [col, col, row],
        out_specs=[wide, wide, half, wide, wide, egl],
        out_shape=[big, big, _sds((N_HD, t, DN_CHUNK)), big, big, _sds((N_HD, nc, 1, LANES))],
        compiler_params=_params(("parallel", "parallel")),
    )(q, k, v, beta_c, gc_c, gc_r)


def _dn_local_bwd(q, k, v, beta_c, gc_c, gc_r, du, dw, dqk, dqd, dkd, degl):
    t = q.shape[0]
    nc = t // DN_CHUNK
    nl = LOCAL_ROWS // DN_CHUNK
    wide, half, col, row, egl = _dn_specs(t)
    dspec = pl.BlockSpec((1, LOCAL_ROWS, LANES), lambda hd, i: (_dir_of(hd), i, _head_of(hd)))

    def body(q_ref, k_ref, v_ref, b_ref, gc_ref, gr_ref, du_ref, dw_ref, dqk_ref, dqd_ref, dkd_ref, degl_ref,
             dq_ref, dk_ref, dv_ref, db_ref, dgc_ref, dgr_ref):
        sgn = _hd_sign(pl.program_id(0))
        for c in range(nl):
            rows = slice(c * DN_CHUNK, (c + 1) * DN_CHUNK)
            _, vjp = jax.vjp(
                functools.partial(_dn_local, sgn=sgn),
                q_ref[rows, :], k_ref[rows, :], v_ref[rows, :], b_ref[0, rows, :], gc_ref[0, rows, :], gr_ref[0, c])
            grads = vjp((du_ref[0, rows, :], dw_ref[0, rows, :], dqk_ref[0, rows, :], dqd_ref[0, rows, :],
                         dkd_ref[0, rows, :], degl_ref[0, c]))
            dq_ref[0, rows, :], dk_ref[0, rows, :], dv_ref[0, rows, :], db_ref[0, rows, :], dgc_ref[0, rows, :], dgr_ref[0, c] = grads

    per_dir = _sds((2, t, DN_WIDTH))
    return pl.pallas_call(
        body, name="dn_local_bwd", grid=(N_HD, t // LOCAL_ROWS),
        in_specs=_qkv_specs() + [col, col, row, wide, wide, half, wide, wide, egl],
        out_specs=[dspec, dspec, dspec, col, col, row],
        out_shape=[per_dir, per_dir, per_dir, _sds((N_HD, t, 1)), _sds((N_HD, t, 1)), _sds((N_HD, nc, 1, DN_CHUNK))],
        compiler_params=_params(("parallel", "parallel")),
    )(q, k, v, beta_c, gc_c, gc_r, du, dw, dqk, dqd, dkd, degl)


def _rec_specs(time_block):
    nr = REC_ROWS // DN_CHUNK
    wide = pl.BlockSpec((1, REC_ROWS, LANES), lambda hd, b: (hd, time_block(hd, b), 0))
    half = pl.BlockSpec((1, REC_ROWS, DN_CHUNK), lambda hd, b: (hd, time_block(hd, b), 0))
    egl = pl.BlockSpec((1, nr, 1, LANES), lambda hd, b: (hd, time_block(hd, b), 0, 0))
    state = pl.BlockSpec((1, nr, DN_HEAD_DIM, DN_HEAD_DIM), lambda hd, b: (hd, time_block(hd, b), 0, 0))
    return wide, half, egl, state


def _dn_rec_fwd(u, w, qk, qd, kd, egl):
    t = u.shape[1]
    nb = t // REC_ROWS
    nr = REC_ROWS // DN_CHUNK
    nc = t // DN_CHUNK

    def time_block(hd, b):
        return jnp.where(hd < DN_HEADS, b, nb - 1 - b)

    wide, half, egl_spec, state = _rec_specs(time_block)
    o_spec = pl.BlockSpec((1, REC_ROWS, LANES), lambda hd, b: (_dir_of(hd), time_block(hd, b), _head_of(hd)))

    def body(u_ref, w_ref, qk_ref, qd_ref, kd_ref, egl_ref, o_ref, vn_ref, s_ref, s_scr):
        fwd = pl.program_id(0) < DN_HEADS

        @pl.when(pl.program_id(1) == 0)
        def _():
            s_scr[...] = jnp.zeros_like(s_scr)

        s = s_scr[...]
        for c in range(nr):
            ce = jnp.where(fwd, c, nr - 1 - c)
            rows = pl.ds(pl.multiple_of(ce * DN_CHUNK, DN_CHUNK), DN_CHUNK)
            s_ref[0, ce] = s
            vn = u_ref[0, rows, :] - _bnn(w_ref[0, rows, :], s)
            o_ref[0, rows, :] = _bnn(qd_ref[0, rows, :], s) + _bnn(qk_ref[0, rows, :], vn)
            vn_ref[0, rows, :] = vn
            s = s * egl_ref[0, ce] + _btn(kd_ref[0, rows, :], vn)
        s_scr[...] = s

    return pl.pallas_call(
        body, name="dn_rec_fwd", grid=(N_HD, nb), in_specs=[wide, wide, half, wide, wide, egl_spec],
        out_specs=[o_spec, wide, state],
        out_shape=[_sds((2, t, DN_WIDTH)), _sds((N_HD, t, LANES)), _sds((N_HD, nc, DN_HEAD_DIM, DN_HEAD_DIM))],
        scratch_shapes=[pltpu.VMEM((DN_HEAD_DIM, DN_HEAD_DIM), F32)],
        compiler_params=_params(("parallel", "arbitrary")),
    )(u, w, qk, qd, kd, egl)


def _dn_rec_bwd(w, qk, qd, kd, egl, vn, states, do):
    t = w.shape[1]
    nb = t // REC_ROWS
    nr = REC_ROWS // DN_CHUNK
    nc = t // DN_CHUNK

    def time_block(hd, b):
        return jnp.where(hd < DN_HEADS, nb - 1 - b, b)

    wide, half, egl_spec, state = _rec_specs(time_block)
    do_spec = pl.BlockSpec((REC_ROWS, LANES), lambda hd, b: (time_block(hd, b), _head_of(hd)))

    def body(w_ref, qk_ref, qd_ref, kd_ref, egl_ref, vn_ref, s_ref, do_ref,
             du_ref, dw_ref, dqk_ref, dqd_ref, dkd_ref, degl_ref, ds_scr):
        fwd = pl.program_id(0) < DN_HEADS

        @pl.when(pl.program_id(1) == 0)
        def _():
            ds_scr[...] = jnp.zeros_like(ds_scr)

        ds = ds_scr[...]
        for c in range(nr):
            ce = jnp.where(fwd, nr - 1 - c, c)
            rows = pl.ds(pl.multiple_of(ce * DN_CHUNK, DN_CHUNK), DN_CHUNK)
            s = s_ref[0, ce]
            do_c = do_ref[rows, :]
            vn_c = vn_ref[0, rows, :]
            qk_c = qk_ref[0, rows, :]
            kd_c = kd_ref[0, rows, :]
            dvn = _btn(qk_c, do_c) + _bnn(kd_c, ds)
            du_ref[0, rows, :] = dvn
            dw_ref[0, rows, :] = -_bnt(dvn, s)
            dqk_ref[0, rows, :] = _bnt(do_c, vn_c)
            dqd_ref[0, rows, :] = _bnt(do_c, s)
            dkd_ref[0, rows, :] = _bnt(vn_c, ds)
            degl_ref[0, ce] = jnp.sum(s * ds, axis=0, keepdims=True)
            ds = ds * egl_ref[0, ce] + _btn(qd_ref[0, rows, :], do_c) - _btn(w_ref[0, rows, :], dvn)
        ds_scr[...] = ds

    big = _sds((N_HD, t, LANES))
    return pl.pallas_call(
        body, name="dn_rec_bwd", grid=(N_HD, nb), in_specs=[wide, half, wide, wide, egl_spec, wide, state, do_spec],
        out_specs=[wide, wide, half, wide, wide, egl_spec],
        out_shape=[big, big, _sds((N_HD, t, DN_CHUNK)), big, big, _sds((N_HD, nc, 1, LANES))],
        scratch_shapes=[pltpu.VMEM((DN_HEAD_DIM, DN_HEAD_DIM), F32)],
        compiler_params=_params(("parallel", "arbitrary")),
    )(w, qk, qd, kd, egl, vn, states, do)


def _post_fn(of, ob, z, gain):
    o = of + ob
    return o * lax.rsqrt(jnp.mean(o * o, axis=-1, keepdims=True) + RMS_EPS) * gain * _silu(z)


def _post_specs():
    o_spec = [pl.BlockSpec((1, ROW_TILE, LANES), functools.partial(lambda i, h, d: (d, i, h), d=d)) for d in (0, 1)]
    z_spec = pl.BlockSpec((ROW_TILE, LANES), lambda i, h: (i, C_Z // LANES + h))
    head = pl.BlockSpec((ROW_TILE, LANES), lambda i, h: (i, h))
    gain = pl.BlockSpec((1, LANES), lambda i, h: (0, 0))
    return o_spec, z_spec, head, gain


def _post_fwd(o2, proj, gain):
    t = proj.shape[0]
    o_spec, z_spec, head, gain_spec = _post_specs()

    def body(of_ref, ob_ref, z_ref, g_ref, out_ref):
        out_ref[...] = _post_fn(of_ref[0], ob_ref[0], z_ref[...], g_ref[...])

    return pl.pallas_call(
        body, name="post_fwd", grid=(t // ROW_TILE, DN_HEADS), in_specs=o_spec + [z_spec, gain_spec], out_specs=head,
        out_shape=_sds((t, DN_WIDTH)), compiler_params=_params(("parallel", "parallel")),
    )(o2, o2, proj, gain)


def _post_bwd(o2, proj, gain, dout):
    t = proj.shape[0]
    o_spec, z_spec, head, gain_spec = _post_specs()

    def body(of_ref, ob_ref, z_ref, g_ref, d_ref, do_ref, dz_ref, dg_ref):
        _, vjp = jax.vjp(_post_fn, of_ref[0], ob_ref[0], z_ref[...], g_ref[...])
        do, _, dz, dg = vjp(d_ref[...])
        do_ref[...] = do
        dz_ref[...] = dz
        _accumulate(dg_ref, dg, pl.program_id(0) * DN_HEADS + pl.program_id(1))

    return pl.pallas_call(
        body, name="post_bwd", grid=(t // ROW_TILE, DN_HEADS), in_specs=o_spec + [z_spec, gain_spec, head],
        out_specs=[head, head, gain_spec], out_shape=[_sds((t, DN_WIDTH)), _sds((t, DN_WIDTH)), _sds((1, LANES))],
        compiler_params=_params(("arbitrary", "arbitrary")),
    )(o2, o2, proj, gain, dout)


def _rope(x, cos, sin):
    lane = lax.broadcasted_iota(jnp.int32, x.shape, 1)
    first = (lane & (SW_HEAD_DIM - 1)) < SW_HEAD_DIM // 2
    rot = jnp.where(first, -pltpu.roll(x, LANES - SW_HEAD_DIM // 2, 1), pltpu.roll(x, SW_HEAD_DIM // 2, 1))
    return x * cos + rot * sin


def _rope_apply(q, k, q_cols, k_cols, cos, sin, name):
    t = cos.shape[0]
    qw, kw = SW_HEADS * SW_HEAD_DIM, SW_KV_WIDTH

    def body(q_ref, k_ref, c_ref, s_ref, qo_ref, ko_ref):
        c, s = c_ref[...], s_ref[...]
        for j in range(qw // LANES):
            cols = slice(j * LANES, (j + 1) * LANES)
            qo_ref[:, cols] = _rope(q_ref[:, cols], c, s)
        for j in range(kw // LANES):
            cols = slice(j * LANES, (j + 1) * LANES)
            ko_ref[:, cols] = _rope(k_ref[:, cols], c, s)

    return pl.pallas_call(
        body, name=name, grid=(t // ROW_TILE,), in_specs=[_cols(qw, q_cols), _cols(kw, k_cols), _rows(LANES), _rows(LANES)],
        out_specs=[_rows(qw), _rows(kw)], out_shape=[_sds((t, qw)), _sds((t, kw))], compiler_params=_params(("parallel",)),
    )(q, k, cos, sin)


def _attn_core(qs, kb, vb, sink, mask):
    s = _bnt(qs, kb) * (SW_HEAD_DIM ** -0.5)
    s = jnp.where(mask, s, -1e30)
    m = lax.stop_gradient(jnp.maximum(jnp.max(s, axis=1, keepdims=True), sink))
    e = jnp.exp(s - m)
    den = jnp.sum(e, axis=1, keepdims=True) + jnp.exp(sink - m)
    return _bnn(e / den, vb)


def _band_mask(n, nb):
    rows = SW_GROUP * SW_BLOCK
    i = lax.broadcasted_iota(jnp.int32, (rows, 3 * SW_BLOCK), 0) & (SW_BLOCK - 1)
    j = lax.broadcasted_iota(jnp.int32, (rows, 3 * SW_BLOCK), 1)
    near = (j - i >= 0) & (j - i <= 2 * SW_BLOCK)
    lo = jnp.where(n == 0, SW_BLOCK, 0)
    hi = jnp.where(n == nb - 1, 2 * SW_BLOCK, 3 * SW_BLOCK)
    return near & (j >= lo) & (j < hi)


def _band_specs(nb, v_cols):
    def spec(width, base, shift):
        return pl.BlockSpec((SW_BLOCK, width), lambda n: (jnp.clip(n + shift, 0, nb - 1), base // width))
    k_specs = [spec(SW_KV_WIDTH, 0, s) for s in (-1, 0, 1)]
    v_specs = [spec(SW_KV_WIDTH, v_cols, s) for s in (-1, 0, 1)]
    return k_specs, v_specs


def _head_cols(kv, g):
    h = kv * SW_GROUP + g
    return slice(h * SW_HEAD_DIM, (h + 1) * SW_HEAD_DIM)


def _attn_fwd(qr, kr, proj, sinks):
    t = qr.shape[0]
    nb = t // SW_BLOCK
    qw = SW_HEADS * SW_HEAD_DIM
    k_specs, v_specs = _band_specs(nb, C_VSW)
    q_spec = pl.BlockSpec((SW_BLOCK, qw), lambda n: (n, 0))

    def body(q_ref, k0, k1, k2, v0, v1, v2, s_ref, o_ref):
        mask = _band_mask(pl.program_id(0), nb)
        kb = jnp.concatenate([k0[...], k1[...], k2[...]], axis=0)
        vb = jnp.concatenate([v0[...], v1[...], v2[...]], axis=0)
        for kv in range(SW_KV_HEADS):
            kvc = slice(kv * SW_HEAD_DIM, (kv + 1) * SW_HEAD_DIM)
            qs = jnp.concatenate([q_ref[:, _head_cols(kv, g)] for g in range(SW_GROUP)], axis=0)
            sink = jnp.concatenate([jnp.broadcast_to(s_ref[kv * SW_GROUP + g], (SW_BLOCK, 1)) for g in range(SW_GROUP)], axis=0)
            o = _attn_core(qs, kb[:, kvc], vb[:, kvc], sink, mask)
            for g in range(SW_GROUP):
                o_ref[:, _head_cols(kv, g)] = o[g * SW_BLOCK:(g + 1) * SW_BLOCK, :]

    return pl.pallas_call(
        body, name="attn_fwd", grid=(nb,), in_specs=[q_spec] + k_specs + v_specs + [_full((SW_HEADS, 1, 1))],
        out_specs=q_spec, out_shape=_sds((t, qw)), compiler_params=_params(("parallel",)),
    )(qr, kr, kr, kr, proj, proj, proj, sinks)


def _attn_bwd(qr, kr, proj, sinks, do):
    t = qr.shape[0]
    nb = t // SW_BLOCK
    qw = SW_HEADS * SW_HEAD_DIM
    k_specs, v_specs = _band_specs(nb, C_VSW)
    q_spec = pl.BlockSpec((SW_BLOCK, qw), lambda n: (n, 0))
    part = pl.BlockSpec((1, 3 * SW_BLOCK, SW_KV_WIDTH), lambda n: (n, 0, 0))

    def body(q_ref, k0, k1, k2, v0, v1, v2, s_ref, do_ref, dq_ref, dk_ref, dv_ref, ds_ref):
        mask = _band_mask(pl.program_id(0), nb)
        kb = jnp.concatenate([k0[...], k1[...], k2[...]], axis=0)
        vb = jnp.concatenate([v0[...], v1[...], v2[...]], axis=0)

        @pl.when(pl.program_id(0) == 0)
        def _():
            ds_ref[...] = jnp.zeros_like(ds_ref)

        for kv in range(SW_KV_HEADS):
            kvc = slice(kv * SW_HEAD_DIM, (kv + 1) * SW_HEAD_DIM)
            qs = jnp.concatenate([q_ref[:, _head_cols(kv, g)] for g in range(SW_GROUP)], axis=0)
            dos = jnp.concatenate([do_ref[:, _head_cols(kv, g)] for g in range(SW_GROUP)], axis=0)
            sink = jnp.concatenate([jnp.broadcast_to(s_ref[kv * SW_GROUP + g], (SW_BLOCK, 1)) for g in range(SW_GROUP)], axis=0)
            _, vjp = jax.vjp(functools.partial(_attn_core, mask=mask), qs, kb[:, kvc], vb[:, kvc], sink)
            dqs, dkb, dvb, dsink = vjp(dos)
            dk_ref[0, :, kvc] = dkb
            dv_ref[0, :, kvc] = dvb
            for g in range(SW_GROUP):
                rows = slice(g * SW_BLOCK, (g + 1) * SW_BLOCK)
                dq_ref[:, _head_cols(kv, g)] = dqs[rows, :]
                ds_ref[kv * SW_GROUP + g] += jnp.sum(dsink[rows, :], axis=0, keepdims=True)

    parts = _sds((nb, 3 * SW_BLOCK, SW_KV_WIDTH))
    return pl.pallas_call(
        body, name="attn_bwd", grid=(nb,), in_specs=[q_spec] + k_specs + v_specs + [_full((SW_HEADS, 1, 1)), q_spec],
        out_specs=[q_spec, part, part, _full((SW_HEADS, 1, 1))], out_shape=[_sds((t, qw)), parts, parts, _sds((SW_HEADS, 1, 1))],
        compiler_params=_params(("arbitrary",)),
    )(qr, kr, kr, kr, proj, proj, proj, sinks, do)


def _band_sum(parts, name):
    nb = parts.shape[0]
    w = parts.shape[2]

    def spec(shift, slot):
        return pl.BlockSpec((1, SW_BLOCK, w), lambda m: (jnp.clip(m + shift, 0, nb - 1), slot, 0))

    def body(prev_ref, own_ref, next_ref, o_ref):
        m = pl.program_id(0)
        o_ref[...] = (own_ref[0] + jnp.where(m > 0, prev_ref[0], 0.0) + jnp.where(m < nb - 1, next_ref[0], 0.0))

    return pl.pallas_call(
        body, name=name, grid=(nb,), in_specs=[spec(-1, 2), spec(0, 1), spec(1, 0)],
        out_specs=pl.BlockSpec((SW_BLOCK, w), lambda m: (m, 0)), out_shape=_sds((nb * SW_BLOCK, w)),
        compiler_params=_params(("parallel",)),
    )(parts, parts, parts)


def _gate_layouts(gbo):
    t = gbo.shape[0]
    beta_c = gbo[:, :N_HD].T.reshape(N_HD, t, 1)
    gc = gbo[:, N_HD:2 * N_HD].T
    return beta_c, gc.reshape(N_HD, t, 1), gc.reshape(N_HD, t // DN_CHUNK, 1, DN_CHUNK)


def _gate_layouts_t(dbeta_c, dgc_c, dgc_r):
    t = dbeta_c.shape[1]
    pad = jnp.zeros((t, LANES - 2 * N_HD), F32)
    none = jnp.zeros((t, N_HD), F32)
    d_a = jnp.concatenate([dbeta_c.reshape(N_HD, t).T, dgc_c.reshape(N_HD, t).T, pad], axis=1)
    d_b = jnp.concatenate([none, dgc_r.reshape(N_HD, t).T, pad], axis=1)
    return d_a, d_b


def _layer_fwd(x, w, cos, sin):
    proj = _mm(x, w["wm"], "nn", name="proj")
    ba = _mm(x, w["wba"], "nn", name="proj_gates")
    qn, kn, vv = [_prep_fwd(proj, w["conv"], kind) for kind in range(3)]
    gbo = _gb_fwd(ba, w["alog"], w["dtb"])
    beta_c, gc_c, gc_r = _gate_layouts(gbo)
    u, wk, qk, qd, kd, egl = _dn_local_fwd(qn, kn, vv, beta_c, gc_c, gc_r)
    o2, vn, states = _dn_rec_fwd(u, wk, qk, qd, kd, egl)
    o_dn = _post_fwd(o2, proj, w["dnw"])
    qr, kr = _rope_apply(proj, proj, C_QSW, C_KSW, cos, sin, "rope_fwd")
    o_sw = _attn_fwd(qr, kr, proj, w["sinks"])
    ya = _mm(o_dn, w["wa"], "nn", name="branch_a")
    yb = _mm(o_sw, w["wb"], "nn", name="branch_b")
    merged = _merge_fwd(proj, ya, yb)
    mix = _mm(merged, w["wo"], "nn", name="mix_out")
    x1 = _ln_fwd(x, mix, w["ln1g"], w["ln1b"], "ln1_fwd")
    gu = _mm(x1, w["wgu"], "nn", name="ffn_up")
    h = _swiglu_fwd(gu)
    f = _mm(h, w["wd"], "nn", name="ffn_down")
    x2 = _ln_fwd(x1, f, w["ln2g"], w["ln2b"], "ln2_fwd")
    res = dict(x=x, proj=proj, ba=ba, qn=qn, kn=kn, vv=vv, gbo=gbo, wk=wk, qk=qk, qd=qd, kd=kd, egl=egl, vn=vn, states=states, o2=o2,
               o_dn=o_dn, qr=qr, kr=kr, o_sw=o_sw, ya=ya, yb=yb, merged=merged, mix=mix, x1=x1, gu=gu, h=h, f=f)
    return x2, res


def _layer_bwd(dx2, w, r, cos, sin):
    dx1, df, dln2g, dln2b = _ln_bwd(r["x1"], r["f"], w["ln2g"], w["ln2b"], dx2, "ln2_bwd")
    dh = _mm(df, w["wd"], "nt", name="d_ffn_hidden")
    dwd = _mm(r["h"], df, "tn", name="dw_ffn_down")
    dgu = _swiglu_bwd(r["gu"], dh)
    dwgu = _mm(r["x1"], dgu, "tn", name="dw_ffn_up")
    dx1 = _mm(dgu, w["wgu"], "nt", name="dx_ffn", add=dx1)
    dx, dmix, dln1g, dln1b = _ln_bwd(r["x"], r["mix"], w["ln1g"], w["ln1b"], dx1, "ln1_bwd")
    dmerged = _mm(dmix, w["wo"], "nt", name="d_merged")
    dwo = _mm(r["merged"], dmix, "tn", name="dw_mix_out")
    dga, dgb, dya, dyb = _merge_bwd(r["proj"], r["ya"], r["yb"], dmerged)
    dwa = _mm(r["o_dn"], dya, "tn", name="dw_branch_a")
    do_dn = _mm(dya, w["wa"], "nt", name="d_branch_a")
    dwb = _mm(r["o_sw"], dyb, "tn", name="dw_branch_b")
    do_sw = _mm(dyb, w["wb"], "nt", name="d_branch_b")
    do, dz, ddnw = _post_bwd(r["o2"], r["proj"], w["dnw"], do_dn)
    du, dwk, dqk, dqd, dkd, degl = _dn_rec_bwd(r["wk"], r["qk"], r["qd"], r["kd"], r["egl"], r["vn"], r["states"], do)
    beta_c, gc_c, gc_r = _gate_layouts(r["gbo"])
    dq3, dk3, dv3, dbeta_c, dgc_c, dgc_r = _dn_local_bwd(r["qn"], r["kn"], r["vv"], beta_c, gc_c, gc_r, du, dwk, dqk, dqd, dkd, degl)
    dqkv, dconv = zip(*[_prep_bwd(r["proj"], w["conv"], d2, kind) for kind, d2 in enumerate((dq3, dk3, dv3))])
    dconv = jnp.concatenate(dconv, axis=1)
    d_a, d_b = _gate_layouts_t(dbeta_c, dgc_c, dgc_r)
    dba, dalog, ddtb = _gb_bwd(r["ba"], w["alog"], w["dtb"], d_a, d_b)
    dqr, dkparts, dvparts, dsinks = _attn_bwd(r["qr"], r["kr"], r["proj"], w["sinks"], do_sw)
    dkr = _band_sum(dkparts, "attn_dk_sum")
    dv = _band_sum(dvparts, "attn_dv_sum")
    dq_sw, dk_sw = _rope_apply(dqr, dkr, 0, 0, cos, -sin, "rope_bwd")
    dproj = jnp.concatenate([*dqkv, dz, dq_sw, dga, dgb, dk_sw, dv], axis=1)
    dwm = _mm(r["x"], dproj, "tn", name="dw_proj")
    dwba = _mm(r["x"], dba, "tn", name="dw_proj_gates")
    dx = _mm(dproj, w["wm"], "nt", name="dx_proj", add=dx)
    dx = _mm(dba, w["wba"], "nt", name="dx_proj_gates", add=dx)
    grads = dict(wm=dwm, wba=dwba, conv=dconv, alog=dalog, dtb=ddtb, dnw=ddnw, sinks=dsinks, wa=dwa, wb=dwb, wo=dwo,
                 ln1g=dln1g, ln1b=dln1b, wgu=dwgu, wd=dwd, ln2g=dln2g, ln2b=dln2b)
    return dx, grads


def _rope_tables(t):
    half = SW_HEAD_DIM // 2
    inv_freq = ROPE_THETA ** (-jnp.arange(half, dtype=F32) / half)
    ang = jnp.arange(t, dtype=F32)[:, None] * inv_freq[None, :]
    return jnp.tile(jnp.cos(ang), (1, LANES // half)), jnp.tile(jnp.sin(ang), (1, LANES // half))


def _trunk(x, target, stacked):
    cos, sin = _rope_tables(x.shape[0])

    def fwd_step(xc, w):
        return _layer_fwd(xc, w, cos, sin)

    y, res = lax.scan(fwd_step, x, stacked)
    sq, dy = _loss_head(y, target)

    def bwd_step(dxc, wr):
        w, r = wr
        return _layer_bwd(dxc, w, r, cos, sin)

    dx, grads = lax.scan(bwd_step, dy, (stacked, res), reverse=True)
    return sq, dx, grads


def _peer(r):
    x, y, c = lax.axis_index("x"), lax.axis_index("y"), lax.axis_index("c")
    px = 1 - x if r & 4 else x
    py = 1 - y if r & 2 else y
    pc = 1 - c if r & 1 else c
    return (px, py, pc), 4 * px + 2 * py + pc


def _all_gather(block, name):
    def body(x_ref, o_ref, send_sems, recv_sems, local_sem):
        _, me = _peer(0)
        mine = pltpu.make_async_copy(x_ref, o_ref.at[me], local_sem)
        mine.start()
        copies = []
        for r in range(1, N_DEV):
            dev, _ = _peer(r)
            cp = pltpu.make_async_remote_copy(
                src_ref=x_ref, dst_ref=o_ref.at[me], send_sem=send_sems.at[r - 1], recv_sem=recv_sems.at[r - 1],
                device_id=dev, device_id_type=pl.DeviceIdType.MESH)
            cp.start()
            copies.append(cp)
        for r in range(1, N_DEV):
            dev, idx = _peer(r)
            pltpu.make_async_remote_copy(
                src_ref=x_ref, dst_ref=o_ref.at[idx], send_sem=send_sems.at[r - 1], recv_sem=recv_sems.at[r - 1],
                device_id=dev, device_id_type=pl.DeviceIdType.MESH).wait_recv()
        for cp in copies:
            cp.wait_send()
        mine.wait()

    return pl.pallas_call(
        body, name=name, in_specs=[pl.BlockSpec(memory_space=pl.ANY)], out_specs=pl.BlockSpec(memory_space=pl.ANY),
        out_shape=_sds((N_DEV,) + block.shape, block.dtype),
        scratch_shapes=[pltpu.SemaphoreType.DMA((N_DEV - 1,)), pltpu.SemaphoreType.DMA((N_DEV - 1,)), pltpu.SemaphoreType.DMA],
        compiler_params=pltpu.CompilerParams(has_side_effects=True),
    )(block)


def _all_to_all(parts, name):
    def body(x_ref, o_ref, send_sems, recv_sems, local_sem):
        _, me = _peer(0)
        mine = pltpu.make_async_copy(x_ref.at[me], o_ref.at[me], local_sem)
        mine.start()
        copies = []
        for r in range(1, N_DEV):
            dev, idx = _peer(r)
            cp = pltpu.make_async_remote_copy(
                src_ref=x_ref.at[idx], dst_ref=o_ref.at[me], send_sem=send_sems.at[r - 1], recv_sem=recv_sems.at[r - 1],
                device_id=dev, device_id_type=pl.DeviceIdType.MESH)
            cp.start()
            copies.append(cp)
        for r in range(1, N_DEV):
            dev, idx = _peer(r)
            pltpu.make_async_remote_copy(
                src_ref=x_ref.at[me], dst_ref=o_ref.at[idx], send_sem=send_sems.at[r - 1], recv_sem=recv_sems.at[r - 1],
                device_id=dev, device_id_type=pl.DeviceIdType.MESH).wait_recv()
        for cp in copies:
            cp.wait_send()
        mine.wait()

    return pl.pallas_call(
        body, name=name, in_specs=[pl.BlockSpec(memory_space=pl.ANY)], out_specs=pl.BlockSpec(memory_space=pl.ANY),
        out_shape=_sds(parts.shape, parts.dtype),
        scratch_shapes=[pltpu.SemaphoreType.DMA((N_DEV - 1,)), pltpu.SemaphoreType.DMA((N_DEV - 1,)), pltpu.SemaphoreType.DMA],
        compiler_params=pltpu.CompilerParams(has_side_effects=True),
    )(parts)


def _sum_adamw(parts, w, m, v, name):
    rows, cols = w.shape
    tr = rows if rows <= 512 else _row_tile(rows)
    blk = pl.BlockSpec((tr, cols), lambda i: (i, 0))

    def body(p_ref, w_ref, m_ref, v_ref, g_ref, d_ref, nm_ref, nv_ref):
        g = p_ref[0]
        for i in range(1, N_DEV):
            g = g + p_ref[i]
        nm = ADAM_B1 * m_ref[...] + (1.0 - ADAM_B1) * g
        nv = ADAM_B2 * v_ref[...] + (1.0 - ADAM_B2) * jnp.square(g)
        m_hat = nm / (1.0 - ADAM_B1 ** ADAM_STEP)
        v_hat = nv / (1.0 - ADAM_B2 ** ADAM_STEP)
        g_ref[...] = g
        d_ref[...] = -ADAM_LR * (m_hat / (jnp.sqrt(v_hat) + ADAM_EPS) + ADAM_WD * w_ref[...])
        nm_ref[...] = nm
        nv_ref[...] = nv

    return pl.pallas_call(
        body, name=name, grid=(rows // tr,), in_specs=[pl.BlockSpec((N_DEV, tr, cols), lambda i: (0, i, 0)), blk, blk, blk],
        out_specs=[blk] * 4, out_shape=[_sds((rows, cols))] * 4, compiler_params=_params(("parallel",)),
    )(parts, w, m, v)


def _row_tile(rows):
    for t in (256, 128, 64, 32, 16, 8):
        if rows % t == 0:
            return t
    return rows


def _gathered_cols(g):
    n, l, rows, c = g.shape
    return jnp.transpose(g, (1, 2, 0, 3)).reshape(l, rows, n * c)


def _gathered_rows(g):
    n, l, rows, c = g.shape
    return jnp.transpose(g, (1, 0, 2, 3)).reshape(l, n * rows, c)


def _col_parts(full):
    l, rows, c = full.shape
    return jnp.transpose(full.reshape(l, rows, N_DEV, c // N_DEV), (2, 0, 1, 3))


def _row_parts(full):
    l, rows, c = full.shape
    return jnp.transpose(full.reshape(l, N_DEV, rows // N_DEV, c), (1, 0, 2, 3))


def _w_in_split(w_in):
    s = lambda a, n: w_in[..., a:a + n]
    main = jnp.concatenate([s(R_QKV, 3072), s(R_Z, 1024), s(R_QSW, 1024), s(R_G, 2048), s(R_KSW, 256), s(R_VSW, 256)], axis=-1)
    gates = jnp.pad(s(R_BA, 2 * N_HD), ((0, 0), (0, 0), (0, LANES - 2 * N_HD)))
    return main, gates


def _w_in_join(dmain, dgates):
    s = lambda a, n: dmain[..., a:a + n]
    return jnp.concatenate([s(C_QKV, 3072), s(C_Z, 1024), dgates[..., :2 * N_HD], s(C_QSW, 1024), s(C_KSW, 256), s(C_VSW, 256),
                            s(C_GA, 2048)], axis=-1)


def _lane_row(a, offset):
    l, n = a.shape
    return jnp.pad(a, ((0, 0), (offset, LANES - offset - n)))[:, None, :]


def kernel(x, w_in, conv_w, a_log, dt_bias, dn_norm_w, sinks, w_branch_a, w_branch_b, w_out, ln1_g, ln1_b, w_gate_up, w_down, ln2_g, ln2_b, loss_target, m_w_in, m_conv_w, m_a_log, m_dt_bias, m_dn_norm_w, m_sinks, m_w_branch_a, m_w_branch_b, m_w_out, m_ln1_g, m_ln1_b, m_w_gate_up, m_w_down, m_ln2_g, m_ln2_b, v_w_in, v_conv_w, v_a_log, v_dt_bias, v_dn_norm_w, v_sinks, v_w_branch_a, v_w_branch_b, v_w_out, v_ln1_g, v_ln1_b, v_w_gate_up, v_w_down, v_ln2_g, v_ln2_b):
    l = DEPTH
    bf = lambda a: a.astype(_MXU)
    w_in_full = _gathered_cols(_all_gather(bf(w_in), "gather_w_in"))
    wgu_full = _gathered_cols(_all_gather(bf(w_gate_up), "gather_w_gate_up"))
    wa_full = _gathered_rows(_all_gather(bf(w_branch_a), "gather_w_branch_a"))
    wb_full = _gathered_rows(_all_gather(bf(w_branch_b), "gather_w_branch_b"))
    wo_full = _gathered_rows(_all_gather(bf(w_out), "gather_w_out"))
    wd_full = _gathered_rows(_all_gather(bf(w_down), "gather_w_down"))
    conv_full = _gathered_cols(_all_gather(conv_w, "gather_conv_w"))
    wm, wba = _w_in_split(w_in_full)
    row = lambda a: a[:, None, :]
    stacked = dict(
        wm=wm, wba=wba, conv=jnp.pad(conv_full, ((0, 0), (0, 8 - DN_CONV), (0, 0))),
        alog=_lane_row(a_log.reshape(l, N_HD), N_HD), dtb=_lane_row(dt_bias.reshape(l, N_HD), N_HD), dnw=row(dn_norm_w),
        sinks=sinks.reshape(l, SW_HEADS, 1, 1), wa=wa_full, wb=wb_full, wo=wo_full, ln1g=row(ln1_g), ln1b=row(ln1_b),
        wgu=wgu_full, wd=wd_full, ln2g=row(ln2_g), ln2b=row(ln2_b))

    sq, dx, g = _trunk(x[0], loss_target[0], stacked)
    loss = lax.psum(0.5 * sq[0, 0] / D_MODEL, ("x", "y", "c"))

    def big(parts, w, m, v, name):
        got = _all_to_all(parts, "exchange_" + name)
        rows = w.shape[0] * w.shape[1]
        flat = lambda a: a.reshape(rows, a.shape[-1])
        outs = _sum_adamw(got.reshape(N_DEV, rows, w.shape[-1]), flat(w), flat(m), flat(v), "adamw_" + name)
        return [o.reshape(w.shape) for o in outs]

    dconv = g["conv"][:, :DN_CONV, :]
    results = {
        "w_in": big(_col_parts(_w_in_join(g["wm"], g["wba"])), w_in, m_w_in, v_w_in, "w_in"),
        "conv_w": big(_col_parts(dconv), conv_w, m_conv_w, v_conv_w, "conv_w"),
        "w_branch_a": big(_row_parts(g["wa"]), w_branch_a, m_w_branch_a, v_w_branch_a, "w_branch_a"),
        "w_branch_b": big(_row_parts(g["wb"]), w_branch_b, m_w_branch_b, v_w_branch_b, "w_branch_b"),
        "w_out": big(_row_parts(g["wo"]), w_out, m_w_out, v_w_out, "w_out"),
        "w_gate_up": big(_col_parts(g["wgu"]), w_gate_up, m_w_gate_up, v_w_gate_up, "w_gate_up"),
        "w_down": big(_row_parts(g["wd"]), w_down, m_w_down, v_w_down, "w_down"),
    }

    small_w = {"a_log": a_log.reshape(l, N_HD), "dt_bias": dt_bias.reshape(l, N_HD), "dn_norm_w": dn_norm_w, "sinks": sinks,
               "ln1_g": ln1_g, "ln1_b": ln1_b, "ln2_g": ln2_g, "ln2_b": ln2_b}
    small_m = {"a_log": m_a_log, "dt_bias": m_dt_bias, "dn_norm_w": m_dn_norm_w, "sinks": m_sinks, "ln1_g": m_ln1_g,
               "ln1_b": m_ln1_b, "ln2_g": m_ln2_g, "ln2_b": m_ln2_b}
    small_v = {"a_log": v_a_log, "dt_bias": v_dt_bias, "dn_norm_w": v_dn_norm_w, "sinks": v_sinks, "ln1_g": v_ln1_g,
               "ln1_b": v_ln1_b, "ln2_g": v_ln2_g, "ln2_b": v_ln2_b}
    small_g = {"a_log": g["alog"][:, 0, N_HD:2 * N_HD], "dt_bias": g["dtb"][:, 0, N_HD:2 * N_HD], "dn_norm_w": g["dnw"][:, 0, :],
               "sinks": g["sinks"].reshape(l, SW_HEADS), "ln1_g": g["ln1g"][:, 0, :], "ln1_b": g["ln1b"][:, 0, :],
               "ln2_g": g["ln2g"][:, 0, :], "ln2_b": g["ln2b"][:, 0, :]}
    names = list(small_w)
    cat = lambda d: jnp.concatenate([d[n].reshape(l, -1) for n in names], axis=1)
    widths = [small_w[n].shape[1] for n in names]
    total = sum(widths)
    padded = -(-total // LANES) * LANES
    pad = lambda a: jnp.pad(a, ((0, 8 - l), (0, padded - total)))
    got = _all_gather(pad(cat(small_g)), "gather_small_grads")
    outs = _sum_adamw(got, pad(cat(small_w)), pad(cat({n: small_m[n].reshape(l, -1) for n in names})),
                      pad(cat({n: small_v[n].reshape(l, -1) for n in names})), "adamw_small")
    off = 0
    for n, wd_ in zip(names, widths):
        shape = {"a_log": a_log.shape, "dt_bias": dt_bias.shape}.get(n, small_w[n].shape)
        results[n] = [o[:l, off:off + wd_].reshape(shape) for o in outs]
        off += wd_

    order = ["w_in", "conv_w", "a_log", "dt_bias", "dn_norm_w", "sinks", "w_branch_a", "w_branch_b", "w_out", "ln1_g", "ln1_b",
             "w_gate_up", "w_down", "ln2_g", "ln2_b"]
    return (loss, dx[None], *[results[n][0] for n in order], *[results[n][1] for n in order],
            *[results[n][2] for n in order], *[results[n][3] for n in order])
```

```python
import functools

import jax
import jax.numpy as jnp
from jax import lax
from jax.experimental import pallas as pl
from jax.experimental.pallas import tpu as pltpu

F32 = jnp.float32
_MXU = jnp.bfloat16
_HI = lax.Precision.HIGHEST
_MID = lax.Precision.HIGH

N_DEV = 8
D_MODEL = 1024
DEPTH = 4
DN_HEADS = 8
DN_HEAD_DIM = 128
DN_WIDTH = DN_HEADS * DN_HEAD_DIM
DN_CONV = 5
DN_CHUNK = 64
SW_HEADS = 16
SW_KV_HEADS = 4
SW_HEAD_DIM = 64
SW_GROUP = SW_HEADS // SW_KV_HEADS
SW_BLOCK = 128
SW_KV_WIDTH = SW_KV_HEADS * SW_HEAD_DIM
ROPE_THETA = 10000.0
FFN_HIDDEN = 2816
DN_ALPHA = (2.0 * DEPTH) ** 0.25
LN_EPS = 1e-5
RMS_EPS = 1e-6
ADAM_LR = 0.001
ADAM_B1 = 0.9
ADAM_B2 = 0.999
ADAM_EPS = 1e-08
ADAM_WD = 0.01
ADAM_STEP = 10

LANES = 128
N_HD = 2 * DN_HEADS
LOCAL_ROWS = 512
REC_ROWS = 512
ROW_TILE = 256
VMEM_LIMIT = 48 << 20

C_QKV, C_Z, C_QSW, C_GA, C_GB, C_KSW, C_VSW = 0, 3072, 4096, 5120, 6144, 7168, 7424
MAIN_COLS = 7680
R_QKV, R_Z, R_BA, R_QSW, R_KSW, R_VSW, R_G = 0, 3072, 4096, 4128, 5152, 5408, 5664
IN_COLS = 7712


_NN = ((1,), (0,))
_NT = ((1,), (1,))
_TN = ((0,), (0,))


def _dg(a, b, dims, precision):
    if precision is not None:
        return lax.dot_general(a, b, (dims, ((), ())), precision=precision, preferred_element_type=F32)
    return lax.dot_general(a.astype(_MXU), b.astype(_MXU), (dims, ((), ())), preferred_element_type=F32)


def _make_dots(hi):
    @jax.custom_vjp
    def nn(a, b):
        return _dg(a, b, _NN, hi)

    @jax.custom_vjp
    def nt(a, b):
        return _dg(a, b, _NT, hi)

    @jax.custom_vjp
    def tn(a, b):
        return _dg(a, b, _TN, hi)

    nn.defvjp(lambda a, b: (nn(a, b), (a, b)), lambda r, g: (nt(g, r[1]), tn(r[0], g)))
    nt.defvjp(lambda a, b: (nt(a, b), (a, b)), lambda r, g: (nn(g, r[1]), tn(g, r[0])))
    tn.defvjp(lambda a, b: (tn(a, b), (a, b)), lambda r, g: (nt(r[1], g), nn(r[0], g)))
    return nn, nt, tn


_bnn, _bnt, _btn = _make_dots(None)
_hnn, _hnt, _htn = _make_dots(_HI)
_mnn, _mnt, _mtn = _make_dots(_MID)


def _inv_unit(a):
    n = a.shape[0]
    eye = (lax.broadcasted_iota(jnp.int32, (n, n), 0) == lax.broadcasted_iota(jnp.int32, (n, n), 1)).astype(F32)
    inv = eye - a
    p = a
    span = 2
    while span < n:
        p = _mnn(p, p)
        inv = inv + _mnn(inv, p)
        span *= 2
    return inv


def _inv_unit_t(t, g):
    return -_mnt(_mtn(t, g), t)


def _silu(x):
    return x * jax.nn.sigmoid(x)


def _softplus(x):
    return jnp.maximum(x, 0.0) + jnp.log1p(jnp.exp(-jnp.abs(x)))


def _params(sem=None):
    kw = {"vmem_limit_bytes": VMEM_LIMIT}
    if sem is not None:
        kw["dimension_semantics"] = sem
    return pltpu.CompilerParams(**kw)


def _tile(dim, pref):
    if dim <= pref:
        return dim
    t = (pref // LANES) * LANES
    while t > LANES and dim % t:
        t -= LANES
    assert dim % t == 0, (dim, pref)
    return t


def _full(shape):
    zeros = (0,) * len(shape)
    return pl.BlockSpec(shape, lambda *_: zeros)


def _sds(shape, dtype=F32):
    return jax.ShapeDtypeStruct(shape, dtype)


def _mm(a, b, mode, *, name, add=None, tm=512, tn=512, tk=1536, out_dtype=F32):
    if mode == "nn":
        (m, k), (k2, n) = a.shape, b.shape
    elif mode == "nt":
        (m, k), (n, k2) = a.shape, b.shape
    else:
        (k, m), (k2, n) = a.shape, b.shape
    assert k == k2, (a.shape, b.shape, mode)
    tm, tn, tk = _tile(m, tm), _tile(n, tn), _tile(k, tk)
    nk = k // tk
    dims = {"nn": _NN, "nt": _NT, "tn": _TN}[mode]

    def body(*refs):
        if add is None:
            a_ref, b_ref, o_ref, acc = refs
        else:
            a_ref, b_ref, add_ref, o_ref, acc = refs
        kk = pl.program_id(2)

        @pl.when(kk == 0)
        def _():
            acc[...] = jnp.zeros_like(acc)

        acc[...] += _dg(a_ref[...], b_ref[...], dims, None)

        @pl.when(kk == nk - 1)
        def _():
            o_ref[...] = (acc[...] if add is None else acc[...] + add_ref[...]).astype(out_dtype)

    a_spec = pl.BlockSpec((tk, tm), lambda i, j, kk: (kk, i)) if mode == "tn" else pl.BlockSpec((tm, tk), lambda i, j, kk: (i, kk))
    b_spec = pl.BlockSpec((tn, tk), lambda i, j, kk: (j, kk)) if mode == "nt" else pl.BlockSpec((tk, tn), lambda i, j, kk: (kk, j))
    o_spec = pl.BlockSpec((tm, tn), lambda i, j, kk: (i, j))
    ins, specs = [a, b], [a_spec, b_spec]
    if add is not None:
        ins.append(add)
        specs.append(o_spec)
    return pl.pallas_call(
        body, name=name, grid=(m // tm, n // tn, nk), in_specs=specs, out_specs=o_spec,
        out_shape=_sds((m, n), out_dtype), scratch_shapes=[pltpu.VMEM((tm, tn), F32)],
        compiler_params=_params(("parallel", "parallel", "arbitrary")),
    )(*ins)


def _cols(width, start):
    assert start % width == 0
    return pl.BlockSpec((ROW_TILE, width), lambda i: (i, start // width))


def _rows(width):
    return pl.BlockSpec((ROW_TILE, width), lambda i: (i, 0))


def _accumulate(ref, value, step):
    @pl.when(step == 0)
    def _():
        ref[...] = value

    @pl.when(step != 0)
    def _():
        ref[...] += value


def _ln_fn(x, r, g, b):
    u = DN_ALPHA * x + r
    mu = jnp.mean(u, axis=-1, keepdims=True)
    var = jnp.mean(jnp.square(u - mu), axis=-1, keepdims=True)
    return (u - mu) * lax.rsqrt(var + LN_EPS) * g + b


def _ln_fwd(x, r, g, b, name):
    t, d = x.shape

    def body(x_ref, r_ref, g_ref, b_ref, o_ref, ob_ref):
        y = _ln_fn(x_ref[...], r_ref[...], g_ref[...], b_ref[...])
        o_ref[...] = y
        ob_ref[...] = y.astype(_MXU)

    return pl.pallas_call(
        body, name=name, grid=(t // ROW_TILE,), in_specs=[_rows(d), _rows(d), _full((1, d)), _full((1, d))],
        out_specs=[_rows(d), _rows(d)], out_shape=[_sds((t, d)), _sds((t, d), _MXU)], compiler_params=_params(("parallel",)),
    )(x, r, g, b)


def _ln_bwd(x, r, g, b, dy, name):
    t, d = x.shape

    def body(x_ref, r_ref, g_ref, b_ref, dy_ref, dx_ref, dr_ref, dg_ref, db_ref):
        _, vjp = jax.vjp(_ln_fn, x_ref[...], r_ref[...], g_ref[...], b_ref[...])
        dx, dr, dg, db = vjp(dy_ref[...])
        dx_ref[...] = dx
        dr_ref[...] = dr.astype(_MXU)
        _accumulate(dg_ref, dg, pl.program_id(0))
        _accumulate(db_ref, db, pl.program_id(0))

    return pl.pallas_call(
        body, name=name, grid=(t // ROW_TILE,),
        in_specs=[_rows(d), _rows(d), _full((1, d)), _full((1, d)), _rows(d)],
        out_specs=[_rows(d), _rows(d), _full((1, d)), _full((1, d))],
        out_shape=[_sds((t, d)), _sds((t, d), _MXU), _sds((1, d)), _sds((1, d))],
        compiler_params=_params(("arbitrary",)),
    )(x, r, g, b, dy)


def _merge_fn(ga, gb, ya, yb):
    return jax.nn.sigmoid(ga) * ya + jax.nn.sigmoid(gb) * yb


def _merge_fwd(proj, ya, yb):
    t, d = ya.shape

    def body(ga_ref, gb_ref, ya_ref, yb_ref, o_ref):
        o_ref[...] = _merge_fn(ga_ref[...], gb_ref[...], ya_ref[...], yb_ref[...]).astype(_MXU)

    return pl.pallas_call(
        body, name="merge_fwd", grid=(t // ROW_TILE,), in_specs=[_cols(d, C_GA), _cols(d, C_GB), _rows(d), _rows(d)],
        out_specs=_rows(d), out_shape=_sds((t, d), _MXU), compiler_params=_params(("parallel",)),
    )(proj, proj, ya, yb)


def _merge_bwd(proj, ya, yb, dm):
    t, d = ya.shape

    def body(ga_ref, gb_ref, ya_ref, yb_ref, dm_ref, dga_ref, dgb_ref, dya_ref, dyb_ref):
        _, vjp = jax.vjp(_merge_fn, ga_ref[...], gb_ref[...], ya_ref[...], yb_ref[...])
        dga_ref[...], dgb_ref[...], dya_ref[...], dyb_ref[...] = [g.astype(_MXU) for g in vjp(dm_ref[...])]

    return pl.pallas_call(
        body, name="merge_bwd", grid=(t // ROW_TILE,),
        in_specs=[_cols(d, C_GA), _cols(d, C_GB), _rows(d), _rows(d), _rows(d)],
        out_specs=[_rows(d)] * 4, out_shape=[_sds((t, d), _MXU)] * 4, compiler_params=_params(("parallel",)),
    )(proj, proj, ya, yb, dm)


def _swiglu_fn(gate, up):
    return _silu(gate) * up


def _swiglu_fwd(gu):
    t = gu.shape[0]
    f = FFN_HIDDEN
    rows = 128

    def body(gu_ref, o_ref):
        o_ref[...] = _swiglu_fn(gu_ref[:, :f], gu_ref[:, f:]).astype(_MXU)

    return pl.pallas_call(
        body, name="swiglu_fwd", grid=(t // rows,), in_specs=[pl.BlockSpec((rows, 2 * f), lambda i: (i, 0))],
        out_specs=pl.BlockSpec((rows, f), lambda i: (i, 0)), out_shape=_sds((t, f), _MXU), compiler_params=_params(("parallel",)),
    )(gu)


def _swiglu_bwd(gu, dh):
    t = gu.shape[0]
    f = FFN_HIDDEN
    rows = 128

    def body(gu_ref, dh_ref, o_ref):
        _, vjp = jax.vjp(_swiglu_fn, gu_ref[:, :f], gu_ref[:, f:])
        o_ref[:, :f], o_ref[:, f:] = [g.astype(_MXU) for g in vjp(dh_ref[...])]

    return pl.pallas_call(
        body, name="swiglu_bwd", grid=(t // rows,),
        in_specs=[pl.BlockSpec((rows, 2 * f), lambda i: (i, 0)), pl.BlockSpec((rows, f), lambda i: (i, 0))],
        out_specs=pl.BlockSpec((rows, 2 * f), lambda i: (i, 0)), out_shape=_sds((t, 2 * f), _MXU),
        compiler_params=_params(("parallel",)),
    )(gu, dh)


def _loss_head(y, target):
    t, d = y.shape

    def body(y_ref, t_ref, s_ref, dy_ref):
        err = y_ref[...] - t_ref[...]
        dy_ref[...] = err / d
        _accumulate(s_ref, jnp.broadcast_to(jnp.sum(jnp.square(err)), (1, LANES)), pl.program_id(0))

    return pl.pallas_call(
        body, name="loss_head", grid=(t // ROW_TILE,), in_specs=[_rows(d), _rows(d)],
        out_specs=[_full((1, LANES)), _rows(d)], out_shape=[_sds((1, LANES)), _sds((t, d))],
        compiler_params=_params(("arbitrary",)),
    )(y, target)


def _shift_rows(x, s):
    if s == 0:
        return x
    t = x.shape[0]
    rolled = pltpu.roll(x, (-s) % t, 0)
    row = lax.broadcasted_iota(jnp.int32, x.shape, 0)
    return jnp.where((row + s >= 0) & (row + s < t), rolled, 0.0)


def _conv(x, w):
    half = DN_CONV // 2
    acc = None
    for k in range(DN_CONV):
        term = _shift_rows(x, k - half) * w[k:k + 1, :]
        acc = term if acc is None else acc + term
    return acc


def _act_norm(c, do_norm, scale):
    a = _silu(c)
    if not do_norm:
        return a
    return a * lax.rsqrt(jnp.sum(a * a, axis=-1, keepdims=True) + RMS_EPS) * scale


PREP_ROWS = 512
HALO = 8
_KINDS = ((True, DN_HEAD_DIM ** -0.5), (True, 1.0), (False, 1.0))


def _halo_rows(read, i, pr, t):
    lo, hi = i * pr - HALO, (i + 1) * pr + HALO
    parts = []
    if lo < 0:
        parts.append(jnp.zeros((HALO, LANES), F32))
    parts.append(read(max(lo, 0), min(hi, t)))
    if hi > t:
        parts.append(jnp.zeros((HALO, LANES), F32))
    return jnp.concatenate(parts, axis=0) if len(parts) > 1 else parts[0]


def _prep_fwd(proj, conv_w, kind):
    t = proj.shape[0]
    pr = min(PREP_ROWS, t)
    do_norm, scale = _KINDS[kind]
    blk = pl.BlockSpec((t, LANES), lambda j: (0, kind * DN_HEADS + j))

    def body(x_ref, w_ref, o_ref):
        w = w_ref[...]
        for i in range(t // pr):
            xx = _halo_rows(lambda lo, hi: x_ref[lo:hi, :], i, pr, t)
            c = _conv(xx, w)[HALO:HALO + pr, :]
            o_ref[i * pr:(i + 1) * pr, :] = _act_norm(c, do_norm, scale)

    return pl.pallas_call(
        body, name=f"prep_fwd_{kind}", grid=(DN_HEADS,),
        in_specs=[blk, pl.BlockSpec((8, LANES), lambda j: (0, kind * DN_HEADS + j))],
        out_specs=pl.BlockSpec((t, LANES), lambda j: (0, j)), out_shape=_sds((t, DN_WIDTH)),
        compiler_params=_params(("parallel",)),
    )(proj, conv_w)


def _prep_bwd(proj, conv_w, d2, kind):
    t = proj.shape[0]
    pr = min(PREP_ROWS, t)
    do_norm, scale = _KINDS[kind]
    half = DN_CONV // 2
    blk = pl.BlockSpec((t, LANES), lambda j: (0, kind * DN_HEADS + j))
    oblk = pl.BlockSpec((t, LANES), lambda j: (0, j))

    def body(x_ref, w_ref, d_ref, dx_ref, dw_ref):
        w = w_ref[...]
        own = slice(HALO, HALO + pr)
        dw = jnp.zeros((8, LANES), F32)
        for i in range(t // pr):
            xx = _halo_rows(lambda lo, hi: x_ref[lo:hi, :], i, pr, t)
            dn = _halo_rows(lambda lo, hi: d_ref[0, lo:hi, :] + d_ref[1, lo:hi, :], i, pr, t)
            _, vjp = jax.vjp(lambda c: _act_norm(c, do_norm, scale), _conv(xx, w))
            (dc,) = vjp(dn)
            dx = None
            rows = []
            for k in range(DN_CONV):
                term = _shift_rows(dc, half - k) * w[k:k + 1, :]
                dx = term if dx is None else dx + term
                rows.append(jnp.sum(dc[own, :] * _shift_rows(xx, k - half)[own, :], axis=0, keepdims=True))
            dx_ref[i * pr:(i + 1) * pr, :] = dx[own, :].astype(_MXU)
            dw = dw + jnp.concatenate(rows + [jnp.zeros((8 - DN_CONV, LANES), F32)], axis=0)
        dw_ref[...] = dw

    return pl.pallas_call(
        body, name=f"prep_bwd_{kind}", grid=(DN_HEADS,),
        in_specs=[blk, pl.BlockSpec((8, LANES), lambda j: (0, kind * DN_HEADS + j)), pl.BlockSpec((2, t, LANES), lambda j: (0, 0, j))],
        out_specs=[oblk, pl.BlockSpec((8, LANES), lambda j: (0, j))], out_shape=[_sds((t, DN_WIDTH), _MXU), _sds((8, DN_WIDTH))],
        compiler_params=_params(("parallel",)),
    )(proj, conv_w, d2)


def _gb_fn(ba, alog_row, dtb_row):
    c = DN_CHUNK
    lane = lax.broadcasted_iota(jnp.int32, (c, LANES), 1)
    ii = lax.broadcasted_iota(jnp.int32, (c, c), 0)
    jj = lax.broadcasted_iota(jnp.int32, (c, c), 1)
    beta = jax.nn.sigmoid(ba)
    g = -jnp.exp(alog_row) * _softplus(ba + dtb_row)
    g = jnp.where((lane >= N_HD) & (lane < 2 * N_HD), g, 0.0)
    gc_fwd = _hnn((ii >= jj).astype(F32), g)
    gc_rev = _hnn((ii <= jj).astype(F32), g)
    gc = jnp.where(lane < N_HD + DN_HEADS, gc_fwd, gc_rev)
    return jnp.where(lane < N_HD, beta, jnp.where(lane < 2 * N_HD, gc, 0.0))


def _gb_fwd(ba, alog_row, dtb_row):
    t = ba.shape[0]
    n = ROW_TILE // DN_CHUNK

    def body(ba_ref, a_ref, d_ref, o_ref):
        for c in range(n):
            rows = slice(c * DN_CHUNK, (c + 1) * DN_CHUNK)
            o_ref[rows, :] = _gb_fn(ba_ref[rows, :], a_ref[...], d_ref[...])

    return pl.pallas_call(
        body, name="gates_fwd", grid=(t // ROW_TILE,), in_specs=[_rows(LANES), _full((1, LANES)), _full((1, LANES))],
        out_specs=_rows(LANES), out_shape=_sds((t, LANES)), compiler_params=_params(("parallel",)),
    )(ba, alog_row, dtb_row)


def _gb_bwd(ba, alog_row, dtb_row, d_a, d_b):
    t = ba.shape[0]
    n = ROW_TILE // DN_CHUNK

    def body(ba_ref, a_ref, d_ref, da_ref, db_ref, dba_ref, dal_ref, ddt_ref):
        dal = jnp.zeros((1, LANES), F32)
        ddt = jnp.zeros((1, LANES), F32)
        for c in range(n):
            rows = slice(c * DN_CHUNK, (c + 1) * DN_CHUNK)
            _, vjp = jax.vjp(_gb_fn, ba_ref[rows, :], a_ref[...], d_ref[...])
            dba, da, dd = vjp(da_ref[rows, :] + db_ref[rows, :])
            dba_ref[rows, :] = dba.astype(_MXU)
            dal = dal + da
            ddt = ddt + dd
        _accumulate(dal_ref, dal, pl.program_id(0))
        _accumulate(ddt_ref, ddt, pl.program_id(0))

    return pl.pallas_call(
        body, name="gates_bwd", grid=(t // ROW_TILE,),
        in_specs=[_rows(LANES), _full((1, LANES)), _full((1, LANES)), _rows(LANES), _rows(LANES)],
        out_specs=[_rows(LANES), _full((1, LANES)), _full((1, LANES))],
        out_shape=[_sds((t, LANES), _MXU), _sds((1, LANES)), _sds((1, LANES))], compiler_params=_params(("arbitrary",)),
    )(ba, alog_row, dtb_row, d_a, d_b)


def _dn_decay(gcc, gcr, sgn):
    c = DN_CHUNK
    ii = lax.broadcasted_iota(jnp.int32, (c, c), 0)
    jj = lax.broadcasted_iota(jnp.int32, (c, c), 1)
    d = (ii - jj) * sgn
    lower = d >= 0
    return jnp.where(lower, jnp.exp(jnp.where(lower, gcc - gcr, 0.0)), 0.0), d > 0


def _dn_a(k, beta, gcc, gcr, sgn):
    decay, strict = _dn_decay(gcc, gcr, sgn)
    return jnp.where(strict, beta * _bnt(k, k) * decay, 0.0)


def _dn_local(t_inv, q, k, v, beta, gcc, gcr, sgn):
    c = DN_CHUNK
    decay, _ = _dn_decay(gcc, gcr, sgn)
    eg = jnp.exp(gcc)
    u = _bnn(t_inv, v * beta)
    w = _bnn(t_inv, k * (beta * eg))
    qk = _bnt(q, k) * decay
    qd = q * eg
    last = jnp.where(sgn > 0, c - 1, 0)
    onehot = (lax.broadcasted_iota(jnp.int32, (c, 1), 0) == last).astype(F32)
    gl = jnp.sum(gcc * onehot, axis=0, keepdims=True)
    kd = k * jnp.exp(gl - gcc)
    egl = jnp.broadcast_to(jnp.exp(gl), (1, LANES))
    return u, w, qk, qd, kd, egl


def _hd_sign(hd):
    return jnp.where(hd < DN_HEADS, 1, -1).astype(jnp.int32)


def _head_of(hd):
    return jnp.where(hd < DN_HEADS, hd, hd - DN_HEADS)


def _dir_of(hd):
    return jnp.where(hd < DN_HEADS, 0, 1)


def _dn_specs(t):
    nl = LOCAL_ROWS // DN_CHUNK
    wide = pl.BlockSpec((1, LOCAL_ROWS, LANES), lambda hd, i: (hd, i, 0))
    half = pl.BlockSpec((1, LOCAL_ROWS, DN_CHUNK), lambda hd, i: (hd, i, 0))
    col = pl.BlockSpec((1, LOCAL_ROWS, 1), lambda hd, i: (hd, i, 0))
    row = pl.BlockSpec((1, nl, 1, DN_CHUNK), lambda hd, i: (hd, i, 0, 0))
    egl = pl.BlockSpec((1, nl, 1, LANES), lambda hd, i: (hd, i, 0, 0))
    return wide, half, col, row, egl


def _qkv_specs():
    return [pl.BlockSpec((LOCAL_ROWS, LANES), lambda hd, i: (i, _head_of(hd)))] * 3


def _dn_local_fwd(q, k, v, beta_c, gc_c, gc_r):
    t = q.shape[0]
    nc = t // DN_CHUNK
    nl = LOCAL_ROWS // DN_CHUNK
    wide, half, col, row, egl = _dn_specs(t)

    def body(q_ref, k_ref, v_ref, b_ref, gc_ref, gr_ref, u_ref, w_ref, qk_ref, qd_ref, kd_ref, egl_ref, t_ref):
        sgn = _hd_sign(pl.program_id(0))
        for c in range(nl):
            rows = slice(c * DN_CHUNK, (c + 1) * DN_CHUNK)
            k, beta, gcc, gcr = k_ref[rows, :], b_ref[0, rows, :], gc_ref[0, rows, :], gr_ref[0, c]
            t_inv = _inv_unit(_dn_a(k, beta, gcc, gcr, sgn))
            outs = _dn_local(t_inv, q_ref[rows, :], k, v_ref[rows, :], beta, gcc, gcr, sgn)
            u_ref[0, rows, :], w_ref[0, rows, :], qk_ref[0, rows, :], qd_ref[0, rows, :], kd_ref[0, rows, :], egl_ref[0, c] = outs
            t_ref[0, rows, :] = t_inv

    big = _sds((N_HD, t, LANES))
    small = _sds((N_HD, t, DN_CHUNK))
    return pl.pallas_call(
        body, name="dn_local_fwd", grid=(N_HD, t // LOCAL_ROWS), in_specs=_qkv_specs() + [col, col, row],
        out_specs=[wide, wide, half, wide, wide, egl, half],
        out_shape=[big, big, small, big, big, _sds((N_HD, nc, 1, LANES)), small],
        compiler_params=_params(("parallel", "parallel")),
    )(q, k, v, beta_c, gc_c, gc_r)


def _dn_local_bwd(q, k, v, beta_c, gc_c, gc_r, t_inv, du, dw, dqk, dqd, dkd, degl):
    t = q.shape[0]
    nc = t // DN_CHUNK
    nl = LOCAL_ROWS // DN_CHUNK
    wide, half, col, row, egl = _dn_specs(t)
    dspec = pl.BlockSpec((1, LOCAL_ROWS, LANES), lambda hd, i: (_dir_of(hd), i, _head_of(hd)))

    def body(q_ref, k_ref, v_ref, b_ref, gc_ref, gr_ref, t_ref, du_ref, dw_ref, dqk_ref, dqd_ref, dkd_ref, degl_ref,
             dq_ref, dk_ref, dv_ref, db_ref, dgc_ref, dgr_ref):
        sgn = _hd_sign(pl.program_id(0))
        for c in range(nl):
            rows = slice(c * DN_CHUNK, (c + 1) * DN_CHUNK)
            k, beta, gcc, gcr, tinv = k_ref[rows, :], b_ref[0, rows, :], gc_ref[0, rows, :], gr_ref[0, c], t_ref[0, rows, :]
            _, vjp = jax.vjp(functools.partial(_dn_local, sgn=sgn), tinv, q_ref[rows, :], k, v_ref[rows, :], beta, gcc, gcr)
            dt, dq, dk, dv, db, dgc, dgr = vjp((du_ref[0, rows, :], dw_ref[0, rows, :], dqk_ref[0, rows, :],
                                                dqd_ref[0, rows, :], dkd_ref[0, rows, :], degl_ref[0, c]))
            _, vjp_a = jax.vjp(functools.partial(_dn_a, sgn=sgn), k, beta, gcc, gcr)
            dk2, db2, dgc2, dgr2 = vjp_a(_inv_unit_t(tinv, dt))
            dq_ref[0, rows, :] = dq
            dk_ref[0, rows, :] = dk + dk2
            dv_ref[0, rows, :] = dv
            db_ref[0, rows, :] = db + db2
            dgc_ref[0, rows, :] = dgc + dgc2
            dgr_ref[0, c] = dgr + dgr2

    per_dir = _sds((2, t, DN_WIDTH))
    return pl.pallas_call(
        body, name="dn_local_bwd", grid=(N_HD, t // LOCAL_ROWS),
        in_specs=_qkv_specs() + [col, col, row, half, wide, wide, half, wide, wide, egl],
        out_specs=[dspec, dspec, dspec, col, col, row],
        out_shape=[per_dir, per_dir, per_dir, _sds((N_HD, t, 1)), _sds((N_HD, t, 1)), _sds((N_HD, nc, 1, DN_CHUNK))],
        compiler_params=_params(("parallel", "parallel")),
    )(q, k, v, beta_c, gc_c, gc_r, t_inv, du, dw, dqk, dqd, dkd, degl)


def _rec_specs(time_block):
    nr = REC_ROWS // DN_CHUNK
    wide = pl.BlockSpec((1, REC_ROWS, LANES), lambda hd, b: (hd, time_block(hd, b), 0))
    half = pl.BlockSpec((1, REC_ROWS, DN_CHUNK), lambda hd, b: (hd, time_block(hd, b), 0))
    egl = pl.BlockSpec((1, nr, 1, LANES), lambda hd, b: (hd, time_block(hd, b), 0, 0))
    state = pl.BlockSpec((1, nr, DN_HEAD_DIM, DN_HEAD_DIM), lambda hd, b: (hd, time_block(hd, b), 0, 0))
    return wide, half, egl, state


def _dn_rec_fwd(u, w, qk, qd, kd, egl):
    t = u.shape[1]
    nb = t // REC_ROWS
    nr = REC_ROWS // DN_CHUNK
    nc = t // DN_CHUNK

    def time_block(hd, b):
        return jnp.where(hd < DN_HEADS, b, nb - 1 - b)

    wide, half, egl_spec, state = _rec_specs(time_block)
    o_spec = pl.BlockSpec((1, REC_ROWS, LANES), lambda hd, b: (_dir_of(hd), time_block(hd, b), _head_of(hd)))

    def body(u_ref, w_ref, qk_ref, qd_ref, kd_ref, egl_ref, o_ref, vn_ref, s_ref, s_scr):
        fwd = pl.program_id(0) < DN_HEADS

        @pl.when(pl.program_id(1) == 0)
        def _():
            s_scr[...] = jnp.zeros_like(s_scr)

        s = s_scr[...]
        for c in range(nr):
            ce = jnp.where(fwd, c, nr - 1 - c)
            rows = pl.ds(pl.multiple_of(ce * DN_CHUNK, DN_CHUNK), DN_CHUNK)
            s_ref[0, ce] = s
            vn = u_ref[0, rows, :] - _bnn(w_ref[0, rows, :], s)
            o_ref[0, rows, :] = _bnn(qd_ref[0, rows, :], s) + _bnn(qk_ref[0, rows, :], vn)
            vn_ref[0, rows, :] = vn
            s = s * egl_ref[0, ce] + _btn(kd_ref[0, rows, :], vn)
        s_scr[...] = s

    return pl.pallas_call(
        body, name="dn_rec_fwd", grid=(N_HD, nb), in_specs=[wide, wide, half, wide, wide, egl_spec],
        out_specs=[o_spec, wide, state],
        out_shape=[_sds((2, t, DN_WIDTH)), _sds((N_HD, t, LANES)), _sds((N_HD, nc, DN_HEAD_DIM, DN_HEAD_DIM))],
        scratch_shapes=[pltpu.VMEM((DN_HEAD_DIM, DN_HEAD_DIM), F32)],
        compiler_params=_params(("parallel", "arbitrary")),
    )(u, w, qk, qd, kd, egl)


def _dn_rec_bwd(w, qk, qd, kd, egl, vn, states, do):
    t = w.shape[1]
    nb = t // REC_ROWS
    nr = REC_ROWS // DN_CHUNK
    nc = t // DN_CHUNK

    def time_block(hd, b):
        return jnp.where(hd < DN_HEADS, nb - 1 - b, b)

    wide, half, egl_spec, state = _rec_specs(time_block)
    do_spec = pl.BlockSpec((REC_ROWS, LANES), lambda hd, b: (time_block(hd, b), _head_of(hd)))

    def body(w_ref, qk_ref, qd_ref, kd_ref, egl_ref, vn_ref, s_ref, do_ref,
             du_ref, dw_ref, dqk_ref, dqd_ref, dkd_ref, degl_ref, ds_scr):
        fwd = pl.program_id(0) < DN_HEADS

        @pl.when(pl.program_id(1) == 0)
        def _():
            ds_scr[...] = jnp.zeros_like(ds_scr)

        ds = ds_scr[...]
        for c in range(nr):
            ce = jnp.where(fwd, nr - 1 - c, c)
            rows = pl.ds(pl.multiple_of(ce * DN_CHUNK, DN_CHUNK), DN_CHUNK)
            s = s_ref[0, ce]
            do_c = do_ref[rows, :]
            vn_c = vn_ref[0, rows, :]
            qk_c = qk_ref[0, rows, :]
            kd_c = kd_ref[0, rows, :]
            dvn = _btn(qk_c, do_c) + _bnn(kd_c, ds)
            du_ref[0, rows, :] = dvn
            dw_ref[0, rows, :] = -_bnt(dvn, s)
            dqk_ref[0, rows, :] = _bnt(do_c, vn_c)
            dqd_ref[0, rows, :] = _bnt(do_c, s)
            dkd_ref[0, rows, :] = _bnt(vn_c, ds)
            degl_ref[0, ce] = jnp.sum(s * ds, axis=0, keepdims=True)
            ds = ds * egl_ref[0, ce] + _btn(qd_ref[0, rows, :], do_c) - _btn(w_ref[0, rows, :], dvn)
        ds_scr[...] = ds

    big = _sds((N_HD, t, LANES))
    return pl.pallas_call(
        body, name="dn_rec_bwd", grid=(N_HD, nb), in_specs=[wide, half, wide, wide, egl_spec, wide, state, do_spec],
        out_specs=[wide, wide, half, wide, wide, egl_spec],
        out_shape=[big, big, _sds((N_HD, t, DN_CHUNK)), big, big, _sds((N_HD, nc, 1, LANES))],
        scratch_shapes=[pltpu.VMEM((DN_HEAD_DIM, DN_HEAD_DIM), F32)],
        compiler_params=_params(("parallel", "arbitrary")),
    )(w, qk, qd, kd, egl, vn, states, do)


def _post_fn(of, ob, z, gain):
    o = of + ob
    return o * lax.rsqrt(jnp.mean(o * o, axis=-1, keepdims=True) + RMS_EPS) * gain * _silu(z)


def _post_specs():
    o_spec = [pl.BlockSpec((1, ROW_TILE, LANES), functools.partial(lambda i, h, d: (d, i, h), d=d)) for d in (0, 1)]
    z_spec = pl.BlockSpec((ROW_TILE, LANES), lambda i, h: (i, C_Z // LANES + h))
    head = pl.BlockSpec((ROW_TILE, LANES), lambda i, h: (i, h))
    gain = pl.BlockSpec((1, LANES), lambda i, h: (0, 0))
    return o_spec, z_spec, head, gain


def _post_fwd(o2, proj, gain):
    t = proj.shape[0]
    o_spec, z_spec, head, gain_spec = _post_specs()

    def body(of_ref, ob_ref, z_ref, g_ref, out_ref):
        out_ref[...] = _post_fn(of_ref[0], ob_ref[0], z_ref[...], g_ref[...]).astype(_MXU)

    return pl.pallas_call(
        body, name="post_fwd", grid=(t // ROW_TILE, DN_HEADS), in_specs=o_spec + [z_spec, gain_spec], out_specs=head,
        out_shape=_sds((t, DN_WIDTH), _MXU), compiler_params=_params(("parallel", "parallel")),
    )(o2, o2, proj, gain)


def _post_bwd(o2, proj, gain, dout):
    t = proj.shape[0]
    o_spec, z_spec, head, gain_spec = _post_specs()

    def body(of_ref, ob_ref, z_ref, g_ref, d_ref, do_ref, dz_ref, dg_ref):
        _, vjp = jax.vjp(_post_fn, of_ref[0], ob_ref[0], z_ref[...], g_ref[...])
        do, _, dz, dg = vjp(d_ref[...])
        do_ref[...] = do
        dz_ref[...] = dz.astype(_MXU)
        _accumulate(dg_ref, dg, pl.program_id(0) * DN_HEADS + pl.program_id(1))

    return pl.pallas_call(
        body, name="post_bwd", grid=(t // ROW_TILE, DN_HEADS), in_specs=o_spec + [z_spec, gain_spec, head],
        out_specs=[head, head, gain_spec], out_shape=[_sds((t, DN_WIDTH)), _sds((t, DN_WIDTH), _MXU), _sds((1, LANES))],
        compiler_params=_params(("arbitrary", "arbitrary")),
    )(o2, o2, proj, gain, dout)


def _rope(x, cos, sin):
    lane = lax.broadcasted_iota(jnp.int32, x.shape, 1)
    first = (lane & (SW_HEAD_DIM - 1)) < SW_HEAD_DIM // 2
    rot = jnp.where(first, -pltpu.roll(x, LANES - SW_HEAD_DIM // 2, 1), pltpu.roll(x, SW_HEAD_DIM // 2, 1))
    return x * cos + rot * sin


def _rope_apply(q, k, q_cols, k_cols, cos, sin, name, dtype):
    t = cos.shape[0]
    qw, kw = SW_HEADS * SW_HEAD_DIM, SW_KV_WIDTH

    def body(q_ref, k_ref, c_ref, s_ref, qo_ref, ko_ref):
        c, s = c_ref[...], s_ref[...]
        for j in range(qw // LANES):
            cols = slice(j * LANES, (j + 1) * LANES)
            qo_ref[:, cols] = _rope(q_ref[:, cols], c, s).astype(dtype)
        for j in range(kw // LANES):
            cols = slice(j * LANES, (j + 1) * LANES)
            ko_ref[:, cols] = _rope(k_ref[:, cols], c, s).astype(dtype)

    return pl.pallas_call(
        body, name=name, grid=(t // ROW_TILE,), in_specs=[_cols(qw, q_cols), _cols(kw, k_cols), _rows(LANES), _rows(LANES)],
        out_specs=[_rows(qw), _rows(kw)], out_shape=[_sds((t, qw), dtype), _sds((t, kw), dtype)],
        compiler_params=_params(("parallel",)),
    )(q, k, cos, sin)


def _attn_core(qs, kb, vb, sink, mask):
    s = _bnt(qs, kb) * (SW_HEAD_DIM ** -0.5)
    s = jnp.where(mask, s, -1e30)
    m = lax.stop_gradient(jnp.maximum(jnp.max(s, axis=1, keepdims=True), sink))
    e = jnp.exp(s - m)
    den = jnp.sum(e, axis=1, keepdims=True) + jnp.exp(sink - m)
    return _bnn(e / den, vb)


def _band_mask(n, nb):
    rows = SW_GROUP * SW_BLOCK
    i = lax.broadcasted_iota(jnp.int32, (rows, 3 * SW_BLOCK), 0) & (SW_BLOCK - 1)
    j = lax.broadcasted_iota(jnp.int32, (rows, 3 * SW_BLOCK), 1)
    near = (j - i >= 0) & (j - i <= 2 * SW_BLOCK)
    lo = jnp.where(n == 0, SW_BLOCK, 0)
    hi = jnp.where(n == nb - 1, 2 * SW_BLOCK, 3 * SW_BLOCK)
    return near & (j >= lo) & (j < hi)


def _band_specs(nb, v_cols):
    def spec(width, base, shift):
        return pl.BlockSpec((SW_BLOCK, width), lambda n: (jnp.clip(n + shift, 0, nb - 1), base // width))
    k_specs = [spec(SW_KV_WIDTH, 0, s) for s in (-1, 0, 1)]
    v_specs = [spec(SW_KV_WIDTH, v_cols, s) for s in (-1, 0, 1)]
    return k_specs, v_specs


def _head_cols(kv, g):
    h = kv * SW_GROUP + g
    return slice(h * SW_HEAD_DIM, (h + 1) * SW_HEAD_DIM)


def _attn_fwd(qr, kr, proj, sinks):
    t = qr.shape[0]
    nb = t // SW_BLOCK
    qw = SW_HEADS * SW_HEAD_DIM
    k_specs, v_specs = _band_specs(nb, C_VSW)
    q_spec = pl.BlockSpec((SW_BLOCK, qw), lambda n: (n, 0))

    def body(q_ref, k0, k1, k2, v0, v1, v2, s_ref, o_ref):
        mask = _band_mask(pl.program_id(0), nb)
        kb = jnp.concatenate([k0[...], k1[...], k2[...]], axis=0)
        vb = jnp.concatenate([v0[...], v1[...], v2[...]], axis=0)
        for kv in range(SW_KV_HEADS):
            kvc = slice(kv * SW_HEAD_DIM, (kv + 1) * SW_HEAD_DIM)
            qs = jnp.concatenate([q_ref[:, _head_cols(kv, g)] for g in range(SW_GROUP)], axis=0)
            sink = jnp.concatenate([jnp.broadcast_to(s_ref[kv * SW_GROUP + g], (SW_BLOCK, 1)) for g in range(SW_GROUP)], axis=0)
            o = _attn_core(qs, kb[:, kvc], vb[:, kvc], sink, mask)
            for g in range(SW_GROUP):
                o_ref[:, _head_cols(kv, g)] = o[g * SW_BLOCK:(g + 1) * SW_BLOCK, :].astype(_MXU)

    return pl.pallas_call(
        body, name="attn_fwd", grid=(nb,), in_specs=[q_spec] + k_specs + v_specs + [_full((SW_HEADS, 1, 1))],
        out_specs=q_spec, out_shape=_sds((t, qw), _MXU), compiler_params=_params(("parallel",)),
    )(qr, kr, kr, kr, proj, proj, proj, sinks)


def _attn_bwd(qr, kr, proj, sinks, do):
    t = qr.shape[0]
    nb = t // SW_BLOCK
    qw = SW_HEADS * SW_HEAD_DIM
    k_specs, v_specs = _band_specs(nb, C_VSW)
    q_spec = pl.BlockSpec((SW_BLOCK, qw), lambda n: (n, 0))
    part = pl.BlockSpec((1, 3 * SW_BLOCK, SW_KV_WIDTH), lambda n: (n, 0, 0))

    def body(q_ref, k0, k1, k2, v0, v1, v2, s_ref, do_ref, dq_ref, dk_ref, dv_ref, ds_ref):
        mask = _band_mask(pl.program_id(0), nb)
        kb = jnp.concatenate([k0[...], k1[...], k2[...]], axis=0).astype(F32)
        vb = jnp.concatenate([v0[...], v1[...], v2[...]], axis=0)

        @pl.when(pl.program_id(0) == 0)
        def _():
            ds_ref[...] = jnp.zeros_like(ds_ref)

        for kv in range(SW_KV_HEADS):
            kvc = slice(kv * SW_HEAD_DIM, (kv + 1) * SW_HEAD_DIM)
            qs = jnp.concatenate([q_ref[:, _head_cols(kv, g)] for g in range(SW_GROUP)], axis=0).astype(F32)
            dos = jnp.concatenate([do_ref[:, _head_cols(kv, g)] for g in range(SW_GROUP)], axis=0)
            sink = jnp.concatenate([jnp.broadcast_to(s_ref[kv * SW_GROUP + g], (SW_BLOCK, 1)) for g in range(SW_GROUP)], axis=0)
            _, vjp = jax.vjp(functools.partial(_attn_core, mask=mask), qs, kb[:, kvc], vb[:, kvc], sink)
            dqs, dkb, dvb, dsink = vjp(dos)
            dk_ref[0, :, kvc] = dkb
            dv_ref[0, :, kvc] = dvb
            for g in range(SW_GROUP):
                rows = slice(g * SW_BLOCK, (g + 1) * SW_BLOCK)
                dq_ref[:, _head_cols(kv, g)] = dqs[rows, :]
                ds_ref[kv * SW_GROUP + g] += jnp.sum(dsink[rows, :], axis=0, keepdims=True)

    parts = _sds((nb, 3 * SW_BLOCK, SW_KV_WIDTH))
    return pl.pallas_call(
        body, name="attn_bwd", grid=(nb,), in_specs=[q_spec] + k_specs + v_specs + [_full((SW_HEADS, 1, 1)), q_spec],
        out_specs=[q_spec, part, part, _full((SW_HEADS, 1, 1))], out_shape=[_sds((t, qw)), parts, parts, _sds((SW_HEADS, 1, 1))],
        compiler_params=_params(("arbitrary",)),
    )(qr, kr, kr, kr, proj, proj, proj, sinks, do)


def _band_sum(parts, name, dtype):
    nb = parts.shape[0]
    w = parts.shape[2]

    def spec(shift, slot):
        return pl.BlockSpec((1, SW_BLOCK, w), lambda m: (jnp.clip(m + shift, 0, nb - 1), slot, 0))

    def body(prev_ref, own_ref, next_ref, o_ref):
        m = pl.program_id(0)
        total = own_ref[0] + jnp.where(m > 0, prev_ref[0], 0.0) + jnp.where(m < nb - 1, next_ref[0], 0.0)
        o_ref[...] = total.astype(dtype)

    return pl.pallas_call(
        body, name=name, grid=(nb,), in_specs=[spec(-1, 2), spec(0, 1), spec(1, 0)],
        out_specs=pl.BlockSpec((SW_BLOCK, w), lambda m: (m, 0)), out_shape=_sds((nb * SW_BLOCK, w), dtype),
        compiler_params=_params(("parallel",)),
    )(parts, parts, parts)


def _gate_layouts(gbo):
    t = gbo.shape[0]
    beta_c = gbo[:, :N_HD].T.reshape(N_HD, t, 1)
    gc = gbo[:, N_HD:2 * N_HD].T
    return beta_c, gc.reshape(N_HD, t, 1), gc.reshape(N_HD, t // DN_CHUNK, 1, DN_CHUNK)


def _gate_layouts_t(dbeta_c, dgc_c, dgc_r):
    t = dbeta_c.shape[1]
    pad = jnp.zeros((t, LANES - 2 * N_HD), F32)
    none = jnp.zeros((t, N_HD), F32)
    d_a = jnp.concatenate([dbeta_c.reshape(N_HD, t).T, dgc_c.reshape(N_HD, t).T, pad], axis=1)
    d_b = jnp.concatenate([none, dgc_r.reshape(N_HD, t).T, pad], axis=1)
    return d_a, d_b


def _layer_fwd(x, xb, w, cos, sin):
    proj = _mm(xb, w["wm"], "nn", name="proj")
    ba = _mm(xb, w["wba"], "nn", name="proj_gates")
    qn, kn, vv = [_prep_fwd(proj, w["conv"], kind) for kind in range(3)]
    gbo = _gb_fwd(ba, w["alog"], w["dtb"])
    beta_c, gc_c, gc_r = _gate_layouts(gbo)
    u, wk, qk, qd, kd, egl, tinv = _dn_local_fwd(qn, kn, vv, beta_c, gc_c, gc_r)
    o2, vn, states = _dn_rec_fwd(u, wk, qk, qd, kd, egl)
    o_dn = _post_fwd(o2, proj, w["dnw"])
    qr, kr = _rope_apply(proj, proj, C_QSW, C_KSW, cos, sin, "rope_fwd", _MXU)
    o_sw = _attn_fwd(qr, kr, proj, w["sinks"])
    ya = _mm(o_dn, w["wa"], "nn", name="branch_a")
    yb = _mm(o_sw, w["wb"], "nn", name="branch_b")
    merged = _merge_fwd(proj, ya, yb)
    mix = _mm(merged, w["wo"], "nn", name="mix_out")
    x1, x1b = _ln_fwd(x, mix, w["ln1g"], w["ln1b"], "ln1_fwd")
    gu = _mm(x1b, w["wgu"], "nn", name="ffn_up")
    h = _swiglu_fwd(gu)
    f = _mm(h, w["wd"], "nn", name="ffn_down")
    x2, x2b = _ln_fwd(x1, f, w["ln2g"], w["ln2b"], "ln2_fwd")
    res = dict(x=x, xb=xb, proj=proj, ba=ba, qn=qn, kn=kn, vv=vv, gbo=gbo, wk=wk, qk=qk, qd=qd, kd=kd, egl=egl, tinv=tinv, vn=vn,
               states=states, o2=o2, o_dn=o_dn, qr=qr, kr=kr, o_sw=o_sw, ya=ya, yb=yb, merged=merged, mix=mix, x1=x1, x1b=x1b,
               gu=gu, h=h, f=f)
    return x2, x2b, res


def _layer_bwd(dx2, w, r, cos, sin):
    dx1, df, dln2g, dln2b = _ln_bwd(r["x1"], r["f"], w["ln2g"], w["ln2b"], dx2, "ln2_bwd")
    dh = _mm(df, w["wd"], "nt", name="d_ffn_hidden")
    dwd = _mm(r["h"], df, "tn", name="dw_ffn_down", out_dtype=_MXU)
    dgu = _swiglu_bwd(r["gu"], dh)
    dwgu = _mm(r["x1b"], dgu, "tn", name="dw_ffn_up", out_dtype=_MXU)
    dx1 = _mm(dgu, w["wgu"], "nt", name="dx_ffn", add=dx1)
    dx, dmix, dln1g, dln1b = _ln_bwd(r["x"], r["mix"], w["ln1g"], w["ln1b"], dx1, "ln1_bwd")
    dmerged = _mm(dmix, w["wo"], "nt", name="d_merged")
    dwo = _mm(r["merged"], dmix, "tn", name="dw_mix_out", out_dtype=_MXU)
    dga, dgb, dya, dyb = _merge_bwd(r["proj"], r["ya"], r["yb"], dmerged)
    dwa = _mm(r["o_dn"], dya, "tn", name="dw_branch_a", out_dtype=_MXU)
    do_dn = _mm(dya, w["wa"], "nt", name="d_branch_a")
    dwb = _mm(r["o_sw"], dyb, "tn", name="dw_branch_b", out_dtype=_MXU)
    do_sw = _mm(dyb, w["wb"], "nt", name="d_branch_b")
    do, dz, ddnw = _post_bwd(r["o2"], r["proj"], w["dnw"], do_dn)
    du, dwk, dqk, dqd, dkd, degl = _dn_rec_bwd(r["wk"], r["qk"], r["qd"], r["kd"], r["egl"], r["vn"], r["states"], do)
    beta_c, gc_c, gc_r = _gate_layouts(r["gbo"])
    dq3, dk3, dv3, dbeta_c, dgc_c, dgc_r = _dn_local_bwd(r["qn"], r["kn"], r["vv"], beta_c, gc_c, gc_r, r["tinv"],
                                                         du, dwk, dqk, dqd, dkd, degl)
    dqkv, dconv = zip(*[_prep_bwd(r["proj"], w["conv"], d2, kind) for kind, d2 in enumerate((dq3, dk3, dv3))])
    dconv = jnp.concatenate(dconv, axis=1)
    d_a, d_b = _gate_layouts_t(dbeta_c, dgc_c, dgc_r)
    dba, dalog, ddtb = _gb_bwd(r["ba"], w["alog"], w["dtb"], d_a, d_b)
    dqr, dkparts, dvparts, dsinks = _attn_bwd(r["qr"], r["kr"], r["proj"], w["sinks"], do_sw)
    dkr = _band_sum(dkparts, "attn_dk_sum", F32)
    dv = _band_sum(dvparts, "attn_dv_sum", _MXU)
    dq_sw, dk_sw = _rope_apply(dqr, dkr, 0, 0, cos, -sin, "rope_bwd", _MXU)
    dproj = jnp.concatenate([*dqkv, dz, dq_sw, dga, dgb, dk_sw, dv], axis=1)
    dwm = _mm(r["xb"], dproj, "tn", name="dw_proj", out_dtype=_MXU)
    dwba = _mm(r["xb"], dba, "tn", name="dw_proj_gates", out_dtype=_MXU)
    dx = _mm(dproj, w["wm"], "nt", name="dx_proj", add=dx)
    dx = _mm(dba, w["wba"], "nt", name="dx_proj_gates", add=dx)
    grads = dict(wm=dwm, wba=dwba, conv=dconv, alog=dalog, dtb=ddtb, dnw=ddnw, sinks=dsinks, wa=dwa, wb=dwb, wo=dwo,
                 ln1g=dln1g, ln1b=dln1b, wgu=dwgu, wd=dwd, ln2g=dln2g, ln2b=dln2b)
    return dx, grads


def _rope_tables(t):
    half = SW_HEAD_DIM // 2
    inv_freq = ROPE_THETA ** (-jnp.arange(half, dtype=F32) / half)
    ang = jnp.arange(t, dtype=F32)[:, None] * inv_freq[None, :]
    return jnp.tile(jnp.cos(ang), (1, LANES // half)), jnp.tile(jnp.sin(ang), (1, LANES // half))


def _trunk(x, target, layers):
    cos, sin = _rope_tables(x.shape[0])
    xb = x.astype(_MXU)
    saved = []
    for w in layers:
        x, xb, res = _layer_fwd(x, xb, w, cos, sin)
        saved.append(res)
    sq, dx = _loss_head(x, target)
    grads = [None] * len(layers)
    for i in reversed(range(len(layers))):
        dx, grads[i] = _layer_bwd(dx, layers[i], saved[i], cos, sin)
    return sq, dx, grads


def _peer(r):
    x, y, c = lax.axis_index("x"), lax.axis_index("y"), lax.axis_index("c")
    px = 1 - x if r & 4 else x
    py = 1 - y if r & 2 else y
    pc = 1 - c if r & 1 else c
    return (px, py, pc), 4 * px + 2 * py + pc


def _all_gather(block, name):
    def body(x_ref, o_ref, send_sems, recv_sems, local_sem):
        _, me = _peer(0)
        mine = pltpu.make_async_copy(x_ref, o_ref.at[me], local_sem)
        mine.start()
        copies = []
        for r in range(1, N_DEV):
            dev, _ = _peer(r)
            cp = pltpu.make_async_remote_copy(
                src_ref=x_ref, dst_ref=o_ref.at[me], send_sem=send_sems.at[r - 1], recv_sem=recv_sems.at[r - 1],
                device_id=dev, device_id_type=pl.DeviceIdType.MESH)
            cp.start()
            copies.append(cp)
        for r in range(1, N_DEV):
            dev, idx = _peer(r)
            pltpu.make_async_remote_copy(
                src_ref=x_ref, dst_ref=o_ref.at[idx], send_sem=send_sems.at[r - 1], recv_sem=recv_sems.at[r - 1],
                device_id=dev, device_id_type=pl.DeviceIdType.MESH).wait_recv()
        for cp in copies:
            cp.wait_send()
        mine.wait()

    return pl.pallas_call(
        body, name=name, in_specs=[pl.BlockSpec(memory_space=pl.ANY)], out_specs=pl.BlockSpec(memory_space=pl.ANY),
        out_shape=_sds((N_DEV,) + block.shape, block.dtype),
        scratch_shapes=[pltpu.SemaphoreType.DMA((N_DEV - 1,)), pltpu.SemaphoreType.DMA((N_DEV - 1,)), pltpu.SemaphoreType.DMA],
        compiler_params=pltpu.CompilerParams(has_side_effects=True),
    )(block)


def _all_to_all(parts, name):
    def body(x_ref, o_ref, send_sems, recv_sems, local_sem):
        _, me = _peer(0)
        mine = pltpu.make_async_copy(x_ref.at[me], o_ref.at[me], local_sem)
        mine.start()
        copies = []
        for r in range(1, N_DEV):
            dev, idx = _peer(r)
            cp = pltpu.make_async_remote_copy(
                src_ref=x_ref.at[idx], dst_ref=o_ref.at[me], send_sem=send_sems.at[r - 1], recv_sem=recv_sems.at[r - 1],
                device_id=dev, device_id_type=pl.DeviceIdType.MESH)
            cp.start()
            copies.append(cp)
        for r in range(1, N_DEV):
            dev, idx = _peer(r)
            pltpu.make_async_remote_copy(
                src_ref=x_ref.at[me], dst_ref=o_ref.at[idx], send_sem=send_sems.at[r - 1], recv_sem=recv_sems.at[r - 1],
                device_id=dev, device_id_type=pl.DeviceIdType.MESH).wait_recv()
        for cp in copies:
            cp.wait_send()
        mine.wait()

    return pl.pallas_call(
        body, name=name, in_specs=[pl.BlockSpec(memory_space=pl.ANY)], out_specs=pl.BlockSpec(memory_space=pl.ANY),
        out_shape=_sds(parts.shape, parts.dtype),
        scratch_shapes=[pltpu.SemaphoreType.DMA((N_DEV - 1,)), pltpu.SemaphoreType.DMA((N_DEV - 1,)), pltpu.SemaphoreType.DMA],
        compiler_params=pltpu.CompilerParams(has_side_effects=True),
    )(parts)


def _sum_adamw(parts, w, m, v, name):
    rows, cols = w.shape
    tr = rows if rows <= 512 else _row_tile(rows)
    blk = pl.BlockSpec((tr, cols), lambda i: (i, 0))

    def body(p_ref, w_ref, m_ref, v_ref, g_ref, d_ref, nm_ref, nv_ref):
        g = p_ref[0].astype(F32)
        for i in range(1, N_DEV):
            g = g + p_ref[i].astype(F32)
        nm = ADAM_B1 * m_ref[...] + (1.0 - ADAM_B1) * g
        nv = ADAM_B2 * v_ref[...] + (1.0 - ADAM_B2) * jnp.square(g)
        m_hat = nm / (1.0 - ADAM_B1 ** ADAM_STEP)
        v_hat = nv / (1.0 - ADAM_B2 ** ADAM_STEP)
        g_ref[...] = g
        d_ref[...] = -ADAM_LR * (m_hat / (jnp.sqrt(v_hat) + ADAM_EPS) + ADAM_WD * w_ref[...])
        nm_ref[...] = nm
        nv_ref[...] = nv

    return pl.pallas_call(
        body, name=name, grid=(rows // tr,), in_specs=[pl.BlockSpec((N_DEV, tr, cols), lambda i: (0, i, 0)), blk, blk, blk],
        out_specs=[blk] * 4, out_shape=[_sds((rows, cols))] * 4, compiler_params=_params(("parallel",)),
    )(parts, w, m, v)


def _row_tile(rows):
    for t in (256, 128, 64, 32, 16, 8):
        if rows % t == 0:
            return t
    return rows


def _gathered_cols(g):
    n, l, rows, c = g.shape
    return jnp.transpose(g, (1, 2, 0, 3)).reshape(l, rows, n * c)


def _gathered_rows(g):
    n, l, rows, c = g.shape
    return jnp.transpose(g, (1, 0, 2, 3)).reshape(l, n * rows, c)


def _col_parts(full):
    l, rows, c = full.shape
    return jnp.transpose(full.reshape(l, rows, N_DEV, c // N_DEV), (2, 0, 1, 3))


def _row_parts(full):
    l, rows, c = full.shape
    return jnp.transpose(full.reshape(l, N_DEV, rows // N_DEV, c), (1, 0, 2, 3))


def _w_in_split(w_in):
    s = lambda a, n: w_in[..., a:a + n]
    main = jnp.concatenate([s(R_QKV, 3072), s(R_Z, 1024), s(R_QSW, 1024), s(R_G, 2048), s(R_KSW, 256), s(R_VSW, 256)], axis=-1)
    gates = jnp.pad(s(R_BA, 2 * N_HD), ((0, 0), (0, 0), (0, LANES - 2 * N_HD)))
    return main, gates


def _w_in_join(dmain, dgates):
    s = lambda a, n: dmain[..., a:a + n]
    return jnp.concatenate([s(C_QKV, 3072), s(C_Z, 1024), dgates[..., :2 * N_HD], s(C_QSW, 1024), s(C_KSW, 256), s(C_VSW, 256),
                            s(C_GA, 2048)], axis=-1)


def _lane_row(a, offset):
    l, n = a.shape
    return jnp.pad(a, ((0, 0), (offset, LANES - offset - n)))[:, None, :]


def kernel(x, w_in, conv_w, a_log, dt_bias, dn_norm_w, sinks, w_branch_a, w_branch_b, w_out, ln1_g, ln1_b, w_gate_up, w_down, ln2_g, ln2_b, loss_target, m_w_in, m_conv_w, m_a_log, m_dt_bias, m_dn_norm_w, m_sinks, m_w_branch_a, m_w_branch_b, m_w_out, m_ln1_g, m_ln1_b, m_w_gate_up, m_w_down, m_ln2_g, m_ln2_b, v_w_in, v_conv_w, v_a_log, v_dt_bias, v_dn_norm_w, v_sinks, v_w_branch_a, v_w_branch_b, v_w_out, v_ln1_g, v_ln1_b, v_w_gate_up, v_w_down, v_ln2_g, v_ln2_b):
    l = DEPTH
    bf = lambda a: a.astype(_MXU)
    w_in_full = _gathered_cols(_all_gather(bf(w_in), "gather_w_in"))
    wgu_full = _gathered_cols(_all_gather(bf(w_gate_up), "gather_w_gate_up"))
    wa_full = _gathered_rows(_all_gather(bf(w_branch_a), "gather_w_branch_a"))
    wb_full = _gathered_rows(_all_gather(bf(w_branch_b), "gather_w_branch_b"))
    wo_full = _gathered_rows(_all_gather(bf(w_out), "gather_w_out"))
    wd_full = _gathered_rows(_all_gather(bf(w_down), "gather_w_down"))
    conv_full = _gathered_cols(_all_gather(conv_w, "gather_conv_w"))
    wm, wba = _w_in_split(w_in_full)
    row = lambda a: a[:, None, :]
    stacked = dict(
        wm=wm, wba=wba, conv=jnp.pad(conv_full, ((0, 0), (0, 8 - DN_CONV), (0, 0))),
        alog=_lane_row(a_log.reshape(l, N_HD), N_HD), dtb=_lane_row(dt_bias.reshape(l, N_HD), N_HD), dnw=row(dn_norm_w),
        sinks=sinks.reshape(l, SW_HEADS, 1, 1), wa=wa_full, wb=wb_full, wo=wo_full, ln1g=row(ln1_g), ln1b=row(ln1_b),
        wgu=wgu_full, wd=wd_full, ln2g=row(ln2_g), ln2b=row(ln2_b))

    sq, dx, grads = _trunk(x[0], loss_target[0], [{k: a[i] for k, a in stacked.items()} for i in range(l)])
    g = {k: jnp.stack([gi[k] for gi in grads]) for k in grads[0]}
    loss = lax.psum(0.5 * sq[0, 0] / D_MODEL, ("x", "y", "c"))

    def big(parts, w, m, v, name):
        got = _all_to_all(parts, "exchange_" + name)
        rows = w.shape[0] * w.shape[1]
        flat = lambda a: a.reshape(rows, a.shape[-1])
        outs = _sum_adamw(got.reshape(N_DEV, rows, w.shape[-1]), flat(w), flat(m), flat(v), "adamw_" + name)
        return [o.reshape(w.shape) for o in outs]

    dconv = g["conv"][:, :DN_CONV, :]
    results = {
        "w_in": big(_col_parts(_w_in_join(g["wm"], g["wba"])), w_in, m_w_in, v_w_in, "w_in"),
        "conv_w": big(_col_parts(dconv), conv_w, m_conv_w, v_conv_w, "conv_w"),
        "w_branch_a": big(_row_parts(g["wa"]), w_branch_a, m_w_branch_a, v_w_branch_a, "w_branch_a"),
        "w_branch_b": big(_row_parts(g["wb"]), w_branch_b, m_w_branch_b, v_w_branch_b, "w_branch_b"),
        "w_out": big(_row_parts(g["wo"]), w_out, m_w_out, v_w_out, "w_out"),
        "w_gate_up": big(_col_parts(g["wgu"]), w_gate_up, m_w_gate_up, v_w_gate_up, "w_gate_up"),
        "w_down": big(_row_parts(g["wd"]), w_down, m_w_down, v_w_down, "w_down"),
    }

    small_w = {"a_log": a_log.reshape(l, N_HD), "dt_bias": dt_bias.reshape(l, N_HD), "dn_norm_w": dn_norm_w, "sinks": sinks,
               "ln1_g": ln1_g, "ln1_b": ln1_b, "ln2_g": ln2_g, "ln2_b": ln2_b}
    small_m = {"a_log": m_a_log, "dt_bias": m_dt_bias, "dn_norm_w": m_dn_norm_w, "sinks": m_sinks, "ln1_g": m_ln1_g,
               "ln1_b": m_ln1_b, "ln2_g": m_ln2_g, "ln2_b": m_ln2_b}
    small_v = {"a_log": v_a_log, "dt_bias": v_dt_bias, "dn_norm_w": v_dn_norm_w, "sinks": v_sinks, "ln1_g": v_ln1_g,
               "ln1_b": v_ln1_b, "ln2_g": v_ln2_g, "ln2_b": v_ln2_b}
    small_g = {"a_log": g["alog"][:, 0, N_HD:2 * N_HD], "dt_bias": g["dtb"][:, 0, N_HD:2 * N_HD], "dn_norm_w": g["dnw"][:, 0, :],
               "sinks": g["sinks"].reshape(l, SW_HEADS), "ln1_g": g["ln1g"][:, 0, :], "ln1_b": g["ln1b"][:, 0, :],
               "ln2_g": g["ln2g"][:, 0, :], "ln2_b": g["ln2b"][:, 0, :]}
    names = list(small_w)
    cat = lambda d: jnp.concatenate([d[n].reshape(l, -1) for n in names], axis=1)
    widths = [small_w[n].shape[1] for n in names]
    total = sum(widths)
    padded = -(-total // LANES) * LANES
    pad = lambda a: jnp.pad(a, ((0, 8 - l), (0, padded - total)))
    got = _all_gather(pad(cat(small_g)), "gather_small_grads")
    outs = _sum_adamw(got, pad(cat(small_w)), pad(cat({n: small_m[n].reshape(l, -1) for n in names})),
                      pad(cat({n: small_v[n].reshape(l, -1) for n in names})), "adamw_small")
    off = 0
    for n, wd_ in zip(names, widths):
        shape = {"a_log": a_log.shape, "dt_bias": dt_bias.shape}.get(n, small_w[n].shape)
        results[n] = [o[:l, off:off + wd_].reshape(shape) for o in outs]
        off += wd_

    order = ["w_in", "conv_w", "a_log", "dt_bias", "dn_norm_w", "sinks", "w_branch_a", "w_branch_b", "w_out", "ln1_g", "ln1_b",
             "w_gate_up", "w_down", "ln2_g", "ln2_b"]
    return (loss, dx[None], *[results[n][0] for n in order], *[results[n][1] for n in order],
            *[results[n][2] for n in order], *[results[n][3] for n in order])
```

```python
import functools

import jax
import jax.numpy as jnp
from jax import lax
from jax.experimental import pallas as pl
from jax.experimental.pallas import tpu as pltpu

F32 = jnp.float32
_MXU = jnp.bfloat16
_HI = lax.Precision.HIGHEST
_MID = lax.Precision.HIGH

N_DEV = 8
D_MODEL = 1024
DEPTH = 4
DN_HEADS = 8
DN_HEAD_DIM = 128
DN_WIDTH = DN_HEADS * DN_HEAD_DIM
DN_CONV = 5
DN_CHUNK = 64
SW_HEADS = 16
SW_KV_HEADS = 4
SW_HEAD_DIM = 64
SW_GROUP = SW_HEADS // SW_KV_HEADS
SW_BLOCK = 128
SW_KV_WIDTH = SW_KV_HEADS * SW_HEAD_DIM
ROPE_THETA = 10000.0
FFN_HIDDEN = 2816
DN_ALPHA = (2.0 * DEPTH) ** 0.25
LN_EPS = 1e-5
RMS_EPS = 1e-6
ADAM_LR = 0.001
ADAM_B1 = 0.9
ADAM_B2 = 0.999
ADAM_EPS = 1e-08
ADAM_WD = 0.01
ADAM_STEP = 10

LANES = 128
N_HD = 2 * DN_HEADS
DN_GROUP = 4 * DN_CHUNK
LOCAL_ROWS = 512
REC_ROWS = 512
ROW_TILE = 256
VMEM_LIMIT = 48 << 20

C_QKV, C_Z, C_QSW, C_GA, C_GB, C_KSW, C_VSW = 0, 3072, 4096, 5120, 6144, 7168, 7424
MAIN_COLS = 7680
R_QKV, R_Z, R_BA, R_QSW, R_KSW, R_VSW, R_G = 0, 3072, 4096, 4128, 5152, 5408, 5664
IN_COLS = 7712


_NN = ((1,), (0,))
_NT = ((1,), (1,))
_TN = ((0,), (0,))


def _dg(a, b, dims, precision):
    if precision is not None:
        return lax.dot_general(a, b, (dims, ((), ())), precision=precision, preferred_element_type=F32)
    return lax.dot_general(a.astype(_MXU), b.astype(_MXU), (dims, ((), ())), preferred_element_type=F32)


def _make_dots(hi):
    @jax.custom_vjp
    def nn(a, b):
        return _dg(a, b, _NN, hi)

    @jax.custom_vjp
    def nt(a, b):
        return _dg(a, b, _NT, hi)

    @jax.custom_vjp
    def tn(a, b):
        return _dg(a, b, _TN, hi)

    nn.defvjp(lambda a, b: (nn(a, b), (a, b)), lambda r, g: (nt(g, r[1]), tn(r[0], g)))
    nt.defvjp(lambda a, b: (nt(a, b), (a, b)), lambda r, g: (nn(g, r[1]), tn(g, r[0])))
    tn.defvjp(lambda a, b: (tn(a, b), (a, b)), lambda r, g: (nt(r[1], g), nn(r[0], g)))
    return nn, nt, tn


_bnn, _bnt, _btn = _make_dots(None)
_hnn, _hnt, _htn = _make_dots(_HI)
_mnn, _mnt, _mtn = _make_dots(_MID)


def _inv_unit(a, order):
    n = a.shape[0]
    eye = (lax.broadcasted_iota(jnp.int32, (n, n), 0) == lax.broadcasted_iota(jnp.int32, (n, n), 1)).astype(F32)
    inv = eye - a
    p = a
    span = 2
    while span < order:
        p = _mnn(p, p)
        inv = inv + _mnn(inv, p)
        span *= 2
    return inv


def _inv_unit_t(t, g):
    return -_mnt(_mtn(t, g), t)


def _silu(x):
    return x * jax.nn.sigmoid(x)


def _softplus(x):
    return jnp.maximum(x, 0.0) + jnp.log1p(jnp.exp(-jnp.abs(x)))


def _params(sem=None):
    kw = {"vmem_limit_bytes": VMEM_LIMIT}
    if sem is not None:
        kw["dimension_semantics"] = sem
    return pltpu.CompilerParams(**kw)


def _tile(dim, pref):
    if dim <= pref:
        return dim
    t = (pref // LANES) * LANES
    while t > LANES and dim % t:
        t -= LANES
    assert dim % t == 0, (dim, pref)
    return t


def _full(shape):
    zeros = (0,) * len(shape)
    return pl.BlockSpec(shape, lambda *_: zeros)


def _sds(shape, dtype=F32):
    return jax.ShapeDtypeStruct(shape, dtype)


def _mm(a, b, mode, *, name, add=None, tm=1536, tn=1536, tk=1536, out_dtype=F32):
    if mode == "nn":
        (m, k), (k2, n) = a.shape, b.shape
    elif mode == "nt":
        (m, k), (n, k2) = a.shape, b.shape
    else:
        (k, m), (k2, n) = a.shape, b.shape
    assert k == k2, (a.shape, b.shape, mode)
    tm, tn, tk = _tile(m, tm), _tile(n, tn), _tile(k, tk)
    nk = k // tk
    dims = {"nn": _NN, "nt": _NT, "tn": _TN}[mode]

    def body(*refs):
        if add is None:
            a_ref, b_ref, o_ref, acc = refs
        else:
            a_ref, b_ref, add_ref, o_ref, acc = refs
        kk = pl.program_id(2)

        @pl.when(kk == 0)
        def _():
            acc[...] = jnp.zeros_like(acc)

        acc[...] += _dg(a_ref[...], b_ref[...], dims, None)

        @pl.when(kk == nk - 1)
        def _():
            o_ref[...] = (acc[...] if add is None else acc[...] + add_ref[...]).astype(out_dtype)

    a_spec = pl.BlockSpec((tk, tm), lambda i, j, kk: (kk, i)) if mode == "tn" else pl.BlockSpec((tm, tk), lambda i, j, kk: (i, kk))
    b_spec = pl.BlockSpec((tn, tk), lambda i, j, kk: (j, kk)) if mode == "nt" else pl.BlockSpec((tk, tn), lambda i, j, kk: (kk, j))
    o_spec = pl.BlockSpec((tm, tn), lambda i, j, kk: (i, j))
    ins, specs = [a, b], [a_spec, b_spec]
    if add is not None:
        ins.append(add)
        specs.append(o_spec)
    return pl.pallas_call(
        body, name=name, grid=(m // tm, n // tn, nk), in_specs=specs, out_specs=o_spec,
        out_shape=_sds((m, n), out_dtype), scratch_shapes=[pltpu.VMEM((tm, tn), F32)],
        compiler_params=_params(("parallel", "parallel", "arbitrary")),
    )(*ins)


def _cols(width, start):
    assert start % width == 0
    return pl.BlockSpec((ROW_TILE, width), lambda i: (i, start // width))


def _rows(width):
    return pl.BlockSpec((ROW_TILE, width), lambda i: (i, 0))


def _accumulate(ref, value, step):
    @pl.when(step == 0)
    def _():
        ref[...] = value

    @pl.when(step != 0)
    def _():
        ref[...] += value


def _ln_fn(x, r, g, b):
    u = DN_ALPHA * x + r
    mu = jnp.mean(u, axis=-1, keepdims=True)
    var = jnp.mean(jnp.square(u - mu), axis=-1, keepdims=True)
    return (u - mu) * lax.rsqrt(var + LN_EPS) * g + b


def _ln_fwd(x, r, g, b, name):
    t, d = x.shape

    def body(x_ref, r_ref, g_ref, b_ref, o_ref, ob_ref):
        y = _ln_fn(x_ref[...], r_ref[...], g_ref[...], b_ref[...])
        o_ref[...] = y
        ob_ref[...] = y.astype(_MXU)

    return pl.pallas_call(
        body, name=name, grid=(t // ROW_TILE,), in_specs=[_rows(d), _rows(d), _full((1, d)), _full((1, d))],
        out_specs=[_rows(d), _rows(d)], out_shape=[_sds((t, d)), _sds((t, d), _MXU)], compiler_params=_params(("parallel",)),
    )(x, r, g, b)


def _ln_bwd(x, r, g, b, dy, name):
    t, d = x.shape

    def body(x_ref, r_ref, g_ref, b_ref, dy_ref, dx_ref, dr_ref, dg_ref, db_ref):
        _, vjp = jax.vjp(_ln_fn, x_ref[...], r_ref[...], g_ref[...], b_ref[...])
        dx, dr, dg, db = vjp(dy_ref[...])
        dx_ref[...] = dx
        dr_ref[...] = dr.astype(_MXU)
        _accumulate(dg_ref, dg, pl.program_id(0))
        _accumulate(db_ref, db, pl.program_id(0))

    return pl.pallas_call(
        body, name=name, grid=(t // ROW_TILE,),
        in_specs=[_rows(d), _rows(d), _full((1, d)), _full((1, d)), _rows(d)],
        out_specs=[_rows(d), _rows(d), _full((1, d)), _full((1, d))],
        out_shape=[_sds((t, d)), _sds((t, d), _MXU), _sds((1, d)), _sds((1, d))],
        compiler_params=_params(("arbitrary",)),
    )(x, r, g, b, dy)


def _merge_fn(ga, gb, ya, yb):
    return jax.nn.sigmoid(ga) * ya + jax.nn.sigmoid(gb) * yb


def _merge_fwd(proj, ya, yb):
    t, d = ya.shape

    def body(ga_ref, gb_ref, ya_ref, yb_ref, o_ref):
        o_ref[...] = _merge_fn(ga_ref[...], gb_ref[...], ya_ref[...], yb_ref[...]).astype(_MXU)

    return pl.pallas_call(
        body, name="merge_fwd", grid=(t // ROW_TILE,), in_specs=[_cols(d, C_GA), _cols(d, C_GB), _rows(d), _rows(d)],
        out_specs=_rows(d), out_shape=_sds((t, d), _MXU), compiler_params=_params(("parallel",)),
    )(proj, proj, ya, yb)


def _merge_bwd(proj, ya, yb, dm):
    t, d = ya.shape

    def body(ga_ref, gb_ref, ya_ref, yb_ref, dm_ref, dga_ref, dgb_ref, dya_ref, dyb_ref):
        _, vjp = jax.vjp(_merge_fn, ga_ref[...], gb_ref[...], ya_ref[...], yb_ref[...])
        dga_ref[...], dgb_ref[...], dya_ref[...], dyb_ref[...] = [g.astype(_MXU) for g in vjp(dm_ref[...])]

    return pl.pallas_call(
        body, name="merge_bwd", grid=(t // ROW_TILE,),
        in_specs=[_cols(d, C_GA), _cols(d, C_GB), _rows(d), _rows(d), _rows(d)],
        out_specs=[_rows(d)] * 4, out_shape=[_sds((t, d), _MXU)] * 4, compiler_params=_params(("parallel",)),
    )(proj, proj, ya, yb, dm)


def _swiglu_fn(gate, up):
    return _silu(gate) * up


def _swiglu_fwd(gu):
    t = gu.shape[0]
    f = FFN_HIDDEN
    rows = 128

    def body(gu_ref, o_ref):
        o_ref[...] = _swiglu_fn(gu_ref[:, :f], gu_ref[:, f:]).astype(_MXU)

    return pl.pallas_call(
        body, name="swiglu_fwd", grid=(t // rows,), in_specs=[pl.BlockSpec((rows, 2 * f), lambda i: (i, 0))],
        out_specs=pl.BlockSpec((rows, f), lambda i: (i, 0)), out_shape=_sds((t, f), _MXU), compiler_params=_params(("parallel",)),
    )(gu)


def _swiglu_bwd(gu, dh):
    t = gu.shape[0]
    f = FFN_HIDDEN
    rows = 128

    def body(gu_ref, dh_ref, o_ref):
        _, vjp = jax.vjp(_swiglu_fn, gu_ref[:, :f], gu_ref[:, f:])
        o_ref[:, :f], o_ref[:, f:] = [g.astype(_MXU) for g in vjp(dh_ref[...])]

    return pl.pallas_call(
        body, name="swiglu_bwd", grid=(t // rows,),
        in_specs=[pl.BlockSpec((rows, 2 * f), lambda i: (i, 0)), pl.BlockSpec((rows, f), lambda i: (i, 0))],
        out_specs=pl.BlockSpec((rows, 2 * f), lambda i: (i, 0)), out_shape=_sds((t, 2 * f), _MXU),
        compiler_params=_params(("parallel",)),
    )(gu, dh)


def _loss_head(y, target):
    t, d = y.shape

    def body(y_ref, t_ref, s_ref, dy_ref):
        err = y_ref[...] - t_ref[...]
        dy_ref[...] = err / d
        _accumulate(s_ref, jnp.broadcast_to(jnp.sum(jnp.square(err)), (1, LANES)), pl.program_id(0))

    return pl.pallas_call(
        body, name="loss_head", grid=(t // ROW_TILE,), in_specs=[_rows(d), _rows(d)],
        out_specs=[_full((1, LANES)), _rows(d)], out_shape=[_sds((1, LANES)), _sds((t, d))],
        compiler_params=_params(("arbitrary",)),
    )(y, target)


def _shift_rows(x, s):
    if s == 0:
        return x
    t = x.shape[0]
    rolled = pltpu.roll(x, (-s) % t, 0)
    row = lax.broadcasted_iota(jnp.int32, x.shape, 0)
    return jnp.where((row + s >= 0) & (row + s < t), rolled, 0.0)


def _conv(x, w):
    half = DN_CONV // 2
    acc = None
    for k in range(DN_CONV):
        term = _shift_rows(x, k - half) * w[k:k + 1, :]
        acc = term if acc is None else acc + term
    return acc


def _act_norm(c, do_norm, scale):
    a = _silu(c)
    if not do_norm:
        return a
    return a * lax.rsqrt(jnp.sum(a * a, axis=-1, keepdims=True) + RMS_EPS) * scale


PREP_ROWS = 512
HALO = 8
_KINDS = ((True, DN_HEAD_DIM ** -0.5), (True, 1.0), (False, 1.0))


def _halo_rows(read, i, pr, t):
    lo, hi = i * pr - HALO, (i + 1) * pr + HALO
    parts = []
    if lo < 0:
        parts.append(jnp.zeros((HALO, LANES), F32))
    parts.append(read(max(lo, 0), min(hi, t)))
    if hi > t:
        parts.append(jnp.zeros((HALO, LANES), F32))
    return jnp.concatenate(parts, axis=0) if len(parts) > 1 else parts[0]


def _prep_fwd(proj, conv_w, kind):
    t = proj.shape[0]
    pr = min(PREP_ROWS, t)
    do_norm, scale = _KINDS[kind]
    blk = pl.BlockSpec((t, LANES), lambda j: (0, kind * DN_HEADS + j))

    def body(x_ref, w_ref, o_ref):
        w = w_ref[...]
        for i in range(t // pr):
            xx = _halo_rows(lambda lo, hi: x_ref[lo:hi, :], i, pr, t)
            c = _conv(xx, w)[HALO:HALO + pr, :]
            o_ref[i * pr:(i + 1) * pr, :] = _act_norm(c, do_norm, scale)

    return pl.pallas_call(
        body, name=f"prep_fwd_{kind}", grid=(DN_HEADS,),
        in_specs=[blk, pl.BlockSpec((8, LANES), lambda j: (0, kind * DN_HEADS + j))],
        out_specs=pl.BlockSpec((t, LANES), lambda j: (0, j)), out_shape=_sds((t, DN_WIDTH)),
        compiler_params=_params(("parallel",)),
    )(proj, conv_w)


def _prep_bwd(proj, conv_w, d2, kind):
    t = proj.shape[0]
    pr = min(PREP_ROWS, t)
    do_norm, scale = _KINDS[kind]
    half = DN_CONV // 2
    blk = pl.BlockSpec((t, LANES), lambda j: (0, kind * DN_HEADS + j))
    oblk = pl.BlockSpec((t, LANES), lambda j: (0, j))

    def body(x_ref, w_ref, d_ref, dx_ref, dw_ref):
        w = w_ref[...]
        own = slice(HALO, HALO + pr)
        dw = jnp.zeros((8, LANES), F32)
        for i in range(t // pr):
            xx = _halo_rows(lambda lo, hi: x_ref[lo:hi, :], i, pr, t)
            dn = _halo_rows(lambda lo, hi: d_ref[0, lo:hi, :] + d_ref[1, lo:hi, :], i, pr, t)
            _, vjp = jax.vjp(lambda c: _act_norm(c, do_norm, scale), _conv(xx, w))
            (dc,) = vjp(dn)
            dx = None
            rows = []
            for k in range(DN_CONV):
                term = _shift_rows(dc, half - k) * w[k:k + 1, :]
                dx = term if dx is None else dx + term
                rows.append(jnp.sum(dc[own, :] * _shift_rows(xx, k - half)[own, :], axis=0, keepdims=True))
            dx_ref[i * pr:(i + 1) * pr, :] = dx[own, :].astype(_MXU)
            dw = dw + jnp.concatenate(rows + [jnp.zeros((8 - DN_CONV, LANES), F32)], axis=0)
        dw_ref[...] = dw

    return pl.pallas_call(
        body, name=f"prep_bwd_{kind}", grid=(DN_HEADS,),
        in_specs=[blk, pl.BlockSpec((8, LANES), lambda j: (0, kind * DN_HEADS + j)), pl.BlockSpec((2, t, LANES), lambda j: (0, 0, j))],
        out_specs=[oblk, pl.BlockSpec((8, LANES), lambda j: (0, j))], out_shape=[_sds((t, DN_WIDTH), _MXU), _sds((8, DN_WIDTH))],
        compiler_params=_params(("parallel",)),
    )(proj, conv_w, d2)


def _gb_fn(ba, alog_row, dtb_row):
    c = DN_CHUNK
    lane = lax.broadcasted_iota(jnp.int32, (c, LANES), 1)
    ii = lax.broadcasted_iota(jnp.int32, (c, c), 0)
    jj = lax.broadcasted_iota(jnp.int32, (c, c), 1)
    beta = jax.nn.sigmoid(ba)
    g = -jnp.exp(alog_row) * _softplus(ba + dtb_row)
    g = jnp.where((lane >= N_HD) & (lane < 2 * N_HD), g, 0.0)
    gc_fwd = _hnn((ii >= jj).astype(F32), g)
    gc_rev = _hnn((ii <= jj).astype(F32), g)
    gc = jnp.where(lane < N_HD + DN_HEADS, gc_fwd, gc_rev)
    return jnp.where(lane < N_HD, beta, jnp.where(lane < 2 * N_HD, gc, 0.0))


def _gb_fwd(ba, alog_row, dtb_row):
    t = ba.shape[0]
    n = ROW_TILE // DN_CHUNK

    def body(ba_ref, a_ref, d_ref, o_ref):
        for c in range(n):
            rows = slice(c * DN_CHUNK, (c + 1) * DN_CHUNK)
            o_ref[rows, :] = _gb_fn(ba_ref[rows, :], a_ref[...], d_ref[...])

    return pl.pallas_call(
        body, name="gates_fwd", grid=(t // ROW_TILE,), in_specs=[_rows(LANES), _full((1, LANES)), _full((1, LANES))],
        out_specs=_rows(LANES), out_shape=_sds((t, LANES)), compiler_params=_params(("parallel",)),
    )(ba, alog_row, dtb_row)


def _gb_bwd(ba, alog_row, dtb_row, d_a, d_b):
    t = ba.shape[0]
    n = ROW_TILE // DN_CHUNK

    def body(ba_ref, a_ref, d_ref, da_ref, db_ref, dba_ref, dal_ref, ddt_ref):
        dal = jnp.zeros((1, LANES), F32)
        ddt = jnp.zeros((1, LANES), F32)
        for c in range(n):
            rows = slice(c * DN_CHUNK, (c + 1) * DN_CHUNK)
            _, vjp = jax.vjp(_gb_fn, ba_ref[rows, :], a_ref[...], d_ref[...])
            dba, da, dd = vjp(da_ref[rows, :] + db_ref[rows, :])
            dba_ref[rows, :] = dba.astype(_MXU)
            dal = dal + da
            ddt = ddt + dd
        _accumulate(dal_ref, dal, pl.program_id(0))
        _accumulate(ddt_ref, ddt, pl.program_id(0))

    return pl.pallas_call(
        body, name="gates_bwd", grid=(t // ROW_TILE,),
        in_specs=[_rows(LANES), _full((1, LANES)), _full((1, LANES)), _rows(LANES), _rows(LANES)],
        out_specs=[_rows(LANES), _full((1, LANES)), _full((1, LANES))],
        out_shape=[_sds((t, LANES), _MXU), _sds((1, LANES)), _sds((1, LANES))], compiler_params=_params(("arbitrary",)),
    )(ba, alog_row, dtb_row, d_a, d_b)


def _dn_decay(gcc, gcr, sgn):
    c = DN_CHUNK
    ii = lax.broadcasted_iota(jnp.int32, (c, c), 0)
    jj = lax.broadcasted_iota(jnp.int32, (c, c), 1)
    d = (ii - jj) * sgn
    lower = d >= 0
    return jnp.where(lower, jnp.exp(jnp.where(lower, gcc - gcr, 0.0)), 0.0), d > 0


def _dn_a(k, beta, gcc, gcr, sgn):
    decay, strict = _dn_decay(gcc, gcr, sgn)
    return jnp.where(strict, beta * _bnt(k, k) * decay, 0.0)


def _dn_group(q, k, v, beta, gcc, gcr, sgn):
    n = DN_GROUP
    ii = lax.broadcasted_iota(jnp.int32, (n, n), 0)
    jj = lax.broadcasted_iota(jnp.int32, (n, n), 1)
    same = (ii & -DN_CHUNK) == (jj & -DN_CHUNK)
    d = (ii - jj) * sgn
    lower = same & (d >= 0)
    decay = jnp.where(lower, jnp.exp(jnp.where(lower, gcc - gcr, 0.0)), 0.0)
    a = jnp.where(same & (d > 0), beta * _bnt(k, k) * decay, 0.0)
    t_inv = _inv_unit(a, DN_CHUNK)
    u = _bnn(t_inv, v * beta)
    w = _bnn(t_inv, k * (beta * jnp.exp(gcc)))
    return u, w, _bnt(q, k) * decay, t_inv


def _dn_local(t_inv, q, k, v, beta, gcc, gcr, sgn):
    c = DN_CHUNK
    decay, _ = _dn_decay(gcc, gcr, sgn)
    eg = jnp.exp(gcc)
    u = _bnn(t_inv, v * beta)
    w = _bnn(t_inv, k * (beta * eg))
    qk = _bnt(q, k) * decay
    qd = q * eg
    last = jnp.where(sgn > 0, c - 1, 0)
    onehot = (lax.broadcasted_iota(jnp.int32, (c, 1), 0) == last).astype(F32)
    gl = jnp.sum(gcc * onehot, axis=0, keepdims=True)
    kd = k * jnp.exp(gl - gcc)
    egl = jnp.broadcast_to(jnp.exp(gl), (1, LANES))
    return u, w, qk, qd, kd, egl


def _hd_sign(hd):
    return jnp.where(hd < DN_HEADS, 1, -1).astype(jnp.int32)


def _head_of(hd):
    return jnp.where(hd < DN_HEADS, hd, hd - DN_HEADS)


def _dir_of(hd):
    return jnp.where(hd < DN_HEADS, 0, 1)


def _dn_specs(t):
    nl = LOCAL_ROWS // DN_CHUNK
    wide = pl.BlockSpec((1, LOCAL_ROWS, LANES), lambda hd, i: (hd, i, 0))
    half = pl.BlockSpec((1, LOCAL_ROWS, DN_CHUNK), lambda hd, i: (hd, i, 0))
    col = pl.BlockSpec((1, LOCAL_ROWS, 1), lambda hd, i: (hd, i, 0))
    row = pl.BlockSpec((1, nl, 1, DN_CHUNK), lambda hd, i: (hd, i, 0, 0))
    egl = pl.BlockSpec((1, nl, 1, LANES), lambda hd, i: (hd, i, 0, 0))
    return wide, half, col, row, egl


def _qkv_specs():
    return [pl.BlockSpec((LOCAL_ROWS, LANES), lambda hd, i: (i, _head_of(hd)))] * 3


def _dn_local_fwd(q, k, v, beta_c, gc_c, gc_r):
    t = q.shape[0]
    nc = t // DN_CHUNK
    nl = LOCAL_ROWS // DN_CHUNK
    wide, half, col, row, egl = _dn_specs(t)

    ng = LOCAL_ROWS // DN_GROUP
    per = DN_GROUP // DN_CHUNK
    grow = pl.BlockSpec((1, ng, 1, DN_GROUP), lambda hd, i: (hd, i, 0, 0))

    def body(q_ref, k_ref, v_ref, b_ref, gc_ref, gg_ref, u_ref, w_ref, qk_ref, qd_ref, kd_ref, egl_ref, t_ref):
        sgn = _hd_sign(pl.program_id(0))
        last = jnp.where(sgn > 0, DN_CHUNK - 1, 0)
        onehot = (lax.broadcasted_iota(jnp.int32, (DN_CHUNK, 1), 0) == last).astype(F32)
        for gi in range(ng):
            grows = slice(gi * DN_GROUP, (gi + 1) * DN_GROUP)
            q, k, gcc = q_ref[grows, :], k_ref[grows, :], gc_ref[0, grows, :]
            u, w, qk, t_inv = _dn_group(q, k, v_ref[grows, :], b_ref[0, grows, :], gcc, gg_ref[0, gi], sgn)
            u_ref[0, grows, :] = u
            w_ref[0, grows, :] = w
            qd_ref[0, grows, :] = q * jnp.exp(gcc)
            for c in range(per):
                blk = slice(c * DN_CHUNK, (c + 1) * DN_CHUNK)
                rows = slice(gi * DN_GROUP + c * DN_CHUNK, gi * DN_GROUP + (c + 1) * DN_CHUNK)
                qk_ref[0, rows, :] = qk[blk, blk]
                t_ref[0, rows, :] = t_inv[blk, blk]
                gl = jnp.sum(gcc[blk, :] * onehot, axis=0, keepdims=True)
                kd_ref[0, rows, :] = k[blk, :] * jnp.exp(gl - gcc[blk, :])
                egl_ref[0, gi * per + c] = jnp.broadcast_to(jnp.exp(gl), (1, LANES))

    big = _sds((N_HD, t, LANES))
    small = _sds((N_HD, t, DN_CHUNK))
    return pl.pallas_call(
        body, name="dn_local_fwd", grid=(N_HD, t // LOCAL_ROWS), in_specs=_qkv_specs() + [col, col, grow],
        out_specs=[wide, wide, half, wide, wide, egl, half],
        out_shape=[big, big, small, big, big, _sds((N_HD, nc, 1, LANES)), small],
        compiler_params=_params(("parallel", "parallel")),
    )(q, k, v, beta_c, gc_c, gc_r.reshape(N_HD, t // DN_GROUP, 1, DN_GROUP))


def _dn_local_bwd(q, k, v, beta_c, gc_c, gc_r, t_inv, du, dw, dqk, dqd, dkd, degl):
    t = q.shape[0]
    nc = t // DN_CHUNK
    nl = LOCAL_ROWS // DN_CHUNK
    wide, half, col, row, egl = _dn_specs(t)
    dspec = pl.BlockSpec((1, LOCAL_ROWS, LANES), lambda hd, i: (_dir_of(hd), i, _head_of(hd)))

    def body(q_ref, k_ref, v_ref, b_ref, gc_ref, gr_ref, t_ref, du_ref, dw_ref, dqk_ref, dqd_ref, dkd_ref, degl_ref,
             dq_ref, dk_ref, dv_ref, db_ref, dgc_ref, dgr_ref):
        sgn = _hd_sign(pl.program_id(0))
        for c in range(nl):
            rows = slice(c * DN_CHUNK, (c + 1) * DN_CHUNK)
            k, beta, gcc, gcr, tinv = k_ref[rows, :], b_ref[0, rows, :], gc_ref[0, rows, :], gr_ref[0, c], t_ref[0, rows, :]
            _, vjp = jax.vjp(functools.partial(_dn_local, sgn=sgn), tinv, q_ref[rows, :], k, v_ref[rows, :], beta, gcc, gcr)
            dt, dq, dk, dv, db, dgc, dgr = vjp((du_ref[0, rows, :], dw_ref[0, rows, :], dqk_ref[0, rows, :],
                                                dqd_ref[0, rows, :], dkd_ref[0, rows, :], degl_ref[0, c]))
            _, vjp_a = jax.vjp(functools.partial(_dn_a, sgn=sgn), k, beta, gcc, gcr)
            dk2, db2, dgc2, dgr2 = vjp_a(_inv_unit_t(tinv, dt))
            dq_ref[0, rows, :] = dq
            dk_ref[0, rows, :] = dk + dk2
            dv_ref[0, rows, :] = dv
            db_ref[0, rows, :] = db + db2
            dgc_ref[0, rows, :] = dgc + dgc2
            dgr_ref[0, c] = dgr + dgr2

    per_dir = _sds((2, t, DN_WIDTH))
    return pl.pallas_call(
        body, name="dn_local_bwd", grid=(N_HD, t // LOCAL_ROWS),
        in_specs=_qkv_specs() + [col, col, row, half, wide, wide, half, wide, wide, egl],
        out_specs=[dspec, dspec, dspec, col, col, row],
        out_shape=[per_dir, per_dir, per_dir, _sds((N_HD, t, 1)), _sds((N_HD, t, 1)), _sds((N_HD, nc, 1, DN_CHUNK))],
        compiler_params=_params(("parallel", "parallel")),
    )(q, k, v, beta_c, gc_c, gc_r, t_inv, du, dw, dqk, dqd, dkd, degl)


REC_HEADS = 4
REC_GROUPS = N_HD // REC_HEADS
REC_FWD_GROUPS = DN_HEADS // REC_HEADS


def _rec_specs(time_block):
    nr = REC_ROWS // DN_CHUNK
    wide = pl.BlockSpec((REC_HEADS, REC_ROWS, LANES), lambda g, b: (g, time_block(g, b), 0))
    half = pl.BlockSpec((REC_HEADS, REC_ROWS, DN_CHUNK), lambda g, b: (g, time_block(g, b), 0))
    egl = pl.BlockSpec((REC_HEADS, nr, 1, LANES), lambda g, b: (g, time_block(g, b), 0, 0))
    state = pl.BlockSpec((REC_HEADS, nr, DN_HEAD_DIM, DN_HEAD_DIM), lambda g, b: (g, time_block(g, b), 0, 0))
    return wide, half, egl, state


def _rec_head_cols(g):
    return jnp.where(g < REC_FWD_GROUPS, g, g - REC_FWD_GROUPS)


def _dn_rec_fwd(u, w, qk, qd, kd, egl):
    t = u.shape[1]
    nb = t // REC_ROWS
    nr = REC_ROWS // DN_CHUNK
    nc = t // DN_CHUNK

    def time_block(g, b):
        return jnp.where(g < REC_FWD_GROUPS, b, nb - 1 - b)

    wide, half, egl_spec, state = _rec_specs(time_block)
    o_spec = pl.BlockSpec((1, REC_ROWS, REC_HEADS * LANES),
                          lambda g, b: (jnp.where(g < REC_FWD_GROUPS, 0, 1), time_block(g, b), _rec_head_cols(g)))

    def body(u_ref, w_ref, qk_ref, qd_ref, kd_ref, egl_ref, o_ref, vn_ref, s_ref, s_scr):
        fwd = pl.program_id(0) < REC_FWD_GROUPS

        @pl.when(pl.program_id(1) == 0)
        def _():
            s_scr[...] = jnp.zeros_like(s_scr)

        s = [s_scr[j] for j in range(REC_HEADS)]
        for c in range(nr):
            ce = jnp.where(fwd, c, nr - 1 - c)
            rows = pl.ds(pl.multiple_of(ce * DN_CHUNK, DN_CHUNK), DN_CHUNK)
            for j in range(REC_HEADS):
                s_ref[j, ce] = s[j]
                vn = u_ref[j, rows, :] - _bnn(w_ref[j, rows, :], s[j])
                o_ref[0, rows, j * LANES:(j + 1) * LANES] = _bnn(qd_ref[j, rows, :], s[j]) + _bnn(qk_ref[j, rows, :], vn)
                vn_ref[j, rows, :] = vn
                s[j] = s[j] * egl_ref[j, ce] + _btn(kd_ref[j, rows, :], vn)
        for j in range(REC_HEADS):
            s_scr[j] = s[j]

    return pl.pallas_call(
        body, name="dn_rec_fwd", grid=(REC_GROUPS, nb), in_specs=[wide, wide, half, wide, wide, egl_spec],
        out_specs=[o_spec, wide, state],
        out_shape=[_sds((2, t, DN_WIDTH)), _sds((N_HD, t, LANES)), _sds((N_HD, nc, DN_HEAD_DIM, DN_HEAD_DIM))],
        scratch_shapes=[pltpu.VMEM((REC_HEADS, DN_HEAD_DIM, DN_HEAD_DIM), F32)],
        compiler_params=_params(("parallel", "arbitrary")),
    )(u, w, qk, qd, kd, egl)


def _dn_rec_bwd(w, qk, qd, kd, egl, vn, states, do):
    t = w.shape[1]
    nb = t // REC_ROWS
    nr = REC_ROWS // DN_CHUNK
    nc = t // DN_CHUNK

    def time_block(g, b):
        return jnp.where(g < REC_FWD_GROUPS, nb - 1 - b, b)

    wide, half, egl_spec, state = _rec_specs(time_block)
    do_spec = pl.BlockSpec((REC_ROWS, REC_HEADS * LANES), lambda g, b: (time_block(g, b), _rec_head_cols(g)))

    def body(w_ref, qk_ref, qd_ref, kd_ref, egl_ref, vn_ref, s_ref, do_ref,
             du_ref, dw_ref, dqk_ref, dqd_ref, dkd_ref, degl_ref, ds_scr):
        fwd = pl.program_id(0) < REC_FWD_GROUPS

        @pl.when(pl.program_id(1) == 0)
        def _():
            ds_scr[...] = jnp.zeros_like(ds_scr)

        ds = [ds_scr[j] for j in range(REC_HEADS)]
        for c in range(nr):
            ce = jnp.where(fwd, nr - 1 - c, c)
            rows = pl.ds(pl.multiple_of(ce * DN_CHUNK, DN_CHUNK), DN_CHUNK)
            for j in range(REC_HEADS):
                s = s_ref[j, ce]
                do_c = do_ref[rows, j * LANES:(j + 1) * LANES]
                vn_c = vn_ref[j, rows, :]
                dvn = _btn(qk_ref[j, rows, :], do_c) + _bnn(kd_ref[j, rows, :], ds[j])
                du_ref[j, rows, :] = dvn
                dw_ref[j, rows, :] = -_bnt(dvn, s)
                dqk_ref[j, rows, :] = _bnt(do_c, vn_c)
                dqd_ref[j, rows, :] = _bnt(do_c, s)
                dkd_ref[j, rows, :] = _bnt(vn_c, ds[j])
                degl_ref[j, ce] = jnp.sum(s * ds[j], axis=0, keepdims=True)
                ds[j] = ds[j] * egl_ref[j, ce] + _btn(qd_ref[j, rows, :], do_c) - _btn(w_ref[j, rows, :], dvn)
        for j in range(REC_HEADS):
            ds_scr[j] = ds[j]

    big = _sds((N_HD, t, LANES))
    return pl.pallas_call(
        body, name="dn_rec_bwd", grid=(REC_GROUPS, nb), in_specs=[wide, half, wide, wide, egl_spec, wide, state, do_spec],
        out_specs=[wide, wide, half, wide, wide, egl_spec],
        out_shape=[big, big, _sds((N_HD, t, DN_CHUNK)), big, big, _sds((N_HD, nc, 1, LANES))],
        scratch_shapes=[pltpu.VMEM((REC_HEADS, DN_HEAD_DIM, DN_HEAD_DIM), F32)],
        compiler_params=_params(("parallel", "arbitrary")),
    )(w, qk, qd, kd, egl, vn, states, do)


def _post_fn(of, ob, z, gain):
    o = of + ob
    return o * lax.rsqrt(jnp.mean(o * o, axis=-1, keepdims=True) + RMS_EPS) * gain * _silu(z)


def _post_specs():
    o_spec = [pl.BlockSpec((1, ROW_TILE, LANES), functools.partial(lambda i, h, d: (d, i, h), d=d)) for d in (0, 1)]
    z_spec = pl.BlockSpec((ROW_TILE, LANES), lambda i, h: (i, C_Z // LANES + h))
    head = pl.BlockSpec((ROW_TILE, LANES), lambda i, h: (i, h))
    gain = pl.BlockSpec((1, LANES), lambda i, h: (0, 0))
    return o_spec, z_spec, head, gain


def _post_fwd(o2, proj, gain):
    t = proj.shape[0]
    o_spec, z_spec, head, gain_spec = _post_specs()

    def body(of_ref, ob_ref, z_ref, g_ref, out_ref):
        out_ref[...] = _post_fn(of_ref[0], ob_ref[0], z_ref[...], g_ref[...]).astype(_MXU)

    return pl.pallas_call(
        body, name="post_fwd", grid=(t // ROW_TILE, DN_HEADS), in_specs=o_spec + [z_spec, gain_spec], out_specs=head,
        out_shape=_sds((t, DN_WIDTH), _MXU), compiler_params=_params(("parallel", "parallel")),
    )(o2, o2, proj, gain)


def _post_bwd(o2, proj, gain, dout):
    t = proj.shape[0]
    o_spec, z_spec, head, gain_spec = _post_specs()

    def body(of_ref, ob_ref, z_ref, g_ref, d_ref, do_ref, dz_ref, dg_ref):
        _, vjp = jax.vjp(_post_fn, of_ref[0], ob_ref[0], z_ref[...], g_ref[...])
        do, _, dz, dg = vjp(d_ref[...])
        do_ref[...] = do
        dz_ref[...] = dz.astype(_MXU)
        _accumulate(dg_ref, dg, pl.program_id(0) * DN_HEADS + pl.program_id(1))

    return pl.pallas_call(
        body, name="post_bwd", grid=(t // ROW_TILE, DN_HEADS), in_specs=o_spec + [z_spec, gain_spec, head],
        out_specs=[head, head, gain_spec], out_shape=[_sds((t, DN_WIDTH)), _sds((t, DN_WIDTH), _MXU), _sds((1, LANES))],
        compiler_params=_params(("arbitrary", "arbitrary")),
    )(o2, o2, proj, gain, dout)


def _rope(x, cos, sin):
    lane = lax.broadcasted_iota(jnp.int32, x.shape, 1)
    first = (lane & (SW_HEAD_DIM - 1)) < SW_HEAD_DIM // 2
    rot = jnp.where(first, -pltpu.roll(x, LANES - SW_HEAD_DIM // 2, 1), pltpu.roll(x, SW_HEAD_DIM // 2, 1))
    return x * cos + rot * sin


def _rope_apply(q, k, q_cols, k_cols, cos, sin, name, dtype):
    t = cos.shape[0]
    qw, kw = SW_HEADS * SW_HEAD_DIM, SW_KV_WIDTH

    def body(q_ref, k_ref, c_ref, s_ref, qo_ref, ko_ref):
        c, s = c_ref[...], s_ref[...]
        for j in range(qw // LANES):
            cols = slice(j * LANES, (j + 1) * LANES)
            qo_ref[:, cols] = _rope(q_ref[:, cols], c, s).astype(dtype)
        for j in range(kw // LANES):
            cols = slice(j * LANES, (j + 1) * LANES)
            ko_ref[:, cols] = _rope(k_ref[:, cols], c, s).astype(dtype)

    return pl.pallas_call(
        body, name=name, grid=(t // ROW_TILE,), in_specs=[_cols(qw, q_cols), _cols(kw, k_cols), _rows(LANES), _rows(LANES)],
        out_specs=[_rows(qw), _rows(kw)], out_shape=[_sds((t, qw), dtype), _sds((t, kw), dtype)],
        compiler_params=_params(("parallel",)),
    )(q, k, cos, sin)


def _attn_core(qs, kb, vb, sink, mask):
    s = _bnt(qs, kb) * (SW_HEAD_DIM ** -0.5)
    s = jnp.where(mask, s, -1e30)
    m = lax.stop_gradient(jnp.maximum(jnp.max(s, axis=1, keepdims=True), sink))
    e = jnp.exp(s - m)
    den = jnp.sum(e, axis=1, keepdims=True) + jnp.exp(sink - m)
    return _bnn(e / den, vb)


def _band_mask(n, nb):
    rows = SW_GROUP * SW_BLOCK
    i = lax.broadcasted_iota(jnp.int32, (rows, 3 * SW_BLOCK), 0) & (SW_BLOCK - 1)
    j = lax.broadcasted_iota(jnp.int32, (rows, 3 * SW_BLOCK), 1)
    near = (j - i >= 0) & (j - i <= 2 * SW_BLOCK)
    lo = jnp.where(n == 0, SW_BLOCK, 0)
    hi = jnp.where(n == nb - 1, 2 * SW_BLOCK, 3 * SW_BLOCK)
    return near & (j >= lo) & (j < hi)


def _band_specs(nb, v_cols):
    def spec(width, base, shift):
        return pl.BlockSpec((SW_BLOCK, width), lambda n: (jnp.clip(n + shift, 0, nb - 1), base // width))
    k_specs = [spec(SW_KV_WIDTH, 0, s) for s in (-1, 0, 1)]
    v_specs = [spec(SW_KV_WIDTH, v_cols, s) for s in (-1, 0, 1)]
    return k_specs, v_specs


def _head_cols(kv, g):
    h = kv * SW_GROUP + g
    return slice(h * SW_HEAD_DIM, (h + 1) * SW_HEAD_DIM)


def _attn_fwd(qr, kr, proj, sinks):
    t = qr.shape[0]
    nb = t // SW_BLOCK
    qw = SW_HEADS * SW_HEAD_DIM
    k_specs, v_specs = _band_specs(nb, C_VSW)
    q_spec = pl.BlockSpec((SW_BLOCK, qw), lambda n: (n, 0))

    def body(q_ref, k0, k1, k2, v0, v1, v2, s_ref, o_ref):
        mask = _band_mask(pl.program_id(0), nb)
        kb = jnp.concatenate([k0[...], k1[...], k2[...]], axis=0)
        vb = jnp.concatenate([v0[...], v1[...], v2[...]], axis=0)
        for kv in range(SW_KV_HEADS):
            kvc = slice(kv * SW_HEAD_DIM, (kv + 1) * SW_HEAD_DIM)
            qs = jnp.concatenate([q_ref[:, _head_cols(kv, g)] for g in range(SW_GROUP)], axis=0)
            sink = jnp.concatenate([jnp.broadcast_to(s_ref[kv * SW_GROUP + g], (SW_BLOCK, 1)) for g in range(SW_GROUP)], axis=0)
            o = _attn_core(qs, kb[:, kvc], vb[:, kvc], sink, mask)
            for g in range(SW_GROUP):
                o_ref[:, _head_cols(kv, g)] = o[g * SW_BLOCK:(g + 1) * SW_BLOCK, :].astype(_MXU)

    return pl.pallas_call(
        body, name="attn_fwd", grid=(nb,), in_specs=[q_spec] + k_specs + v_specs + [_full((SW_HEADS, 1, 1))],
        out_specs=q_spec, out_shape=_sds((t, qw), _MXU), compiler_params=_params(("parallel",)),
    )(qr, kr, kr, kr, proj, proj, proj, sinks)


def _attn_bwd(qr, kr, proj, sinks, do):
    t = qr.shape[0]
    nb = t // SW_BLOCK
    qw = SW_HEADS * SW_HEAD_DIM
    k_specs, v_specs = _band_specs(nb, C_VSW)
    q_spec = pl.BlockSpec((SW_BLOCK, qw), lambda n: (n, 0))
    part = pl.BlockSpec((1, 3 * SW_BLOCK, SW_KV_WIDTH), lambda n: (n, 0, 0))

    def body(q_ref, k0, k1, k2, v0, v1, v2, s_ref, do_ref, dq_ref, dk_ref, dv_ref, ds_ref):
        mask = _band_mask(pl.program_id(0), nb)
        kb = jnp.concatenate([k0[...], k1[...], k2[...]], axis=0).astype(F32)
        vb = jnp.concatenate([v0[...], v1[...], v2[...]], axis=0)

        @pl.when(pl.program_id(0) == 0)
        def _():
            ds_ref[...] = jnp.zeros_like(ds_ref)

        for kv in range(SW_KV_HEADS):
            kvc = slice(kv * SW_HEAD_DIM, (kv + 1) * SW_HEAD_DIM)
            qs = jnp.concatenate([q_ref[:, _head_cols(kv, g)] for g in range(SW_GROUP)], axis=0).astype(F32)
            dos = jnp.concatenate([do_ref[:, _head_cols(kv, g)] for g in range(SW_GROUP)], axis=0)
            sink = jnp.concatenate([jnp.broadcast_to(s_ref[kv * SW_GROUP + g], (SW_BLOCK, 1)) for g in range(SW_GROUP)], axis=0)
            _, vjp = jax.vjp(functools.partial(_attn_core, mask=mask), qs, kb[:, kvc], vb[:, kvc], sink)
            dqs, dkb, dvb, dsink = vjp(dos)
            dk_ref[0, :, kvc] = dkb
            dv_ref[0, :, kvc] = dvb
            for g in range(SW_GROUP):
                rows = slice(g * SW_BLOCK, (g + 1) * SW_BLOCK)
                dq_ref[:, _head_cols(kv, g)] = dqs[rows, :]
                ds_ref[kv * SW_GROUP + g] += jnp.sum(dsink[rows, :], axis=0, keepdims=True)

    parts = _sds((nb, 3 * SW_BLOCK, SW_KV_WIDTH))
    return pl.pallas_call(
        body, name="attn_bwd", grid=(nb,), in_specs=[q_spec] + k_specs + v_specs + [_full((SW_HEADS, 1, 1)), q_spec],
        out_specs=[q_spec, part, part, _full((SW_HEADS, 1, 1))], out_shape=[_sds((t, qw)), parts, parts, _sds((SW_HEADS, 1, 1))],
        compiler_params=_params(("arbitrary",)),
    )(qr, kr, kr, kr, proj, proj, proj, sinks, do)


def _band_sum(parts, name, dtype):
    nb = parts.shape[0]
    w = parts.shape[2]

    def spec(shift, slot):
        return pl.BlockSpec((1, SW_BLOCK, w), lambda m: (jnp.clip(m + shift, 0, nb - 1), slot, 0))

    def body(prev_ref, own_ref, next_ref, o_ref):
        m = pl.program_id(0)
        total = own_ref[0] + jnp.where(m > 0, prev_ref[0], 0.0) + jnp.where(m < nb - 1, next_ref[0], 0.0)
        o_ref[...] = total.astype(dtype)

    return pl.pallas_call(
        body, name=name, grid=(nb,), in_specs=[spec(-1, 2), spec(0, 1), spec(1, 0)],
        out_specs=pl.BlockSpec((SW_BLOCK, w), lambda m: (m, 0)), out_shape=_sds((nb * SW_BLOCK, w), dtype),
        compiler_params=_params(("parallel",)),
    )(parts, parts, parts)


def _gate_layouts(gbo):
    t = gbo.shape[0]
    beta_c = gbo[:, :N_HD].T.reshape(N_HD, t, 1)
    gc = gbo[:, N_HD:2 * N_HD].T
    return beta_c, gc.reshape(N_HD, t, 1), gc.reshape(N_HD, t // DN_CHUNK, 1, DN_CHUNK)


def _gate_layouts_t(dbeta_c, dgc_c, dgc_r):
    t = dbeta_c.shape[1]
    pad = jnp.zeros((t, LANES - 2 * N_HD), F32)
    none = jnp.zeros((t, N_HD), F32)
    d_a = jnp.concatenate([dbeta_c.reshape(N_HD, t).T, dgc_c.reshape(N_HD, t).T, pad], axis=1)
    d_b = jnp.concatenate([none, dgc_r.reshape(N_HD, t).T, pad], axis=1)
    return d_a, d_b


def _layer_fwd(x, xb, w, cos, sin):
    proj = _mm(xb, w["wm"], "nn", name="proj")
    ba = _mm(xb, w["wba"], "nn", name="proj_gates")
    qn, kn, vv = [_prep_fwd(proj, w["conv"], kind) for kind in range(3)]
    gbo = _gb_fwd(ba, w["alog"], w["dtb"])
    beta_c, gc_c, gc_r = _gate_layouts(gbo)
    u, wk, qk, qd, kd, egl, tinv = _dn_local_fwd(qn, kn, vv, beta_c, gc_c, gc_r)
    o2, vn, states = _dn_rec_fwd(u, wk, qk, qd, kd, egl)
    o_dn = _post_fwd(o2, proj, w["dnw"])
    qr, kr = _rope_apply(proj, proj, C_QSW, C_KSW, cos, sin, "rope_fwd", _MXU)
    o_sw = _attn_fwd(qr, kr, proj, w["sinks"])
    ya = _mm(o_dn, w["wa"], "nn", name="branch_a")
    yb = _mm(o_sw, w["wb"], "nn", name="branch_b")
    merged = _merge_fwd(proj, ya, yb)
    mix = _mm(merged, w["wo"], "nn", name="mix_out")
    x1, x1b = _ln_fwd(x, mix, w["ln1g"], w["ln1b"], "ln1_fwd")
    gu = _mm(x1b, w["wgu"], "nn", name="ffn_up")
    h = _swiglu_fwd(gu)
    f = _mm(h, w["wd"], "nn", name="ffn_down")
    x2, x2b = _ln_fwd(x1, f, w["ln2g"], w["ln2b"], "ln2_fwd")
    res = dict(x=x, xb=xb, proj=proj, ba=ba, qn=qn, kn=kn, vv=vv, gbo=gbo, wk=wk, qk=qk, qd=qd, kd=kd, egl=egl, tinv=tinv, vn=vn,
               states=states, o2=o2, o_dn=o_dn, qr=qr, kr=kr, o_sw=o_sw, ya=ya, yb=yb, merged=merged, mix=mix, x1=x1, x1b=x1b,
               gu=gu, h=h, f=f)
    return x2, x2b, res


def _layer_bwd(dx2, w, r, cos, sin):
    dx1, df, dln2g, dln2b = _ln_bwd(r["x1"], r["f"], w["ln2g"], w["ln2b"], dx2, "ln2_bwd")
    dh = _mm(df, w["wd"], "nt", name="d_ffn_hidden")
    dwd = _mm(r["h"], df, "tn", name="dw_ffn_down", out_dtype=_MXU)
    dgu = _swiglu_bwd(r["gu"], dh)
    dwgu = _mm(r["x1b"], dgu, "tn", name="dw_ffn_up", out_dtype=_MXU)
    dx1 = _mm(dgu, w["wgu"], "nt", name="dx_ffn", add=dx1)
    dx, dmix, dln1g, dln1b = _ln_bwd(r["x"], r["mix"], w["ln1g"], w["ln1b"], dx1, "ln1_bwd")
    dmerged = _mm(dmix, w["wo"], "nt", name="d_merged")
    dwo = _mm(r["merged"], dmix, "tn", name="dw_mix_out", out_dtype=_MXU)
    dga, dgb, dya, dyb = _merge_bwd(r["proj"], r["ya"], r["yb"], dmerged)
    dwa = _mm(r["o_dn"], dya, "tn", name="dw_branch_a", out_dtype=_MXU)
    do_dn = _mm(dya, w["wa"], "nt", name="d_branch_a")
    dwb = _mm(r["o_sw"], dyb, "tn", name="dw_branch_b", out_dtype=_MXU)
    do_sw = _mm(dyb, w["wb"], "nt", name="d_branch_b")
    do, dz, ddnw = _post_bwd(r["o2"], r["proj"], w["dnw"], do_dn)
    du, dwk, dqk, dqd, dkd, degl = _dn_rec_bwd(r["wk"], r["qk"], r["qd"], r["kd"], r["egl"], r["vn"], r["states"], do)
    beta_c, gc_c, gc_r = _gate_layouts(r["gbo"])
    dq3, dk3, dv3, dbeta_c, dgc_c, dgc_r = _dn_local_bwd(r["qn"], r["kn"], r["vv"], beta_c, gc_c, gc_r, r["tinv"],
                                                         du, dwk, dqk, dqd, dkd, degl)
    dqkv, dconv = zip(*[_prep_bwd(r["proj"], w["conv"], d2, kind) for kind, d2 in enumerate((dq3, dk3, dv3))])
    dconv = jnp.concatenate(dconv, axis=1)
    d_a, d_b = _gate_layouts_t(dbeta_c, dgc_c, dgc_r)
    dba, dalog, ddtb = _gb_bwd(r["ba"], w["alog"], w["dtb"], d_a, d_b)
    dqr, dkparts, dvparts, dsinks = _attn_bwd(r["qr"], r["kr"], r["proj"], w["sinks"], do_sw)
    dkr = _band_sum(dkparts, "attn_dk_sum", F32)
    dv = _band_sum(dvparts, "attn_dv_sum", _MXU)
    dq_sw, dk_sw = _rope_apply(dqr, dkr, 0, 0, cos, -sin, "rope_bwd", _MXU)
    dproj = jnp.concatenate([*dqkv, dz, dq_sw, dga, dgb, dk_sw, dv], axis=1)
    dwm = _mm(r["xb"], dproj, "tn", name="dw_proj", out_dtype=_MXU)
    dwba = _mm(r["xb"], dba, "tn", name="dw_proj_gates", out_dtype=_MXU)
    dx = _mm(dproj, w["wm"], "nt", name="dx_proj", add=dx)
    dx = _mm(dba, w["wba"], "nt", name="dx_proj_gates", add=dx)
    grads = dict(wm=dwm, wba=dwba, conv=dconv, alog=dalog, dtb=ddtb, dnw=ddnw, sinks=dsinks, wa=dwa, wb=dwb, wo=dwo,
                 ln1g=dln1g, ln1b=dln1b, wgu=dwgu, wd=dwd, ln2g=dln2g, ln2b=dln2b)
    return dx, grads


def _rope_tables(t):
    half = SW_HEAD_DIM // 2
    inv_freq = ROPE_THETA ** (-jnp.arange(half, dtype=F32) / half)
    ang = jnp.arange(t, dtype=F32)[:, None] * inv_freq[None, :]
    return jnp.tile(jnp.cos(ang), (1, LANES // half)), jnp.tile(jnp.sin(ang), (1, LANES // half))


def _trunk(x, target, layers):
    cos, sin = _rope_tables(x.shape[0])
    xb = x.astype(_MXU)
    saved = []
    for w in layers:
        x, xb, res = _layer_fwd(x, xb, w, cos, sin)
        saved.append(res)
    sq, dx = _loss_head(x, target)
    grads = [None] * len(layers)
    for i in reversed(range(len(layers))):
        dx, grads[i] = _layer_bwd(dx, layers[i], saved[i], cos, sin)
    return sq, dx, grads


N_CHIPS = 4


def _mesh_pos():
    return lax.axis_index("x"), lax.axis_index("y"), lax.axis_index("c")


def _other_chips(x, y):
    return [(1 - x, y), (x, 1 - y), (1 - x, 1 - y)]


def _remote_copy(src, dst, sems, k, to):
    send_sems, recv_sems = sems
    return pltpu.make_async_remote_copy(src_ref=src, dst_ref=dst, send_sem=send_sems.at[k], recv_sem=recv_sems.at[k],
                                        device_id=to, device_id_type=pl.DeviceIdType.MESH)


def _comm_call(body, name, out_shape, n_sems, *operands):
    return pl.pallas_call(
        body, name=name, in_specs=[pl.BlockSpec(memory_space=pl.ANY)] * len(operands),
        out_specs=pl.BlockSpec(memory_space=pl.ANY), out_shape=out_shape,
        scratch_shapes=[pltpu.SemaphoreType.DMA((n_sems,)), pltpu.SemaphoreType.DMA((n_sems,)), pltpu.SemaphoreType.DMA],
        compiler_params=pltpu.CompilerParams(has_side_effects=True),
    )(*operands)


def _all_gather(block, name):
    def body(x_ref, o_ref, send_sems, recv_sems, local_sem):
        x, y, c = _mesh_pos()
        sems = (send_sems, recv_sems)
        sibling = (x, y, 1 - c)
        chips = _other_chips(x, y)

        def slot(px, py, pc):
            return o_ref.at[4 * px + 2 * py + pc]

        mine = pltpu.make_async_copy(x_ref, slot(x, y, c), local_sem)
        mine.start()
        first = [_remote_copy(x_ref, slot(x, y, c), sems, 0, sibling)]
        first += [_remote_copy(x_ref, slot(x, y, c), sems, 1 + j, (*chip, c)) for j, chip in enumerate(chips)]
        for cp in first:
            cp.start()
        passed = [_remote_copy(slot(*chip, c), slot(*chip, c), sems, 4 + j, sibling) for j, chip in enumerate(chips)]
        for j, chip in enumerate(chips):
            _remote_copy(x_ref, slot(*chip, c), sems, 1 + j, sibling).wait_recv()
            passed[j].start()
        _remote_copy(x_ref, slot(x, y, 1 - c), sems, 0, sibling).wait_recv()
        for j, chip in enumerate(chips):
            _remote_copy(x_ref, slot(*chip, 1 - c), sems, 4 + j, sibling).wait_recv()
        for cp in first + passed:
            cp.wait_send()
        mine.wait()

    return _comm_call(body, name, _sds((N_DEV,) + block.shape, block.dtype), N_DEV - 1, block)


def _sibling_swap(parts, name):
    def body(x_ref, o_ref, send_sems, recv_sems, local_sem):
        x, y, c = _mesh_pos()
        sems = (send_sems, recv_sems)
        sibling = (x, y, 1 - c)
        copies = [_remote_copy(x_ref.at[2 * q + (1 - c)], o_ref.at[q], sems, q, sibling) for q in range(N_CHIPS)]
        for cp in copies:
            cp.start()
        for cp in copies:
            cp.wait()

    return _comm_call(body, name, _sds((N_CHIPS,) + parts.shape[1:], parts.dtype), N_CHIPS, parts)


def _chip_exchange(parts, name):
    def body(x_ref, o_ref, send_sems, recv_sems, local_sem):
        x, y, c = _mesh_pos()
        sems = (send_sems, recv_sems)
        me = 2 * x + y
        mine = pltpu.make_async_copy(x_ref.at[me], o_ref.at[me], local_sem)
        mine.start()
        copies = [_remote_copy(x_ref.at[2 * cx + cy], o_ref.at[me], sems, j, (cx, cy, c))
                  for j, (cx, cy) in enumerate(_other_chips(x, y))]
        for cp in copies:
            cp.start()
        for j, (cx, cy) in enumerate(_other_chips(x, y)):
            _remote_copy(x_ref.at[me], o_ref.at[2 * cx + cy], sems, j, (cx, cy, c)).wait_recv()
        for cp in copies:
            cp.wait_send()
        mine.wait()

    return _comm_call(body, name, _sds(parts.shape, parts.dtype), N_CHIPS - 1, parts)


def _pair_sum(a, b, name):
    n, rows, cols = a.shape
    tr = rows if rows <= 512 else _row_tile(rows)
    blk = pl.BlockSpec((1, tr, cols), lambda q, i: (q, i, 0))

    def body(a_ref, b_ref, o_ref):
        o_ref[...] = (a_ref[...].astype(F32) + b_ref[...].astype(F32)).astype(o_ref.dtype)

    return pl.pallas_call(
        body, name=name, grid=(n, rows // tr), in_specs=[blk, blk], out_specs=blk, out_shape=_sds(a.shape, a.dtype),
        compiler_params=_params(("parallel", "parallel")),
    )(a, b)


def _reduce_to_owner(parts, name):
    c = lax.axis_index("c")
    from_sibling = _sibling_swap(parts, "swap_" + name)
    own = lax.dynamic_index_in_dim(parts.reshape((N_CHIPS, 2) + parts.shape[1:]), c, axis=1, keepdims=False)
    return _chip_exchange(_pair_sum(own, from_sibling, "pair_sum_" + name), "exchange_" + name)


def _sum_adamw(parts, w, m, v, name):
    rows, cols = w.shape
    n_parts = parts.shape[0]
    tr = rows if rows <= 512 else _row_tile(rows)
    blk = pl.BlockSpec((tr, cols), lambda i: (i, 0))

    def body(p_ref, w_ref, m_ref, v_ref, g_ref, d_ref, nm_ref, nv_ref):
        g = p_ref[0].astype(F32)
        for i in range(1, n_parts):
            g = g + p_ref[i].astype(F32)
        nm = ADAM_B1 * m_ref[...] + (1.0 - ADAM_B1) * g
        nv = ADAM_B2 * v_ref[...] + (1.0 - ADAM_B2) * jnp.square(g)
        m_hat = nm / (1.0 - ADAM_B1 ** ADAM_STEP)
        v_hat = nv / (1.0 - ADAM_B2 ** ADAM_STEP)
        g_ref[...] = g
        d_ref[...] = -ADAM_LR * (m_hat / (jnp.sqrt(v_hat) + ADAM_EPS) + ADAM_WD * w_ref[...])
        nm_ref[...] = nm
        nv_ref[...] = nv

    return pl.pallas_call(
        body, name=name, grid=(rows // tr,), in_specs=[pl.BlockSpec((n_parts, tr, cols), lambda i: (0, i, 0)), blk, blk, blk],
        out_specs=[blk] * 4, out_shape=[_sds((rows, cols))] * 4, compiler_params=_params(("parallel",)),
    )(parts, w, m, v)


def _row_tile(rows):
    for t in (256, 128, 64, 32, 16, 8):
        if rows % t == 0:
            return t
    return rows


def _gathered_cols(g):
    n, l, rows, c = g.shape
    return jnp.transpose(g, (1, 2, 0, 3)).reshape(l, rows, n * c)


def _gathered_rows(g):
    n, l, rows, c = g.shape
    return jnp.transpose(g, (1, 0, 2, 3)).reshape(l, n * rows, c)


def _col_parts(full):
    l, rows, c = full.shape
    return jnp.transpose(full.reshape(l, rows, N_DEV, c // N_DEV), (2, 0, 1, 3))


def _row_parts(full):
    l, rows, c = full.shape
    return jnp.transpose(full.reshape(l, N_DEV, rows // N_DEV, c), (1, 0, 2, 3))


def _w_in_split(w_in):
    s = lambda a, n: w_in[..., a:a + n]
    main = jnp.concatenate([s(R_QKV, 3072), s(R_Z, 1024), s(R_QSW, 1024), s(R_G, 2048), s(R_KSW, 256), s(R_VSW, 256)], axis=-1)
    gates = jnp.pad(s(R_BA, 2 * N_HD), ((0, 0), (0, 0), (0, LANES - 2 * N_HD)))
    return main, gates


def _w_in_join(dmain, dgates):
    s = lambda a, n: dmain[..., a:a + n]
    return jnp.concatenate([s(C_QKV, 3072), s(C_Z, 1024), dgates[..., :2 * N_HD], s(C_QSW, 1024), s(C_KSW, 256), s(C_VSW, 256),
                            s(C_GA, 2048)], axis=-1)


def _lane_row(a, offset):
    l, n = a.shape
    return jnp.pad(a, ((0, 0), (offset, LANES - offset - n)))[:, None, :]


def kernel(x, w_in, conv_w, a_log, dt_bias, dn_norm_w, sinks, w_branch_a, w_branch_b, w_out, ln1_g, ln1_b, w_gate_up, w_down, ln2_g, ln2_b, loss_target, m_w_in, m_conv_w, m_a_log, m_dt_bias, m_dn_norm_w, m_sinks, m_w_branch_a, m_w_branch_b, m_w_out, m_ln1_g, m_ln1_b, m_w_gate_up, m_w_down, m_ln2_g, m_ln2_b, v_w_in, v_conv_w, v_a_log, v_dt_bias, v_dn_norm_w, v_sinks, v_w_branch_a, v_w_branch_b, v_w_out, v_ln1_g, v_ln1_b, v_w_gate_up, v_w_down, v_ln2_g, v_ln2_b):
    l = DEPTH
    bf = lambda a: a.astype(_MXU)
    w_in_full = _gathered_cols(_all_gather(bf(w_in), "gather_w_in"))
    wgu_full = _gathered_cols(_all_gather(bf(w_gate_up), "gather_w_gate_up"))
    wa_full = _gathered_rows(_all_gather(bf(w_branch_a), "gather_w_branch_a"))
    wb_full = _gathered_rows(_all_gather(bf(w_branch_b), "gather_w_branch_b"))
    wo_full = _gathered_rows(_all_gather(bf(w_out), "gather_w_out"))
    wd_full = _gathered_rows(_all_gather(bf(w_down), "gather_w_down"))
    conv_full = _gathered_cols(_all_gather(conv_w, "gather_conv_w"))
    wm, wba = _w_in_split(w_in_full)
    row = lambda a: a[:, None, :]
    stacked = dict(
        wm=wm, wba=wba, conv=jnp.pad(conv_full, ((0, 0), (0, 8 - DN_CONV), (0, 0))),
        alog=_lane_row(a_log.reshape(l, N_HD), N_HD), dtb=_lane_row(dt_bias.reshape(l, N_HD), N_HD), dnw=row(dn_norm_w),
        sinks=sinks.reshape(l, SW_HEADS, 1, 1), wa=wa_full, wb=wb_full, wo=wo_full, ln1g=row(ln1_g), ln1b=row(ln1_b),
        wgu=wgu_full, wd=wd_full, ln2g=row(ln2_g), ln2b=row(ln2_b))

    sq, dx, grads = _trunk(x[0], loss_target[0], [{k: a[i] for k, a in stacked.items()} for i in range(l)])
    g = {k: jnp.stack([gi[k] for gi in grads]) for k in grads[0]}
    loss = lax.psum(0.5 * sq[0, 0] / D_MODEL, ("x", "y", "c"))

    def big(parts, w, m, v, name):
        rows = w.shape[0] * w.shape[1]
        flat = lambda a: a.reshape(rows, a.shape[-1])
        got = _reduce_to_owner(parts.reshape(N_DEV, rows, w.shape[-1]), name)
        outs = _sum_adamw(got, flat(w), flat(m), flat(v), "adamw_" + name)
        return [o.reshape(w.shape) for o in outs]

    dconv = g["conv"][:, :DN_CONV, :]
    results = {
        "w_in": big(_col_parts(_w_in_join(g["wm"], g["wba"])), w_in, m_w_in, v_w_in, "w_in"),
        "conv_w": big(_col_parts(dconv), conv_w, m_conv_w, v_conv_w, "conv_w"),
        "w_branch_a": big(_row_parts(g["wa"]), w_branch_a, m_w_branch_a, v_w_branch_a, "w_branch_a"),
        "w_branch_b": big(_row_parts(g["wb"]), w_branch_b, m_w_branch_b, v_w_branch_b, "w_branch_b"),
        "w_out": big(_row_parts(g["wo"]), w_out, m_w_out, v_w_out, "w_out"),
        "w_gate_up": big(_col_parts(g["wgu"]), w_gate_up, m_w_gate_up, v_w_gate_up, "w_gate_up"),
        "w_down": big(_row_parts(g["wd"]), w_down, m_w_down, v_w_down, "w_down"),
    }

    small_w = {"a_log": a_log.reshape(l, N_HD), "dt_bias": dt_bias.reshape(l, N_HD), "dn_norm_w": dn_norm_w, "sinks": sinks,
               "ln1_g": ln1_g, "ln1_b": ln1_b, "ln2_g": ln2_g, "ln2_b": ln2_b}
    small_m = {"a_log": m_a_log, "dt_bias": m_dt_bias, "dn_norm_w": m_dn_norm_w, "sinks": m_sinks, "ln1_g": m_ln1_g,
               "ln1_b": m_ln1_b, "ln2_g": m_ln2_g, "ln2_b": m_ln2_b}
    small_v = {"a_log": v_a_log, "dt_bias": v_dt_bias, "dn_norm_w": v_dn_norm_w, "sinks": v_sinks, "ln1_g": v_ln1_g,
               "ln1_b": v_ln1_b, "ln2_g": v_ln2_g, "ln2_b": v_ln2_b}
    small_g = {"a_log": g["alog"][:, 0, N_HD:2 * N_HD], "dt_bias": g["dtb"][:, 0, N_HD:2 * N_HD], "dn_norm_w": g["dnw"][:, 0, :],
               "sinks": g["sinks"].reshape(l, SW_HEADS), "ln1_g": g["ln1g"][:, 0, :], "ln1_b": g["ln1b"][:, 0, :],
               "ln2_g": g["ln2g"][:, 0, :], "ln2_b": g["ln2b"][:, 0, :]}
    names = list(small_w)
    cat = lambda d: jnp.concatenate([d[n].reshape(l, -1) for n in names], axis=1)
    widths = [small_w[n].shape[1] for n in names]
    total = sum(widths)
    padded = -(-total // LANES) * LANES
    pad = lambda a: jnp.pad(a, ((0, 8 - l), (0, padded - total)))
    got = _all_gather(pad(cat(small_g)), "gather_small_grads")
    outs = _sum_adamw(got, pad(cat(small_w)), pad(cat({n: small_m[n].reshape(l, -1) for n in names})),
                      pad(cat({n: small_v[n].reshape(l, -1) for n in names})), "adamw_small")
    off = 0
    for n, wd_ in zip(names, widths):
        shape = {"a_log": a_log.shape, "dt_bias": dt_bias.shape}.get(n, small_w[n].shape)
        results[n] = [o[:l, off:off + wd_].reshape(shape) for o in outs]
        off += wd_

    order = ["w_in", "conv_w", "a_log", "dt_bias", "dn_norm_w", "sinks", "w_branch_a", "w_branch_b", "w_out", "ln1_g", "ln1_b",
             "w_gate_up", "w_down", "ln2_g", "ln2_b"]
    return (loss, dx[None], *[results[n][0] for n in order], *[results[n][1] for n in order],
            *[results[n][2] for n in order], *[results[n][3] for n in order])
```

```python
import functools

import jax
import jax.numpy as jnp
from jax import lax
from jax.experimental import pallas as pl
from jax.experimental.pallas import tpu as pltpu

F32 = jnp.float32
_MXU = jnp.bfloat16
_HI = lax.Precision.HIGHEST
_MID = lax.Precision.HIGH

N_DEV = 8
D_MODEL = 1024
DEPTH = 4
DN_HEADS = 8
DN_HEAD_DIM = 128
DN_WIDTH = DN_HEADS * DN_HEAD_DIM
DN_CONV = 5
DN_CHUNK = 64
SW_HEADS = 16
SW_KV_HEADS = 4
SW_HEAD_DIM = 64
SW_GROUP = SW_HEADS // SW_KV_HEADS
SW_BLOCK = 128
SW_KV_WIDTH = SW_KV_HEADS * SW_HEAD_DIM
ROPE_THETA = 10000.0
FFN_HIDDEN = 2816
DN_ALPHA = (2.0 * DEPTH) ** 0.25
LN_EPS = 1e-5
RMS_EPS = 1e-6
ADAM_LR = 0.001
ADAM_B1 = 0.9
ADAM_B2 = 0.999
ADAM_EPS = 1e-08
ADAM_WD = 0.01
ADAM_STEP = 10

LANES = 128
N_HD = 2 * DN_HEADS
DN_GROUP = 4 * DN_CHUNK
INV_SUB = 16
LOCAL_ROWS = 512
REC_ROWS = 256
ROW_TILE = 256
VMEM_LIMIT = 48 << 20

C_QKV, C_Z, C_QSW, C_GA, C_GB, C_KSW, C_VSW = 0, 3072, 4096, 5120, 6144, 7168, 7424
MAIN_COLS = 7680
R_QKV, R_Z, R_BA, R_QSW, R_KSW, R_VSW, R_G = 0, 3072, 4096, 4128, 5152, 5408, 5664
IN_COLS = 7712


_NN = ((1,), (0,))
_NT = ((1,), (1,))
_TN = ((0,), (0,))


def _dg(a, b, dims, precision):
    if precision is not None:
        return lax.dot_general(a, b, (dims, ((), ())), precision=precision, preferred_element_type=F32)
    return lax.dot_general(a.astype(_MXU), b.astype(_MXU), (dims, ((), ())), preferred_element_type=F32)


def _make_dots(hi):
    @jax.custom_vjp
    def nn(a, b):
        return _dg(a, b, _NN, hi)

    @jax.custom_vjp
    def nt(a, b):
        return _dg(a, b, _NT, hi)

    @jax.custom_vjp
    def tn(a, b):
        return _dg(a, b, _TN, hi)

    nn.defvjp(lambda a, b: (nn(a, b), (a, b)), lambda r, g: (nt(g, r[1]), tn(r[0], g)))
    nt.defvjp(lambda a, b: (nt(a, b), (a, b)), lambda r, g: (nn(g, r[1]), tn(g, r[0])))
    tn.defvjp(lambda a, b: (tn(a, b), (a, b)), lambda r, g: (nt(r[1], g), nn(r[0], g)))
    return nn, nt, tn


_bnn, _bnt, _btn = _make_dots(None)
_hnn, _hnt, _htn = _make_dots(_HI)
_mnn, _mnt, _mtn = _make_dots(_MID)


def _neumann(a, order):
    n = a.shape[0]
    eye = (lax.broadcasted_iota(jnp.int32, (n, n), 0) == lax.broadcasted_iota(jnp.int32, (n, n), 1)).astype(F32)
    inv = eye - a
    p = a
    span = 2
    while span < order:
        p = _mnn(p, p)
        inv = inv + _mnn(inv, p)
        span *= 2
    return inv


def _inv_unit(a, order):
    n = a.shape[0]
    ii = lax.broadcasted_iota(jnp.int32, (n, n), 0)
    jj = lax.broadcasted_iota(jnp.int32, (n, n), 1)
    near = (ii & -INV_SUB) == (jj & -INV_SUB)
    d_inv = _neumann(jnp.where(near, a, 0.0), INV_SUB)
    outer = _neumann(_mnn(d_inv, jnp.where(near, 0.0, a)), order // INV_SUB)
    return _mnn(outer, d_inv)


def _inv_unit_t(t, g):
    return -_mnt(_mtn(t, g), t)


def _silu(x):
    return x * jax.nn.sigmoid(x)


def _softplus(x):
    return jnp.maximum(x, 0.0) + jnp.log1p(jnp.exp(-jnp.abs(x)))


def _params(sem=None):
    kw = {"vmem_limit_bytes": VMEM_LIMIT}
    if sem is not None:
        kw["dimension_semantics"] = sem
    return pltpu.CompilerParams(**kw)


def _tile(dim, pref):
    if dim <= pref:
        return dim
    t = (pref // LANES) * LANES
    while t > LANES and dim % t:
        t -= LANES
    assert dim % t == 0, (dim, pref)
    return t


def _full(shape):
    zeros = (0,) * len(shape)
    return pl.BlockSpec(shape, lambda *_: zeros)


def _sds(shape, dtype=F32):
    return jax.ShapeDtypeStruct(shape, dtype)


def _mm(a, b, mode, *, name, add=None, tm=1536, tn=1536, tk=1536, out_dtype=F32):
    if mode == "nn":
        (m, k), (k2, n) = a.shape, b.shape
    elif mode == "nt":
        (m, k), (n, k2) = a.shape, b.shape
    else:
        (k, m), (k2, n) = a.shape, b.shape
    assert k == k2, (a.shape, b.shape, mode)
    tm, tn, tk = _tile(m, tm), _tile(n, tn), _tile(k, tk)
    nk = k // tk
    dims = {"nn": _NN, "nt": _NT, "tn": _TN}[mode]

    def body(*refs):
        if add is None:
            a_ref, b_ref, o_ref, acc = refs
        else:
            a_ref, b_ref, add_ref, o_ref, acc = refs
        kk = pl.program_id(2)

        @pl.when(kk == 0)
        def _():
            acc[...] = jnp.zeros_like(acc)

        acc[...] += _dg(a_ref[...], b_ref[...], dims, None)

        @pl.when(kk == nk - 1)
        def _():
            o_ref[...] = (acc[...] if add is None else acc[...] + add_ref[...]).astype(out_dtype)

    a_spec = pl.BlockSpec((tk, tm), lambda i, j, kk: (kk, i)) if mode == "tn" else pl.BlockSpec((tm, tk), lambda i, j, kk: (i, kk))
    b_spec = pl.BlockSpec((tn, tk), lambda i, j, kk: (j, kk)) if mode == "nt" else pl.BlockSpec((tk, tn), lambda i, j, kk: (kk, j))
    o_spec = pl.BlockSpec((tm, tn), lambda i, j, kk: (i, j))
    ins, specs = [a, b], [a_spec, b_spec]
    if add is not None:
        ins.append(add)
        specs.append(o_spec)
    return pl.pallas_call(
        body, name=name, grid=(m // tm, n // tn, nk), in_specs=specs, out_specs=o_spec,
        out_shape=_sds((m, n), out_dtype), scratch_shapes=[pltpu.VMEM((tm, tn), F32)],
        compiler_params=_params(("parallel", "parallel", "arbitrary")),
    )(*ins)


def _cols(width, start):
    assert start % width == 0
    return pl.BlockSpec((ROW_TILE, width), lambda i: (i, start // width))


def _rows(width):
    return pl.BlockSpec((ROW_TILE, width), lambda i: (i, 0))


def _accumulate(ref, value, step):
    @pl.when(step == 0)
    def _():
        ref[...] = value

    @pl.when(step != 0)
    def _():
        ref[...] += value


def _ln_fn(x, r, g, b):
    u = DN_ALPHA * x + r
    mu = jnp.mean(u, axis=-1, keepdims=True)
    var = jnp.mean(jnp.square(u - mu), axis=-1, keepdims=True)
    return (u - mu) * lax.rsqrt(var + LN_EPS) * g + b


def _ln_fwd(x, r, g, b, name):
    t, d = x.shape

    def body(x_ref, r_ref, g_ref, b_ref, o_ref, ob_ref):
        y = _ln_fn(x_ref[...], r_ref[...], g_ref[...], b_ref[...])
        o_ref[...] = y
        ob_ref[...] = y.astype(_MXU)

    return pl.pallas_call(
        body, name=name, grid=(t // ROW_TILE,), in_specs=[_rows(d), _rows(d), _full((1, d)), _full((1, d))],
        out_specs=[_rows(d), _rows(d)], out_shape=[_sds((t, d)), _sds((t, d), _MXU)], compiler_params=_params(("parallel",)),
    )(x, r, g, b)


def _ln_bwd(x, r, g, b, dy, name):
    t, d = x.shape

    def body(x_ref, r_ref, g_ref, b_ref, dy_ref, dx_ref, dr_ref, dg_ref, db_ref):
        _, vjp = jax.vjp(_ln_fn, x_ref[...], r_ref[...], g_ref[...], b_ref[...])
        dx, dr, dg, db = vjp(dy_ref[...])
        dx_ref[...] = dx
        dr_ref[...] = dr.astype(_MXU)
        _accumulate(dg_ref, dg, pl.program_id(0))
        _accumulate(db_ref, db, pl.program_id(0))

    return pl.pallas_call(
        body, name=name, grid=(t // ROW_TILE,),
        in_specs=[_rows(d), _rows(d), _full((1, d)), _full((1, d)), _rows(d)],
        out_specs=[_rows(d), _rows(d), _full((1, d)), _full((1, d))],
        out_shape=[_sds((t, d)), _sds((t, d), _MXU), _sds((1, d)), _sds((1, d))],
        compiler_params=_params(("arbitrary",)),
    )(x, r, g, b, dy)


def _merge_fn(ga, gb, ya, yb):
    return jax.nn.sigmoid(ga) * ya + jax.nn.sigmoid(gb) * yb


def _merge_fwd(proj, ya, yb):
    t, d = ya.shape

    def body(ga_ref, gb_ref, ya_ref, yb_ref, o_ref):
        o_ref[...] = _merge_fn(ga_ref[...], gb_ref[...], ya_ref[...], yb_ref[...]).astype(_MXU)

    return pl.pallas_call(
        body, name="merge_fwd", grid=(t // ROW_TILE,), in_specs=[_cols(d, C_GA), _cols(d, C_GB), _rows(d), _rows(d)],
        out_specs=_rows(d), out_shape=_sds((t, d), _MXU), compiler_params=_params(("parallel",)),
    )(proj, proj, ya, yb)


def _merge_bwd(proj, ya, yb, dm):
    t, d = ya.shape

    def body(ga_ref, gb_ref, ya_ref, yb_ref, dm_ref, dga_ref, dgb_ref, dya_ref, dyb_ref):
        _, vjp = jax.vjp(_merge_fn, ga_ref[...], gb_ref[...], ya_ref[...], yb_ref[...])
        dga_ref[...], dgb_ref[...], dya_ref[...], dyb_ref[...] = [g.astype(_MXU) for g in vjp(dm_ref[...])]

    return pl.pallas_call(
        body, name="merge_bwd", grid=(t // ROW_TILE,),
        in_specs=[_cols(d, C_GA), _cols(d, C_GB), _rows(d), _rows(d), _rows(d)],
        out_specs=[_rows(d)] * 4, out_shape=[_sds((t, d), _MXU)] * 4, compiler_params=_params(("parallel",)),
    )(proj, proj, ya, yb, dm)


def _swiglu_fn(gate, up):
    return _silu(gate) * up


def _swiglu_fwd(gu):
    t = gu.shape[0]
    f = FFN_HIDDEN
    rows = 128

    def body(gu_ref, o_ref):
        o_ref[...] = _swiglu_fn(gu_ref[:, :f], gu_ref[:, f:]).astype(_MXU)

    return pl.pallas_call(
        body, name="swiglu_fwd", grid=(t // rows,), in_specs=[pl.BlockSpec((rows, 2 * f), lambda i: (i, 0))],
        out_specs=pl.BlockSpec((rows, f), lambda i: (i, 0)), out_shape=_sds((t, f), _MXU), compiler_params=_params(("parallel",)),
    )(gu)


def _swiglu_bwd(gu, dh):
    t = gu.shape[0]
    f = FFN_HIDDEN
    rows = 128

    def body(gu_ref, dh_ref, o_ref):
        _, vjp = jax.vjp(_swiglu_fn, gu_ref[:, :f], gu_ref[:, f:])
        o_ref[:, :f], o_ref[:, f:] = [g.astype(_MXU) for g in vjp(dh_ref[...])]

    return pl.pallas_call(
        body, name="swiglu_bwd", grid=(t // rows,),
        in_specs=[pl.BlockSpec((rows, 2 * f), lambda i: (i, 0)), pl.BlockSpec((rows, f), lambda i: (i, 0))],
        out_specs=pl.BlockSpec((rows, 2 * f), lambda i: (i, 0)), out_shape=_sds((t, 2 * f), _MXU),
        compiler_params=_params(("parallel",)),
    )(gu, dh)


def _loss_head(y, target):
    t, d = y.shape

    def body(y_ref, t_ref, s_ref, dy_ref):
        err = y_ref[...] - t_ref[...]
        dy_ref[...] = err / d
        _accumulate(s_ref, jnp.broadcast_to(jnp.sum(jnp.square(err)), (1, LANES)), pl.program_id(0))

    return pl.pallas_call(
        body, name="loss_head", grid=(t // ROW_TILE,), in_specs=[_rows(d), _rows(d)],
        out_specs=[_full((1, LANES)), _rows(d)], out_shape=[_sds((1, LANES)), _sds((t, d))],
        compiler_params=_params(("arbitrary",)),
    )(y, target)


def _shift_rows(x, s):
    if s == 0:
        return x
    t = x.shape[0]
    rolled = pltpu.roll(x, (-s) % t, 0)
    row = lax.broadcasted_iota(jnp.int32, x.shape, 0)
    return jnp.where((row + s >= 0) & (row + s < t), rolled, 0.0)


def _conv(x, w):
    half = DN_CONV // 2
    acc = None
    for k in range(DN_CONV):
        term = _shift_rows(x, k - half) * w[k:k + 1, :]
        acc = term if acc is None else acc + term
    return acc


def _act_norm(c, do_norm, scale):
    a = _silu(c)
    if not do_norm:
        return a
    return a * lax.rsqrt(jnp.sum(a * a, axis=-1, keepdims=True) + RMS_EPS) * scale


PREP_ROWS = 512
HALO = 8
_KINDS = ((True, DN_HEAD_DIM ** -0.5), (True, 1.0), (False, 1.0))


def _halo_rows(read, i, pr, t):
    lo, hi = i * pr - HALO, (i + 1) * pr + HALO
    parts = []
    if lo < 0:
        parts.append(jnp.zeros((HALO, LANES), F32))
    parts.append(read(max(lo, 0), min(hi, t)))
    if hi > t:
        parts.append(jnp.zeros((HALO, LANES), F32))
    return jnp.concatenate(parts, axis=0) if len(parts) > 1 else parts[0]


def _prep_fwd(proj, conv_w, kind):
    t = proj.shape[0]
    pr = min(PREP_ROWS, t)
    do_norm, scale = _KINDS[kind]
    blk = pl.BlockSpec((t, LANES), lambda j: (0, kind * DN_HEADS + j))

    def body(x_ref, w_ref, o_ref):
        w = w_ref[...]
        for i in range(t // pr):
            xx = _halo_rows(lambda lo, hi: x_ref[lo:hi, :], i, pr, t)
            c = _conv(xx, w)[HALO:HALO + pr, :]
            o_ref[i * pr:(i + 1) * pr, :] = _act_norm(c, do_norm, scale)

    return pl.pallas_call(
        body, name=f"prep_fwd_{kind}", grid=(DN_HEADS,),
        in_specs=[blk, pl.BlockSpec((8, LANES), lambda j: (0, kind * DN_HEADS + j))],
        out_specs=pl.BlockSpec((t, LANES), lambda j: (0, j)), out_shape=_sds((t, DN_WIDTH)),
        compiler_params=_params(("parallel",)),
    )(proj, conv_w)


def _prep_bwd(proj, conv_w, d2, kind):
    t = proj.shape[0]
    pr = min(PREP_ROWS, t)
    do_norm, scale = _KINDS[kind]
    half = DN_CONV // 2
    blk = pl.BlockSpec((t, LANES), lambda j: (0, kind * DN_HEADS + j))
    oblk = pl.BlockSpec((t, LANES), lambda j: (0, j))

    def body(x_ref, w_ref, d_ref, dx_ref, dw_ref):
        w = w_ref[...]
        own = slice(HALO, HALO + pr)
        dw = jnp.zeros((8, LANES), F32)
        for i in range(t // pr):
            xx = _halo_rows(lambda lo, hi: x_ref[lo:hi, :], i, pr, t)
            dn = _halo_rows(lambda lo, hi: d_ref[0, lo:hi, :] + d_ref[1, lo:hi, :], i, pr, t)
            _, vjp = jax.vjp(lambda c: _act_norm(c, do_norm, scale), _conv(xx, w))
            (dc,) = vjp(dn)
            dx = None
            rows = []
            for k in range(DN_CONV):
                term = _shift_rows(dc, half - k) * w[k:k + 1, :]
                dx = term if dx is None else dx + term
                rows.append(jnp.sum(dc[own, :] * _shift_rows(xx, k - half)[own, :], axis=0, keepdims=True))
            dx_ref[i * pr:(i + 1) * pr, :] = dx[own, :].astype(_MXU)
            dw = dw + jnp.concatenate(rows + [jnp.zeros((8 - DN_CONV, LANES), F32)], axis=0)
        dw_ref[...] = dw

    return pl.pallas_call(
        body, name=f"prep_bwd_{kind}", grid=(DN_HEADS,),
        in_specs=[blk, pl.BlockSpec((8, LANES), lambda j: (0, kind * DN_HEADS + j)), pl.BlockSpec((2, t, LANES), lambda j: (0, 0, j))],
        out_specs=[oblk, pl.BlockSpec((8, LANES), lambda j: (0, j))], out_shape=[_sds((t, DN_WIDTH), _MXU), _sds((8, DN_WIDTH))],
        compiler_params=_params(("parallel",)),
    )(proj, conv_w, d2)


def _gb_fn(ba, alog_row, dtb_row):
    c = DN_CHUNK
    lane = lax.broadcasted_iota(jnp.int32, (c, LANES), 1)
    ii = lax.broadcasted_iota(jnp.int32, (c, c), 0)
    jj = lax.broadcasted_iota(jnp.int32, (c, c), 1)
    beta = jax.nn.sigmoid(ba)
    g = -jnp.exp(alog_row) * _softplus(ba + dtb_row)
    g = jnp.where((lane >= N_HD) & (lane < 2 * N_HD), g, 0.0)
    gc_fwd = _hnn((ii >= jj).astype(F32), g)
    gc_rev = _hnn((ii <= jj).astype(F32), g)
    gc = jnp.where(lane < N_HD + DN_HEADS, gc_fwd, gc_rev)
    return jnp.where(lane < N_HD, beta, jnp.where(lane < 2 * N_HD, gc, 0.0))


def _gb_fwd(ba, alog_row, dtb_row):
    t = ba.shape[0]
    n = ROW_TILE // DN_CHUNK

    def body(ba_ref, a_ref, d_ref, o_ref):
        for c in range(n):
            rows = slice(c * DN_CHUNK, (c + 1) * DN_CHUNK)
            o_ref[rows, :] = _gb_fn(ba_ref[rows, :], a_ref[...], d_ref[...])

    return pl.pallas_call(
        body, name="gates_fwd", grid=(t // ROW_TILE,), in_specs=[_rows(LANES), _full((1, LANES)), _full((1, LANES))],
        out_specs=_rows(LANES), out_shape=_sds((t, LANES)), compiler_params=_params(("parallel",)),
    )(ba, alog_row, dtb_row)


def _gb_bwd(ba, alog_row, dtb_row, d_a, d_b):
    t = ba.shape[0]
    n = ROW_TILE // DN_CHUNK

    def body(ba_ref, a_ref, d_ref, da_ref, db_ref, dba_ref, dal_ref, ddt_ref):
        dal = jnp.zeros((1, LANES), F32)
        ddt = jnp.zeros((1, LANES), F32)
        for c in range(n):
            rows = slice(c * DN_CHUNK, (c + 1) * DN_CHUNK)
            _, vjp = jax.vjp(_gb_fn, ba_ref[rows, :], a_ref[...], d_ref[...])
            dba, da, dd = vjp(da_ref[rows, :] + db_ref[rows, :])
            dba_ref[rows, :] = dba.astype(_MXU)
            dal = dal + da
            ddt = ddt + dd
        _accumulate(dal_ref, dal, pl.program_id(0))
        _accumulate(ddt_ref, ddt, pl.program_id(0))

    return pl.pallas_call(
        body, name="gates_bwd", grid=(t // ROW_TILE,),
        in_specs=[_rows(LANES), _full((1, LANES)), _full((1, LANES)), _rows(LANES), _rows(LANES)],
        out_specs=[_rows(LANES), _full((1, LANES)), _full((1, LANES))],
        out_shape=[_sds((t, LANES), _MXU), _sds((1, LANES)), _sds((1, LANES))], compiler_params=_params(("arbitrary",)),
    )(ba, alog_row, dtb_row, d_a, d_b)


def _dn_decay(gcc, gcr, sgn):
    c = DN_CHUNK
    ii = lax.broadcasted_iota(jnp.int32, (c, c), 0)
    jj = lax.broadcasted_iota(jnp.int32, (c, c), 1)
    d = (ii - jj) * sgn
    lower = d >= 0
    return jnp.where(lower, jnp.exp(jnp.where(lower, gcc - gcr, 0.0)), 0.0), d > 0


def _dn_a(k, beta, gcc, gcr, sgn):
    decay, strict = _dn_decay(gcc, gcr, sgn)
    return jnp.where(strict, beta * _bnt(k, k) * decay, 0.0)


def _dn_group(q, k, v, beta, gcc, gcr, sgn):
    n = DN_GROUP
    ii = lax.broadcasted_iota(jnp.int32, (n, n), 0)
    jj = lax.broadcasted_iota(jnp.int32, (n, n), 1)
    same = (ii & -DN_CHUNK) == (jj & -DN_CHUNK)
    d = (ii - jj) * sgn
    lower = same & (d >= 0)
    decay = jnp.where(lower, jnp.exp(jnp.where(lower, gcc - gcr, 0.0)), 0.0)
    a = jnp.where(same & (d > 0), beta * _bnt(k, k) * decay, 0.0)
    t_inv = _inv_unit(a, DN_CHUNK)
    u = _bnn(t_inv, v * beta)
    w = _bnn(t_inv, k * (beta * jnp.exp(gcc)))
    return u, w, _bnt(q, k) * decay, t_inv


def _dn_local(t_inv, q, k, v, beta, gcc, gcr, sgn):
    c = DN_CHUNK
    decay, _ = _dn_decay(gcc, gcr, sgn)
    eg = jnp.exp(gcc)
    u = _bnn(t_inv, v * beta)
    w = _bnn(t_inv, k * (beta * eg))
    qk = _bnt(q, k) * decay
    qd = q * eg
    last = jnp.where(sgn > 0, c - 1, 0)
    onehot = (lax.broadcasted_iota(jnp.int32, (c, 1), 0) == last).astype(F32)
    gl = jnp.sum(gcc * onehot, axis=0, keepdims=True)
    kd = k * jnp.exp(gl - gcc)
    egl = jnp.broadcast_to(jnp.exp(gl), (1, LANES))
    return u, w, qk, qd, kd, egl


def _hd_sign(hd):
    return jnp.where(hd < DN_HEADS, 1, -1).astype(jnp.int32)


def _head_of(hd):
    return jnp.where(hd < DN_HEADS, hd, hd - DN_HEADS)


def _dir_of(hd):
    return jnp.where(hd < DN_HEADS, 0, 1)


def _dn_specs(t):
    nl = LOCAL_ROWS // DN_CHUNK
    wide = pl.BlockSpec((1, LOCAL_ROWS, LANES), lambda hd, i: (hd, i, 0))
    half = pl.BlockSpec((1, LOCAL_ROWS, DN_CHUNK), lambda hd, i: (hd, i, 0))
    col = pl.BlockSpec((1, LOCAL_ROWS, 1), lambda hd, i: (hd, i, 0))
    row = pl.BlockSpec((1, nl, 1, DN_CHUNK), lambda hd, i: (hd, i, 0, 0))
    egl = pl.BlockSpec((1, nl, 1, LANES), lambda hd, i: (hd, i, 0, 0))
    return wide, half, col, row, egl


def _qkv_specs():
    return [pl.BlockSpec((LOCAL_ROWS, LANES), lambda hd, i: (i, _head_of(hd)))] * 3


def _dn_local_fwd(q, k, v, beta_c, gc_c, gc_r):
    t = q.shape[0]
    nc = t // DN_CHUNK
    nl = LOCAL_ROWS // DN_CHUNK
    wide, half, col, row, egl = _dn_specs(t)

    ng = LOCAL_ROWS // DN_GROUP
    per = DN_GROUP // DN_CHUNK
    grow = pl.BlockSpec((1, ng, 1, DN_GROUP), lambda hd, i: (hd, i, 0, 0))

    def body(q_ref, k_ref, v_ref, b_ref, gc_ref, gg_ref, u_ref, w_ref, qk_ref, qd_ref, kd_ref, egl_ref, t_ref):
        sgn = _hd_sign(pl.program_id(0))
        last = jnp.where(sgn > 0, DN_CHUNK - 1, 0)
        onehot = (lax.broadcasted_iota(jnp.int32, (DN_CHUNK, 1), 0) == last).astype(F32)
        for gi in range(ng):
            grows = slice(gi * DN_GROUP, (gi + 1) * DN_GROUP)
            q, k, gcc = q_ref[grows, :], k_ref[grows, :], gc_ref[0, grows, :]
            u, w, qk, t_inv = _dn_group(q, k, v_ref[grows, :], b_ref[0, grows, :], gcc, gg_ref[0, gi], sgn)
            u_ref[0, grows, :] = u
            w_ref[0, grows, :] = w
            qd_ref[0, grows, :] = q * jnp.exp(gcc)
            for c in range(per):
                blk = slice(c * DN_CHUNK, (c + 1) * DN_CHUNK)
                rows = slice(gi * DN_GROUP + c * DN_CHUNK, gi * DN_GROUP + (c + 1) * DN_CHUNK)
                qk_ref[0, rows, :] = qk[blk, blk]
                t_ref[0, rows, :] = t_inv[blk, blk]
                gl = jnp.sum(gcc[blk, :] * onehot, axis=0, keepdims=True)
                kd_ref[0, rows, :] = k[blk, :] * jnp.exp(gl - gcc[blk, :])
                egl_ref[0, gi * per + c] = jnp.broadcast_to(jnp.exp(gl), (1, LANES))

    big = _sds((N_HD, t, LANES))
    small = _sds((N_HD, t, DN_CHUNK))
    return pl.pallas_call(
        body, name="dn_local_fwd", grid=(N_HD, t // LOCAL_ROWS), in_specs=_qkv_specs() + [col, col, grow],
        out_specs=[wide, wide, half, wide, wide, egl, half],
        out_shape=[big, big, small, big, big, _sds((N_HD, nc, 1, LANES)), small],
        compiler_params=_params(("parallel", "parallel")),
    )(q, k, v, beta_c, gc_c, gc_r.reshape(N_HD, t // DN_GROUP, 1, DN_GROUP))


def _dn_local_bwd(q, k, v, beta_c, gc_c, gc_r, t_inv, du, dw, dqk, dqd, dkd, degl):
    t = q.shape[0]
    nc = t // DN_CHUNK
    nl = LOCAL_ROWS // DN_CHUNK
    wide, half, col, row, egl = _dn_specs(t)
    dspec = pl.BlockSpec((1, LOCAL_ROWS, LANES), lambda hd, i: (_dir_of(hd), i, _head_of(hd)))

    def body(q_ref, k_ref, v_ref, b_ref, gc_ref, gr_ref, t_ref, du_ref, dw_ref, dqk_ref, dqd_ref, dkd_ref, degl_ref,
             dq_ref, dk_ref, dv_ref, db_ref, dgc_ref, dgr_ref):
        sgn = _hd_sign(pl.program_id(0))
        for c in range(nl):
            rows = slice(c * DN_CHUNK, (c + 1) * DN_CHUNK)
            k, beta, gcc, gcr, tinv = k_ref[rows, :], b_ref[0, rows, :], gc_ref[0, rows, :], gr_ref[0, c], t_ref[0, rows, :]
            _, vjp = jax.vjp(functools.partial(_dn_local, sgn=sgn), tinv, q_ref[rows, :], k, v_ref[rows, :], beta, gcc, gcr)
            dt, dq, dk, dv, db, dgc, dgr = vjp((du_ref[0, rows, :], dw_ref[0, rows, :], dqk_ref[0, rows, :],
                                                dqd_ref[0, rows, :], dkd_ref[0, rows, :], degl_ref[0, c]))
            _, vjp_a = jax.vjp(functools.partial(_dn_a, sgn=sgn), k, beta, gcc, gcr)
            dk2, db2, dgc2, dgr2 = vjp_a(_inv_unit_t(tinv, dt))
            dq_ref[0, rows, :] = dq
            dk_ref[0, rows, :] = dk + dk2
            dv_ref[0, rows, :] = dv
            db_ref[0, rows, :] = db + db2
            dgc_ref[0, rows, :] = dgc + dgc2
            dgr_ref[0, c] = dgr + dgr2

    per_dir = _sds((2, t, DN_WIDTH))
    return pl.pallas_call(
        body, name="dn_local_bwd", grid=(N_HD, t // LOCAL_ROWS),
        in_specs=_qkv_specs() + [col, col, row, half, wide, wide, half, wide, wide, egl],
        out_specs=[dspec, dspec, dspec, col, col, row],
        out_shape=[per_dir, per_dir, per_dir, _sds((N_HD, t, 1)), _sds((N_HD, t, 1)), _sds((N_HD, nc, 1, DN_CHUNK))],
        compiler_params=_params(("parallel", "parallel")),
    )(q, k, v, beta_c, gc_c, gc_r, t_inv, du, dw, dqk, dqd, dkd, degl)


REC_HEADS = 8
REC_GROUPS = N_HD // REC_HEADS
REC_FWD_GROUPS = DN_HEADS // REC_HEADS


def _rec_specs(time_block):
    nr = REC_ROWS // DN_CHUNK
    wide = pl.BlockSpec((REC_HEADS, REC_ROWS, LANES), lambda g, b: (g, time_block(g, b), 0))
    half = pl.BlockSpec((REC_HEADS, REC_ROWS, DN_CHUNK), lambda g, b: (g, time_block(g, b), 0))
    egl = pl.BlockSpec((REC_HEADS, nr, 1, LANES), lambda g, b: (g, time_block(g, b), 0, 0))
    state = pl.BlockSpec((REC_HEADS, nr, DN_HEAD_DIM, DN_HEAD_DIM), lambda g, b: (g, time_block(g, b), 0, 0))
    return wide, half, egl, state


def _rec_head_cols(g):
    return jnp.where(g < REC_FWD_GROUPS, g, g - REC_FWD_GROUPS)


def _dn_rec_fwd(u, w, qk, qd, kd, egl):
    t = u.shape[1]
    nb = t // REC_ROWS
    nr = REC_ROWS // DN_CHUNK
    nc = t // DN_CHUNK

    def time_block(g, b):
        return jnp.where(g < REC_FWD_GROUPS, b, nb - 1 - b)

    wide, half, egl_spec, state = _rec_specs(time_block)
    o_spec = pl.BlockSpec((1, REC_ROWS, REC_HEADS * LANES),
                          lambda g, b: (jnp.where(g < REC_FWD_GROUPS, 0, 1), time_block(g, b), _rec_head_cols(g)))

    def body(u_ref, w_ref, qk_ref, qd_ref, kd_ref, egl_ref, o_ref, vn_ref, s_ref, s_scr):
        fwd = pl.program_id(0) < REC_FWD_GROUPS

        @pl.when(pl.program_id(1) == 0)
        def _():
            s_scr[...] = jnp.zeros_like(s_scr)

        s = [s_scr[j] for j in range(REC_HEADS)]
        for c in range(nr):
            ce = jnp.where(fwd, c, nr - 1 - c)
            rows = pl.ds(pl.multiple_of(ce * DN_CHUNK, DN_CHUNK), DN_CHUNK)
            for j in range(REC_HEADS):
                s_ref[j, ce] = s[j]
                vn = u_ref[j, rows, :] - _bnn(w_ref[j, rows, :], s[j])
                o_ref[0, rows, j * LANES:(j + 1) * LANES] = _bnn(qd_ref[j, rows, :], s[j]) + _bnn(qk_ref[j, rows, :], vn)
                vn_ref[j, rows, :] = vn
                s[j] = s[j] * egl_ref[j, ce] + _btn(kd_ref[j, rows, :], vn)
        for j in range(REC_HEADS):
            s_scr[j] = s[j]

    return pl.pallas_call(
        body, name="dn_rec_fwd", grid=(REC_GROUPS, nb), in_specs=[wide, wide, half, wide, wide, egl_spec],
        out_specs=[o_spec, wide, state],
        out_shape=[_sds((2, t, DN_WIDTH)), _sds((N_HD, t, LANES)), _sds((N_HD, nc, DN_HEAD_DIM, DN_HEAD_DIM))],
        scratch_shapes=[pltpu.VMEM((REC_HEADS, DN_HEAD_DIM, DN_HEAD_DIM), F32)],
        compiler_params=_params(("parallel", "arbitrary")),
    )(u, w, qk, qd, kd, egl)


def _dn_rec_bwd(w, qk, qd, kd, egl, vn, states, do):
    t = w.shape[1]
    nb = t // REC_ROWS
    nr = REC_ROWS // DN_CHUNK
    nc = t // DN_CHUNK

    def time_block(g, b):
        return jnp.where(g < REC_FWD_GROUPS, nb - 1 - b, b)

    wide, half, egl_spec, state = _rec_specs(time_block)
    do_spec = pl.BlockSpec((REC_ROWS, REC_HEADS * LANES), lambda g, b: (time_block(g, b), _rec_head_cols(g)))

    def body(w_ref, qk_ref, qd_ref, kd_ref, egl_ref, vn_ref, s_ref, do_ref,
             du_ref, dw_ref, dqk_ref, dqd_ref, dkd_ref, degl_ref, ds_scr):
        fwd = pl.program_id(0) < REC_FWD_GROUPS

        @pl.when(pl.program_id(1) == 0)
        def _():
            ds_scr[...] = jnp.zeros_like(ds_scr)

        ds = [ds_scr[j] for j in range(REC_HEADS)]
        for c in range(nr):
            ce = jnp.where(fwd, nr - 1 - c, c)
            rows = pl.ds(pl.multiple_of(ce * DN_CHUNK, DN_CHUNK), DN_CHUNK)
            for j in range(REC_HEADS):
                s = s_ref[j, ce]
                do_c = do_ref[rows, j * LANES:(j + 1) * LANES]
                vn_c = vn_ref[j, rows, :]
                dvn = _btn(qk_ref[j, rows, :], do_c) + _bnn(kd_ref[j, rows, :], ds[j])
                du_ref[j, rows, :] = dvn
                dw_ref[j, rows, :] = -_bnt(dvn, s)
                dqk_ref[j, rows, :] = _bnt(do_c, vn_c)
                dqd_ref[j, rows, :] = _bnt(do_c, s)
                dkd_ref[j, rows, :] = _bnt(vn_c, ds[j])
                degl_ref[j, ce] = jnp.sum(s * ds[j], axis=0, keepdims=True)
                ds[j] = ds[j] * egl_ref[j, ce] + _btn(qd_ref[j, rows, :], do_c) - _btn(w_ref[j, rows, :], dvn)
        for j in range(REC_HEADS):
            ds_scr[j] = ds[j]

    big = _sds((N_HD, t, LANES))
    return pl.pallas_call(
        body, name="dn_rec_bwd", grid=(REC_GROUPS, nb), in_specs=[wide, half, wide, wide, egl_spec, wide, state, do_spec],
        out_specs=[wide, wide, half, wide, wide, egl_spec],
        out_shape=[big, big, _sds((N_HD, t, DN_CHUNK)), big, big, _sds((N_HD, nc, 1, LANES))],
        scratch_shapes=[pltpu.VMEM((REC_HEADS, DN_HEAD_DIM, DN_HEAD_DIM), F32)],
        compiler_params=_params(("parallel", "arbitrary")),
    )(w, qk, qd, kd, egl, vn, states, do)


def _post_fn(of, ob, z, gain):
    o = of + ob
    return o * lax.rsqrt(jnp.mean(o * o, axis=-1, keepdims=True) + RMS_EPS) * gain * _silu(z)


def _post_specs():
    o_spec = [pl.BlockSpec((1, ROW_TILE, DN_WIDTH), functools.partial(lambda i, d: (d, i, 0), d=d)) for d in (0, 1)]
    return o_spec, _cols(DN_WIDTH, C_Z), _rows(DN_WIDTH), _full((1, LANES))


def _post_fwd(o2, proj, gain):
    t = proj.shape[0]
    o_spec, z_spec, wide, gain_spec = _post_specs()

    def body(of_ref, ob_ref, z_ref, g_ref, out_ref):
        for h in range(DN_HEADS):
            cols = slice(h * LANES, (h + 1) * LANES)
            out_ref[:, cols] = _post_fn(of_ref[0, :, cols], ob_ref[0, :, cols], z_ref[:, cols], g_ref[...]).astype(_MXU)

    return pl.pallas_call(
        body, name="post_fwd", grid=(t // ROW_TILE,), in_specs=o_spec + [z_spec, gain_spec], out_specs=wide,
        out_shape=_sds((t, DN_WIDTH), _MXU), compiler_params=_params(("parallel",)),
    )(o2, o2, proj, gain)


def _post_bwd(o2, proj, gain, dout):
    t = proj.shape[0]
    o_spec, z_spec, wide, gain_spec = _post_specs()

    def body(of_ref, ob_ref, z_ref, g_ref, d_ref, do_ref, dz_ref, dg_ref):
        dg_sum = jnp.zeros((1, LANES), F32)
        for h in range(DN_HEADS):
            cols = slice(h * LANES, (h + 1) * LANES)
            _, vjp = jax.vjp(_post_fn, of_ref[0, :, cols], ob_ref[0, :, cols], z_ref[:, cols], g_ref[...])
            do, _, dz, dg = vjp(d_ref[:, cols])
            do_ref[:, cols] = do
            dz_ref[:, cols] = dz.astype(_MXU)
            dg_sum = dg_sum + dg
        _accumulate(dg_ref, dg_sum, pl.program_id(0))

    return pl.pallas_call(
        body, name="post_bwd", grid=(t // ROW_TILE,), in_specs=o_spec + [z_spec, gain_spec, wide],
        out_specs=[wide, wide, gain_spec], out_shape=[_sds((t, DN_WIDTH)), _sds((t, DN_WIDTH), _MXU), _sds((1, LANES))],
        compiler_params=_params(("arbitrary",)),
    )(o2, o2, proj, gain, dout)


def _rope(x, cos, sin):
    lane = lax.broadcasted_iota(jnp.int32, x.shape, 1)
    first = (lane & (SW_HEAD_DIM - 1)) < SW_HEAD_DIM // 2
    rot = jnp.where(first, -pltpu.roll(x, LANES - SW_HEAD_DIM // 2, 1), pltpu.roll(x, SW_HEAD_DIM // 2, 1))
    return x * cos + rot * sin


def _rope_apply(q, k, q_cols, k_cols, cos, sin, name, dtype):
    t = cos.shape[0]
    qw, kw = SW_HEADS * SW_HEAD_DIM, SW_KV_WIDTH

    def body(q_ref, k_ref, c_ref, s_ref, qo_ref, ko_ref):
        c, s = c_ref[...], s_ref[...]
        for j in range(qw // LANES):
            cols = slice(j * LANES, (j + 1) * LANES)
            qo_ref[:, cols] = _rope(q_ref[:, cols], c, s).astype(dtype)
        for j in range(kw // LANES):
            cols = slice(j * LANES, (j + 1) * LANES)
            ko_ref[:, cols] = _rope(k_ref[:, cols], c, s).astype(dtype)

    return pl.pallas_call(
        body, name=name, grid=(t // ROW_TILE,), in_specs=[_cols(qw, q_cols), _cols(kw, k_cols), _rows(LANES), _rows(LANES)],
        out_specs=[_rows(qw), _rows(kw)], out_shape=[_sds((t, qw), dtype), _sds((t, kw), dtype)],
        compiler_params=_params(("parallel",)),
    )(q, k, cos, sin)


def _attn_core(qs, kb, vb, sink, mask):
    s = _bnt(qs, kb) * (SW_HEAD_DIM ** -0.5)
    s = jnp.where(mask, s, -1e30)
    m = lax.stop_gradient(jnp.maximum(jnp.max(s, axis=1, keepdims=True), sink))
    e = jnp.exp(s - m)
    den = jnp.sum(e, axis=1, keepdims=True) + jnp.exp(sink - m)
    return _bnn(e, vb) / den


def _band_mask(n, nb):
    rows = SW_GROUP * SW_BLOCK
    i = lax.broadcasted_iota(jnp.int32, (rows, 3 * SW_BLOCK), 0) & (SW_BLOCK - 1)
    j = lax.broadcasted_iota(jnp.int32, (rows, 3 * SW_BLOCK), 1)
    near = (j - i >= 0) & (j - i <= 2 * SW_BLOCK)
    lo = jnp.where(n == 0, SW_BLOCK, 0)
    hi = jnp.where(n == nb - 1, 2 * SW_BLOCK, 3 * SW_BLOCK)
    return near & (j >= lo) & (j < hi)


def _band_specs(nb, v_cols):
    def spec(width, base, shift):
        return pl.BlockSpec((SW_BLOCK, width), lambda n: (jnp.clip(n + shift, 0, nb - 1), base // width))
    k_specs = [spec(SW_KV_WIDTH, 0, s) for s in (-1, 0, 1)]
    v_specs = [spec(SW_KV_WIDTH, v_cols, s) for s in (-1, 0, 1)]
    return k_specs, v_specs


def _head_cols(kv, g):
    h = kv * SW_GROUP + g
    return slice(h * SW_HEAD_DIM, (h + 1) * SW_HEAD_DIM)


def _attn_fwd(qr, kr, proj, sinks):
    t = qr.shape[0]
    nb = t // SW_BLOCK
    qw = SW_HEADS * SW_HEAD_DIM
    k_specs, v_specs = _band_specs(nb, C_VSW)
    q_spec = pl.BlockSpec((SW_BLOCK, qw), lambda n: (n, 0))

    def body(q_ref, k0, k1, k2, v0, v1, v2, s_ref, o_ref):
        mask = _band_mask(pl.program_id(0), nb)
        kb = jnp.concatenate([k0[...], k1[...], k2[...]], axis=0)
        vb = jnp.concatenate([v0[...], v1[...], v2[...]], axis=0)
        for kv in range(SW_KV_HEADS):
            kvc = slice(kv * SW_HEAD_DIM, (kv + 1) * SW_HEAD_DIM)
            qs = jnp.concatenate([q_ref[:, _head_cols(kv, g)] for g in range(SW_GROUP)], axis=0)
            sink = jnp.concatenate([jnp.broadcast_to(s_ref[kv * SW_GROUP + g], (SW_BLOCK, 1)) for g in range(SW_GROUP)], axis=0)
            o = _attn_core(qs, kb[:, kvc], vb[:, kvc], sink, mask)
            for g in range(SW_GROUP):
                o_ref[:, _head_cols(kv, g)] = o[g * SW_BLOCK:(g + 1) * SW_BLOCK, :].astype(_MXU)

    return pl.pallas_call(
        body, name="attn_fwd", grid=(nb,), in_specs=[q_spec] + k_specs + v_specs + [_full((SW_HEADS, 1, 1))],
        out_specs=q_spec, out_shape=_sds((t, qw), _MXU), compiler_params=_params(("parallel",)),
    )(qr, kr, kr, kr, proj, proj, proj, sinks)


def _attn_bwd(qr, kr, proj, sinks, do):
    t = qr.shape[0]
    nb = t // SW_BLOCK
    qw = SW_HEADS * SW_HEAD_DIM
    k_specs, v_specs = _band_specs(nb, C_VSW)
    q_spec = pl.BlockSpec((SW_BLOCK, qw), lambda n: (n, 0))
    part = pl.BlockSpec((1, 3 * SW_BLOCK, SW_KV_WIDTH), lambda n: (n, 0, 0))

    def body(q_ref, k0, k1, k2, v0, v1, v2, s_ref, do_ref, dq_ref, dk_ref, dv_ref, ds_ref):
        mask = _band_mask(pl.program_id(0), nb)
        kb = jnp.concatenate([k0[...], k1[...], k2[...]], axis=0).astype(F32)
        vb = jnp.concatenate([v0[...], v1[...], v2[...]], axis=0)

        @pl.when(pl.program_id(0) == 0)
        def _():
            ds_ref[...] = jnp.zeros_like(ds_ref)

        for kv in range(SW_KV_HEADS):
            kvc = slice(kv * SW_HEAD_DIM, (kv + 1) * SW_HEAD_DIM)
            qs = jnp.concatenate([q_ref[:, _head_cols(kv, g)] for g in range(SW_GROUP)], axis=0).astype(F32)
            dos = jnp.concatenate([do_ref[:, _head_cols(kv, g)] for g in range(SW_GROUP)], axis=0)
            sink = jnp.concatenate([jnp.broadcast_to(s_ref[kv * SW_GROUP + g], (SW_BLOCK, 1)) for g in range(SW_GROUP)], axis=0)
            _, vjp = jax.vjp(functools.partial(_attn_core, mask=mask), qs, kb[:, kvc], vb[:, kvc], sink)
            dqs, dkb, dvb, dsink = vjp(dos)
            dk_ref[0, :, kvc] = dkb
            dv_ref[0, :, kvc] = dvb
            for g in range(SW_GROUP):
                rows = slice(g * SW_BLOCK, (g + 1) * SW_BLOCK)
                dq_ref[:, _head_cols(kv, g)] = dqs[rows, :]
                ds_ref[kv * SW_GROUP + g] += jnp.sum(dsink[rows, :], axis=0, keepdims=True)

    parts = _sds((nb, 3 * SW_BLOCK, SW_KV_WIDTH))
    return pl.pallas_call(
        body, name="attn_bwd", grid=(nb,), in_specs=[q_spec] + k_specs + v_specs + [_full((SW_HEADS, 1, 1)), q_spec],
        out_specs=[q_spec, part, part, _full((SW_HEADS, 1, 1))], out_shape=[_sds((t, qw)), parts, parts, _sds((SW_HEADS, 1, 1))],
        compiler_params=_params(("arbitrary",)),
    )(qr, kr, kr, kr, proj, proj, proj, sinks, do)


def _band_sum(parts, name, dtype):
    nb = parts.shape[0]
    w = parts.shape[2]

    def spec(shift, slot):
        return pl.BlockSpec((1, SW_BLOCK, w), lambda m: (jnp.clip(m + shift, 0, nb - 1), slot, 0))

    def body(prev_ref, own_ref, next_ref, o_ref):
        m = pl.program_id(0)
        total = own_ref[0] + jnp.where(m > 0, prev_ref[0], 0.0) + jnp.where(m < nb - 1, next_ref[0], 0.0)
        o_ref[...] = total.astype(dtype)

    return pl.pallas_call(
        body, name=name, grid=(nb,), in_specs=[spec(-1, 2), spec(0, 1), spec(1, 0)],
        out_specs=pl.BlockSpec((SW_BLOCK, w), lambda m: (m, 0)), out_shape=_sds((nb * SW_BLOCK, w), dtype),
        compiler_params=_params(("parallel",)),
    )(parts, parts, parts)


def _gate_layouts(gbo):
    t = gbo.shape[0]
    beta_c = gbo[:, :N_HD].T.reshape(N_HD, t, 1)
    gc = gbo[:, N_HD:2 * N_HD].T
    return beta_c, gc.reshape(N_HD, t, 1), gc.reshape(N_HD, t // DN_CHUNK, 1, DN_CHUNK)


def _gate_layouts_t(dbeta_c, dgc_c, dgc_r):
    t = dbeta_c.shape[1]
    pad = jnp.zeros((t, LANES - 2 * N_HD), F32)
    none = jnp.zeros((t, N_HD), F32)
    d_a = jnp.concatenate([dbeta_c.reshape(N_HD, t).T, dgc_c.reshape(N_HD, t).T, pad], axis=1)
    d_b = jnp.concatenate([none, dgc_r.reshape(N_HD, t).T, pad], axis=1)
    return d_a, d_b


def _layer_fwd(x, xb, w, cos, sin):
    proj = _mm(xb, w["wm"], "nn", name="proj")
    ba = _mm(xb, w["wba"], "nn", name="proj_gates")
    qn, kn, vv = [_prep_fwd(proj, w["conv"], kind) for kind in range(3)]
    gbo = _gb_fwd(ba, w["alog"], w["dtb"])
    beta_c, gc_c, gc_r = _gate_layouts(gbo)
    u, wk, qk, qd, kd, egl, tinv = _dn_local_fwd(qn, kn, vv, beta_c, gc_c, gc_r)
    o2, vn, states = _dn_rec_fwd(u, wk, qk, qd, kd, egl)
    o_dn = _post_fwd(o2, proj, w["dnw"])
    qr, kr = _rope_apply(proj, proj, C_QSW, C_KSW, cos, sin, "rope_fwd", _MXU)
    o_sw = _attn_fwd(qr, kr, proj, w["sinks"])
    ya = _mm(o_dn, w["wa"], "nn", name="branch_a")
    yb = _mm(o_sw, w["wb"], "nn", name="branch_b")
    merged = _merge_fwd(proj, ya, yb)
    mix = _mm(merged, w["wo"], "nn", name="mix_out")
    x1, x1b = _ln_fwd(x, mix, w["ln1g"], w["ln1b"], "ln1_fwd")
    gu = _mm(x1b, w["wgu"], "nn", name="ffn_up")
    h = _swiglu_fwd(gu)
    f = _mm(h, w["wd"], "nn", name="ffn_down")
    x2, x2b = _ln_fwd(x1, f, w["ln2g"], w["ln2b"], "ln2_fwd")
    res = dict(x=x, xb=xb, proj=proj, ba=ba, qn=qn, kn=kn, vv=vv, gbo=gbo, wk=wk, qk=qk, qd=qd, kd=kd, egl=egl, tinv=tinv, vn=vn,
               states=states, o2=o2, o_dn=o_dn, qr=qr, kr=kr, o_sw=o_sw, ya=ya, yb=yb, merged=merged, mix=mix, x1=x1, x1b=x1b,
               gu=gu, h=h, f=f)
    return x2, x2b, res


def _layer_bwd(dx2, w, r, cos, sin):
    dx1, df, dln2g, dln2b = _ln_bwd(r["x1"], r["f"], w["ln2g"], w["ln2b"], dx2, "ln2_bwd")
    dh = _mm(df, w["wd"], "nt", name="d_ffn_hidden")
    dwd = _mm(r["h"], df, "tn", name="dw_ffn_down", out_dtype=_MXU)
    dgu = _swiglu_bwd(r["gu"], dh)
    dwgu = _mm(r["x1b"], dgu, "tn", name="dw_ffn_up", out_dtype=_MXU)
    dx1 = _mm(dgu, w["wgu"], "nt", name="dx_ffn", add=dx1)
    dx, dmix, dln1g, dln1b = _ln_bwd(r["x"], r["mix"], w["ln1g"], w["ln1b"], dx1, "ln1_bwd")
    dmerged = _mm(dmix, w["wo"], "nt", name="d_merged")
    dwo = _mm(r["merged"], dmix, "tn", name="dw_mix_out", out_dtype=_MXU)
    dga, dgb, dya, dyb = _merge_bwd(r["proj"], r["ya"], r["yb"], dmerged)
    dwa = _mm(r["o_dn"], dya, "tn", name="dw_branch_a", out_dtype=_MXU)
    do_dn = _mm(dya, w["wa"], "nt", name="d_branch_a")
    dwb = _mm(r["o_sw"], dyb, "tn", name="dw_branch_b", out_dtype=_MXU)
    do_sw = _mm(dyb, w["wb"], "nt", name="d_branch_b")
    do, dz, ddnw = _post_bwd(r["o2"], r["proj"], w["dnw"], do_dn)
    du, dwk, dqk, dqd, dkd, degl = _dn_rec_bwd(r["wk"], r["qk"], r["qd"], r["kd"], r["egl"], r["vn"], r["states"], do)
    beta_c, gc_c, gc_r = _gate_layouts(r["gbo"])
    dq3, dk3, dv3, dbeta_c, dgc_c, dgc_r = _dn_local_bwd(r["qn"], r["kn"], r["vv"], beta_c, gc_c, gc_r, r["tinv"],
                                                         du, dwk, dqk, dqd, dkd, degl)
    dqkv, dconv = zip(*[_prep_bwd(r["proj"], w["conv"], d2, kind) for kind, d2 in enumerate((dq3, dk3, dv3))])
    dconv = jnp.concatenate(dconv, axis=1)
    d_a, d_b = _gate_layouts_t(dbeta_c, dgc_c, dgc_r)
    dba, dalog, ddtb = _gb_bwd(r["ba"], w["alog"], w["dtb"], d_a, d_b)
    dqr, dkparts, dvparts, dsinks = _attn_bwd(r["qr"], r["kr"], r["proj"], w["sinks"], do_sw)
    dkr = _band_sum(dkparts, "attn_dk_sum", F32)
    dv = _band_sum(dvparts, "attn_dv_sum", _MXU)
    dq_sw, dk_sw = _rope_apply(dqr, dkr, 0, 0, cos, -sin, "rope_bwd", _MXU)
    dproj = jnp.concatenate([*dqkv, dz, dq_sw, dga, dgb, dk_sw, dv], axis=1)
    dwm = _mm(r["xb"], dproj, "tn", name="dw_proj", out_dtype=_MXU)
    dwba = _mm(r["xb"], dba, "tn", name="dw_proj_gates", out_dtype=_MXU)
    dx = _mm(dproj, w["wm"], "nt", name="dx_proj", add=dx)
    dx = _mm(dba, w["wba"], "nt", name="dx_proj_gates", add=dx)
    grads = dict(wm=dwm, wba=dwba, conv=dconv, alog=dalog, dtb=ddtb, dnw=ddnw, sinks=dsinks, wa=dwa, wb=dwb, wo=dwo,
                 ln1g=dln1g, ln1b=dln1b, wgu=dwgu, wd=dwd, ln2g=dln2g, ln2b=dln2b)
    return dx, grads


def _rope_tables(t):
    half = SW_HEAD_DIM // 2
    inv_freq = ROPE_THETA ** (-jnp.arange(half, dtype=F32) / half)
    ang = jnp.arange(t, dtype=F32)[:, None] * inv_freq[None, :]
    return jnp.tile(jnp.cos(ang), (1, LANES // half)), jnp.tile(jnp.sin(ang), (1, LANES // half))


def _trunk(x, target, layers):
    cos, sin = _rope_tables(x.shape[0])
    xb = x.astype(_MXU)
    saved = []
    for w in layers:
        x, xb, res = _layer_fwd(x, xb, w, cos, sin)
        saved.append(res)
    sq, dx = _loss_head(x, target)
    grads = [None] * len(layers)
    for i in reversed(range(len(layers))):
        dx, grads[i] = _layer_bwd(dx, layers[i], saved[i], cos, sin)
    return sq, dx, grads


N_CHIPS = 4


def _mesh_pos():
    return lax.axis_index("x"), lax.axis_index("y"), lax.axis_index("c")


def _other_chips(x, y):
    return [(1 - x, y), (x, 1 - y), (1 - x, 1 - y)]


def _remote_copy(src, dst, sems, k, to):
    send_sems, recv_sems = sems
    return pltpu.make_async_remote_copy(src_ref=src, dst_ref=dst, send_sem=send_sems.at[k], recv_sem=recv_sems.at[k],
                                        device_id=to, device_id_type=pl.DeviceIdType.MESH)


def _comm_call(body, name, out_shape, n_sems, *operands):
    return pl.pallas_call(
        body, name=name, in_specs=[pl.BlockSpec(memory_space=pl.ANY)] * len(operands),
        out_specs=pl.BlockSpec(memory_space=pl.ANY), out_shape=out_shape,
        scratch_shapes=[pltpu.SemaphoreType.DMA((n_sems,)), pltpu.SemaphoreType.DMA((n_sems,)), pltpu.SemaphoreType.DMA],
        compiler_params=pltpu.CompilerParams(has_side_effects=True),
    )(*operands)


def _all_gather(block, name):
    def body(x_ref, o_ref, send_sems, recv_sems, local_sem):
        x, y, c = _mesh_pos()
        sems = (send_sems, recv_sems)
        sibling = (x, y, 1 - c)
        chips = _other_chips(x, y)

        def slot(px, py, pc):
            return o_ref.at[4 * px + 2 * py + pc]

        mine = pltpu.make_async_copy(x_ref, slot(x, y, c), local_sem)
        mine.start()
        first = [_remote_copy(x_ref, slot(x, y, c), sems, 0, sibling)]
        first += [_remote_copy(x_ref, slot(x, y, c), sems, 1 + j, (*chip, c)) for j, chip in enumerate(chips)]
        for cp in first:
            cp.start()
        passed = [_remote_copy(slot(*chip, c), slot(*chip, c), sems, 4 + j, sibling) for j, chip in enumerate(chips)]
        for j, chip in enumerate(chips):
            _remote_copy(x_ref, slot(*chip, c), sems, 1 + j, sibling).wait_recv()
            passed[j].start()
        _remote_copy(x_ref, slot(x, y, 1 - c), sems, 0, sibling).wait_recv()
        for j, chip in enumerate(chips):
            _remote_copy(x_ref, slot(*chip, 1 - c), sems, 4 + j, sibling).wait_recv()
        for cp in first + passed:
            cp.wait_send()
        mine.wait()

    return _comm_call(body, name, _sds((N_DEV,) + block.shape, block.dtype), N_DEV - 1, block)


def _sibling_swap(parts, name):
    def body(x_ref, o_ref, send_sems, recv_sems, local_sem):
        x, y, c = _mesh_pos()
        sems = (send_sems, recv_sems)
        sibling = (x, y, 1 - c)
        copies = [_remote_copy(x_ref.at[2 * q + (1 - c)], o_ref.at[q], sems, q, sibling) for q in range(N_CHIPS)]
        for cp in copies:
            cp.start()
        for cp in copies:
            cp.wait()

    return _comm_call(body, name, _sds((N_CHIPS,) + parts.shape[1:], parts.dtype), N_CHIPS, parts)


def _chip_exchange(parts, name):
    def body(x_ref, o_ref, send_sems, recv_sems, local_sem):
        x, y, c = _mesh_pos()
        sems = (send_sems, recv_sems)
        me = 2 * x + y
        mine = pltpu.make_async_copy(x_ref.at[me], o_ref.at[me], local_sem)
        mine.start()
        copies = [_remote_copy(x_ref.at[2 * cx + cy], o_ref.at[me], sems, j, (cx, cy, c))
                  for j, (cx, cy) in enumerate(_other_chips(x, y))]
        for cp in copies:
            cp.start()
        for j, (cx, cy) in enumerate(_other_chips(x, y)):
            _remote_copy(x_ref.at[me], o_ref.at[2 * cx + cy], sems, j, (cx, cy, c)).wait_recv()
        for cp in copies:
            cp.wait_send()
        mine.wait()

    return _comm_call(body, name, _sds(parts.shape, parts.dtype), N_CHIPS - 1, parts)


def _pair_sum(a, b, name):
    n, rows, cols = a.shape
    tr = rows if rows <= 512 else _row_tile(rows)
    blk = pl.BlockSpec((1, tr, cols), lambda q, i: (q, i, 0))

    def body(a_ref, b_ref, o_ref):
        o_ref[...] = (a_ref[...].astype(F32) + b_ref[...].astype(F32)).astype(o_ref.dtype)

    return pl.pallas_call(
        body, name=name, grid=(n, rows // tr), in_specs=[blk, blk], out_specs=blk, out_shape=_sds(a.shape, a.dtype),
        compiler_params=_params(("parallel", "parallel")),
    )(a, b)


def _reduce_to_owner(parts, name):
    c = lax.axis_index("c")
    from_sibling = _sibling_swap(parts, "swap_" + name)
    own = lax.dynamic_index_in_dim(parts.reshape((N_CHIPS, 2) + parts.shape[1:]), c, axis=1, keepdims=False)
    return _chip_exchange(_pair_sum(own, from_sibling, "pair_sum_" + name), "exchange_" + name)


def _sum_adamw(parts, w, m, v, name):
    rows, cols = w.shape
    n_parts = parts.shape[0]
    tr = rows if rows <= 512 else _row_tile(rows)
    blk = pl.BlockSpec((tr, cols), lambda i: (i, 0))

    def body(p_ref, w_ref, m_ref, v_ref, g_ref, d_ref, nm_ref, nv_ref):
        g = p_ref[0].astype(F32)
        for i in range(1, n_parts):
            g = g + p_ref[i].astype(F32)
        nm = ADAM_B1 * m_ref[...] + (1.0 - ADAM_B1) * g
        nv = ADAM_B2 * v_ref[...] + (1.0 - ADAM_B2) * jnp.square(g)
        m_hat = nm / (1.0 - ADAM_B1 ** ADAM_STEP)
        v_hat = nv / (1.0 - ADAM_B2 ** ADAM_STEP)
        g_ref[...] = g
        d_ref[...] = -ADAM_LR * (m_hat / (jnp.sqrt(v_hat) + ADAM_EPS) + ADAM_WD * w_ref[...])
        nm_ref[...] = nm
        nv_ref[...] = nv

    return pl.pallas_call(
        body, name=name, grid=(rows // tr,), in_specs=[pl.BlockSpec((n_parts, tr, cols), lambda i: (0, i, 0)), blk, blk, blk],
        out_specs=[blk] * 4, out_shape=[_sds((rows, cols))] * 4, compiler_params=_params(("parallel",)),
    )(parts, w, m, v)


def _row_tile(rows):
    for t in (256, 128, 64, 32, 16, 8):
        if rows % t == 0:
            return t
    return rows


def _gathered_cols(g):
    n, l, rows, c = g.shape
    return jnp.transpose(g, (1, 2, 0, 3)).reshape(l, rows, n * c)


def _gathered_rows(g):
    n, l, rows, c = g.shape
    return jnp.transpose(g, (1, 0, 2, 3)).reshape(l, n * rows, c)


def _col_parts(full):
    l, rows, c = full.shape
    return jnp.transpose(full.reshape(l, rows, N_DEV, c // N_DEV), (2, 0, 1, 3))


def _row_parts(full):
    l, rows, c = full.shape
    return jnp.transpose(full.reshape(l, N_DEV, rows // N_DEV, c), (1, 0, 2, 3))


def _w_in_split(w_in):
    s = lambda a, n: w_in[..., a:a + n]
    main = jnp.concatenate([s(R_QKV, 3072), s(R_Z, 1024), s(R_QSW, 1024), s(R_G, 2048), s(R_KSW, 256), s(R_VSW, 256)], axis=-1)
    gates = jnp.pad(s(R_BA, 2 * N_HD), ((0, 0), (0, 0), (0, LANES - 2 * N_HD)))
    return main, gates


def _w_in_join(dmain, dgates):
    s = lambda a, n: dmain[..., a:a + n]
    return jnp.concatenate([s(C_QKV, 3072), s(C_Z, 1024), dgates[..., :2 * N_HD], s(C_QSW, 1024), s(C_KSW, 256), s(C_VSW, 256),
                            s(C_GA, 2048)], axis=-1)


def _lane_row(a, offset):
    l, n = a.shape
    return jnp.pad(a, ((0, 0), (offset, LANES - offset - n)))[:, None, :]


def kernel(x, w_in, conv_w, a_log, dt_bias, dn_norm_w, sinks, w_branch_a, w_branch_b, w_out, ln1_g, ln1_b, w_gate_up, w_down, ln2_g, ln2_b, loss_target, m_w_in, m_conv_w, m_a_log, m_dt_bias, m_dn_norm_w, m_sinks, m_w_branch_a, m_w_branch_b, m_w_out, m_ln1_g, m_ln1_b, m_w_gate_up, m_w_down, m_ln2_g, m_ln2_b, v_w_in, v_conv_w, v_a_log, v_dt_bias, v_dn_norm_w, v_sinks, v_w_branch_a, v_w_branch_b, v_w_out, v_ln1_g, v_ln1_b, v_w_gate_up, v_w_down, v_ln2_g, v_ln2_b):
    l = DEPTH
    bf = lambda a: a.astype(_MXU)
    w_in_full = _gathered_cols(_all_gather(bf(w_in), "gather_w_in"))
    wgu_full = _gathered_cols(_all_gather(bf(w_gate_up), "gather_w_gate_up"))
    wa_full = _gathered_rows(_all_gather(bf(w_branch_a), "gather_w_branch_a"))
    wb_full = _gathered_rows(_all_gather(bf(w_branch_b), "gather_w_branch_b"))
    wo_full = _gathered_rows(_all_gather(bf(w_out), "gather_w_out"))
    wd_full = _gathered_rows(_all_gather(bf(w_down), "gather_w_down"))
    conv_full = _gathered_cols(_all_gather(conv_w, "gather_conv_w"))
    wm, wba = _w_in_split(w_in_full)
    row = lambda a: a[:, None, :]
    stacked = dict(
        wm=wm, wba=wba, conv=jnp.pad(conv_full, ((0, 0), (0, 8 - DN_CONV), (0, 0))),
        alog=_lane_row(a_log.reshape(l, N_HD), N_HD), dtb=_lane_row(dt_bias.reshape(l, N_HD), N_HD), dnw=row(dn_norm_w),
        sinks=sinks.reshape(l, SW_HEADS, 1, 1), wa=wa_full, wb=wb_full, wo=wo_full, ln1g=row(ln1_g), ln1b=row(ln1_b),
        wgu=wgu_full, wd=wd_full, ln2g=row(ln2_g), ln2b=row(ln2_b))

    sq, dx, grads = _trunk(x[0], loss_target[0], [{k: a[i] for k, a in stacked.items()} for i in range(l)])
    g = {k: jnp.stack([gi[k] for gi in grads]) for k in grads[0]}
    loss = lax.psum(0.5 * sq[0, 0] / D_MODEL, ("x", "y", "c"))

    def big(parts, w, m, v, name):
        rows = w.shape[0] * w.shape[1]
        flat = lambda a: a.reshape(rows, a.shape[-1])
        got = _reduce_to_owner(parts.reshape(N_DEV, rows, w.shape[-1]), name)
        outs = _sum_adamw(got, flat(w), flat(m), flat(v), "adamw_" + name)
        return [o.reshape(w.shape) for o in outs]

    dconv = g["conv"][:, :DN_CONV, :]
    results = {
        "w_in": big(_col_parts(_w_in_join(g["wm"], g["wba"])), w_in, m_w_in, v_w_in, "w_in"),
        "conv_w": big(_col_parts(dconv), conv_w, m_conv_w, v_conv_w, "conv_w"),
        "w_branch_a": big(_row_parts(g["wa"]), w_branch_a, m_w_branch_a, v_w_branch_a, "w_branch_a"),
        "w_branch_b": big(_row_parts(g["wb"]), w_branch_b, m_w_branch_b, v_w_branch_b, "w_branch_b"),
        "w_out": big(_row_parts(g["wo"]), w_out, m_w_out, v_w_out, "w_out"),
        "w_gate_up": big(_col_parts(g["wgu"]), w_gate_up, m_w_gate_up, v_w_gate_up, "w_gate_up"),
        "w_down": big(_row_parts(g["wd"]), w_down, m_w_down, v_w_down, "w_down"),
    }

    small_w = {"a_log": a_log.reshape(l, N_HD), "dt_bias": dt_bias.reshape(l, N_HD), "dn_norm_w": dn_norm_w, "sinks": sinks,
               "ln1_g": ln1_g, "ln1_b": ln1_b, "ln2_g": ln2_g, "ln2_b": ln2_b}
    small_m = {"a_log": m_a_log, "dt_bias": m_dt_bias, "dn_norm_w": m_dn_norm_w, "sinks": m_sinks, "ln1_g": m_ln1_g,
               "ln1_b": m_ln1_b, "ln2_g": m_ln2_g, "ln2_b": m_ln2_b}
    small_v = {"a_log": v_a_log, "dt_bias": v_dt_bias, "dn_norm_w": v_dn_norm_w, "sinks": v_sinks, "ln1_g": v_ln1_g,
               "ln1_b": v_ln1_b, "ln2_g": v_ln2_g, "ln2_b": v_ln2_b}
    small_g = {"a_log": g["alog"][:, 0, N_HD:2 * N_HD], "dt_bias": g["dtb"][:, 0, N_HD:2 * N_HD], "dn_norm_w": g["dnw"][:, 0, :],
               "sinks": g["sinks"].reshape(l, SW_HEADS), "ln1_g": g["ln1g"][:, 0, :], "ln1_b": g["ln1b"][:, 0, :],
               "ln2_g": g["ln2g"][:, 0, :], "ln2_b": g["ln2b"][:, 0, :]}
    names = list(small_w)
    cat = lambda d: jnp.concatenate([d[n].reshape(l, -1) for n in names], axis=1)
    widths = [small_w[n].shape[1] for n in names]
    total = sum(widths)
    padded = -(-total // LANES) * LANES
    pad = lambda a: jnp.pad(a, ((0, 8 - l), (0, padded - total)))
    got = _all_gather(pad(cat(small_g)), "gather_small_grads")
    outs = _sum_adamw(got, pad(cat(small_w)), pad(cat({n: small_m[n].reshape(l, -1) for n in names})),
                      pad(cat({n: small_v[n].reshape(l, -1) for n in names})), "adamw_small")
    off = 0
    for n, wd_ in zip(names, widths):
        shape = {"a_log": a_log.shape, "dt_bias": dt_bias.shape}.get(n, small_w[n].shape)
        results[n] = [o[:l, off:off + wd_].reshape(shape) for o in outs]
        off += wd_

    order = ["w_in", "conv_w", "a_log", "dt_bias", "dn_norm_w", "sinks", "w_branch_a", "w_branch_b", "w_out", "ln1_g", "ln1_b",
             "w_gate_up", "w_down", "ln2_g", "ln2_b"]
    return (loss, dx[None], *[results[n][0] for n in order], *[results[n][1] for n in order],
            *[results[n][2] for n in order], *[results[n][3] for n in order])
```

```python
import functools

import jax
import jax.numpy as jnp
from jax import lax
from jax.experimental import pallas as pl
from jax.experimental.pallas import tpu as pltpu

F32 = jnp.float32
_MXU = jnp.bfloat16
_HI = lax.Precision.HIGHEST
_MID = lax.Precision.HIGH

N_DEV = 8
D_MODEL = 1024
DEPTH = 4
DN_HEADS = 8
DN_HEAD_DIM = 128
DN_WIDTH = DN_HEADS * DN_HEAD_DIM
DN_CONV = 5
DN_CHUNK = 64
SW_HEADS = 16
SW_KV_HEADS = 4
SW_HEAD_DIM = 64
SW_GROUP = SW_HEADS // SW_KV_HEADS
SW_BLOCK = 128
SW_KV_WIDTH = SW_KV_HEADS * SW_HEAD_DIM
ROPE_THETA = 10000.0
FFN_HIDDEN = 2816
DN_ALPHA = (2.0 * DEPTH) ** 0.25
LN_EPS = 1e-5
RMS_EPS = 1e-6
ADAM_LR = 0.001
ADAM_B1 = 0.9
ADAM_B2 = 0.999
ADAM_EPS = 1e-08
ADAM_WD = 0.01
ADAM_STEP = 10

LANES = 128
N_HD = 2 * DN_HEADS
DN_GROUP = 4 * DN_CHUNK
INV_SUB = 16
LOCAL_ROWS = 512
REC_ROWS = 256
ROW_TILE = 256
VMEM_LIMIT = 48 << 20

C_QKV, C_Z, C_QSW, C_GA, C_GB, C_KSW, C_VSW = 0, 3072, 4096, 5120, 6144, 7168, 7424
MAIN_COLS = 7680
R_QKV, R_Z, R_BA, R_QSW, R_KSW, R_VSW, R_G = 0, 3072, 4096, 4128, 5152, 5408, 5664
IN_COLS = 7712


_NN = ((1,), (0,))
_NT = ((1,), (1,))
_TN = ((0,), (0,))


def _dg(a, b, dims, precision):
    if precision is not None:
        return lax.dot_general(a, b, (dims, ((), ())), precision=precision, preferred_element_type=F32)
    return lax.dot_general(a.astype(_MXU), b.astype(_MXU), (dims, ((), ())), preferred_element_type=F32)


def _make_dots(hi):
    @jax.custom_vjp
    def nn(a, b):
        return _dg(a, b, _NN, hi)

    @jax.custom_vjp
    def nt(a, b):
        return _dg(a, b, _NT, hi)

    @jax.custom_vjp
    def tn(a, b):
        return _dg(a, b, _TN, hi)

    nn.defvjp(lambda a, b: (nn(a, b), (a, b)), lambda r, g: (nt(g, r[1]), tn(r[0], g)))
    nt.defvjp(lambda a, b: (nt(a, b), (a, b)), lambda r, g: (nn(g, r[1]), tn(g, r[0])))
    tn.defvjp(lambda a, b: (tn(a, b), (a, b)), lambda r, g: (nt(r[1], g), nn(r[0], g)))
    return nn, nt, tn


_bnn, _bnt, _btn = _make_dots(None)
_hnn, _hnt, _htn = _make_dots(_HI)
_mnn, _mnt, _mtn = _make_dots(_MID)


def _neumann(a, order):
    n = a.shape[0]
    eye = (lax.broadcasted_iota(jnp.int32, (n, n), 0) == lax.broadcasted_iota(jnp.int32, (n, n), 1)).astype(F32)
    inv = eye - a
    p = a
    span = 2
    while span < order:
        p = _mnn(p, p)
        inv = inv + _mnn(inv, p)
        span *= 2
    return inv


def _inv_unit(a, order):
    n = a.shape[0]
    ii = lax.broadcasted_iota(jnp.int32, (n, n), 0)
    jj = lax.broadcasted_iota(jnp.int32, (n, n), 1)
    near = (ii & -INV_SUB) == (jj & -INV_SUB)
    d_inv = _neumann(jnp.where(near, a, 0.0), INV_SUB)
    outer = _neumann(_mnn(d_inv, jnp.where(near, 0.0, a)), order // INV_SUB)
    return _mnn(outer, d_inv)


def _inv_unit_t(t, g):
    return -_mnt(_mtn(t, g), t)


def _silu(x):
    return x * jax.nn.sigmoid(x)


def _softplus(x):
    return jnp.maximum(x, 0.0) + jnp.log1p(jnp.exp(-jnp.abs(x)))


def _params(sem=None):
    kw = {"vmem_limit_bytes": VMEM_LIMIT}
    if sem is not None:
        kw["dimension_semantics"] = sem
    return pltpu.CompilerParams(**kw)


def _tile(dim, pref):
    if dim <= pref:
        return dim
    t = (pref // LANES) * LANES
    while t > LANES and dim % t:
        t -= LANES
    assert dim % t == 0, (dim, pref)
    return t


def _full(shape):
    zeros = (0,) * len(shape)
    return pl.BlockSpec(shape, lambda *_: zeros)


def _sds(shape, dtype=F32):
    return jax.ShapeDtypeStruct(shape, dtype)


def _mm(a, b, mode, *, name, add=None, tm=1536, tn=1536, tk=1536, out_dtype=F32):
    if mode == "nn":
        (m, k), (k2, n) = a.shape, b.shape
    elif mode == "nt":
        (m, k), (n, k2) = a.shape, b.shape
    else:
        (k, m), (k2, n) = a.shape, b.shape
    assert k == k2, (a.shape, b.shape, mode)
    tm, tn, tk = _tile(m, tm), _tile(n, tn), _tile(k, tk)
    nk = k // tk
    dims = {"nn": _NN, "nt": _NT, "tn": _TN}[mode]

    def body(*refs):
        if add is None:
            a_ref, b_ref, o_ref, acc = refs
        else:
            a_ref, b_ref, add_ref, o_ref, acc = refs
        kk = pl.program_id(2)

        @pl.when(kk == 0)
        def _():
            acc[...] = jnp.zeros_like(acc)

        acc[...] += _dg(a_ref[...], b_ref[...], dims, None)

        @pl.when(kk == nk - 1)
        def _():
            o_ref[...] = (acc[...] if add is None else acc[...] + add_ref[...]).astype(out_dtype)

    a_spec = pl.BlockSpec((tk, tm), lambda i, j, kk: (kk, i)) if mode == "tn" else pl.BlockSpec((tm, tk), lambda i, j, kk: (i, kk))
    b_spec = pl.BlockSpec((tn, tk), lambda i, j, kk: (j, kk)) if mode == "nt" else pl.BlockSpec((tk, tn), lambda i, j, kk: (kk, j))
    o_spec = pl.BlockSpec((tm, tn), lambda i, j, kk: (i, j))
    ins, specs = [a, b], [a_spec, b_spec]
    if add is not None:
        ins.append(add)
        specs.append(o_spec)
    return pl.pallas_call(
        body, name=name, grid=(m // tm, n // tn, nk), in_specs=specs, out_specs=o_spec,
        out_shape=_sds((m, n), out_dtype), scratch_shapes=[pltpu.VMEM((tm, tn), F32)],
        compiler_params=_params(("parallel", "parallel", "arbitrary")),
    )(*ins)


def _cols(width, start):
    assert start % width == 0
    return pl.BlockSpec((ROW_TILE, width), lambda i: (i, start // width))


def _rows(width):
    return pl.BlockSpec((ROW_TILE, width), lambda i: (i, 0))


def _accumulate(ref, value, step):
    @pl.when(step == 0)
    def _():
        ref[...] = value

    @pl.when(step != 0)
    def _():
        ref[...] += value


def _ln_fn(x, r, g, b):
    u = DN_ALPHA * x + r
    mu = jnp.mean(u, axis=-1, keepdims=True)
    var = jnp.mean(jnp.square(u - mu), axis=-1, keepdims=True)
    return (u - mu) * lax.rsqrt(var + LN_EPS) * g + b


def _ln_fwd(x, r, g, b, name):
    t, d = x.shape

    def body(x_ref, r_ref, g_ref, b_ref, o_ref, ob_ref):
        y = _ln_fn(x_ref[...], r_ref[...], g_ref[...], b_ref[...])
        o_ref[...] = y
        ob_ref[...] = y.astype(_MXU)

    return pl.pallas_call(
        body, name=name, grid=(t // ROW_TILE,), in_specs=[_rows(d), _rows(d), _full((1, d)), _full((1, d))],
        out_specs=[_rows(d), _rows(d)], out_shape=[_sds((t, d)), _sds((t, d), _MXU)], compiler_params=_params(("parallel",)),
    )(x, r, g, b)


def _ln_bwd(x, r, g, b, dy, name):
    t, d = x.shape

    def body(x_ref, r_ref, g_ref, b_ref, dy_ref, dx_ref, dr_ref, dg_ref, db_ref):
        _, vjp = jax.vjp(_ln_fn, x_ref[...], r_ref[...], g_ref[...], b_ref[...])
        dx, dr, dg, db = vjp(dy_ref[...])
        dx_ref[...] = dx
        dr_ref[...] = dr.astype(_MXU)
        _accumulate(dg_ref, dg, pl.program_id(0))
        _accumulate(db_ref, db, pl.program_id(0))

    return pl.pallas_call(
        body, name=name, grid=(t // ROW_TILE,),
        in_specs=[_rows(d), _rows(d), _full((1, d)), _full((1, d)), _rows(d)],
        out_specs=[_rows(d), _rows(d), _full((1, d)), _full((1, d))],
        out_shape=[_sds((t, d)), _sds((t, d), _MXU), _sds((1, d)), _sds((1, d))],
        compiler_params=_params(("arbitrary",)),
    )(x, r, g, b, dy)


def _merge_fn(ga, gb, ya, yb):
    return jax.nn.sigmoid(ga) * ya + jax.nn.sigmoid(gb) * yb


def _merge_fwd(proj, ya, yb):
    t, d = ya.shape

    def body(ga_ref, gb_ref, ya_ref, yb_ref, o_ref):
        o_ref[...] = _merge_fn(ga_ref[...], gb_ref[...], ya_ref[...], yb_ref[...]).astype(_MXU)

    return pl.pallas_call(
        body, name="merge_fwd", grid=(t // ROW_TILE,), in_specs=[_cols(d, C_GA), _cols(d, C_GB), _rows(d), _rows(d)],
        out_specs=_rows(d), out_shape=_sds((t, d), _MXU), compiler_params=_params(("parallel",)),
    )(proj, proj, ya, yb)


def _merge_bwd(proj, ya, yb, dm):
    t, d = ya.shape

    def body(ga_ref, gb_ref, ya_ref, yb_ref, dm_ref, dga_ref, dgb_ref, dya_ref, dyb_ref):
        _, vjp = jax.vjp(_merge_fn, ga_ref[...], gb_ref[...], ya_ref[...], yb_ref[...])
        dga_ref[...], dgb_ref[...], dya_ref[...], dyb_ref[...] = [g.astype(_MXU) for g in vjp(dm_ref[...])]

    return pl.pallas_call(
        body, name="merge_bwd", grid=(t // ROW_TILE,),
        in_specs=[_cols(d, C_GA), _cols(d, C_GB), _rows(d), _rows(d), _rows(d)],
        out_specs=[_rows(d)] * 4, out_shape=[_sds((t, d), _MXU)] * 4, compiler_params=_params(("parallel",)),
    )(proj, proj, ya, yb, dm)


def _swiglu_fn(gate, up):
    return _silu(gate) * up


def _swiglu_fwd(gu):
    t = gu.shape[0]
    f = FFN_HIDDEN
    rows = 128

    def body(gu_ref, o_ref):
        o_ref[...] = _swiglu_fn(gu_ref[:, :f], gu_ref[:, f:]).astype(_MXU)

    return pl.pallas_call(
        body, name="swiglu_fwd", grid=(t // rows,), in_specs=[pl.BlockSpec((rows, 2 * f), lambda i: (i, 0))],
        out_specs=pl.BlockSpec((rows, f), lambda i: (i, 0)), out_shape=_sds((t, f), _MXU), compiler_params=_params(("parallel",)),
    )(gu)


def _swiglu_bwd(gu, dh):
    t = gu.shape[0]
    f = FFN_HIDDEN
    rows = 128

    def body(gu_ref, dh_ref, o_ref):
        _, vjp = jax.vjp(_swiglu_fn, gu_ref[:, :f], gu_ref[:, f:])
        o_ref[:, :f], o_ref[:, f:] = [g.astype(_MXU) for g in vjp(dh_ref[...])]

    return pl.pallas_call(
        body, name="swiglu_bwd", grid=(t // rows,),
        in_specs=[pl.BlockSpec((rows, 2 * f), lambda i: (i, 0)), pl.BlockSpec((rows, f), lambda i: (i, 0))],
        out_specs=pl.BlockSpec((rows, 2 * f), lambda i: (i, 0)), out_shape=_sds((t, 2 * f), _MXU),
        compiler_params=_params(("parallel",)),
    )(gu, dh)


def _loss_head(y, target):
    t, d = y.shape

    def body(y_ref, t_ref, s_ref, dy_ref):
        err = y_ref[...] - t_ref[...]
        dy_ref[...] = err / d
        _accumulate(s_ref, jnp.broadcast_to(jnp.sum(jnp.square(err)), (1, LANES)), pl.program_id(0))

    return pl.pallas_call(
        body, name="loss_head", grid=(t // ROW_TILE,), in_specs=[_rows(d), _rows(d)],
        out_specs=[_full((1, LANES)), _rows(d)], out_shape=[_sds((1, LANES)), _sds((t, d))],
        compiler_params=_params(("arbitrary",)),
    )(y, target)


def _shift_rows(x, s):
    if s == 0:
        return x
    t = x.shape[0]
    rolled = pltpu.roll(x, (-s) % t, 0)
    row = lax.broadcasted_iota(jnp.int32, x.shape, 0)
    return jnp.where((row + s >= 0) & (row + s < t), rolled, 0.0)


def _conv(x, w):
    half = DN_CONV // 2
    acc = None
    for k in range(DN_CONV):
        term = _shift_rows(x, k - half) * w[k:k + 1, :]
        acc = term if acc is None else acc + term
    return acc


def _act_norm(c, do_norm, scale):
    a = _silu(c)
    if not do_norm:
        return a
    return a * lax.rsqrt(jnp.sum(a * a, axis=-1, keepdims=True) + RMS_EPS) * scale


PREP_ROWS = 512
HALO = 8
_KINDS = ((True, DN_HEAD_DIM ** -0.5), (True, 1.0), (False, 1.0))


def _halo_rows(read, i, pr, t):
    lo, hi = i * pr - HALO, (i + 1) * pr + HALO
    parts = []
    if lo < 0:
        parts.append(jnp.zeros((HALO, LANES), F32))
    parts.append(read(max(lo, 0), min(hi, t)))
    if hi > t:
        parts.append(jnp.zeros((HALO, LANES), F32))
    return jnp.concatenate(parts, axis=0) if len(parts) > 1 else parts[0]


def _prep_fwd(proj, conv_w, kind):
    t = proj.shape[0]
    pr = min(PREP_ROWS, t)
    do_norm, scale = _KINDS[kind]
    blk = pl.BlockSpec((t, LANES), lambda j: (0, kind * DN_HEADS + j))

    def body(x_ref, w_ref, o_ref):
        w = w_ref[...]
        for i in range(t // pr):
            xx = _halo_rows(lambda lo, hi: x_ref[lo:hi, :], i, pr, t)
            c = _conv(xx, w)[HALO:HALO + pr, :]
            o_ref[i * pr:(i + 1) * pr, :] = _act_norm(c, do_norm, scale)

    return pl.pallas_call(
        body, name=f"prep_fwd_{kind}", grid=(DN_HEADS,),
        in_specs=[blk, pl.BlockSpec((8, LANES), lambda j: (0, kind * DN_HEADS + j))],
        out_specs=pl.BlockSpec((t, LANES), lambda j: (0, j)), out_shape=_sds((t, DN_WIDTH)),
        compiler_params=_params(("parallel",)),
    )(proj, conv_w)


def _prep_bwd(proj, conv_w, d2, kind):
    t = proj.shape[0]
    pr = min(PREP_ROWS, t)
    do_norm, scale = _KINDS[kind]
    half = DN_CONV // 2
    blk = pl.BlockSpec((t, LANES), lambda j: (0, kind * DN_HEADS + j))
    oblk = pl.BlockSpec((t, LANES), lambda j: (0, j))

    def body(x_ref, w_ref, d_ref, dx_ref, dw_ref):
        w = w_ref[...]
        own = slice(HALO, HALO + pr)
        dw = jnp.zeros((8, LANES), F32)
        for i in range(t // pr):
            xx = _halo_rows(lambda lo, hi: x_ref[lo:hi, :], i, pr, t)
            dn = _halo_rows(lambda lo, hi: d_ref[0, lo:hi, :] + d_ref[1, lo:hi, :], i, pr, t)
            _, vjp = jax.vjp(lambda c: _act_norm(c, do_norm, scale), _conv(xx, w))
            (dc,) = vjp(dn)
            dx = None
            rows = []
            for k in range(DN_CONV):
                term = _shift_rows(dc, half - k) * w[k:k + 1, :]
                dx = term if dx is None else dx + term
                rows.append(jnp.sum(dc[own, :] * _shift_rows(xx, k - half)[own, :], axis=0, keepdims=True))
            dx_ref[i * pr:(i + 1) * pr, :] = dx[own, :].astype(_MXU)
            dw = dw + jnp.concatenate(rows + [jnp.zeros((8 - DN_CONV, LANES), F32)], axis=0)
        dw_ref[...] = dw

    return pl.pallas_call(
        body, name=f"prep_bwd_{kind}", grid=(DN_HEADS,),
        in_specs=[blk, pl.BlockSpec((8, LANES), lambda j: (0, kind * DN_HEADS + j)), pl.BlockSpec((2, t, LANES), lambda j: (0, 0, j))],
        out_specs=[oblk, pl.BlockSpec((8, LANES), lambda j: (0, j))], out_shape=[_sds((t, DN_WIDTH), _MXU), _sds((8, DN_WIDTH))],
        compiler_params=_params(("parallel",)),
    )(proj, conv_w, d2)


def _gb_fn(ba, alog_row, dtb_row):
    c = DN_CHUNK
    lane = lax.broadcasted_iota(jnp.int32, (c, LANES), 1)
    ii = lax.broadcasted_iota(jnp.int32, (c, c), 0)
    jj = lax.broadcasted_iota(jnp.int32, (c, c), 1)
    beta = jax.nn.sigmoid(ba)
    g = -jnp.exp(alog_row) * _softplus(ba + dtb_row)
    g = jnp.where((lane >= N_HD) & (lane < 2 * N_HD), g, 0.0)
    gc_fwd = _hnn((ii >= jj).astype(F32), g)
    gc_rev = _hnn((ii <= jj).astype(F32), g)
    gc = jnp.where(lane < N_HD + DN_HEADS, gc_fwd, gc_rev)
    return jnp.where(lane < N_HD, beta, jnp.where(lane < 2 * N_HD, gc, 0.0))


def _gb_fwd(ba, alog_row, dtb_row):
    t = ba.shape[0]
    n = ROW_TILE // DN_CHUNK

    def body(ba_ref, a_ref, d_ref, o_ref):
        for c in range(n):
            rows = slice(c * DN_CHUNK, (c + 1) * DN_CHUNK)
            o_ref[rows, :] = _gb_fn(ba_ref[rows, :], a_ref[...], d_ref[...])

    return pl.pallas_call(
        body, name="gates_fwd", grid=(t // ROW_TILE,), in_specs=[_rows(LANES), _full((1, LANES)), _full((1, LANES))],
        out_specs=_rows(LANES), out_shape=_sds((t, LANES)), compiler_params=_params(("parallel",)),
    )(ba, alog_row, dtb_row)


def _gb_bwd(ba, alog_row, dtb_row, d_a, d_b):
    t = ba.shape[0]
    n = ROW_TILE // DN_CHUNK

    def body(ba_ref, a_ref, d_ref, da_ref, db_ref, dba_ref, dal_ref, ddt_ref):
        dal = jnp.zeros((1, LANES), F32)
        ddt = jnp.zeros((1, LANES), F32)
        for c in range(n):
            rows = slice(c * DN_CHUNK, (c + 1) * DN_CHUNK)
            _, vjp = jax.vjp(_gb_fn, ba_ref[rows, :], a_ref[...], d_ref[...])
            dba, da, dd = vjp(da_ref[rows, :] + db_ref[rows, :])
            dba_ref[rows, :] = dba.astype(_MXU)
            dal = dal + da
            ddt = ddt + dd
        _accumulate(dal_ref, dal, pl.program_id(0))
        _accumulate(ddt_ref, ddt, pl.program_id(0))

    return pl.pallas_call(
        body, name="gates_bwd", grid=(t // ROW_TILE,),
        in_specs=[_rows(LANES), _full((1, LANES)), _full((1, LANES)), _rows(LANES), _rows(LANES)],
        out_specs=[_rows(LANES), _full((1, LANES)), _full((1, LANES))],
        out_shape=[_sds((t, LANES), _MXU), _sds((1, LANES)), _sds((1, LANES))], compiler_params=_params(("arbitrary",)),
    )(ba, alog_row, dtb_row, d_a, d_b)


def _dn_decay(gcc, gcr, sgn):
    c = DN_CHUNK
    ii = lax.broadcasted_iota(jnp.int32, (c, c), 0)
    jj = lax.broadcasted_iota(jnp.int32, (c, c), 1)
    d = (ii - jj) * sgn
    lower = d >= 0
    return jnp.where(lower, jnp.exp(jnp.where(lower, gcc - gcr, 0.0)), 0.0), d > 0


def _dn_a(k, beta, gcc, gcr, sgn):
    decay, strict = _dn_decay(gcc, gcr, sgn)
    return jnp.where(strict, beta * _bnt(k, k) * decay, 0.0)


def _dn_group(q, k, v, beta, gcc, gcr, sgn):
    n = DN_GROUP
    ii = lax.broadcasted_iota(jnp.int32, (n, n), 0)
    jj = lax.broadcasted_iota(jnp.int32, (n, n), 1)
    same = (ii & -DN_CHUNK) == (jj & -DN_CHUNK)
    d = (ii - jj) * sgn
    lower = same & (d >= 0)
    decay = jnp.where(lower, jnp.exp(jnp.where(lower, gcc - gcr, 0.0)), 0.0)
    a = jnp.where(same & (d > 0), beta * _bnt(k, k) * decay, 0.0)
    t_inv = _inv_unit(a, DN_CHUNK)
    u = _bnn(t_inv, v * beta)
    w = _bnn(t_inv, k * (beta * jnp.exp(gcc)))
    return u, w, _bnt(q, k) * decay, t_inv


def _dn_local(t_inv, q, k, v, beta, gcc, gcr, sgn):
    c = DN_CHUNK
    decay, _ = _dn_decay(gcc, gcr, sgn)
    eg = jnp.exp(gcc)
    u = _bnn(t_inv, v * beta)
    w = _bnn(t_inv, k * (beta * eg))
    qk = _bnt(q, k) * decay
    qd = q * eg
    last = jnp.where(sgn > 0, c - 1, 0)
    onehot = (lax.broadcasted_iota(jnp.int32, (c, 1), 0) == last).astype(F32)
    gl = jnp.sum(gcc * onehot, axis=0, keepdims=True)
    kd = k * jnp.exp(gl - gcc)
    egl = jnp.broadcast_to(jnp.exp(gl), (1, LANES))
    return u, w, qk, qd, kd, egl


def _hd_sign(hd):
    return jnp.where(hd < DN_HEADS, 1, -1).astype(jnp.int32)


def _head_of(hd):
    return jnp.where(hd < DN_HEADS, hd, hd - DN_HEADS)


def _dir_of(hd):
    return jnp.where(hd < DN_HEADS, 0, 1)


def _dn_specs(t):
    nl = LOCAL_ROWS // DN_CHUNK
    wide = pl.BlockSpec((1, LOCAL_ROWS, LANES), lambda hd, i: (hd, i, 0))
    half = pl.BlockSpec((1, LOCAL_ROWS, DN_CHUNK), lambda hd, i: (hd, i, 0))
    col = pl.BlockSpec((1, LOCAL_ROWS, 1), lambda hd, i: (hd, i, 0))
    row = pl.BlockSpec((1, nl, 1, DN_CHUNK), lambda hd, i: (hd, i, 0, 0))
    egl = pl.BlockSpec((1, nl, 1, LANES), lambda hd, i: (hd, i, 0, 0))
    return wide, half, col, row, egl


def _qkv_specs():
    return [pl.BlockSpec((LOCAL_ROWS, LANES), lambda hd, i: (i, _head_of(hd)))] * 3


def _dn_local_fwd(q, k, v, beta_c, gc_c, gc_r):
    t = q.shape[0]
    nc = t // DN_CHUNK
    nl = LOCAL_ROWS // DN_CHUNK
    wide, half, col, row, egl = _dn_specs(t)

    ng = LOCAL_ROWS // DN_GROUP
    per = DN_GROUP // DN_CHUNK
    grow = pl.BlockSpec((1, ng, 1, DN_GROUP), lambda hd, i: (hd, i, 0, 0))

    def body(q_ref, k_ref, v_ref, b_ref, gc_ref, gg_ref, u_ref, w_ref, qk_ref, qd_ref, kd_ref, egl_ref, t_ref):
        sgn = _hd_sign(pl.program_id(0))
        last = jnp.where(sgn > 0, DN_CHUNK - 1, 0)
        onehot = (lax.broadcasted_iota(jnp.int32, (DN_CHUNK, 1), 0) == last).astype(F32)
        groups = lambda a: a.reshape((ng, DN_GROUP) + a.shape[1:])
        q_all, k_all, gcc_all = q_ref[...], k_ref[...], gc_ref[0]
        u, w, qk, t_inv = jax.vmap(functools.partial(_dn_group, sgn=sgn))(
            groups(q_all), groups(k_all), groups(v_ref[...]), groups(b_ref[0]), groups(gcc_all), gg_ref[0])
        u_ref[0] = u.reshape(LOCAL_ROWS, LANES)
        w_ref[0] = w.reshape(LOCAL_ROWS, LANES)
        qd_ref[0] = q_all * jnp.exp(gcc_all)
        for gi in range(ng):
            for c in range(per):
                blk = slice(c * DN_CHUNK, (c + 1) * DN_CHUNK)
                rows = slice(gi * DN_GROUP + c * DN_CHUNK, gi * DN_GROUP + (c + 1) * DN_CHUNK)
                qk_ref[0, rows, :] = qk[gi, blk, blk]
                t_ref[0, rows, :] = t_inv[gi, blk, blk]
                gl = jnp.sum(gcc_all[rows, :] * onehot, axis=0, keepdims=True)
                kd_ref[0, rows, :] = k_all[rows, :] * jnp.exp(gl - gcc_all[rows, :])
                egl_ref[0, gi * per + c] = jnp.broadcast_to(jnp.exp(gl), (1, LANES))

    big = _sds((N_HD, t, LANES))
    small = _sds((N_HD, t, DN_CHUNK))
    return pl.pallas_call(
        body, name="dn_local_fwd", grid=(N_HD, t // LOCAL_ROWS), in_specs=_qkv_specs() + [col, col, grow],
        out_specs=[wide, wide, half, wide, wide, egl, half],
        out_shape=[big, big, small, big, big, _sds((N_HD, nc, 1, LANES)), small],
        compiler_params=_params(("parallel", "parallel")),
    )(q, k, v, beta_c, gc_c, gc_r.reshape(N_HD, t // DN_GROUP, 1, DN_GROUP))


def _dn_local_bwd(q, k, v, beta_c, gc_c, gc_r, t_inv, du, dw, dqk, dqd, dkd, degl):
    t = q.shape[0]
    nc = t // DN_CHUNK
    nl = LOCAL_ROWS // DN_CHUNK
    wide, half, col, row, egl = _dn_specs(t)
    dspec = pl.BlockSpec((1, LOCAL_ROWS, LANES), lambda hd, i: (_dir_of(hd), i, _head_of(hd)))

    def body(q_ref, k_ref, v_ref, b_ref, gc_ref, gr_ref, t_ref, du_ref, dw_ref, dqk_ref, dqd_ref, dkd_ref, degl_ref,
             dq_ref, dk_ref, dv_ref, db_ref, dgc_ref, dgr_ref):
        sgn = _hd_sign(pl.program_id(0))

        def chunk_bwd(tinv, q, k, v, beta, gcc, gcr, du, dw, dqk, dqd, dkd, degl):
            _, vjp = jax.vjp(functools.partial(_dn_local, sgn=sgn), tinv, q, k, v, beta, gcc, gcr)
            dt, dq, dk, dv, db, dgc, dgr = vjp((du, dw, dqk, dqd, dkd, degl))
            _, vjp_a = jax.vjp(functools.partial(_dn_a, sgn=sgn), k, beta, gcc, gcr)
            dk2, db2, dgc2, dgr2 = vjp_a(_inv_unit_t(tinv, dt))
            return dq, dk + dk2, dv, db + db2, dgc + dgc2, dgr + dgr2

        chunks = lambda a: a.reshape((nl, DN_CHUNK) + a.shape[1:])
        dq, dk, dv, db, dgc, dgr = jax.vmap(chunk_bwd)(
            chunks(t_ref[0]), chunks(q_ref[...]), chunks(k_ref[...]), chunks(v_ref[...]), chunks(b_ref[0]), chunks(gc_ref[0]),
            gr_ref[0], chunks(du_ref[0]), chunks(dw_ref[0]), chunks(dqk_ref[0]), chunks(dqd_ref[0]), chunks(dkd_ref[0]),
            degl_ref[0])
        dq_ref[0] = dq.reshape(LOCAL_ROWS, LANES)
        dk_ref[0] = dk.reshape(LOCAL_ROWS, LANES)
        dv_ref[0] = dv.reshape(LOCAL_ROWS, LANES)
        db_ref[0] = db.reshape(LOCAL_ROWS, 1)
        dgc_ref[0] = dgc.reshape(LOCAL_ROWS, 1)
        dgr_ref[0] = dgr

    per_dir = _sds((2, t, DN_WIDTH))
    return pl.pallas_call(
        body, name="dn_local_bwd", grid=(N_HD, t // LOCAL_ROWS),
        in_specs=_qkv_specs() + [col, col, row, half, wide, wide, half, wide, wide, egl],
        out_specs=[dspec, dspec, dspec, col, col, row],
        out_shape=[per_dir, per_dir, per_dir, _sds((N_HD, t, 1)), _sds((N_HD, t, 1)), _sds((N_HD, nc, 1, DN_CHUNK))],
        compiler_params=_params(("parallel", "parallel")),
    )(q, k, v, beta_c, gc_c, gc_r, t_inv, du, dw, dqk, dqd, dkd, degl)


REC_HEADS = 8
REC_GROUPS = N_HD // REC_HEADS
REC_FWD_GROUPS = DN_HEADS // REC_HEADS


def _rec_specs(time_block):
    nr = REC_ROWS // DN_CHUNK
    wide = pl.BlockSpec((REC_HEADS, REC_ROWS, LANES), lambda g, b: (g, time_block(g, b), 0))
    half = pl.BlockSpec((REC_HEADS, REC_ROWS, DN_CHUNK), lambda g, b: (g, time_block(g, b), 0))
    egl = pl.BlockSpec((REC_HEADS, nr, 1, LANES), lambda g, b: (g, time_block(g, b), 0, 0))
    state = pl.BlockSpec((REC_HEADS, nr, DN_HEAD_DIM, DN_HEAD_DIM), lambda g, b: (g, time_block(g, b), 0, 0))
    return wide, half, egl, state


def _rec_head_cols(g):
    return jnp.where(g < REC_FWD_GROUPS, g, g - REC_FWD_GROUPS)


def _dn_rec_fwd(u, w, qk, qd, kd, egl):
    t = u.shape[1]
    nb = t // REC_ROWS
    nr = REC_ROWS // DN_CHUNK
    nc = t // DN_CHUNK

    def time_block(g, b):
        return jnp.where(g < REC_FWD_GROUPS, b, nb - 1 - b)

    wide, half, egl_spec, state = _rec_specs(time_block)
    o_spec = pl.BlockSpec((1, REC_ROWS, REC_HEADS * LANES),
                          lambda g, b: (jnp.where(g < REC_FWD_GROUPS, 0, 1), time_block(g, b), _rec_head_cols(g)))

    def body(u_ref, w_ref, qk_ref, qd_ref, kd_ref, egl_ref, o_ref, vn_ref, s_ref, s_scr):
        fwd = pl.program_id(0) < REC_FWD_GROUPS

        @pl.when(pl.program_id(1) == 0)
        def _():
            s_scr[...] = jnp.zeros_like(s_scr)

        def run(order):
            heads = range(REC_HEADS)
            s = [s_scr[j] for j in heads]
            for ce in order:
                rows = slice(ce * DN_CHUNK, (ce + 1) * DN_CHUNK)
                vn = [u_ref[j, rows, :] - _bnn(w_ref[j, rows, :], s[j]) for j in heads]
                o = [_bnn(qd_ref[j, rows, :], s[j]) + _bnn(qk_ref[j, rows, :], vn[j]) for j in heads]
                nxt = [s[j] * egl_ref[j, ce] + _btn(kd_ref[j, rows, :], vn[j]) for j in heads]
                for j in heads:
                    s_ref[j, ce] = s[j]
                    vn_ref[j, rows, :] = vn[j]
                    o_ref[0, rows, j * LANES:(j + 1) * LANES] = o[j]
                s = nxt
            for j in heads:
                s_scr[j] = s[j]

        pl.when(fwd)(lambda: run(range(nr)))
        pl.when(jnp.logical_not(fwd))(lambda: run(range(nr - 1, -1, -1)))

    return pl.pallas_call(
        body, name="dn_rec_fwd", grid=(REC_GROUPS, nb), in_specs=[wide, wide, half, wide, wide, egl_spec],
        out_specs=[o_spec, wide, state],
        out_shape=[_sds((2, t, DN_WIDTH)), _sds((N_HD, t, LANES)), _sds((N_HD, nc, DN_HEAD_DIM, DN_HEAD_DIM))],
        scratch_shapes=[pltpu.VMEM((REC_HEADS, DN_HEAD_DIM, DN_HEAD_DIM), F32)],
        compiler_params=_params(("parallel", "arbitrary")),
    )(u, w, qk, qd, kd, egl)


def _dn_rec_bwd(w, qk, qd, kd, egl, vn, states, do):
    t = w.shape[1]
    nb = t // REC_ROWS
    nr = REC_ROWS // DN_CHUNK
    nc = t // DN_CHUNK

    def time_block(g, b):
        return jnp.where(g < REC_FWD_GROUPS, nb - 1 - b, b)

    wide, half, egl_spec, state = _rec_specs(time_block)
    do_spec = pl.BlockSpec((REC_ROWS, REC_HEADS * LANES), lambda g, b: (time_block(g, b), _rec_head_cols(g)))

    def body(w_ref, qk_ref, qd_ref, kd_ref, egl_ref, vn_ref, s_ref, do_ref,
             du_ref, dw_ref, dqk_ref, dqd_ref, dkd_ref, degl_ref, ds_scr):
        fwd = pl.program_id(0) < REC_FWD_GROUPS

        @pl.when(pl.program_id(1) == 0)
        def _():
            ds_scr[...] = jnp.zeros_like(ds_scr)

        def run(order):
            heads = range(REC_HEADS)
            ds = [ds_scr[j] for j in heads]
            for ce in order:
                rows = slice(ce * DN_CHUNK, (ce + 1) * DN_CHUNK)
                s = [s_ref[j, ce] for j in heads]
                do_c = [do_ref[rows, j * LANES:(j + 1) * LANES] for j in heads]
                vn_c = [vn_ref[j, rows, :] for j in heads]
                dvn = [_btn(qk_ref[j, rows, :], do_c[j]) + _bnn(kd_ref[j, rows, :], ds[j]) for j in heads]
                nxt = [ds[j] * egl_ref[j, ce] + _btn(qd_ref[j, rows, :], do_c[j]) - _btn(w_ref[j, rows, :], dvn[j])
                       for j in heads]
                for j in heads:
                    du_ref[j, rows, :] = dvn[j]
                    dw_ref[j, rows, :] = -_bnt(dvn[j], s[j])
                for j in heads:
                    dqk_ref[j, rows, :] = _bnt(do_c[j], vn_c[j])
                    dqd_ref[j, rows, :] = _bnt(do_c[j], s[j])
                for j in heads:
                    dkd_ref[j, rows, :] = _bnt(vn_c[j], ds[j])
                    degl_ref[j, ce] = jnp.sum(s[j] * ds[j], axis=0, keepdims=True)
                ds = nxt
            for j in heads:
                ds_scr[j] = ds[j]

        pl.when(fwd)(lambda: run(range(nr - 1, -1, -1)))
        pl.when(jnp.logical_not(fwd))(lambda: run(range(nr)))

    big = _sds((N_HD, t, LANES))
    return pl.pallas_call(
        body, name="dn_rec_bwd", grid=(REC_GROUPS, nb), in_specs=[wide, half, wide, wide, egl_spec, wide, state, do_spec],
        out_specs=[wide, wide, half, wide, wide, egl_spec],
        out_shape=[big, big, _sds((N_HD, t, DN_CHUNK)), big, big, _sds((N_HD, nc, 1, LANES))],
        scratch_shapes=[pltpu.VMEM((REC_HEADS, DN_HEAD_DIM, DN_HEAD_DIM), F32)],
        compiler_params=_params(("parallel", "arbitrary")),
    )(w, qk, qd, kd, egl, vn, states, do)


def _post_fn(of, ob, z, gain):
    o = of + ob
    return o * lax.rsqrt(jnp.mean(o * o, axis=-1, keepdims=True) + RMS_EPS) * gain * _silu(z)


def _post_specs():
    o_spec = [pl.BlockSpec((1, ROW_TILE, DN_WIDTH), functools.partial(lambda i, d: (d, i, 0), d=d)) for d in (0, 1)]
    return o_spec, _cols(DN_WIDTH, C_Z), _rows(DN_WIDTH), _full((1, LANES))


def _post_fwd(o2, proj, gain):
    t = proj.shape[0]
    o_spec, z_spec, wide, gain_spec = _post_specs()

    def body(of_ref, ob_ref, z_ref, g_ref, out_ref):
        for h in range(DN_HEADS):
            cols = slice(h * LANES, (h + 1) * LANES)
            out_ref[:, cols] = _post_fn(of_ref[0, :, cols], ob_ref[0, :, cols], z_ref[:, cols], g_ref[...]).astype(_MXU)

    return pl.pallas_call(
        body, name="post_fwd", grid=(t // ROW_TILE,), in_specs=o_spec + [z_spec, gain_spec], out_specs=wide,
        out_shape=_sds((t, DN_WIDTH), _MXU), compiler_params=_params(("parallel",)),
    )(o2, o2, proj, gain)


def _post_bwd(o2, proj, gain, dout):
    t = proj.shape[0]
    o_spec, z_spec, wide, gain_spec = _post_specs()

    def body(of_ref, ob_ref, z_ref, g_ref, d_ref, do_ref, dz_ref, dg_ref):
        dg_sum = jnp.zeros((1, LANES), F32)
        for h in range(DN_HEADS):
            cols = slice(h * LANES, (h + 1) * LANES)
            _, vjp = jax.vjp(_post_fn, of_ref[0, :, cols], ob_ref[0, :, cols], z_ref[:, cols], g_ref[...])
            do, _, dz, dg = vjp(d_ref[:, cols])
            do_ref[:, cols] = do
            dz_ref[:, cols] = dz.astype(_MXU)
            dg_sum = dg_sum + dg
        _accumulate(dg_ref, dg_sum, pl.program_id(0))

    return pl.pallas_call(
        body, name="post_bwd", grid=(t // ROW_TILE,), in_specs=o_spec + [z_spec, gain_spec, wide],
        out_specs=[wide, wide, gain_spec], out_shape=[_sds((t, DN_WIDTH)), _sds((t, DN_WIDTH), _MXU), _sds((1, LANES))],
        compiler_params=_params(("arbitrary",)),
    )(o2, o2, proj, gain, dout)


def _rope(x, cos, sin):
    lane = lax.broadcasted_iota(jnp.int32, x.shape, 1)
    first = (lane & (SW_HEAD_DIM - 1)) < SW_HEAD_DIM // 2
    rot = jnp.where(first, -pltpu.roll(x, LANES - SW_HEAD_DIM // 2, 1), pltpu.roll(x, SW_HEAD_DIM // 2, 1))
    return x * cos + rot * sin


def _rope_apply(q, k, q_cols, k_cols, cos, sin, name, dtype):
    t = cos.shape[0]
    qw, kw = SW_HEADS * SW_HEAD_DIM, SW_KV_WIDTH

    def body(q_ref, k_ref, c_ref, s_ref, qo_ref, ko_ref):
        c, s = c_ref[...], s_ref[...]
        for j in range(qw // LANES):
            cols = slice(j * LANES, (j + 1) * LANES)
            qo_ref[:, cols] = _rope(q_ref[:, cols], c, s).astype(dtype)
        for j in range(kw // LANES):
            cols = slice(j * LANES, (j + 1) * LANES)
            ko_ref[:, cols] = _rope(k_ref[:, cols], c, s).astype(dtype)

    return pl.pallas_call(
        body, name=name, grid=(t // ROW_TILE,), in_specs=[_cols(qw, q_cols), _cols(kw, k_cols), _rows(LANES), _rows(LANES)],
        out_specs=[_rows(qw), _rows(kw)], out_shape=[_sds((t, qw), dtype), _sds((t, kw), dtype)],
        compiler_params=_params(("parallel",)),
    )(q, k, cos, sin)


def _attn_core(qs, kb, vb, sink, mask):
    s = _bnt(qs, kb) * (SW_HEAD_DIM ** -0.5)
    s = jnp.where(mask, s, -1e30)
    m = lax.stop_gradient(jnp.maximum(jnp.max(s, axis=1, keepdims=True), sink))
    e = jnp.exp(s - m)
    den = jnp.sum(e, axis=1, keepdims=True) + jnp.exp(sink - m)
    return _bnn(e / den, vb)


def _band_mask(n, nb):
    rows = SW_GROUP * SW_BLOCK
    i = lax.broadcasted_iota(jnp.int32, (rows, 3 * SW_BLOCK), 0) & (SW_BLOCK - 1)
    j = lax.broadcasted_iota(jnp.int32, (rows, 3 * SW_BLOCK), 1)
    near = (j - i >= 0) & (j - i <= 2 * SW_BLOCK)
    lo = jnp.where(n == 0, SW_BLOCK, 0)
    hi = jnp.where(n == nb - 1, 2 * SW_BLOCK, 3 * SW_BLOCK)
    return near & (j >= lo) & (j < hi)


def _band_specs(nb, v_cols):
    def spec(width, base, shift):
        return pl.BlockSpec((SW_BLOCK, width), lambda n: (jnp.clip(n + shift, 0, nb - 1), base // width))
    k_specs = [spec(SW_KV_WIDTH, 0, s) for s in (-1, 0, 1)]
    v_specs = [spec(SW_KV_WIDTH, v_cols, s) for s in (-1, 0, 1)]
    return k_specs, v_specs


def _head_cols(kv, g):
    h = kv * SW_GROUP + g
    return slice(h * SW_HEAD_DIM, (h + 1) * SW_HEAD_DIM)


def _kv_batches(q_ref, kb, vb, s_ref):
    kvs = range(SW_KV_HEADS)
    cols = lambda kv: slice(kv * SW_HEAD_DIM, (kv + 1) * SW_HEAD_DIM)
    qs = jnp.stack([jnp.concatenate([q_ref[:, _head_cols(kv, g)] for g in range(SW_GROUP)], axis=0) for kv in kvs])
    sinks = jnp.stack([jnp.concatenate([jnp.broadcast_to(s_ref[kv * SW_GROUP + g], (SW_BLOCK, 1)) for g in range(SW_GROUP)],
                                       axis=0) for kv in kvs])
    return qs, jnp.stack([kb[:, cols(kv)] for kv in kvs]), jnp.stack([vb[:, cols(kv)] for kv in kvs]), sinks


def _attn_fwd(qr, kr, proj, sinks):
    t = qr.shape[0]
    nb = t // SW_BLOCK
    qw = SW_HEADS * SW_HEAD_DIM
    k_specs, v_specs = _band_specs(nb, C_VSW)
    q_spec = pl.BlockSpec((SW_BLOCK, qw), lambda n: (n, 0))

    def body(q_ref, k0, k1, k2, v0, v1, v2, s_ref, o_ref):
        mask = _band_mask(pl.program_id(0), nb)
        kb = jnp.concatenate([k0[...], k1[...], k2[...]], axis=0)
        vb = jnp.concatenate([v0[...], v1[...], v2[...]], axis=0)
        qs, kbs, vbs, sinks_ = _kv_batches(q_ref, kb, vb, s_ref)
        o = jax.vmap(functools.partial(_attn_core, mask=mask))(qs, kbs, vbs, sinks_)
        for kv in range(SW_KV_HEADS):
            for g in range(SW_GROUP):
                o_ref[:, _head_cols(kv, g)] = o[kv, g * SW_BLOCK:(g + 1) * SW_BLOCK, :].astype(_MXU)

    return pl.pallas_call(
        body, name="attn_fwd", grid=(nb,), in_specs=[q_spec] + k_specs + v_specs + [_full((SW_HEADS, 1, 1))],
        out_specs=q_spec, out_shape=_sds((t, qw), _MXU), compiler_params=_params(("parallel",)),
    )(qr, kr, kr, kr, proj, proj, proj, sinks)


def _attn_bwd(qr, kr, proj, sinks, do):
    t = qr.shape[0]
    nb = t // SW_BLOCK
    qw = SW_HEADS * SW_HEAD_DIM
    k_specs, v_specs = _band_specs(nb, C_VSW)
    q_spec = pl.BlockSpec((SW_BLOCK, qw), lambda n: (n, 0))
    part = pl.BlockSpec((1, 3 * SW_BLOCK, SW_KV_WIDTH), lambda n: (n, 0, 0))

    def body(q_ref, k0, k1, k2, v0, v1, v2, s_ref, do_ref, dq_ref, dk_ref, dv_ref, ds_ref):
        mask = _band_mask(pl.program_id(0), nb)
        kb = jnp.concatenate([k0[...], k1[...], k2[...]], axis=0).astype(F32)
        vb = jnp.concatenate([v0[...], v1[...], v2[...]], axis=0)

        @pl.when(pl.program_id(0) == 0)
        def _():
            ds_ref[...] = jnp.zeros_like(ds_ref)

        qs, kbs, vbs, sinks_ = _kv_batches(q_ref, kb, vb, s_ref)
        dos = jnp.stack([jnp.concatenate([do_ref[:, _head_cols(kv, g)] for g in range(SW_GROUP)], axis=0)
                         for kv in range(SW_KV_HEADS)])

        def head_bwd(q_, k_, v_, sink_, do_):
            _, vjp = jax.vjp(functools.partial(_attn_core, mask=mask), q_, k_, v_, sink_)
            return vjp(do_)

        dqs, dkb, dvb, dsink = jax.vmap(head_bwd)(qs.astype(F32), kbs, vbs, sinks_, dos)
        for kv in range(SW_KV_HEADS):
            kvc = slice(kv * SW_HEAD_DIM, (kv + 1) * SW_HEAD_DIM)
            dk_ref[0, :, kvc] = dkb[kv]
            dv_ref[0, :, kvc] = dvb[kv]
            for g in range(SW_GROUP):
                rows = slice(g * SW_BLOCK, (g + 1) * SW_BLOCK)
                dq_ref[:, _head_cols(kv, g)] = dqs[kv, rows, :]
                ds_ref[kv * SW_GROUP + g] += jnp.sum(dsink[kv, rows, :], axis=0, keepdims=True)

    parts = _sds((nb, 3 * SW_BLOCK, SW_KV_WIDTH))
    return pl.pallas_call(
        body, name="attn_bwd", grid=(nb,), in_specs=[q_spec] + k_specs + v_specs + [_full((SW_HEADS, 1, 1)), q_spec],
        out_specs=[q_spec, part, part, _full((SW_HEADS, 1, 1))], out_shape=[_sds((t, qw)), parts, parts, _sds((SW_HEADS, 1, 1))],
        compiler_params=_params(("arbitrary",)),
    )(qr, kr, kr, kr, proj, proj, proj, sinks, do)


def _band_sum(parts, name, dtype):
    nb = parts.shape[0]
    w = parts.shape[2]

    def spec(shift, slot):
        return pl.BlockSpec((1, SW_BLOCK, w), lambda m: (jnp.clip(m + shift, 0, nb - 1), slot, 0))

    def body(prev_ref, own_ref, next_ref, o_ref):
        m = pl.program_id(0)
        total = own_ref[0] + jnp.where(m > 0, prev_ref[0], 0.0) + jnp.where(m < nb - 1, next_ref[0], 0.0)
        o_ref[...] = total.astype(dtype)

    return pl.pallas_call(
        body, name=name, grid=(nb,), in_specs=[spec(-1, 2), spec(0, 1), spec(1, 0)],
        out_specs=pl.BlockSpec((SW_BLOCK, w), lambda m: (m, 0)), out_shape=_sds((nb * SW_BLOCK, w), dtype),
        compiler_params=_params(("parallel",)),
    )(parts, parts, parts)


def _gate_layouts(gbo):
    t = gbo.shape[0]
    beta_c = gbo[:, :N_HD].T.reshape(N_HD, t, 1)
    gc = gbo[:, N_HD:2 * N_HD].T
    return beta_c, gc.reshape(N_HD, t, 1), gc.reshape(N_HD, t // DN_CHUNK, 1, DN_CHUNK)


def _gate_layouts_t(dbeta_c, dgc_c, dgc_r):
    t = dbeta_c.shape[1]
    pad = jnp.zeros((t, LANES - 2 * N_HD), F32)
    none = jnp.zeros((t, N_HD), F32)
    d_a = jnp.concatenate([dbeta_c.reshape(N_HD, t).T, dgc_c.reshape(N_HD, t).T, pad], axis=1)
    d_b = jnp.concatenate([none, dgc_r.reshape(N_HD, t).T, pad], axis=1)
    return d_a, d_b


def _layer_fwd(x, xb, w, cos, sin):
    proj = _mm(xb, w["wm"], "nn", name="proj")
    ba = _mm(xb, w["wba"], "nn", name="proj_gates")
    qn, kn, vv = [_prep_fwd(proj, w["conv"], kind) for kind in range(3)]
    gbo = _gb_fwd(ba, w["alog"], w["dtb"])
    beta_c, gc_c, gc_r = _gate_layouts(gbo)
    u, wk, qk, qd, kd, egl, tinv = _dn_local_fwd(qn, kn, vv, beta_c, gc_c, gc_r)
    o2, vn, states = _dn_rec_fwd(u, wk, qk, qd, kd, egl)
    o_dn = _post_fwd(o2, proj, w["dnw"])
    qr, kr = _rope_apply(proj, proj, C_QSW, C_KSW, cos, sin, "rope_fwd", _MXU)
    o_sw = _attn_fwd(qr, kr, proj, w["sinks"])
    ya = _mm(o_dn, w["wa"], "nn", name="branch_a")
    yb = _mm(o_sw, w["wb"], "nn", name="branch_b")
    merged = _merge_fwd(proj, ya, yb)
    mix = _mm(merged, w["wo"], "nn", name="mix_out")
    x1, x1b = _ln_fwd(x, mix, w["ln1g"], w["ln1b"], "ln1_fwd")
    gu = _mm(x1b, w["wgu"], "nn", name="ffn_up")
    h = _swiglu_fwd(gu)
    f = _mm(h, w["wd"], "nn", name="ffn_down")
    x2, x2b = _ln_fwd(x1, f, w["ln2g"], w["ln2b"], "ln2_fwd")
    res = dict(x=x, xb=xb, proj=proj, ba=ba, qn=qn, kn=kn, vv=vv, gbo=gbo, wk=wk, qk=qk, qd=qd, kd=kd, egl=egl, tinv=tinv, vn=vn,
               states=states, o2=o2, o_dn=o_dn, qr=qr, kr=kr, o_sw=o_sw, ya=ya, yb=yb, merged=merged, mix=mix, x1=x1, x1b=x1b,
               gu=gu, h=h, f=f)
    return x2, x2b, res


def _layer_bwd(dx2, w, r, cos, sin):
    dx1, df, dln2g, dln2b = _ln_bwd(r["x1"], r["f"], w["ln2g"], w["ln2b"], dx2, "ln2_bwd")
    dh = _mm(df, w["wd"], "nt", name="d_ffn_hidden")
    dwd = _mm(r["h"], df, "tn", name="dw_ffn_down", out_dtype=_MXU)
    dgu = _swiglu_bwd(r["gu"], dh)
    dwgu = _mm(r["x1b"], dgu, "tn", name="dw_ffn_up", out_dtype=_MXU)
    dx1 = _mm(dgu, w["wgu"], "nt", name="dx_ffn", add=dx1)
    dx, dmix, dln1g, dln1b = _ln_bwd(r["x"], r["mix"], w["ln1g"], w["ln1b"], dx1, "ln1_bwd")
    dmerged = _mm(dmix, w["wo"], "nt", name="d_merged")
    dwo = _mm(r["merged"], dmix, "tn", name="dw_mix_out", out_dtype=_MXU)
    dga, dgb, dya, dyb = _merge_bwd(r["proj"], r["ya"], r["yb"], dmerged)
    dwa = _mm(r["o_dn"], dya, "tn", name="dw_branch_a", out_dtype=_MXU)
    do_dn = _mm(dya, w["wa"], "nt", name="d_branch_a")
    dwb = _mm(r["o_sw"], dyb, "tn", name="dw_branch_b", out_dtype=_MXU)
    do_sw = _mm(dyb, w["wb"], "nt", name="d_branch_b")
    do, dz, ddnw = _post_bwd(r["o2"], r["proj"], w["dnw"], do_dn)
    du, dwk, dqk, dqd, dkd, degl = _dn_rec_bwd(r["wk"], r["qk"], r["qd"], r["kd"], r["egl"], r["vn"], r["states"], do)
    beta_c, gc_c, gc_r = _gate_layouts(r["gbo"])
    dq3, dk3, dv3, dbeta_c, dgc_c, dgc_r = _dn_local_bwd(r["qn"], r["kn"], r["vv"], beta_c, gc_c, gc_r, r["tinv"],
                                                         du, dwk, dqk, dqd, dkd, degl)
    dqkv, dconv = zip(*[_prep_bwd(r["proj"], w["conv"], d2, kind) for kind, d2 in enumerate((dq3, dk3, dv3))])
    dconv = jnp.concatenate(dconv, axis=1)
    d_a, d_b = _gate_layouts_t(dbeta_c, dgc_c, dgc_r)
    dba, dalog, ddtb = _gb_bwd(r["ba"], w["alog"], w["dtb"], d_a, d_b)
    dqr, dkparts, dvparts, dsinks = _attn_bwd(r["qr"], r["kr"], r["proj"], w["sinks"], do_sw)
    dkr = _band_sum(dkparts, "attn_dk_sum", F32)
    dv = _band_sum(dvparts, "attn_dv_sum", _MXU)
    dq_sw, dk_sw = _rope_apply(dqr, dkr, 0, 0, cos, -sin, "rope_bwd", _MXU)
    dproj = jnp.concatenate([*dqkv, dz, dq_sw, dga, dgb, dk_sw, dv], axis=1)
    dwm = _mm(r["xb"], dproj, "tn", name="dw_proj", out_dtype=_MXU)
    dwba = _mm(r["xb"], dba, "tn", name="dw_proj_gates", out_dtype=_MXU)
    dx = _mm(dproj, w["wm"], "nt", name="dx_proj", add=dx)
    dx = _mm(dba, w["wba"], "nt", name="dx_proj_gates", add=dx)
    grads = dict(wm=dwm, wba=dwba, conv=dconv, alog=dalog, dtb=ddtb, dnw=ddnw, sinks=dsinks, wa=dwa, wb=dwb, wo=dwo,
                 ln1g=dln1g, ln1b=dln1b, wgu=dwgu, wd=dwd, ln2g=dln2g, ln2b=dln2b)
    return dx, grads


def _rope_tables(t):
    half = SW_HEAD_DIM // 2
    inv_freq = ROPE_THETA ** (-jnp.arange(half, dtype=F32) / half)
    ang = jnp.arange(t, dtype=F32)[:, None] * inv_freq[None, :]
    return jnp.tile(jnp.cos(ang), (1, LANES // half)), jnp.tile(jnp.sin(ang), (1, LANES // half))


def _trunk(x, target, layers):
    cos, sin = _rope_tables(x.shape[0])
    xb = x.astype(_MXU)
    saved = []
    for w in layers:
        x, xb, res = _layer_fwd(x, xb, w, cos, sin)
        saved.append(res)
    sq, dx = _loss_head(x, target)
    grads = [None] * len(layers)
    for i in reversed(range(len(layers))):
        dx, grads[i] = _layer_bwd(dx, layers[i], saved[i], cos, sin)
    return sq, dx, grads


N_CHIPS = 4


def _mesh_pos():
    return lax.axis_index("x"), lax.axis_index("y"), lax.axis_index("c")


def _other_chips(x, y):
    return [(1 - x, y), (x, 1 - y), (1 - x, 1 - y)]


def _remote_copy(src, dst, sems, k, to):
    send_sems, recv_sems = sems
    return pltpu.make_async_remote_copy(src_ref=src, dst_ref=dst, send_sem=send_sems.at[k], recv_sem=recv_sems.at[k],
                                        device_id=to, device_id_type=pl.DeviceIdType.MESH)


def _comm_call(body, name, out_shape, n_sems, *operands):
    return pl.pallas_call(
        body, name=name, in_specs=[pl.BlockSpec(memory_space=pl.ANY)] * len(operands),
        out_specs=pl.BlockSpec(memory_space=pl.ANY), out_shape=out_shape,
        scratch_shapes=[pltpu.SemaphoreType.DMA((n_sems,)), pltpu.SemaphoreType.DMA((n_sems,)), pltpu.SemaphoreType.DMA],
        compiler_params=pltpu.CompilerParams(has_side_effects=True),
    )(*operands)


def _all_gather(block, name):
    def body(x_ref, o_ref, send_sems, recv_sems, local_sem):
        x, y, c = _mesh_pos()
        sems = (send_sems, recv_sems)
        sibling = (x, y, 1 - c)
        chips = _other_chips(x, y)

        def slot(px, py, pc):
            return o_ref.at[4 * px + 2 * py + pc]

        mine = pltpu.make_async_copy(x_ref, slot(x, y, c), local_sem)
        mine.start()
        first = [_remote_copy(x_ref, slot(x, y, c), sems, 0, sibling)]
        first += [_remote_copy(x_ref, slot(x, y, c), sems, 1 + j, (*chip, c)) for j, chip in enumerate(chips)]
        for cp in first:
            cp.start()
        passed = [_remote_copy(slot(*chip, c), slot(*chip, c), sems, 4 + j, sibling) for j, chip in enumerate(chips)]
        for j, chip in enumerate(chips):
            _remote_copy(x_ref, slot(*chip, c), sems, 1 + j, sibling).wait_recv()
            passed[j].start()
        _remote_copy(x_ref, slot(x, y, 1 - c), sems, 0, sibling).wait_recv()
        for j, chip in enumerate(chips):
            _remote_copy(x_ref, slot(*chip, 1 - c), sems, 4 + j, sibling).wait_recv()
        for cp in first + passed:
            cp.wait_send()
        mine.wait()

    return _comm_call(body, name, _sds((N_DEV,) + block.shape, block.dtype), N_DEV - 1, block)


def _sibling_swap(parts, name):
    def body(x_ref, o_ref, send_sems, recv_sems, local_sem):
        x, y, c = _mesh_pos()
        sems = (send_sems, recv_sems)
        sibling = (x, y, 1 - c)
        copies = [_remote_copy(x_ref.at[2 * q + (1 - c)], o_ref.at[q], sems, q, sibling) for q in range(N_CHIPS)]
        for cp in copies:
            cp.start()
        for cp in copies:
            cp.wait()

    return _comm_call(body, name, _sds((N_CHIPS,) + parts.shape[1:], parts.dtype), N_CHIPS, parts)


def _chip_exchange(parts, name):
    def body(x_ref, o_ref, send_sems, recv_sems, local_sem):
        x, y, c = _mesh_pos()
        sems = (send_sems, recv_sems)
        me = 2 * x + y
        mine = pltpu.make_async_copy(x_ref.at[me], o_ref.at[me], local_sem)
        mine.start()
        copies = [_remote_copy(x_ref.at[2 * cx + cy], o_ref.at[me], sems, j, (cx, cy, c))
                  for j, (cx, cy) in enumerate(_other_chips(x, y))]
        for cp in copies:
            cp.start()
        for j, (cx, cy) in enumerate(_other_chips(x, y)):
            _remote_copy(x_ref.at[me], o_ref.at[2 * cx + cy], sems, j, (cx, cy, c)).wait_recv()
        for cp in copies:
            cp.wait_send()
        mine.wait()

    return _comm_call(body, name, _sds(parts.shape, parts.dtype), N_CHIPS - 1, parts)


def _pair_sum(a, b, name):
    n, rows, cols = a.shape
    tr = rows if rows <= 512 else _row_tile(rows)
    blk = pl.BlockSpec((1, tr, cols), lambda q, i: (q, i, 0))

    def body(a_ref, b_ref, o_ref):
        o_ref[...] = (a_ref[...].astype(F32) + b_ref[...].astype(F32)).astype(o_ref.dtype)

    return pl.pallas_call(
        body, name=name, grid=(n, rows // tr), in_specs=[blk, blk], out_specs=blk, out_shape=_sds(a.shape, a.dtype),
        compiler_params=_params(("parallel", "parallel")),
    )(a, b)


def _reduce_to_owner(parts, name):
    c = lax.axis_index("c")
    from_sibling = _sibling_swap(parts, "swap_" + name)
    own = lax.dynamic_index_in_dim(parts.reshape((N_CHIPS, 2) + parts.shape[1:]), c, axis=1, keepdims=False)
    return _chip_exchange(_pair_sum(own, from_sibling, "pair_sum_" + name), "exchange_" + name)


def _sum_adamw(parts, w, m, v, name):
    rows, cols = w.shape
    n_parts = parts.shape[0]
    tr = rows if rows <= 512 else _row_tile(rows)
    blk = pl.BlockSpec((tr, cols), lambda i: (i, 0))

    def body(p_ref, w_ref, m_ref, v_ref, g_ref, d_ref, nm_ref, nv_ref):
        g = p_ref[0].astype(F32)
        for i in range(1, n_parts):
            g = g + p_ref[i].astype(F32)
        nm = ADAM_B1 * m_ref[...] + (1.0 - ADAM_B1) * g
        nv = ADAM_B2 * v_ref[...] + (1.0 - ADAM_B2) * jnp.square(g)
        m_hat = nm / (1.0 - ADAM_B1 ** ADAM_STEP)
        v_hat = nv / (1.0 - ADAM_B2 ** ADAM_STEP)
        g_ref[...] = g
        d_ref[...] = -ADAM_LR * (m_hat / (jnp.sqrt(v_hat) + ADAM_EPS) + ADAM_WD * w_ref[...])
        nm_ref[...] = nm
        nv_ref[...] = nv

    return pl.pallas_call(
        body, name=name, grid=(rows // tr,), in_specs=[pl.BlockSpec((n_parts, tr, cols), lambda i: (0, i, 0)), blk, blk, blk],
        out_specs=[blk] * 4, out_shape=[_sds((rows, cols))] * 4, compiler_params=_params(("parallel",)),
    )(parts, w, m, v)


def _row_tile(rows):
    for t in (256, 128, 64, 32, 16, 8):
        if rows % t == 0:
            return t
    return rows


def _gathered_cols(g):
    n, l, rows, c = g.shape
    return jnp.transpose(g, (1, 2, 0, 3)).reshape(l, rows, n * c)


def _gathered_rows(g):
    n, l, rows, c = g.shape
    return jnp.transpose(g, (1, 0, 2, 3)).reshape(l, n * rows, c)


def _col_parts(full):
    l, rows, c = full.shape
    return jnp.transpose(full.reshape(l, rows, N_DEV, c // N_DEV), (2, 0, 1, 3))


def _row_parts(full):
    l, rows, c = full.shape
    return jnp.transpose(full.reshape(l, N_DEV, rows // N_DEV, c), (1, 0, 2, 3))


def _w_in_split(w_in):
    s = lambda a, n: w_in[..., a:a + n]
    main = jnp.concatenate([s(R_QKV, 3072), s(R_Z, 1024), s(R_QSW, 1024), s(R_G, 2048), s(R_KSW, 256), s(R_VSW, 256)], axis=-1)
    gates = jnp.pad(s(R_BA, 2 * N_HD), ((0, 0), (0, 0), (0, LANES - 2 * N_HD)))
    return main, gates


def _w_in_join(dmain, dgates):
    s = lambda a, n: dmain[..., a:a + n]
    return jnp.concatenate([s(C_QKV, 3072), s(C_Z, 1024), dgates[..., :2 * N_HD], s(C_QSW, 1024), s(C_KSW, 256), s(C_VSW, 256),
                            s(C_GA, 2048)], axis=-1)


def _lane_row(a, offset):
    l, n = a.shape
    return jnp.pad(a, ((0, 0), (offset, LANES - offset - n)))[:, None, :]


def kernel(x, w_in, conv_w, a_log, dt_bias, dn_norm_w, sinks, w_branch_a, w_branch_b, w_out, ln1_g, ln1_b, w_gate_up, w_down, ln2_g, ln2_b, loss_target, m_w_in, m_conv_w, m_a_log, m_dt_bias, m_dn_norm_w, m_sinks, m_w_branch_a, m_w_branch_b, m_w_out, m_ln1_g, m_ln1_b, m_w_gate_up, m_w_down, m_ln2_g, m_ln2_b, v_w_in, v_conv_w, v_a_log, v_dt_bias, v_dn_norm_w, v_sinks, v_w_branch_a, v_w_branch_b, v_w_out, v_ln1_g, v_ln1_b, v_w_gate_up, v_w_down, v_ln2_g, v_ln2_b):
    l = DEPTH
    bf = lambda a: a.astype(_MXU)
    w_in_full = _gathered_cols(_all_gather(bf(w_in), "gather_w_in"))
    wgu_full = _gathered_cols(_all_gather(bf(w_gate_up), "gather_w_gate_up"))
    wa_full = _gathered_rows(_all_gather(bf(w_branch_a), "gather_w_branch_a"))
    wb_full = _gathered_rows(_all_gather(bf(w_branch_b), "gather_w_branch_b"))
    wo_full = _gathered_rows(_all_gather(bf(w_out), "gather_w_out"))
    wd_full = _gathered_rows(_all_gather(bf(w_down), "gather_w_down"))
    conv_full = _gathered_cols(_all_gather(conv_w, "gather_conv_w"))
    wm, wba = _w_in_split(w_in_full)
    row = lambda a: a[:, None, :]
    stacked = dict(
        wm=wm, wba=wba, conv=jnp.pad(conv_full, ((0, 0), (0, 8 - DN_CONV), (0, 0))),
        alog=_lane_row(a_log.reshape(l, N_HD), N_HD), dtb=_lane_row(dt_bias.reshape(l, N_HD), N_HD), dnw=row(dn_norm_w),
        sinks=sinks.reshape(l, SW_HEADS, 1, 1), wa=wa_full, wb=wb_full, wo=wo_full, ln1g=row(ln1_g), ln1b=row(ln1_b),
        wgu=wgu_full, wd=wd_full, ln2g=row(ln2_g), ln2b=row(ln2_b))

    sq, dx, grads = _trunk(x[0], loss_target[0], [{k: a[i] for k, a in stacked.items()} for i in range(l)])
    g = {k: jnp.stack([gi[k] for gi in grads]) for k in grads[0]}
    loss = lax.psum(0.5 * sq[0, 0] / D_MODEL, ("x", "y", "c"))

    def big(parts, w, m, v, name):
        rows = w.shape[0] * w.shape[1]
        flat = lambda a: a.reshape(rows, a.shape[-1])
        got = _reduce_to_owner(parts.reshape(N_DEV, rows, w.shape[-1]), name)
        outs = _sum_adamw(got, flat(w), flat(m), flat(v), "adamw_" + name)
        return [o.reshape(w.shape) for o in outs]

    dconv = g["conv"][:, :DN_CONV, :]
    results = {
        "w_in": big(_col_parts(_w_in_join(g["wm"], g["wba"])), w_in, m_w_in, v_w_in, "w_in"),
        "conv_w": big(_col_parts(dconv), conv_w, m_conv_w, v_conv_w, "conv_w"),
        "w_branch_a": big(_row_parts(g["wa"]), w_branch_a, m_w_branch_a, v_w_branch_a, "w_branch_a"),
        "w_branch_b": big(_row_parts(g["wb"]), w_branch_b, m_w_branch_b, v_w_branch_b, "w_branch_b"),
        "w_out": big(_row_parts(g["wo"]), w_out, m_w_out, v_w_out, "w_out"),
        "w_gate_up": big(_col_parts(g["wgu"]), w_gate_up, m_w_gate_up, v_w_gate_up, "w_gate_up"),
        "w_down": big(_row_parts(g["wd"]), w_down, m_w_down, v_w_down, "w_down"),
    }

    small_w = {"a_log": a_log.reshape(l, N_HD), "dt_bias": dt_bias.reshape(l, N_HD), "dn_norm_w": dn_norm_w, "sinks": sinks,
               "ln1_g": ln1_g, "ln1_b": ln1_b, "ln2_g": ln2_g, "ln2_b": ln2_b}
    small_m = {"a_log": m_a_log, "dt_bias": m_dt_bias, "dn_norm_w": m_dn_norm_w, "sinks": m_sinks, "ln1_g": m_ln1_g,
               "ln1_b": m_ln1_b, "ln2_g": m_ln2_g, "ln2_b": m_ln2_b}
    small_v = {"a_log": v_a_log, "dt_bias": v_dt_bias, "dn_norm_w": v_dn_norm_w, "sinks": v_sinks, "ln1_g": v_ln1_g,
               "ln1_b": v_ln1_b, "ln2_g": v_ln2_g, "ln2_b": v_ln2_b}
    small_g = {"a_log": g["alog"][:, 0, N_HD:2 * N_HD], "dt_bias": g["dtb"][:, 0, N_HD:2 * N_HD], "dn_norm_w": g["dnw"][:, 0, :],
               "sinks": g["sinks"].reshape(l, SW_HEADS), "ln1_g": g["ln1g"][:, 0, :], "ln1_b": g["ln1b"][:, 0, :],
               "ln2_g": g["ln2g"][:, 0, :], "ln2_b": g["ln2b"][:, 0, :]}
    names = list(small_w)
    cat = lambda d: jnp.concatenate([d[n].reshape(l, -1) for n in names], axis=1)
    widths = [small_w[n].shape[1] for n in names]
    total = sum(widths)
    padded = -(-total // LANES) * LANES
    pad = lambda a: jnp.pad(a, ((0, 8 - l), (0, padded - total)))
    got = _all_gather(pad(cat(small_g)), "gather_small_grads")
    outs = _sum_adamw(got, pad(cat(small_w)), pad(cat({n: small_m[n].reshape(l, -1) for n in names})),
                      pad(cat({n: small_v[n].reshape(l, -1) for n in names})), "adamw_small")
    off = 0
    for n, wd_ in zip(names, widths):
        shape = {"a_log": a_log.shape, "dt_bias": dt_bias.shape}.get(n, small_w[n].shape)
        results[n] = [o[:l, off:off + wd_].reshape(shape) for o in outs]
        off += wd_

    order = ["w_in", "conv_w", "a_log", "dt_bias", "dn_norm_w", "sinks", "w_branch_a", "w_branch_b", "w_out", "ln1_g", "ln1_b",
             "w_gate_up", "w_down", "ln2_g", "ln2_b"]
    return (loss, dx[None], *[results[n][0] for n in order], *[results[n][1] for n in order],
            *[results[n][2] for n in order], *[results[n][3] for n in order])
```

```python
import functools

import jax
import jax.numpy as jnp
from jax import lax
from jax.experimental import pallas as pl
from jax.experimental.pallas import tpu as pltpu

F32 = jnp.float32
_MXU = jnp.bfloat16
_HI = lax.Precision.HIGHEST
_MID = lax.Precision.HIGH

N_DEV = 8
D_MODEL = 1024
DEPTH = 4
DN_HEADS = 8
DN_HEAD_DIM = 128
DN_WIDTH = DN_HEADS * DN_HEAD_DIM
DN_CONV = 5
DN_CHUNK = 64
SW_HEADS = 16
SW_KV_HEADS = 4
SW_HEAD_DIM = 64
SW_GROUP = SW_HEADS // SW_KV_HEADS
SW_BLOCK = 128
SW_KV_WIDTH = SW_KV_HEADS * SW_HEAD_DIM
ROPE_THETA = 10000.0
FFN_HIDDEN = 2816
DN_ALPHA = (2.0 * DEPTH) ** 0.25
LN_EPS = 1e-5
RMS_EPS = 1e-6
ADAM_LR = 0.001
ADAM_B1 = 0.9
ADAM_B2 = 0.999
ADAM_EPS = 1e-08
ADAM_WD = 0.01
ADAM_STEP = 10

LANES = 128
N_HD = 2 * DN_HEADS
DN_GROUP = 4 * DN_CHUNK
INV_SUB = 16
LOCAL_ROWS = 512
REC_ROWS = 256
ROW_TILE = 256
VMEM_LIMIT = 48 << 20

C_QKV, C_Z, C_QSW, C_GA, C_GB, C_KSW, C_VSW = 0, 3072, 4096, 5120, 6144, 7168, 7424
MAIN_COLS = 7680
R_QKV, R_Z, R_BA, R_QSW, R_KSW, R_VSW, R_G = 0, 3072, 4096, 4128, 5152, 5408, 5664
IN_COLS = 7712


_NN = ((1,), (0,))
_NT = ((1,), (1,))
_TN = ((0,), (0,))


def _dg(a, b, dims, precision):
    if precision is not None:
        return lax.dot_general(a, b, (dims, ((), ())), precision=precision, preferred_element_type=F32)
    return lax.dot_general(a.astype(_MXU), b.astype(_MXU), (dims, ((), ())), preferred_element_type=F32)


def _make_dots(hi):
    @jax.custom_vjp
    def nn(a, b):
        return _dg(a, b, _NN, hi)

    @jax.custom_vjp
    def nt(a, b):
        return _dg(a, b, _NT, hi)

    @jax.custom_vjp
    def tn(a, b):
        return _dg(a, b, _TN, hi)

    nn.defvjp(lambda a, b: (nn(a, b), (a, b)), lambda r, g: (nt(g, r[1]), tn(r[0], g)))
    nt.defvjp(lambda a, b: (nt(a, b), (a, b)), lambda r, g: (nn(g, r[1]), tn(g, r[0])))
    tn.defvjp(lambda a, b: (tn(a, b), (a, b)), lambda r, g: (nt(r[1], g), nn(r[0], g)))
    return nn, nt, tn


_bnn, _bnt, _btn = _make_dots(None)
_hnn, _hnt, _htn = _make_dots(_HI)
_mnn, _mnt, _mtn = _make_dots(_MID)


def _neumann(a, order):
    n = a.shape[0]
    eye = (lax.broadcasted_iota(jnp.int32, (n, n), 0) == lax.broadcasted_iota(jnp.int32, (n, n), 1)).astype(F32)
    inv = eye - a
    p = a
    span = 2
    while span < order:
        p = _mnn(p, p)
        inv = inv + _mnn(inv, p)
        span *= 2
    return inv


def _inv_unit(a, order):
    n = a.shape[0]
    ii = lax.broadcasted_iota(jnp.int32, (n, n), 0)
    jj = lax.broadcasted_iota(jnp.int32, (n, n), 1)
    near = (ii & -INV_SUB) == (jj & -INV_SUB)
    d_inv = _neumann(jnp.where(near, a, 0.0), INV_SUB)
    outer = _neumann(_mnn(d_inv, jnp.where(near, 0.0, a)), order // INV_SUB)
    return _mnn(outer, d_inv)


def _inv_unit_t(t, g):
    return -_mnt(_mtn(t, g), t)


def _silu(x):
    return x * jax.nn.sigmoid(x)


def _softplus(x):
    return jnp.maximum(x, 0.0) + jnp.log1p(jnp.exp(-jnp.abs(x)))


def _params(sem=None):
    kw = {"vmem_limit_bytes": VMEM_LIMIT}
    if sem is not None:
        kw["dimension_semantics"] = sem
    return pltpu.CompilerParams(**kw)


def _tile(dim, pref):
    if dim <= pref:
        return dim
    t = (pref // LANES) * LANES
    while t > LANES and dim % t:
        t -= LANES
    assert dim % t == 0, (dim, pref)
    return t


def _full(shape):
    zeros = (0,) * len(shape)
    return pl.BlockSpec(shape, lambda *_: zeros)


def _sds(shape, dtype=F32):
    return jax.ShapeDtypeStruct(shape, dtype)


def _mm(a, b, mode, *, name, add=None, tm=1536, tn=1536, tk=1536, out_dtype=F32):
    if mode == "nn":
        (m, k), (k2, n) = a.shape, b.shape
    elif mode == "nt":
        (m, k), (n, k2) = a.shape, b.shape
    else:
        (k, m), (k2, n) = a.shape, b.shape
    assert k == k2, (a.shape, b.shape, mode)
    tm, tn, tk = _tile(m, tm), _tile(n, tn), _tile(k, tk)
    nk = k // tk
    dims = {"nn": _NN, "nt": _NT, "tn": _TN}[mode]

    def body(*refs):
        if add is None:
            a_ref, b_ref, o_ref, acc = refs
        else:
            a_ref, b_ref, add_ref, o_ref, acc = refs
        kk = pl.program_id(2)

        @pl.when(kk == 0)
        def _():
            acc[...] = jnp.zeros_like(acc)

        acc[...] += _dg(a_ref[...], b_ref[...], dims, None)

        @pl.when(kk == nk - 1)
        def _():
            o_ref[...] = (acc[...] if add is None else acc[...] + add_ref[...]).astype(out_dtype)

    a_spec = pl.BlockSpec((tk, tm), lambda i, j, kk: (kk, i)) if mode == "tn" else pl.BlockSpec((tm, tk), lambda i, j, kk: (i, kk))
    b_spec = pl.BlockSpec((tn, tk), lambda i, j, kk: (j, kk)) if mode == "nt" else pl.BlockSpec((tk, tn), lambda i, j, kk: (kk, j))
    o_spec = pl.BlockSpec((tm, tn), lambda i, j, kk: (i, j))
    ins, specs = [a, b], [a_spec, b_spec]
    if add is not None:
        ins.append(add)
        specs.append(o_spec)
    return pl.pallas_call(
        body, name=name, grid=(m // tm, n // tn, nk), in_specs=specs, out_specs=o_spec,
        out_shape=_sds((m, n), out_dtype), scratch_shapes=[pltpu.VMEM((tm, tn), F32)],
        compiler_params=_params(("parallel", "parallel", "arbitrary")),
    )(*ins)


def _cols(width, start):
    assert start % width == 0
    return pl.BlockSpec((ROW_TILE, width), lambda i: (i, start // width))


def _rows(width):
    return pl.BlockSpec((ROW_TILE, width), lambda i: (i, 0))


def _accumulate(ref, value, step):
    @pl.when(step == 0)
    def _():
        ref[...] = value

    @pl.when(step != 0)
    def _():
        ref[...] += value


def _ln_fn(x, r, g, b):
    u = DN_ALPHA * x + r
    mu = jnp.mean(u, axis=-1, keepdims=True)
    var = jnp.mean(jnp.square(u - mu), axis=-1, keepdims=True)
    return (u - mu) * lax.rsqrt(var + LN_EPS) * g + b


def _ln_fwd(x, r, g, b, name):
    t, d = x.shape

    def body(x_ref, r_ref, g_ref, b_ref, o_ref, ob_ref):
        y = _ln_fn(x_ref[...], r_ref[...], g_ref[...], b_ref[...])
        o_ref[...] = y
        ob_ref[...] = y.astype(_MXU)

    return pl.pallas_call(
        body, name=name, grid=(t // ROW_TILE,), in_specs=[_rows(d), _rows(d), _full((1, d)), _full((1, d))],
        out_specs=[_rows(d), _rows(d)], out_shape=[_sds((t, d)), _sds((t, d), _MXU)], compiler_params=_params(("parallel",)),
    )(x, r, g, b)


def _ln_bwd(x, r, g, b, dy, name):
    t, d = x.shape

    def body(x_ref, r_ref, g_ref, b_ref, dy_ref, dx_ref, dr_ref, dg_ref, db_ref):
        _, vjp = jax.vjp(_ln_fn, x_ref[...], r_ref[...], g_ref[...], b_ref[...])
        dx, dr, dg, db = vjp(dy_ref[...])
        dx_ref[...] = dx
        dr_ref[...] = dr.astype(_MXU)
        _accumulate(dg_ref, dg, pl.program_id(0))
        _accumulate(db_ref, db, pl.program_id(0))

    return pl.pallas_call(
        body, name=name, grid=(t // ROW_TILE,),
        in_specs=[_rows(d), _rows(d), _full((1, d)), _full((1, d)), _rows(d)],
        out_specs=[_rows(d), _rows(d), _full((1, d)), _full((1, d))],
        out_shape=[_sds((t, d)), _sds((t, d), _MXU), _sds((1, d)), _sds((1, d))],
        compiler_params=_params(("arbitrary",)),
    )(x, r, g, b, dy)


def _merge_fn(ga, gb, ya, yb):
    return jax.nn.sigmoid(ga) * ya + jax.nn.sigmoid(gb) * yb


def _merge_fwd(proj, ya, yb):
    t, d = ya.shape

    def body(ga_ref, gb_ref, ya_ref, yb_ref, o_ref):
        o_ref[...] = _merge_fn(ga_ref[...], gb_ref[...], ya_ref[...], yb_ref[...]).astype(_MXU)

    return pl.pallas_call(
        body, name="merge_fwd", grid=(t // ROW_TILE,), in_specs=[_cols(d, C_GA), _cols(d, C_GB), _rows(d), _rows(d)],
        out_specs=_rows(d), out_shape=_sds((t, d), _MXU), compiler_params=_params(("parallel",)),
    )(proj, proj, ya, yb)


def _merge_bwd(proj, ya, yb, dm):
    t, d = ya.shape

    def body(ga_ref, gb_ref, ya_ref, yb_ref, dm_ref, dga_ref, dgb_ref, dya_ref, dyb_ref):
        _, vjp = jax.vjp(_merge_fn, ga_ref[...], gb_ref[...], ya_ref[...], yb_ref[...])
        dga_ref[...], dgb_ref[...], dya_ref[...], dyb_ref[...] = [g.astype(_MXU) for g in vjp(dm_ref[...])]

    return pl.pallas_call(
        body, name="merge_bwd", grid=(t // ROW_TILE,),
        in_specs=[_cols(d, C_GA), _cols(d, C_GB), _rows(d), _rows(d), _rows(d)],
        out_specs=[_rows(d)] * 4, out_shape=[_sds((t, d), _MXU)] * 4, compiler_params=_params(("parallel",)),
    )(proj, proj, ya, yb, dm)


def _swiglu_fn(gate, up):
    return _silu(gate) * up


def _swiglu_fwd(gu):
    t = gu.shape[0]
    f = FFN_HIDDEN
    rows = 128

    def body(gu_ref, o_ref):
        o_ref[...] = _swiglu_fn(gu_ref[:, :f], gu_ref[:, f:]).astype(_MXU)

    return pl.pallas_call(
        body, name="swiglu_fwd", grid=(t // rows,), in_specs=[pl.BlockSpec((rows, 2 * f), lambda i: (i, 0))],
        out_specs=pl.BlockSpec((rows, f), lambda i: (i, 0)), out_shape=_sds((t, f), _MXU), compiler_params=_params(("parallel",)),
    )(gu)


def _swiglu_bwd(gu, dh):
    t = gu.shape[0]
    f = FFN_HIDDEN
    rows = 128

    def body(gu_ref, dh_ref, o_ref):
        _, vjp = jax.vjp(_swiglu_fn, gu_ref[:, :f], gu_ref[:, f:])
        o_ref[:, :f], o_ref[:, f:] = [g.astype(_MXU) for g in vjp(dh_ref[...])]

    return pl.pallas_call(
        body, name="swiglu_bwd", grid=(t // rows,),
        in_specs=[pl.BlockSpec((rows, 2 * f), lambda i: (i, 0)), pl.BlockSpec((rows, f), lambda i: (i, 0))],
        out_specs=pl.BlockSpec((rows, 2 * f), lambda i: (i, 0)), out_shape=_sds((t, 2 * f), _MXU),
        compiler_params=_params(("parallel",)),
    )(gu, dh)


def _loss_head(y, target):
    t, d = y.shape

    def body(y_ref, t_ref, s_ref, dy_ref):
        err = y_ref[...] - t_ref[...]
        dy_ref[...] = err / d
        _accumulate(s_ref, jnp.broadcast_to(jnp.sum(jnp.square(err)), (1, LANES)), pl.program_id(0))

    return pl.pallas_call(
        body, name="loss_head", grid=(t // ROW_TILE,), in_specs=[_rows(d), _rows(d)],
        out_specs=[_full((1, LANES)), _rows(d)], out_shape=[_sds((1, LANES)), _sds((t, d))],
        compiler_params=_params(("arbitrary",)),
    )(y, target)


def _shift_rows(x, s):
    if s == 0:
        return x
    t = x.shape[0]
    rolled = pltpu.roll(x, (-s) % t, 0)
    row = lax.broadcasted_iota(jnp.int32, x.shape, 0)
    return jnp.where((row + s >= 0) & (row + s < t), rolled, 0.0)


def _conv(x, w):
    half = DN_CONV // 2
    acc = None
    for k in range(DN_CONV):
        term = _shift_rows(x, k - half) * w[k:k + 1, :]
        acc = term if acc is None else acc + term
    return acc


def _act_norm(c, do_norm, scale):
    a = _silu(c)
    if not do_norm:
        return a
    return a * lax.rsqrt(jnp.sum(a * a, axis=-1, keepdims=True) + RMS_EPS) * scale


PREP_ROWS = 512
HALO = 8
_KINDS = ((True, DN_HEAD_DIM ** -0.5), (True, 1.0), (False, 1.0))


def _halo_rows(read, i, pr, t):
    lo, hi = i * pr - HALO, (i + 1) * pr + HALO
    parts = []
    if lo < 0:
        parts.append(jnp.zeros((HALO, LANES), F32))
    parts.append(read(max(lo, 0), min(hi, t)))
    if hi > t:
        parts.append(jnp.zeros((HALO, LANES), F32))
    return jnp.concatenate(parts, axis=0) if len(parts) > 1 else parts[0]


def _prep_fwd(proj, conv_w, kind):
    t = proj.shape[0]
    pr = min(PREP_ROWS, t)
    do_norm, scale = _KINDS[kind]
    blk = pl.BlockSpec((t, LANES), lambda j: (0, kind * DN_HEADS + j))

    def body(x_ref, w_ref, o_ref):
        w = w_ref[...]
        for i in range(t // pr):
            xx = _halo_rows(lambda lo, hi: x_ref[lo:hi, :], i, pr, t)
            c = _conv(xx, w)[HALO:HALO + pr, :]
            o_ref[i * pr:(i + 1) * pr, :] = _act_norm(c, do_norm, scale)

    return pl.pallas_call(
        body, name=f"prep_fwd_{kind}", grid=(DN_HEADS,),
        in_specs=[blk, pl.BlockSpec((8, LANES), lambda j: (0, kind * DN_HEADS + j))],
        out_specs=pl.BlockSpec((t, LANES), lambda j: (0, j)), out_shape=_sds((t, DN_WIDTH)),
        compiler_params=_params(("parallel",)),
    )(proj, conv_w)


def _prep_bwd(proj, conv_w, d2, kind):
    t = proj.shape[0]
    pr = min(PREP_ROWS, t)
    do_norm, scale = _KINDS[kind]
    half = DN_CONV // 2
    blk = pl.BlockSpec((t, LANES), lambda j: (0, kind * DN_HEADS + j))
    oblk = pl.BlockSpec((t, LANES), lambda j: (0, j))

    def body(x_ref, w_ref, d_ref, dx_ref, dw_ref):
        w = w_ref[...]
        own = slice(HALO, HALO + pr)
        dw = jnp.zeros((8, LANES), F32)
        for i in range(t // pr):
            xx = _halo_rows(lambda lo, hi: x_ref[lo:hi, :], i, pr, t)
            dn = _halo_rows(lambda lo, hi: d_ref[0, lo:hi, :] + d_ref[1, lo:hi, :], i, pr, t)
            _, vjp = jax.vjp(lambda c: _act_norm(c, do_norm, scale), _conv(xx, w))
            (dc,) = vjp(dn)
            dx = None
            rows = []
            for k in range(DN_CONV):
                term = _shift_rows(dc, half - k) * w[k:k + 1, :]
                dx = term if dx is None else dx + term
                rows.append(jnp.sum(dc[own, :] * _shift_rows(xx, k - half)[own, :], axis=0, keepdims=True))
            dx_ref[i * pr:(i + 1) * pr, :] = dx[own, :].astype(_MXU)
            dw = dw + jnp.concatenate(rows + [jnp.zeros((8 - DN_CONV, LANES), F32)], axis=0)
        dw_ref[...] = dw

    return pl.pallas_call(
        body, name=f"prep_bwd_{kind}", grid=(DN_HEADS,),
        in_specs=[blk, pl.BlockSpec((8, LANES), lambda j: (0, kind * DN_HEADS + j)), pl.BlockSpec((2, t, LANES), lambda j: (0, 0, j))],
        out_specs=[oblk, pl.BlockSpec((8, LANES), lambda j: (0, j))], out_shape=[_sds((t, DN_WIDTH), _MXU), _sds((8, DN_WIDTH))],
        compiler_params=_params(("parallel",)),
    )(proj, conv_w, d2)


def _gb_fn(ba, alog_row, dtb_row):
    c = DN_CHUNK
    lane = lax.broadcasted_iota(jnp.int32, (c, LANES), 1)
    ii = lax.broadcasted_iota(jnp.int32, (c, c), 0)
    jj = lax.broadcasted_iota(jnp.int32, (c, c), 1)
    beta = jax.nn.sigmoid(ba)
    g = -jnp.exp(alog_row) * _softplus(ba + dtb_row)
    g = jnp.where((lane >= N_HD) & (lane < 2 * N_HD), g, 0.0)
    gc_fwd = _hnn((ii >= jj).astype(F32), g)
    gc_rev = _hnn((ii <= jj).astype(F32), g)
    gc = jnp.where(lane < N_HD + DN_HEADS, gc_fwd, gc_rev)
    return jnp.where(lane < N_HD, beta, jnp.where(lane < 2 * N_HD, gc, 0.0))


def _gb_fwd(ba, alog_row, dtb_row):
    t = ba.shape[0]
    n = ROW_TILE // DN_CHUNK

    def body(ba_ref, a_ref, d_ref, o_ref):
        for c in range(n):
            rows = slice(c * DN_CHUNK, (c + 1) * DN_CHUNK)
            o_ref[rows, :] = _gb_fn(ba_ref[rows, :], a_ref[...], d_ref[...])

    return pl.pallas_call(
        body, name="gates_fwd", grid=(t // ROW_TILE,), in_specs=[_rows(LANES), _full((1, LANES)), _full((1, LANES))],
        out_specs=_rows(LANES), out_shape=_sds((t, LANES)), compiler_params=_params(("parallel",)),
    )(ba, alog_row, dtb_row)


def _gb_bwd(ba, alog_row, dtb_row, d_a, d_b):
    t = ba.shape[0]
    n = ROW_TILE // DN_CHUNK

    def body(ba_ref, a_ref, d_ref, da_ref, db_ref, dba_ref, dal_ref, ddt_ref):
        dal = jnp.zeros((1, LANES), F32)
        ddt = jnp.zeros((1, LANES), F32)
        for c in range(n):
            rows = slice(c * DN_CHUNK, (c + 1) * DN_CHUNK)
            _, vjp = jax.vjp(_gb_fn, ba_ref[rows, :], a_ref[...], d_ref[...])
            dba, da, dd = vjp(da_ref[rows, :] + db_ref[rows, :])
            dba_ref[rows, :] = dba.astype(_MXU)
            dal = dal + da
            ddt = ddt + dd
        _accumulate(dal_ref, dal, pl.program_id(0))
        _accumulate(ddt_ref, ddt, pl.program_id(0))

    return pl.pallas_call(
        body, name="gates_bwd", grid=(t // ROW_TILE,),
        in_specs=[_rows(LANES), _full((1, LANES)), _full((1, LANES)), _rows(LANES), _rows(LANES)],
        out_specs=[_rows(LANES), _full((1, LANES)), _full((1, LANES))],
        out_shape=[_sds((t, LANES), _MXU), _sds((1, LANES)), _sds((1, LANES))], compiler_params=_params(("arbitrary",)),
    )(ba, alog_row, dtb_row, d_a, d_b)


def _dn_decay(gcc, gcr, sgn):
    c = DN_CHUNK
    ii = lax.broadcasted_iota(jnp.int32, (c, c), 0)
    jj = lax.broadcasted_iota(jnp.int32, (c, c), 1)
    d = (ii - jj) * sgn
    lower = d >= 0
    return jnp.where(lower, jnp.exp(jnp.where(lower, gcc - gcr, 0.0)), 0.0), d > 0


def _dn_a(k, beta, gcc, gcr, sgn):
    decay, strict = _dn_decay(gcc, gcr, sgn)
    return jnp.where(strict, beta * _bnt(k, k) * decay, 0.0)


def _dn_group(q, k, v, beta, gcc, gcr, sgn):
    n = DN_GROUP
    ii = lax.broadcasted_iota(jnp.int32, (n, n), 0)
    jj = lax.broadcasted_iota(jnp.int32, (n, n), 1)
    same = (ii & -DN_CHUNK) == (jj & -DN_CHUNK)
    d = (ii - jj) * sgn
    lower = same & (d >= 0)
    decay = jnp.where(lower, jnp.exp(jnp.where(lower, gcc - gcr, 0.0)), 0.0)
    a = jnp.where(same & (d > 0), beta * _bnt(k, k) * decay, 0.0)
    t_inv = _inv_unit(a, DN_CHUNK)
    u = _bnn(t_inv, v * beta)
    w = _bnn(t_inv, k * (beta * jnp.exp(gcc)))
    return u, w, _bnt(q, k) * decay, t_inv


def _dn_local(t_inv, q, k, v, beta, gcc, gcr, sgn):
    c = DN_CHUNK
    decay, _ = _dn_decay(gcc, gcr, sgn)
    eg = jnp.exp(gcc)
    u = _bnn(t_inv, v * beta)
    w = _bnn(t_inv, k * (beta * eg))
    qk = _bnt(q, k) * decay
    qd = q * eg
    last = jnp.where(sgn > 0, c - 1, 0)
    onehot = (lax.broadcasted_iota(jnp.int32, (c, 1), 0) == last).astype(F32)
    gl = jnp.sum(gcc * onehot, axis=0, keepdims=True)
    kd = k * jnp.exp(gl - gcc)
    egl = jnp.broadcast_to(jnp.exp(gl), (1, LANES))
    return u, w, qk, qd, kd, egl


def _hd_sign(hd):
    return jnp.where(hd < DN_HEADS, 1, -1).astype(jnp.int32)


def _head_of(hd):
    return jnp.where(hd < DN_HEADS, hd, hd - DN_HEADS)


def _dir_of(hd):
    return jnp.where(hd < DN_HEADS, 0, 1)


def _dn_specs(t):
    nl = LOCAL_ROWS // DN_CHUNK
    wide = pl.BlockSpec((1, LOCAL_ROWS, LANES), lambda hd, i: (hd, i, 0))
    half = pl.BlockSpec((1, LOCAL_ROWS, DN_CHUNK), lambda hd, i: (hd, i, 0))
    col = pl.BlockSpec((1, LOCAL_ROWS, 1), lambda hd, i: (hd, i, 0))
    row = pl.BlockSpec((1, nl, 1, DN_CHUNK), lambda hd, i: (hd, i, 0, 0))
    egl = pl.BlockSpec((1, nl, 1, LANES), lambda hd, i: (hd, i, 0, 0))
    return wide, half, col, row, egl


def _qkv_specs():
    return [pl.BlockSpec((LOCAL_ROWS, LANES), lambda hd, i: (i, _head_of(hd)))] * 3


def _maybe_carrying(carry, body, name, grid, operands, in_specs, out_specs, out_shape):
    extra_scratch = []
    if carry is not None:
        kind, operand = carry
        body, more_in, more_out, more_shape, extra_scratch = _carried(kind, operand, body, len(operands), len(out_shape), grid)
        operands, in_specs = operands + [operand], in_specs + more_in
        out_specs, out_shape = out_specs + more_out, out_shape + more_shape
    sem = ("arbitrary",) * len(grid) if carry is not None else ("parallel",) * len(grid)
    return pl.pallas_call(
        body, name=name, grid=grid, in_specs=in_specs, out_specs=out_specs, out_shape=out_shape,
        scratch_shapes=extra_scratch, compiler_params=_params(sem),
    )(*operands)


def _dn_local_fwd(q, k, v, beta_c, gc_c, gc_r, carry=None):
    t = q.shape[0]
    nc = t // DN_CHUNK
    nl = LOCAL_ROWS // DN_CHUNK
    wide, half, col, row, egl = _dn_specs(t)

    ng = LOCAL_ROWS // DN_GROUP
    per = DN_GROUP // DN_CHUNK
    grow = pl.BlockSpec((1, ng, 1, DN_GROUP), lambda hd, i: (hd, i, 0, 0))

    def body(q_ref, k_ref, v_ref, b_ref, gc_ref, gg_ref, u_ref, w_ref, qk_ref, qd_ref, kd_ref, egl_ref, t_ref):
        sgn = _hd_sign(pl.program_id(0))
        last = jnp.where(sgn > 0, DN_CHUNK - 1, 0)
        onehot = (lax.broadcasted_iota(jnp.int32, (DN_CHUNK, 1), 0) == last).astype(F32)
        groups = lambda a: a.reshape((ng, DN_GROUP) + a.shape[1:])
        q_all, k_all, gcc_all = q_ref[...], k_ref[...], gc_ref[0]
        u, w, qk, t_inv = jax.vmap(functools.partial(_dn_group, sgn=sgn))(
            groups(q_all), groups(k_all), groups(v_ref[...]), groups(b_ref[0]), groups(gcc_all), gg_ref[0])
        u_ref[0] = u.reshape(LOCAL_ROWS, LANES)
        w_ref[0] = w.reshape(LOCAL_ROWS, LANES)
        qd_ref[0] = q_all * jnp.exp(gcc_all)
        for gi in range(ng):
            for c in range(per):
                blk = slice(c * DN_CHUNK, (c + 1) * DN_CHUNK)
                rows = slice(gi * DN_GROUP + c * DN_CHUNK, gi * DN_GROUP + (c + 1) * DN_CHUNK)
                qk_ref[0, rows, :] = qk[gi, blk, blk]
                t_ref[0, rows, :] = t_inv[gi, blk, blk]
                gl = jnp.sum(gcc_all[rows, :] * onehot, axis=0, keepdims=True)
                kd_ref[0, rows, :] = k_all[rows, :] * jnp.exp(gl - gcc_all[rows, :])
                egl_ref[0, gi * per + c] = jnp.broadcast_to(jnp.exp(gl), (1, LANES))

    big = _sds((N_HD, t, LANES))
    small = _sds((N_HD, t, DN_CHUNK))
    operands = [q, k, v, beta_c, gc_c, gc_r.reshape(N_HD, t // DN_GROUP, 1, DN_GROUP)]
    return _maybe_carrying(
        carry, body, "dn_local_fwd", (N_HD, t // LOCAL_ROWS), operands, _qkv_specs() + [col, col, grow],
        [wide, wide, half, wide, wide, egl, half], [big, big, small, big, big, _sds((N_HD, nc, 1, LANES)), small])


def _dn_local_bwd(q, k, v, beta_c, gc_c, gc_r, t_inv, du, dw, dqk, dqd, dkd, degl, carry=None):
    t = q.shape[0]
    nc = t // DN_CHUNK
    nl = LOCAL_ROWS // DN_CHUNK
    wide, half, col, row, egl = _dn_specs(t)
    dspec = pl.BlockSpec((1, LOCAL_ROWS, LANES), lambda hd, i: (_dir_of(hd), i, _head_of(hd)))

    def body(q_ref, k_ref, v_ref, b_ref, gc_ref, gr_ref, t_ref, du_ref, dw_ref, dqk_ref, dqd_ref, dkd_ref, degl_ref,
             dq_ref, dk_ref, dv_ref, db_ref, dgc_ref, dgr_ref):
        sgn = _hd_sign(pl.program_id(0))

        def chunk_bwd(tinv, q, k, v, beta, gcc, gcr, du, dw, dqk, dqd, dkd, degl):
            _, vjp = jax.vjp(functools.partial(_dn_local, sgn=sgn), tinv, q, k, v, beta, gcc, gcr)
            dt, dq, dk, dv, db, dgc, dgr = vjp((du, dw, dqk, dqd, dkd, degl))
            _, vjp_a = jax.vjp(functools.partial(_dn_a, sgn=sgn), k, beta, gcc, gcr)
            dk2, db2, dgc2, dgr2 = vjp_a(_inv_unit_t(tinv, dt))
            return dq, dk + dk2, dv, db + db2, dgc + dgc2, dgr + dgr2

        chunks = lambda a: a.reshape((nl, DN_CHUNK) + a.shape[1:])
        dq, dk, dv, db, dgc, dgr = jax.vmap(chunk_bwd)(
            chunks(t_ref[0]), chunks(q_ref[...]), chunks(k_ref[...]), chunks(v_ref[...]), chunks(b_ref[0]), chunks(gc_ref[0]),
            gr_ref[0], chunks(du_ref[0]), chunks(dw_ref[0]), chunks(dqk_ref[0]), chunks(dqd_ref[0]), chunks(dkd_ref[0]),
            degl_ref[0])
        dq_ref[0] = dq.reshape(LOCAL_ROWS, LANES)
        dk_ref[0] = dk.reshape(LOCAL_ROWS, LANES)
        dv_ref[0] = dv.reshape(LOCAL_ROWS, LANES)
        db_ref[0] = db.reshape(LOCAL_ROWS, 1)
        dgc_ref[0] = dgc.reshape(LOCAL_ROWS, 1)
        dgr_ref[0] = dgr

    per_dir = _sds((2, t, DN_WIDTH))
    return _maybe_carrying(
        carry, body, "dn_local_bwd", (N_HD, t // LOCAL_ROWS), [q, k, v, beta_c, gc_c, gc_r, t_inv, du, dw, dqk, dqd, dkd, degl],
        _qkv_specs() + [col, col, row, half, wide, wide, half, wide, wide, egl], [dspec, dspec, dspec, col, col, row],
        [per_dir, per_dir, per_dir, _sds((N_HD, t, 1)), _sds((N_HD, t, 1)), _sds((N_HD, nc, 1, DN_CHUNK))])


REC_HEADS = 8
REC_GROUPS = N_HD // REC_HEADS
REC_FWD_GROUPS = DN_HEADS // REC_HEADS


def _rec_specs(time_block):
    nr = REC_ROWS // DN_CHUNK
    wide = pl.BlockSpec((REC_HEADS, REC_ROWS, LANES), lambda g, b: (g, time_block(g, b), 0))
    half = pl.BlockSpec((REC_HEADS, REC_ROWS, DN_CHUNK), lambda g, b: (g, time_block(g, b), 0))
    egl = pl.BlockSpec((REC_HEADS, nr, 1, LANES), lambda g, b: (g, time_block(g, b), 0, 0))
    state = pl.BlockSpec((REC_HEADS, nr, DN_HEAD_DIM, DN_HEAD_DIM), lambda g, b: (g, time_block(g, b), 0, 0))
    return wide, half, egl, state


def _rec_head_cols(g):
    return jnp.where(g < REC_FWD_GROUPS, g, g - REC_FWD_GROUPS)


def _dn_rec_fwd(u, w, qk, qd, kd, egl):
    t = u.shape[1]
    nb = t // REC_ROWS
    nr = REC_ROWS // DN_CHUNK
    nc = t // DN_CHUNK

    def time_block(g, b):
        return jnp.where(g < REC_FWD_GROUPS, b, nb - 1 - b)

    wide, half, egl_spec, state = _rec_specs(time_block)
    o_spec = pl.BlockSpec((1, REC_ROWS, REC_HEADS * LANES),
                          lambda g, b: (jnp.where(g < REC_FWD_GROUPS, 0, 1), time_block(g, b), _rec_head_cols(g)))

    def body(u_ref, w_ref, qk_ref, qd_ref, kd_ref, egl_ref, o_ref, vn_ref, s_ref, s_scr):
        fwd = pl.program_id(0) < REC_FWD_GROUPS

        @pl.when(pl.program_id(1) == 0)
        def _():
            s_scr[...] = jnp.zeros_like(s_scr)

        def run(order):
            heads = range(REC_HEADS)
            s = [s_scr[j] for j in heads]
            for ce in order:
                rows = slice(ce * DN_CHUNK, (ce + 1) * DN_CHUNK)
                vn = [u_ref[j, rows, :] - _bnn(w_ref[j, rows, :], s[j]) for j in heads]
                o = [_bnn(qd_ref[j, rows, :], s[j]) + _bnn(qk_ref[j, rows, :], vn[j]) for j in heads]
                nxt = [s[j] * egl_ref[j, ce] + _btn(kd_ref[j, rows, :], vn[j]) for j in heads]
                for j in heads:
                    s_ref[j, ce] = s[j]
                    vn_ref[j, rows, :] = vn[j]
                    o_ref[0, rows, j * LANES:(j + 1) * LANES] = o[j]
                s = nxt
            for j in heads:
                s_scr[j] = s[j]

        pl.when(fwd)(lambda: run(range(nr)))
        pl.when(jnp.logical_not(fwd))(lambda: run(range(nr - 1, -1, -1)))

    return pl.pallas_call(
        body, name="dn_rec_fwd", grid=(REC_GROUPS, nb), in_specs=[wide, wide, half, wide, wide, egl_spec],
        out_specs=[o_spec, wide, state],
        out_shape=[_sds((2, t, DN_WIDTH)), _sds((N_HD, t, LANES)), _sds((N_HD, nc, DN_HEAD_DIM, DN_HEAD_DIM))],
        scratch_shapes=[pltpu.VMEM((REC_HEADS, DN_HEAD_DIM, DN_HEAD_DIM), F32)],
        compiler_params=_params(("parallel", "arbitrary")),
    )(u, w, qk, qd, kd, egl)


def _dn_rec_bwd(w, qk, qd, kd, egl, vn, states, do):
    t = w.shape[1]
    nb = t // REC_ROWS
    nr = REC_ROWS // DN_CHUNK
    nc = t // DN_CHUNK

    def time_block(g, b):
        return jnp.where(g < REC_FWD_GROUPS, nb - 1 - b, b)

    wide, half, egl_spec, state = _rec_specs(time_block)
    do_spec = pl.BlockSpec((REC_ROWS, REC_HEADS * LANES), lambda g, b: (time_block(g, b), _rec_head_cols(g)))

    def body(w_ref, qk_ref, qd_ref, kd_ref, egl_ref, vn_ref, s_ref, do_ref,
             du_ref, dw_ref, dqk_ref, dqd_ref, dkd_ref, degl_ref, ds_scr):
        fwd = pl.program_id(0) < REC_FWD_GROUPS

        @pl.when(pl.program_id(1) == 0)
        def _():
            ds_scr[...] = jnp.zeros_like(ds_scr)

        def run(order):
            heads = range(REC_HEADS)
            ds = [ds_scr[j] for j in heads]
            for ce in order:
                rows = slice(ce * DN_CHUNK, (ce + 1) * DN_CHUNK)
                s = [s_ref[j, ce] for j in heads]
                do_c = [do_ref[rows, j * LANES:(j + 1) * LANES] for j in heads]
                vn_c = [vn_ref[j, rows, :] for j in heads]
                dvn = [_btn(qk_ref[j, rows, :], do_c[j]) + _bnn(kd_ref[j, rows, :], ds[j]) for j in heads]
                nxt = [ds[j] * egl_ref[j, ce] + _btn(qd_ref[j, rows, :], do_c[j]) - _btn(w_ref[j, rows, :], dvn[j])
                       for j in heads]
                for j in heads:
                    du_ref[j, rows, :] = dvn[j]
                    dw_ref[j, rows, :] = -_bnt(dvn[j], s[j])
                for j in heads:
                    dqk_ref[j, rows, :] = _bnt(do_c[j], vn_c[j])
                    dqd_ref[j, rows, :] = _bnt(do_c[j], s[j])
                for j in heads:
                    dkd_ref[j, rows, :] = _bnt(vn_c[j], ds[j])
                    degl_ref[j, ce] = jnp.sum(s[j] * ds[j], axis=0, keepdims=True)
                ds = nxt
            for j in heads:
                ds_scr[j] = ds[j]

        pl.when(fwd)(lambda: run(range(nr - 1, -1, -1)))
        pl.when(jnp.logical_not(fwd))(lambda: run(range(nr)))

    big = _sds((N_HD, t, LANES))
    return pl.pallas_call(
        body, name="dn_rec_bwd", grid=(REC_GROUPS, nb), in_specs=[wide, half, wide, wide, egl_spec, wide, state, do_spec],
        out_specs=[wide, wide, half, wide, wide, egl_spec],
        out_shape=[big, big, _sds((N_HD, t, DN_CHUNK)), big, big, _sds((N_HD, nc, 1, LANES))],
        scratch_shapes=[pltpu.VMEM((REC_HEADS, DN_HEAD_DIM, DN_HEAD_DIM), F32)],
        compiler_params=_params(("parallel", "arbitrary")),
    )(w, qk, qd, kd, egl, vn, states, do)


def _post_fn(of, ob, z, gain):
    o = of + ob
    return o * lax.rsqrt(jnp.mean(o * o, axis=-1, keepdims=True) + RMS_EPS) * gain * _silu(z)


def _post_specs():
    o_spec = [pl.BlockSpec((1, ROW_TILE, DN_WIDTH), functools.partial(lambda i, d: (d, i, 0), d=d)) for d in (0, 1)]
    return o_spec, _cols(DN_WIDTH, C_Z), _rows(DN_WIDTH), _full((1, LANES))


def _post_fwd(o2, proj, gain):
    t = proj.shape[0]
    o_spec, z_spec, wide, gain_spec = _post_specs()

    def body(of_ref, ob_ref, z_ref, g_ref, out_ref):
        for h in range(DN_HEADS):
            cols = slice(h * LANES, (h + 1) * LANES)
            out_ref[:, cols] = _post_fn(of_ref[0, :, cols], ob_ref[0, :, cols], z_ref[:, cols], g_ref[...]).astype(_MXU)

    return pl.pallas_call(
        body, name="post_fwd", grid=(t // ROW_TILE,), in_specs=o_spec + [z_spec, gain_spec], out_specs=wide,
        out_shape=_sds((t, DN_WIDTH), _MXU), compiler_params=_params(("parallel",)),
    )(o2, o2, proj, gain)


def _post_bwd(o2, proj, gain, dout):
    t = proj.shape[0]
    o_spec, z_spec, wide, gain_spec = _post_specs()

    def body(of_ref, ob_ref, z_ref, g_ref, d_ref, do_ref, dz_ref, dg_ref):
        dg_sum = jnp.zeros((1, LANES), F32)
        for h in range(DN_HEADS):
            cols = slice(h * LANES, (h + 1) * LANES)
            _, vjp = jax.vjp(_post_fn, of_ref[0, :, cols], ob_ref[0, :, cols], z_ref[:, cols], g_ref[...])
            do, _, dz, dg = vjp(d_ref[:, cols])
            do_ref[:, cols] = do
            dz_ref[:, cols] = dz.astype(_MXU)
            dg_sum = dg_sum + dg
        _accumulate(dg_ref, dg_sum, pl.program_id(0))

    return pl.pallas_call(
        body, name="post_bwd", grid=(t // ROW_TILE,), in_specs=o_spec + [z_spec, gain_spec, wide],
        out_specs=[wide, wide, gain_spec], out_shape=[_sds((t, DN_WIDTH)), _sds((t, DN_WIDTH), _MXU), _sds((1, LANES))],
        compiler_params=_params(("arbitrary",)),
    )(o2, o2, proj, gain, dout)


def _rope(x, cos, sin):
    lane = lax.broadcasted_iota(jnp.int32, x.shape, 1)
    first = (lane & (SW_HEAD_DIM - 1)) < SW_HEAD_DIM // 2
    rot = jnp.where(first, -pltpu.roll(x, LANES - SW_HEAD_DIM // 2, 1), pltpu.roll(x, SW_HEAD_DIM // 2, 1))
    return x * cos + rot * sin


def _rope_apply(q, k, q_cols, k_cols, cos, sin, name, dtype):
    t = cos.shape[0]
    qw, kw = SW_HEADS * SW_HEAD_DIM, SW_KV_WIDTH

    def body(q_ref, k_ref, c_ref, s_ref, qo_ref, ko_ref):
        c, s = c_ref[...], s_ref[...]
        for j in range(qw // LANES):
            cols = slice(j * LANES, (j + 1) * LANES)
            qo_ref[:, cols] = _rope(q_ref[:, cols], c, s).astype(dtype)
        for j in range(kw // LANES):
            cols = slice(j * LANES, (j + 1) * LANES)
            ko_ref[:, cols] = _rope(k_ref[:, cols], c, s).astype(dtype)

    return pl.pallas_call(
        body, name=name, grid=(t // ROW_TILE,), in_specs=[_cols(qw, q_cols), _cols(kw, k_cols), _rows(LANES), _rows(LANES)],
        out_specs=[_rows(qw), _rows(kw)], out_shape=[_sds((t, qw), dtype), _sds((t, kw), dtype)],
        compiler_params=_params(("parallel",)),
    )(q, k, cos, sin)


def _attn_core(qs, kb, vb, sink, mask):
    s = _bnt(qs, kb) * (SW_HEAD_DIM ** -0.5)
    s = jnp.where(mask, s, -1e30)
    m = lax.stop_gradient(jnp.maximum(jnp.max(s, axis=1, keepdims=True), sink))
    e = jnp.exp(s - m)
    den = jnp.sum(e, axis=1, keepdims=True) + jnp.exp(sink - m)
    return _bnn(e / den, vb)


def _band_mask(n, nb):
    rows = SW_GROUP * SW_BLOCK
    i = lax.broadcasted_iota(jnp.int32, (rows, 3 * SW_BLOCK), 0) & (SW_BLOCK - 1)
    j = lax.broadcasted_iota(jnp.int32, (rows, 3 * SW_BLOCK), 1)
    near = (j - i >= 0) & (j - i <= 2 * SW_BLOCK)
    lo = jnp.where(n == 0, SW_BLOCK, 0)
    hi = jnp.where(n == nb - 1, 2 * SW_BLOCK, 3 * SW_BLOCK)
    return near & (j >= lo) & (j < hi)


def _band_specs(nb, v_cols):
    def spec(width, base, shift):
        return pl.BlockSpec((SW_BLOCK, width), lambda n: (jnp.clip(n + shift, 0, nb - 1), base // width))
    k_specs = [spec(SW_KV_WIDTH, 0, s) for s in (-1, 0, 1)]
    v_specs = [spec(SW_KV_WIDTH, v_cols, s) for s in (-1, 0, 1)]
    return k_specs, v_specs


def _head_cols(kv, g):
    h = kv * SW_GROUP + g
    return slice(h * SW_HEAD_DIM, (h + 1) * SW_HEAD_DIM)


def _kv_batches(q_ref, kb, vb, s_ref):
    kvs = range(SW_KV_HEADS)
    cols = lambda kv: slice(kv * SW_HEAD_DIM, (kv + 1) * SW_HEAD_DIM)
    qs = jnp.stack([jnp.concatenate([q_ref[:, _head_cols(kv, g)] for g in range(SW_GROUP)], axis=0) for kv in kvs])
    sinks = jnp.stack([jnp.concatenate([jnp.broadcast_to(s_ref[kv * SW_GROUP + g], (SW_BLOCK, 1)) for g in range(SW_GROUP)],
                                       axis=0) for kv in kvs])
    return qs, jnp.stack([kb[:, cols(kv)] for kv in kvs]), jnp.stack([vb[:, cols(kv)] for kv in kvs]), sinks


def _attn_fwd(qr, kr, proj, sinks):
    t = qr.shape[0]
    nb = t // SW_BLOCK
    qw = SW_HEADS * SW_HEAD_DIM
    k_specs, v_specs = _band_specs(nb, C_VSW)
    q_spec = pl.BlockSpec((SW_BLOCK, qw), lambda n: (n, 0))

    def body(q_ref, k0, k1, k2, v0, v1, v2, s_ref, o_ref):
        mask = _band_mask(pl.program_id(0), nb)
        kb = jnp.concatenate([k0[...], k1[...], k2[...]], axis=0)
        vb = jnp.concatenate([v0[...], v1[...], v2[...]], axis=0)
        qs, kbs, vbs, sinks_ = _kv_batches(q_ref, kb, vb, s_ref)
        o = jax.vmap(functools.partial(_attn_core, mask=mask))(qs, kbs, vbs, sinks_)
        for kv in range(SW_KV_HEADS):
            for g in range(SW_GROUP):
                o_ref[:, _head_cols(kv, g)] = o[kv, g * SW_BLOCK:(g + 1) * SW_BLOCK, :].astype(_MXU)

    return pl.pallas_call(
        body, name="attn_fwd", grid=(nb,), in_specs=[q_spec] + k_specs + v_specs + [_full((SW_HEADS, 1, 1))],
        out_specs=q_spec, out_shape=_sds((t, qw), _MXU), compiler_params=_params(("parallel",)),
    )(qr, kr, kr, kr, proj, proj, proj, sinks)


def _attn_bwd(qr, kr, proj, sinks, do):
    t = qr.shape[0]
    nb = t // SW_BLOCK
    qw = SW_HEADS * SW_HEAD_DIM
    k_specs, v_specs = _band_specs(nb, C_VSW)
    q_spec = pl.BlockSpec((SW_BLOCK, qw), lambda n: (n, 0))
    part = pl.BlockSpec((1, 3 * SW_BLOCK, SW_KV_WIDTH), lambda n: (n, 0, 0))

    def body(q_ref, k0, k1, k2, v0, v1, v2, s_ref, do_ref, dq_ref, dk_ref, dv_ref, ds_ref):
        mask = _band_mask(pl.program_id(0), nb)
        kb = jnp.concatenate([k0[...], k1[...], k2[...]], axis=0).astype(F32)
        vb = jnp.concatenate([v0[...], v1[...], v2[...]], axis=0)

        @pl.when(pl.program_id(0) == 0)
        def _():
            ds_ref[...] = jnp.zeros_like(ds_ref)

        qs, kbs, vbs, sinks_ = _kv_batches(q_ref, kb, vb, s_ref)
        dos = jnp.stack([jnp.concatenate([do_ref[:, _head_cols(kv, g)] for g in range(SW_GROUP)], axis=0)
                         for kv in range(SW_KV_HEADS)])

        def head_bwd(q_, k_, v_, sink_, do_):
            _, vjp = jax.vjp(functools.partial(_attn_core, mask=mask), q_, k_, v_, sink_)
            return vjp(do_)

        dqs, dkb, dvb, dsink = jax.vmap(head_bwd)(qs.astype(F32), kbs, vbs, sinks_, dos)
        for kv in range(SW_KV_HEADS):
            kvc = slice(kv * SW_HEAD_DIM, (kv + 1) * SW_HEAD_DIM)
            dk_ref[0, :, kvc] = dkb[kv]
            dv_ref[0, :, kvc] = dvb[kv]
            for g in range(SW_GROUP):
                rows = slice(g * SW_BLOCK, (g + 1) * SW_BLOCK)
                dq_ref[:, _head_cols(kv, g)] = dqs[kv, rows, :]
                ds_ref[kv * SW_GROUP + g] += jnp.sum(dsink[kv, rows, :], axis=0, keepdims=True)

    parts = _sds((nb, 3 * SW_BLOCK, SW_KV_WIDTH))
    return pl.pallas_call(
        body, name="attn_bwd", grid=(nb,), in_specs=[q_spec] + k_specs + v_specs + [_full((SW_HEADS, 1, 1)), q_spec],
        out_specs=[q_spec, part, part, _full((SW_HEADS, 1, 1))], out_shape=[_sds((t, qw)), parts, parts, _sds((SW_HEADS, 1, 1))],
        compiler_params=_params(("arbitrary",)),
    )(qr, kr, kr, kr, proj, proj, proj, sinks, do)


def _band_sum(parts, name, dtype):
    nb = parts.shape[0]
    w = parts.shape[2]

    def spec(shift, slot):
        return pl.BlockSpec((1, SW_BLOCK, w), lambda m: (jnp.clip(m + shift, 0, nb - 1), slot, 0))

    def body(prev_ref, own_ref, next_ref, o_ref):
        m = pl.program_id(0)
        total = own_ref[0] + jnp.where(m > 0, prev_ref[0], 0.0) + jnp.where(m < nb - 1, next_ref[0], 0.0)
        o_ref[...] = total.astype(dtype)

    return pl.pallas_call(
        body, name=name, grid=(nb,), in_specs=[spec(-1, 2), spec(0, 1), spec(1, 0)],
        out_specs=pl.BlockSpec((SW_BLOCK, w), lambda m: (m, 0)), out_shape=_sds((nb * SW_BLOCK, w), dtype),
        compiler_params=_params(("parallel",)),
    )(parts, parts, parts)


def _gate_layouts(gbo):
    t = gbo.shape[0]
    beta_c = gbo[:, :N_HD].T.reshape(N_HD, t, 1)
    gc = gbo[:, N_HD:2 * N_HD].T
    return beta_c, gc.reshape(N_HD, t, 1), gc.reshape(N_HD, t // DN_CHUNK, 1, DN_CHUNK)


def _gate_layouts_t(dbeta_c, dgc_c, dgc_r):
    t = dbeta_c.shape[1]
    pad = jnp.zeros((t, LANES - 2 * N_HD), F32)
    none = jnp.zeros((t, N_HD), F32)
    d_a = jnp.concatenate([dbeta_c.reshape(N_HD, t).T, dgc_c.reshape(N_HD, t).T, pad], axis=1)
    d_b = jnp.concatenate([none, dgc_r.reshape(N_HD, t).T, pad], axis=1)
    return d_a, d_b


def _layer_fwd(x, xb, w, cos, sin, carry=None):
    proj = _mm(xb, w["wm"], "nn", name="proj")
    ba = _mm(xb, w["wba"], "nn", name="proj_gates")
    qn, kn, vv = [_prep_fwd(proj, w["conv"], kind) for kind in range(3)]
    gbo = _gb_fwd(ba, w["alog"], w["dtb"])
    beta_c, gc_c, gc_r = _gate_layouts(gbo)
    u, wk, qk, qd, kd, egl, tinv, *carried = _dn_local_fwd(qn, kn, vv, beta_c, gc_c, gc_r, carry=carry)
    o2, vn, states = _dn_rec_fwd(u, wk, qk, qd, kd, egl)
    o_dn = _post_fwd(o2, proj, w["dnw"])
    qr, kr = _rope_apply(proj, proj, C_QSW, C_KSW, cos, sin, "rope_fwd", _MXU)
    o_sw = _attn_fwd(qr, kr, proj, w["sinks"])
    ya = _mm(o_dn, w["wa"], "nn", name="branch_a")
    yb = _mm(o_sw, w["wb"], "nn", name="branch_b")
    merged = _merge_fwd(proj, ya, yb)
    mix = _mm(merged, w["wo"], "nn", name="mix_out")
    x1, x1b = _ln_fwd(x, mix, w["ln1g"], w["ln1b"], "ln1_fwd")
    gu = _mm(x1b, w["wgu"], "nn", name="ffn_up")
    h = _swiglu_fwd(gu)
    f = _mm(h, w["wd"], "nn", name="ffn_down")
    x2, x2b = _ln_fwd(x1, f, w["ln2g"], w["ln2b"], "ln2_fwd")
    res = dict(x=x, xb=xb, proj=proj, ba=ba, qn=qn, kn=kn, vv=vv, gbo=gbo, wk=wk, qk=qk, qd=qd, kd=kd, egl=egl, tinv=tinv, vn=vn,
               states=states, o2=o2, o_dn=o_dn, qr=qr, kr=kr, o_sw=o_sw, ya=ya, yb=yb, merged=merged, mix=mix, x1=x1, x1b=x1b,
               gu=gu, h=h, f=f)
    return x2, x2b, res, (carried[0] if carried else None)


def _layer_bwd(dx2, w, r, cos, sin, carry=None):
    dx1, df, dln2g, dln2b = _ln_bwd(r["x1"], r["f"], w["ln2g"], w["ln2b"], dx2, "ln2_bwd")
    dh = _mm(df, w["wd"], "nt", name="d_ffn_hidden")
    dwd = _mm(r["h"], df, "tn", name="dw_ffn_down", out_dtype=_MXU)
    dgu = _swiglu_bwd(r["gu"], dh)
    dwgu = _mm(r["x1b"], dgu, "tn", name="dw_ffn_up", out_dtype=_MXU)
    dx1 = _mm(dgu, w["wgu"], "nt", name="dx_ffn", add=dx1)
    dx, dmix, dln1g, dln1b = _ln_bwd(r["x"], r["mix"], w["ln1g"], w["ln1b"], dx1, "ln1_bwd")
    dmerged = _mm(dmix, w["wo"], "nt", name="d_merged")
    dwo = _mm(r["merged"], dmix, "tn", name="dw_mix_out", out_dtype=_MXU)
    dga, dgb, dya, dyb = _merge_bwd(r["proj"], r["ya"], r["yb"], dmerged)
    dwa = _mm(r["o_dn"], dya, "tn", name="dw_branch_a", out_dtype=_MXU)
    do_dn = _mm(dya, w["wa"], "nt", name="d_branch_a")
    dwb = _mm(r["o_sw"], dyb, "tn", name="dw_branch_b", out_dtype=_MXU)
    do_sw = _mm(dyb, w["wb"], "nt", name="d_branch_b")
    do, dz, ddnw = _post_bwd(r["o2"], r["proj"], w["dnw"], do_dn)
    du, dwk, dqk, dqd, dkd, degl = _dn_rec_bwd(r["wk"], r["qk"], r["qd"], r["kd"], r["egl"], r["vn"], r["states"], do)
    beta_c, gc_c, gc_r = _gate_layouts(r["gbo"])
    dq3, dk3, dv3, dbeta_c, dgc_c, dgc_r, *carried = _dn_local_bwd(r["qn"], r["kn"], r["vv"], beta_c, gc_c, gc_r, r["tinv"],
                                                                   du, dwk, dqk, dqd, dkd, degl, carry=carry)
    dqkv, dconv = zip(*[_prep_bwd(r["proj"], w["conv"], d2, kind) for kind, d2 in enumerate((dq3, dk3, dv3))])
    dconv = jnp.concatenate(dconv, axis=1)
    d_a, d_b = _gate_layouts_t(dbeta_c, dgc_c, dgc_r)
    dba, dalog, ddtb = _gb_bwd(r["ba"], w["alog"], w["dtb"], d_a, d_b)
    dqr, dkparts, dvparts, dsinks = _attn_bwd(r["qr"], r["kr"], r["proj"], w["sinks"], do_sw)
    dkr = _band_sum(dkparts, "attn_dk_sum", F32)
    dv = _band_sum(dvparts, "attn_dv_sum", _MXU)
    dq_sw, dk_sw = _rope_apply(dqr, dkr, 0, 0, cos, -sin, "rope_bwd", _MXU)
    dproj = jnp.concatenate([*dqkv, dz, dq_sw, dga, dgb, dk_sw, dv], axis=1)
    dwm = _mm(r["xb"], dproj, "tn", name="dw_proj", out_dtype=_MXU)
    dwba = _mm(r["xb"], dba, "tn", name="dw_proj_gates", out_dtype=_MXU)
    dx = _mm(dproj, w["wm"], "nt", name="dx_proj", add=dx)
    dx = _mm(dba, w["wba"], "nt", name="dx_proj_gates", add=dx)
    grads = dict(wm=dwm, wba=dwba, conv=dconv, alog=dalog, dtb=ddtb, dnw=ddnw, sinks=dsinks, wa=dwa, wb=dwb, wo=dwo,
                 ln1g=dln1g, ln1b=dln1b, wgu=dwgu, wd=dwd, ln2g=dln2g, ln2b=dln2b)
    return dx, grads, (carried[0] if carried else None)


def _rope_tables(t):
    half = SW_HEAD_DIM // 2
    inv_freq = ROPE_THETA ** (-jnp.arange(half, dtype=F32) / half)
    ang = jnp.arange(t, dtype=F32)[:, None] * inv_freq[None, :]
    return jnp.tile(jnp.cos(ang), (1, LANES // half)), jnp.tile(jnp.sin(ang), (1, LANES // half))


def _trunk(x, target, n_layers, layer_weights, fwd_carry, grads_done):
    cos, sin = _rope_tables(x.shape[0])
    xb = x.astype(_MXU)
    saved, weights, carried = [], [], None
    for i in range(n_layers):
        w = layer_weights(i, carried)
        x, xb, res, carried = _layer_fwd(x, xb, w, cos, sin, carry=fwd_carry(i))
        saved.append(res)
        weights.append(w)
    sq, dx = _loss_head(x, target)
    carry = None
    for i in reversed(range(n_layers)):
        dx, grads, carried = _layer_bwd(dx, weights[i], saved[i], cos, sin, carry=carry)
        carry = grads_done(i, grads, carried)
    return sq, dx, carry


N_CHIPS = 4


def _mesh_pos():
    return lax.axis_index("x"), lax.axis_index("y"), lax.axis_index("c")


def _other_chips(x, y):
    return [(1 - x, y), (x, 1 - y), (1 - x, 1 - y)]


def _remote_copy(src, dst, sems, k, to):
    send_sems, recv_sems = sems
    return pltpu.make_async_remote_copy(src_ref=src, dst_ref=dst, send_sem=send_sems.at[k], recv_sem=recv_sems.at[k],
                                        device_id=to, device_id_type=pl.DeviceIdType.MESH)


def _comm_call(body, name, out_shape, n_sems, *operands):
    return pl.pallas_call(
        body, name=name, in_specs=[pl.BlockSpec(memory_space=pl.ANY)] * len(operands),
        out_specs=pl.BlockSpec(memory_space=pl.ANY), out_shape=out_shape,
        scratch_shapes=[pltpu.SemaphoreType.DMA((n_sems,)), pltpu.SemaphoreType.DMA((n_sems,)), pltpu.SemaphoreType.DMA],
        compiler_params=pltpu.CompilerParams(has_side_effects=True),
    )(*operands)


class _Gather:
    n_sems = N_DEV - 1

    @staticmethod
    def out_shape(block):
        return _sds((N_DEV,) + block.shape, block.dtype)

    @staticmethod
    def _own(x_ref, o_ref, sems, local_sem):
        x, y, c = _mesh_pos()
        mine = o_ref.at[4 * x + 2 * y + c]
        first = [_remote_copy(x_ref, mine, sems, 0, (x, y, 1 - c))]
        first += [_remote_copy(x_ref, mine, sems, 1 + j, (*chip, c)) for j, chip in enumerate(_other_chips(x, y))]
        return pltpu.make_async_copy(x_ref, mine, local_sem), first

    @classmethod
    def start(cls, x_ref, o_ref, sems, local_sem):
        mine, first = cls._own(x_ref, o_ref, sems, local_sem)
        mine.start()
        for cp in first:
            cp.start()

    @classmethod
    def finish(cls, x_ref, o_ref, sems, local_sem):
        x, y, c = _mesh_pos()
        sibling = (x, y, 1 - c)
        chips = _other_chips(x, y)
        slot = lambda px, py, pc: o_ref.at[4 * px + 2 * py + pc]
        mine, first = cls._own(x_ref, o_ref, sems, local_sem)
        passed = [_remote_copy(slot(*chip, c), slot(*chip, c), sems, 4 + j, sibling) for j, chip in enumerate(chips)]
        for j, chip in enumerate(chips):
            _remote_copy(x_ref, slot(*chip, c), sems, 1 + j, sibling).wait_recv()
            passed[j].start()
        _remote_copy(x_ref, slot(x, y, 1 - c), sems, 0, sibling).wait_recv()
        for j, chip in enumerate(chips):
            _remote_copy(x_ref, slot(*chip, 1 - c), sems, 4 + j, sibling).wait_recv()
        for cp in first + passed:
            cp.wait_send()
        mine.wait()


class _ChipExchange:
    n_sems = N_CHIPS - 1

    @staticmethod
    def out_shape(parts):
        return _sds(parts.shape, parts.dtype)

    @staticmethod
    def _own(x_ref, o_ref, sems, local_sem):
        x, y, c = _mesh_pos()
        me = 2 * x + y
        sent = [_remote_copy(x_ref.at[2 * cx + cy], o_ref.at[me], sems, j, (cx, cy, c))
                for j, (cx, cy) in enumerate(_other_chips(x, y))]
        return pltpu.make_async_copy(x_ref.at[me], o_ref.at[me], local_sem), sent

    @classmethod
    def start(cls, x_ref, o_ref, sems, local_sem):
        mine, sent = cls._own(x_ref, o_ref, sems, local_sem)
        mine.start()
        for cp in sent:
            cp.start()

    @classmethod
    def finish(cls, x_ref, o_ref, sems, local_sem):
        x, y, c = _mesh_pos()
        mine, sent = cls._own(x_ref, o_ref, sems, local_sem)
        for j, (cx, cy) in enumerate(_other_chips(x, y)):
            _remote_copy(x_ref.at[2 * x + y], o_ref.at[2 * cx + cy], sems, j, (cx, cy, c)).wait_recv()
        for cp in sent:
            cp.wait_send()
        mine.wait()


def _exchange_alone(kind, operand, name):
    def body(x_ref, o_ref, send_sems, recv_sems, local_sem):
        kind.start(x_ref, o_ref, (send_sems, recv_sems), local_sem)
        kind.finish(x_ref, o_ref, (send_sems, recv_sems), local_sem)

    return _comm_call(body, name, kind.out_shape(operand), kind.n_sems, operand)


def _all_gather(block, name):
    return _exchange_alone(_Gather, block, name)


def _carried(kind, operand, body, n_in, n_out, grid):
    hbm = pl.BlockSpec(memory_space=pl.ANY)

    def wrapped(*refs):
        ins, x_ref = refs[:n_in], refs[n_in]
        outs, o_ref = refs[n_in + 1:n_in + 1 + n_out], refs[n_in + 1 + n_out]
        send_sems, recv_sems, local_sem = refs[n_in + n_out + 2:n_in + n_out + 5]
        rest = refs[n_in + n_out + 5:]
        first, last = None, None
        for axis, size in enumerate(grid):
            at0, at1 = pl.program_id(axis) == 0, pl.program_id(axis) == size - 1
            first = at0 if first is None else first & at0
            last = at1 if last is None else last & at1
        pl.when(first)(lambda: kind.start(x_ref, o_ref, (send_sems, recv_sems), local_sem))
        body(*ins, *outs, *rest)
        pl.when(last)(lambda: kind.finish(x_ref, o_ref, (send_sems, recv_sems), local_sem))

    sems = [pltpu.SemaphoreType.DMA((kind.n_sems,)), pltpu.SemaphoreType.DMA((kind.n_sems,)), pltpu.SemaphoreType.DMA]
    return wrapped, [hbm], [hbm], [kind.out_shape(operand)], sems


def _sibling_swap(parts, name):
    def body(x_ref, o_ref, send_sems, recv_sems, local_sem):
        x, y, c = _mesh_pos()
        sems = (send_sems, recv_sems)
        sibling = (x, y, 1 - c)
        copies = [_remote_copy(x_ref.at[2 * q + (1 - c)], o_ref.at[q], sems, q, sibling) for q in range(N_CHIPS)]
        for cp in copies:
            cp.start()
        for cp in copies:
            cp.wait()

    return _comm_call(body, name, _sds((N_CHIPS,) + parts.shape[1:], parts.dtype), N_CHIPS, parts)


def _pair_sum(a, b, name):
    n, rows, cols = a.shape
    tr = rows if rows <= 512 else _row_tile(rows, 2048)
    blk = pl.BlockSpec((1, tr, cols), lambda q, i: (q, i, 0))

    def body(a_ref, b_ref, o_ref):
        o_ref[...] = (a_ref[...].astype(F32) + b_ref[...].astype(F32)).astype(o_ref.dtype)

    return pl.pallas_call(
        body, name=name, grid=(n, rows // tr), in_specs=[blk, blk], out_specs=blk, out_shape=_sds(a.shape, a.dtype),
        compiler_params=_params(("parallel", "parallel")),
    )(a, b)


def _chip_sums(parts, name):
    c = lax.axis_index("c")
    from_sibling = _sibling_swap(parts, "swap_" + name)
    own = lax.dynamic_index_in_dim(parts.reshape((N_CHIPS, 2) + parts.shape[1:]), c, axis=1, keepdims=False)
    return _pair_sum(own, from_sibling, "pair_sum_" + name)


def _reduce_to_owner(parts, name):
    return _exchange_alone(_ChipExchange, _chip_sums(parts, name), "exchange_" + name)


def _sum_adamw(parts, w, m, v, name):
    rows, cols = w.shape
    n_parts = parts.shape[0]
    tr = rows if rows <= 512 else _row_tile(rows)
    blk = pl.BlockSpec((tr, cols), lambda i: (i, 0))

    def body(p_ref, w_ref, m_ref, v_ref, g_ref, d_ref, nm_ref, nv_ref):
        g = p_ref[0].astype(F32)
        for i in range(1, n_parts):
            g = g + p_ref[i].astype(F32)
        nm = ADAM_B1 * m_ref[...] + (1.0 - ADAM_B1) * g
        nv = ADAM_B2 * v_ref[...] + (1.0 - ADAM_B2) * jnp.square(g)
        m_hat = nm / (1.0 - ADAM_B1 ** ADAM_STEP)
        v_hat = nv / (1.0 - ADAM_B2 ** ADAM_STEP)
        g_ref[...] = g
        d_ref[...] = -ADAM_LR * (m_hat / (jnp.sqrt(v_hat) + ADAM_EPS) + ADAM_WD * w_ref[...])
        nm_ref[...] = nm
        nv_ref[...] = nv

    return pl.pallas_call(
        body, name=name, grid=(rows // tr,), in_specs=[pl.BlockSpec((n_parts, tr, cols), lambda i: (0, i, 0)), blk, blk, blk],
        out_specs=[blk] * 4, out_shape=[_sds((rows, cols))] * 4, compiler_params=_params(("parallel",)),
    )(parts, w, m, v)


def _row_tile(rows, pref=256):
    t = pref
    while t >= 8:
        if rows % t == 0:
            return t
        t //= 2
    return rows


def _gathered_cols(g):
    g = jnp.moveaxis(g, 0, -2)
    return g.reshape(g.shape[:-2] + (g.shape[-2] * g.shape[-1],))


def _gathered_rows(g):
    g = jnp.moveaxis(g, 0, -3)
    return g.reshape(g.shape[:-3] + (g.shape[-3] * g.shape[-2], g.shape[-1]))


def _col_parts(full):
    c = full.shape[-1]
    return jnp.moveaxis(full.reshape(full.shape[:-1] + (N_DEV, c // N_DEV)), -2, 0)


def _row_parts(full):
    rows, c = full.shape[-2:]
    return jnp.moveaxis(full.reshape(full.shape[:-2] + (N_DEV, rows // N_DEV, c)), -3, 0)


SHARD_SHAPES = ((D_MODEL, IN_COLS // N_DEV), (D_MODEL, 2 * FFN_HIDDEN // N_DEV), (DN_WIDTH // N_DEV, D_MODEL),
                (SW_HEADS * SW_HEAD_DIM // N_DEV, D_MODEL), (D_MODEL // N_DEV, D_MODEL), (FFN_HIDDEN // N_DEV, D_MODEL))
PACK_COLS = D_MODEL
PACK_ROWS = 2816
assert sum(r for r, c in SHARD_SHAPES) <= PACK_ROWS and all(c <= PACK_COLS for r, c in SHARD_SHAPES)


def _pack(pieces):
    nd = pieces[0].ndim
    wide = [jnp.pad(p, [(0, 0)] * (nd - 1) + [(0, PACK_COLS - p.shape[-1])]) for p in pieces]
    rows = sum(p.shape[-2] for p in pieces)
    wide.append(jnp.zeros(pieces[0].shape[:-2] + (PACK_ROWS - rows, PACK_COLS), pieces[0].dtype))
    return jnp.concatenate(wide, axis=-2)


def _unpack(packed):
    out, off = [], 0
    for r, c in SHARD_SHAPES:
        out.append(packed[..., off:off + r, :c])
        off += r
    return out


def _w_in_split(w_in):
    s = lambda a, n: w_in[..., a:a + n]
    main = jnp.concatenate([s(R_QKV, 3072), s(R_Z, 1024), s(R_QSW, 1024), s(R_G, 2048), s(R_KSW, 256), s(R_VSW, 256)], axis=-1)
    gates = jnp.pad(s(R_BA, 2 * N_HD), [(0, 0)] * (w_in.ndim - 1) + [(0, LANES - 2 * N_HD)])
    return main, gates


def _w_in_join(dmain, dgates):
    s = lambda a, n: dmain[..., a:a + n]
    return jnp.concatenate([s(C_QKV, 3072), s(C_Z, 1024), dgates[..., :2 * N_HD], s(C_QSW, 1024), s(C_KSW, 256), s(C_VSW, 256),
                            s(C_GA, 2048)], axis=-1)


def _lane_row(a, offset):
    l, n = a.shape
    return jnp.pad(a, ((0, 0), (offset, LANES - offset - n)))[:, None, :]


def kernel(x, w_in, conv_w, a_log, dt_bias, dn_norm_w, sinks, w_branch_a, w_branch_b, w_out, ln1_g, ln1_b, w_gate_up, w_down, ln2_g, ln2_b, loss_target, m_w_in, m_conv_w, m_a_log, m_dt_bias, m_dn_norm_w, m_sinks, m_w_branch_a, m_w_branch_b, m_w_out, m_ln1_g, m_ln1_b, m_w_gate_up, m_w_down, m_ln2_g, m_ln2_b, v_w_in, v_conv_w, v_a_log, v_dt_bias, v_dn_norm_w, v_sinks, v_w_branch_a, v_w_branch_b, v_w_out, v_ln1_g, v_ln1_b, v_w_gate_up, v_w_down, v_ln2_g, v_ln2_b):
    l = DEPTH
    bf = lambda a: a.astype(_MXU)
    packed = _pack([bf(w_in), bf(w_gate_up), bf(w_branch_a), bf(w_branch_b), bf(w_out), bf(w_down)])
    first = _all_gather(packed[0], "gather_layer_0")
    conv_full = _gathered_cols(_all_gather(conv_w, "gather_conv_w"))
    row = lambda a: a[:, None, :]
    small = dict(
        conv=jnp.pad(conv_full, ((0, 0), (0, 8 - DN_CONV), (0, 0))), alog=_lane_row(a_log.reshape(l, N_HD), N_HD),
        dtb=_lane_row(dt_bias.reshape(l, N_HD), N_HD), dnw=row(dn_norm_w), sinks=sinks.reshape(l, SW_HEADS, 1, 1),
        ln1g=row(ln1_g), ln1b=row(ln1_b), ln2g=row(ln2_g), ln2b=row(ln2_b))

    def layer_weights(i, carried):
        s_in, s_gu, s_a, s_b, s_o, s_d = _unpack(first if i == 0 else carried)
        wm, wba = _w_in_split(_gathered_cols(s_in))
        return dict(wm=wm, wba=wba, wgu=_gathered_cols(s_gu), wa=_gathered_rows(s_a), wb=_gathered_rows(s_b),
                    wo=_gathered_rows(s_o), wd=_gathered_rows(s_d), **{k: a[i] for k, a in small.items()})

    def fwd_carry(i):
        return (_Gather, packed[i + 1]) if i + 1 < l else None

    layer_grads, received, waiting = [None] * l, [None] * l, []

    def grads_done(i, g_i, carried):
        if waiting:
            received[waiting.pop()] = carried
        layer_grads[i] = g_i
        parts = _pack([_col_parts(_w_in_join(g_i["wm"], g_i["wba"])), _col_parts(g_i["wgu"]), _row_parts(g_i["wa"]),
                       _row_parts(g_i["wb"]), _row_parts(g_i["wo"]), _row_parts(g_i["wd"])])
        waiting.append(i)
        return (_ChipExchange, _chip_sums(parts, f"layer_{i}"))

    sq, dx, last = _trunk(x[0], loss_target[0], l, layer_weights, fwd_carry, grads_done)
    received[waiting.pop()] = _exchange_alone(last[0], last[1], "exchange_layer_0")
    loss = lax.psum(0.5 * sq[0, 0] / D_MODEL, ("x", "y", "c"))
    g = {k: jnp.stack([gi[k] for gi in layer_grads]) for k in small}

    def adamw(parts, w, m, v, name):
        rows = w.shape[0] * w.shape[1]
        flat = lambda a: a.reshape(rows, a.shape[-1])
        outs = _sum_adamw(parts.reshape(parts.shape[0], rows, w.shape[-1]), flat(w), flat(m), flat(v), "adamw_" + name)
        return [o.reshape(w.shape) for o in outs]

    got = [jnp.stack(per_layer, axis=1) for per_layer in zip(*[_unpack(r) for r in received])]
    dconv = _reduce_to_owner(_col_parts(g["conv"][:, :DN_CONV, :]).reshape(N_DEV, l * DN_CONV, -1), "conv_w")
    results = {
        "w_in": adamw(got[0], w_in, m_w_in, v_w_in, "w_in"),
        "conv_w": adamw(dconv.reshape(N_CHIPS, l, DN_CONV, -1), conv_w, m_conv_w, v_conv_w, "conv_w"),
        "w_branch_a": adamw(got[2], w_branch_a, m_w_branch_a, v_w_branch_a, "w_branch_a"),
        "w_branch_b": adamw(got[3], w_branch_b, m_w_branch_b, v_w_branch_b, "w_branch_b"),
        "w_out": adamw(got[4], w_out, m_w_out, v_w_out, "w_out"),
        "w_gate_up": adamw(got[1], w_gate_up, m_w_gate_up, v_w_gate_up, "w_gate_up"),
        "w_down": adamw(got[5], w_down, m_w_down, v_w_down, "w_down"),
    }

    small_w = {"a_log": a_log.reshape(l, N_HD), "dt_bias": dt_bias.reshape(l, N_HD), "dn_norm_w": dn_norm_w, "sinks": sinks,
               "ln1_g": ln1_g, "ln1_b": ln1_b, "ln2_g": ln2_g, "ln2_b": ln2_b}
    small_m = {"a_log": m_a_log, "dt_bias": m_dt_bias, "dn_norm_w": m_dn_norm_w, "sinks": m_sinks, "ln1_g": m_ln1_g,
               "ln1_b": m_ln1_b, "ln2_g": m_ln2_g, "ln2_b": m_ln2_b}
    small_v = {"a_log": v_a_log, "dt_bias": v_dt_bias, "dn_norm_w": v_dn_norm_w, "sinks": v_sinks, "ln1_g": v_ln1_g,
               "ln1_b": v_ln1_b, "ln2_g": v_ln2_g, "ln2_b": v_ln2_b}
    small_g = {"a_log": g["alog"][:, 0, N_HD:2 * N_HD], "dt_bias": g["dtb"][:, 0, N_HD:2 * N_HD], "dn_norm_w": g["dnw"][:, 0, :],
               "sinks": g["sinks"].reshape(l, SW_HEADS), "ln1_g": g["ln1g"][:, 0, :], "ln1_b": g["ln1b"][:, 0, :],
               "ln2_g": g["ln2g"][:, 0, :], "ln2_b": g["ln2b"][:, 0, :]}
    names = list(small_w)
    cat = lambda d: jnp.concatenate([d[n].reshape(l, -1) for n in names], axis=1)
    widths = [small_w[n].shape[1] for n in names]
    total = sum(widths)
    padded = -(-total // LANES) * LANES
    pad = lambda a: jnp.pad(a, ((0, 8 - l), (0, padded - total)))
    got = _all_gather(pad(cat(small_g)), "gather_small_grads")
    outs = _sum_adamw(got, pad(cat(small_w)), pad(cat({n: small_m[n].reshape(l, -1) for n in names})),
                      pad(cat({n: small_v[n].reshape(l, -1) for n in names})), "adamw_small")
    off = 0
    for n, wd_ in zip(names, widths):
        shape = {"a_log": a_log.shape, "dt_bias": dt_bias.shape}.get(n, small_w[n].shape)
        results[n] = [o[:l, off:off + wd_].reshape(shape) for o in outs]
        off += wd_

    order = ["w_in", "conv_w", "a_log", "dt_bias", "dn_norm_w", "sinks", "w_branch_a", "w_branch_b", "w_out", "ln1_g", "ln1_b",
             "w_gate_up", "w_down", "ln2_g", "ln2_b"]
    return (loss, dx[None], *[results[n][0] for n in order], *[results[n][1] for n in order],
            *[results[n][2] for n in order], *[results[n][3] for n in order])
```

```python
import functools

import jax
import jax.numpy as jnp
from jax import lax
from jax.experimental import pallas as pl
from jax.experimental.pallas import tpu as pltpu

F32 = jnp.float32
_MXU = jnp.bfloat16
_HI = lax.Precision.HIGHEST
_MID = lax.Precision.HIGH

N_DEV = 8
D_MODEL = 1024
DEPTH = 4
DN_HEADS = 8
DN_HEAD_DIM = 128
DN_WIDTH = DN_HEADS * DN_HEAD_DIM
DN_CONV = 5
DN_CHUNK = 64
SW_HEADS = 16
SW_KV_HEADS = 4
SW_HEAD_DIM = 64
SW_GROUP = SW_HEADS // SW_KV_HEADS
SW_BLOCK = 128
SW_KV_WIDTH = SW_KV_HEADS * SW_HEAD_DIM
ROPE_THETA = 10000.0
FFN_HIDDEN = 2816
DN_ALPHA = (2.0 * DEPTH) ** 0.25
LN_EPS = 1e-5
RMS_EPS = 1e-6
ADAM_LR = 0.001
ADAM_B1 = 0.9
ADAM_B2 = 0.999
ADAM_EPS = 1e-08
ADAM_WD = 0.01
ADAM_STEP = 10

LANES = 128
N_HD = 2 * DN_HEADS
DN_GROUP = 4 * DN_CHUNK
INV_SUB = 16
LOCAL_ROWS = 512
REC_ROWS = 256
ROW_TILE = 256
VMEM_LIMIT = 48 << 20

C_QKV, C_Z, C_QSW, C_GA, C_GB, C_KSW, C_VSW = 0, 3072, 4096, 5120, 6144, 7168, 7424
MAIN_COLS = 7680
R_QKV, R_Z, R_BA, R_QSW, R_KSW, R_VSW, R_G = 0, 3072, 4096, 4128, 5152, 5408, 5664
IN_COLS = 7712


_NN = ((1,), (0,))
_NT = ((1,), (1,))
_TN = ((0,), (0,))


def _dg(a, b, dims, precision):
    if precision is not None:
        return lax.dot_general(a, b, (dims, ((), ())), precision=precision, preferred_element_type=F32)
    return lax.dot_general(a.astype(_MXU), b.astype(_MXU), (dims, ((), ())), preferred_element_type=F32)


def _make_dots(hi):
    @jax.custom_vjp
    def nn(a, b):
        return _dg(a, b, _NN, hi)

    @jax.custom_vjp
    def nt(a, b):
        return _dg(a, b, _NT, hi)

    @jax.custom_vjp
    def tn(a, b):
        return _dg(a, b, _TN, hi)

    nn.defvjp(lambda a, b: (nn(a, b), (a, b)), lambda r, g: (nt(g, r[1]), tn(r[0], g)))
    nt.defvjp(lambda a, b: (nt(a, b), (a, b)), lambda r, g: (nn(g, r[1]), tn(g, r[0])))
    tn.defvjp(lambda a, b: (tn(a, b), (a, b)), lambda r, g: (nt(r[1], g), nn(r[0], g)))
    return nn, nt, tn


_bnn, _bnt, _btn = _make_dots(None)
_hnn, _hnt, _htn = _make_dots(_HI)
_mnn, _mnt, _mtn = _make_dots(_MID)


def _neumann(a, order):
    n = a.shape[0]
    eye = (lax.broadcasted_iota(jnp.int32, (n, n), 0) == lax.broadcasted_iota(jnp.int32, (n, n), 1)).astype(F32)
    inv = eye - a
    p = a
    span = 2
    while span < order:
        p = _mnn(p, p)
        inv = inv + _mnn(inv, p)
        span *= 2
    return inv


def _inv_unit(a, order):
    n = a.shape[0]
    ii = lax.broadcasted_iota(jnp.int32, (n, n), 0)
    jj = lax.broadcasted_iota(jnp.int32, (n, n), 1)
    near = (ii & -INV_SUB) == (jj & -INV_SUB)
    d_inv = _neumann(jnp.where(near, a, 0.0), INV_SUB)
    outer = _neumann(_mnn(d_inv, jnp.where(near, 0.0, a)), order // INV_SUB)
    return _mnn(outer, d_inv)


def _inv_unit_t(t, g):
    return -_mnt(_mtn(t, g), t)


def _silu(x):
    return x * jax.nn.sigmoid(x)


def _softplus(x):
    return jnp.maximum(x, 0.0) + jnp.log1p(jnp.exp(-jnp.abs(x)))


def _params(sem=None):
    kw = {"vmem_limit_bytes": VMEM_LIMIT}
    if sem is not None:
        kw["dimension_semantics"] = sem
    return pltpu.CompilerParams(**kw)


def _tile(dim, pref):
    if dim <= pref:
        return dim
    t = (pref // LANES) * LANES
    while t > LANES and dim % t:
        t -= LANES
    assert dim % t == 0, (dim, pref)
    return t


def _full(shape):
    zeros = (0,) * len(shape)
    return pl.BlockSpec(shape, lambda *_: zeros)


def _sds(shape, dtype=F32):
    return jax.ShapeDtypeStruct(shape, dtype)


def _mm(a, b, mode, *, name, add=None, tm=1536, tn=1536, tk=1536, out_dtype=F32):
    if mode == "nn":
        (m, k), (k2, n) = a.shape, b.shape
    elif mode == "nt":
        (m, k), (n, k2) = a.shape, b.shape
    else:
        (k, m), (k2, n) = a.shape, b.shape
    assert k == k2, (a.shape, b.shape, mode)
    tm, tn, tk = _tile(m, tm), _tile(n, tn), _tile(k, tk)
    nk = k // tk
    dims = {"nn": _NN, "nt": _NT, "tn": _TN}[mode]

    def body(*refs):
        if add is None:
            a_ref, b_ref, o_ref, acc = refs
        else:
            a_ref, b_ref, add_ref, o_ref, acc = refs
        kk = pl.program_id(2)

        @pl.when(kk == 0)
        def _():
            acc[...] = jnp.zeros_like(acc)

        acc[...] += _dg(a_ref[...], b_ref[...], dims, None)

        @pl.when(kk == nk - 1)
        def _():
            o_ref[...] = (acc[...] if add is None else acc[...] + add_ref[...]).astype(out_dtype)

    a_spec = pl.BlockSpec((tk, tm), lambda i, j, kk: (kk, i)) if mode == "tn" else pl.BlockSpec((tm, tk), lambda i, j, kk: (i, kk))
    b_spec = pl.BlockSpec((tn, tk), lambda i, j, kk: (j, kk)) if mode == "nt" else pl.BlockSpec((tk, tn), lambda i, j, kk: (kk, j))
    o_spec = pl.BlockSpec((tm, tn), lambda i, j, kk: (i, j))
    ins, specs = [a, b], [a_spec, b_spec]
    if add is not None:
        ins.append(add)
        specs.append(o_spec)
    return pl.pallas_call(
        body, name=name, grid=(m // tm, n // tn, nk), in_specs=specs, out_specs=o_spec,
        out_shape=_sds((m, n), out_dtype), scratch_shapes=[pltpu.VMEM((tm, tn), F32)],
        compiler_params=_params(("parallel", "parallel", "arbitrary")),
    )(*ins)


def _cols(width, start):
    assert start % width == 0
    return pl.BlockSpec((ROW_TILE, width), lambda i: (i, start // width))


def _rows(width):
    return pl.BlockSpec((ROW_TILE, width), lambda i: (i, 0))


def _accumulate(ref, value, step):
    @pl.when(step == 0)
    def _():
        ref[...] = value

    @pl.when(step != 0)
    def _():
        ref[...] += value


def _ln_fn(x, r, g, b):
    u = DN_ALPHA * x + r
    mu = jnp.mean(u, axis=-1, keepdims=True)
    var = jnp.mean(jnp.square(u - mu), axis=-1, keepdims=True)
    return (u - mu) * lax.rsqrt(var + LN_EPS) * g + b


def _ln_fwd(x, r, g, b, name):
    t, d = x.shape

    def body(x_ref, r_ref, g_ref, b_ref, o_ref, ob_ref):
        y = _ln_fn(x_ref[...], r_ref[...], g_ref[...], b_ref[...])
        o_ref[...] = y
        ob_ref[...] = y.astype(_MXU)

    return pl.pallas_call(
        body, name=name, grid=(t // ROW_TILE,), in_specs=[_rows(d), _rows(d), _full((1, d)), _full((1, d))],
        out_specs=[_rows(d), _rows(d)], out_shape=[_sds((t, d)), _sds((t, d), _MXU)], compiler_params=_params(("parallel",)),
    )(x, r, g, b)


def _ln_bwd(x, r, g, b, dy, name):
    t, d = x.shape

    def body(x_ref, r_ref, g_ref, b_ref, dy_ref, dx_ref, dr_ref, dg_ref, db_ref):
        _, vjp = jax.vjp(_ln_fn, x_ref[...], r_ref[...], g_ref[...], b_ref[...])
        dx, dr, dg, db = vjp(dy_ref[...])
        dx_ref[...] = dx
        dr_ref[...] = dr.astype(_MXU)
        _accumulate(dg_ref, dg, pl.program_id(0))
        _accumulate(db_ref, db, pl.program_id(0))

    return pl.pallas_call(
        body, name=name, grid=(t // ROW_TILE,),
        in_specs=[_rows(d), _rows(d), _full((1, d)), _full((1, d)), _rows(d)],
        out_specs=[_rows(d), _rows(d), _full((1, d)), _full((1, d))],
        out_shape=[_sds((t, d)), _sds((t, d), _MXU), _sds((1, d)), _sds((1, d))],
        compiler_params=_params(("arbitrary",)),
    )(x, r, g, b, dy)


def _merge_fn(ga, gb, ya, yb):
    return jax.nn.sigmoid(ga) * ya + jax.nn.sigmoid(gb) * yb


def _merge_fwd(proj, ya, yb):
    t, d = ya.shape

    def body(ga_ref, gb_ref, ya_ref, yb_ref, o_ref):
        o_ref[...] = _merge_fn(ga_ref[...], gb_ref[...], ya_ref[...], yb_ref[...]).astype(_MXU)

    return pl.pallas_call(
        body, name="merge_fwd", grid=(t // ROW_TILE,), in_specs=[_cols(d, C_GA), _cols(d, C_GB), _rows(d), _rows(d)],
        out_specs=_rows(d), out_shape=_sds((t, d), _MXU), compiler_params=_params(("parallel",)),
    )(proj, proj, ya, yb)


def _merge_bwd(proj, ya, yb, dm):
    t, d = ya.shape

    def body(ga_ref, gb_ref, ya_ref, yb_ref, dm_ref, dga_ref, dgb_ref, dya_ref, dyb_ref):
        _, vjp = jax.vjp(_merge_fn, ga_ref[...], gb_ref[...], ya_ref[...], yb_ref[...])
        dga_ref[...], dgb_ref[...], dya_ref[...], dyb_ref[...] = [g.astype(_MXU) for g in vjp(dm_ref[...])]

    return pl.pallas_call(
        body, name="merge_bwd", grid=(t // ROW_TILE,),
        in_specs=[_cols(d, C_GA), _cols(d, C_GB), _rows(d), _rows(d), _rows(d)],
        out_specs=[_rows(d)] * 4, out_shape=[_sds((t, d), _MXU)] * 4, compiler_params=_params(("parallel",)),
    )(proj, proj, ya, yb, dm)


def _swiglu_fn(gate, up):
    return _silu(gate) * up


def _swiglu_fwd(gu):
    t = gu.shape[0]
    f = FFN_HIDDEN
    rows = 128

    def body(gu_ref, o_ref):
        o_ref[...] = _swiglu_fn(gu_ref[:, :f], gu_ref[:, f:]).astype(_MXU)

    return pl.pallas_call(
        body, name="swiglu_fwd", grid=(t // rows,), in_specs=[pl.BlockSpec((rows, 2 * f), lambda i: (i, 0))],
        out_specs=pl.BlockSpec((rows, f), lambda i: (i, 0)), out_shape=_sds((t, f), _MXU), compiler_params=_params(("parallel",)),
    )(gu)


def _swiglu_bwd(gu, dh):
    t = gu.shape[0]
    f = FFN_HIDDEN
    rows = 128

    def body(gu_ref, dh_ref, o_ref):
        _, vjp = jax.vjp(_swiglu_fn, gu_ref[:, :f], gu_ref[:, f:])
        o_ref[:, :f], o_ref[:, f:] = [g.astype(_MXU) for g in vjp(dh_ref[...])]

    return pl.pallas_call(
        body, name="swiglu_bwd", grid=(t // rows,),
        in_specs=[pl.BlockSpec((rows, 2 * f), lambda i: (i, 0)), pl.BlockSpec((rows, f), lambda i: (i, 0))],
        out_specs=pl.BlockSpec((rows, 2 * f), lambda i: (i, 0)), out_shape=_sds((t, 2 * f), _MXU),
        compiler_params=_params(("parallel",)),
    )(gu, dh)


def _loss_head(y, target):
    t, d = y.shape

    def body(y_ref, t_ref, s_ref, dy_ref):
        err = y_ref[...] - t_ref[...]
        dy_ref[...] = err / d
        _accumulate(s_ref, jnp.broadcast_to(jnp.sum(jnp.square(err)), (1, LANES)), pl.program_id(0))

    return pl.pallas_call(
        body, name="loss_head", grid=(t // ROW_TILE,), in_specs=[_rows(d), _rows(d)],
        out_specs=[_full((1, LANES)), _rows(d)], out_shape=[_sds((1, LANES)), _sds((t, d))],
        compiler_params=_params(("arbitrary",)),
    )(y, target)


def _shift_rows(x, s):
    if s == 0:
        return x
    return pltpu.roll(x, (-s) % x.shape[0], 0)


def _conv(x, w):
    half = DN_CONV // 2
    acc = None
    for k in range(DN_CONV):
        term = _shift_rows(x, k - half) * w[k:k + 1, :]
        acc = term if acc is None else acc + term
    return acc


def _act_norm(c, do_norm, scale):
    a = _silu(c)
    if not do_norm:
        return a
    return a * lax.rsqrt(jnp.sum(a * a, axis=-1, keepdims=True) + RMS_EPS) * scale


PREP_ROWS = 512
HALO = 8
_KINDS = ((True, DN_HEAD_DIM ** -0.5), (True, 1.0), (False, 1.0))


def _halo_rows(read, i, pr, t):
    lo, hi = i * pr - HALO, (i + 1) * pr + HALO
    parts = []
    if lo < 0:
        parts.append(jnp.zeros((HALO, LANES), F32))
    parts.append(read(max(lo, 0), min(hi, t)))
    if hi > t:
        parts.append(jnp.zeros((HALO, LANES), F32))
    return jnp.concatenate(parts, axis=0) if len(parts) > 1 else parts[0]


def _prep_fwd(proj, conv_w, kind):
    t = proj.shape[0]
    pr = min(PREP_ROWS, t)
    do_norm, scale = _KINDS[kind]
    blk = pl.BlockSpec((t, LANES), lambda j: (0, kind * DN_HEADS + j))

    def body(x_ref, w_ref, o_ref):
        w = w_ref[...]
        for i in range(t // pr):
            xx = _halo_rows(lambda lo, hi: x_ref[lo:hi, :], i, pr, t)
            c = _conv(xx, w)[HALO:HALO + pr, :]
            o_ref[i * pr:(i + 1) * pr, :] = _act_norm(c, do_norm, scale)

    return pl.pallas_call(
        body, name=f"prep_fwd_{kind}", grid=(DN_HEADS,),
        in_specs=[blk, pl.BlockSpec((8, LANES), lambda j: (0, kind * DN_HEADS + j))],
        out_specs=pl.BlockSpec((t, LANES), lambda j: (0, j)), out_shape=_sds((t, DN_WIDTH)),
        compiler_params=_params(("parallel",)),
    )(proj, conv_w)


def _prep_bwd(proj, conv_w, d2, kind):
    t = proj.shape[0]
    pr = min(PREP_ROWS, t)
    do_norm, scale = _KINDS[kind]
    half = DN_CONV // 2
    blk = pl.BlockSpec((t, LANES), lambda j: (0, kind * DN_HEADS + j))
    oblk = pl.BlockSpec((t, LANES), lambda j: (0, j))

    def body(x_ref, w_ref, d_ref, dx_ref, dw_ref):
        w = w_ref[...]
        own = slice(HALO, HALO + pr)
        dw = jnp.zeros((8, LANES), F32)
        for i in range(t // pr):
            xx = _halo_rows(lambda lo, hi: x_ref[lo:hi, :], i, pr, t)
            dn = _halo_rows(lambda lo, hi: d_ref[0, lo:hi, :] + d_ref[1, lo:hi, :], i, pr, t)
            _, vjp = jax.vjp(lambda c: _act_norm(c, do_norm, scale), _conv(xx, w))
            (dc,) = vjp(dn)
            dx = None
            rows = []
            for k in range(DN_CONV):
                term = _shift_rows(dc, half - k) * w[k:k + 1, :]
                dx = term if dx is None else dx + term
                rows.append(jnp.sum(dc[own, :] * _shift_rows(xx, k - half)[own, :], axis=0, keepdims=True))
            dx_ref[i * pr:(i + 1) * pr, :] = dx[own, :].astype(_MXU)
            dw = dw + jnp.concatenate(rows + [jnp.zeros((8 - DN_CONV, LANES), F32)], axis=0)
        dw_ref[...] = dw

    return pl.pallas_call(
        body, name=f"prep_bwd_{kind}", grid=(DN_HEADS,),
        in_specs=[blk, pl.BlockSpec((8, LANES), lambda j: (0, kind * DN_HEADS + j)), pl.BlockSpec((2, t, LANES), lambda j: (0, 0, j))],
        out_specs=[oblk, pl.BlockSpec((8, LANES), lambda j: (0, j))], out_shape=[_sds((t, DN_WIDTH), _MXU), _sds((8, DN_WIDTH))],
        compiler_params=_params(("parallel",)),
    )(proj, conv_w, d2)


def _gb_fn(ba, alog_row, dtb_row):
    c = DN_CHUNK
    lane = lax.broadcasted_iota(jnp.int32, (c, LANES), 1)
    ii = lax.broadcasted_iota(jnp.int32, (c, c), 0)
    jj = lax.broadcasted_iota(jnp.int32, (c, c), 1)
    beta = jax.nn.sigmoid(ba)
    g = -jnp.exp(alog_row) * _softplus(ba + dtb_row)
    g = jnp.where((lane >= N_HD) & (lane < 2 * N_HD), g, 0.0)
    gc_fwd = _hnn((ii >= jj).astype(F32), g)
    gc_rev = _hnn((ii <= jj).astype(F32), g)
    gc = jnp.where(lane < N_HD + DN_HEADS, gc_fwd, gc_rev)
    return jnp.where(lane < N_HD, beta, jnp.where(lane < 2 * N_HD, gc, 0.0))


def _per_head_spec():
    return pl.BlockSpec((N_HD, ROW_TILE, LANES), lambda i: (0, i, 0))


def _gb_fwd(ba, alog_row, dtb_row):
    t = ba.shape[0]
    n = ROW_TILE // DN_CHUNK

    def body(ba_ref, a_ref, d_ref, o_ref, beta_ref, gc_ref):
        for c in range(n):
            rows = slice(c * DN_CHUNK, (c + 1) * DN_CHUNK)
            out = _gb_fn(ba_ref[rows, :], a_ref[...], d_ref[...])
            o_ref[rows, :] = out
            for j in range(N_HD):
                beta_ref[j, rows, :] = jnp.broadcast_to(out[:, j:j + 1], (DN_CHUNK, LANES))
                gc_ref[j, rows, :] = jnp.broadcast_to(out[:, N_HD + j:N_HD + j + 1], (DN_CHUNK, LANES))

    per_head = _sds((N_HD, t, LANES))
    return pl.pallas_call(
        body, name="gates_fwd", grid=(t // ROW_TILE,), in_specs=[_rows(LANES), _full((1, LANES)), _full((1, LANES))],
        out_specs=[_rows(LANES), _per_head_spec(), _per_head_spec()], out_shape=[_sds((t, LANES)), per_head, per_head],
        compiler_params=_params(("parallel",)),
    )(ba, alog_row, dtb_row)


def _gb_bwd(ba, alog_row, dtb_row, dbeta, dgc, d_rows):
    t = ba.shape[0]
    n = ROW_TILE // DN_CHUNK

    def body(ba_ref, a_ref, d_ref, dbeta_ref, dgc_ref, dr_ref, dba_ref, dal_ref, ddt_ref):
        dal = jnp.zeros((1, LANES), F32)
        ddt = jnp.zeros((1, LANES), F32)
        lane = lax.broadcasted_iota(jnp.int32, (DN_CHUNK, LANES), 1)
        for c in range(n):
            rows = slice(c * DN_CHUNK, (c + 1) * DN_CHUNK)
            cot = dr_ref[rows, :]
            for j in range(N_HD):
                cot = jnp.where(lane == j, dbeta_ref[j, rows, :], cot)
                cot = jnp.where(lane == N_HD + j, dgc_ref[j, rows, :] + cot, cot)
            _, vjp = jax.vjp(_gb_fn, ba_ref[rows, :], a_ref[...], d_ref[...])
            dba, da, dd = vjp(cot)
            dba_ref[rows, :] = dba.astype(_MXU)
            dal = dal + da
            ddt = ddt + dd
        _accumulate(dal_ref, dal, pl.program_id(0))
        _accumulate(ddt_ref, ddt, pl.program_id(0))

    return pl.pallas_call(
        body, name="gates_bwd", grid=(t // ROW_TILE,),
        in_specs=[_rows(LANES), _full((1, LANES)), _full((1, LANES)), _per_head_spec(), _per_head_spec(), _rows(LANES)],
        out_specs=[_rows(LANES), _full((1, LANES)), _full((1, LANES))],
        out_shape=[_sds((t, LANES), _MXU), _sds((1, LANES)), _sds((1, LANES))], compiler_params=_params(("arbitrary",)),
    )(ba, alog_row, dtb_row, dbeta, dgc, d_rows)


def _dn_decay(gcc, gcr, sgn):
    c = DN_CHUNK
    ii = lax.broadcasted_iota(jnp.int32, (c, c), 0)
    jj = lax.broadcasted_iota(jnp.int32, (c, c), 1)
    d = (ii - jj) * sgn
    lower = d >= 0
    return jnp.where(lower, jnp.exp(jnp.where(lower, gcc - gcr, 0.0)), 0.0), d > 0


def _dn_a(k, beta, gcc, gcr, sgn):
    decay, strict = _dn_decay(gcc, gcr, sgn)
    return jnp.where(strict, beta * _bnt(k, k) * decay, 0.0)


def _dn_group(q, k, v, beta, gcc, gcr, sgn):
    n = DN_GROUP
    ii = lax.broadcasted_iota(jnp.int32, (n, n), 0)
    jj = lax.broadcasted_iota(jnp.int32, (n, n), 1)
    same = (ii & -DN_CHUNK) == (jj & -DN_CHUNK)
    d = (ii - jj) * sgn
    lower = same & (d >= 0)
    decay = jnp.where(lower, jnp.exp(jnp.where(lower, gcc - gcr, 0.0)), 0.0)
    a = jnp.where(same & (d > 0), beta * _bnt(k, k) * decay, 0.0)
    t_inv = _inv_unit(a, DN_CHUNK)
    u = _bnn(t_inv, v * beta)
    w = _bnn(t_inv, k * (beta * jnp.exp(gcc)))
    return u, w, _bnt(q, k) * decay, t_inv


def _dn_local(t_inv, q, k, v, beta, gcc, gcr, sgn):
    c = DN_CHUNK
    decay, _ = _dn_decay(gcc, gcr, sgn)
    eg = jnp.exp(gcc)
    u = _bnn(t_inv, v * beta)
    w = _bnn(t_inv, k * (beta * eg))
    qk = _bnt(q, k) * decay
    qd = q * eg
    last = jnp.where(sgn > 0, c - 1, 0)
    onehot = (lax.broadcasted_iota(jnp.int32, (c, 1), 0) == last).astype(F32)
    gl = jnp.sum(gcc * onehot, axis=0, keepdims=True)
    kd = k * jnp.exp(gl - gcc)
    egl = jnp.broadcast_to(jnp.exp(gl), (1, LANES))
    return u, w, qk, qd, kd, egl


def _hd_sign(hd):
    return jnp.where(hd < DN_HEADS, 1, -1).astype(jnp.int32)


def _head_of(hd):
    return jnp.where(hd < DN_HEADS, hd, hd - DN_HEADS)


def _dir_of(hd):
    return jnp.where(hd < DN_HEADS, 0, 1)


def _dn_specs(t):
    nl = LOCAL_ROWS // DN_CHUNK
    wide = pl.BlockSpec((1, LOCAL_ROWS, LANES), lambda hd, i: (hd, i, 0))
    half = pl.BlockSpec((1, LOCAL_ROWS, DN_CHUNK), lambda hd, i: (hd, i, 0))
    col = wide
    row = pl.BlockSpec((1, nl, 1, DN_CHUNK), lambda hd, i: (hd, i, 0, 0))
    egl = pl.BlockSpec((1, nl, 1, LANES), lambda hd, i: (hd, i, 0, 0))
    return wide, half, col, row, egl


def _qkv_specs():
    return [pl.BlockSpec((LOCAL_ROWS, LANES), lambda hd, i: (i, _head_of(hd)))] * 3


def _maybe_carrying(carry, body, name, grid, operands, in_specs, out_specs, out_shape):
    extra_scratch = []
    if carry is not None:
        kind, operand = carry
        body, more_in, more_out, more_shape, extra_scratch = _carried(kind, operand, body, len(operands), len(out_shape), grid)
        operands, in_specs = operands + [operand], in_specs + more_in
        out_specs, out_shape = out_specs + more_out, out_shape + more_shape
    sem = ("arbitrary",) * len(grid) if carry is not None else ("parallel",) * len(grid)
    return pl.pallas_call(
        body, name=name, grid=grid, in_specs=in_specs, out_specs=out_specs, out_shape=out_shape,
        scratch_shapes=extra_scratch, compiler_params=_params(sem),
    )(*operands)


def _dn_local_fwd(q, k, v, beta_c, gc_c, gc_r, carry=None):
    t = q.shape[0]
    nc = t // DN_CHUNK
    nl = LOCAL_ROWS // DN_CHUNK
    wide, half, col, row, egl = _dn_specs(t)

    ng = LOCAL_ROWS // DN_GROUP
    per = DN_GROUP // DN_CHUNK
    grow = pl.BlockSpec((1, ng, 1, DN_GROUP), lambda hd, i: (hd, i, 0, 0))

    def body(q_ref, k_ref, v_ref, b_ref, gc_ref, gg_ref, u_ref, w_ref, qk_ref, qd_ref, kd_ref, egl_ref, t_ref):
        sgn = _hd_sign(pl.program_id(0))
        last = jnp.where(sgn > 0, DN_CHUNK - 1, 0)
        onehot = (lax.broadcasted_iota(jnp.int32, (DN_CHUNK, 1), 0) == last).astype(F32)
        groups = lambda a: a.reshape((ng, DN_GROUP) + a.shape[1:])
        q_all, k_all, gcc_all = q_ref[...], k_ref[...], gc_ref[0][:, :1]
        u, w, qk, t_inv = jax.vmap(functools.partial(_dn_group, sgn=sgn))(
            groups(q_all), groups(k_all), groups(v_ref[...]), groups(b_ref[0][:, :1]), groups(gcc_all), gg_ref[0])
        u_ref[0] = u.reshape(LOCAL_ROWS, LANES)
        w_ref[0] = w.reshape(LOCAL_ROWS, LANES)
        qd_ref[0] = q_all * jnp.exp(gcc_all)
        for gi in range(ng):
            for c in range(per):
                blk = slice(c * DN_CHUNK, (c + 1) * DN_CHUNK)
                rows = slice(gi * DN_GROUP + c * DN_CHUNK, gi * DN_GROUP + (c + 1) * DN_CHUNK)
                qk_ref[0, rows, :] = qk[gi, blk, blk]
                t_ref[0, rows, :] = t_inv[gi, blk, blk]
                gl = jnp.sum(gcc_all[rows, :] * onehot, axis=0, keepdims=True)
                kd_ref[0, rows, :] = k_all[rows, :] * jnp.exp(gl - gcc_all[rows, :])
                egl_ref[0, gi * per + c] = jnp.broadcast_to(jnp.exp(gl), (1, LANES))

    big = _sds((N_HD, t, LANES))
    small = _sds((N_HD, t, DN_CHUNK))
    operands = [q, k, v, beta_c, gc_c, gc_r.reshape(N_HD, t // DN_GROUP, 1, DN_GROUP)]
    return _maybe_carrying(
        carry, body, "dn_local_fwd", (N_HD, t // LOCAL_ROWS), operands, _qkv_specs() + [col, col, grow],
        [wide, wide, half, wide, wide, egl, half], [big, big, small, big, big, _sds((N_HD, nc, 1, LANES)), small])


def _dn_local_bwd(q, k, v, beta_c, gc_c, gc_r, t_inv, du, dw, dqk, dqd, dkd, degl, carry=None):
    t = q.shape[0]
    nc = t // DN_CHUNK
    nl = LOCAL_ROWS // DN_CHUNK
    wide, half, col, row, egl = _dn_specs(t)
    dspec = pl.BlockSpec((1, LOCAL_ROWS, LANES), lambda hd, i: (_dir_of(hd), i, _head_of(hd)))

    def body(q_ref, k_ref, v_ref, b_ref, gc_ref, gr_ref, t_ref, du_ref, dw_ref, dqk_ref, dqd_ref, dkd_ref, degl_ref,
             dq_ref, dk_ref, dv_ref, db_ref, dgc_ref, dgr_ref):
        sgn = _hd_sign(pl.program_id(0))

        def chunk_bwd(tinv, q, k, v, beta, gcc, gcr, du, dw, dqk, dqd, dkd, degl):
            _, vjp = jax.vjp(functools.partial(_dn_local, sgn=sgn), tinv, q, k, v, beta, gcc, gcr)
            dt, dq, dk, dv, db, dgc, dgr = vjp((du, dw, dqk, dqd, dkd, degl))
            _, vjp_a = jax.vjp(functools.partial(_dn_a, sgn=sgn), k, beta, gcc, gcr)
            dk2, db2, dgc2, dgr2 = vjp_a(_inv_unit_t(tinv, dt))
            return dq, dk + dk2, dv, db + db2, dgc + dgc2, dgr + dgr2

        chunks = lambda a: a.reshape((nl, DN_CHUNK) + a.shape[1:])
        dq, dk, dv, db, dgc, dgr = jax.vmap(chunk_bwd)(
            chunks(t_ref[0]), chunks(q_ref[...]), chunks(k_ref[...]), chunks(v_ref[...]), chunks(b_ref[0][:, :1]),
            chunks(gc_ref[0][:, :1]),
            gr_ref[0], chunks(du_ref[0]), chunks(dw_ref[0]), chunks(dqk_ref[0]), chunks(dqd_ref[0]), chunks(dkd_ref[0]),
            degl_ref[0])
        dq_ref[0] = dq.reshape(LOCAL_ROWS, LANES)
        dk_ref[0] = dk.reshape(LOCAL_ROWS, LANES)
        dv_ref[0] = dv.reshape(LOCAL_ROWS, LANES)
        db_ref[0] = jnp.broadcast_to(db.reshape(LOCAL_ROWS, 1), (LOCAL_ROWS, LANES))
        dgc_ref[0] = jnp.broadcast_to(dgc.reshape(LOCAL_ROWS, 1), (LOCAL_ROWS, LANES))
        dgr_ref[0] = dgr

    per_dir = _sds((2, t, DN_WIDTH))
    return _maybe_carrying(
        carry, body, "dn_local_bwd", (N_HD, t // LOCAL_ROWS), [q, k, v, beta_c, gc_c, gc_r, t_inv, du, dw, dqk, dqd, dkd, degl],
        _qkv_specs() + [col, col, row, half, wide, wide, half, wide, wide, egl], [dspec, dspec, dspec, col, col, row],
        [per_dir, per_dir, per_dir, _sds((N_HD, t, LANES)), _sds((N_HD, t, LANES)), _sds((N_HD, nc, 1, DN_CHUNK))])


REC_HEADS = 8
REC_GROUPS = N_HD // REC_HEADS
REC_FWD_GROUPS = DN_HEADS // REC_HEADS


def _rec_specs(time_block):
    nr = REC_ROWS // DN_CHUNK
    wide = pl.BlockSpec((REC_HEADS, REC_ROWS, LANES), lambda g, b: (g, time_block(g, b), 0))
    half = pl.BlockSpec((REC_HEADS, REC_ROWS, DN_CHUNK), lambda g, b: (g, time_block(g, b), 0))
    egl = pl.BlockSpec((REC_HEADS, nr, 1, LANES), lambda g, b: (g, time_block(g, b), 0, 0))
    state = pl.BlockSpec((REC_HEADS, nr, DN_HEAD_DIM, DN_HEAD_DIM), lambda g, b: (g, time_block(g, b), 0, 0))
    return wide, half, egl, state


def _rec_head_cols(g):
    return jnp.where(g < REC_FWD_GROUPS, g, g - REC_FWD_GROUPS)


def _dn_rec_fwd(u, w, qk, qd, kd, egl):
    t = u.shape[1]
    nb = t // REC_ROWS
    nr = REC_ROWS // DN_CHUNK
    nc = t // DN_CHUNK

    def time_block(g, b):
        return jnp.where(g < REC_FWD_GROUPS, b, nb - 1 - b)

    wide, half, egl_spec, state = _rec_specs(time_block)
    o_spec = pl.BlockSpec((1, REC_ROWS, REC_HEADS * LANES),
                          lambda g, b: (jnp.where(g < REC_FWD_GROUPS, 0, 1), time_block(g, b), _rec_head_cols(g)))

    def body(u_ref, w_ref, qk_ref, qd_ref, kd_ref, egl_ref, o_ref, vn_ref, s_ref, s_scr):
        fwd = pl.program_id(0) < REC_FWD_GROUPS

        @pl.when(pl.program_id(1) == 0)
        def _():
            s_scr[...] = jnp.zeros_like(s_scr)

        def run(order):
            heads = range(REC_HEADS)
            s = [s_scr[j] for j in heads]
            for ce in order:
                rows = slice(ce * DN_CHUNK, (ce + 1) * DN_CHUNK)
                vn = [u_ref[j, rows, :] - _bnn(w_ref[j, rows, :], s[j]) for j in heads]
                o = [_bnn(qd_ref[j, rows, :], s[j]) + _bnn(qk_ref[j, rows, :], vn[j]) for j in heads]
                nxt = [s[j] * egl_ref[j, ce] + _btn(kd_ref[j, rows, :], vn[j]) for j in heads]
                for j in heads:
                    s_ref[j, ce] = s[j]
                    vn_ref[j, rows, :] = vn[j]
                    o_ref[0, rows, j * LANES:(j + 1) * LANES] = o[j]
                s = nxt
            for j in heads:
                s_scr[j] = s[j]

        pl.when(fwd)(lambda: run(range(nr)))
        pl.when(jnp.logical_not(fwd))(lambda: run(range(nr - 1, -1, -1)))

    return pl.pallas_call(
        body, name="dn_rec_fwd", grid=(REC_GROUPS, nb), in_specs=[wide, wide, half, wide, wide, egl_spec],
        out_specs=[o_spec, wide, state],
        out_shape=[_sds((2, t, DN_WIDTH)), _sds((N_HD, t, LANES)), _sds((N_HD, nc, DN_HEAD_DIM, DN_HEAD_DIM))],
        scratch_shapes=[pltpu.VMEM((REC_HEADS, DN_HEAD_DIM, DN_HEAD_DIM), F32)],
        compiler_params=_params(("parallel", "arbitrary")),
    )(u, w, qk, qd, kd, egl)


def _dn_rec_bwd(w, qk, qd, kd, egl, vn, states, do):
    t = w.shape[1]
    nb = t // REC_ROWS
    nr = REC_ROWS // DN_CHUNK
    nc = t // DN_CHUNK

    def time_block(g, b):
        return jnp.where(g < REC_FWD_GROUPS, nb - 1 - b, b)

    wide, half, egl_spec, state = _rec_specs(time_block)
    do_spec = pl.BlockSpec((REC_ROWS, REC_HEADS * LANES), lambda g, b: (time_block(g, b), _rec_head_cols(g)))

    def body(w_ref, qk_ref, qd_ref, kd_ref, egl_ref, vn_ref, s_ref, do_ref,
             du_ref, dw_ref, dqk_ref, dqd_ref, dkd_ref, degl_ref, ds_scr):
        fwd = pl.program_id(0) < REC_FWD_GROUPS

        @pl.when(pl.program_id(1) == 0)
        def _():
            ds_scr[...] = jnp.zeros_like(ds_scr)

        def run(order):
            heads = range(REC_HEADS)
            ds = [ds_scr[j] for j in heads]
            for ce in order:
                rows = slice(ce * DN_CHUNK, (ce + 1) * DN_CHUNK)
                s = [s_ref[j, ce] for j in heads]
                do_c = [do_ref[rows, j * LANES:(j + 1) * LANES] for j in heads]
                vn_c = [vn_ref[j, rows, :] for j in heads]
                dvn = [_btn(qk_ref[j, rows, :], do_c[j]) + _bnn(kd_ref[j, rows, :], ds[j]) for j in heads]
                nxt = [ds[j] * egl_ref[j, ce] + _btn(qd_ref[j, rows, :], do_c[j]) - _btn(w_ref[j, rows, :], dvn[j])
                       for j in heads]
                for j in heads:
                    du_ref[j, rows, :] = dvn[j]
                    dw_ref[j, rows, :] = -_bnt(dvn[j], s[j])
                for j in heads:
                    dqk_ref[j, rows, :] = _bnt(do_c[j], vn_c[j])
                    dqd_ref[j, rows, :] = _bnt(do_c[j], s[j])
                for j in heads:
                    dkd_ref[j, rows, :] = _bnt(vn_c[j], ds[j])
                    degl_ref[j, ce] = jnp.sum(s[j] * ds[j], axis=0, keepdims=True)
                ds = nxt
            for j in heads:
                ds_scr[j] = ds[j]

        pl.when(fwd)(lambda: run(range(nr - 1, -1, -1)))
        pl.when(jnp.logical_not(fwd))(lambda: run(range(nr)))

    big = _sds((N_HD, t, LANES))
    return pl.pallas_call(
        body, name="dn_rec_bwd", grid=(REC_GROUPS, nb), in_specs=[wide, half, wide, wide, egl_spec, wide, state, do_spec],
        out_specs=[wide, wide, half, wide, wide, egl_spec],
        out_shape=[big, big, _sds((N_HD, t, DN_CHUNK)), big, big, _sds((N_HD, nc, 1, LANES))],
        scratch_shapes=[pltpu.VMEM((REC_HEADS, DN_HEAD_DIM, DN_HEAD_DIM), F32)],
        compiler_params=_params(("parallel", "arbitrary")),
    )(w, qk, qd, kd, egl, vn, states, do)


def _post_fn(of, ob, z, gain):
    o = of + ob
    return o * lax.rsqrt(jnp.mean(o * o, axis=-1, keepdims=True) + RMS_EPS) * gain * _silu(z)


def _post_specs():
    o_spec = [pl.BlockSpec((1, ROW_TILE, DN_WIDTH), functools.partial(lambda i, d: (d, i, 0), d=d)) for d in (0, 1)]
    return o_spec, _cols(DN_WIDTH, C_Z), _rows(DN_WIDTH), _full((1, LANES))


def _post_fwd(o2, proj, gain):
    t = proj.shape[0]
    o_spec, z_spec, wide, gain_spec = _post_specs()

    def body(of_ref, ob_ref, z_ref, g_ref, out_ref):
        for h in range(DN_HEADS):
            cols = slice(h * LANES, (h + 1) * LANES)
            out_ref[:, cols] = _post_fn(of_ref[0, :, cols], ob_ref[0, :, cols], z_ref[:, cols], g_ref[...]).astype(_MXU)

    return pl.pallas_call(
        body, name="post_fwd", grid=(t // ROW_TILE,), in_specs=o_spec + [z_spec, gain_spec], out_specs=wide,
        out_shape=_sds((t, DN_WIDTH), _MXU), compiler_params=_params(("parallel",)),
    )(o2, o2, proj, gain)


def _post_bwd(o2, proj, gain, dout):
    t = proj.shape[0]
    o_spec, z_spec, wide, gain_spec = _post_specs()

    def body(of_ref, ob_ref, z_ref, g_ref, d_ref, do_ref, dz_ref, dg_ref):
        dg_sum = jnp.zeros((1, LANES), F32)
        for h in range(DN_HEADS):
            cols = slice(h * LANES, (h + 1) * LANES)
            _, vjp = jax.vjp(_post_fn, of_ref[0, :, cols], ob_ref[0, :, cols], z_ref[:, cols], g_ref[...])
            do, _, dz, dg = vjp(d_ref[:, cols])
            do_ref[:, cols] = do
            dz_ref[:, cols] = dz.astype(_MXU)
            dg_sum = dg_sum + dg
        _accumulate(dg_ref, dg_sum, pl.program_id(0))

    return pl.pallas_call(
        body, name="post_bwd", grid=(t // ROW_TILE,), in_specs=o_spec + [z_spec, gain_spec, wide],
        out_specs=[wide, wide, gain_spec], out_shape=[_sds((t, DN_WIDTH)), _sds((t, DN_WIDTH), _MXU), _sds((1, LANES))],
        compiler_params=_params(("arbitrary",)),
    )(o2, o2, proj, gain, dout)


def _rope(x, cos, sin):
    lane = lax.broadcasted_iota(jnp.int32, x.shape, 1)
    first = (lane & (SW_HEAD_DIM - 1)) < SW_HEAD_DIM // 2
    rot = jnp.where(first, -pltpu.roll(x, LANES - SW_HEAD_DIM // 2, 1), pltpu.roll(x, SW_HEAD_DIM // 2, 1))
    return x * cos + rot * sin


def _rope_apply(q, k, q_cols, k_cols, cos, sin, name, dtype):
    t = cos.shape[0]
    qw, kw = SW_HEADS * SW_HEAD_DIM, SW_KV_WIDTH

    def body(q_ref, k_ref, c_ref, s_ref, qo_ref, ko_ref):
        c, s = c_ref[...], s_ref[...]
        for j in range(qw // LANES):
            cols = slice(j * LANES, (j + 1) * LANES)
            qo_ref[:, cols] = _rope(q_ref[:, cols], c, s).astype(dtype)
        for j in range(kw // LANES):
            cols = slice(j * LANES, (j + 1) * LANES)
            ko_ref[:, cols] = _rope(k_ref[:, cols], c, s).astype(dtype)

    return pl.pallas_call(
        body, name=name, grid=(t // ROW_TILE,), in_specs=[_cols(qw, q_cols), _cols(kw, k_cols), _rows(LANES), _rows(LANES)],
        out_specs=[_rows(qw), _rows(kw)], out_shape=[_sds((t, qw), dtype), _sds((t, kw), dtype)],
        compiler_params=_params(("parallel",)),
    )(q, k, cos, sin)


def _attn_core(qs, kb, vb, sink, mask):
    s = _bnt(qs, kb) * (SW_HEAD_DIM ** -0.5)
    s = jnp.where(mask, s, -1e30)
    m = lax.stop_gradient(jnp.maximum(jnp.max(s, axis=1, keepdims=True), sink))
    e = jnp.exp(s - m)
    den = jnp.sum(e, axis=1, keepdims=True) + jnp.exp(sink - m)
    return _bnn(e / den, vb)


def _band_mask(n, nb):
    rows = SW_GROUP * SW_BLOCK
    i = lax.broadcasted_iota(jnp.int32, (rows, 3 * SW_BLOCK), 0) & (SW_BLOCK - 1)
    j = lax.broadcasted_iota(jnp.int32, (rows, 3 * SW_BLOCK), 1)
    near = (j - i >= 0) & (j - i <= 2 * SW_BLOCK)
    lo = jnp.where(n == 0, SW_BLOCK, 0)
    hi = jnp.where(n == nb - 1, 2 * SW_BLOCK, 3 * SW_BLOCK)
    return near & (j >= lo) & (j < hi)


def _band_specs(nb, v_cols):
    def spec(width, base, shift):
        return pl.BlockSpec((SW_BLOCK, width), lambda n: (jnp.clip(n + shift, 0, nb - 1), base // width))
    k_specs = [spec(SW_KV_WIDTH, 0, s) for s in (-1, 0, 1)]
    v_specs = [spec(SW_KV_WIDTH, v_cols, s) for s in (-1, 0, 1)]
    return k_specs, v_specs


def _head_cols(kv, g):
    h = kv * SW_GROUP + g
    return slice(h * SW_HEAD_DIM, (h + 1) * SW_HEAD_DIM)


def _kv_batches(q_ref, kb, vb, s_ref):
    kvs = range(SW_KV_HEADS)
    cols = lambda kv: slice(kv * SW_HEAD_DIM, (kv + 1) * SW_HEAD_DIM)
    qs = jnp.stack([jnp.concatenate([q_ref[:, _head_cols(kv, g)] for g in range(SW_GROUP)], axis=0) for kv in kvs])
    sinks = jnp.stack([jnp.concatenate([jnp.broadcast_to(s_ref[kv * SW_GROUP + g], (SW_BLOCK, 1)) for g in range(SW_GROUP)],
                                       axis=0) for kv in kvs])
    return qs, jnp.stack([kb[:, cols(kv)] for kv in kvs]), jnp.stack([vb[:, cols(kv)] for kv in kvs]), sinks


def _attn_fwd(qr, kr, proj, sinks):
    t = qr.shape[0]
    nb = t // SW_BLOCK
    qw = SW_HEADS * SW_HEAD_DIM
    k_specs, v_specs = _band_specs(nb, C_VSW)
    q_spec = pl.BlockSpec((SW_BLOCK, qw), lambda n: (n, 0))

    def body(q_ref, k0, k1, k2, v0, v1, v2, s_ref, o_ref):
        mask = _band_mask(pl.program_id(0), nb)
        kb = jnp.concatenate([k0[...], k1[...], k2[...]], axis=0)
        vb = jnp.concatenate([v0[...], v1[...], v2[...]], axis=0)
        qs, kbs, vbs, sinks_ = _kv_batches(q_ref, kb, vb, s_ref)
        o = jax.vmap(functools.partial(_attn_core, mask=mask))(qs, kbs, vbs, sinks_)
        for kv in range(SW_KV_HEADS):
            for g in range(SW_GROUP):
                o_ref[:, _head_cols(kv, g)] = o[kv, g * SW_BLOCK:(g + 1) * SW_BLOCK, :].astype(_MXU)

    return pl.pallas_call(
        body, name="attn_fwd", grid=(nb,), in_specs=[q_spec] + k_specs + v_specs + [_full((SW_HEADS, 1, 1))],
        out_specs=q_spec, out_shape=_sds((t, qw), _MXU), compiler_params=_params(("parallel",)),
    )(qr, kr, kr, kr, proj, proj, proj, sinks)


def _attn_bwd(qr, kr, proj, sinks, do):
    t = qr.shape[0]
    nb = t // SW_BLOCK
    qw = SW_HEADS * SW_HEAD_DIM
    k_specs, v_specs = _band_specs(nb, C_VSW)
    q_spec = pl.BlockSpec((SW_BLOCK, qw), lambda n: (n, 0))
    part = pl.BlockSpec((1, 3 * SW_BLOCK, SW_KV_WIDTH), lambda n: (n, 0, 0))

    def body(q_ref, k0, k1, k2, v0, v1, v2, s_ref, do_ref, dq_ref, dk_ref, dv_ref, ds_ref):
        mask = _band_mask(pl.program_id(0), nb)
        kb = jnp.concatenate([k0[...], k1[...], k2[...]], axis=0).astype(F32)
        vb = jnp.concatenate([v0[...], v1[...], v2[...]], axis=0)

        @pl.when(pl.program_id(0) == 0)
        def _():
            ds_ref[...] = jnp.zeros_like(ds_ref)

        qs, kbs, vbs, sinks_ = _kv_batches(q_ref, kb, vb, s_ref)
        dos = jnp.stack([jnp.concatenate([do_ref[:, _head_cols(kv, g)] for g in range(SW_GROUP)], axis=0)
                         for kv in range(SW_KV_HEADS)])

        def head_bwd(q_, k_, v_, sink_, do_):
            _, vjp = jax.vjp(functools.partial(_attn_core, mask=mask), q_, k_, v_, sink_)
            return vjp(do_)

        dqs, dkb, dvb, dsink = jax.vmap(head_bwd)(qs.astype(F32), kbs, vbs, sinks_, dos)
        for kv in range(SW_KV_HEADS):
            kvc = slice(kv * SW_HEAD_DIM, (kv + 1) * SW_HEAD_DIM)
            dk_ref[0, :, kvc] = dkb[kv]
            dv_ref[0, :, kvc] = dvb[kv]
            for g in range(SW_GROUP):
                rows = slice(g * SW_BLOCK, (g + 1) * SW_BLOCK)
                dq_ref[:, _head_cols(kv, g)] = dqs[kv, rows, :]
                ds_ref[kv * SW_GROUP + g] += jnp.sum(dsink[kv, rows, :], axis=0, keepdims=True)

    parts = _sds((nb, 3 * SW_BLOCK, SW_KV_WIDTH))
    return pl.pallas_call(
        body, name="attn_bwd", grid=(nb,), in_specs=[q_spec] + k_specs + v_specs + [_full((SW_HEADS, 1, 1)), q_spec],
        out_specs=[q_spec, part, part, _full((SW_HEADS, 1, 1))], out_shape=[_sds((t, qw)), parts, parts, _sds((SW_HEADS, 1, 1))],
        compiler_params=_params(("arbitrary",)),
    )(qr, kr, kr, kr, proj, proj, proj, sinks, do)


def _band_sum(parts, name, dtype):
    nb = parts.shape[0]
    w = parts.shape[2]

    def spec(shift, slot):
        return pl.BlockSpec((1, SW_BLOCK, w), lambda m: (jnp.clip(m + shift, 0, nb - 1), slot, 0))

    def body(prev_ref, own_ref, next_ref, o_ref):
        m = pl.program_id(0)
        total = own_ref[0] + jnp.where(m > 0, prev_ref[0], 0.0) + jnp.where(m < nb - 1, next_ref[0], 0.0)
        o_ref[...] = total.astype(dtype)

    return pl.pallas_call(
        body, name=name, grid=(nb,), in_specs=[spec(-1, 2), spec(0, 1), spec(1, 0)],
        out_specs=pl.BlockSpec((SW_BLOCK, w), lambda m: (m, 0)), out_shape=_sds((nb * SW_BLOCK, w), dtype),
        compiler_params=_params(("parallel",)),
    )(parts, parts, parts)


def _gate_rows(gbo):
    t = gbo.shape[0]
    return gbo[:, N_HD:2 * N_HD].T.reshape(N_HD, t // DN_CHUNK, 1, DN_CHUNK)


def _gate_rows_t(dgc_r):
    t = dgc_r.shape[1] * DN_CHUNK
    return jnp.pad(dgc_r.reshape(N_HD, t).T, ((0, 0), (N_HD, LANES - 2 * N_HD)))


def _layer_fwd(x, xb, w, cos, sin, carry=None):
    proj = _mm(xb, w["wm"], "nn", name="proj")
    ba = _mm(xb, w["wba"], "nn", name="proj_gates")
    qn, kn, vv = [_prep_fwd(proj, w["conv"], kind) for kind in range(3)]
    gbo, beta_c, gc_c = _gb_fwd(ba, w["alog"], w["dtb"])
    gc_r = _gate_rows(gbo)
    u, wk, qk, qd, kd, egl, tinv, *carried = _dn_local_fwd(qn, kn, vv, beta_c, gc_c, gc_r, carry=carry)
    o2, vn, states = _dn_rec_fwd(u, wk, qk, qd, kd, egl)
    o_dn = _post_fwd(o2, proj, w["dnw"])
    qr, kr = _rope_apply(proj, proj, C_QSW, C_KSW, cos, sin, "rope_fwd", _MXU)
    o_sw = _attn_fwd(qr, kr, proj, w["sinks"])
    ya = _mm(o_dn, w["wa"], "nn", name="branch_a")
    yb = _mm(o_sw, w["wb"], "nn", name="branch_b")
    merged = _merge_fwd(proj, ya, yb)
    mix = _mm(merged, w["wo"], "nn", name="mix_out")
    x1, x1b = _ln_fwd(x, mix, w["ln1g"], w["ln1b"], "ln1_fwd")
    gu = _mm(x1b, w["wgu"], "nn", name="ffn_up")
    h = _swiglu_fwd(gu)
    f = _mm(h, w["wd"], "nn", name="ffn_down")
    x2, x2b = _ln_fwd(x1, f, w["ln2g"], w["ln2b"], "ln2_fwd")
    res = dict(x=x, xb=xb, proj=proj, ba=ba, qn=qn, kn=kn, vv=vv, beta_c=beta_c, gc_c=gc_c, gc_r=gc_r, wk=wk, qk=qk, qd=qd, kd=kd, egl=egl, tinv=tinv, vn=vn,
               states=states, o2=o2, o_dn=o_dn, qr=qr, kr=kr, o_sw=o_sw, ya=ya, yb=yb, merged=merged, mix=mix, x1=x1, x1b=x1b,
               gu=gu, h=h, f=f)
    return x2, x2b, res, (carried[0] if carried else None)


def _layer_bwd(dx2, w, r, cos, sin, carry=None):
    dx1, df, dln2g, dln2b = _ln_bwd(r["x1"], r["f"], w["ln2g"], w["ln2b"], dx2, "ln2_bwd")
    dh = _mm(df, w["wd"], "nt", name="d_ffn_hidden")
    dwd = _mm(r["h"], df, "tn", name="dw_ffn_down", out_dtype=_MXU)
    dgu = _swiglu_bwd(r["gu"], dh)
    dwgu = _mm(r["x1b"], dgu, "tn", name="dw_ffn_up", out_dtype=_MXU)
    dx1 = _mm(dgu, w["wgu"], "nt", name="dx_ffn", add=dx1)
    dx, dmix, dln1g, dln1b = _ln_bwd(r["x"], r["mix"], w["ln1g"], w["ln1b"], dx1, "ln1_bwd")
    dmerged = _mm(dmix, w["wo"], "nt", name="d_merged")
    dwo = _mm(r["merged"], dmix, "tn", name="dw_mix_out", out_dtype=_MXU)
    dga, dgb, dya, dyb = _merge_bwd(r["proj"], r["ya"], r["yb"], dmerged)
    dwa = _mm(r["o_dn"], dya, "tn", name="dw_branch_a", out_dtype=_MXU)
    do_dn = _mm(dya, w["wa"], "nt", name="d_branch_a")
    dwb = _mm(r["o_sw"], dyb, "tn", name="dw_branch_b", out_dtype=_MXU)
    do_sw = _mm(dyb, w["wb"], "nt", name="d_branch_b")
    do, dz, ddnw = _post_bwd(r["o2"], r["proj"], w["dnw"], do_dn)
    du, dwk, dqk, dqd, dkd, degl = _dn_rec_bwd(r["wk"], r["qk"], r["qd"], r["kd"], r["egl"], r["vn"], r["states"], do)
    dq3, dk3, dv3, dbeta_c, dgc_c, dgc_r, *carried = _dn_local_bwd(r["qn"], r["kn"], r["vv"], r["beta_c"], r["gc_c"], r["gc_r"],
                                                                   r["tinv"], du, dwk, dqk, dqd, dkd, degl, carry=carry)
    dqkv, dconv = zip(*[_prep_bwd(r["proj"], w["conv"], d2, kind) for kind, d2 in enumerate((dq3, dk3, dv3))])
    dconv = jnp.concatenate(dconv, axis=1)
    dba, dalog, ddtb = _gb_bwd(r["ba"], w["alog"], w["dtb"], dbeta_c, dgc_c, _gate_rows_t(dgc_r))
    dqr, dkparts, dvparts, dsinks = _attn_bwd(r["qr"], r["kr"], r["proj"], w["sinks"], do_sw)
    dkr = _band_sum(dkparts, "attn_dk_sum", F32)
    dv = _band_sum(dvparts, "attn_dv_sum", _MXU)
    dq_sw, dk_sw = _rope_apply(dqr, dkr, 0, 0, cos, -sin, "rope_bwd", _MXU)
    dproj = jnp.concatenate([*dqkv, dz, dq_sw, dga, dgb, dk_sw, dv], axis=1)
    dwm = _mm(r["xb"], dproj, "tn", name="dw_proj", out_dtype=_MXU)
    dwba = _mm(r["xb"], dba, "tn", name="dw_proj_gates", out_dtype=_MXU)
    dx = _mm(dproj, w["wm"], "nt", name="dx_proj", add=dx)
    dx = _mm(dba, w["wba"], "nt", name="dx_proj_gates", add=dx)
    grads = dict(wm=dwm, wba=dwba, conv=dconv, alog=dalog, dtb=ddtb, dnw=ddnw, sinks=dsinks, wa=dwa, wb=dwb, wo=dwo,
                 ln1g=dln1g, ln1b=dln1b, wgu=dwgu, wd=dwd, ln2g=dln2g, ln2b=dln2b)
    return dx, grads, (carried[0] if carried else None)


def _rope_tables(t):
    half = SW_HEAD_DIM // 2
    inv_freq = ROPE_THETA ** (-jnp.arange(half, dtype=F32) / half)
    ang = jnp.arange(t, dtype=F32)[:, None] * inv_freq[None, :]
    return jnp.tile(jnp.cos(ang), (1, LANES // half)), jnp.tile(jnp.sin(ang), (1, LANES // half))


def _trunk(x, target, n_layers, layer_weights, fwd_carry, grads_done):
    cos, sin = _rope_tables(x.shape[0])
    xb = x.astype(_MXU)
    saved, weights, carried = [], [], None
    for i in range(n_layers):
        w = layer_weights(i, carried)
        x, xb, res, carried = _layer_fwd(x, xb, w, cos, sin, carry=fwd_carry(i))
        saved.append(res)
        weights.append(w)
    sq, dx = _loss_head(x, target)
    carry = None
    for i in reversed(range(n_layers)):
        dx, grads, carried = _layer_bwd(dx, weights[i], saved[i], cos, sin, carry=carry)
        carry = grads_done(i, grads, carried)
    return sq, dx, carry


N_CHIPS = 4


def _mesh_pos():
    return lax.axis_index("x"), lax.axis_index("y"), lax.axis_index("c")


def _other_chips(x, y):
    return [(1 - x, y), (x, 1 - y), (1 - x, 1 - y)]


def _remote_copy(src, dst, sems, k, to):
    send_sems, recv_sems = sems
    return pltpu.make_async_remote_copy(src_ref=src, dst_ref=dst, send_sem=send_sems.at[k], recv_sem=recv_sems.at[k],
                                        device_id=to, device_id_type=pl.DeviceIdType.MESH)


def _comm_call(body, name, out_shape, n_sems, *operands):
    return pl.pallas_call(
        body, name=name, in_specs=[pl.BlockSpec(memory_space=pl.ANY)] * len(operands),
        out_specs=pl.BlockSpec(memory_space=pl.ANY), out_shape=out_shape,
        scratch_shapes=[pltpu.SemaphoreType.DMA((n_sems,)), pltpu.SemaphoreType.DMA((n_sems,)), pltpu.SemaphoreType.DMA],
        compiler_params=pltpu.CompilerParams(has_side_effects=True),
    )(*operands)


class _Gather:
    n_sems = N_DEV - 1

    @staticmethod
    def out_shape(block):
        return _sds((N_DEV,) + block.shape, block.dtype)

    @staticmethod
    def _own(x_ref, o_ref, sems, local_sem):
        x, y, c = _mesh_pos()
        mine = o_ref.at[4 * x + 2 * y + c]
        first = [_remote_copy(x_ref, mine, sems, 0, (x, y, 1 - c))]
        first += [_remote_copy(x_ref, mine, sems, 1 + j, (*chip, c)) for j, chip in enumerate(_other_chips(x, y))]
        return pltpu.make_async_copy(x_ref, mine, local_sem), first

    @classmethod
    def start(cls, x_ref, o_ref, sems, local_sem):
        mine, first = cls._own(x_ref, o_ref, sems, local_sem)
        mine.start()
        for cp in first:
            cp.start()

    @classmethod
    def finish(cls, x_ref, o_ref, sems, local_sem):
        x, y, c = _mesh_pos()
        sibling = (x, y, 1 - c)
        chips = _other_chips(x, y)
        slot = lambda px, py, pc: o_ref.at[4 * px + 2 * py + pc]
        mine, first = cls._own(x_ref, o_ref, sems, local_sem)
        passed = [_remote_copy(slot(*chip, c), slot(*chip, c), sems, 4 + j, sibling) for j, chip in enumerate(chips)]
        for j, chip in enumerate(chips):
            _remote_copy(x_ref, slot(*chip, c), sems, 1 + j, sibling).wait_recv()
            passed[j].start()
        _remote_copy(x_ref, slot(x, y, 1 - c), sems, 0, sibling).wait_recv()
        for j, chip in enumerate(chips):
            _remote_copy(x_ref, slot(*chip, 1 - c), sems, 4 + j, sibling).wait_recv()
        for cp in first + passed:
            cp.wait_send()
        mine.wait()


class _ChipExchange:
    n_sems = N_CHIPS - 1

    @staticmethod
    def out_shape(parts):
        return _sds(parts.shape, parts.dtype)

    @staticmethod
    def _own(x_ref, o_ref, sems, local_sem):
        x, y, c = _mesh_pos()
        me = 2 * x + y
        sent = [_remote_copy(x_ref.at[2 * cx + cy], o_ref.at[me], sems, j, (cx, cy, c))
                for j, (cx, cy) in enumerate(_other_chips(x, y))]
        return pltpu.make_async_copy(x_ref.at[me], o_ref.at[me], local_sem), sent

    @classmethod
    def start(cls, x_ref, o_ref, sems, local_sem):
        mine, sent = cls._own(x_ref, o_ref, sems, local_sem)
        mine.start()
        for cp in sent:
            cp.start()

    @classmethod
    def finish(cls, x_ref, o_ref, sems, local_sem):
        x, y, c = _mesh_pos()
        mine, sent = cls._own(x_ref, o_ref, sems, local_sem)
        for j, (cx, cy) in enumerate(_other_chips(x, y)):
            _remote_copy(x_ref.at[2 * x + y], o_ref.at[2 * cx + cy], sems, j, (cx, cy, c)).wait_recv()
        for cp in sent:
            cp.wait_send()
        mine.wait()


def _exchange_alone(kind, operand, name):
    def body(x_ref, o_ref, send_sems, recv_sems, local_sem):
        kind.start(x_ref, o_ref, (send_sems, recv_sems), local_sem)
        kind.finish(x_ref, o_ref, (send_sems, recv_sems), local_sem)

    return _comm_call(body, name, kind.out_shape(operand), kind.n_sems, operand)


def _all_gather(block, name):
    return _exchange_alone(_Gather, block, name)


def _carried(kind, operand, body, n_in, n_out, grid):
    hbm = pl.BlockSpec(memory_space=pl.ANY)

    def wrapped(*refs):
        ins, x_ref = refs[:n_in], refs[n_in]
        outs, o_ref = refs[n_in + 1:n_in + 1 + n_out], refs[n_in + 1 + n_out]
        send_sems, recv_sems, local_sem = refs[n_in + n_out + 2:n_in + n_out + 5]
        rest = refs[n_in + n_out + 5:]
        first, last = None, None
        for axis, size in enumerate(grid):
            at0, at1 = pl.program_id(axis) == 0, pl.program_id(axis) == size - 1
            first = at0 if first is None else first & at0
            last = at1 if last is None else last & at1
        pl.when(first)(lambda: kind.start(x_ref, o_ref, (send_sems, recv_sems), local_sem))
        body(*ins, *outs, *rest)
        pl.when(last)(lambda: kind.finish(x_ref, o_ref, (send_sems, recv_sems), local_sem))

    sems = [pltpu.SemaphoreType.DMA((kind.n_sems,)), pltpu.SemaphoreType.DMA((kind.n_sems,)), pltpu.SemaphoreType.DMA]
    return wrapped, [hbm], [hbm], [kind.out_shape(operand)], sems


def _sibling_swap(parts, name):
    def body(x_ref, o_ref, send_sems, recv_sems, local_sem):
        x, y, c = _mesh_pos()
        sems = (send_sems, recv_sems)
        sibling = (x, y, 1 - c)
        copies = [_remote_copy(x_ref.at[2 * q + (1 - c)], o_ref.at[q], sems, q, sibling) for q in range(N_CHIPS)]
        for cp in copies:
            cp.start()
        for cp in copies:
            cp.wait()

    return _comm_call(body, name, _sds((N_CHIPS,) + parts.shape[1:], parts.dtype), N_CHIPS, parts)


def _pair_sum(a, b, name):
    n, rows, cols = a.shape
    tr = rows if rows <= 512 else _row_tile(rows, 2048)
    blk = pl.BlockSpec((1, tr, cols), lambda q, i: (q, i, 0))

    def body(a_ref, b_ref, o_ref):
        o_ref[...] = (a_ref[...].astype(F32) + b_ref[...].astype(F32)).astype(o_ref.dtype)

    return pl.pallas_call(
        body, name=name, grid=(n, rows // tr), in_specs=[blk, blk], out_specs=blk, out_shape=_sds(a.shape, a.dtype),
        compiler_params=_params(("parallel", "parallel")),
    )(a, b)


def _chip_sums(parts, name):
    c = lax.axis_index("c")
    from_sibling = _sibling_swap(parts, "swap_" + name)
    own = lax.dynamic_index_in_dim(parts.reshape((N_CHIPS, 2) + parts.shape[1:]), c, axis=1, keepdims=False)
    return _pair_sum(own, from_sibling, "pair_sum_" + name)


def _reduce_to_owner(parts, name):
    return _exchange_alone(_ChipExchange, _chip_sums(parts, name), "exchange_" + name)


def _sum_adamw(parts, w, m, v, name):
    rows, cols = w.shape
    n_parts = parts.shape[0]
    tr = rows if rows <= 512 else _row_tile(rows)
    blk = pl.BlockSpec((tr, cols), lambda i: (i, 0))

    def body(p_ref, w_ref, m_ref, v_ref, g_ref, d_ref, nm_ref, nv_ref):
        g = p_ref[0].astype(F32)
        for i in range(1, n_parts):
            g = g + p_ref[i].astype(F32)
        nm = ADAM_B1 * m_ref[...] + (1.0 - ADAM_B1) * g
        nv = ADAM_B2 * v_ref[...] + (1.0 - ADAM_B2) * jnp.square(g)
        m_hat = nm / (1.0 - ADAM_B1 ** ADAM_STEP)
        v_hat = nv / (1.0 - ADAM_B2 ** ADAM_STEP)
        g_ref[...] = g
        d_ref[...] = -ADAM_LR * (m_hat / (jnp.sqrt(v_hat) + ADAM_EPS) + ADAM_WD * w_ref[...])
        nm_ref[...] = nm
        nv_ref[...] = nv

    return pl.pallas_call(
        body, name=name, grid=(rows // tr,), in_specs=[pl.BlockSpec((n_parts, tr, cols), lambda i: (0, i, 0)), blk, blk, blk],
        out_specs=[blk] * 4, out_shape=[_sds((rows, cols))] * 4, compiler_params=_params(("parallel",)),
    )(parts, w, m, v)


def _row_tile(rows, pref=256):
    t = pref
    while t >= 8:
        if rows % t == 0:
            return t
        t //= 2
    return rows


def _gathered_cols(g):
    g = jnp.moveaxis(g, 0, -2)
    return g.reshape(g.shape[:-2] + (g.shape[-2] * g.shape[-1],))


def _gathered_rows(g):
    g = jnp.moveaxis(g, 0, -3)
    return g.reshape(g.shape[:-3] + (g.shape[-3] * g.shape[-2], g.shape[-1]))


def _col_parts(full):
    c = full.shape[-1]
    return jnp.moveaxis(full.reshape(full.shape[:-1] + (N_DEV, c // N_DEV)), -2, 0)


def _row_parts(full):
    rows, c = full.shape[-2:]
    return jnp.moveaxis(full.reshape(full.shape[:-2] + (N_DEV, rows // N_DEV, c)), -3, 0)


SHARD_SHAPES = ((D_MODEL, IN_COLS // N_DEV), (D_MODEL, 2 * FFN_HIDDEN // N_DEV), (DN_WIDTH // N_DEV, D_MODEL),
                (SW_HEADS * SW_HEAD_DIM // N_DEV, D_MODEL), (D_MODEL // N_DEV, D_MODEL), (FFN_HIDDEN // N_DEV, D_MODEL))
PACK_COLS = D_MODEL
PACK_ROWS = 2816
assert sum(r for r, c in SHARD_SHAPES) <= PACK_ROWS and all(c <= PACK_COLS for r, c in SHARD_SHAPES)


def _pack(pieces):
    nd = pieces[0].ndim
    wide = [jnp.pad(p, [(0, 0)] * (nd - 1) + [(0, PACK_COLS - p.shape[-1])]) for p in pieces]
    rows = sum(p.shape[-2] for p in pieces)
    wide.append(jnp.zeros(pieces[0].shape[:-2] + (PACK_ROWS - rows, PACK_COLS), pieces[0].dtype))
    return jnp.concatenate(wide, axis=-2)


def _unpack(packed):
    out, off = [], 0
    for r, c in SHARD_SHAPES:
        out.append(packed[..., off:off + r, :c])
        off += r
    return out


def _w_in_split(w_in):
    s = lambda a, n: w_in[..., a:a + n]
    main = jnp.concatenate([s(R_QKV, 3072), s(R_Z, 1024), s(R_QSW, 1024), s(R_G, 2048), s(R_KSW, 256), s(R_VSW, 256)], axis=-1)
    gates = jnp.pad(s(R_BA, 2 * N_HD), [(0, 0)] * (w_in.ndim - 1) + [(0, LANES - 2 * N_HD)])
    return main, gates


def _w_in_join(dmain, dgates):
    s = lambda a, n: dmain[..., a:a + n]
    return jnp.concatenate([s(C_QKV, 3072), s(C_Z, 1024), dgates[..., :2 * N_HD], s(C_QSW, 1024), s(C_KSW, 256), s(C_VSW, 256),
                            s(C_GA, 2048)], axis=-1)


def _lane_row(a, offset):
    l, n = a.shape
    return jnp.pad(a, ((0, 0), (offset, LANES - offset - n)))[:, None, :]


def kernel(x, w_in, conv_w, a_log, dt_bias, dn_norm_w, sinks, w_branch_a, w_branch_b, w_out, ln1_g, ln1_b, w_gate_up, w_down, ln2_g, ln2_b, loss_target, m_w_in, m_conv_w, m_a_log, m_dt_bias, m_dn_norm_w, m_sinks, m_w_branch_a, m_w_branch_b, m_w_out, m_ln1_g, m_ln1_b, m_w_gate_up, m_w_down, m_ln2_g, m_ln2_b, v_w_in, v_conv_w, v_a_log, v_dt_bias, v_dn_norm_w, v_sinks, v_w_branch_a, v_w_branch_b, v_w_out, v_ln1_g, v_ln1_b, v_w_gate_up, v_w_down, v_ln2_g, v_ln2_b):
    l = DEPTH
    bf = lambda a: a.astype(_MXU)
    packed = _pack([bf(w_in), bf(w_gate_up), bf(w_branch_a), bf(w_branch_b), bf(w_out), bf(w_down)])
    first = _all_gather(packed[0], "gather_layer_0")
    conv_full = _gathered_cols(_all_gather(conv_w, "gather_conv_w"))
    row = lambda a: a[:, None, :]
    small = dict(
        conv=jnp.pad(conv_full, ((0, 0), (0, 8 - DN_CONV), (0, 0))), alog=_lane_row(a_log.reshape(l, N_HD), N_HD),
        dtb=_lane_row(dt_bias.reshape(l, N_HD), N_HD), dnw=row(dn_norm_w), sinks=sinks.reshape(l, SW_HEADS, 1, 1),
        ln1g=row(ln1_g), ln1b=row(ln1_b), ln2g=row(ln2_g), ln2b=row(ln2_b))

    def layer_weights(i, carried):
        s_in, s_gu, s_a, s_b, s_o, s_d = _unpack(first if i == 0 else carried)
        wm, wba = _w_in_split(_gathered_cols(s_in))
        return dict(wm=wm, wba=wba, wgu=_gathered_cols(s_gu), wa=_gathered_rows(s_a), wb=_gathered_rows(s_b),
                    wo=_gathered_rows(s_o), wd=_gathered_rows(s_d), **{k: a[i] for k, a in small.items()})

    def fwd_carry(i):
        return (_Gather, packed[i + 1]) if i + 1 < l else None

    layer_grads, received, waiting = [None] * l, [None] * l, []

    def grads_done(i, g_i, carried):
        if waiting:
            received[waiting.pop()] = carried
        layer_grads[i] = g_i
        parts = _pack([_col_parts(_w_in_join(g_i["wm"], g_i["wba"])), _col_parts(g_i["wgu"]), _row_parts(g_i["wa"]),
                       _row_parts(g_i["wb"]), _row_parts(g_i["wo"]), _row_parts(g_i["wd"])])
        waiting.append(i)
        return (_ChipExchange, _chip_sums(parts, f"layer_{i}"))

    sq, dx, last = _trunk(x[0], loss_target[0], l, layer_weights, fwd_carry, grads_done)
    received[waiting.pop()] = _exchange_alone(last[0], last[1], "exchange_layer_0")
    loss = lax.psum(0.5 * sq[0, 0] / D_MODEL, ("x", "y", "c"))
    g = {k: jnp.stack([gi[k] for gi in layer_grads]) for k in small}

    def adamw(parts, w, m, v, name):
        rows = w.shape[0] * w.shape[1]
        flat = lambda a: a.reshape(rows, a.shape[-1])
        outs = _sum_adamw(parts.reshape(parts.shape[0], rows, w.shape[-1]), flat(w), flat(m), flat(v), "adamw_" + name)
        return [o.reshape(w.shape) for o in outs]

    got = [jnp.stack(per_layer, axis=1) for per_layer in zip(*[_unpack(r) for r in received])]
    dconv = _reduce_to_owner(_col_parts(g["conv"][:, :DN_CONV, :]).reshape(N_DEV, l * DN_CONV, -1), "conv_w")
    results = {
        "w_in": adamw(got[0], w_in, m_w_in, v_w_in, "w_in"),
        "conv_w": adamw(dconv.reshape(N_CHIPS, l, DN_CONV, -1), conv_w, m_conv_w, v_conv_w, "conv_w"),
        "w_branch_a": adamw(got[2], w_branch_a, m_w_branch_a, v_w_branch_a, "w_branch_a"),
        "w_branch_b": adamw(got[3], w_branch_b, m_w_branch_b, v_w_branch_b, "w_branch_b"),
        "w_out": adamw(got[4], w_out, m_w_out, v_w_out, "w_out"),
        "w_gate_up": adamw(got[1], w_gate_up, m_w_gate_up, v_w_gate_up, "w_gate_up"),
        "w_down": adamw(got[5], w_down, m_w_down, v_w_down, "w_down"),
    }

    small_w = {"a_log": a_log.reshape(l, N_HD), "dt_bias": dt_bias.reshape(l, N_HD), "dn_norm_w": dn_norm_w, "sinks": sinks,
               "ln1_g": ln1_g, "ln1_b": ln1_b, "ln2_g": ln2_g, "ln2_b": ln2_b}
    small_m = {"a_log": m_a_log, "dt_bias": m_dt_bias, "dn_norm_w": m_dn_norm_w, "sinks": m_sinks, "ln1_g": m_ln1_g,
               "ln1_b": m_ln1_b, "ln2_g": m_ln2_g, "ln2_b": m_ln2_b}
    small_v = {"a_log": v_a_log, "dt_bias": v_dt_bias, "dn_norm_w": v_dn_norm_w, "sinks": v_sinks, "ln1_g": v_ln1_g,
               "ln1_b": v_ln1_b, "ln2_g": v_ln2_g, "ln2_b": v_ln2_b}
    small_g = {"a_log": g["alog"][:, 0, N_HD:2 * N_HD], "dt_bias": g["dtb"][:, 0, N_HD:2 * N_HD], "dn_norm_w": g["dnw"][:, 0, :],
               "sinks": g["sinks"].reshape(l, SW_HEADS), "ln1_g": g["ln1g"][:, 0, :], "ln1_b": g["ln1b"][:, 0, :],
               "ln2_g": g["ln2g"][:, 0, :], "ln2_b": g["ln2b"][:, 0, :]}
    names = list(small_w)
    cat = lambda d: jnp.concatenate([d[n].reshape(l, -1) for n in names], axis=1)
    widths = [small_w[n].shape[1] for n in names]
    total = sum(widths)
    padded = -(-total // LANES) * LANES
    pad = lambda a: jnp.pad(a, ((0, 8 - l), (0, padded - total)))
    got = _all_gather(pad(cat(small_g)), "gather_small_grads")
    outs = _sum_adamw(got, pad(cat(small_w)), pad(cat({n: small_m[n].reshape(l, -1) for n in names})),
                      pad(cat({n: small_v[n].reshape(l, -1) for n in names})), "adamw_small")
    off = 0
    for n, wd_ in zip(names, widths):
        shape = {"a_log": a_log.shape, "dt_bias": dt_bias.shape}.get(n, small_w[n].shape)
        results[n] = [o[:l, off:off + wd_].reshape(shape) for o in outs]
        off += wd_

    order = ["w_in", "conv_w", "a_log", "dt_bias", "dn_norm_w", "sinks", "w_branch_a", "w_branch_b", "w_out", "ln1_g", "ln1_b",
             "w_gate_up", "w_down", "ln2_g", "ln2_b"]
    return (loss, dx[None], *[results[n][0] for n in order], *[results[n][1] for n in order],
            *[results[n][2] for n in order], *[results[n][3] for n in order])
```

```python
import functools

import jax
import jax.numpy as jnp
from jax import lax
from jax.experimental import pallas as pl
from jax.experimental.pallas import tpu as pltpu

F32 = jnp.float32
_MXU = jnp.bfloat16
_HI = lax.Precision.HIGHEST
_MID = lax.Precision.HIGH

N_DEV = 8
D_MODEL = 1024
DEPTH = 4
DN_HEADS = 8
DN_HEAD_DIM = 128
DN_WIDTH = DN_HEADS * DN_HEAD_DIM
DN_CONV = 5
DN_CHUNK = 64
SW_HEADS = 16
SW_KV_HEADS = 4
SW_HEAD_DIM = 64
SW_GROUP = SW_HEADS // SW_KV_HEADS
SW_BLOCK = 128
SW_KV_WIDTH = SW_KV_HEADS * SW_HEAD_DIM
ROPE_THETA = 10000.0
FFN_HIDDEN = 2816
DN_ALPHA = (2.0 * DEPTH) ** 0.25
LN_EPS = 1e-5
RMS_EPS = 1e-6
ADAM_LR = 0.001
ADAM_B1 = 0.9
ADAM_B2 = 0.999
ADAM_EPS = 1e-08
ADAM_WD = 0.01
ADAM_STEP = 10

LANES = 128
N_HD = 2 * DN_HEADS
DN_GROUP = 4 * DN_CHUNK
INV_SUB = 16
LOCAL_ROWS = 512
REC_ROWS = 256
ROW_TILE = 256
VMEM_LIMIT = 48 << 20

C_QKV, C_Z, C_QSW, C_GA, C_GB, C_KSW, C_VSW = 0, 3072, 4096, 5120, 6144, 7168, 7424
MAIN_COLS = 7680
R_QKV, R_Z, R_BA, R_QSW, R_KSW, R_VSW, R_G = 0, 3072, 4096, 4128, 5152, 5408, 5664
IN_COLS = 7712


_NN = ((1,), (0,))
_NT = ((1,), (1,))
_TN = ((0,), (0,))


def _dg(a, b, dims, precision):
    if precision is not None:
        return lax.dot_general(a, b, (dims, ((), ())), precision=precision, preferred_element_type=F32)
    return lax.dot_general(a.astype(_MXU), b.astype(_MXU), (dims, ((), ())), preferred_element_type=F32)


def _make_dots(hi):
    @jax.custom_vjp
    def nn(a, b):
        return _dg(a, b, _NN, hi)

    @jax.custom_vjp
    def nt(a, b):
        return _dg(a, b, _NT, hi)

    @jax.custom_vjp
    def tn(a, b):
        return _dg(a, b, _TN, hi)

    nn.defvjp(lambda a, b: (nn(a, b), (a, b)), lambda r, g: (nt(g, r[1]), tn(r[0], g)))
    nt.defvjp(lambda a, b: (nt(a, b), (a, b)), lambda r, g: (nn(g, r[1]), tn(g, r[0])))
    tn.defvjp(lambda a, b: (tn(a, b), (a, b)), lambda r, g: (nt(r[1], g), nn(r[0], g)))
    return nn, nt, tn


_bnn, _bnt, _btn = _make_dots(None)
_hnn, _hnt, _htn = _make_dots(_HI)
_mnn, _mnt, _mtn = _make_dots(_MID)


def _neumann(a, order):
    n = a.shape[0]
    eye = (lax.broadcasted_iota(jnp.int32, (n, n), 0) == lax.broadcasted_iota(jnp.int32, (n, n), 1)).astype(F32)
    inv = eye - a
    p = a
    span = 2
    while span < order:
        p = _mnn(p, p)
        inv = inv + _mnn(inv, p)
        span *= 2
    return inv


def _inv_unit(a, order):
    n = a.shape[0]
    ii = lax.broadcasted_iota(jnp.int32, (n, n), 0)
    jj = lax.broadcasted_iota(jnp.int32, (n, n), 1)
    near = (ii & -INV_SUB) == (jj & -INV_SUB)
    d_inv = _neumann(jnp.where(near, a, 0.0), INV_SUB)
    outer = _neumann(_mnn(d_inv, jnp.where(near, 0.0, a)), order // INV_SUB)
    return _mnn(outer, d_inv)


def _inv_unit_t(t, g):
    return -_mnt(_mtn(t, g), t)


def _silu(x):
    return x * jax.nn.sigmoid(x)


def _softplus(x):
    return jnp.maximum(x, 0.0) + jnp.log1p(jnp.exp(-jnp.abs(x)))


def _params(sem=None):
    kw = {"vmem_limit_bytes": VMEM_LIMIT}
    if sem is not None:
        kw["dimension_semantics"] = sem
    return pltpu.CompilerParams(**kw)


def _tile(dim, pref):
    if dim <= pref:
        return dim
    t = (pref // LANES) * LANES
    while t > LANES and dim % t:
        t -= LANES
    assert dim % t == 0, (dim, pref)
    return t


def _full(shape):
    zeros = (0,) * len(shape)
    return pl.BlockSpec(shape, lambda *_: zeros)


def _sds(shape, dtype=F32):
    return jax.ShapeDtypeStruct(shape, dtype)


def _mm(a, b, mode, *, name, add=None, tm=1536, tn=1536, tk=1536, out_dtype=F32):
    if mode == "nn":
        (m, k), (k2, n) = a.shape, b.shape
    elif mode == "nt":
        (m, k), (n, k2) = a.shape, b.shape
    else:
        (k, m), (k2, n) = a.shape, b.shape
    assert k == k2, (a.shape, b.shape, mode)
    tm, tn, tk = _tile(m, tm), _tile(n, tn), _tile(k, tk)
    nk = k // tk
    dims = {"nn": _NN, "nt": _NT, "tn": _TN}[mode]

    def body(*refs):
        if add is None:
            a_ref, b_ref, o_ref, acc = refs
        else:
            a_ref, b_ref, add_ref, o_ref, acc = refs
        kk = pl.program_id(2)

        @pl.when(kk == 0)
        def _():
            acc[...] = jnp.zeros_like(acc)

        acc[...] += _dg(a_ref[...], b_ref[...], dims, None)

        @pl.when(kk == nk - 1)
        def _():
            o_ref[...] = (acc[...] if add is None else acc[...] + add_ref[...]).astype(out_dtype)

    a_spec = pl.BlockSpec((tk, tm), lambda i, j, kk: (kk, i)) if mode == "tn" else pl.BlockSpec((tm, tk), lambda i, j, kk: (i, kk))
    b_spec = pl.BlockSpec((tn, tk), lambda i, j, kk: (j, kk)) if mode == "nt" else pl.BlockSpec((tk, tn), lambda i, j, kk: (kk, j))
    o_spec = pl.BlockSpec((tm, tn), lambda i, j, kk: (i, j))
    ins, specs = [a, b], [a_spec, b_spec]
    if add is not None:
        ins.append(add)
        specs.append(o_spec)
    return pl.pallas_call(
        body, name=name, grid=(m // tm, n // tn, nk), in_specs=specs, out_specs=o_spec,
        out_shape=_sds((m, n), out_dtype), scratch_shapes=[pltpu.VMEM((tm, tn), F32)],
        compiler_params=_params(("parallel", "parallel", "arbitrary")),
    )(*ins)


def _cols(width, start):
    assert start % width == 0
    return pl.BlockSpec((ROW_TILE, width), lambda i: (i, start // width))


def _rows(width):
    return pl.BlockSpec((ROW_TILE, width), lambda i: (i, 0))


def _accumulate(ref, value, step):
    @pl.when(step == 0)
    def _():
        ref[...] = value

    @pl.when(step != 0)
    def _():
        ref[...] += value


def _ln_fn(x, r, g, b):
    u = DN_ALPHA * x + r
    mu = jnp.mean(u, axis=-1, keepdims=True)
    var = jnp.mean(jnp.square(u - mu), axis=-1, keepdims=True)
    return (u - mu) * lax.rsqrt(var + LN_EPS) * g + b


def _ln_fwd(x, r, g, b, name):
    t, d = x.shape

    def body(x_ref, r_ref, g_ref, b_ref, o_ref, ob_ref):
        y = _ln_fn(x_ref[...], r_ref[...], g_ref[...], b_ref[...])
        o_ref[...] = y
        ob_ref[...] = y.astype(_MXU)

    return pl.pallas_call(
        body, name=name, grid=(t // ROW_TILE,), in_specs=[_rows(d), _rows(d), _full((1, d)), _full((1, d))],
        out_specs=[_rows(d), _rows(d)], out_shape=[_sds((t, d)), _sds((t, d), _MXU)], compiler_params=_params(("parallel",)),
    )(x, r, g, b)


def _ln_bwd(x, r, g, b, dy, name):
    t, d = x.shape

    def body(x_ref, r_ref, g_ref, b_ref, dy_ref, dx_ref, dr_ref, dg_ref, db_ref):
        _, vjp = jax.vjp(_ln_fn, x_ref[...], r_ref[...], g_ref[...], b_ref[...])
        dx, dr, dg, db = vjp(dy_ref[...])
        dx_ref[...] = dx
        dr_ref[...] = dr.astype(_MXU)
        _accumulate(dg_ref, dg, pl.program_id(0))
        _accumulate(db_ref, db, pl.program_id(0))

    return pl.pallas_call(
        body, name=name, grid=(t // ROW_TILE,),
        in_specs=[_rows(d), _rows(d), _full((1, d)), _full((1, d)), _rows(d)],
        out_specs=[_rows(d), _rows(d), _full((1, d)), _full((1, d))],
        out_shape=[_sds((t, d)), _sds((t, d), _MXU), _sds((1, d)), _sds((1, d))],
        compiler_params=_params(("arbitrary",)),
    )(x, r, g, b, dy)


def _merge_fn(ga, gb, ya, yb):
    return jax.nn.sigmoid(ga) * ya + jax.nn.sigmoid(gb) * yb


def _merge_fwd(proj, ya, yb):
    t, d = ya.shape

    def body(ga_ref, gb_ref, ya_ref, yb_ref, o_ref):
        o_ref[...] = _merge_fn(ga_ref[...], gb_ref[...], ya_ref[...], yb_ref[...]).astype(_MXU)

    return pl.pallas_call(
        body, name="merge_fwd", grid=(t // ROW_TILE,), in_specs=[_cols(d, C_GA), _cols(d, C_GB), _rows(d), _rows(d)],
        out_specs=_rows(d), out_shape=_sds((t, d), _MXU), compiler_params=_params(("parallel",)),
    )(proj, proj, ya, yb)


def _merge_bwd(proj, ya, yb, dm):
    t, d = ya.shape

    def body(ga_ref, gb_ref, ya_ref, yb_ref, dm_ref, dga_ref, dgb_ref, dya_ref, dyb_ref):
        _, vjp = jax.vjp(_merge_fn, ga_ref[...], gb_ref[...], ya_ref[...], yb_ref[...])
        dga_ref[...], dgb_ref[...], dya_ref[...], dyb_ref[...] = [g.astype(_MXU) for g in vjp(dm_ref[...])]

    return pl.pallas_call(
        body, name="merge_bwd", grid=(t // ROW_TILE,),
        in_specs=[_cols(d, C_GA), _cols(d, C_GB), _rows(d), _rows(d), _rows(d)],
        out_specs=[_rows(d)] * 4, out_shape=[_sds((t, d), _MXU)] * 4, compiler_params=_params(("parallel",)),
    )(proj, proj, ya, yb, dm)


def _swiglu_fn(gate, up):
    return _silu(gate) * up


def _swiglu_fwd(gu):
    t = gu.shape[0]
    f = FFN_HIDDEN
    rows = 128

    def body(gu_ref, o_ref):
        o_ref[...] = _swiglu_fn(gu_ref[:, :f], gu_ref[:, f:]).astype(_MXU)

    return pl.pallas_call(
        body, name="swiglu_fwd", grid=(t // rows,), in_specs=[pl.BlockSpec((rows, 2 * f), lambda i: (i, 0))],
        out_specs=pl.BlockSpec((rows, f), lambda i: (i, 0)), out_shape=_sds((t, f), _MXU), compiler_params=_params(("parallel",)),
    )(gu)


def _swiglu_bwd(gu, dh):
    t = gu.shape[0]
    f = FFN_HIDDEN
    rows = 128

    def body(gu_ref, dh_ref, o_ref):
        _, vjp = jax.vjp(_swiglu_fn, gu_ref[:, :f], gu_ref[:, f:])
        o_ref[:, :f], o_ref[:, f:] = [g.astype(_MXU) for g in vjp(dh_ref[...])]

    return pl.pallas_call(
        body, name="swiglu_bwd", grid=(t // rows,),
        in_specs=[pl.BlockSpec((rows, 2 * f), lambda i: (i, 0)), pl.BlockSpec((rows, f), lambda i: (i, 0))],
        out_specs=pl.BlockSpec((rows, 2 * f), lambda i: (i, 0)), out_shape=_sds((t, 2 * f), _MXU),
        compiler_params=_params(("parallel",)),
    )(gu, dh)


def _loss_head(y, target):
    t, d = y.shape

    def body(y_ref, t_ref, s_ref, dy_ref):
        err = y_ref[...] - t_ref[...]
        dy_ref[...] = err / d
        _accumulate(s_ref, jnp.broadcast_to(jnp.sum(jnp.square(err)), (1, LANES)), pl.program_id(0))

    return pl.pallas_call(
        body, name="loss_head", grid=(t // ROW_TILE,), in_specs=[_rows(d), _rows(d)],
        out_specs=[_full((1, LANES)), _rows(d)], out_shape=[_sds((1, LANES)), _sds((t, d))],
        compiler_params=_params(("arbitrary",)),
    )(y, target)


def _shift_rows(x, s):
    if s == 0:
        return x
    return pltpu.roll(x, (-s) % x.shape[0], 0)


def _conv(x, w):
    half = DN_CONV // 2
    acc = None
    for k in range(DN_CONV):
        term = _shift_rows(x, k - half) * w[k:k + 1, :]
        acc = term if acc is None else acc + term
    return acc


def _act_norm(c, do_norm, scale):
    a = _silu(c)
    if not do_norm:
        return a
    return a * lax.rsqrt(jnp.sum(a * a, axis=-1, keepdims=True) + RMS_EPS) * scale


PREP_ROWS = 512
HALO = 8
_KINDS = ((True, DN_HEAD_DIM ** -0.5), (True, 1.0), (False, 1.0))


def _halo_rows(read, i, pr, t):
    lo, hi = i * pr - HALO, (i + 1) * pr + HALO
    parts = []
    if lo < 0:
        parts.append(jnp.zeros((HALO, LANES), F32))
    parts.append(read(max(lo, 0), min(hi, t)))
    if hi > t:
        parts.append(jnp.zeros((HALO, LANES), F32))
    return jnp.concatenate(parts, axis=0) if len(parts) > 1 else parts[0]


def _prep_fwd(proj, conv_w, kind):
    t = proj.shape[0]
    pr = min(PREP_ROWS, t)
    do_norm, scale = _KINDS[kind]
    blk = pl.BlockSpec((t, LANES), lambda j: (0, kind * DN_HEADS + j))

    def body(x_ref, w_ref, o_ref):
        w = w_ref[...]
        for i in range(t // pr):
            xx = _halo_rows(lambda lo, hi: x_ref[lo:hi, :], i, pr, t)
            c = _conv(xx, w)[HALO:HALO + pr, :]
            o_ref[i * pr:(i + 1) * pr, :] = _act_norm(c, do_norm, scale)

    return pl.pallas_call(
        body, name=f"prep_fwd_{kind}", grid=(DN_HEADS,),
        in_specs=[blk, pl.BlockSpec((8, LANES), lambda j: (0, kind * DN_HEADS + j))],
        out_specs=pl.BlockSpec((t, LANES), lambda j: (0, j)), out_shape=_sds((t, DN_WIDTH)),
        compiler_params=_params(("parallel",)),
    )(proj, conv_w)


def _prep_bwd(proj, conv_w, d2, kind):
    t = proj.shape[0]
    pr = min(PREP_ROWS, t)
    do_norm, scale = _KINDS[kind]
    half = DN_CONV // 2
    blk = pl.BlockSpec((t, LANES), lambda j: (0, kind * DN_HEADS + j))
    oblk = pl.BlockSpec((t, LANES), lambda j: (0, j))

    def body(x_ref, w_ref, d_ref, dx_ref, dw_ref):
        w = w_ref[...]
        own = slice(HALO, HALO + pr)
        dw = jnp.zeros((8, LANES), F32)
        for i in range(t // pr):
            xx = _halo_rows(lambda lo, hi: x_ref[lo:hi, :], i, pr, t)
            dn = _halo_rows(lambda lo, hi: d_ref[0, lo:hi, :] + d_ref[1, lo:hi, :], i, pr, t)
            _, vjp = jax.vjp(lambda c: _act_norm(c, do_norm, scale), _conv(xx, w))
            (dc,) = vjp(dn)
            dx = None
            rows = []
            for k in range(DN_CONV):
                term = _shift_rows(dc, half - k) * w[k:k + 1, :]
                dx = term if dx is None else dx + term
                rows.append(jnp.sum(dc[own, :] * _shift_rows(xx, k - half)[own, :], axis=0, keepdims=True))
            dx_ref[i * pr:(i + 1) * pr, :] = dx[own, :].astype(_MXU)
            dw = dw + jnp.concatenate(rows + [jnp.zeros((8 - DN_CONV, LANES), F32)], axis=0)
        dw_ref[...] = dw

    return pl.pallas_call(
        body, name=f"prep_bwd_{kind}", grid=(DN_HEADS,),
        in_specs=[blk, pl.BlockSpec((8, LANES), lambda j: (0, kind * DN_HEADS + j)), pl.BlockSpec((2, t, LANES), lambda j: (0, 0, j))],
        out_specs=[oblk, pl.BlockSpec((8, LANES), lambda j: (0, j))], out_shape=[_sds((t, DN_WIDTH), _MXU), _sds((8, DN_WIDTH))],
        compiler_params=_params(("parallel",)),
    )(proj, conv_w, d2)


def _gb_fn(ba, alog_row, dtb_row):
    c = DN_CHUNK
    lane = lax.broadcasted_iota(jnp.int32, (c, LANES), 1)
    ii = lax.broadcasted_iota(jnp.int32, (c, c), 0)
    jj = lax.broadcasted_iota(jnp.int32, (c, c), 1)
    beta = jax.nn.sigmoid(ba)
    g = -jnp.exp(alog_row) * _softplus(ba + dtb_row)
    g = jnp.where((lane >= N_HD) & (lane < 2 * N_HD), g, 0.0)
    gc_fwd = _hnn((ii >= jj).astype(F32), g)
    gc_rev = _hnn((ii <= jj).astype(F32), g)
    gc = jnp.where(lane < N_HD + DN_HEADS, gc_fwd, gc_rev)
    return jnp.where(lane < N_HD, beta, jnp.where(lane < 2 * N_HD, gc, 0.0))


def _per_head_spec():
    return pl.BlockSpec((N_HD, ROW_TILE, LANES), lambda i: (0, i, 0))


def _gb_fwd(ba, alog_row, dtb_row):
    t = ba.shape[0]
    n = ROW_TILE // DN_CHUNK

    def body(ba_ref, a_ref, d_ref, o_ref, beta_ref, gc_ref):
        for c in range(n):
            rows = slice(c * DN_CHUNK, (c + 1) * DN_CHUNK)
            out = _gb_fn(ba_ref[rows, :], a_ref[...], d_ref[...])
            o_ref[rows, :] = out
            for j in range(N_HD):
                beta_ref[j, rows, :] = jnp.broadcast_to(out[:, j:j + 1], (DN_CHUNK, LANES))
                gc_ref[j, rows, :] = jnp.broadcast_to(out[:, N_HD + j:N_HD + j + 1], (DN_CHUNK, LANES))

    per_head = _sds((N_HD, t, LANES))
    return pl.pallas_call(
        body, name="gates_fwd", grid=(t // ROW_TILE,), in_specs=[_rows(LANES), _full((1, LANES)), _full((1, LANES))],
        out_specs=[_rows(LANES), _per_head_spec(), _per_head_spec()], out_shape=[_sds((t, LANES)), per_head, per_head],
        compiler_params=_params(("parallel",)),
    )(ba, alog_row, dtb_row)


def _gb_bwd(ba, alog_row, dtb_row, dbeta, dgc, d_rows):
    t = ba.shape[0]
    n = ROW_TILE // DN_CHUNK

    def body(ba_ref, a_ref, d_ref, dbeta_ref, dgc_ref, dr_ref, dba_ref, dal_ref, ddt_ref):
        dal = jnp.zeros((1, LANES), F32)
        ddt = jnp.zeros((1, LANES), F32)
        lane = lax.broadcasted_iota(jnp.int32, (DN_CHUNK, LANES), 1)
        for c in range(n):
            rows = slice(c * DN_CHUNK, (c + 1) * DN_CHUNK)
            cot = dr_ref[rows, :]
            for j in range(N_HD):
                cot = jnp.where(lane == j, dbeta_ref[j, rows, :], cot)
                cot = jnp.where(lane == N_HD + j, dgc_ref[j, rows, :] + cot, cot)
            _, vjp = jax.vjp(_gb_fn, ba_ref[rows, :], a_ref[...], d_ref[...])
            dba, da, dd = vjp(cot)
            dba_ref[rows, :] = dba.astype(_MXU)
            dal = dal + da
            ddt = ddt + dd
        _accumulate(dal_ref, dal, pl.program_id(0))
        _accumulate(ddt_ref, ddt, pl.program_id(0))

    return pl.pallas_call(
        body, name="gates_bwd", grid=(t // ROW_TILE,),
        in_specs=[_rows(LANES), _full((1, LANES)), _full((1, LANES)), _per_head_spec(), _per_head_spec(), _rows(LANES)],
        out_specs=[_rows(LANES), _full((1, LANES)), _full((1, LANES))],
        out_shape=[_sds((t, LANES), _MXU), _sds((1, LANES)), _sds((1, LANES))], compiler_params=_params(("arbitrary",)),
    )(ba, alog_row, dtb_row, dbeta, dgc, d_rows)


def _dn_decay(gcc, gcr, sgn):
    c = DN_CHUNK
    ii = lax.broadcasted_iota(jnp.int32, (c, c), 0)
    jj = lax.broadcasted_iota(jnp.int32, (c, c), 1)
    d = (ii - jj) * sgn
    lower = d >= 0
    return jnp.where(lower, jnp.exp(jnp.where(lower, gcc - gcr, 0.0)), 0.0), d > 0


def _dn_a(k, beta, gcc, gcr, sgn):
    decay, strict = _dn_decay(gcc, gcr, sgn)
    return jnp.where(strict, beta * _bnt(k, k) * decay, 0.0)


def _dn_group(q, k, v, beta, gcc, gcr, sgn):
    n = DN_GROUP
    ii = lax.broadcasted_iota(jnp.int32, (n, n), 0)
    jj = lax.broadcasted_iota(jnp.int32, (n, n), 1)
    same = (ii & -DN_CHUNK) == (jj & -DN_CHUNK)
    d = (ii - jj) * sgn
    lower = same & (d >= 0)
    decay = jnp.where(lower, jnp.exp(jnp.where(lower, gcc - gcr, 0.0)), 0.0)
    a = jnp.where(same & (d > 0), beta * _bnt(k, k) * decay, 0.0)
    t_inv = _inv_unit(a, DN_CHUNK)
    u = _bnn(t_inv, v * beta)
    w = _bnn(t_inv, k * (beta * jnp.exp(gcc)))
    return u, w, _bnt(q, k) * decay, t_inv


def _dn_local(t_inv, q, k, v, beta, gcc, gcr, sgn):
    c = DN_CHUNK
    decay, _ = _dn_decay(gcc, gcr, sgn)
    eg = jnp.exp(gcc)
    u = _bnn(t_inv, v * beta)
    w = _bnn(t_inv, k * (beta * eg))
    qk = _bnt(q, k) * decay
    qd = q * eg
    last = jnp.where(sgn > 0, c - 1, 0)
    onehot = (lax.broadcasted_iota(jnp.int32, (c, 1), 0) == last).astype(F32)
    gl = jnp.sum(gcc * onehot, axis=0, keepdims=True)
    kd = k * jnp.exp(gl - gcc)
    egl = jnp.broadcast_to(jnp.exp(gl), (1, LANES))
    return u, w, qk, qd, kd, egl


def _hd_sign(hd):
    return jnp.where(hd < DN_HEADS, 1, -1).astype(jnp.int32)


def _head_of(hd):
    return jnp.where(hd < DN_HEADS, hd, hd - DN_HEADS)


def _dir_of(hd):
    return jnp.where(hd < DN_HEADS, 0, 1)


def _dn_specs(t):
    nl = LOCAL_ROWS // DN_CHUNK
    wide = pl.BlockSpec((1, LOCAL_ROWS, LANES), lambda hd, i: (hd, i, 0))
    half = pl.BlockSpec((1, LOCAL_ROWS, DN_CHUNK), lambda hd, i: (hd, i, 0))
    col = wide
    row = pl.BlockSpec((1, nl, 1, DN_CHUNK), lambda hd, i: (hd, i, 0, 0))
    egl = pl.BlockSpec((1, nl, 1, LANES), lambda hd, i: (hd, i, 0, 0))
    return wide, half, col, row, egl


def _qkv_specs():
    return [pl.BlockSpec((LOCAL_ROWS, LANES), lambda hd, i: (i, _head_of(hd)))] * 3


def _maybe_carrying(carry, body, name, grid, operands, in_specs, out_specs, out_shape):
    extra_scratch = []
    if carry is not None:
        kind, arrays = carry
        body, more_in, more_out, more_shape, extra_scratch = _carried(kind, arrays, body, len(operands), len(out_shape), grid)
        operands, in_specs = operands + list(arrays), in_specs + more_in
        out_specs, out_shape = out_specs + more_out, out_shape + more_shape
    sem = ("arbitrary",) * len(grid) if carry is not None else ("parallel",) * len(grid)
    return pl.pallas_call(
        body, name=name, grid=grid, in_specs=in_specs, out_specs=out_specs, out_shape=out_shape,
        scratch_shapes=extra_scratch, compiler_params=_params(sem),
    )(*operands)


def _dn_local_fwd(q, k, v, beta_c, gc_c, gc_r, carry=None):
    t = q.shape[0]
    nc = t // DN_CHUNK
    nl = LOCAL_ROWS // DN_CHUNK
    wide, half, col, row, egl = _dn_specs(t)

    ng = LOCAL_ROWS // DN_GROUP
    per = DN_GROUP // DN_CHUNK
    grow = pl.BlockSpec((1, ng, 1, DN_GROUP), lambda hd, i: (hd, i, 0, 0))

    def body(q_ref, k_ref, v_ref, b_ref, gc_ref, gg_ref, u_ref, w_ref, qk_ref, qd_ref, kd_ref, egl_ref, t_ref):
        sgn = _hd_sign(pl.program_id(0))
        last = jnp.where(sgn > 0, DN_CHUNK - 1, 0)
        onehot = (lax.broadcasted_iota(jnp.int32, (DN_CHUNK, 1), 0) == last).astype(F32)
        groups = lambda a: a.reshape((ng, DN_GROUP) + a.shape[1:])
        q_all, k_all, gcc_all = q_ref[...], k_ref[...], gc_ref[0][:, :1]
        u, w, qk, t_inv = jax.vmap(functools.partial(_dn_group, sgn=sgn))(
            groups(q_all), groups(k_all), groups(v_ref[...]), groups(b_ref[0][:, :1]), groups(gcc_all), gg_ref[0])
        u_ref[0] = u.reshape(LOCAL_ROWS, LANES)
        w_ref[0] = w.reshape(LOCAL_ROWS, LANES)
        qd_ref[0] = q_all * jnp.exp(gcc_all)
        for gi in range(ng):
            for c in range(per):
                blk = slice(c * DN_CHUNK, (c + 1) * DN_CHUNK)
                rows = slice(gi * DN_GROUP + c * DN_CHUNK, gi * DN_GROUP + (c + 1) * DN_CHUNK)
                qk_ref[0, rows, :] = qk[gi, blk, blk]
                t_ref[0, rows, :] = t_inv[gi, blk, blk]
                gl = jnp.sum(gcc_all[rows, :] * onehot, axis=0, keepdims=True)
                kd_ref[0, rows, :] = k_all[rows, :] * jnp.exp(gl - gcc_all[rows, :])
                egl_ref[0, gi * per + c] = jnp.broadcast_to(jnp.exp(gl), (1, LANES))

    big = _sds((N_HD, t, LANES))
    small = _sds((N_HD, t, DN_CHUNK))
    operands = [q, k, v, beta_c, gc_c, gc_r.reshape(N_HD, t // DN_GROUP, 1, DN_GROUP)]
    return _maybe_carrying(
        carry, body, "dn_local_fwd", (N_HD, t // LOCAL_ROWS), operands, _qkv_specs() + [col, col, grow],
        [wide, wide, half, wide, wide, egl, half], [big, big, small, big, big, _sds((N_HD, nc, 1, LANES)), small])


def _dn_local_bwd(q, k, v, beta_c, gc_c, gc_r, t_inv, du, dw, dqk, dqd, dkd, degl, carry=None):
    t = q.shape[0]
    nc = t // DN_CHUNK
    nl = LOCAL_ROWS // DN_CHUNK
    wide, half, col, row, egl = _dn_specs(t)
    dspec = pl.BlockSpec((1, LOCAL_ROWS, LANES), lambda hd, i: (_dir_of(hd), i, _head_of(hd)))

    def body(q_ref, k_ref, v_ref, b_ref, gc_ref, gr_ref, t_ref, du_ref, dw_ref, dqk_ref, dqd_ref, dkd_ref, degl_ref,
             dq_ref, dk_ref, dv_ref, db_ref, dgc_ref, dgr_ref):
        sgn = _hd_sign(pl.program_id(0))

        def chunk_bwd(tinv, q, k, v, beta, gcc, gcr, du, dw, dqk, dqd, dkd, degl):
            _, vjp = jax.vjp(functools.partial(_dn_local, sgn=sgn), tinv, q, k, v, beta, gcc, gcr)
            dt, dq, dk, dv, db, dgc, dgr = vjp((du, dw, dqk, dqd, dkd, degl))
            _, vjp_a = jax.vjp(functools.partial(_dn_a, sgn=sgn), k, beta, gcc, gcr)
            dk2, db2, dgc2, dgr2 = vjp_a(_inv_unit_t(tinv, dt))
            return dq, dk + dk2, dv, db + db2, dgc + dgc2, dgr + dgr2

        chunks = lambda a: a.reshape((nl, DN_CHUNK) + a.shape[1:])
        dq, dk, dv, db, dgc, dgr = jax.vmap(chunk_bwd)(
            chunks(t_ref[0]), chunks(q_ref[...]), chunks(k_ref[...]), chunks(v_ref[...]), chunks(b_ref[0][:, :1]),
            chunks(gc_ref[0][:, :1]),
            gr_ref[0], chunks(du_ref[0]), chunks(dw_ref[0]), chunks(dqk_ref[0]), chunks(dqd_ref[0]), chunks(dkd_ref[0]),
            degl_ref[0])
        dq_ref[0] = dq.reshape(LOCAL_ROWS, LANES)
        dk_ref[0] = dk.reshape(LOCAL_ROWS, LANES)
        dv_ref[0] = dv.reshape(LOCAL_ROWS, LANES)
        db_ref[0] = jnp.broadcast_to(db.reshape(LOCAL_ROWS, 1), (LOCAL_ROWS, LANES))
        dgc_ref[0] = jnp.broadcast_to(dgc.reshape(LOCAL_ROWS, 1), (LOCAL_ROWS, LANES))
        dgr_ref[0] = dgr

    per_dir = _sds((2, t, DN_WIDTH))
    return _maybe_carrying(
        carry, body, "dn_local_bwd", (N_HD, t // LOCAL_ROWS), [q, k, v, beta_c, gc_c, gc_r, t_inv, du, dw, dqk, dqd, dkd, degl],
        _qkv_specs() + [col, col, row, half, wide, wide, half, wide, wide, egl], [dspec, dspec, dspec, col, col, row],
        [per_dir, per_dir, per_dir, _sds((N_HD, t, LANES)), _sds((N_HD, t, LANES)), _sds((N_HD, nc, 1, DN_CHUNK))])


REC_HEADS = 8
REC_GROUPS = N_HD // REC_HEADS
REC_FWD_GROUPS = DN_HEADS // REC_HEADS


def _rec_specs(time_block):
    nr = REC_ROWS // DN_CHUNK
    wide = pl.BlockSpec((REC_HEADS, REC_ROWS, LANES), lambda g, b: (g, time_block(g, b), 0))
    half = pl.BlockSpec((REC_HEADS, REC_ROWS, DN_CHUNK), lambda g, b: (g, time_block(g, b), 0))
    egl = pl.BlockSpec((REC_HEADS, nr, 1, LANES), lambda g, b: (g, time_block(g, b), 0, 0))
    state = pl.BlockSpec((REC_HEADS, nr, DN_HEAD_DIM, DN_HEAD_DIM), lambda g, b: (g, time_block(g, b), 0, 0))
    return wide, half, egl, state


def _rec_head_cols(g):
    return jnp.where(g < REC_FWD_GROUPS, g, g - REC_FWD_GROUPS)


def _dn_rec_fwd(u, w, qk, qd, kd, egl):
    t = u.shape[1]
    nb = t // REC_ROWS
    nr = REC_ROWS // DN_CHUNK
    nc = t // DN_CHUNK

    def time_block(g, b):
        return jnp.where(g < REC_FWD_GROUPS, b, nb - 1 - b)

    wide, half, egl_spec, state = _rec_specs(time_block)
    o_spec = pl.BlockSpec((1, REC_ROWS, REC_HEADS * LANES),
                          lambda g, b: (jnp.where(g < REC_FWD_GROUPS, 0, 1), time_block(g, b), _rec_head_cols(g)))

    def body(u_ref, w_ref, qk_ref, qd_ref, kd_ref, egl_ref, o_ref, vn_ref, s_ref, s_scr):
        fwd = pl.program_id(0) < REC_FWD_GROUPS

        @pl.when(pl.program_id(1) == 0)
        def _():
            s_scr[...] = jnp.zeros_like(s_scr)

        def run(order):
            heads = range(REC_HEADS)
            s = [s_scr[j] for j in heads]
            for ce in order:
                rows = slice(ce * DN_CHUNK, (ce + 1) * DN_CHUNK)
                vn = [u_ref[j, rows, :] - _bnn(w_ref[j, rows, :], s[j]) for j in heads]
                o = [_bnn(qd_ref[j, rows, :], s[j]) + _bnn(qk_ref[j, rows, :], vn[j]) for j in heads]
                nxt = [s[j] * egl_ref[j, ce] + _btn(kd_ref[j, rows, :], vn[j]) for j in heads]
                for j in heads:
                    s_ref[j, ce] = s[j]
                    vn_ref[j, rows, :] = vn[j]
                    o_ref[0, rows, j * LANES:(j + 1) * LANES] = o[j]
                s = nxt
            for j in heads:
                s_scr[j] = s[j]

        pl.when(fwd)(lambda: run(range(nr)))
        pl.when(jnp.logical_not(fwd))(lambda: run(range(nr - 1, -1, -1)))

    return pl.pallas_call(
        body, name="dn_rec_fwd", grid=(REC_GROUPS, nb), in_specs=[wide, wide, half, wide, wide, egl_spec],
        out_specs=[o_spec, wide, state],
        out_shape=[_sds((2, t, DN_WIDTH)), _sds((N_HD, t, LANES)), _sds((N_HD, nc, DN_HEAD_DIM, DN_HEAD_DIM))],
        scratch_shapes=[pltpu.VMEM((REC_HEADS, DN_HEAD_DIM, DN_HEAD_DIM), F32)],
        compiler_params=_params(("parallel", "arbitrary")),
    )(u, w, qk, qd, kd, egl)


def _dn_rec_bwd(w, qk, qd, kd, egl, vn, states, do):
    t = w.shape[1]
    nb = t // REC_ROWS
    nr = REC_ROWS // DN_CHUNK
    nc = t // DN_CHUNK

    def time_block(g, b):
        return jnp.where(g < REC_FWD_GROUPS, nb - 1 - b, b)

    wide, half, egl_spec, state = _rec_specs(time_block)
    do_spec = pl.BlockSpec((REC_ROWS, REC_HEADS * LANES), lambda g, b: (time_block(g, b), _rec_head_cols(g)))

    def body(w_ref, qk_ref, qd_ref, kd_ref, egl_ref, vn_ref, s_ref, do_ref,
             du_ref, dw_ref, dqk_ref, dqd_ref, dkd_ref, degl_ref, ds_scr):
        fwd = pl.program_id(0) < REC_FWD_GROUPS

        @pl.when(pl.program_id(1) == 0)
        def _():
            ds_scr[...] = jnp.zeros_like(ds_scr)

        def run(order):
            heads = range(REC_HEADS)
            ds = [ds_scr[j] for j in heads]
            for ce in order:
                rows = slice(ce * DN_CHUNK, (ce + 1) * DN_CHUNK)
                s = [s_ref[j, ce] for j in heads]
                do_c = [do_ref[rows, j * LANES:(j + 1) * LANES] for j in heads]
                vn_c = [vn_ref[j, rows, :] for j in heads]
                dvn = [_btn(qk_ref[j, rows, :], do_c[j]) + _bnn(kd_ref[j, rows, :], ds[j]) for j in heads]
                nxt = [ds[j] * egl_ref[j, ce] + _btn(qd_ref[j, rows, :], do_c[j]) - _btn(w_ref[j, rows, :], dvn[j])
                       for j in heads]
                for j in heads:
                    du_ref[j, rows, :] = dvn[j]
                    dw_ref[j, rows, :] = -_bnt(dvn[j], s[j])
                for j in heads:
                    dqk_ref[j, rows, :] = _bnt(do_c[j], vn_c[j])
                    dqd_ref[j, rows, :] = _bnt(do_c[j], s[j])
                for j in heads:
                    dkd_ref[j, rows, :] = _bnt(vn_c[j], ds[j])
                    degl_ref[j, ce] = jnp.sum(s[j] * ds[j], axis=0, keepdims=True)
                ds = nxt
            for j in heads:
                ds_scr[j] = ds[j]

        pl.when(fwd)(lambda: run(range(nr - 1, -1, -1)))
        pl.when(jnp.logical_not(fwd))(lambda: run(range(nr)))

    big = _sds((N_HD, t, LANES))
    return pl.pallas_call(
        body, name="dn_rec_bwd", grid=(REC_GROUPS, nb), in_specs=[wide, half, wide, wide, egl_spec, wide, state, do_spec],
        out_specs=[wide, wide, half, wide, wide, egl_spec],
        out_shape=[big, big, _sds((N_HD, t, DN_CHUNK)), big, big, _sds((N_HD, nc, 1, LANES))],
        scratch_shapes=[pltpu.VMEM((REC_HEADS, DN_HEAD_DIM, DN_HEAD_DIM), F32)],
        compiler_params=_params(("parallel", "arbitrary")),
    )(w, qk, qd, kd, egl, vn, states, do)


def _post_fn(of, ob, z, gain):
    o = of + ob
    return o * lax.rsqrt(jnp.mean(o * o, axis=-1, keepdims=True) + RMS_EPS) * gain * _silu(z)


def _post_specs():
    o_spec = [pl.BlockSpec((1, ROW_TILE, DN_WIDTH), functools.partial(lambda i, d: (d, i, 0), d=d)) for d in (0, 1)]
    return o_spec, _cols(DN_WIDTH, C_Z), _rows(DN_WIDTH), _full((1, LANES))


def _post_fwd(o2, proj, gain):
    t = proj.shape[0]
    o_spec, z_spec, wide, gain_spec = _post_specs()

    def body(of_ref, ob_ref, z_ref, g_ref, out_ref):
        for h in range(DN_HEADS):
            cols = slice(h * LANES, (h + 1) * LANES)
            out_ref[:, cols] = _post_fn(of_ref[0, :, cols], ob_ref[0, :, cols], z_ref[:, cols], g_ref[...]).astype(_MXU)

    return pl.pallas_call(
        body, name="post_fwd", grid=(t // ROW_TILE,), in_specs=o_spec + [z_spec, gain_spec], out_specs=wide,
        out_shape=_sds((t, DN_WIDTH), _MXU), compiler_params=_params(("parallel",)),
    )(o2, o2, proj, gain)


def _post_bwd(o2, proj, gain, dout):
    t = proj.shape[0]
    o_spec, z_spec, wide, gain_spec = _post_specs()

    def body(of_ref, ob_ref, z_ref, g_ref, d_ref, do_ref, dz_ref, dg_ref):
        dg_sum = jnp.zeros((1, LANES), F32)
        for h in range(DN_HEADS):
            cols = slice(h * LANES, (h + 1) * LANES)
            _, vjp = jax.vjp(_post_fn, of_ref[0, :, cols], ob_ref[0, :, cols], z_ref[:, cols], g_ref[...])
            do, _, dz, dg = vjp(d_ref[:, cols])
            do_ref[:, cols] = do
            dz_ref[:, cols] = dz.astype(_MXU)
            dg_sum = dg_sum + dg
        _accumulate(dg_ref, dg_sum, pl.program_id(0))

    return pl.pallas_call(
        body, name="post_bwd", grid=(t // ROW_TILE,), in_specs=o_spec + [z_spec, gain_spec, wide],
        out_specs=[wide, wide, gain_spec], out_shape=[_sds((t, DN_WIDTH)), _sds((t, DN_WIDTH), _MXU), _sds((1, LANES))],
        compiler_params=_params(("arbitrary",)),
    )(o2, o2, proj, gain, dout)


def _rope(x, cos, sin):
    lane = lax.broadcasted_iota(jnp.int32, x.shape, 1)
    first = (lane & (SW_HEAD_DIM - 1)) < SW_HEAD_DIM // 2
    rot = jnp.where(first, -pltpu.roll(x, LANES - SW_HEAD_DIM // 2, 1), pltpu.roll(x, SW_HEAD_DIM // 2, 1))
    return x * cos + rot * sin


def _rope_apply(q, k, q_cols, k_cols, cos, sin, name, dtype):
    t = cos.shape[0]
    qw, kw = SW_HEADS * SW_HEAD_DIM, SW_KV_WIDTH

    def body(q_ref, k_ref, c_ref, s_ref, qo_ref, ko_ref):
        c, s = c_ref[...], s_ref[...]
        for j in range(qw // LANES):
            cols = slice(j * LANES, (j + 1) * LANES)
            qo_ref[:, cols] = _rope(q_ref[:, cols], c, s).astype(dtype)
        for j in range(kw // LANES):
            cols = slice(j * LANES, (j + 1) * LANES)
            ko_ref[:, cols] = _rope(k_ref[:, cols], c, s).astype(dtype)

    return pl.pallas_call(
        body, name=name, grid=(t // ROW_TILE,), in_specs=[_cols(qw, q_cols), _cols(kw, k_cols), _rows(LANES), _rows(LANES)],
        out_specs=[_rows(qw), _rows(kw)], out_shape=[_sds((t, qw), dtype), _sds((t, kw), dtype)],
        compiler_params=_params(("parallel",)),
    )(q, k, cos, sin)


def _attn_core(qs, kb, vb, sink, mask):
    s = _bnt(qs, kb) * (SW_HEAD_DIM ** -0.5)
    s = jnp.where(mask, s, -1e30)
    m = lax.stop_gradient(jnp.maximum(jnp.max(s, axis=1, keepdims=True), sink))
    e = jnp.exp(s - m)
    den = jnp.sum(e, axis=1, keepdims=True) + jnp.exp(sink - m)
    return _bnn(e / den, vb)


def _band_mask(n, nb):
    rows = SW_GROUP * SW_BLOCK
    i = lax.broadcasted_iota(jnp.int32, (rows, 3 * SW_BLOCK), 0) & (SW_BLOCK - 1)
    j = lax.broadcasted_iota(jnp.int32, (rows, 3 * SW_BLOCK), 1)
    near = (j - i >= 0) & (j - i <= 2 * SW_BLOCK)
    lo = jnp.where(n == 0, SW_BLOCK, 0)
    hi = jnp.where(n == nb - 1, 2 * SW_BLOCK, 3 * SW_BLOCK)
    return near & (j >= lo) & (j < hi)


def _band_specs(nb, v_cols):
    def spec(width, base, shift):
        return pl.BlockSpec((SW_BLOCK, width), lambda n: (jnp.clip(n + shift, 0, nb - 1), base // width))
    k_specs = [spec(SW_KV_WIDTH, 0, s) for s in (-1, 0, 1)]
    v_specs = [spec(SW_KV_WIDTH, v_cols, s) for s in (-1, 0, 1)]
    return k_specs, v_specs


def _head_cols(kv, g):
    h = kv * SW_GROUP + g
    return slice(h * SW_HEAD_DIM, (h + 1) * SW_HEAD_DIM)


def _kv_batches(q_ref, kb, vb, s_ref):
    kvs = range(SW_KV_HEADS)
    cols = lambda kv: slice(kv * SW_HEAD_DIM, (kv + 1) * SW_HEAD_DIM)
    qs = jnp.stack([jnp.concatenate([q_ref[:, _head_cols(kv, g)] for g in range(SW_GROUP)], axis=0) for kv in kvs])
    sinks = jnp.stack([jnp.concatenate([jnp.broadcast_to(s_ref[kv * SW_GROUP + g], (SW_BLOCK, 1)) for g in range(SW_GROUP)],
                                       axis=0) for kv in kvs])
    return qs, jnp.stack([kb[:, cols(kv)] for kv in kvs]), jnp.stack([vb[:, cols(kv)] for kv in kvs]), sinks


def _attn_fwd(qr, kr, proj, sinks):
    t = qr.shape[0]
    nb = t // SW_BLOCK
    qw = SW_HEADS * SW_HEAD_DIM
    k_specs, v_specs = _band_specs(nb, C_VSW)
    q_spec = pl.BlockSpec((SW_BLOCK, qw), lambda n: (n, 0))

    def body(q_ref, k0, k1, k2, v0, v1, v2, s_ref, o_ref):
        mask = _band_mask(pl.program_id(0), nb)
        kb = jnp.concatenate([k0[...], k1[...], k2[...]], axis=0)
        vb = jnp.concatenate([v0[...], v1[...], v2[...]], axis=0)
        qs, kbs, vbs, sinks_ = _kv_batches(q_ref, kb, vb, s_ref)
        o = jax.vmap(functools.partial(_attn_core, mask=mask))(qs, kbs, vbs, sinks_)
        for kv in range(SW_KV_HEADS):
            for g in range(SW_GROUP):
                o_ref[:, _head_cols(kv, g)] = o[kv, g * SW_BLOCK:(g + 1) * SW_BLOCK, :].astype(_MXU)

    return pl.pallas_call(
        body, name="attn_fwd", grid=(nb,), in_specs=[q_spec] + k_specs + v_specs + [_full((SW_HEADS, 1, 1))],
        out_specs=q_spec, out_shape=_sds((t, qw), _MXU), compiler_params=_params(("parallel",)),
    )(qr, kr, kr, kr, proj, proj, proj, sinks)


def _attn_bwd(qr, kr, proj, sinks, do):
    t = qr.shape[0]
    nb = t // SW_BLOCK
    qw = SW_HEADS * SW_HEAD_DIM
    k_specs, v_specs = _band_specs(nb, C_VSW)
    q_spec = pl.BlockSpec((SW_BLOCK, qw), lambda n: (n, 0))
    part = pl.BlockSpec((1, 3 * SW_BLOCK, SW_KV_WIDTH), lambda n: (n, 0, 0))

    def body(q_ref, k0, k1, k2, v0, v1, v2, s_ref, do_ref, dq_ref, dk_ref, dv_ref, ds_ref):
        mask = _band_mask(pl.program_id(0), nb)
        kb = jnp.concatenate([k0[...], k1[...], k2[...]], axis=0).astype(F32)
        vb = jnp.concatenate([v0[...], v1[...], v2[...]], axis=0)

        @pl.when(pl.program_id(0) == 0)
        def _():
            ds_ref[...] = jnp.zeros_like(ds_ref)

        qs, kbs, vbs, sinks_ = _kv_batches(q_ref, kb, vb, s_ref)
        dos = jnp.stack([jnp.concatenate([do_ref[:, _head_cols(kv, g)] for g in range(SW_GROUP)], axis=0)
                         for kv in range(SW_KV_HEADS)])

        def head_bwd(q_, k_, v_, sink_, do_):
            _, vjp = jax.vjp(functools.partial(_attn_core, mask=mask), q_, k_, v_, sink_)
            return vjp(do_)

        dqs, dkb, dvb, dsink = jax.vmap(head_bwd)(qs.astype(F32), kbs, vbs, sinks_, dos)
        for kv in range(SW_KV_HEADS):
            kvc = slice(kv * SW_HEAD_DIM, (kv + 1) * SW_HEAD_DIM)
            dk_ref[0, :, kvc] = dkb[kv]
            dv_ref[0, :, kvc] = dvb[kv]
            for g in range(SW_GROUP):
                rows = slice(g * SW_BLOCK, (g + 1) * SW_BLOCK)
                dq_ref[:, _head_cols(kv, g)] = dqs[kv, rows, :]
                ds_ref[kv * SW_GROUP + g] += jnp.sum(dsink[kv, rows, :], axis=0, keepdims=True)

    parts = _sds((nb, 3 * SW_BLOCK, SW_KV_WIDTH))
    return pl.pallas_call(
        body, name="attn_bwd", grid=(nb,), in_specs=[q_spec] + k_specs + v_specs + [_full((SW_HEADS, 1, 1)), q_spec],
        out_specs=[q_spec, part, part, _full((SW_HEADS, 1, 1))], out_shape=[_sds((t, qw)), parts, parts, _sds((SW_HEADS, 1, 1))],
        compiler_params=_params(("arbitrary",)),
    )(qr, kr, kr, kr, proj, proj, proj, sinks, do)


def _band_sum(parts, name, dtype):
    nb = parts.shape[0]
    w = parts.shape[2]

    def spec(shift, slot):
        return pl.BlockSpec((1, SW_BLOCK, w), lambda m: (jnp.clip(m + shift, 0, nb - 1), slot, 0))

    def body(prev_ref, own_ref, next_ref, o_ref):
        m = pl.program_id(0)
        total = own_ref[0] + jnp.where(m > 0, prev_ref[0], 0.0) + jnp.where(m < nb - 1, next_ref[0], 0.0)
        o_ref[...] = total.astype(dtype)

    return pl.pallas_call(
        body, name=name, grid=(nb,), in_specs=[spec(-1, 2), spec(0, 1), spec(1, 0)],
        out_specs=pl.BlockSpec((SW_BLOCK, w), lambda m: (m, 0)), out_shape=_sds((nb * SW_BLOCK, w), dtype),
        compiler_params=_params(("parallel",)),
    )(parts, parts, parts)


def _gate_rows(gbo):
    t = gbo.shape[0]
    return gbo[:, N_HD:2 * N_HD].T.reshape(N_HD, t // DN_CHUNK, 1, DN_CHUNK)


def _gate_rows_t(dgc_r):
    t = dgc_r.shape[1] * DN_CHUNK
    return jnp.pad(dgc_r.reshape(N_HD, t).T, ((0, 0), (N_HD, LANES - 2 * N_HD)))


def _layer_fwd(x, xb, w, cos, sin, carry=None):
    proj = _mm(xb, w["wm"], "nn", name="proj")
    ba = _mm(xb, w["wba"], "nn", name="proj_gates")
    qn, kn, vv = [_prep_fwd(proj, w["conv"], kind) for kind in range(3)]
    gbo, beta_c, gc_c = _gb_fwd(ba, w["alog"], w["dtb"])
    gc_r = _gate_rows(gbo)
    u, wk, qk, qd, kd, egl, tinv, *carried = _dn_local_fwd(qn, kn, vv, beta_c, gc_c, gc_r, carry=carry)
    o2, vn, states = _dn_rec_fwd(u, wk, qk, qd, kd, egl)
    o_dn = _post_fwd(o2, proj, w["dnw"])
    qr, kr = _rope_apply(proj, proj, C_QSW, C_KSW, cos, sin, "rope_fwd", _MXU)
    o_sw = _attn_fwd(qr, kr, proj, w["sinks"])
    ya = _mm(o_dn, w["wa"], "nn", name="branch_a")
    yb = _mm(o_sw, w["wb"], "nn", name="branch_b")
    merged = _merge_fwd(proj, ya, yb)
    mix = _mm(merged, w["wo"], "nn", name="mix_out")
    x1, x1b = _ln_fwd(x, mix, w["ln1g"], w["ln1b"], "ln1_fwd")
    gu = _mm(x1b, w["wgu"], "nn", name="ffn_up")
    h = _swiglu_fwd(gu)
    f = _mm(h, w["wd"], "nn", name="ffn_down")
    x2, x2b = _ln_fwd(x1, f, w["ln2g"], w["ln2b"], "ln2_fwd")
    res = dict(x=x, xb=xb, proj=proj, ba=ba, qn=qn, kn=kn, vv=vv, beta_c=beta_c, gc_c=gc_c, gc_r=gc_r, wk=wk, qk=qk, qd=qd, kd=kd, egl=egl, tinv=tinv, vn=vn,
               states=states, o2=o2, o_dn=o_dn, qr=qr, kr=kr, o_sw=o_sw, ya=ya, yb=yb, merged=merged, mix=mix, x1=x1, x1b=x1b,
               gu=gu, h=h, f=f)
    return x2, x2b, res, (carried or None)


def _layer_bwd(dx2, w, r, cos, sin, carry=None):
    dx1, df, dln2g, dln2b = _ln_bwd(r["x1"], r["f"], w["ln2g"], w["ln2b"], dx2, "ln2_bwd")
    dh = _mm(df, w["wd"], "nt", name="d_ffn_hidden")
    dwd = _mm(r["h"], df, "tn", name="dw_ffn_down", out_dtype=_MXU)
    dgu = _swiglu_bwd(r["gu"], dh)
    dwgu = _mm(r["x1b"], dgu, "tn", name="dw_ffn_up", out_dtype=_MXU)
    dx1 = _mm(dgu, w["wgu"], "nt", name="dx_ffn", add=dx1)
    dx, dmix, dln1g, dln1b = _ln_bwd(r["x"], r["mix"], w["ln1g"], w["ln1b"], dx1, "ln1_bwd")
    dmerged = _mm(dmix, w["wo"], "nt", name="d_merged")
    dwo = _mm(r["merged"], dmix, "tn", name="dw_mix_out", out_dtype=_MXU)
    dga, dgb, dya, dyb = _merge_bwd(r["proj"], r["ya"], r["yb"], dmerged)
    dwa = _mm(r["o_dn"], dya, "tn", name="dw_branch_a", out_dtype=_MXU)
    do_dn = _mm(dya, w["wa"], "nt", name="d_branch_a")
    dwb = _mm(r["o_sw"], dyb, "tn", name="dw_branch_b", out_dtype=_MXU)
    do_sw = _mm(dyb, w["wb"], "nt", name="d_branch_b")
    do, dz, ddnw = _post_bwd(r["o2"], r["proj"], w["dnw"], do_dn)
    du, dwk, dqk, dqd, dkd, degl = _dn_rec_bwd(r["wk"], r["qk"], r["qd"], r["kd"], r["egl"], r["vn"], r["states"], do)
    dq3, dk3, dv3, dbeta_c, dgc_c, dgc_r, *carried = _dn_local_bwd(r["qn"], r["kn"], r["vv"], r["beta_c"], r["gc_c"], r["gc_r"],
                                                                   r["tinv"], du, dwk, dqk, dqd, dkd, degl, carry=carry)
    dqkv, dconv = zip(*[_prep_bwd(r["proj"], w["conv"], d2, kind) for kind, d2 in enumerate((dq3, dk3, dv3))])
    dconv = jnp.concatenate(dconv, axis=1)
    dba, dalog, ddtb = _gb_bwd(r["ba"], w["alog"], w["dtb"], dbeta_c, dgc_c, _gate_rows_t(dgc_r))
    dqr, dkparts, dvparts, dsinks = _attn_bwd(r["qr"], r["kr"], r["proj"], w["sinks"], do_sw)
    dkr = _band_sum(dkparts, "attn_dk_sum", F32)
    dv = _band_sum(dvparts, "attn_dv_sum", _MXU)
    dq_sw, dk_sw = _rope_apply(dqr, dkr, 0, 0, cos, -sin, "rope_bwd", _MXU)
    dproj = jnp.concatenate([*dqkv, dz, dq_sw, dga, dgb, dk_sw, dv], axis=1)
    dwm = _mm(r["xb"], dproj, "tn", name="dw_proj", out_dtype=_MXU)
    dwba = _mm(r["xb"], dba, "tn", name="dw_proj_gates", out_dtype=_MXU)
    dx = _mm(dproj, w["wm"], "nt", name="dx_proj", add=dx)
    dx = _mm(dba, w["wba"], "nt", name="dx_proj_gates", add=dx)
    grads = dict(wm=dwm, wba=dwba, conv=dconv, alog=dalog, dtb=ddtb, dnw=ddnw, sinks=dsinks, wa=dwa, wb=dwb, wo=dwo,
                 ln1g=dln1g, ln1b=dln1b, wgu=dwgu, wd=dwd, ln2g=dln2g, ln2b=dln2b)
    return dx, grads, (carried or None)


def _rope_tables(t):
    half = SW_HEAD_DIM // 2
    inv_freq = ROPE_THETA ** (-jnp.arange(half, dtype=F32) / half)
    ang = jnp.arange(t, dtype=F32)[:, None] * inv_freq[None, :]
    return jnp.tile(jnp.cos(ang), (1, LANES // half)), jnp.tile(jnp.sin(ang), (1, LANES // half))


def _trunk(x, target, n_layers, layer_weights, fwd_carry, grads_done):
    cos, sin = _rope_tables(x.shape[0])
    xb = x.astype(_MXU)
    saved, weights, carried = [], [], None
    for i in range(n_layers):
        w = layer_weights(i, carried)
        x, xb, res, carried = _layer_fwd(x, xb, w, cos, sin, carry=fwd_carry(i))
        saved.append(res)
        weights.append(w)
    sq, dx = _loss_head(x, target)
    carry = None
    for i in reversed(range(n_layers)):
        dx, grads, carried = _layer_bwd(dx, weights[i], saved[i], cos, sin, carry=carry)
        carry = grads_done(i, grads, carried)
    return sq, dx, carry


N_CHIPS = 4


def _mesh_pos():
    return lax.axis_index("x"), lax.axis_index("y"), lax.axis_index("c")


def _other_chips(x, y):
    return [(1 - x, y), (x, 1 - y), (1 - x, 1 - y)]


def _remote_copy(src, dst, sems, k, to):
    send_sems, recv_sems, base = sems
    return pltpu.make_async_remote_copy(src_ref=src, dst_ref=dst, send_sem=send_sems.at[base + k],
                                        recv_sem=recv_sems.at[base + k], device_id=to, device_id_type=pl.DeviceIdType.MESH)


def _exchange_sems(n_arrays, per_array):
    return [pltpu.SemaphoreType.DMA((n_arrays * per_array,)), pltpu.SemaphoreType.DMA((n_arrays * per_array,)),
            pltpu.SemaphoreType.DMA((n_arrays,))]


def _comm_call(body, name, out_shapes, per_array, operands):
    n = len(operands)
    hbm = pl.BlockSpec(memory_space=pl.ANY)

    def flat_body(*refs):
        body(refs[:n], refs[n:2 * n], *refs[2 * n:])

    return pl.pallas_call(
        flat_body, name=name, in_specs=[hbm] * n, out_specs=[hbm] * n, out_shape=list(out_shapes),
        scratch_shapes=_exchange_sems(n, per_array), compiler_params=pltpu.CompilerParams(has_side_effects=True),
    )(*operands)


class _Gather:
    n_sems = N_DEV - 1

    @staticmethod
    def out_shape(block):
        return _sds((N_DEV,) + block.shape, block.dtype)

    @staticmethod
    def _own(x_ref, o_ref, sems, local_sem):
        x, y, c = _mesh_pos()
        mine = o_ref.at[4 * x + 2 * y + c]
        first = [_remote_copy(x_ref, mine, sems, 0, (x, y, 1 - c))]
        first += [_remote_copy(x_ref, mine, sems, 1 + j, (*chip, c)) for j, chip in enumerate(_other_chips(x, y))]
        return pltpu.make_async_copy(x_ref, mine, local_sem), first

    @classmethod
    def start(cls, x_ref, o_ref, sems, local_sem):
        mine, first = cls._own(x_ref, o_ref, sems, local_sem)
        mine.start()
        for cp in first:
            cp.start()

    @classmethod
    def finish(cls, x_ref, o_ref, sems, local_sem):
        x, y, c = _mesh_pos()
        sibling = (x, y, 1 - c)
        chips = _other_chips(x, y)
        slot = lambda px, py, pc: o_ref.at[4 * px + 2 * py + pc]
        mine, first = cls._own(x_ref, o_ref, sems, local_sem)
        passed = [_remote_copy(slot(*chip, c), slot(*chip, c), sems, 4 + j, sibling) for j, chip in enumerate(chips)]
        for j, chip in enumerate(chips):
            _remote_copy(x_ref, slot(*chip, c), sems, 1 + j, sibling).wait_recv()
            passed[j].start()
        _remote_copy(x_ref, slot(x, y, 1 - c), sems, 0, sibling).wait_recv()
        for j, chip in enumerate(chips):
            _remote_copy(x_ref, slot(*chip, 1 - c), sems, 4 + j, sibling).wait_recv()
        for cp in first + passed:
            cp.wait_send()
        mine.wait()


class _ChipExchange:
    n_sems = N_CHIPS - 1

    @staticmethod
    def out_shape(parts):
        return _sds(parts.shape, parts.dtype)

    @staticmethod
    def _own(x_ref, o_ref, sems, local_sem):
        x, y, c = _mesh_pos()
        me = 2 * x + y
        sent = [_remote_copy(x_ref.at[2 * cx + cy], o_ref.at[me], sems, j, (cx, cy, c))
                for j, (cx, cy) in enumerate(_other_chips(x, y))]
        return pltpu.make_async_copy(x_ref.at[me], o_ref.at[me], local_sem), sent

    @classmethod
    def start(cls, x_ref, o_ref, sems, local_sem):
        mine, sent = cls._own(x_ref, o_ref, sems, local_sem)
        mine.start()
        for cp in sent:
            cp.start()

    @classmethod
    def finish(cls, x_ref, o_ref, sems, local_sem):
        x, y, c = _mesh_pos()
        mine, sent = cls._own(x_ref, o_ref, sems, local_sem)
        for j, (cx, cy) in enumerate(_other_chips(x, y)):
            _remote_copy(x_ref.at[2 * x + y], o_ref.at[2 * cx + cy], sems, j, (cx, cy, c)).wait_recv()
        for cp in sent:
            cp.wait_send()
        mine.wait()


def _run_exchange(kind, phase, x_refs, o_refs, send_sems, recv_sems, local_sems):
    for i, (x_ref, o_ref) in enumerate(zip(x_refs, o_refs)):
        getattr(kind, phase)(x_ref, o_ref, (send_sems, recv_sems, i * kind.n_sems), local_sems.at[i])


def _exchange_alone(kind, operands, name):
    def body(x_refs, o_refs, *sems):
        _run_exchange(kind, "start", x_refs, o_refs, *sems)
        _run_exchange(kind, "finish", x_refs, o_refs, *sems)

    return _comm_call(body, name, [kind.out_shape(a) for a in operands], kind.n_sems, operands)


def _all_gather(block, name):
    return _exchange_alone(_Gather, [block], name)[0]


def _carried(kind, operands, body, n_in, n_out, grid):
    hbm = pl.BlockSpec(memory_space=pl.ANY)
    n_x = len(operands)

    def wrapped(*refs):
        ins, x_refs = refs[:n_in], refs[n_in:n_in + n_x]
        outs = refs[n_in + n_x:n_in + n_x + n_out]
        o_refs = refs[n_in + n_x + n_out:n_in + 2 * n_x + n_out]
        sems = refs[n_in + 2 * n_x + n_out:n_in + 2 * n_x + n_out + 3]
        rest = refs[n_in + 2 * n_x + n_out + 3:]
        first, last = None, None
        for axis, size in enumerate(grid):
            at0, at1 = pl.program_id(axis) == 0, pl.program_id(axis) == size - 1
            first = at0 if first is None else first & at0
            last = at1 if last is None else last & at1
        pl.when(first)(lambda: _run_exchange(kind, "start", x_refs, o_refs, *sems))
        body(*ins, *outs, *rest)
        pl.when(last)(lambda: _run_exchange(kind, "finish", x_refs, o_refs, *sems))

    return wrapped, [hbm] * n_x, [hbm] * n_x, [kind.out_shape(a) for a in operands], _exchange_sems(n_x, kind.n_sems)


def _sibling_swap(parts, name):
    def body(x_refs, o_refs, send_sems, recv_sems, local_sems):
        x, y, c = _mesh_pos()
        copies = [_remote_copy(x_ref.at[2 * q + (1 - c)], o_ref.at[q], (send_sems, recv_sems, i * N_CHIPS), q, (x, y, 1 - c))
                  for i, (x_ref, o_ref) in enumerate(zip(x_refs, o_refs)) for q in range(N_CHIPS)]
        for cp in copies:
            cp.start()
        for cp in copies:
            cp.wait()

    return _comm_call(body, name, [_sds((N_CHIPS,) + p.shape[1:], p.dtype) for p in parts], N_CHIPS, parts)


def _pair_sum(a, b, name):
    n, rows, cols = a.shape
    tr = rows if rows <= 512 else _row_tile(rows, 2048)
    blk = pl.BlockSpec((1, tr, cols), lambda q, i: (q, i, 0))

    def body(a_ref, b_ref, o_ref):
        o_ref[...] = (a_ref[...].astype(F32) + b_ref[...].astype(F32)).astype(o_ref.dtype)

    return pl.pallas_call(
        body, name=name, grid=(n, rows // tr), in_specs=[blk, blk], out_specs=blk, out_shape=_sds(a.shape, a.dtype),
        compiler_params=_params(("parallel", "parallel")),
    )(a, b)


def _chip_sums(parts, name):
    c = lax.axis_index("c")
    from_sibling = _sibling_swap(parts, "swap_" + name)
    own = [lax.dynamic_index_in_dim(p.reshape((N_CHIPS, 2) + p.shape[1:]), c, axis=1, keepdims=False) for p in parts]
    return [_pair_sum(a, b, f"pair_sum_{name}_{k}") for k, (a, b) in enumerate(zip(own, from_sibling))]


def _reduce_to_owner(parts, name):
    return _exchange_alone(_ChipExchange, _chip_sums([parts], name), "exchange_" + name)[0]


def _sum_adamw(parts, w, m, v, name):
    rows, cols = w.shape
    n_parts = parts.shape[0]
    tr = rows if rows <= 512 else _row_tile(rows)
    blk = pl.BlockSpec((tr, cols), lambda i: (i, 0))

    def body(p_ref, w_ref, m_ref, v_ref, g_ref, d_ref, nm_ref, nv_ref):
        g = p_ref[0].astype(F32)
        for i in range(1, n_parts):
            g = g + p_ref[i].astype(F32)
        nm = ADAM_B1 * m_ref[...] + (1.0 - ADAM_B1) * g
        nv = ADAM_B2 * v_ref[...] + (1.0 - ADAM_B2) * jnp.square(g)
        m_hat = nm / (1.0 - ADAM_B1 ** ADAM_STEP)
        v_hat = nv / (1.0 - ADAM_B2 ** ADAM_STEP)
        g_ref[...] = g
        d_ref[...] = -ADAM_LR * (m_hat / (jnp.sqrt(v_hat) + ADAM_EPS) + ADAM_WD * w_ref[...])
        nm_ref[...] = nm
        nv_ref[...] = nv

    return pl.pallas_call(
        body, name=name, grid=(rows // tr,), in_specs=[pl.BlockSpec((n_parts, tr, cols), lambda i: (0, i, 0)), blk, blk, blk],
        out_specs=[blk] * 4, out_shape=[_sds((rows, cols))] * 4, compiler_params=_params(("parallel",)),
    )(parts, w, m, v)


def _row_tile(rows, pref=256):
    t = pref
    while t >= 8:
        if rows % t == 0:
            return t
        t //= 2
    return rows


def _gathered_cols(g):
    g = jnp.moveaxis(g, 0, -2)
    return g.reshape(g.shape[:-2] + (g.shape[-2] * g.shape[-1],))


def _gathered_rows(g):
    g = jnp.moveaxis(g, 0, -3)
    return g.reshape(g.shape[:-3] + (g.shape[-3] * g.shape[-2], g.shape[-1]))


def _col_parts(full):
    c = full.shape[-1]
    return jnp.moveaxis(full.reshape(full.shape[:-1] + (N_DEV, c // N_DEV)), -2, 0)


def _row_parts(full):
    rows, c = full.shape[-2:]
    return jnp.moveaxis(full.reshape(full.shape[:-2] + (N_DEV, rows // N_DEV, c)), -3, 0)


def _w_in_split(w_in):
    s = lambda a, n: w_in[..., a:a + n]
    main = jnp.concatenate([s(R_QKV, 3072), s(R_Z, 1024), s(R_QSW, 1024), s(R_G, 2048), s(R_KSW, 256), s(R_VSW, 256)], axis=-1)
    gates = jnp.pad(s(R_BA, 2 * N_HD), [(0, 0)] * (w_in.ndim - 1) + [(0, LANES - 2 * N_HD)])
    return main, gates


def _w_in_join(dmain, dgates):
    s = lambda a, n: dmain[..., a:a + n]
    return jnp.concatenate([s(C_QKV, 3072), s(C_Z, 1024), dgates[..., :2 * N_HD], s(C_QSW, 1024), s(C_KSW, 256), s(C_VSW, 256),
                            s(C_GA, 2048)], axis=-1)


def _lane_row(a, offset):
    l, n = a.shape
    return jnp.pad(a, ((0, 0), (offset, LANES - offset - n)))[:, None, :]


def kernel(x, w_in, conv_w, a_log, dt_bias, dn_norm_w, sinks, w_branch_a, w_branch_b, w_out, ln1_g, ln1_b, w_gate_up, w_down, ln2_g, ln2_b, loss_target, m_w_in, m_conv_w, m_a_log, m_dt_bias, m_dn_norm_w, m_sinks, m_w_branch_a, m_w_branch_b, m_w_out, m_ln1_g, m_ln1_b, m_w_gate_up, m_w_down, m_ln2_g, m_ln2_b, v_w_in, v_conv_w, v_a_log, v_dt_bias, v_dn_norm_w, v_sinks, v_w_branch_a, v_w_branch_b, v_w_out, v_ln1_g, v_ln1_b, v_w_gate_up, v_w_down, v_ln2_g, v_ln2_b):
    l = DEPTH
    bf = lambda a: a.astype(_MXU)
    shards = [bf(w_in), bf(w_gate_up), bf(w_branch_a), bf(w_branch_b), bf(w_out), bf(w_down)]
    first = _exchange_alone(_Gather, [s[0] for s in shards], "gather_layer_0")
    conv_full = _gathered_cols(_all_gather(conv_w, "gather_conv_w"))
    row = lambda a: a[:, None, :]
    small = dict(
        conv=jnp.pad(conv_full, ((0, 0), (0, 8 - DN_CONV), (0, 0))), alog=_lane_row(a_log.reshape(l, N_HD), N_HD),
        dtb=_lane_row(dt_bias.reshape(l, N_HD), N_HD), dnw=row(dn_norm_w), sinks=sinks.reshape(l, SW_HEADS, 1, 1),
        ln1g=row(ln1_g), ln1b=row(ln1_b), ln2g=row(ln2_g), ln2b=row(ln2_b))

    def layer_weights(i, carried):
        s_in, s_gu, s_a, s_b, s_o, s_d = first if i == 0 else carried
        wm, wba = _w_in_split(_gathered_cols(s_in))
        return dict(wm=wm, wba=wba, wgu=_gathered_cols(s_gu), wa=_gathered_rows(s_a), wb=_gathered_rows(s_b),
                    wo=_gathered_rows(s_o), wd=_gathered_rows(s_d), **{k: a[i] for k, a in small.items()})

    def fwd_carry(i):
        return (_Gather, [s[i + 1] for s in shards]) if i + 1 < l else None

    layer_grads, received, waiting = [None] * l, [None] * l, []

    def grads_done(i, g_i, carried):
        if waiting:
            received[waiting.pop()] = carried
        layer_grads[i] = g_i
        parts = [_col_parts(_w_in_join(g_i["wm"], g_i["wba"])), _col_parts(g_i["wgu"]), _row_parts(g_i["wa"]),
                 _row_parts(g_i["wb"]), _row_parts(g_i["wo"]), _row_parts(g_i["wd"])]
        waiting.append(i)
        return (_ChipExchange, _chip_sums(parts, f"layer_{i}"))

    sq, dx, last = _trunk(x[0], loss_target[0], l, layer_weights, fwd_carry, grads_done)
    received[waiting.pop()] = _exchange_alone(last[0], last[1], "exchange_layer_0")
    loss = lax.psum(0.5 * sq[0, 0] / D_MODEL, ("x", "y", "c"))
    g = {k: jnp.stack([gi[k] for gi in layer_grads]) for k in small}

    def adamw(parts, w, m, v, name):
        rows = w.shape[0] * w.shape[1]
        flat = lambda a: a.reshape(rows, a.shape[-1])
        outs = _sum_adamw(parts.reshape(parts.shape[0], rows, w.shape[-1]), flat(w), flat(m), flat(v), "adamw_" + name)
        return [o.reshape(w.shape) for o in outs]

    got = [jnp.stack(per_layer, axis=1) for per_layer in zip(*received)]
    dconv = _reduce_to_owner(_col_parts(g["conv"][:, :DN_CONV, :]).reshape(N_DEV, l * DN_CONV, -1), "conv_w")
    results = {
        "w_in": adamw(got[0], w_in, m_w_in, v_w_in, "w_in"),
        "conv_w": adamw(dconv.reshape(N_CHIPS, l, DN_CONV, -1), conv_w, m_conv_w, v_conv_w, "conv_w"),
        "w_branch_a": adamw(got[2], w_branch_a, m_w_branch_a, v_w_branch_a, "w_branch_a"),
        "w_branch_b": adamw(got[3], w_branch_b, m_w_branch_b, v_w_branch_b, "w_branch_b"),
        "w_out": adamw(got[4], w_out, m_w_out, v_w_out, "w_out"),
        "w_gate_up": adamw(got[1], w_gate_up, m_w_gate_up, v_w_gate_up, "w_gate_up"),
        "w_down": adamw(got[5], w_down, m_w_down, v_w_down, "w_down"),
    }

    small_w = {"a_log": a_log.reshape(l, N_HD), "dt_bias": dt_bias.reshape(l, N_HD), "dn_norm_w": dn_norm_w, "sinks": sinks,
               "ln1_g": ln1_g, "ln1_b": ln1_b, "ln2_g": ln2_g, "ln2_b": ln2_b}
    small_m = {"a_log": m_a_log, "dt_bias": m_dt_bias, "dn_norm_w": m_dn_norm_w, "sinks": m_sinks, "ln1_g": m_ln1_g,
               "ln1_b": m_ln1_b, "ln2_g": m_ln2_g, "ln2_b": m_ln2_b}
    small_v = {"a_log": v_a_log, "dt_bias": v_dt_bias, "dn_norm_w": v_dn_norm_w, "sinks": v_sinks, "ln1_g": v_ln1_g,
               "ln1_b": v_ln1_b, "ln2_g": v_ln2_g, "ln2_b": v_ln2_b}
    small_g = {"a_log": g["alog"][:, 0, N_HD:2 * N_HD], "dt_bias": g["dtb"][:, 0, N_HD:2 * N_HD], "dn_norm_w": g["dnw"][:, 0, :],
               "sinks": g["sinks"].reshape(l, SW_HEADS), "ln1_g": g["ln1g"][:, 0, :], "ln1_b": g["ln1b"][:, 0, :],
               "ln2_g": g["ln2g"][:, 0, :], "ln2_b": g["ln2b"][:, 0, :]}
    names = list(small_w)
    cat = lambda d: jnp.concatenate([d[n].reshape(l, -1) for n in names], axis=1)
    widths = [small_w[n].shape[1] for n in names]
    total = sum(widths)
    padded = -(-total // LANES) * LANES
    pad = lambda a: jnp.pad(a, ((0, 8 - l), (0, padded - total)))
    got = _all_gather(pad(cat(small_g)), "gather_small_grads")
    outs = _sum_adamw(got, pad(cat(small_w)), pad(cat({n: small_m[n].reshape(l, -1) for n in names})),
                      pad(cat({n: small_v[n].reshape(l, -1) for n in names})), "adamw_small")
    off = 0
    for n, wd_ in zip(names, widths):
        shape = {"a_log": a_log.shape, "dt_bias": dt_bias.shape}.get(n, small_w[n].shape)
        results[n] = [o[:l, off:off + wd_].reshape(shape) for o in outs]
        off += wd_

    order = ["w_in", "conv_w", "a_log", "dt_bias", "dn_norm_w", "sinks", "w_branch_a", "w_branch_b", "w_out", "ln1_g", "ln1_b",
             "w_gate_up", "w_down", "ln2_g", "ln2_b"]
    return (loss, dx[None], *[results[n][0] for n in order], *[results[n][1] for n in order],
            *[results[n][2] for n in order], *[results[n][3] for n in order])
```

```python
import functools

import jax
import jax.numpy as jnp
from jax import lax
from jax.experimental import pallas as pl
from jax.experimental.pallas import tpu as pltpu

F32 = jnp.float32
_MXU = jnp.bfloat16
_HI = lax.Precision.HIGHEST
_MID = lax.Precision.HIGH

N_DEV = 8
D_MODEL = 1024
DEPTH = 4
DN_HEADS = 8
DN_HEAD_DIM = 128
DN_WIDTH = DN_HEADS * DN_HEAD_DIM
DN_CONV = 5
DN_CHUNK = 64
SW_HEADS = 16
SW_KV_HEADS = 4
SW_HEAD_DIM = 64
SW_GROUP = SW_HEADS // SW_KV_HEADS
SW_BLOCK = 128
SW_KV_WIDTH = SW_KV_HEADS * SW_HEAD_DIM
ROPE_THETA = 10000.0
FFN_HIDDEN = 2816
DN_ALPHA = (2.0 * DEPTH) ** 0.25
LN_EPS = 1e-5
RMS_EPS = 1e-6
ADAM_LR = 0.001
ADAM_B1 = 0.9
ADAM_B2 = 0.999
ADAM_EPS = 1e-08
ADAM_WD = 0.01
ADAM_STEP = 10

LANES = 128
N_HD = 2 * DN_HEADS
DN_GROUP = 4 * DN_CHUNK
INV_SUB = 16
LOCAL_ROWS = 512
REC_ROWS = 256
ROW_TILE = 256
VMEM_LIMIT = 48 << 20

C_QKV, C_Z, C_QSW, C_GA, C_GB, C_KSW, C_VSW = 0, 3072, 4096, 5120, 6144, 7168, 7424
MAIN_COLS = 7680
R_QKV, R_Z, R_BA, R_QSW, R_KSW, R_VSW, R_G = 0, 3072, 4096, 4128, 5152, 5408, 5664
IN_COLS = 7712


_NN = ((1,), (0,))
_NT = ((1,), (1,))
_TN = ((0,), (0,))


def _dg(a, b, dims, precision):
    if precision is not None:
        return lax.dot_general(a, b, (dims, ((), ())), precision=precision, preferred_element_type=F32)
    return lax.dot_general(a.astype(_MXU), b.astype(_MXU), (dims, ((), ())), preferred_element_type=F32)


def _make_dots(hi):
    @jax.custom_vjp
    def nn(a, b):
        return _dg(a, b, _NN, hi)

    @jax.custom_vjp
    def nt(a, b):
        return _dg(a, b, _NT, hi)

    @jax.custom_vjp
    def tn(a, b):
        return _dg(a, b, _TN, hi)

    nn.defvjp(lambda a, b: (nn(a, b), (a, b)), lambda r, g: (nt(g, r[1]), tn(r[0], g)))
    nt.defvjp(lambda a, b: (nt(a, b), (a, b)), lambda r, g: (nn(g, r[1]), tn(g, r[0])))
    tn.defvjp(lambda a, b: (tn(a, b), (a, b)), lambda r, g: (nt(r[1], g), nn(r[0], g)))
    return nn, nt, tn


_bnn, _bnt, _btn = _make_dots(None)
_hnn, _hnt, _htn = _make_dots(_HI)
_mnn, _mnt, _mtn = _make_dots(_MID)


def _neumann(a, order):
    n = a.shape[0]
    eye = (lax.broadcasted_iota(jnp.int32, (n, n), 0) == lax.broadcasted_iota(jnp.int32, (n, n), 1)).astype(F32)
    inv = eye - a
    p = a
    span = 2
    while span < order:
        p = _mnn(p, p)
        inv = inv + _mnn(inv, p)
        span *= 2
    return inv


def _inv_unit(a, order):
    n = a.shape[0]
    ii = lax.broadcasted_iota(jnp.int32, (n, n), 0)
    jj = lax.broadcasted_iota(jnp.int32, (n, n), 1)
    near = (ii & -INV_SUB) == (jj & -INV_SUB)
    d_inv = _neumann(jnp.where(near, a, 0.0), INV_SUB)
    outer = _neumann(_mnn(d_inv, jnp.where(near, 0.0, a)), order // INV_SUB)
    return _mnn(outer, d_inv)


def _inv_unit_t(t, g):
    return -_mnt(_mtn(t, g), t)


def _silu(x):
    return x * jax.nn.sigmoid(x)


def _softplus(x):
    return jnp.maximum(x, 0.0) + jnp.log1p(jnp.exp(-jnp.abs(x)))


def _params(sem=None):
    kw = {"vmem_limit_bytes": VMEM_LIMIT}
    if sem is not None:
        kw["dimension_semantics"] = sem
    return pltpu.CompilerParams(**kw)


def _tile(dim, pref):
    if dim <= pref:
        return dim
    t = (pref // LANES) * LANES
    while t > LANES and dim % t:
        t -= LANES
    assert dim % t == 0, (dim, pref)
    return t


def _full(shape):
    zeros = (0,) * len(shape)
    return pl.BlockSpec(shape, lambda *_: zeros)


def _sds(shape, dtype=F32):
    return jax.ShapeDtypeStruct(shape, dtype)


def _mm(a, b, mode, *, name, add=None, tm=1536, tn=1536, tk=1536, out_dtype=F32):
    if mode == "nn":
        (m, k), (k2, n) = a.shape, b.shape
    elif mode == "nt":
        (m, k), (n, k2) = a.shape, b.shape
    else:
        (k, m), (k2, n) = a.shape, b.shape
    assert k == k2, (a.shape, b.shape, mode)
    tm, tn, tk = _tile(m, tm), _tile(n, tn), _tile(k, tk)
    nk = k // tk
    dims = {"nn": _NN, "nt": _NT, "tn": _TN}[mode]

    def body(*refs):
        if add is None:
            a_ref, b_ref, o_ref, acc = refs
        else:
            a_ref, b_ref, add_ref, o_ref, acc = refs
        kk = pl.program_id(2)

        @pl.when(kk == 0)
        def _():
            acc[...] = jnp.zeros_like(acc)

        acc[...] += _dg(a_ref[...], b_ref[...], dims, None)

        @pl.when(kk == nk - 1)
        def _():
            o_ref[...] = (acc[...] if add is None else acc[...] + add_ref[...]).astype(out_dtype)

    a_spec = pl.BlockSpec((tk, tm), lambda i, j, kk: (kk, i)) if mode == "tn" else pl.BlockSpec((tm, tk), lambda i, j, kk: (i, kk))
    b_spec = pl.BlockSpec((tn, tk), lambda i, j, kk: (j, kk)) if mode == "nt" else pl.BlockSpec((tk, tn), lambda i, j, kk: (kk, j))
    o_spec = pl.BlockSpec((tm, tn), lambda i, j, kk: (i, j))
    ins, specs = [a, b], [a_spec, b_spec]
    if add is not None:
        ins.append(add)
        specs.append(o_spec)
    return pl.pallas_call(
        body, name=name, grid=(m // tm, n // tn, nk), in_specs=specs, out_specs=o_spec,
        out_shape=_sds((m, n), out_dtype), scratch_shapes=[pltpu.VMEM((tm, tn), F32)],
        compiler_params=_params(("parallel", "parallel", "arbitrary")),
    )(*ins)


def _cols(width, start):
    assert start % width == 0
    return pl.BlockSpec((ROW_TILE, width), lambda i: (i, start // width))


def _rows(width):
    return pl.BlockSpec((ROW_TILE, width), lambda i: (i, 0))


def _accumulate(ref, value, step):
    @pl.when(step == 0)
    def _():
        ref[...] = value

    @pl.when(step != 0)
    def _():
        ref[...] += value


def _ln_fn(x, r, g, b):
    u = DN_ALPHA * x + r
    mu = jnp.mean(u, axis=-1, keepdims=True)
    var = jnp.mean(jnp.square(u - mu), axis=-1, keepdims=True)
    return (u - mu) * lax.rsqrt(var + LN_EPS) * g + b


def _ln_fwd(x, r, g, b, name):
    t, d = x.shape

    def body(x_ref, r_ref, g_ref, b_ref, o_ref, ob_ref):
        y = _ln_fn(x_ref[...], r_ref[...], g_ref[...], b_ref[...])
        o_ref[...] = y
        ob_ref[...] = y.astype(_MXU)

    return pl.pallas_call(
        body, name=name, grid=(t // ROW_TILE,), in_specs=[_rows(d), _rows(d), _full((1, d)), _full((1, d))],
        out_specs=[_rows(d), _rows(d)], out_shape=[_sds((t, d)), _sds((t, d), _MXU)], compiler_params=_params(("parallel",)),
    )(x, r, g, b)


def _ln_bwd(x, r, g, b, dy, name):
    t, d = x.shape

    def body(x_ref, r_ref, g_ref, b_ref, dy_ref, dx_ref, dr_ref, dg_ref, db_ref):
        _, vjp = jax.vjp(_ln_fn, x_ref[...], r_ref[...], g_ref[...], b_ref[...])
        dx, dr, dg, db = vjp(dy_ref[...])
        dx_ref[...] = dx
        dr_ref[...] = dr.astype(_MXU)
        _accumulate(dg_ref, dg, pl.program_id(0))
        _accumulate(db_ref, db, pl.program_id(0))

    return pl.pallas_call(
        body, name=name, grid=(t // ROW_TILE,),
        in_specs=[_rows(d), _rows(d), _full((1, d)), _full((1, d)), _rows(d)],
        out_specs=[_rows(d), _rows(d), _full((1, d)), _full((1, d))],
        out_shape=[_sds((t, d)), _sds((t, d), _MXU), _sds((1, d)), _sds((1, d))],
        compiler_params=_params(("arbitrary",)),
    )(x, r, g, b, dy)


def _merge_fn(ga, gb, ya, yb):
    return jax.nn.sigmoid(ga) * ya + jax.nn.sigmoid(gb) * yb


def _merge_fwd(proj, ya, yb):
    t, d = ya.shape

    def body(ga_ref, gb_ref, ya_ref, yb_ref, o_ref):
        o_ref[...] = _merge_fn(ga_ref[...], gb_ref[...], ya_ref[...], yb_ref[...]).astype(_MXU)

    return pl.pallas_call(
        body, name="merge_fwd", grid=(t // ROW_TILE,), in_specs=[_cols(d, C_GA), _cols(d, C_GB), _rows(d), _rows(d)],
        out_specs=_rows(d), out_shape=_sds((t, d), _MXU), compiler_params=_params(("parallel",)),
    )(proj, proj, ya, yb)


def _merge_bwd(proj, ya, yb, dm):
    t, d = ya.shape

    def body(ga_ref, gb_ref, ya_ref, yb_ref, dm_ref, dga_ref, dgb_ref, dya_ref, dyb_ref):
        _, vjp = jax.vjp(_merge_fn, ga_ref[...], gb_ref[...], ya_ref[...], yb_ref[...])
        dga_ref[...], dgb_ref[...], dya_ref[...], dyb_ref[...] = [g.astype(_MXU) for g in vjp(dm_ref[...])]

    return pl.pallas_call(
        body, name="merge_bwd", grid=(t // ROW_TILE,),
        in_specs=[_cols(d, C_GA), _cols(d, C_GB), _rows(d), _rows(d), _rows(d)],
        out_specs=[_rows(d)] * 4, out_shape=[_sds((t, d), _MXU)] * 4, compiler_params=_params(("parallel",)),
    )(proj, proj, ya, yb, dm)


def _swiglu_fn(gate, up):
    return _silu(gate) * up


def _swiglu_fwd(gu):
    t = gu.shape[0]
    f = FFN_HIDDEN
    rows = 128

    def body(gu_ref, o_ref):
        o_ref[...] = _swiglu_fn(gu_ref[:, :f], gu_ref[:, f:]).astype(_MXU)

    return pl.pallas_call(
        body, name="swiglu_fwd", grid=(t // rows,), in_specs=[pl.BlockSpec((rows, 2 * f), lambda i: (i, 0))],
        out_specs=pl.BlockSpec((rows, f), lambda i: (i, 0)), out_shape=_sds((t, f), _MXU), compiler_params=_params(("parallel",)),
    )(gu)


def _swiglu_bwd(gu, dh):
    t = gu.shape[0]
    f = FFN_HIDDEN
    rows = 128

    def body(gu_ref, dh_ref, o_ref):
        _, vjp = jax.vjp(_swiglu_fn, gu_ref[:, :f], gu_ref[:, f:])
        o_ref[:, :f], o_ref[:, f:] = [g.astype(_MXU) for g in vjp(dh_ref[...])]

    return pl.pallas_call(
        body, name="swiglu_bwd", grid=(t // rows,),
        in_specs=[pl.BlockSpec((rows, 2 * f), lambda i: (i, 0)), pl.BlockSpec((rows, f), lambda i: (i, 0))],
        out_specs=pl.BlockSpec((rows, 2 * f), lambda i: (i, 0)), out_shape=_sds((t, 2 * f), _MXU),
        compiler_params=_params(("parallel",)),
    )(gu, dh)


def _loss_head(y, target):
    t, d = y.shape

    def body(y_ref, t_ref, s_ref, dy_ref):
        err = y_ref[...] - t_ref[...]
        dy_ref[...] = err / d
        _accumulate(s_ref, jnp.broadcast_to(jnp.sum(jnp.square(err)), (1, LANES)), pl.program_id(0))

    return pl.pallas_call(
        body, name="loss_head", grid=(t // ROW_TILE,), in_specs=[_rows(d), _rows(d)],
        out_specs=[_full((1, LANES)), _rows(d)], out_shape=[_sds((1, LANES)), _sds((t, d))],
        compiler_params=_params(("arbitrary",)),
    )(y, target)


def _shift_rows(x, s):
    if s == 0:
        return x
    return pltpu.roll(x, (-s) % x.shape[0], 0)


def _conv(x, w):
    half = DN_CONV // 2
    acc = None
    for k in range(DN_CONV):
        term = _shift_rows(x, k - half) * w[k:k + 1, :]
        acc = term if acc is None else acc + term
    return acc


def _act_norm(c, do_norm, scale):
    a = _silu(c)
    if not do_norm:
        return a
    return a * lax.rsqrt(jnp.sum(a * a, axis=-1, keepdims=True) + RMS_EPS) * scale


PREP_ROWS = 512
HALO = 8
_KINDS = ((True, DN_HEAD_DIM ** -0.5), (True, 1.0), (False, 1.0))


def _halo_rows(read, i, pr, t):
    lo, hi = i * pr - HALO, (i + 1) * pr + HALO
    parts = []
    if lo < 0:
        parts.append(jnp.zeros((HALO, LANES), F32))
    parts.append(read(max(lo, 0), min(hi, t)))
    if hi > t:
        parts.append(jnp.zeros((HALO, LANES), F32))
    return jnp.concatenate(parts, axis=0) if len(parts) > 1 else parts[0]


def _prep_fwd(proj, conv_w, kind):
    t = proj.shape[0]
    pr = min(PREP_ROWS, t)
    do_norm, scale = _KINDS[kind]
    blk = pl.BlockSpec((t, LANES), lambda j: (0, kind * DN_HEADS + j))

    def body(x_ref, w_ref, o_ref):
        w = w_ref[...]
        for i in range(t // pr):
            xx = _halo_rows(lambda lo, hi: x_ref[lo:hi, :], i, pr, t)
            c = _conv(xx, w)[HALO:HALO + pr, :]
            o_ref[i * pr:(i + 1) * pr, :] = _act_norm(c, do_norm, scale)

    return pl.pallas_call(
        body, name=f"prep_fwd_{kind}", grid=(DN_HEADS,),
        in_specs=[blk, pl.BlockSpec((8, LANES), lambda j: (0, kind * DN_HEADS + j))],
        out_specs=pl.BlockSpec((t, LANES), lambda j: (0, j)), out_shape=_sds((t, DN_WIDTH)),
        compiler_params=_params(("parallel",)),
    )(proj, conv_w)


def _prep_bwd(proj, conv_w, d2, kind):
    t = proj.shape[0]
    pr = min(PREP_ROWS, t)
    do_norm, scale = _KINDS[kind]
    half = DN_CONV // 2
    blk = pl.BlockSpec((t, LANES), lambda j: (0, kind * DN_HEADS + j))
    oblk = pl.BlockSpec((t, LANES), lambda j: (0, j))

    def body(x_ref, w_ref, d_ref, dx_ref, dw_ref):
        w = w_ref[...]
        own = slice(HALO, HALO + pr)
        dw = jnp.zeros((8, LANES), F32)
        for i in range(t // pr):
            xx = _halo_rows(lambda lo, hi: x_ref[lo:hi, :], i, pr, t)
            dn = _halo_rows(lambda lo, hi: d_ref[0, lo:hi, :] + d_ref[1, lo:hi, :], i, pr, t)
            _, vjp = jax.vjp(lambda c: _act_norm(c, do_norm, scale), _conv(xx, w))
            (dc,) = vjp(dn)
            dx = None
            rows = []
            for k in range(DN_CONV):
                term = _shift_rows(dc, half - k) * w[k:k + 1, :]
                dx = term if dx is None else dx + term
                rows.append(jnp.sum(dc[own, :] * _shift_rows(xx, k - half)[own, :], axis=0, keepdims=True))
            dx_ref[i * pr:(i + 1) * pr, :] = dx[own, :].astype(_MXU)
            dw = dw + jnp.concatenate(rows + [jnp.zeros((8 - DN_CONV, LANES), F32)], axis=0)
        dw_ref[...] = dw

    return pl.pallas_call(
        body, name=f"prep_bwd_{kind}", grid=(DN_HEADS,),
        in_specs=[blk, pl.BlockSpec((8, LANES), lambda j: (0, kind * DN_HEADS + j)), pl.BlockSpec((2, t, LANES), lambda j: (0, 0, j))],
        out_specs=[oblk, pl.BlockSpec((8, LANES), lambda j: (0, j))], out_shape=[_sds((t, DN_WIDTH), _MXU), _sds((8, DN_WIDTH))],
        compiler_params=_params(("parallel",)),
    )(proj, conv_w, d2)


def _gb_fn(ba, alog_row, dtb_row):
    c = DN_CHUNK
    lane = lax.broadcasted_iota(jnp.int32, (c, LANES), 1)
    ii = lax.broadcasted_iota(jnp.int32, (c, c), 0)
    jj = lax.broadcasted_iota(jnp.int32, (c, c), 1)
    beta = jax.nn.sigmoid(ba)
    g = -jnp.exp(alog_row) * _softplus(ba + dtb_row)
    g = jnp.where((lane >= N_HD) & (lane < 2 * N_HD), g, 0.0)
    gc_fwd = _hnn((ii >= jj).astype(F32), g)
    gc_rev = _hnn((ii <= jj).astype(F32), g)
    gc = jnp.where(lane < N_HD + DN_HEADS, gc_fwd, gc_rev)
    return jnp.where(lane < N_HD, beta, jnp.where(lane < 2 * N_HD, gc, 0.0))


def _per_head_spec():
    return pl.BlockSpec((N_HD, ROW_TILE, LANES), lambda i: (0, i, 0))


def _gb_fwd(ba, alog_row, dtb_row):
    t = ba.shape[0]
    n = ROW_TILE // DN_CHUNK

    def body(ba_ref, a_ref, d_ref, o_ref, beta_ref, gc_ref):
        for c in range(n):
            rows = slice(c * DN_CHUNK, (c + 1) * DN_CHUNK)
            out = _gb_fn(ba_ref[rows, :], a_ref[...], d_ref[...])
            o_ref[rows, :] = out
            for j in range(N_HD):
                beta_ref[j, rows, :] = jnp.broadcast_to(out[:, j:j + 1], (DN_CHUNK, LANES))
                gc_ref[j, rows, :] = jnp.broadcast_to(out[:, N_HD + j:N_HD + j + 1], (DN_CHUNK, LANES))

    per_head = _sds((N_HD, t, LANES))
    return pl.pallas_call(
        body, name="gates_fwd", grid=(t // ROW_TILE,), in_specs=[_rows(LANES), _full((1, LANES)), _full((1, LANES))],
        out_specs=[_rows(LANES), _per_head_spec(), _per_head_spec()], out_shape=[_sds((t, LANES)), per_head, per_head],
        compiler_params=_params(("parallel",)),
    )(ba, alog_row, dtb_row)


def _gb_bwd(ba, alog_row, dtb_row, dbeta, dgc, d_rows):
    t = ba.shape[0]
    n = ROW_TILE // DN_CHUNK

    def body(ba_ref, a_ref, d_ref, dbeta_ref, dgc_ref, dr_ref, dba_ref, dal_ref, ddt_ref):
        dal = jnp.zeros((1, LANES), F32)
        ddt = jnp.zeros((1, LANES), F32)
        lane = lax.broadcasted_iota(jnp.int32, (DN_CHUNK, LANES), 1)
        for c in range(n):
            rows = slice(c * DN_CHUNK, (c + 1) * DN_CHUNK)
            cot = dr_ref[rows, :]
            for j in range(N_HD):
                cot = jnp.where(lane == j, dbeta_ref[j, rows, :], cot)
                cot = jnp.where(lane == N_HD + j, dgc_ref[j, rows, :] + cot, cot)
            _, vjp = jax.vjp(_gb_fn, ba_ref[rows, :], a_ref[...], d_ref[...])
            dba, da, dd = vjp(cot)
            dba_ref[rows, :] = dba.astype(_MXU)
            dal = dal + da
            ddt = ddt + dd
        _accumulate(dal_ref, dal, pl.program_id(0))
        _accumulate(ddt_ref, ddt, pl.program_id(0))

    return pl.pallas_call(
        body, name="gates_bwd", grid=(t // ROW_TILE,),
        in_specs=[_rows(LANES), _full((1, LANES)), _full((1, LANES)), _per_head_spec(), _per_head_spec(), _rows(LANES)],
        out_specs=[_rows(LANES), _full((1, LANES)), _full((1, LANES))],
        out_shape=[_sds((t, LANES), _MXU), _sds((1, LANES)), _sds((1, LANES))], compiler_params=_params(("arbitrary",)),
    )(ba, alog_row, dtb_row, dbeta, dgc, d_rows)


def _dn_decay(gcc, gcr, sgn):
    c = DN_CHUNK
    ii = lax.broadcasted_iota(jnp.int32, (c, c), 0)
    jj = lax.broadcasted_iota(jnp.int32, (c, c), 1)
    d = (ii - jj) * sgn
    lower = d >= 0
    return jnp.where(lower, jnp.exp(jnp.where(lower, gcc - gcr, 0.0)), 0.0), d > 0


def _dn_a(k, beta, gcc, gcr, sgn):
    decay, strict = _dn_decay(gcc, gcr, sgn)
    return jnp.where(strict, beta * _bnt(k, k) * decay, 0.0)


def _dn_group(q, k, v, beta, gcc, gcr, sgn):
    n = DN_GROUP
    ii = lax.broadcasted_iota(jnp.int32, (n, n), 0)
    jj = lax.broadcasted_iota(jnp.int32, (n, n), 1)
    same = (ii & -DN_CHUNK) == (jj & -DN_CHUNK)
    d = (ii - jj) * sgn
    lower = same & (d >= 0)
    decay = jnp.where(lower, jnp.exp(jnp.where(lower, gcc - gcr, 0.0)), 0.0)
    a = jnp.where(same & (d > 0), beta * _bnt(k, k) * decay, 0.0)
    t_inv = _inv_unit(a, DN_CHUNK)
    u = _bnn(t_inv, v * beta)
    w = _bnn(t_inv, k * (beta * jnp.exp(gcc)))
    return u, w, _bnt(q, k) * decay, t_inv


def _dn_local(t_inv, q, k, v, beta, gcc, gcr, sgn):
    c = DN_CHUNK
    decay, _ = _dn_decay(gcc, gcr, sgn)
    eg = jnp.exp(gcc)
    u = _bnn(t_inv, v * beta)
    w = _bnn(t_inv, k * (beta * eg))
    qk = _bnt(q, k) * decay
    qd = q * eg
    last = jnp.where(sgn > 0, c - 1, 0)
    onehot = (lax.broadcasted_iota(jnp.int32, (c, 1), 0) == last).astype(F32)
    gl = jnp.sum(gcc * onehot, axis=0, keepdims=True)
    kd = k * jnp.exp(gl - gcc)
    egl = jnp.broadcast_to(jnp.exp(gl), (1, LANES))
    return u, w, qk, qd, kd, egl


def _hd_sign(hd):
    return jnp.where(hd < DN_HEADS, 1, -1).astype(jnp.int32)


def _head_of(hd):
    return jnp.where(hd < DN_HEADS, hd, hd - DN_HEADS)


def _dir_of(hd):
    return jnp.where(hd < DN_HEADS, 0, 1)


def _dn_specs(t):
    nl = LOCAL_ROWS // DN_CHUNK
    wide = pl.BlockSpec((1, LOCAL_ROWS, LANES), lambda hd, i: (hd, i, 0))
    half = pl.BlockSpec((1, LOCAL_ROWS, DN_CHUNK), lambda hd, i: (hd, i, 0))
    col = wide
    row = pl.BlockSpec((1, nl, 1, DN_CHUNK), lambda hd, i: (hd, i, 0, 0))
    egl = pl.BlockSpec((1, nl, 1, LANES), lambda hd, i: (hd, i, 0, 0))
    return wide, half, col, row, egl


def _qkv_specs():
    return [pl.BlockSpec((LOCAL_ROWS, LANES), lambda hd, i: (i, _head_of(hd)))] * 3


def _maybe_carrying(carry, body, name, grid, operands, in_specs, out_specs, out_shape):
    extra_scratch = []
    if carry is not None:
        kind, arrays = carry
        body, more_in, more_out, more_shape, extra_scratch = _carried(kind, arrays, body, len(operands), len(out_shape), grid)
        operands, in_specs = operands + list(arrays), in_specs + more_in
        out_specs, out_shape = out_specs + more_out, out_shape + more_shape
    sem = ("arbitrary",) * len(grid) if carry is not None else ("parallel",) * len(grid)
    return pl.pallas_call(
        body, name=name, grid=grid, in_specs=in_specs, out_specs=out_specs, out_shape=out_shape,
        scratch_shapes=extra_scratch, compiler_params=_params(sem),
    )(*operands)


def _dn_local_fwd(q, k, v, beta_c, gc_c, gc_r, carry=None):
    t = q.shape[0]
    nc = t // DN_CHUNK
    nl = LOCAL_ROWS // DN_CHUNK
    wide, half, col, row, egl = _dn_specs(t)

    ng = LOCAL_ROWS // DN_GROUP
    per = DN_GROUP // DN_CHUNK
    grow = pl.BlockSpec((1, ng, 1, DN_GROUP), lambda hd, i: (hd, i, 0, 0))

    def body(q_ref, k_ref, v_ref, b_ref, gc_ref, gg_ref, u_ref, w_ref, qk_ref, qd_ref, kd_ref, egl_ref, t_ref):
        sgn = _hd_sign(pl.program_id(0))
        last = jnp.where(sgn > 0, DN_CHUNK - 1, 0)
        onehot = (lax.broadcasted_iota(jnp.int32, (DN_CHUNK, 1), 0) == last).astype(F32)
        groups = lambda a: a.reshape((ng, DN_GROUP) + a.shape[1:])
        q_all, k_all, gcc_all = q_ref[...], k_ref[...], gc_ref[0][:, :1]
        u, w, qk, t_inv = jax.vmap(functools.partial(_dn_group, sgn=sgn))(
            groups(q_all), groups(k_all), groups(v_ref[...]), groups(b_ref[0][:, :1]), groups(gcc_all), gg_ref[0])
        u_ref[0] = u.reshape(LOCAL_ROWS, LANES)
        w_ref[0] = w.reshape(LOCAL_ROWS, LANES).astype(_MXU)
        qd_ref[0] = (q_all * jnp.exp(gcc_all)).astype(_MXU)
        for gi in range(ng):
            for c in range(per):
                blk = slice(c * DN_CHUNK, (c + 1) * DN_CHUNK)
                rows = slice(gi * DN_GROUP + c * DN_CHUNK, gi * DN_GROUP + (c + 1) * DN_CHUNK)
                qk_ref[0, rows, :] = qk[gi, blk, blk].astype(_MXU)
                t_ref[0, rows, :] = t_inv[gi, blk, blk]
                gl = jnp.sum(gcc_all[rows, :] * onehot, axis=0, keepdims=True)
                kd_ref[0, rows, :] = (k_all[rows, :] * jnp.exp(gl - gcc_all[rows, :])).astype(_MXU)
                egl_ref[0, gi * per + c] = jnp.broadcast_to(jnp.exp(gl), (1, LANES))

    big = _sds((N_HD, t, LANES))
    small = _sds((N_HD, t, DN_CHUNK))
    operands = [q, k, v, beta_c, gc_c, gc_r.reshape(N_HD, t // DN_GROUP, 1, DN_GROUP)]
    return _maybe_carrying(
        carry, body, "dn_local_fwd", (N_HD, t // LOCAL_ROWS), operands, _qkv_specs() + [col, col, grow],
        [wide, wide, half, wide, wide, egl, half],
        [big, _sds(big.shape, _MXU), _sds(small.shape, _MXU), _sds(big.shape, _MXU), _sds(big.shape, _MXU),
         _sds((N_HD, nc, 1, LANES)), small])


def _dn_local_bwd(q, k, v, beta_c, gc_c, gc_r, t_inv, du, dw, dqk, dqd, dkd, degl, carry=None):
    t = q.shape[0]
    nc = t // DN_CHUNK
    nl = LOCAL_ROWS // DN_CHUNK
    wide, half, col, row, egl = _dn_specs(t)
    dspec = pl.BlockSpec((1, LOCAL_ROWS, LANES), lambda hd, i: (_dir_of(hd), i, _head_of(hd)))

    def body(q_ref, k_ref, v_ref, b_ref, gc_ref, gr_ref, t_ref, du_ref, dw_ref, dqk_ref, dqd_ref, dkd_ref, degl_ref,
             dq_ref, dk_ref, dv_ref, db_ref, dgc_ref, dgr_ref):
        sgn = _hd_sign(pl.program_id(0))

        def chunk_bwd(tinv, q, k, v, beta, gcc, gcr, du, dw, dqk, dqd, dkd, degl):
            _, vjp = jax.vjp(functools.partial(_dn_local, sgn=sgn), tinv, q, k, v, beta, gcc, gcr)
            dt, dq, dk, dv, db, dgc, dgr = vjp((du, dw, dqk, dqd, dkd, degl))
            _, vjp_a = jax.vjp(functools.partial(_dn_a, sgn=sgn), k, beta, gcc, gcr)
            dk2, db2, dgc2, dgr2 = vjp_a(_inv_unit_t(tinv, dt))
            return dq, dk + dk2, dv, db + db2, dgc + dgc2, dgr + dgr2

        chunks = lambda a: a.reshape((nl, DN_CHUNK) + a.shape[1:])
        dq, dk, dv, db, dgc, dgr = jax.vmap(chunk_bwd)(
            chunks(t_ref[0]), chunks(q_ref[...]), chunks(k_ref[...]), chunks(v_ref[...]), chunks(b_ref[0][:, :1]),
            chunks(gc_ref[0][:, :1]),
            gr_ref[0], chunks(du_ref[0]), chunks(dw_ref[0]), chunks(dqk_ref[0]), chunks(dqd_ref[0]), chunks(dkd_ref[0]),
            degl_ref[0])
        dq_ref[0] = dq.reshape(LOCAL_ROWS, LANES)
        dk_ref[0] = dk.reshape(LOCAL_ROWS, LANES)
        dv_ref[0] = dv.reshape(LOCAL_ROWS, LANES)
        db_ref[0] = jnp.broadcast_to(db.reshape(LOCAL_ROWS, 1), (LOCAL_ROWS, LANES))
        dgc_ref[0] = jnp.broadcast_to(dgc.reshape(LOCAL_ROWS, 1), (LOCAL_ROWS, LANES))
        dgr_ref[0] = dgr

    per_dir = _sds((2, t, DN_WIDTH))
    return _maybe_carrying(
        carry, body, "dn_local_bwd", (N_HD, t // LOCAL_ROWS), [q, k, v, beta_c, gc_c, gc_r, t_inv, du, dw, dqk, dqd, dkd, degl],
        _qkv_specs() + [col, col, row, half, wide, wide, half, wide, wide, egl], [dspec, dspec, dspec, col, col, row],
        [per_dir, per_dir, per_dir, _sds((N_HD, t, LANES)), _sds((N_HD, t, LANES)), _sds((N_HD, nc, 1, DN_CHUNK))])


REC_HEADS = 8
REC_GROUPS = N_HD // REC_HEADS
REC_FWD_GROUPS = DN_HEADS // REC_HEADS


def _rec_specs(time_block):
    nr = REC_ROWS // DN_CHUNK
    wide = pl.BlockSpec((REC_HEADS, REC_ROWS, LANES), lambda g, b: (g, time_block(g, b), 0))
    half = pl.BlockSpec((REC_HEADS, REC_ROWS, DN_CHUNK), lambda g, b: (g, time_block(g, b), 0))
    egl = pl.BlockSpec((REC_HEADS, nr, 1, LANES), lambda g, b: (g, time_block(g, b), 0, 0))
    state = pl.BlockSpec((REC_HEADS, nr, DN_HEAD_DIM, DN_HEAD_DIM), lambda g, b: (g, time_block(g, b), 0, 0))
    return wide, half, egl, state


def _rec_head_cols(g):
    return jnp.where(g < REC_FWD_GROUPS, g, g - REC_FWD_GROUPS)


def _dn_rec_fwd(u, w, qk, qd, kd, egl):
    t = u.shape[1]
    nb = t // REC_ROWS
    nr = REC_ROWS // DN_CHUNK
    nc = t // DN_CHUNK

    def time_block(g, b):
        return jnp.where(g < REC_FWD_GROUPS, b, nb - 1 - b)

    wide, half, egl_spec, state = _rec_specs(time_block)
    o_spec = pl.BlockSpec((1, REC_ROWS, REC_HEADS * LANES),
                          lambda g, b: (jnp.where(g < REC_FWD_GROUPS, 0, 1), time_block(g, b), _rec_head_cols(g)))

    def body(u_ref, w_ref, qk_ref, qd_ref, kd_ref, egl_ref, o_ref, vn_ref, s_ref, s_scr):
        fwd = pl.program_id(0) < REC_FWD_GROUPS

        @pl.when(pl.program_id(1) == 0)
        def _():
            s_scr[...] = jnp.zeros_like(s_scr)

        def run(order):
            heads = range(REC_HEADS)
            s = [s_scr[j] for j in heads]
            for ce in order:
                rows = slice(ce * DN_CHUNK, (ce + 1) * DN_CHUNK)
                vn = [u_ref[j, rows, :] - _bnn(w_ref[j, rows, :], s[j]) for j in heads]
                o = [_bnn(qd_ref[j, rows, :], s[j]) + _bnn(qk_ref[j, rows, :], vn[j]) for j in heads]
                nxt = [s[j] * egl_ref[j, ce] + _btn(kd_ref[j, rows, :], vn[j]) for j in heads]
                for j in heads:
                    s_ref[j, ce] = s[j]
                    vn_ref[j, rows, :] = vn[j].astype(_MXU)
                    o_ref[0, rows, j * LANES:(j + 1) * LANES] = o[j]
                s = nxt
            for j in heads:
                s_scr[j] = s[j]

        pl.when(fwd)(lambda: run(range(nr)))
        pl.when(jnp.logical_not(fwd))(lambda: run(range(nr - 1, -1, -1)))

    return pl.pallas_call(
        body, name="dn_rec_fwd", grid=(REC_GROUPS, nb), in_specs=[wide, wide, half, wide, wide, egl_spec],
        out_specs=[o_spec, wide, state],
        out_shape=[_sds((2, t, DN_WIDTH)), _sds((N_HD, t, LANES), _MXU), _sds((N_HD, nc, DN_HEAD_DIM, DN_HEAD_DIM))],
        scratch_shapes=[pltpu.VMEM((REC_HEADS, DN_HEAD_DIM, DN_HEAD_DIM), F32)],
        compiler_params=_params(("parallel", "arbitrary")),
    )(u, w, qk, qd, kd, egl)


def _dn_rec_bwd(w, qk, qd, kd, egl, vn, states, do):
    t = w.shape[1]
    nb = t // REC_ROWS
    nr = REC_ROWS // DN_CHUNK
    nc = t // DN_CHUNK

    def time_block(g, b):
        return jnp.where(g < REC_FWD_GROUPS, nb - 1 - b, b)

    wide, half, egl_spec, state = _rec_specs(time_block)
    do_spec = pl.BlockSpec((REC_ROWS, REC_HEADS * LANES), lambda g, b: (time_block(g, b), _rec_head_cols(g)))

    def body(w_ref, qk_ref, qd_ref, kd_ref, egl_ref, vn_ref, s_ref, do_ref,
             du_ref, dw_ref, dqk_ref, dqd_ref, dkd_ref, degl_ref, ds_scr):
        fwd = pl.program_id(0) < REC_FWD_GROUPS

        @pl.when(pl.program_id(1) == 0)
        def _():
            ds_scr[...] = jnp.zeros_like(ds_scr)

        def run(order):
            heads = range(REC_HEADS)
            ds = [ds_scr[j] for j in heads]
            for ce in order:
                rows = slice(ce * DN_CHUNK, (ce + 1) * DN_CHUNK)
                s = [s_ref[j, ce] for j in heads]
                do_c = [do_ref[rows, j * LANES:(j + 1) * LANES] for j in heads]
                vn_c = [vn_ref[j, rows, :] for j in heads]
                dvn = [_btn(qk_ref[j, rows, :], do_c[j]) + _bnn(kd_ref[j, rows, :], ds[j]) for j in heads]
                nxt = [ds[j] * egl_ref[j, ce] + _btn(qd_ref[j, rows, :], do_c[j]) - _btn(w_ref[j, rows, :], dvn[j])
                       for j in heads]
                for j in heads:
                    du_ref[j, rows, :] = dvn[j]
                    dw_ref[j, rows, :] = -_bnt(dvn[j], s[j])
                for j in heads:
                    dqk_ref[j, rows, :] = _bnt(do_c[j], vn_c[j])
                    dqd_ref[j, rows, :] = _bnt(do_c[j], s[j])
                for j in heads:
                    dkd_ref[j, rows, :] = _bnt(vn_c[j], ds[j])
                    degl_ref[j, ce] = jnp.sum(s[j] * ds[j], axis=0, keepdims=True)
                ds = nxt
            for j in heads:
                ds_scr[j] = ds[j]

        pl.when(fwd)(lambda: run(range(nr - 1, -1, -1)))
        pl.when(jnp.logical_not(fwd))(lambda: run(range(nr)))

    big = _sds((N_HD, t, LANES))
    return pl.pallas_call(
        body, name="dn_rec_bwd", grid=(REC_GROUPS, nb), in_specs=[wide, half, wide, wide, egl_spec, wide, state, do_spec],
        out_specs=[wide, wide, half, wide, wide, egl_spec],
        out_shape=[big, big, _sds((N_HD, t, DN_CHUNK)), big, big, _sds((N_HD, nc, 1, LANES))],
        scratch_shapes=[pltpu.VMEM((REC_HEADS, DN_HEAD_DIM, DN_HEAD_DIM), F32)],
        compiler_params=_params(("parallel", "arbitrary")),
    )(w, qk, qd, kd, egl, vn, states, do)


def _post_fn(of, ob, z, gain):
    o = of + ob
    return o * lax.rsqrt(jnp.mean(o * o, axis=-1, keepdims=True) + RMS_EPS) * gain * _silu(z)


def _post_specs():
    o_spec = [pl.BlockSpec((1, ROW_TILE, DN_WIDTH), functools.partial(lambda i, d: (d, i, 0), d=d)) for d in (0, 1)]
    return o_spec, _cols(DN_WIDTH, C_Z), _rows(DN_WIDTH), _full((1, LANES))


def _post_fwd(o2, proj, gain):
    t = proj.shape[0]
    o_spec, z_spec, wide, gain_spec = _post_specs()

    def body(of_ref, ob_ref, z_ref, g_ref, out_ref):
        for h in range(DN_HEADS):
            cols = slice(h * LANES, (h + 1) * LANES)
            out_ref[:, cols] = _post_fn(of_ref[0, :, cols], ob_ref[0, :, cols], z_ref[:, cols], g_ref[...]).astype(_MXU)

    return pl.pallas_call(
        body, name="post_fwd", grid=(t // ROW_TILE,), in_specs=o_spec + [z_spec, gain_spec], out_specs=wide,
        out_shape=_sds((t, DN_WIDTH), _MXU), compiler_params=_params(("parallel",)),
    )(o2, o2, proj, gain)


def _post_bwd(o2, proj, gain, dout):
    t = proj.shape[0]
    o_spec, z_spec, wide, gain_spec = _post_specs()

    def body(of_ref, ob_ref, z_ref, g_ref, d_ref, do_ref, dz_ref, dg_ref):
        dg_sum = jnp.zeros((1, LANES), F32)
        for h in range(DN_HEADS):
            cols = slice(h * LANES, (h + 1) * LANES)
            _, vjp = jax.vjp(_post_fn, of_ref[0, :, cols], ob_ref[0, :, cols], z_ref[:, cols], g_ref[...])
            do, _, dz, dg = vjp(d_ref[:, cols])
            do_ref[:, cols] = do
            dz_ref[:, cols] = dz.astype(_MXU)
            dg_sum = dg_sum + dg
        _accumulate(dg_ref, dg_sum, pl.program_id(0))

    return pl.pallas_call(
        body, name="post_bwd", grid=(t // ROW_TILE,), in_specs=o_spec + [z_spec, gain_spec, wide],
        out_specs=[wide, wide, gain_spec], out_shape=[_sds((t, DN_WIDTH)), _sds((t, DN_WIDTH), _MXU), _sds((1, LANES))],
        compiler_params=_params(("arbitrary",)),
    )(o2, o2, proj, gain, dout)


def _rope(x, cos, sin):
    lane = lax.broadcasted_iota(jnp.int32, x.shape, 1)
    first = (lane & (SW_HEAD_DIM - 1)) < SW_HEAD_DIM // 2
    rot = jnp.where(first, -pltpu.roll(x, LANES - SW_HEAD_DIM // 2, 1), pltpu.roll(x, SW_HEAD_DIM // 2, 1))
    return x * cos + rot * sin


def _rope_apply(q, k, q_cols, k_cols, cos, sin, name, dtype):
    t = cos.shape[0]
    qw, kw = SW_HEADS * SW_HEAD_DIM, SW_KV_WIDTH

    def body(q_ref, k_ref, c_ref, s_ref, qo_ref, ko_ref):
        c, s = c_ref[...], s_ref[...]
        for j in range(qw // LANES):
            cols = slice(j * LANES, (j + 1) * LANES)
            qo_ref[:, cols] = _rope(q_ref[:, cols], c, s).astype(dtype)
        for j in range(kw // LANES):
            cols = slice(j * LANES, (j + 1) * LANES)
            ko_ref[:, cols] = _rope(k_ref[:, cols], c, s).astype(dtype)

    return pl.pallas_call(
        body, name=name, grid=(t // ROW_TILE,), in_specs=[_cols(qw, q_cols), _cols(kw, k_cols), _rows(LANES), _rows(LANES)],
        out_specs=[_rows(qw), _rows(kw)], out_shape=[_sds((t, qw), dtype), _sds((t, kw), dtype)],
        compiler_params=_params(("parallel",)),
    )(q, k, cos, sin)


def _attn_core(qs, kb, vb, sink, mask):
    s = _bnt(qs, kb) * (SW_HEAD_DIM ** -0.5)
    s = jnp.where(mask, s, -1e30)
    m = lax.stop_gradient(jnp.maximum(jnp.max(s, axis=1, keepdims=True), sink))
    e = jnp.exp(s - m)
    den = jnp.sum(e, axis=1, keepdims=True) + jnp.exp(sink - m)
    return _bnn(e / den, vb)


def _band_mask(n, nb):
    rows = SW_GROUP * SW_BLOCK
    i = lax.broadcasted_iota(jnp.int32, (rows, 3 * SW_BLOCK), 0) & (SW_BLOCK - 1)
    j = lax.broadcasted_iota(jnp.int32, (rows, 3 * SW_BLOCK), 1)
    near = (j - i >= 0) & (j - i <= 2 * SW_BLOCK)
    lo = jnp.where(n == 0, SW_BLOCK, 0)
    hi = jnp.where(n == nb - 1, 2 * SW_BLOCK, 3 * SW_BLOCK)
    return near & (j >= lo) & (j < hi)


def _band_specs(nb, v_cols):
    def spec(width, base, shift):
        return pl.BlockSpec((SW_BLOCK, width), lambda n: (jnp.clip(n + shift, 0, nb - 1), base // width))
    k_specs = [spec(SW_KV_WIDTH, 0, s) for s in (-1, 0, 1)]
    v_specs = [spec(SW_KV_WIDTH, v_cols, s) for s in (-1, 0, 1)]
    return k_specs, v_specs


def _head_cols(kv, g):
    h = kv * SW_GROUP + g
    return slice(h * SW_HEAD_DIM, (h + 1) * SW_HEAD_DIM)


def _kv_batches(q_ref, kb, vb, s_ref):
    kvs = range(SW_KV_HEADS)
    cols = lambda kv: slice(kv * SW_HEAD_DIM, (kv + 1) * SW_HEAD_DIM)
    qs = jnp.stack([jnp.concatenate([q_ref[:, _head_cols(kv, g)] for g in range(SW_GROUP)], axis=0) for kv in kvs])
    sinks = jnp.stack([jnp.concatenate([jnp.broadcast_to(s_ref[kv * SW_GROUP + g], (SW_BLOCK, 1)) for g in range(SW_GROUP)],
                                       axis=0) for kv in kvs])
    return qs, jnp.stack([kb[:, cols(kv)] for kv in kvs]), jnp.stack([vb[:, cols(kv)] for kv in kvs]), sinks


def _attn_fwd(qr, kr, proj, sinks):
    t = qr.shape[0]
    nb = t // SW_BLOCK
    qw = SW_HEADS * SW_HEAD_DIM
    k_specs, v_specs = _band_specs(nb, C_VSW)
    q_spec = pl.BlockSpec((SW_BLOCK, qw), lambda n: (n, 0))

    def body(q_ref, k0, k1, k2, v0, v1, v2, s_ref, o_ref):
        mask = _band_mask(pl.program_id(0), nb)
        kb = jnp.concatenate([k0[...], k1[...], k2[...]], axis=0)
        vb = jnp.concatenate([v0[...], v1[...], v2[...]], axis=0)
        qs, kbs, vbs, sinks_ = _kv_batches(q_ref, kb, vb, s_ref)
        o = jax.vmap(functools.partial(_attn_core, mask=mask))(qs, kbs, vbs, sinks_)
        for kv in range(SW_KV_HEADS):
            for g in range(SW_GROUP):
                o_ref[:, _head_cols(kv, g)] = o[kv, g * SW_BLOCK:(g + 1) * SW_BLOCK, :].astype(_MXU)

    return pl.pallas_call(
        body, name="attn_fwd", grid=(nb,), in_specs=[q_spec] + k_specs + v_specs + [_full((SW_HEADS, 1, 1))],
        out_specs=q_spec, out_shape=_sds((t, qw), _MXU), compiler_params=_params(("parallel",)),
    )(qr, kr, kr, kr, proj, proj, proj, sinks)


def _attn_bwd(qr, kr, proj, sinks, do):
    t = qr.shape[0]
    nb = t // SW_BLOCK
    qw = SW_HEADS * SW_HEAD_DIM
    k_specs, v_specs = _band_specs(nb, C_VSW)
    q_spec = pl.BlockSpec((SW_BLOCK, qw), lambda n: (n, 0))
    part = pl.BlockSpec((1, 3 * SW_BLOCK, SW_KV_WIDTH), lambda n: (n, 0, 0))

    def body(q_ref, k0, k1, k2, v0, v1, v2, s_ref, do_ref, dq_ref, dk_ref, dv_ref, ds_ref):
        mask = _band_mask(pl.program_id(0), nb)
        kb = jnp.concatenate([k0[...], k1[...], k2[...]], axis=0).astype(F32)
        vb = jnp.concatenate([v0[...], v1[...], v2[...]], axis=0)

        @pl.when(pl.program_id(0) == 0)
        def _():
            ds_ref[...] = jnp.zeros_like(ds_ref)

        qs, kbs, vbs, sinks_ = _kv_batches(q_ref, kb, vb, s_ref)
        dos = jnp.stack([jnp.concatenate([do_ref[:, _head_cols(kv, g)] for g in range(SW_GROUP)], axis=0)
                         for kv in range(SW_KV_HEADS)])

        def head_bwd(q_, k_, v_, sink_, do_):
            _, vjp = jax.vjp(functools.partial(_attn_core, mask=mask), q_, k_, v_, sink_)
            return vjp(do_)

        dqs, dkb, dvb, dsink = jax.vmap(head_bwd)(qs.astype(F32), kbs, vbs, sinks_, dos)
        for kv in range(SW_KV_HEADS):
            kvc = slice(kv * SW_HEAD_DIM, (kv + 1) * SW_HEAD_DIM)
            dk_ref[0, :, kvc] = dkb[kv]
            dv_ref[0, :, kvc] = dvb[kv]
            for g in range(SW_GROUP):
                rows = slice(g * SW_BLOCK, (g + 1) * SW_BLOCK)
                dq_ref[:, _head_cols(kv, g)] = dqs[kv, rows, :]
                ds_ref[kv * SW_GROUP + g] += jnp.sum(dsink[kv, rows, :], axis=0, keepdims=True)

    parts = _sds((nb, 3 * SW_BLOCK, SW_KV_WIDTH))
    return pl.pallas_call(
        body, name="attn_bwd", grid=(nb,), in_specs=[q_spec] + k_specs + v_specs + [_full((SW_HEADS, 1, 1)), q_spec],
        out_specs=[q_spec, part, part, _full((SW_HEADS, 1, 1))], out_shape=[_sds((t, qw)), parts, parts, _sds((SW_HEADS, 1, 1))],
        compiler_params=_params(("arbitrary",)),
    )(qr, kr, kr, kr, proj, proj, proj, sinks, do)


def _band_sum(parts, name, dtype):
    nb = parts.shape[0]
    w = parts.shape[2]

    def spec(shift, slot):
        return pl.BlockSpec((1, SW_BLOCK, w), lambda m: (jnp.clip(m + shift, 0, nb - 1), slot, 0))

    def body(prev_ref, own_ref, next_ref, o_ref):
        m = pl.program_id(0)
        total = own_ref[0] + jnp.where(m > 0, prev_ref[0], 0.0) + jnp.where(m < nb - 1, next_ref[0], 0.0)
        o_ref[...] = total.astype(dtype)

    return pl.pallas_call(
        body, name=name, grid=(nb,), in_specs=[spec(-1, 2), spec(0, 1), spec(1, 0)],
        out_specs=pl.BlockSpec((SW_BLOCK, w), lambda m: (m, 0)), out_shape=_sds((nb * SW_BLOCK, w), dtype),
        compiler_params=_params(("parallel",)),
    )(parts, parts, parts)


def _gate_rows(gbo):
    t = gbo.shape[0]
    return gbo[:, N_HD:2 * N_HD].T.reshape(N_HD, t // DN_CHUNK, 1, DN_CHUNK)


def _gate_rows_t(dgc_r):
    t = dgc_r.shape[1] * DN_CHUNK
    return jnp.pad(dgc_r.reshape(N_HD, t).T, ((0, 0), (N_HD, LANES - 2 * N_HD)))


def _layer_fwd(x, xb, w, cos, sin, carry=None):
    proj = _mm(xb, w["wm"], "nn", name="proj")
    ba = _mm(xb, w["wba"], "nn", name="proj_gates")
    qn, kn, vv = [_prep_fwd(proj, w["conv"], kind) for kind in range(3)]
    gbo, beta_c, gc_c = _gb_fwd(ba, w["alog"], w["dtb"])
    gc_r = _gate_rows(gbo)
    u, wk, qk, qd, kd, egl, tinv, *carried = _dn_local_fwd(qn, kn, vv, beta_c, gc_c, gc_r, carry=carry)
    o2, vn, states = _dn_rec_fwd(u, wk, qk, qd, kd, egl)
    o_dn = _post_fwd(o2, proj, w["dnw"])
    qr, kr = _rope_apply(proj, proj, C_QSW, C_KSW, cos, sin, "rope_fwd", _MXU)
    o_sw = _attn_fwd(qr, kr, proj, w["sinks"])
    ya = _mm(o_dn, w["wa"], "nn", name="branch_a")
    yb = _mm(o_sw, w["wb"], "nn", name="branch_b")
    merged = _merge_fwd(proj, ya, yb)
    mix = _mm(merged, w["wo"], "nn", name="mix_out")
    x1, x1b = _ln_fwd(x, mix, w["ln1g"], w["ln1b"], "ln1_fwd")
    gu = _mm(x1b, w["wgu"], "nn", name="ffn_up")
    h = _swiglu_fwd(gu)
    f = _mm(h, w["wd"], "nn", name="ffn_down")
    x2, x2b = _ln_fwd(x1, f, w["ln2g"], w["ln2b"], "ln2_fwd")
    res = dict(x=x, xb=xb, proj=proj, ba=ba, qn=qn, kn=kn, vv=vv, beta_c=beta_c, gc_c=gc_c, gc_r=gc_r, wk=wk, qk=qk, qd=qd, kd=kd, egl=egl, tinv=tinv, vn=vn,
               states=states, o2=o2, o_dn=o_dn, qr=qr, kr=kr, o_sw=o_sw, ya=ya, yb=yb, merged=merged, mix=mix, x1=x1, x1b=x1b,
               gu=gu, h=h, f=f)
    return x2, x2b, res, (carried or None)


def _layer_bwd(dx2, w, r, cos, sin, carry=None):
    dx1, df, dln2g, dln2b = _ln_bwd(r["x1"], r["f"], w["ln2g"], w["ln2b"], dx2, "ln2_bwd")
    dh = _mm(df, w["wd"], "nt", name="d_ffn_hidden")
    dwd = _mm(r["h"], df, "tn", name="dw_ffn_down", out_dtype=_MXU)
    dgu = _swiglu_bwd(r["gu"], dh)
    dwgu = _mm(r["x1b"], dgu, "tn", name="dw_ffn_up", out_dtype=_MXU)
    dx1 = _mm(dgu, w["wgu"], "nt", name="dx_ffn", add=dx1)
    dx, dmix, dln1g, dln1b = _ln_bwd(r["x"], r["mix"], w["ln1g"], w["ln1b"], dx1, "ln1_bwd")
    dmerged = _mm(dmix, w["wo"], "nt", name="d_merged")
    dwo = _mm(r["merged"], dmix, "tn", name="dw_mix_out", out_dtype=_MXU)
    dga, dgb, dya, dyb = _merge_bwd(r["proj"], r["ya"], r["yb"], dmerged)
    dwa = _mm(r["o_dn"], dya, "tn", name="dw_branch_a", out_dtype=_MXU)
    do_dn = _mm(dya, w["wa"], "nt", name="d_branch_a")
    dwb = _mm(r["o_sw"], dyb, "tn", name="dw_branch_b", out_dtype=_MXU)
    do_sw = _mm(dyb, w["wb"], "nt", name="d_branch_b")
    do, dz, ddnw = _post_bwd(r["o2"], r["proj"], w["dnw"], do_dn)
    du, dwk, dqk, dqd, dkd, degl = _dn_rec_bwd(r["wk"], r["qk"], r["qd"], r["kd"], r["egl"], r["vn"], r["states"], do)
    dq3, dk3, dv3, dbeta_c, dgc_c, dgc_r, *carried = _dn_local_bwd(r["qn"], r["kn"], r["vv"], r["beta_c"], r["gc_c"], r["gc_r"],
                                                                   r["tinv"], du, dwk, dqk, dqd, dkd, degl, carry=carry)
    dqkv, dconv = zip(*[_prep_bwd(r["proj"], w["conv"], d2, kind) for kind, d2 in enumerate((dq3, dk3, dv3))])
    dconv = jnp.concatenate(dconv, axis=1)
    dba, dalog, ddtb = _gb_bwd(r["ba"], w["alog"], w["dtb"], dbeta_c, dgc_c, _gate_rows_t(dgc_r))
    dqr, dkparts, dvparts, dsinks = _attn_bwd(r["qr"], r["kr"], r["proj"], w["sinks"], do_sw)
    dkr = _band_sum(dkparts, "attn_dk_sum", F32)
    dv = _band_sum(dvparts, "attn_dv_sum", _MXU)
    dq_sw, dk_sw = _rope_apply(dqr, dkr, 0, 0, cos, -sin, "rope_bwd", _MXU)
    dproj = jnp.concatenate([*dqkv, dz, dq_sw, dga, dgb, dk_sw, dv], axis=1)
    dwm = _mm(r["xb"], dproj, "tn", name="dw_proj", out_dtype=_MXU)
    dwba = _mm(r["xb"], dba, "tn", name="dw_proj_gates", out_dtype=_MXU)
    dx = _mm(dproj, w["wm"], "nt", name="dx_proj", add=dx)
    dx = _mm(dba, w["wba"], "nt", name="dx_proj_gates", add=dx)
    grads = dict(wm=dwm, wba=dwba, conv=dconv, alog=dalog, dtb=ddtb, dnw=ddnw, sinks=dsinks, wa=dwa, wb=dwb, wo=dwo,
                 ln1g=dln1g, ln1b=dln1b, wgu=dwgu, wd=dwd, ln2g=dln2g, ln2b=dln2b)
    return dx, grads, (carried or None)


def _rope_tables(t):
    half = SW_HEAD_DIM // 2
    inv_freq = ROPE_THETA ** (-jnp.arange(half, dtype=F32) / half)
    ang = jnp.arange(t, dtype=F32)[:, None] * inv_freq[None, :]
    return jnp.tile(jnp.cos(ang), (1, LANES // half)), jnp.tile(jnp.sin(ang), (1, LANES // half))


def _trunk(x, target, n_layers, layer_weights, fwd_carry, grads_done):
    cos, sin = _rope_tables(x.shape[0])
    xb = x.astype(_MXU)
    saved, weights, carried = [], [], None
    for i in range(n_layers):
        w = layer_weights(i, carried)
        x, xb, res, carried = _layer_fwd(x, xb, w, cos, sin, carry=fwd_carry(i))
        saved.append(res)
        weights.append(w)
    sq, dx = _loss_head(x, target)
    carry = None
    for i in reversed(range(n_layers)):
        dx, grads, carried = _layer_bwd(dx, weights[i], saved[i], cos, sin, carry=carry)
        carry = grads_done(i, grads, carried)
    return sq, dx, carry


N_CHIPS = 4


def _mesh_pos():
    return lax.axis_index("x"), lax.axis_index("y"), lax.axis_index("c")


def _other_chips(x, y):
    return [(1 - x, y), (x, 1 - y), (1 - x, 1 - y)]


def _remote_copy(src, dst, sems, k, to):
    send_sems, recv_sems, base = sems
    return pltpu.make_async_remote_copy(src_ref=src, dst_ref=dst, send_sem=send_sems.at[base + k],
                                        recv_sem=recv_sems.at[base + k], device_id=to, device_id_type=pl.DeviceIdType.MESH)


def _exchange_sems(n_arrays, per_array):
    return [pltpu.SemaphoreType.DMA((n_arrays * per_array,)), pltpu.SemaphoreType.DMA((n_arrays * per_array,)),
            pltpu.SemaphoreType.DMA((n_arrays,))]


def _comm_call(body, name, out_shapes, per_array, operands):
    n = len(operands)
    hbm = pl.BlockSpec(memory_space=pl.ANY)

    def flat_body(*refs):
        body(refs[:n], refs[n:2 * n], *refs[2 * n:])

    return pl.pallas_call(
        flat_body, name=name, in_specs=[hbm] * n, out_specs=[hbm] * n, out_shape=list(out_shapes),
        scratch_shapes=_exchange_sems(n, per_array), compiler_params=pltpu.CompilerParams(has_side_effects=True),
    )(*operands)


class _Gather:
    n_sems = N_DEV - 1

    @staticmethod
    def out_shape(block):
        return _sds((N_DEV,) + block.shape, block.dtype)

    @staticmethod
    def _own(x_ref, o_ref, sems, local_sem):
        x, y, c = _mesh_pos()
        mine = o_ref.at[4 * x + 2 * y + c]
        first = [_remote_copy(x_ref, mine, sems, 0, (x, y, 1 - c))]
        first += [_remote_copy(x_ref, mine, sems, 1 + j, (*chip, c)) for j, chip in enumerate(_other_chips(x, y))]
        return pltpu.make_async_copy(x_ref, mine, local_sem), first

    @classmethod
    def start(cls, x_ref, o_ref, sems, local_sem):
        mine, first = cls._own(x_ref, o_ref, sems, local_sem)
        mine.start()
        for cp in first:
            cp.start()

    @classmethod
    def finish(cls, x_ref, o_ref, sems, local_sem):
        x, y, c = _mesh_pos()
        sibling = (x, y, 1 - c)
        chips = _other_chips(x, y)
        slot = lambda px, py, pc: o_ref.at[4 * px + 2 * py + pc]
        mine, first = cls._own(x_ref, o_ref, sems, local_sem)
        passed = [_remote_copy(slot(*chip, c), slot(*chip, c), sems, 4 + j, sibling) for j, chip in enumerate(chips)]
        for j, chip in enumerate(chips):
            _remote_copy(x_ref, slot(*chip, c), sems, 1 + j, sibling).wait_recv()
            passed[j].start()
        _remote_copy(x_ref, slot(x, y, 1 - c), sems, 0, sibling).wait_recv()
        for j, chip in enumerate(chips):
            _remote_copy(x_ref, slot(*chip, 1 - c), sems, 4 + j, sibling).wait_recv()
        for cp in first + passed:
            cp.wait_send()
        mine.wait()


class _ChipExchange:
    n_sems = N_CHIPS - 1

    @staticmethod
    def out_shape(parts):
        return _sds(parts.shape, parts.dtype)

    @staticmethod
    def _own(x_ref, o_ref, sems, local_sem):
        x, y, c = _mesh_pos()
        me = 2 * x + y
        sent = [_remote_copy(x_ref.at[2 * cx + cy], o_ref.at[me], sems, j, (cx, cy, c))
                for j, (cx, cy) in enumerate(_other_chips(x, y))]
        return pltpu.make_async_copy(x_ref.at[me], o_ref.at[me], local_sem), sent

    @classmethod
    def start(cls, x_ref, o_ref, sems, local_sem):
        mine, sent = cls._own(x_ref, o_ref, sems, local_sem)
        mine.start()
        for cp in sent:
            cp.start()

    @classmethod
    def finish(cls, x_ref, o_ref, sems, local_sem):
        x, y, c = _mesh_pos()
        mine, sent = cls._own(x_ref, o_ref, sems, local_sem)
        for j, (cx, cy) in enumerate(_other_chips(x, y)):
            _remote_copy(x_ref.at[2 * x + y], o_ref.at[2 * cx + cy], sems, j, (cx, cy, c)).wait_recv()
        for cp in sent:
            cp.wait_send()
        mine.wait()


def _run_exchange(kind, phase, x_refs, o_refs, send_sems, recv_sems, local_sems):
    for i, (x_ref, o_ref) in enumerate(zip(x_refs, o_refs)):
        getattr(kind, phase)(x_ref, o_ref, (send_sems, recv_sems, i * kind.n_sems), local_sems.at[i])


def _exchange_alone(kind, operands, name):
    def body(x_refs, o_refs, *sems):
        _run_exchange(kind, "start", x_refs, o_refs, *sems)
        _run_exchange(kind, "finish", x_refs, o_refs, *sems)

    return _comm_call(body, name, [kind.out_shape(a) for a in operands], kind.n_sems, operands)


def _all_gather(block, name):
    return _exchange_alone(_Gather, [block], name)[0]


def _carried(kind, operands, body, n_in, n_out, grid):
    hbm = pl.BlockSpec(memory_space=pl.ANY)
    n_x = len(operands)

    def wrapped(*refs):
        ins, x_refs = refs[:n_in], refs[n_in:n_in + n_x]
        outs = refs[n_in + n_x:n_in + n_x + n_out]
        o_refs = refs[n_in + n_x + n_out:n_in + 2 * n_x + n_out]
        sems = refs[n_in + 2 * n_x + n_out:n_in + 2 * n_x + n_out + 3]
        rest = refs[n_in + 2 * n_x + n_out + 3:]
        first, last = None, None
        for axis, size in enumerate(grid):
            at0, at1 = pl.program_id(axis) == 0, pl.program_id(axis) == size - 1
            first = at0 if first is None else first & at0
            last = at1 if last is None else last & at1
        pl.when(first)(lambda: _run_exchange(kind, "start", x_refs, o_refs, *sems))
        body(*ins, *outs, *rest)
        pl.when(last)(lambda: _run_exchange(kind, "finish", x_refs, o_refs, *sems))

    return wrapped, [hbm] * n_x, [hbm] * n_x, [kind.out_shape(a) for a in operands], _exchange_sems(n_x, kind.n_sems)


def _sibling_swap(parts, name):
    def body(x_refs, o_refs, send_sems, recv_sems, local_sems):
        x, y, c = _mesh_pos()
        copies = [_remote_copy(x_ref.at[2 * q + (1 - c)], o_ref.at[q], (send_sems, recv_sems, i * N_CHIPS), q, (x, y, 1 - c))
                  for i, (x_ref, o_ref) in enumerate(zip(x_refs, o_refs)) for q in range(N_CHIPS)]
        for cp in copies:
            cp.start()
        for cp in copies:
            cp.wait()

    return _comm_call(body, name, [_sds((N_CHIPS,) + p.shape[1:], p.dtype) for p in parts], N_CHIPS, parts)


def _pair_sum(a, b, name):
    n, rows, cols = a.shape
    tr = rows if rows <= 512 else _row_tile(rows, 2048)
    blk = pl.BlockSpec((1, tr, cols), lambda q, i: (q, i, 0))

    def body(a_ref, b_ref, o_ref):
        o_ref[...] = (a_ref[...].astype(F32) + b_ref[...].astype(F32)).astype(o_ref.dtype)

    return pl.pallas_call(
        body, name=name, grid=(n, rows // tr), in_specs=[blk, blk], out_specs=blk, out_shape=_sds(a.shape, a.dtype),
        compiler_params=_params(("parallel", "parallel")),
    )(a, b)


def _chip_sums(parts, name):
    c = lax.axis_index("c")
    from_sibling = _sibling_swap(parts, "swap_" + name)
    own = [lax.dynamic_index_in_dim(p.reshape((N_CHIPS, 2) + p.shape[1:]), c, axis=1, keepdims=False) for p in parts]
    return [_pair_sum(a, b, f"pair_sum_{name}_{k}") for k, (a, b) in enumerate(zip(own, from_sibling))]


def _reduce_to_owner(parts, name):
    return _exchange_alone(_ChipExchange, _chip_sums([parts], name), "exchange_" + name)[0]


def _sum_adamw(parts, w, m, v, name):
    rows, cols = w.shape
    n_parts = parts.shape[0]
    tr = rows if rows <= 512 else _row_tile(rows)
    blk = pl.BlockSpec((tr, cols), lambda i: (i, 0))

    def body(p_ref, w_ref, m_ref, v_ref, g_ref, d_ref, nm_ref, nv_ref):
        g = p_ref[0].astype(F32)
        for i in range(1, n_parts):
            g = g + p_ref[i].astype(F32)
        nm = ADAM_B1 * m_ref[...] + (1.0 - ADAM_B1) * g
        nv = ADAM_B2 * v_ref[...] + (1.0 - ADAM_B2) * jnp.square(g)
        m_hat = nm / (1.0 - ADAM_B1 ** ADAM_STEP)
        v_hat = nv / (1.0 - ADAM_B2 ** ADAM_STEP)
        g_ref[...] = g
        d_ref[...] = -ADAM_LR * (m_hat / (jnp.sqrt(v_hat) + ADAM_EPS) + ADAM_WD * w_ref[...])
        nm_ref[...] = nm
        nv_ref[...] = nv

    return pl.pallas_call(
        body, name=name, grid=(rows // tr,), in_specs=[pl.BlockSpec((n_parts, tr, cols), lambda i: (0, i, 0)), blk, blk, blk],
        out_specs=[blk] * 4, out_shape=[_sds((rows, cols))] * 4, compiler_params=_params(("parallel",)),
    )(parts, w, m, v)


def _row_tile(rows, pref=256):
    t = pref
    while t >= 8:
        if rows % t == 0:
            return t
        t //= 2
    return rows


def _gathered_cols(g):
    g = jnp.moveaxis(g, 0, -2)
    return g.reshape(g.shape[:-2] + (g.shape[-2] * g.shape[-1],))


def _gathered_rows(g):
    g = jnp.moveaxis(g, 0, -3)
    return g.reshape(g.shape[:-3] + (g.shape[-3] * g.shape[-2], g.shape[-1]))


def _col_parts(full):
    c = full.shape[-1]
    return jnp.moveaxis(full.reshape(full.shape[:-1] + (N_DEV, c // N_DEV)), -2, 0)


def _row_parts(full):
    rows, c = full.shape[-2:]
    return jnp.moveaxis(full.reshape(full.shape[:-2] + (N_DEV, rows // N_DEV, c)), -3, 0)


def _w_in_split(w_in):
    s = lambda a, n: w_in[..., a:a + n]
    main = jnp.concatenate([s(R_QKV, 3072), s(R_Z, 1024), s(R_QSW, 1024), s(R_G, 2048), s(R_KSW, 256), s(R_VSW, 256)], axis=-1)
    gates = jnp.pad(s(R_BA, 2 * N_HD), [(0, 0)] * (w_in.ndim - 1) + [(0, LANES - 2 * N_HD)])
    return main, gates


def _w_in_join(dmain, dgates):
    s = lambda a, n: dmain[..., a:a + n]
    return jnp.concatenate([s(C_QKV, 3072), s(C_Z, 1024), dgates[..., :2 * N_HD], s(C_QSW, 1024), s(C_KSW, 256), s(C_VSW, 256),
                            s(C_GA, 2048)], axis=-1)


def _lane_row(a, offset):
    l, n = a.shape
    return jnp.pad(a, ((0, 0), (offset, LANES - offset - n)))[:, None, :]


def kernel(x, w_in, conv_w, a_log, dt_bias, dn_norm_w, sinks, w_branch_a, w_branch_b, w_out, ln1_g, ln1_b, w_gate_up, w_down, ln2_g, ln2_b, loss_target, m_w_in, m_conv_w, m_a_log, m_dt_bias, m_dn_norm_w, m_sinks, m_w_branch_a, m_w_branch_b, m_w_out, m_ln1_g, m_ln1_b, m_w_gate_up, m_w_down, m_ln2_g, m_ln2_b, v_w_in, v_conv_w, v_a_log, v_dt_bias, v_dn_norm_w, v_sinks, v_w_branch_a, v_w_branch_b, v_w_out, v_ln1_g, v_ln1_b, v_w_gate_up, v_w_down, v_ln2_g, v_ln2_b):
    l = DEPTH
    bf = lambda a: a.astype(_MXU)
    shards = [bf(w_in), bf(w_gate_up), bf(w_branch_a), bf(w_branch_b), bf(w_out), bf(w_down)]
    first = _exchange_alone(_Gather, [s[0] for s in shards], "gather_layer_0")
    conv_full = _gathered_cols(_all_gather(conv_w, "gather_conv_w"))
    row = lambda a: a[:, None, :]
    small = dict(
        conv=jnp.pad(conv_full, ((0, 0), (0, 8 - DN_CONV), (0, 0))), alog=_lane_row(a_log.reshape(l, N_HD), N_HD),
        dtb=_lane_row(dt_bias.reshape(l, N_HD), N_HD), dnw=row(dn_norm_w), sinks=sinks.reshape(l, SW_HEADS, 1, 1),
        ln1g=row(ln1_g), ln1b=row(ln1_b), ln2g=row(ln2_g), ln2b=row(ln2_b))

    def layer_weights(i, carried):
        s_in, s_gu, s_a, s_b, s_o, s_d = first if i == 0 else carried
        wm, wba = _w_in_split(_gathered_cols(s_in))
        return dict(wm=wm, wba=wba, wgu=_gathered_cols(s_gu), wa=_gathered_rows(s_a), wb=_gathered_rows(s_b),
                    wo=_gathered_rows(s_o), wd=_gathered_rows(s_d), **{k: a[i] for k, a in small.items()})

    def fwd_carry(i):
        return (_Gather, [s[i + 1] for s in shards]) if i + 1 < l else None

    layer_grads, received, waiting = [None] * l, [None] * l, []

    def grads_done(i, g_i, carried):
        if waiting:
            received[waiting.pop()] = carried
        layer_grads[i] = g_i
        parts = [_col_parts(_w_in_join(g_i["wm"], g_i["wba"])), _col_parts(g_i["wgu"]), _row_parts(g_i["wa"]),
                 _row_parts(g_i["wb"]), _row_parts(g_i["wo"]), _row_parts(g_i["wd"])]
        waiting.append(i)
        return (_ChipExchange, _chip_sums(parts, f"layer_{i}"))

    sq, dx, last = _trunk(x[0], loss_target[0], l, layer_weights, fwd_carry, grads_done)
    received[waiting.pop()] = _exchange_alone(last[0], last[1], "exchange_layer_0")
    loss = lax.psum(0.5 * sq[0, 0] / D_MODEL, ("x", "y", "c"))
    g = {k: jnp.stack([gi[k] for gi in layer_grads]) for k in small}

    def adamw(parts, w, m, v, name):
        rows = w.shape[0] * w.shape[1]
        flat = lambda a: a.reshape(rows, a.shape[-1])
        outs = _sum_adamw(parts.reshape(parts.shape[0], rows, w.shape[-1]), flat(w), flat(m), flat(v), "adamw_" + name)
        return [o.reshape(w.shape) for o in outs]

    got = [jnp.stack(per_layer, axis=1) for per_layer in zip(*received)]
    dconv = _reduce_to_owner(_col_parts(g["conv"][:, :DN_CONV, :]).reshape(N_DEV, l * DN_CONV, -1), "conv_w")
    results = {
        "w_in": adamw(got[0], w_in, m_w_in, v_w_in, "w_in"),
        "conv_w": adamw(dconv.reshape(N_CHIPS, l, DN_CONV, -1), conv_w, m_conv_w, v_conv_w, "conv_w"),
        "w_branch_a": adamw(got[2], w_branch_a, m_w_branch_a, v_w_branch_a, "w_branch_a"),
        "w_branch_b": adamw(got[3], w_branch_b, m_w_branch_b, v_w_branch_b, "w_branch_b"),
        "w_out": adamw(got[4], w_out, m_w_out, v_w_out, "w_out"),
        "w_gate_up": adamw(got[1], w_gate_up, m_w_gate_up, v_w_gate_up, "w_gate_up"),
        "w_down": adamw(got[5], w_down, m_w_down, v_w_down, "w_down"),
    }

    small_w = {"a_log": a_log.reshape(l, N_HD), "dt_bias": dt_bias.reshape(l, N_HD), "dn_norm_w": dn_norm_w, "sinks": sinks,
               "ln1_g": ln1_g, "ln1_b": ln1_b, "ln2_g": ln2_g, "ln2_b": ln2_b}
    small_m = {"a_log": m_a_log, "dt_bias": m_dt_bias, "dn_norm_w": m_dn_norm_w, "sinks": m_sinks, "ln1_g": m_ln1_g,
               "ln1_b": m_ln1_b, "ln2_g": m_ln2_g, "ln2_b": m_ln2_b}
    small_v = {"a_log": v_a_log, "dt_bias": v_dt_bias, "dn_norm_w": v_dn_norm_w, "sinks": v_sinks, "ln1_g": v_ln1_g,
               "ln1_b": v_ln1_b, "ln2_g": v_ln2_g, "ln2_b": v_ln2_b}
    small_g = {"a_log": g["alog"][:, 0, N_HD:2 * N_HD], "dt_bias": g["dtb"][:, 0, N_HD:2 * N_HD], "dn_norm_w": g["dnw"][:, 0, :],
               "sinks": g["sinks"].reshape(l, SW_HEADS), "ln1_g": g["ln1g"][:, 0, :], "ln1_b": g["ln1b"][:, 0, :],
               "ln2_g": g["ln2g"][:, 0, :], "ln2_b": g["ln2b"][:, 0, :]}
    names = list(small_w)
    cat = lambda d: jnp.concatenate([d[n].reshape(l, -1) for n in names], axis=1)
    widths = [small_w[n].shape[1] for n in names]
    total = sum(widths)
    padded = -(-total // LANES) * LANES
    pad = lambda a: jnp.pad(a, ((0, 8 - l), (0, padded - total)))
    got = _all_gather(pad(cat(small_g)), "gather_small_grads")
    outs = _sum_adamw(got, pad(cat(small_w)), pad(cat({n: small_m[n].reshape(l, -1) for n in names})),
                      pad(cat({n: small_v[n].reshape(l, -1) for n in names})), "adamw_small")
    off = 0
    for n, wd_ in zip(names, widths):
        shape = {"a_log": a_log.shape, "dt_bias": dt_bias.shape}.get(n, small_w[n].shape)
        results[n] = [o[:l, off:off + wd_].reshape(shape) for o in outs]
        off += wd_

    order = ["w_in", "conv_w", "a_log", "dt_bias", "dn_norm_w", "sinks", "w_branch_a", "w_branch_b", "w_out", "ln1_g", "ln1_b",
             "w_gate_up", "w_down", "ln2_g", "ln2_b"]
    return (loss, dx[None], *[results[n][0] for n in order], *[results[n][1] for n in order],
            *[results[n][2] for n in order], *[results[n][3] for n in order])
```

```python
import functools

import jax
import jax.numpy as jnp
from jax import lax
from jax.experimental import pallas as pl
from jax.experimental.pallas import tpu as pltpu

F32 = jnp.float32
_MXU = jnp.bfloat16
_HI = lax.Precision.HIGHEST
_MID = lax.Precision.HIGH

N_DEV = 8
D_MODEL = 1024
DEPTH = 4
DN_HEADS = 8
DN_HEAD_DIM = 128
DN_WIDTH = DN_HEADS * DN_HEAD_DIM
DN_CONV = 5
DN_CHUNK = 64
SW_HEADS = 16
SW_KV_HEADS = 4
SW_HEAD_DIM = 64
SW_GROUP = SW_HEADS // SW_KV_HEADS
SW_BLOCK = 128
SW_KV_WIDTH = SW_KV_HEADS * SW_HEAD_DIM
ROPE_THETA = 10000.0
FFN_HIDDEN = 2816
DN_ALPHA = (2.0 * DEPTH) ** 0.25
LN_EPS = 1e-5
RMS_EPS = 1e-6
ADAM_LR = 0.001
ADAM_B1 = 0.9
ADAM_B2 = 0.999
ADAM_EPS = 1e-08
ADAM_WD = 0.01
ADAM_STEP = 10

LANES = 128
N_HD = 2 * DN_HEADS
DN_GROUP = 4 * DN_CHUNK
INV_SUB = 16
LOCAL_ROWS_FWD = 1024
LOCAL_ROWS_BWD = 512
REC_ROWS = 256
ROW_TILE = 256
VMEM_LIMIT = 48 << 20

C_QKV, C_Z, C_QSW, C_GA, C_GB, C_KSW, C_VSW = 0, 3072, 4096, 5120, 6144, 7168, 7424
MAIN_COLS = 7680
R_QKV, R_Z, R_BA, R_QSW, R_KSW, R_VSW, R_G = 0, 3072, 4096, 4128, 5152, 5408, 5664
IN_COLS = 7712


_NN = ((1,), (0,))
_NT = ((1,), (1,))
_TN = ((0,), (0,))


def _dg(a, b, dims, precision):
    if precision is not None:
        return lax.dot_general(a, b, (dims, ((), ())), precision=precision, preferred_element_type=F32)
    return lax.dot_general(a.astype(_MXU), b.astype(_MXU), (dims, ((), ())), preferred_element_type=F32)


def _make_dots(hi):
    @jax.custom_vjp
    def nn(a, b):
        return _dg(a, b, _NN, hi)

    @jax.custom_vjp
    def nt(a, b):
        return _dg(a, b, _NT, hi)

    @jax.custom_vjp
    def tn(a, b):
        return _dg(a, b, _TN, hi)

    nn.defvjp(lambda a, b: (nn(a, b), (a, b)), lambda r, g: (nt(g, r[1]), tn(r[0], g)))
    nt.defvjp(lambda a, b: (nt(a, b), (a, b)), lambda r, g: (nn(g, r[1]), tn(g, r[0])))
    tn.defvjp(lambda a, b: (tn(a, b), (a, b)), lambda r, g: (nt(r[1], g), nn(r[0], g)))
    return nn, nt, tn


_bnn, _bnt, _btn = _make_dots(None)
_hnn, _hnt, _htn = _make_dots(_HI)
_mnn, _mnt, _mtn = _make_dots(_MID)


def _neumann(a, order):
    n = a.shape[0]
    eye = (lax.broadcasted_iota(jnp.int32, (n, n), 0) == lax.broadcasted_iota(jnp.int32, (n, n), 1)).astype(F32)
    inv = eye - a
    p = a
    span = 2
    while span < order:
        p = _mnn(p, p)
        inv = inv + _mnn(inv, p)
        span *= 2
    return inv


def _inv_unit(a, order):
    n = a.shape[0]
    ii = lax.broadcasted_iota(jnp.int32, (n, n), 0)
    jj = lax.broadcasted_iota(jnp.int32, (n, n), 1)
    near = (ii & -INV_SUB) == (jj & -INV_SUB)
    d_inv = _neumann(jnp.where(near, a, 0.0), INV_SUB)
    outer = _neumann(_mnn(d_inv, jnp.where(near, 0.0, a)), order // INV_SUB)
    return _mnn(outer, d_inv)


def _inv_unit_t(t, g):
    return -_mnt(_mtn(t, g), t)


def _silu(x):
    return x * jax.nn.sigmoid(x)


def _softplus(x):
    return jnp.maximum(x, 0.0) + jnp.log1p(jnp.exp(-jnp.abs(x)))


def _params(sem=None):
    kw = {"vmem_limit_bytes": VMEM_LIMIT}
    if sem is not None:
        kw["dimension_semantics"] = sem
    return pltpu.CompilerParams(**kw)


def _tile(dim, pref):
    if dim <= pref:
        return dim
    t = (pref // LANES) * LANES
    while t > LANES and dim % t:
        t -= LANES
    assert dim % t == 0, (dim, pref)
    return t


def _full(shape):
    zeros = (0,) * len(shape)
    return pl.BlockSpec(shape, lambda *_: zeros)


def _sds(shape, dtype=F32):
    return jax.ShapeDtypeStruct(shape, dtype)


def _mm(a, b, mode, *, name, add=None, tm=1536, tn=1536, tk=1536, out_dtype=F32):
    if mode == "nn":
        (m, k), (k2, n) = a.shape, b.shape
    elif mode == "nt":
        (m, k), (n, k2) = a.shape, b.shape
    else:
        (k, m), (k2, n) = a.shape, b.shape
    assert k == k2, (a.shape, b.shape, mode)
    tm, tn, tk = _tile(m, tm), _tile(n, tn), _tile(k, tk)
    nk = k // tk
    dims = {"nn": _NN, "nt": _NT, "tn": _TN}[mode]

    def body(*refs):
        if add is None:
            a_ref, b_ref, o_ref, acc = refs
        else:
            a_ref, b_ref, add_ref, o_ref, acc = refs
        kk = pl.program_id(2)

        @pl.when(kk == 0)
        def _():
            acc[...] = jnp.zeros_like(acc)

        acc[...] += _dg(a_ref[...], b_ref[...], dims, None)

        @pl.when(kk == nk - 1)
        def _():
            o_ref[...] = (acc[...] if add is None else acc[...] + add_ref[...]).astype(out_dtype)

    a_spec = pl.BlockSpec((tk, tm), lambda i, j, kk: (kk, i)) if mode == "tn" else pl.BlockSpec((tm, tk), lambda i, j, kk: (i, kk))
    b_spec = pl.BlockSpec((tn, tk), lambda i, j, kk: (j, kk)) if mode == "nt" else pl.BlockSpec((tk, tn), lambda i, j, kk: (kk, j))
    o_spec = pl.BlockSpec((tm, tn), lambda i, j, kk: (i, j))
    ins, specs = [a, b], [a_spec, b_spec]
    if add is not None:
        ins.append(add)
        specs.append(o_spec)
    return pl.pallas_call(
        body, name=name, grid=(m // tm, n // tn, nk), in_specs=specs, out_specs=o_spec,
        out_shape=_sds((m, n), out_dtype), scratch_shapes=[pltpu.VMEM((tm, tn), F32)],
        compiler_params=_params(("parallel", "parallel", "arbitrary")),
    )(*ins)


def _cols(width, start):
    assert start % width == 0
    return pl.BlockSpec((ROW_TILE, width), lambda i: (i, start // width))


def _rows(width):
    return pl.BlockSpec((ROW_TILE, width), lambda i: (i, 0))


def _accumulate(ref, value, step):
    @pl.when(step == 0)
    def _():
        ref[...] = value

    @pl.when(step != 0)
    def _():
        ref[...] += value


def _ln_fn(x, r, g, b):
    u = DN_ALPHA * x + r
    mu = jnp.mean(u, axis=-1, keepdims=True)
    var = jnp.mean(jnp.square(u - mu), axis=-1, keepdims=True)
    return (u - mu) * lax.rsqrt(var + LN_EPS) * g + b


def _ln_fwd(x, r, g, b, name):
    t, d = x.shape

    def body(x_ref, r_ref, g_ref, b_ref, o_ref, ob_ref):
        y = _ln_fn(x_ref[...], r_ref[...], g_ref[...], b_ref[...])
        o_ref[...] = y
        ob_ref[...] = y.astype(_MXU)

    return pl.pallas_call(
        body, name=name, grid=(t // ROW_TILE,), in_specs=[_rows(d), _rows(d), _full((1, d)), _full((1, d))],
        out_specs=[_rows(d), _rows(d)], out_shape=[_sds((t, d)), _sds((t, d), _MXU)], compiler_params=_params(("parallel",)),
    )(x, r, g, b)


def _ln_bwd(x, r, g, b, dy, name):
    t, d = x.shape

    def body(x_ref, r_ref, g_ref, b_ref, dy_ref, dx_ref, dr_ref, dg_ref, db_ref):
        _, vjp = jax.vjp(_ln_fn, x_ref[...], r_ref[...], g_ref[...], b_ref[...])
        dx, dr, dg, db = vjp(dy_ref[...])
        dx_ref[...] = dx
        dr_ref[...] = dr.astype(_MXU)
        _accumulate(dg_ref, dg, pl.program_id(0))
        _accumulate(db_ref, db, pl.program_id(0))

    return pl.pallas_call(
        body, name=name, grid=(t // ROW_TILE,),
        in_specs=[_rows(d), _rows(d), _full((1, d)), _full((1, d)), _rows(d)],
        out_specs=[_rows(d), _rows(d), _full((1, d)), _full((1, d))],
        out_shape=[_sds((t, d)), _sds((t, d), _MXU), _sds((1, d)), _sds((1, d))],
        compiler_params=_params(("arbitrary",)),
    )(x, r, g, b, dy)


def _merge_fn(ga, gb, ya, yb):
    return jax.nn.sigmoid(ga) * ya + jax.nn.sigmoid(gb) * yb


def _merge_fwd(proj, ya, yb):
    t, d = ya.shape

    def body(ga_ref, gb_ref, ya_ref, yb_ref, o_ref):
        o_ref[...] = _merge_fn(ga_ref[...], gb_ref[...], ya_ref[...], yb_ref[...]).astype(_MXU)

    return pl.pallas_call(
        body, name="merge_fwd", grid=(t // ROW_TILE,), in_specs=[_cols(d, C_GA), _cols(d, C_GB), _rows(d), _rows(d)],
        out_specs=_rows(d), out_shape=_sds((t, d), _MXU), compiler_params=_params(("parallel",)),
    )(proj, proj, ya, yb)


def _merge_bwd(proj, ya, yb, dm):
    t, d = ya.shape

    def body(ga_ref, gb_ref, ya_ref, yb_ref, dm_ref, dga_ref, dgb_ref, dya_ref, dyb_ref):
        _, vjp = jax.vjp(_merge_fn, ga_ref[...], gb_ref[...], ya_ref[...], yb_ref[...])
        dga_ref[...], dgb_ref[...], dya_ref[...], dyb_ref[...] = [g.astype(_MXU) for g in vjp(dm_ref[...])]

    return pl.pallas_call(
        body, name="merge_bwd", grid=(t // ROW_TILE,),
        in_specs=[_cols(d, C_GA), _cols(d, C_GB), _rows(d), _rows(d), _rows(d)],
        out_specs=[_rows(d)] * 4, out_shape=[_sds((t, d), _MXU)] * 4, compiler_params=_params(("parallel",)),
    )(proj, proj, ya, yb, dm)


def _swiglu_fn(gate, up):
    return _silu(gate) * up


def _swiglu_fwd(gu):
    t = gu.shape[0]
    f = FFN_HIDDEN
    rows = 128

    def body(gu_ref, o_ref):
        o_ref[...] = _swiglu_fn(gu_ref[:, :f], gu_ref[:, f:]).astype(_MXU)

    return pl.pallas_call(
        body, name="swiglu_fwd", grid=(t // rows,), in_specs=[pl.BlockSpec((rows, 2 * f), lambda i: (i, 0))],
        out_specs=pl.BlockSpec((rows, f), lambda i: (i, 0)), out_shape=_sds((t, f), _MXU), compiler_params=_params(("parallel",)),
    )(gu)


def _swiglu_bwd(gu, dh):
    t = gu.shape[0]
    f = FFN_HIDDEN
    rows = 128

    def body(gu_ref, dh_ref, o_ref):
        _, vjp = jax.vjp(_swiglu_fn, gu_ref[:, :f], gu_ref[:, f:])
        o_ref[:, :f], o_ref[:, f:] = [g.astype(_MXU) for g in vjp(dh_ref[...])]

    return pl.pallas_call(
        body, name="swiglu_bwd", grid=(t // rows,),
        in_specs=[pl.BlockSpec((rows, 2 * f), lambda i: (i, 0)), pl.BlockSpec((rows, f), lambda i: (i, 0))],
        out_specs=pl.BlockSpec((rows, 2 * f), lambda i: (i, 0)), out_shape=_sds((t, 2 * f), _MXU),
        compiler_params=_params(("parallel",)),
    )(gu, dh)


def _loss_head(y, target):
    t, d = y.shape

    def body(y_ref, t_ref, s_ref, dy_ref):
        err = y_ref[...] - t_ref[...]
        dy_ref[...] = err / d
        _accumulate(s_ref, jnp.broadcast_to(jnp.sum(jnp.square(err)), (1, LANES)), pl.program_id(0))

    return pl.pallas_call(
        body, name="loss_head", grid=(t // ROW_TILE,), in_specs=[_rows(d), _rows(d)],
        out_specs=[_full((1, LANES)), _rows(d)], out_shape=[_sds((1, LANES)), _sds((t, d))],
        compiler_params=_params(("arbitrary",)),
    )(y, target)


def _shift_rows(x, s):
    if s == 0:
        return x
    return pltpu.roll(x, (-s) % x.shape[0], 0)


def _conv(x, w):
    half = DN_CONV // 2
    acc = None
    for k in range(DN_CONV):
        term = _shift_rows(x, k - half) * w[k:k + 1, :]
        acc = term if acc is None else acc + term
    return acc


def _act_norm(c, do_norm, scale):
    a = _silu(c)
    if not do_norm:
        return a
    return a * lax.rsqrt(jnp.sum(a * a, axis=-1, keepdims=True) + RMS_EPS) * scale


PREP_ROWS = 512
HALO = 8
_KINDS = ((True, DN_HEAD_DIM ** -0.5), (True, 1.0), (False, 1.0))


def _halo_rows(read, i, pr, t):
    lo, hi = i * pr - HALO, (i + 1) * pr + HALO
    parts = []
    if lo < 0:
        parts.append(jnp.zeros((HALO, LANES), F32))
    parts.append(read(max(lo, 0), min(hi, t)))
    if hi > t:
        parts.append(jnp.zeros((HALO, LANES), F32))
    return jnp.concatenate(parts, axis=0) if len(parts) > 1 else parts[0]


def _prep_fwd(proj, conv_w, kind):
    t = proj.shape[0]
    pr = min(PREP_ROWS, t)
    do_norm, scale = _KINDS[kind]
    blk = pl.BlockSpec((t, LANES), lambda j: (0, kind * DN_HEADS + j))

    def body(x_ref, w_ref, o_ref):
        w = w_ref[...]
        for i in range(t // pr):
            xx = _halo_rows(lambda lo, hi: x_ref[lo:hi, :], i, pr, t)
            c = _conv(xx, w)[HALO:HALO + pr, :]
            o_ref[i * pr:(i + 1) * pr, :] = _act_norm(c, do_norm, scale)

    return pl.pallas_call(
        body, name=f"prep_fwd_{kind}", grid=(DN_HEADS,),
        in_specs=[blk, pl.BlockSpec((8, LANES), lambda j: (0, kind * DN_HEADS + j))],
        out_specs=pl.BlockSpec((t, LANES), lambda j: (0, j)), out_shape=_sds((t, DN_WIDTH)),
        compiler_params=_params(("parallel",)),
    )(proj, conv_w)


def _prep_bwd(proj, conv_w, d2, kind):
    t = proj.shape[0]
    pr = min(PREP_ROWS, t)
    do_norm, scale = _KINDS[kind]
    half = DN_CONV // 2
    blk = pl.BlockSpec((t, LANES), lambda j: (0, kind * DN_HEADS + j))
    oblk = pl.BlockSpec((t, LANES), lambda j: (0, j))

    def body(x_ref, w_ref, d_ref, dx_ref, dw_ref):
        w = w_ref[...]
        own = slice(HALO, HALO + pr)
        dw = jnp.zeros((8, LANES), F32)
        for i in range(t // pr):
            xx = _halo_rows(lambda lo, hi: x_ref[lo:hi, :], i, pr, t)
            dn = _halo_rows(lambda lo, hi: d_ref[0, lo:hi, :] + d_ref[1, lo:hi, :], i, pr, t)
            _, vjp = jax.vjp(lambda c: _act_norm(c, do_norm, scale), _conv(xx, w))
            (dc,) = vjp(dn)
            dx = None
            rows = []
            for k in range(DN_CONV):
                term = _shift_rows(dc, half - k) * w[k:k + 1, :]
                dx = term if dx is None else dx + term
                rows.append(jnp.sum(dc[own, :] * _shift_rows(xx, k - half)[own, :], axis=0, keepdims=True))
            dx_ref[i * pr:(i + 1) * pr, :] = dx[own, :].astype(_MXU)
            dw = dw + jnp.concatenate(rows + [jnp.zeros((8 - DN_CONV, LANES), F32)], axis=0)
        dw_ref[...] = dw

    return pl.pallas_call(
        body, name=f"prep_bwd_{kind}", grid=(DN_HEADS,),
        in_specs=[blk, pl.BlockSpec((8, LANES), lambda j: (0, kind * DN_HEADS + j)), pl.BlockSpec((2, t, LANES), lambda j: (0, 0, j))],
        out_specs=[oblk, pl.BlockSpec((8, LANES), lambda j: (0, j))], out_shape=[_sds((t, DN_WIDTH), _MXU), _sds((8, DN_WIDTH))],
        compiler_params=_params(("parallel",)),
    )(proj, conv_w, d2)


def _gb_fn(ba, alog_row, dtb_row):
    c = DN_CHUNK
    lane = lax.broadcasted_iota(jnp.int32, (c, LANES), 1)
    ii = lax.broadcasted_iota(jnp.int32, (c, c), 0)
    jj = lax.broadcasted_iota(jnp.int32, (c, c), 1)
    beta = jax.nn.sigmoid(ba)
    g = -jnp.exp(alog_row) * _softplus(ba + dtb_row)
    g = jnp.where((lane >= N_HD) & (lane < 2 * N_HD), g, 0.0)
    gc_fwd = _hnn((ii >= jj).astype(F32), g)
    gc_rev = _hnn((ii <= jj).astype(F32), g)
    gc = jnp.where(lane < N_HD + DN_HEADS, gc_fwd, gc_rev)
    return jnp.where(lane < N_HD, beta, jnp.where(lane < 2 * N_HD, gc, 0.0))


def _per_head_spec():
    return pl.BlockSpec((N_HD, ROW_TILE, LANES), lambda i: (0, i, 0))


def _gb_fwd(ba, alog_row, dtb_row):
    t = ba.shape[0]
    n = ROW_TILE // DN_CHUNK

    def body(ba_ref, a_ref, d_ref, o_ref, beta_ref, gc_ref):
        for c in range(n):
            rows = slice(c * DN_CHUNK, (c + 1) * DN_CHUNK)
            out = _gb_fn(ba_ref[rows, :], a_ref[...], d_ref[...])
            o_ref[rows, :] = out
            for j in range(N_HD):
                beta_ref[j, rows, :] = jnp.broadcast_to(out[:, j:j + 1], (DN_CHUNK, LANES))
                gc_ref[j, rows, :] = jnp.broadcast_to(out[:, N_HD + j:N_HD + j + 1], (DN_CHUNK, LANES))

    per_head = _sds((N_HD, t, LANES))
    return pl.pallas_call(
        body, name="gates_fwd", grid=(t // ROW_TILE,), in_specs=[_rows(LANES), _full((1, LANES)), _full((1, LANES))],
        out_specs=[_rows(LANES), _per_head_spec(), _per_head_spec()], out_shape=[_sds((t, LANES)), per_head, per_head],
        compiler_params=_params(("parallel",)),
    )(ba, alog_row, dtb_row)


def _gb_bwd(ba, alog_row, dtb_row, dbeta, dgc, d_rows):
    t = ba.shape[0]
    n = ROW_TILE // DN_CHUNK

    def body(ba_ref, a_ref, d_ref, dbeta_ref, dgc_ref, dr_ref, dba_ref, dal_ref, ddt_ref):
        dal = jnp.zeros((1, LANES), F32)
        ddt = jnp.zeros((1, LANES), F32)
        lane = lax.broadcasted_iota(jnp.int32, (DN_CHUNK, LANES), 1)
        for c in range(n):
            rows = slice(c * DN_CHUNK, (c + 1) * DN_CHUNK)
            cot = dr_ref[rows, :]
            for j in range(N_HD):
                cot = jnp.where(lane == j, dbeta_ref[j, rows, :], cot)
                cot = jnp.where(lane == N_HD + j, dgc_ref[j, rows, :] + cot, cot)
            _, vjp = jax.vjp(_gb_fn, ba_ref[rows, :], a_ref[...], d_ref[...])
            dba, da, dd = vjp(cot)
            dba_ref[rows, :] = dba.astype(_MXU)
            dal = dal + da
            ddt = ddt + dd
        _accumulate(dal_ref, dal, pl.program_id(0))
        _accumulate(ddt_ref, ddt, pl.program_id(0))

    return pl.pallas_call(
        body, name="gates_bwd", grid=(t // ROW_TILE,),
        in_specs=[_rows(LANES), _full((1, LANES)), _full((1, LANES)), _per_head_spec(), _per_head_spec(), _rows(LANES)],
        out_specs=[_rows(LANES), _full((1, LANES)), _full((1, LANES))],
        out_shape=[_sds((t, LANES), _MXU), _sds((1, LANES)), _sds((1, LANES))], compiler_params=_params(("arbitrary",)),
    )(ba, alog_row, dtb_row, dbeta, dgc, d_rows)


def _dn_decay(gcc, gcr, sgn):
    c = DN_CHUNK
    ii = lax.broadcasted_iota(jnp.int32, (c, c), 0)
    jj = lax.broadcasted_iota(jnp.int32, (c, c), 1)
    d = (ii - jj) * sgn
    lower = d >= 0
    return jnp.where(lower, jnp.exp(jnp.where(lower, gcc - gcr, 0.0)), 0.0), d > 0


def _dn_a(k, beta, gcc, gcr, sgn):
    decay, strict = _dn_decay(gcc, gcr, sgn)
    return jnp.where(strict, beta * _bnt(k, k) * decay, 0.0)


def _dn_group(q, k, v, beta, gcc, gcr, sgn):
    n = DN_GROUP
    ii = lax.broadcasted_iota(jnp.int32, (n, n), 0)
    jj = lax.broadcasted_iota(jnp.int32, (n, n), 1)
    same = (ii & -DN_CHUNK) == (jj & -DN_CHUNK)
    d = (ii - jj) * sgn
    lower = same & (d >= 0)
    decay = jnp.where(lower, jnp.exp(jnp.where(lower, gcc - gcr, 0.0)), 0.0)
    a = jnp.where(same & (d > 0), beta * _bnt(k, k) * decay, 0.0)
    t_inv = _inv_unit(a, DN_CHUNK)
    u = _bnn(t_inv, v * beta)
    w = _bnn(t_inv, k * (beta * jnp.exp(gcc)))
    return u, w, _bnt(q, k) * decay, t_inv


def _dn_local(t_inv, q, k, v, beta, gcc, gcr, sgn):
    c = DN_CHUNK
    decay, _ = _dn_decay(gcc, gcr, sgn)
    eg = jnp.exp(gcc)
    u = _bnn(t_inv, v * beta)
    w = _bnn(t_inv, k * (beta * eg))
    qk = _bnt(q, k) * decay
    qd = q * eg
    last = jnp.where(sgn > 0, c - 1, 0)
    onehot = (lax.broadcasted_iota(jnp.int32, (c, 1), 0) == last).astype(F32)
    gl = jnp.sum(gcc * onehot, axis=0, keepdims=True)
    kd = k * jnp.exp(gl - gcc)
    egl = jnp.broadcast_to(jnp.exp(gl), (1, LANES))
    return u, w, qk, qd, kd, egl


def _hd_sign(hd):
    return jnp.where(hd < DN_HEADS, 1, -1).astype(jnp.int32)


def _head_of(hd):
    return jnp.where(hd < DN_HEADS, hd, hd - DN_HEADS)


def _dir_of(hd):
    return jnp.where(hd < DN_HEADS, 0, 1)


def _dn_specs(rows_step):
    nl = rows_step // DN_CHUNK
    wide = pl.BlockSpec((1, rows_step, LANES), lambda hd, i: (hd, i, 0))
    half = pl.BlockSpec((1, rows_step, DN_CHUNK), lambda hd, i: (hd, i, 0))
    col = wide
    row = pl.BlockSpec((1, nl, 1, DN_CHUNK), lambda hd, i: (hd, i, 0, 0))
    egl = pl.BlockSpec((1, nl, 1, LANES), lambda hd, i: (hd, i, 0, 0))
    return wide, half, col, row, egl


def _qkv_specs(rows_step):
    return [pl.BlockSpec((rows_step, LANES), lambda hd, i: (i, _head_of(hd)))] * 3


def _maybe_carrying(carry, body, name, grid, operands, in_specs, out_specs, out_shape):
    extra_scratch = []
    if carry is not None:
        kind, arrays = carry
        body, more_in, more_out, more_shape, extra_scratch = _carried(kind, arrays, body, len(operands), len(out_shape), grid)
        operands, in_specs = operands + list(arrays), in_specs + more_in
        out_specs, out_shape = out_specs + more_out, out_shape + more_shape
    sem = ("arbitrary",) * len(grid) if carry is not None else ("parallel",) * len(grid)
    return pl.pallas_call(
        body, name=name, grid=grid, in_specs=in_specs, out_specs=out_specs, out_shape=out_shape,
        scratch_shapes=extra_scratch, compiler_params=_params(sem),
    )(*operands)


def _dn_local_fwd(q, k, v, beta_c, gc_c, gc_r, carry=None):
    t = q.shape[0]
    nc = t // DN_CHUNK
    rows_step = min(LOCAL_ROWS_FWD, t)
    wide, half, col, row, egl = _dn_specs(rows_step)

    ng = rows_step // DN_GROUP
    per = DN_GROUP // DN_CHUNK
    grow = pl.BlockSpec((1, ng, 1, DN_GROUP), lambda hd, i: (hd, i, 0, 0))

    def body(q_ref, k_ref, v_ref, b_ref, gc_ref, gg_ref, u_ref, w_ref, qk_ref, qd_ref, kd_ref, egl_ref, t_ref):
        sgn = _hd_sign(pl.program_id(0))
        last = jnp.where(sgn > 0, DN_CHUNK - 1, 0)
        onehot = (lax.broadcasted_iota(jnp.int32, (DN_CHUNK, 1), 0) == last).astype(F32)
        groups = lambda a: a.reshape((ng, DN_GROUP) + a.shape[1:])
        q_all, k_all, gcc_all = q_ref[...], k_ref[...], gc_ref[0][:, :1]
        u, w, qk, t_inv = jax.vmap(functools.partial(_dn_group, sgn=sgn))(
            groups(q_all), groups(k_all), groups(v_ref[...]), groups(b_ref[0][:, :1]), groups(gcc_all), gg_ref[0])
        u_ref[0] = u.reshape(rows_step, LANES)
        w_ref[0] = w.reshape(rows_step, LANES).astype(_MXU)
        qd_ref[0] = (q_all * jnp.exp(gcc_all)).astype(_MXU)
        for gi in range(ng):
            for c in range(per):
                blk = slice(c * DN_CHUNK, (c + 1) * DN_CHUNK)
                rows = slice(gi * DN_GROUP + c * DN_CHUNK, gi * DN_GROUP + (c + 1) * DN_CHUNK)
                qk_ref[0, rows, :] = qk[gi, blk, blk].astype(_MXU)
                t_ref[0, rows, :] = t_inv[gi, blk, blk]
                gl = jnp.sum(gcc_all[rows, :] * onehot, axis=0, keepdims=True)
                kd_ref[0, rows, :] = (k_all[rows, :] * jnp.exp(gl - gcc_all[rows, :])).astype(_MXU)
                egl_ref[0, gi * per + c] = jnp.broadcast_to(jnp.exp(gl), (1, LANES))

    big = _sds((N_HD, t, LANES))
    small = _sds((N_HD, t, DN_CHUNK))
    operands = [q, k, v, beta_c, gc_c, gc_r.reshape(N_HD, t // DN_GROUP, 1, DN_GROUP)]
    return _maybe_carrying(
        carry, body, "dn_local_fwd", (N_HD, t // rows_step), operands, _qkv_specs(rows_step) + [col, col, grow],
        [wide, wide, half, wide, wide, egl, half],
        [big, _sds(big.shape, _MXU), _sds(small.shape, _MXU), _sds(big.shape, _MXU), _sds(big.shape, _MXU),
         _sds((N_HD, nc, 1, LANES)), small])


def _dn_local_bwd(q, k, v, beta_c, gc_c, gc_r, t_inv, du, dw, dqk, dqd, dkd, degl, carry=None):
    t = q.shape[0]
    nc = t // DN_CHUNK
    rows_step = min(LOCAL_ROWS_BWD, t)
    nl = rows_step // DN_CHUNK
    wide, half, col, row, egl = _dn_specs(rows_step)
    dspec = pl.BlockSpec((1, rows_step, LANES), lambda hd, i: (_dir_of(hd), i, _head_of(hd)))

    def body(q_ref, k_ref, v_ref, b_ref, gc_ref, gr_ref, t_ref, du_ref, dw_ref, dqk_ref, dqd_ref, dkd_ref, degl_ref,
             dq_ref, dk_ref, dv_ref, db_ref, dgc_ref, dgr_ref):
        sgn = _hd_sign(pl.program_id(0))

        def chunk_bwd(tinv, q, k, v, beta, gcc, gcr, du, dw, dqk, dqd, dkd, degl):
            _, vjp = jax.vjp(functools.partial(_dn_local, sgn=sgn), tinv, q, k, v, beta, gcc, gcr)
            dt, dq, dk, dv, db, dgc, dgr = vjp((du, dw, dqk, dqd, dkd, degl))
            _, vjp_a = jax.vjp(functools.partial(_dn_a, sgn=sgn), k, beta, gcc, gcr)
            dk2, db2, dgc2, dgr2 = vjp_a(_inv_unit_t(tinv, dt))
            return dq, dk + dk2, dv, db + db2, dgc + dgc2, dgr + dgr2

        chunks = lambda a: a.reshape((nl, DN_CHUNK) + a.shape[1:])
        dq, dk, dv, db, dgc, dgr = jax.vmap(chunk_bwd)(
            chunks(t_ref[0]), chunks(q_ref[...]), chunks(k_ref[...]), chunks(v_ref[...]), chunks(b_ref[0][:, :1]),
            chunks(gc_ref[0][:, :1]),
            gr_ref[0], chunks(du_ref[0]), chunks(dw_ref[0]), chunks(dqk_ref[0]), chunks(dqd_ref[0]), chunks(dkd_ref[0]),
            degl_ref[0])
        dq_ref[0] = dq.reshape(rows_step, LANES)
        dk_ref[0] = dk.reshape(rows_step, LANES)
        dv_ref[0] = dv.reshape(rows_step, LANES)
        db_ref[0] = jnp.broadcast_to(db.reshape(rows_step, 1), (rows_step, LANES))
        dgc_ref[0] = jnp.broadcast_to(dgc.reshape(rows_step, 1), (rows_step, LANES))
        dgr_ref[0] = dgr

    per_dir = _sds((2, t, DN_WIDTH))
    return _maybe_carrying(
        carry, body, "dn_local_bwd", (N_HD, t // rows_step), [q, k, v, beta_c, gc_c, gc_r, t_inv, du, dw, dqk, dqd, dkd, degl],
        _qkv_specs(rows_step) + [col, col, row, half, wide, wide, half, wide, wide, egl], [dspec, dspec, dspec, col, col, row],
        [per_dir, per_dir, per_dir, _sds((N_HD, t, LANES)), _sds((N_HD, t, LANES)), _sds((N_HD, nc, 1, DN_CHUNK))])


REC_HEADS = 8
REC_GROUPS = N_HD // REC_HEADS
REC_FWD_GROUPS = DN_HEADS // REC_HEADS


def _rec_specs(time_block):
    nr = REC_ROWS // DN_CHUNK
    wide = pl.BlockSpec((REC_HEADS, REC_ROWS, LANES), lambda g, b: (g, time_block(g, b), 0))
    half = pl.BlockSpec((REC_HEADS, REC_ROWS, DN_CHUNK), lambda g, b: (g, time_block(g, b), 0))
    egl = pl.BlockSpec((REC_HEADS, nr, 1, LANES), lambda g, b: (g, time_block(g, b), 0, 0))
    state = pl.BlockSpec((REC_HEADS, nr, DN_HEAD_DIM, DN_HEAD_DIM), lambda g, b: (g, time_block(g, b), 0, 0))
    return wide, half, egl, state


def _rec_head_cols(g):
    return jnp.where(g < REC_FWD_GROUPS, g, g - REC_FWD_GROUPS)


def _dn_rec_fwd(u, w, qk, qd, kd, egl):
    t = u.shape[1]
    nb = t // REC_ROWS
    nr = REC_ROWS // DN_CHUNK
    nc = t // DN_CHUNK

    def time_block(g, b):
        return jnp.where(g < REC_FWD_GROUPS, b, nb - 1 - b)

    wide, half, egl_spec, state = _rec_specs(time_block)
    o_spec = pl.BlockSpec((1, REC_ROWS, REC_HEADS * LANES),
                          lambda g, b: (jnp.where(g < REC_FWD_GROUPS, 0, 1), time_block(g, b), _rec_head_cols(g)))

    def body(u_ref, w_ref, qk_ref, qd_ref, kd_ref, egl_ref, o_ref, vn_ref, s_ref, s_scr):
        fwd = pl.program_id(0) < REC_FWD_GROUPS

        @pl.when(pl.program_id(1) == 0)
        def _():
            s_scr[...] = jnp.zeros_like(s_scr)

        def run(order):
            heads = range(REC_HEADS)
            s = [s_scr[j] for j in heads]
            for ce in order:
                rows = slice(ce * DN_CHUNK, (ce + 1) * DN_CHUNK)
                vn = [u_ref[j, rows, :] - _bnn(w_ref[j, rows, :], s[j]) for j in heads]
                o = [_bnn(qd_ref[j, rows, :], s[j]) + _bnn(qk_ref[j, rows, :], vn[j]) for j in heads]
                nxt = [s[j] * egl_ref[j, ce] + _btn(kd_ref[j, rows, :], vn[j]) for j in heads]
                for j in heads:
                    s_ref[j, ce] = s[j]
                    vn_ref[j, rows, :] = vn[j].astype(_MXU)
                    o_ref[0, rows, j * LANES:(j + 1) * LANES] = o[j]
                s = nxt
            for j in heads:
                s_scr[j] = s[j]

        pl.when(fwd)(lambda: run(range(nr)))
        pl.when(jnp.logical_not(fwd))(lambda: run(range(nr - 1, -1, -1)))

    return pl.pallas_call(
        body, name="dn_rec_fwd", grid=(REC_GROUPS, nb), in_specs=[wide, wide, half, wide, wide, egl_spec],
        out_specs=[o_spec, wide, state],
        out_shape=[_sds((2, t, DN_WIDTH)), _sds((N_HD, t, LANES), _MXU), _sds((N_HD, nc, DN_HEAD_DIM, DN_HEAD_DIM))],
        scratch_shapes=[pltpu.VMEM((REC_HEADS, DN_HEAD_DIM, DN_HEAD_DIM), F32)],
        compiler_params=_params(("parallel", "arbitrary")),
    )(u, w, qk, qd, kd, egl)


def _dn_rec_bwd(w, qk, qd, kd, egl, vn, states, do):
    t = w.shape[1]
    nb = t // REC_ROWS
    nr = REC_ROWS // DN_CHUNK
    nc = t // DN_CHUNK

    def time_block(g, b):
        return jnp.where(g < REC_FWD_GROUPS, nb - 1 - b, b)

    wide, half, egl_spec, state = _rec_specs(time_block)
    do_spec = pl.BlockSpec((REC_ROWS, REC_HEADS * LANES), lambda g, b: (time_block(g, b), _rec_head_cols(g)))

    def body(w_ref, qk_ref, qd_ref, kd_ref, egl_ref, vn_ref, s_ref, do_ref,
             du_ref, dw_ref, dqk_ref, dqd_ref, dkd_ref, degl_ref, ds_scr):
        fwd = pl.program_id(0) < REC_FWD_GROUPS

        @pl.when(pl.program_id(1) == 0)
        def _():
            ds_scr[...] = jnp.zeros_like(ds_scr)

        def run(order):
            heads = range(REC_HEADS)
            ds = [ds_scr[j] for j in heads]
            for ce in order:
                rows = slice(ce * DN_CHUNK, (ce + 1) * DN_CHUNK)
                s = [s_ref[j, ce] for j in heads]
                do_c = [do_ref[rows, j * LANES:(j + 1) * LANES] for j in heads]
                vn_c = [vn_ref[j, rows, :] for j in heads]
                dvn = [_btn(qk_ref[j, rows, :], do_c[j]) + _bnn(kd_ref[j, rows, :], ds[j]) for j in heads]
                nxt = [ds[j] * egl_ref[j, ce] + _btn(qd_ref[j, rows, :], do_c[j]) - _btn(w_ref[j, rows, :], dvn[j])
                       for j in heads]
                for j in heads:
                    du_ref[j, rows, :] = dvn[j]
                    dw_ref[j, rows, :] = -_bnt(dvn[j], s[j])
                for j in heads:
                    dqk_ref[j, rows, :] = _bnt(do_c[j], vn_c[j])
                    dqd_ref[j, rows, :] = _bnt(do_c[j], s[j])
                for j in heads:
                    dkd_ref[j, rows, :] = _bnt(vn_c[j], ds[j])
                    degl_ref[j, ce] = jnp.sum(s[j] * ds[j], axis=0, keepdims=True)
                ds = nxt
            for j in heads:
                ds_scr[j] = ds[j]

        pl.when(fwd)(lambda: run(range(nr - 1, -1, -1)))
        pl.when(jnp.logical_not(fwd))(lambda: run(range(nr)))

    big = _sds((N_HD, t, LANES))
    return pl.pallas_call(
        body, name="dn_rec_bwd", grid=(REC_GROUPS, nb), in_specs=[wide, half, wide, wide, egl_spec, wide, state, do_spec],
        out_specs=[wide, wide, half, wide, wide, egl_spec],
        out_shape=[big, big, _sds((N_HD, t, DN_CHUNK)), big, big, _sds((N_HD, nc, 1, LANES))],
        scratch_shapes=[pltpu.VMEM((REC_HEADS, DN_HEAD_DIM, DN_HEAD_DIM), F32)],
        compiler_params=_params(("parallel", "arbitrary")),
    )(w, qk, qd, kd, egl, vn, states, do)


def _post_fn(of, ob, z, gain):
    o = of + ob
    return o * lax.rsqrt(jnp.mean(o * o, axis=-1, keepdims=True) + RMS_EPS) * gain * _silu(z)


def _post_specs():
    o_spec = [pl.BlockSpec((1, ROW_TILE, DN_WIDTH), functools.partial(lambda i, d: (d, i, 0), d=d)) for d in (0, 1)]
    return o_spec, _cols(DN_WIDTH, C_Z), _rows(DN_WIDTH), _full((1, LANES))


def _post_fwd(o2, proj, gain):
    t = proj.shape[0]
    o_spec, z_spec, wide, gain_spec = _post_specs()

    def body(of_ref, ob_ref, z_ref, g_ref, out_ref):
        for h in range(DN_HEADS):
            cols = slice(h * LANES, (h + 1) * LANES)
            out_ref[:, cols] = _post_fn(of_ref[0, :, cols], ob_ref[0, :, cols], z_ref[:, cols], g_ref[...]).astype(_MXU)

    return pl.pallas_call(
        body, name="post_fwd", grid=(t // ROW_TILE,), in_specs=o_spec + [z_spec, gain_spec], out_specs=wide,
        out_shape=_sds((t, DN_WIDTH), _MXU), compiler_params=_params(("parallel",)),
    )(o2, o2, proj, gain)


def _post_bwd(o2, proj, gain, dout):
    t = proj.shape[0]
    o_spec, z_spec, wide, gain_spec = _post_specs()

    def body(of_ref, ob_ref, z_ref, g_ref, d_ref, do_ref, dz_ref, dg_ref):
        dg_sum = jnp.zeros((1, LANES), F32)
        for h in range(DN_HEADS):
            cols = slice(h * LANES, (h + 1) * LANES)
            _, vjp = jax.vjp(_post_fn, of_ref[0, :, cols], ob_ref[0, :, cols], z_ref[:, cols], g_ref[...])
            do, _, dz, dg = vjp(d_ref[:, cols])
            do_ref[:, cols] = do
            dz_ref[:, cols] = dz.astype(_MXU)
            dg_sum = dg_sum + dg
        _accumulate(dg_ref, dg_sum, pl.program_id(0))

    return pl.pallas_call(
        body, name="post_bwd", grid=(t // ROW_TILE,), in_specs=o_spec + [z_spec, gain_spec, wide],
        out_specs=[wide, wide, gain_spec], out_shape=[_sds((t, DN_WIDTH)), _sds((t, DN_WIDTH), _MXU), _sds((1, LANES))],
        compiler_params=_params(("arbitrary",)),
    )(o2, o2, proj, gain, dout)


def _rope(x, cos, sin):
    lane = lax.broadcasted_iota(jnp.int32, x.shape, 1)
    first = (lane & (SW_HEAD_DIM - 1)) < SW_HEAD_DIM // 2
    rot = jnp.where(first, -pltpu.roll(x, LANES - SW_HEAD_DIM // 2, 1), pltpu.roll(x, SW_HEAD_DIM // 2, 1))
    return x * cos + rot * sin


def _rope_apply(q, k, q_cols, k_cols, cos, sin, name, dtype):
    t = cos.shape[0]
    qw, kw = SW_HEADS * SW_HEAD_DIM, SW_KV_WIDTH

    def body(q_ref, k_ref, c_ref, s_ref, qo_ref, ko_ref):
        c, s = c_ref[...], s_ref[...]
        for j in range(qw // LANES):
            cols = slice(j * LANES, (j + 1) * LANES)
            qo_ref[:, cols] = _rope(q_ref[:, cols], c, s).astype(dtype)
        for j in range(kw // LANES):
            cols = slice(j * LANES, (j + 1) * LANES)
            ko_ref[:, cols] = _rope(k_ref[:, cols], c, s).astype(dtype)

    return pl.pallas_call(
        body, name=name, grid=(t // ROW_TILE,), in_specs=[_cols(qw, q_cols), _cols(kw, k_cols), _rows(LANES), _rows(LANES)],
        out_specs=[_rows(qw), _rows(kw)], out_shape=[_sds((t, qw), dtype), _sds((t, kw), dtype)],
        compiler_params=_params(("parallel",)),
    )(q, k, cos, sin)


def _attn_core(qs, kb, vb, sink, mask):
    s = _bnt(qs, kb) * (SW_HEAD_DIM ** -0.5)
    s = jnp.where(mask, s, -1e30)
    m = lax.stop_gradient(jnp.maximum(jnp.max(s, axis=1, keepdims=True), sink))
    e = jnp.exp(s - m)
    den = jnp.sum(e, axis=1, keepdims=True) + jnp.exp(sink - m)
    return _bnn(e / den, vb)


def _band_mask(n, nb):
    rows = SW_GROUP * SW_BLOCK
    i = lax.broadcasted_iota(jnp.int32, (rows, 3 * SW_BLOCK), 0) & (SW_BLOCK - 1)
    j = lax.broadcasted_iota(jnp.int32, (rows, 3 * SW_BLOCK), 1)
    near = (j - i >= 0) & (j - i <= 2 * SW_BLOCK)
    lo = jnp.where(n == 0, SW_BLOCK, 0)
    hi = jnp.where(n == nb - 1, 2 * SW_BLOCK, 3 * SW_BLOCK)
    return near & (j >= lo) & (j < hi)


def _band_specs(nb, v_cols):
    def spec(width, base, shift):
        return pl.BlockSpec((SW_BLOCK, width), lambda n: (jnp.clip(n + shift, 0, nb - 1), base // width))
    k_specs = [spec(SW_KV_WIDTH, 0, s) for s in (-1, 0, 1)]
    v_specs = [spec(SW_KV_WIDTH, v_cols, s) for s in (-1, 0, 1)]
    return k_specs, v_specs


def _head_cols(kv, g):
    h = kv * SW_GROUP + g
    return slice(h * SW_HEAD_DIM, (h + 1) * SW_HEAD_DIM)


def _kv_batches(q_ref, kb, vb, s_ref):
    kvs = range(SW_KV_HEADS)
    cols = lambda kv: slice(kv * SW_HEAD_DIM, (kv + 1) * SW_HEAD_DIM)
    qs = jnp.stack([jnp.concatenate([q_ref[:, _head_cols(kv, g)] for g in range(SW_GROUP)], axis=0) for kv in kvs])
    sinks = jnp.stack([jnp.concatenate([jnp.broadcast_to(s_ref[kv * SW_GROUP + g], (SW_BLOCK, 1)) for g in range(SW_GROUP)],
                                       axis=0) for kv in kvs])
    return qs, jnp.stack([kb[:, cols(kv)] for kv in kvs]), jnp.stack([vb[:, cols(kv)] for kv in kvs]), sinks


def _attn_fwd(qr, kr, proj, sinks):
    t = qr.shape[0]
    nb = t // SW_BLOCK
    qw = SW_HEADS * SW_HEAD_DIM
    k_specs, v_specs = _band_specs(nb, C_VSW)
    q_spec = pl.BlockSpec((SW_BLOCK, qw), lambda n: (n, 0))

    def body(q_ref, k0, k1, k2, v0, v1, v2, s_ref, o_ref):
        mask = _band_mask(pl.program_id(0), nb)
        kb = jnp.concatenate([k0[...], k1[...], k2[...]], axis=0)
        vb = jnp.concatenate([v0[...], v1[...], v2[...]], axis=0)
        qs, kbs, vbs, sinks_ = _kv_batches(q_ref, kb, vb, s_ref)
        o = jax.vmap(functools.partial(_attn_core, mask=mask))(qs, kbs, vbs, sinks_)
        for kv in range(SW_KV_HEADS):
            for g in range(SW_GROUP):
                o_ref[:, _head_cols(kv, g)] = o[kv, g * SW_BLOCK:(g + 1) * SW_BLOCK, :].astype(_MXU)

    return pl.pallas_call(
        body, name="attn_fwd", grid=(nb,), in_specs=[q_spec] + k_specs + v_specs + [_full((SW_HEADS, 1, 1))],
        out_specs=q_spec, out_shape=_sds((t, qw), _MXU), compiler_params=_params(("parallel",)),
    )(qr, kr, kr, kr, proj, proj, proj, sinks)


def _attn_bwd(qr, kr, proj, sinks, do):
    t = qr.shape[0]
    nb = t // SW_BLOCK
    qw = SW_HEADS * SW_HEAD_DIM
    k_specs, v_specs = _band_specs(nb, C_VSW)
    q_spec = pl.BlockSpec((SW_BLOCK, qw), lambda n: (n, 0))
    part = pl.BlockSpec((1, 3 * SW_BLOCK, SW_KV_WIDTH), lambda n: (n, 0, 0))

    def body(q_ref, k0, k1, k2, v0, v1, v2, s_ref, do_ref, dq_ref, dk_ref, dv_ref, ds_ref):
        mask = _band_mask(pl.program_id(0), nb)
        kb = jnp.concatenate([k0[...], k1[...], k2[...]], axis=0).astype(F32)
        vb = jnp.concatenate([v0[...], v1[...], v2[...]], axis=0)

        @pl.when(pl.program_id(0) == 0)
        def _():
            ds_ref[...] = jnp.zeros_like(ds_ref)

        qs, kbs, vbs, sinks_ = _kv_batches(q_ref, kb, vb, s_ref)
        dos = jnp.stack([jnp.concatenate([do_ref[:, _head_cols(kv, g)] for g in range(SW_GROUP)], axis=0)
                         for kv in range(SW_KV_HEADS)])

        def head_bwd(q_, k_, v_, sink_, do_):
            _, vjp = jax.vjp(functools.partial(_attn_core, mask=mask), q_, k_, v_, sink_)
            return vjp(do_)

        dqs, dkb, dvb, dsink = jax.vmap(head_bwd)(qs.astype(F32), kbs, vbs, sinks_, dos)
        for kv in range(SW_KV_HEADS):
            kvc = slice(kv * SW_HEAD_DIM, (kv + 1) * SW_HEAD_DIM)
            dk_ref[0, :, kvc] = dkb[kv]
            dv_ref[0, :, kvc] = dvb[kv]
            for g in range(SW_GROUP):
                rows = slice(g * SW_BLOCK, (g + 1) * SW_BLOCK)
                dq_ref[:, _head_cols(kv, g)] = dqs[kv, rows, :]
                ds_ref[kv * SW_GROUP + g] += jnp.sum(dsink[kv, rows, :], axis=0, keepdims=True)

    parts = _sds((nb, 3 * SW_BLOCK, SW_KV_WIDTH))
    return pl.pallas_call(
        body, name="attn_bwd", grid=(nb,), in_specs=[q_spec] + k_specs + v_specs + [_full((SW_HEADS, 1, 1)), q_spec],
        out_specs=[q_spec, part, part, _full((SW_HEADS, 1, 1))], out_shape=[_sds((t, qw)), parts, parts, _sds((SW_HEADS, 1, 1))],
        compiler_params=_params(("arbitrary",)),
    )(qr, kr, kr, kr, proj, proj, proj, sinks, do)


def _band_sum(parts, name, dtype):
    nb = parts.shape[0]
    w = parts.shape[2]

    def spec(shift, slot):
        return pl.BlockSpec((1, SW_BLOCK, w), lambda m: (jnp.clip(m + shift, 0, nb - 1), slot, 0))

    def body(prev_ref, own_ref, next_ref, o_ref):
        m = pl.program_id(0)
        total = own_ref[0] + jnp.where(m > 0, prev_ref[0], 0.0) + jnp.where(m < nb - 1, next_ref[0], 0.0)
        o_ref[...] = total.astype(dtype)

    return pl.pallas_call(
        body, name=name, grid=(nb,), in_specs=[spec(-1, 2), spec(0, 1), spec(1, 0)],
        out_specs=pl.BlockSpec((SW_BLOCK, w), lambda m: (m, 0)), out_shape=_sds((nb * SW_BLOCK, w), dtype),
        compiler_params=_params(("parallel",)),
    )(parts, parts, parts)


def _gate_rows(gbo):
    t = gbo.shape[0]
    return gbo[:, N_HD:2 * N_HD].T.reshape(N_HD, t // DN_CHUNK, 1, DN_CHUNK)


def _gate_rows_t(dgc_r):
    t = dgc_r.shape[1] * DN_CHUNK
    return jnp.pad(dgc_r.reshape(N_HD, t).T, ((0, 0), (N_HD, LANES - 2 * N_HD)))


def _layer_fwd(x, xb, w, cos, sin, carry=None):
    proj = _mm(xb, w["wm"], "nn", name="proj")
    ba = _mm(xb, w["wba"], "nn", name="proj_gates")
    qn, kn, vv = [_prep_fwd(proj, w["conv"], kind) for kind in range(3)]
    gbo, beta_c, gc_c = _gb_fwd(ba, w["alog"], w["dtb"])
    gc_r = _gate_rows(gbo)
    u, wk, qk, qd, kd, egl, tinv, *carried = _dn_local_fwd(qn, kn, vv, beta_c, gc_c, gc_r, carry=carry)
    o2, vn, states = _dn_rec_fwd(u, wk, qk, qd, kd, egl)
    o_dn = _post_fwd(o2, proj, w["dnw"])
    qr, kr = _rope_apply(proj, proj, C_QSW, C_KSW, cos, sin, "rope_fwd", _MXU)
    o_sw = _attn_fwd(qr, kr, proj, w["sinks"])
    ya = _mm(o_dn, w["wa"], "nn", name="branch_a")
    yb = _mm(o_sw, w["wb"], "nn", name="branch_b")
    merged = _merge_fwd(proj, ya, yb)
    mix = _mm(merged, w["wo"], "nn", name="mix_out")
    x1, x1b = _ln_fwd(x, mix, w["ln1g"], w["ln1b"], "ln1_fwd")
    gu = _mm(x1b, w["wgu"], "nn", name="ffn_up")
    h = _swiglu_fwd(gu)
    f = _mm(h, w["wd"], "nn", name="ffn_down")
    x2, x2b = _ln_fwd(x1, f, w["ln2g"], w["ln2b"], "ln2_fwd")
    res = dict(x=x, xb=xb, proj=proj, ba=ba, qn=qn, kn=kn, vv=vv, beta_c=beta_c, gc_c=gc_c, gc_r=gc_r, wk=wk, qk=qk, qd=qd, kd=kd, egl=egl, tinv=tinv, vn=vn,
               states=states, o2=o2, o_dn=o_dn, qr=qr, kr=kr, o_sw=o_sw, ya=ya, yb=yb, merged=merged, mix=mix, x1=x1, x1b=x1b,
               gu=gu, h=h, f=f)
    return x2, x2b, res, (carried or None)


def _layer_bwd(dx2, w, r, cos, sin, carry=None):
    dx1, df, dln2g, dln2b = _ln_bwd(r["x1"], r["f"], w["ln2g"], w["ln2b"], dx2, "ln2_bwd")
    dh = _mm(df, w["wd"], "nt", name="d_ffn_hidden")
    dwd = _mm(r["h"], df, "tn", name="dw_ffn_down", out_dtype=_MXU)
    dgu = _swiglu_bwd(r["gu"], dh)
    dwgu = _mm(r["x1b"], dgu, "tn", name="dw_ffn_up", out_dtype=_MXU)
    dx1 = _mm(dgu, w["wgu"], "nt", name="dx_ffn", add=dx1)
    dx, dmix, dln1g, dln1b = _ln_bwd(r["x"], r["mix"], w["ln1g"], w["ln1b"], dx1, "ln1_bwd")
    dmerged = _mm(dmix, w["wo"], "nt", name="d_merged")
    dwo = _mm(r["merged"], dmix, "tn", name="dw_mix_out", out_dtype=_MXU)
    dga, dgb, dya, dyb = _merge_bwd(r["proj"], r["ya"], r["yb"], dmerged)
    dwa = _mm(r["o_dn"], dya, "tn", name="dw_branch_a", out_dtype=_MXU)
    do_dn = _mm(dya, w["wa"], "nt", name="d_branch_a")
    dwb = _mm(r["o_sw"], dyb, "tn", name="dw_branch_b", out_dtype=_MXU)
    do_sw = _mm(dyb, w["wb"], "nt", name="d_branch_b")
    do, dz, ddnw = _post_bwd(r["o2"], r["proj"], w["dnw"], do_dn)
    du, dwk, dqk, dqd, dkd, degl = _dn_rec_bwd(r["wk"], r["qk"], r["qd"], r["kd"], r["egl"], r["vn"], r["states"], do)
    dq3, dk3, dv3, dbeta_c, dgc_c, dgc_r, *carried = _dn_local_bwd(r["qn"], r["kn"], r["vv"], r["beta_c"], r["gc_c"], r["gc_r"],
                                                                   r["tinv"], du, dwk, dqk, dqd, dkd, degl, carry=carry)
    dqkv, dconv = zip(*[_prep_bwd(r["proj"], w["conv"], d2, kind) for kind, d2 in enumerate((dq3, dk3, dv3))])
    dconv = jnp.concatenate(dconv, axis=1)
    dba, dalog, ddtb = _gb_bwd(r["ba"], w["alog"], w["dtb"], dbeta_c, dgc_c, _gate_rows_t(dgc_r))
    dqr, dkparts, dvparts, dsinks = _attn_bwd(r["qr"], r["kr"], r["proj"], w["sinks"], do_sw)
    dkr = _band_sum(dkparts, "attn_dk_sum", F32)
    dv = _band_sum(dvparts, "attn_dv_sum", _MXU)
    dq_sw, dk_sw = _rope_apply(dqr, dkr, 0, 0, cos, -sin, "rope_bwd", _MXU)
    dproj = jnp.concatenate([*dqkv, dz, dq_sw, dga, dgb, dk_sw, dv], axis=1)
    dwm = _mm(r["xb"], dproj, "tn", name="dw_proj", out_dtype=_MXU)
    dwba = _mm(r["xb"], dba, "tn", name="dw_proj_gates", out_dtype=_MXU)
    dx = _mm(dproj, w["wm"], "nt", name="dx_proj", add=dx)
    dx = _mm(dba, w["wba"], "nt", name="dx_proj_gates", add=dx)
    grads = dict(wm=dwm, wba=dwba, conv=dconv, alog=dalog, dtb=ddtb, dnw=ddnw, sinks=dsinks, wa=dwa, wb=dwb, wo=dwo,
                 ln1g=dln1g, ln1b=dln1b, wgu=dwgu, wd=dwd, ln2g=dln2g, ln2b=dln2b)
    return dx, grads, (carried or None)


def _rope_tables(t):
    half = SW_HEAD_DIM // 2
    inv_freq = ROPE_THETA ** (-jnp.arange(half, dtype=F32) / half)
    ang = jnp.arange(t, dtype=F32)[:, None] * inv_freq[None, :]
    return jnp.tile(jnp.cos(ang), (1, LANES // half)), jnp.tile(jnp.sin(ang), (1, LANES // half))


def _trunk(x, target, n_layers, layer_weights, fwd_carry, grads_done):
    cos, sin = _rope_tables(x.shape[0])
    xb = x.astype(_MXU)
    saved, weights, carried = [], [], None
    for i in range(n_layers):
        w = layer_weights(i, carried)
        x, xb, res, carried = _layer_fwd(x, xb, w, cos, sin, carry=fwd_carry(i))
        saved.append(res)
        weights.append(w)
    sq, dx = _loss_head(x, target)
    carry = None
    for i in reversed(range(n_layers)):
        dx, grads, carried = _layer_bwd(dx, weights[i], saved[i], cos, sin, carry=carry)
        carry = grads_done(i, grads, carried)
    return sq, dx, carry


N_CHIPS = 4


def _mesh_pos():
    return lax.axis_index("x"), lax.axis_index("y"), lax.axis_index("c")


def _other_chips(x, y):
    return [(1 - x, y), (x, 1 - y), (1 - x, 1 - y)]


def _remote_copy(src, dst, sems, k, to):
    send_sems, recv_sems, base = sems
    return pltpu.make_async_remote_copy(src_ref=src, dst_ref=dst, send_sem=send_sems.at[base + k],
                                        recv_sem=recv_sems.at[base + k], device_id=to, device_id_type=pl.DeviceIdType.MESH)


def _exchange_sems(n_arrays, per_array):
    return [pltpu.SemaphoreType.DMA((n_arrays * per_array,)), pltpu.SemaphoreType.DMA((n_arrays * per_array,)),
            pltpu.SemaphoreType.DMA((n_arrays,))]


def _comm_call(body, name, out_shapes, per_array, operands):
    n = len(operands)
    hbm = pl.BlockSpec(memory_space=pl.ANY)

    def flat_body(*refs):
        body(refs[:n], refs[n:2 * n], *refs[2 * n:])

    return pl.pallas_call(
        flat_body, name=name, in_specs=[hbm] * n, out_specs=[hbm] * n, out_shape=list(out_shapes),
        scratch_shapes=_exchange_sems(n, per_array), compiler_params=pltpu.CompilerParams(has_side_effects=True),
    )(*operands)


class _Gather:
    n_sems = N_DEV - 1

    @staticmethod
    def out_shape(block):
        return _sds((N_DEV,) + block.shape, block.dtype)

    @staticmethod
    def _own(x_ref, o_ref, sems, local_sem):
        x, y, c = _mesh_pos()
        mine = o_ref.at[4 * x + 2 * y + c]
        first = [_remote_copy(x_ref, mine, sems, 0, (x, y, 1 - c))]
        first += [_remote_copy(x_ref, mine, sems, 1 + j, (*chip, c)) for j, chip in enumerate(_other_chips(x, y))]
        return pltpu.make_async_copy(x_ref, mine, local_sem), first

    @classmethod
    def start(cls, x_ref, o_ref, sems, local_sem):
        mine, first = cls._own(x_ref, o_ref, sems, local_sem)
        mine.start()
        for cp in first:
            cp.start()

    @classmethod
    def finish(cls, x_ref, o_ref, sems, local_sem):
        x, y, c = _mesh_pos()
        sibling = (x, y, 1 - c)
        chips = _other_chips(x, y)
        slot = lambda px, py, pc: o_ref.at[4 * px + 2 * py + pc]
        mine, first = cls._own(x_ref, o_ref, sems, local_sem)
        passed = [_remote_copy(slot(*chip, c), slot(*chip, c), sems, 4 + j, sibling) for j, chip in enumerate(chips)]
        for j, chip in enumerate(chips):
            _remote_copy(x_ref, slot(*chip, c), sems, 1 + j, sibling).wait_recv()
            passed[j].start()
        _remote_copy(x_ref, slot(x, y, 1 - c), sems, 0, sibling).wait_recv()
        for j, chip in enumerate(chips):
            _remote_copy(x_ref, slot(*chip, 1 - c), sems, 4 + j, sibling).wait_recv()
        for cp in first + passed:
            cp.wait_send()
        mine.wait()


class _ChipExchange:
    n_sems = N_CHIPS - 1

    @staticmethod
    def out_shape(parts):
        return _sds(parts.shape, parts.dtype)

    @staticmethod
    def _own(x_ref, o_ref, sems, local_sem):
        x, y, c = _mesh_pos()
        me = 2 * x + y
        sent = [_remote_copy(x_ref.at[2 * cx + cy], o_ref.at[me], sems, j, (cx, cy, c))
                for j, (cx, cy) in enumerate(_other_chips(x, y))]
        return pltpu.make_async_copy(x_ref.at[me], o_ref.at[me], local_sem), sent

    @classmethod
    def start(cls, x_ref, o_ref, sems, local_sem):
        mine, sent = cls._own(x_ref, o_ref, sems, local_sem)
        mine.start()
        for cp in sent:
            cp.start()

    @classmethod
    def finish(cls, x_ref, o_ref, sems, local_sem):
        x, y, c = _mesh_pos()
        mine, sent = cls._own(x_ref, o_ref, sems, local_sem)
        for j, (cx, cy) in enumerate(_other_chips(x, y)):
            _remote_copy(x_ref.at[2 * x + y], o_ref.at[2 * cx + cy], sems, j, (cx, cy, c)).wait_recv()
        for cp in sent:
            cp.wait_send()
        mine.wait()


def _run_exchange(kind, phase, x_refs, o_refs, send_sems, recv_sems, local_sems):
    for i, (x_ref, o_ref) in enumerate(zip(x_refs, o_refs)):
        getattr(kind, phase)(x_ref, o_ref, (send_sems, recv_sems, i * kind.n_sems), local_sems.at[i])


def _exchange_alone(kind, operands, name):
    def body(x_refs, o_refs, *sems):
        _run_exchange(kind, "start", x_refs, o_refs, *sems)
        _run_exchange(kind, "finish", x_refs, o_refs, *sems)

    return _comm_call(body, name, [kind.out_shape(a) for a in operands], kind.n_sems, operands)


def _all_gather(block, name):
    return _exchange_alone(_Gather, [block], name)[0]


def _carried(kind, operands, body, n_in, n_out, grid):
    hbm = pl.BlockSpec(memory_space=pl.ANY)
    n_x = len(operands)

    def wrapped(*refs):
        ins, x_refs = refs[:n_in], refs[n_in:n_in + n_x]
        outs = refs[n_in + n_x:n_in + n_x + n_out]
        o_refs = refs[n_in + n_x + n_out:n_in + 2 * n_x + n_out]
        sems = refs[n_in + 2 * n_x + n_out:n_in + 2 * n_x + n_out + 3]
        rest = refs[n_in + 2 * n_x + n_out + 3:]
        first, last = None, None
        for axis, size in enumerate(grid):
            at0, at1 = pl.program_id(axis) == 0, pl.program_id(axis) == size - 1
            first = at0 if first is None else first & at0
            last = at1 if last is None else last & at1
        pl.when(first)(lambda: _run_exchange(kind, "start", x_refs, o_refs, *sems))
        body(*ins, *outs, *rest)
        pl.when(last)(lambda: _run_exchange(kind, "finish", x_refs, o_refs, *sems))

    return wrapped, [hbm] * n_x, [hbm] * n_x, [kind.out_shape(a) for a in operands], _exchange_sems(n_x, kind.n_sems)


def _sibling_swap(parts, name):
    def body(x_refs, o_refs, send_sems, recv_sems, local_sems):
        x, y, c = _mesh_pos()
        copies = [_remote_copy(x_ref.at[2 * q + (1 - c)], o_ref.at[q], (send_sems, recv_sems, i * N_CHIPS), q, (x, y, 1 - c))
                  for i, (x_ref, o_ref) in enumerate(zip(x_refs, o_refs)) for q in range(N_CHIPS)]
        for cp in copies:
            cp.start()
        for cp in copies:
            cp.wait()

    return _comm_call(body, name, [_sds((N_CHIPS,) + p.shape[1:], p.dtype) for p in parts], N_CHIPS, parts)


def _pair_sum(a, b, name):
    n, rows, cols = a.shape
    tr = rows if rows <= 512 else _row_tile(rows, 2048)
    blk = pl.BlockSpec((1, tr, cols), lambda q, i: (q, i, 0))

    def body(a_ref, b_ref, o_ref):
        o_ref[...] = (a_ref[...].astype(F32) + b_ref[...].astype(F32)).astype(o_ref.dtype)

    return pl.pallas_call(
        body, name=name, grid=(n, rows // tr), in_specs=[blk, blk], out_specs=blk, out_shape=_sds(a.shape, a.dtype),
        compiler_params=_params(("parallel", "parallel")),
    )(a, b)


def _chip_sums(parts, name):
    c = lax.axis_index("c")
    from_sibling = _sibling_swap(parts, "swap_" + name)
    own = [lax.dynamic_index_in_dim(p.reshape((N_CHIPS, 2) + p.shape[1:]), c, axis=1, keepdims=False) for p in parts]
    return [_pair_sum(a, b, f"pair_sum_{name}_{k}") for k, (a, b) in enumerate(zip(own, from_sibling))]


def _reduce_to_owner(parts, name):
    return _exchange_alone(_ChipExchange, _chip_sums([parts], name), "exchange_" + name)[0]


def _sum_adamw(parts, w, m, v, name):
    rows, cols = w.shape
    n_parts = parts.shape[0]
    tr = rows if rows <= 512 else _row_tile(rows)
    blk = pl.BlockSpec((tr, cols), lambda i: (i, 0))

    def body(p_ref, w_ref, m_ref, v_ref, g_ref, d_ref, nm_ref, nv_ref):
        g = p_ref[0].astype(F32)
        for i in range(1, n_parts):
            g = g + p_ref[i].astype(F32)
        nm = ADAM_B1 * m_ref[...] + (1.0 - ADAM_B1) * g
        nv = ADAM_B2 * v_ref[...] + (1.0 - ADAM_B2) * jnp.square(g)
        m_hat = nm / (1.0 - ADAM_B1 ** ADAM_STEP)
        v_hat = nv / (1.0 - ADAM_B2 ** ADAM_STEP)
        g_ref[...] = g
        d_ref[...] = -ADAM_LR * (m_hat / (jnp.sqrt(v_hat) + ADAM_EPS) + ADAM_WD * w_ref[...])
        nm_ref[...] = nm
        nv_ref[...] = nv

    return pl.pallas_call(
        body, name=name, grid=(rows // tr,), in_specs=[pl.BlockSpec((n_parts, tr, cols), lambda i: (0, i, 0)), blk, blk, blk],
        out_specs=[blk] * 4, out_shape=[_sds((rows, cols))] * 4, compiler_params=_params(("parallel",)),
    )(parts, w, m, v)


def _row_tile(rows, pref=256):
    t = pref
    while t >= 8:
        if rows % t == 0:
            return t
        t //= 2
    return rows


def _gathered_cols(g):
    g = jnp.moveaxis(g, 0, -2)
    return g.reshape(g.shape[:-2] + (g.shape[-2] * g.shape[-1],))


def _gathered_rows(g):
    g = jnp.moveaxis(g, 0, -3)
    return g.reshape(g.shape[:-3] + (g.shape[-3] * g.shape[-2], g.shape[-1]))


def _col_parts(full):
    c = full.shape[-1]
    return jnp.moveaxis(full.reshape(full.shape[:-1] + (N_DEV, c // N_DEV)), -2, 0)


def _row_parts(full):
    rows, c = full.shape[-2:]
    return jnp.moveaxis(full.reshape(full.shape[:-2] + (N_DEV, rows // N_DEV, c)), -3, 0)


def _w_in_split(w_in):
    s = lambda a, n: w_in[..., a:a + n]
    main = jnp.concatenate([s(R_QKV, 3072), s(R_Z, 1024), s(R_QSW, 1024), s(R_G, 2048), s(R_KSW, 256), s(R_VSW, 256)], axis=-1)
    gates = jnp.pad(s(R_BA, 2 * N_HD), [(0, 0)] * (w_in.ndim - 1) + [(0, LANES - 2 * N_HD)])
    return main, gates


def _w_in_join(dmain, dgates):
    s = lambda a, n: dmain[..., a:a + n]
    return jnp.concatenate([s(C_QKV, 3072), s(C_Z, 1024), dgates[..., :2 * N_HD], s(C_QSW, 1024), s(C_KSW, 256), s(C_VSW, 256),
                            s(C_GA, 2048)], axis=-1)


def _lane_row(a, offset):
    l, n = a.shape
    return jnp.pad(a, ((0, 0), (offset, LANES - offset - n)))[:, None, :]


def kernel(x, w_in, conv_w, a_log, dt_bias, dn_norm_w, sinks, w_branch_a, w_branch_b, w_out, ln1_g, ln1_b, w_gate_up, w_down, ln2_g, ln2_b, loss_target, m_w_in, m_conv_w, m_a_log, m_dt_bias, m_dn_norm_w, m_sinks, m_w_branch_a, m_w_branch_b, m_w_out, m_ln1_g, m_ln1_b, m_w_gate_up, m_w_down, m_ln2_g, m_ln2_b, v_w_in, v_conv_w, v_a_log, v_dt_bias, v_dn_norm_w, v_sinks, v_w_branch_a, v_w_branch_b, v_w_out, v_ln1_g, v_ln1_b, v_w_gate_up, v_w_down, v_ln2_g, v_ln2_b):
    l = DEPTH
    bf = lambda a: a.astype(_MXU)
    shards = [bf(w_in), bf(w_gate_up), bf(w_branch_a), bf(w_branch_b), bf(w_out), bf(w_down)]
    first = _exchange_alone(_Gather, [s[0] for s in shards], "gather_layer_0")
    conv_full = _gathered_cols(_all_gather(conv_w, "gather_conv_w"))
    row = lambda a: a[:, None, :]
    small = dict(
        conv=jnp.pad(conv_full, ((0, 0), (0, 8 - DN_CONV), (0, 0))), alog=_lane_row(a_log.reshape(l, N_HD), N_HD),
        dtb=_lane_row(dt_bias.reshape(l, N_HD), N_HD), dnw=row(dn_norm_w), sinks=sinks.reshape(l, SW_HEADS, 1, 1),
        ln1g=row(ln1_g), ln1b=row(ln1_b), ln2g=row(ln2_g), ln2b=row(ln2_b))

    def layer_weights(i, carried):
        s_in, s_gu, s_a, s_b, s_o, s_d = first if i == 0 else carried
        wm, wba = _w_in_split(_gathered_cols(s_in))
        return dict(wm=wm, wba=wba, wgu=_gathered_cols(s_gu), wa=_gathered_rows(s_a), wb=_gathered_rows(s_b),
                    wo=_gathered_rows(s_o), wd=_gathered_rows(s_d), **{k: a[i] for k, a in small.items()})

    def fwd_carry(i):
        return (_Gather, [s[i + 1] for s in shards]) if i + 1 < l else None

    layer_grads, received, waiting = [None] * l, [None] * l, []

    def grads_done(i, g_i, carried):
        if waiting:
            received[waiting.pop()] = carried
        layer_grads[i] = g_i
        parts = [_col_parts(_w_in_join(g_i["wm"], g_i["wba"])), _col_parts(g_i["wgu"]), _row_parts(g_i["wa"]),
                 _row_parts(g_i["wb"]), _row_parts(g_i["wo"]), _row_parts(g_i["wd"])]
        waiting.append(i)
        return (_ChipExchange, _chip_sums(parts, f"layer_{i}"))

    sq, dx, last = _trunk(x[0], loss_target[0], l, layer_weights, fwd_carry, grads_done)
    received[waiting.pop()] = _exchange_alone(last[0], last[1], "exchange_layer_0")
    loss = lax.psum(0.5 * sq[0, 0] / D_MODEL, ("x", "y", "c"))
    g = {k: jnp.stack([gi[k] for gi in layer_grads]) for k in small}

    def adamw(parts, w, m, v, name):
        rows = w.shape[0] * w.shape[1]
        flat = lambda a: a.reshape(rows, a.shape[-1])
        outs = _sum_adamw(parts.reshape(parts.shape[0], rows, w.shape[-1]), flat(w), flat(m), flat(v), "adamw_" + name)
        return [o.reshape(w.shape) for o in outs]

    got = [jnp.stack(per_layer, axis=1) for per_layer in zip(*received)]
    dconv = _reduce_to_owner(_col_parts(g["conv"][:, :DN_CONV, :]).reshape(N_DEV, l * DN_CONV, -1), "conv_w")
    results = {
        "w_in": adamw(got[0], w_in, m_w_in, v_w_in, "w_in"),
        "conv_w": adamw(dconv.reshape(N_CHIPS, l, DN_CONV, -1), conv_w, m_conv_w, v_conv_w, "conv_w"),
        "w_branch_a": adamw(got[2], w_branch_a, m_w_branch_a, v_w_branch_a, "w_branch_a"),
        "w_branch_b": adamw(got[3], w_branch_b, m_w_branch_b, v_w_branch_b, "w_branch_b"),
        "w_out": adamw(got[4], w_out, m_w_out, v_w_out, "w_out"),
        "w_gate_up": adamw(got[1], w_gate_up, m_w_gate_up, v_w_gate_up, "w_gate_up"),
        "w_down": adamw(got[5], w_down, m_w_down, v_w_down, "w_down"),
    }

    small_w = {"a_log": a_log.reshape(l, N_HD), "dt_bias": dt_bias.reshape(l, N_HD), "dn_norm_w": dn_norm_w, "sinks": sinks,
               "ln1_g": ln1_g, "ln1_b": ln1_b, "ln2_g": ln2_g, "ln2_b": ln2_b}
    small_m = {"a_log": m_a_log, "dt_bias": m_dt_bias, "dn_norm_w": m_dn_norm_w, "sinks": m_sinks, "ln1_g": m_ln1_g,
               "ln1_b": m_ln1_b, "ln2_g": m_ln2_g, "ln2_b": m_ln2_b}
    small_v = {"a_log": v_a_log, "dt_bias": v_dt_bias, "dn_norm_w": v_dn_norm_w, "sinks": v_sinks, "ln1_g": v_ln1_g,
               "ln1_b": v_ln1_b, "ln2_g": v_ln2_g, "ln2_b": v_ln2_b}
    small_g = {"a_log": g["alog"][:, 0, N_HD:2 * N_HD], "dt_bias": g["dtb"][:, 0, N_HD:2 * N_HD], "dn_norm_w": g["dnw"][:, 0, :],
               "sinks": g["sinks"].reshape(l, SW_HEADS), "ln1_g": g["ln1g"][:, 0, :], "ln1_b": g["ln1b"][:, 0, :],
               "ln2_g": g["ln2g"][:, 0, :], "ln2_b": g["ln2b"][:, 0, :]}
    names = list(small_w)
    cat = lambda d: jnp.concatenate([d[n].reshape(l, -1) for n in names], axis=1)
    widths = [small_w[n].shape[1] for n in names]
    total = sum(widths)
    padded = -(-total // LANES) * LANES
    pad = lambda a: jnp.pad(a, ((0, 8 - l), (0, padded - total)))
    got = _all_gather(pad(cat(small_g)), "gather_small_grads")
    outs = _sum_adamw(got, pad(cat(small_w)), pad(cat({n: small_m[n].reshape(l, -1) for n in names})),
                      pad(cat({n: small_v[n].reshape(l, -1) for n in names})), "adamw_small")
    off = 0
    for n, wd_ in zip(names, widths):
        shape = {"a_log": a_log.shape, "dt_bias": dt_bias.shape}.get(n, small_w[n].shape)
        results[n] = [o[:l, off:off + wd_].reshape(shape) for o in outs]
        off += wd_

    order = ["w_in", "conv_w", "a_log", "dt_bias", "dn_norm_w", "sinks", "w_branch_a", "w_branch_b", "w_out", "ln1_g", "ln1_b",
             "w_gate_up", "w_down", "ln2_g", "ln2_b"]
    return (loss, dx[None], *[results[n][0] for n in order], *[results[n][1] for n in order],
            *[results[n][2] for n in order], *[results[n][3] for n in order])
```

```python
import functools

import jax
import jax.numpy as jnp
from jax import lax
from jax.experimental import pallas as pl
from jax.experimental.pallas import tpu as pltpu

F32 = jnp.float32
_MXU = jnp.bfloat16
_HI = lax.Precision.HIGHEST
_MID = lax.Precision.HIGH

N_DEV = 8
D_MODEL = 1024
DEPTH = 4
DN_HEADS = 8
DN_HEAD_DIM = 128
DN_WIDTH = DN_HEADS * DN_HEAD_DIM
DN_CONV = 5
DN_CHUNK = 64
SW_HEADS = 16
SW_KV_HEADS = 4
SW_HEAD_DIM = 64
SW_GROUP = SW_HEADS // SW_KV_HEADS
SW_BLOCK = 128
SW_KV_WIDTH = SW_KV_HEADS * SW_HEAD_DIM
ROPE_THETA = 10000.0
FFN_HIDDEN = 2816
DN_ALPHA = (2.0 * DEPTH) ** 0.25
LN_EPS = 1e-5
RMS_EPS = 1e-6
ADAM_LR = 0.001
ADAM_B1 = 0.9
ADAM_B2 = 0.999
ADAM_EPS = 1e-08
ADAM_WD = 0.01
ADAM_STEP = 10

LANES = 128
N_HD = 2 * DN_HEADS
DN_GROUP = 4 * DN_CHUNK
INV_SUB = 16
LOCAL_ROWS_FWD = 1024
LOCAL_ROWS_BWD = 512
REC_ROWS = 512
ROW_TILE = 512
VMEM_LIMIT = 48 << 20

C_QKV, C_Z, C_QSW, C_GA, C_GB, C_KSW, C_VSW = 0, 3072, 4096, 5120, 6144, 7168, 7424
MAIN_COLS = 7680
R_QKV, R_Z, R_BA, R_QSW, R_KSW, R_VSW, R_G = 0, 3072, 4096, 4128, 5152, 5408, 5664
IN_COLS = 7712


_NN = ((1,), (0,))
_NT = ((1,), (1,))
_TN = ((0,), (0,))


def _dg(a, b, dims, precision):
    if precision is not None:
        return lax.dot_general(a, b, (dims, ((), ())), precision=precision, preferred_element_type=F32)
    return lax.dot_general(a.astype(_MXU), b.astype(_MXU), (dims, ((), ())), preferred_element_type=F32)


def _make_dots(hi):
    @jax.custom_vjp
    def nn(a, b):
        return _dg(a, b, _NN, hi)

    @jax.custom_vjp
    def nt(a, b):
        return _dg(a, b, _NT, hi)

    @jax.custom_vjp
    def tn(a, b):
        return _dg(a, b, _TN, hi)

    nn.defvjp(lambda a, b: (nn(a, b), (a, b)), lambda r, g: (nt(g, r[1]), tn(r[0], g)))
    nt.defvjp(lambda a, b: (nt(a, b), (a, b)), lambda r, g: (nn(g, r[1]), tn(g, r[0])))
    tn.defvjp(lambda a, b: (tn(a, b), (a, b)), lambda r, g: (nt(r[1], g), nn(r[0], g)))
    return nn, nt, tn


_bnn, _bnt, _btn = _make_dots(None)
_hnn, _hnt, _htn = _make_dots(_HI)
_mnn, _mnt, _mtn = _make_dots(_MID)


def _neumann(a, order):
    n = a.shape[0]
    eye = (lax.broadcasted_iota(jnp.int32, (n, n), 0) == lax.broadcasted_iota(jnp.int32, (n, n), 1)).astype(F32)
    inv = eye - a
    p = a
    span = 2
    while span < order:
        p = _mnn(p, p)
        inv = inv + _mnn(inv, p)
        span *= 2
    return inv


def _inv_unit(a, order):
    n = a.shape[0]
    ii = lax.broadcasted_iota(jnp.int32, (n, n), 0)
    jj = lax.broadcasted_iota(jnp.int32, (n, n), 1)
    near = (ii & -INV_SUB) == (jj & -INV_SUB)
    d_inv = _neumann(jnp.where(near, a, 0.0), INV_SUB)
    outer = _neumann(_mnn(d_inv, jnp.where(near, 0.0, a)), order // INV_SUB)
    return _mnn(outer, d_inv)


def _inv_unit_t(t, g):
    return -_mnt(_mtn(t, g), t)


def _silu(x):
    return x * jax.nn.sigmoid(x)


def _softplus(x):
    return jnp.maximum(x, 0.0) + jnp.log1p(jnp.exp(-jnp.abs(x)))


def _params(sem=None):
    kw = {"vmem_limit_bytes": VMEM_LIMIT}
    if sem is not None:
        kw["dimension_semantics"] = sem
    return pltpu.CompilerParams(**kw)


def _tile(dim, pref):
    if dim <= pref:
        return dim
    t = (pref // LANES) * LANES
    while t > LANES and dim % t:
        t -= LANES
    assert dim % t == 0, (dim, pref)
    return t


def _full(shape):
    zeros = (0,) * len(shape)
    return pl.BlockSpec(shape, lambda *_: zeros)


def _sds(shape, dtype=F32):
    return jax.ShapeDtypeStruct(shape, dtype)


def _mm(a, b, mode, *, name, add=None, tm=1536, tn=1536, tk=1536, out_dtype=F32):
    if mode == "nn":
        (m, k), (k2, n) = a.shape, b.shape
    elif mode == "nt":
        (m, k), (n, k2) = a.shape, b.shape
    else:
        (k, m), (k2, n) = a.shape, b.shape
    assert k == k2, (a.shape, b.shape, mode)
    tm, tn, tk = _tile(m, tm), _tile(n, tn), _tile(k, tk)
    nk = k // tk
    dims = {"nn": _NN, "nt": _NT, "tn": _TN}[mode]

    def body(*refs):
        if add is None:
            a_ref, b_ref, o_ref, acc = refs
        else:
            a_ref, b_ref, add_ref, o_ref, acc = refs
        kk = pl.program_id(2)

        @pl.when(kk == 0)
        def _():
            acc[...] = jnp.zeros_like(acc)

        acc[...] += _dg(a_ref[...], b_ref[...], dims, None)

        @pl.when(kk == nk - 1)
        def _():
            o_ref[...] = (acc[...] if add is None else acc[...] + add_ref[...]).astype(out_dtype)

    a_spec = pl.BlockSpec((tk, tm), lambda i, j, kk: (kk, i)) if mode == "tn" else pl.BlockSpec((tm, tk), lambda i, j, kk: (i, kk))
    b_spec = pl.BlockSpec((tn, tk), lambda i, j, kk: (j, kk)) if mode == "nt" else pl.BlockSpec((tk, tn), lambda i, j, kk: (kk, j))
    o_spec = pl.BlockSpec((tm, tn), lambda i, j, kk: (i, j))
    ins, specs = [a, b], [a_spec, b_spec]
    if add is not None:
        ins.append(add)
        specs.append(o_spec)
    return pl.pallas_call(
        body, name=name, grid=(m // tm, n // tn, nk), in_specs=specs, out_specs=o_spec,
        out_shape=_sds((m, n), out_dtype), scratch_shapes=[pltpu.VMEM((tm, tn), F32)],
        compiler_params=_params(("parallel", "parallel", "arbitrary")),
    )(*ins)


def _cols(width, start):
    assert start % width == 0
    return pl.BlockSpec((ROW_TILE, width), lambda i: (i, start // width))


def _rows(width):
    return pl.BlockSpec((ROW_TILE, width), lambda i: (i, 0))


def _accumulate(ref, value, step):
    @pl.when(step == 0)
    def _():
        ref[...] = value

    @pl.when(step != 0)
    def _():
        ref[...] += value


def _ln_fn(x, r, g, b):
    u = DN_ALPHA * x + r
    mu = jnp.mean(u, axis=-1, keepdims=True)
    var = jnp.mean(jnp.square(u - mu), axis=-1, keepdims=True)
    return (u - mu) * lax.rsqrt(var + LN_EPS) * g + b


def _ln_fwd(x, r, g, b, name):
    t, d = x.shape

    def body(x_ref, r_ref, g_ref, b_ref, o_ref, ob_ref):
        y = _ln_fn(x_ref[...], r_ref[...], g_ref[...], b_ref[...])
        o_ref[...] = y
        ob_ref[...] = y.astype(_MXU)

    return pl.pallas_call(
        body, name=name, grid=(t // ROW_TILE,), in_specs=[_rows(d), _rows(d), _full((1, d)), _full((1, d))],
        out_specs=[_rows(d), _rows(d)], out_shape=[_sds((t, d)), _sds((t, d), _MXU)], compiler_params=_params(("parallel",)),
    )(x, r, g, b)


def _ln_bwd(x, r, g, b, dy, name):
    t, d = x.shape

    def body(x_ref, r_ref, g_ref, b_ref, dy_ref, dx_ref, dr_ref, dg_ref, db_ref):
        _, vjp = jax.vjp(_ln_fn, x_ref[...], r_ref[...], g_ref[...], b_ref[...])
        dx, dr, dg, db = vjp(dy_ref[...])
        dx_ref[...] = dx
        dr_ref[...] = dr.astype(_MXU)
        _accumulate(dg_ref, dg, pl.program_id(0))
        _accumulate(db_ref, db, pl.program_id(0))

    return pl.pallas_call(
        body, name=name, grid=(t // ROW_TILE,),
        in_specs=[_rows(d), _rows(d), _full((1, d)), _full((1, d)), _rows(d)],
        out_specs=[_rows(d), _rows(d), _full((1, d)), _full((1, d))],
        out_shape=[_sds((t, d)), _sds((t, d), _MXU), _sds((1, d)), _sds((1, d))],
        compiler_params=_params(("arbitrary",)),
    )(x, r, g, b, dy)


def _merge_fn(ga, gb, ya, yb):
    return jax.nn.sigmoid(ga) * ya + jax.nn.sigmoid(gb) * yb


def _merge_fwd(proj, ya, yb):
    t, d = ya.shape

    def body(ga_ref, gb_ref, ya_ref, yb_ref, o_ref):
        o_ref[...] = _merge_fn(ga_ref[...], gb_ref[...], ya_ref[...], yb_ref[...]).astype(_MXU)

    return pl.pallas_call(
        body, name="merge_fwd", grid=(t // ROW_TILE,), in_specs=[_cols(d, C_GA), _cols(d, C_GB), _rows(d), _rows(d)],
        out_specs=_rows(d), out_shape=_sds((t, d), _MXU), compiler_params=_params(("parallel",)),
    )(proj, proj, ya, yb)


def _merge_bwd(proj, ya, yb, dm):
    t, d = ya.shape

    def body(ga_ref, gb_ref, ya_ref, yb_ref, dm_ref, dga_ref, dgb_ref, dya_ref, dyb_ref):
        _, vjp = jax.vjp(_merge_fn, ga_ref[...], gb_ref[...], ya_ref[...], yb_ref[...])
        dga_ref[...], dgb_ref[...], dya_ref[...], dyb_ref[...] = [g.astype(_MXU) for g in vjp(dm_ref[...])]

    return pl.pallas_call(
        body, name="merge_bwd", grid=(t // ROW_TILE,),
        in_specs=[_cols(d, C_GA), _cols(d, C_GB), _rows(d), _rows(d), _rows(d)],
        out_specs=[_rows(d)] * 4, out_shape=[_sds((t, d), _MXU)] * 4, compiler_params=_params(("parallel",)),
    )(proj, proj, ya, yb, dm)


def _swiglu_fn(gate, up):
    return _silu(gate) * up


def _swiglu_fwd(gu):
    t = gu.shape[0]
    f = FFN_HIDDEN
    rows = 128

    def body(gu_ref, o_ref):
        o_ref[...] = _swiglu_fn(gu_ref[:, :f], gu_ref[:, f:]).astype(_MXU)

    return pl.pallas_call(
        body, name="swiglu_fwd", grid=(t // rows,), in_specs=[pl.BlockSpec((rows, 2 * f), lambda i: (i, 0))],
        out_specs=pl.BlockSpec((rows, f), lambda i: (i, 0)), out_shape=_sds((t, f), _MXU), compiler_params=_params(("parallel",)),
    )(gu)


def _swiglu_bwd(gu, dh):
    t = gu.shape[0]
    f = FFN_HIDDEN
    rows = 128

    def body(gu_ref, dh_ref, o_ref):
        _, vjp = jax.vjp(_swiglu_fn, gu_ref[:, :f], gu_ref[:, f:])
        o_ref[:, :f], o_ref[:, f:] = [g.astype(_MXU) for g in vjp(dh_ref[...])]

    return pl.pallas_call(
        body, name="swiglu_bwd", grid=(t // rows,),
        in_specs=[pl.BlockSpec((rows, 2 * f), lambda i: (i, 0)), pl.BlockSpec((rows, f), lambda i: (i, 0))],
        out_specs=pl.BlockSpec((rows, 2 * f), lambda i: (i, 0)), out_shape=_sds((t, 2 * f), _MXU),
        compiler_params=_params(("parallel",)),
    )(gu, dh)


def _loss_head(y, target):
    t, d = y.shape

    def body(y_ref, t_ref, s_ref, dy_ref):
        err = y_ref[...] - t_ref[...]
        dy_ref[...] = err / d
        _accumulate(s_ref, jnp.broadcast_to(jnp.sum(jnp.square(err)), (1, LANES)), pl.program_id(0))

    return pl.pallas_call(
        body, name="loss_head", grid=(t // ROW_TILE,), in_specs=[_rows(d), _rows(d)],
        out_specs=[_full((1, LANES)), _rows(d)], out_shape=[_sds((1, LANES)), _sds((t, d))],
        compiler_params=_params(("arbitrary",)),
    )(y, target)


def _shift_rows(x, s):
    if s == 0:
        return x
    return pltpu.roll(x, (-s) % x.shape[0], 0)


def _conv(x, w):
    half = DN_CONV // 2
    acc = None
    for k in range(DN_CONV):
        term = _shift_rows(x, k - half) * w[k:k + 1, :]
        acc = term if acc is None else acc + term
    return acc


def _act_norm(c, do_norm, scale):
    a = _silu(c)
    if not do_norm:
        return a
    return a * lax.rsqrt(jnp.sum(a * a, axis=-1, keepdims=True) + RMS_EPS) * scale


PREP_ROWS = 512
HALO = 8
_KINDS = ((True, DN_HEAD_DIM ** -0.5), (True, 1.0), (False, 1.0))


def _halo_rows(read, i, pr, t):
    lo, hi = i * pr - HALO, (i + 1) * pr + HALO
    parts = []
    if lo < 0:
        parts.append(jnp.zeros((HALO, LANES), F32))
    parts.append(read(max(lo, 0), min(hi, t)))
    if hi > t:
        parts.append(jnp.zeros((HALO, LANES), F32))
    return jnp.concatenate(parts, axis=0) if len(parts) > 1 else parts[0]


def _prep_fwd(proj, conv_w, kind):
    t = proj.shape[0]
    pr = min(PREP_ROWS, t)
    do_norm, scale = _KINDS[kind]
    blk = pl.BlockSpec((t, LANES), lambda j: (0, kind * DN_HEADS + j))

    def body(x_ref, w_ref, o_ref):
        w = w_ref[...]
        for i in range(t // pr):
            xx = _halo_rows(lambda lo, hi: x_ref[lo:hi, :], i, pr, t)
            c = _conv(xx, w)[HALO:HALO + pr, :]
            o_ref[i * pr:(i + 1) * pr, :] = _act_norm(c, do_norm, scale)

    return pl.pallas_call(
        body, name=f"prep_fwd_{kind}", grid=(DN_HEADS,),
        in_specs=[blk, pl.BlockSpec((8, LANES), lambda j: (0, kind * DN_HEADS + j))],
        out_specs=pl.BlockSpec((t, LANES), lambda j: (0, j)), out_shape=_sds((t, DN_WIDTH)),
        compiler_params=_params(("parallel",)),
    )(proj, conv_w)


def _prep_bwd(proj, conv_w, d2, kind):
    t = proj.shape[0]
    pr = min(PREP_ROWS, t)
    do_norm, scale = _KINDS[kind]
    half = DN_CONV // 2
    blk = pl.BlockSpec((t, LANES), lambda j: (0, kind * DN_HEADS + j))
    oblk = pl.BlockSpec((t, LANES), lambda j: (0, j))

    def body(x_ref, w_ref, d_ref, dx_ref, dw_ref):
        w = w_ref[...]
        own = slice(HALO, HALO + pr)
        dw = jnp.zeros((8, LANES), F32)
        for i in range(t // pr):
            xx = _halo_rows(lambda lo, hi: x_ref[lo:hi, :], i, pr, t)
            dn = _halo_rows(lambda lo, hi: d_ref[0, lo:hi, :] + d_ref[1, lo:hi, :], i, pr, t)
            _, vjp = jax.vjp(lambda c: _act_norm(c, do_norm, scale), _conv(xx, w))
            (dc,) = vjp(dn)
            dx = None
            rows = []
            for k in range(DN_CONV):
                term = _shift_rows(dc, half - k) * w[k:k + 1, :]
                dx = term if dx is None else dx + term
                rows.append(jnp.sum(dc[own, :] * _shift_rows(xx, k - half)[own, :], axis=0, keepdims=True))
            dx_ref[i * pr:(i + 1) * pr, :] = dx[own, :].astype(_MXU)
            dw = dw + jnp.concatenate(rows + [jnp.zeros((8 - DN_CONV, LANES), F32)], axis=0)
        dw_ref[...] = dw

    return pl.pallas_call(
        body, name=f"prep_bwd_{kind}", grid=(DN_HEADS,),
        in_specs=[blk, pl.BlockSpec((8, LANES), lambda j: (0, kind * DN_HEADS + j)), pl.BlockSpec((2, t, LANES), lambda j: (0, 0, j))],
        out_specs=[oblk, pl.BlockSpec((8, LANES), lambda j: (0, j))], out_shape=[_sds((t, DN_WIDTH), _MXU), _sds((8, DN_WIDTH))],
        compiler_params=_params(("parallel",)),
    )(proj, conv_w, d2)


def _gb_fn(ba, alog_row, dtb_row):
    c = DN_CHUNK
    lane = lax.broadcasted_iota(jnp.int32, (c, LANES), 1)
    ii = lax.broadcasted_iota(jnp.int32, (c, c), 0)
    jj = lax.broadcasted_iota(jnp.int32, (c, c), 1)
    beta = jax.nn.sigmoid(ba)
    g = -jnp.exp(alog_row) * _softplus(ba + dtb_row)
    g = jnp.where((lane >= N_HD) & (lane < 2 * N_HD), g, 0.0)
    gc_fwd = _hnn((ii >= jj).astype(F32), g)
    gc_rev = _hnn((ii <= jj).astype(F32), g)
    gc = jnp.where(lane < N_HD + DN_HEADS, gc_fwd, gc_rev)
    return jnp.where(lane < N_HD, beta, jnp.where(lane < 2 * N_HD, gc, 0.0))


def _per_head_spec():
    return pl.BlockSpec((N_HD, ROW_TILE, LANES), lambda i: (0, i, 0))


def _gb_fwd(ba, alog_row, dtb_row):
    t = ba.shape[0]
    n = ROW_TILE // DN_CHUNK

    def body(ba_ref, a_ref, d_ref, o_ref, beta_ref, gc_ref):
        for c in range(n):
            rows = slice(c * DN_CHUNK, (c + 1) * DN_CHUNK)
            out = _gb_fn(ba_ref[rows, :], a_ref[...], d_ref[...])
            o_ref[rows, :] = out
            for j in range(N_HD):
                beta_ref[j, rows, :] = jnp.broadcast_to(out[:, j:j + 1], (DN_CHUNK, LANES))
                gc_ref[j, rows, :] = jnp.broadcast_to(out[:, N_HD + j:N_HD + j + 1], (DN_CHUNK, LANES))

    per_head = _sds((N_HD, t, LANES))
    return pl.pallas_call(
        body, name="gates_fwd", grid=(t // ROW_TILE,), in_specs=[_rows(LANES), _full((1, LANES)), _full((1, LANES))],
        out_specs=[_rows(LANES), _per_head_spec(), _per_head_spec()], out_shape=[_sds((t, LANES)), per_head, per_head],
        compiler_params=_params(("parallel",)),
    )(ba, alog_row, dtb_row)


def _gb_bwd(ba, alog_row, dtb_row, dbeta, dgc, d_rows):
    t = ba.shape[0]
    n = ROW_TILE // DN_CHUNK

    def body(ba_ref, a_ref, d_ref, dbeta_ref, dgc_ref, dr_ref, dba_ref, dal_ref, ddt_ref):
        dal = jnp.zeros((1, LANES), F32)
        ddt = jnp.zeros((1, LANES), F32)
        lane = lax.broadcasted_iota(jnp.int32, (DN_CHUNK, LANES), 1)
        for c in range(n):
            rows = slice(c * DN_CHUNK, (c + 1) * DN_CHUNK)
            cot = dr_ref[rows, :]
            for j in range(N_HD):
                cot = jnp.where(lane == j, dbeta_ref[j, rows, :], cot)
                cot = jnp.where(lane == N_HD + j, dgc_ref[j, rows, :] + cot, cot)
            _, vjp = jax.vjp(_gb_fn, ba_ref[rows, :], a_ref[...], d_ref[...])
            dba, da, dd = vjp(cot)
            dba_ref[rows, :] = dba.astype(_MXU)
            dal = dal + da
            ddt = ddt + dd
        _accumulate(dal_ref, dal, pl.program_id(0))
        _accumulate(ddt_ref, ddt, pl.program_id(0))

    return pl.pallas_call(
        body, name="gates_bwd", grid=(t // ROW_TILE,),
        in_specs=[_rows(LANES), _full((1, LANES)), _full((1, LANES)), _per_head_spec(), _per_head_spec(), _rows(LANES)],
        out_specs=[_rows(LANES), _full((1, LANES)), _full((1, LANES))],
        out_shape=[_sds((t, LANES), _MXU), _sds((1, LANES)), _sds((1, LANES))], compiler_params=_params(("arbitrary",)),
    )(ba, alog_row, dtb_row, dbeta, dgc, d_rows)


def _dn_decay(gcc, gcr, sgn):
    c = DN_CHUNK
    ii = lax.broadcasted_iota(jnp.int32, (c, c), 0)
    jj = lax.broadcasted_iota(jnp.int32, (c, c), 1)
    d = (ii - jj) * sgn
    lower = d >= 0
    return jnp.where(lower, jnp.exp(jnp.where(lower, gcc - gcr, 0.0)), 0.0), d > 0


def _dn_a(k, beta, gcc, gcr, sgn):
    decay, strict = _dn_decay(gcc, gcr, sgn)
    return jnp.where(strict, beta * _bnt(k, k) * decay, 0.0)


def _dn_group(q, k, v, beta, gcc, gcr, sgn):
    n = DN_GROUP
    ii = lax.broadcasted_iota(jnp.int32, (n, n), 0)
    jj = lax.broadcasted_iota(jnp.int32, (n, n), 1)
    same = (ii & -DN_CHUNK) == (jj & -DN_CHUNK)
    d = (ii - jj) * sgn
    lower = same & (d >= 0)
    decay = jnp.where(lower, jnp.exp(jnp.where(lower, gcc - gcr, 0.0)), 0.0)
    a = jnp.where(same & (d > 0), beta * _bnt(k, k) * decay, 0.0)
    t_inv = _inv_unit(a, DN_CHUNK)
    u = _bnn(t_inv, v * beta)
    w = _bnn(t_inv, k * (beta * jnp.exp(gcc)))
    return u, w, _bnt(q, k) * decay, t_inv


def _dn_local(t_inv, q, k, v, beta, gcc, gcr, sgn):
    c = DN_CHUNK
    decay, _ = _dn_decay(gcc, gcr, sgn)
    eg = jnp.exp(gcc)
    u = _bnn(t_inv, v * beta)
    w = _bnn(t_inv, k * (beta * eg))
    qk = _bnt(q, k) * decay
    qd = q * eg
    last = jnp.where(sgn > 0, c - 1, 0)
    onehot = (lax.broadcasted_iota(jnp.int32, (c, 1), 0) == last).astype(F32)
    gl = jnp.sum(gcc * onehot, axis=0, keepdims=True)
    kd = k * jnp.exp(gl - gcc)
    egl = jnp.broadcast_to(jnp.exp(gl), (1, LANES))
    return u, w, qk, qd, kd, egl


def _hd_sign(hd):
    return jnp.where(hd < DN_HEADS, 1, -1).astype(jnp.int32)


def _head_of(hd):
    return jnp.where(hd < DN_HEADS, hd, hd - DN_HEADS)


def _dir_of(hd):
    return jnp.where(hd < DN_HEADS, 0, 1)


def _dn_specs(rows_step):
    nl = rows_step // DN_CHUNK
    wide = pl.BlockSpec((1, rows_step, LANES), lambda hd, i: (hd, i, 0))
    half = pl.BlockSpec((1, rows_step, DN_CHUNK), lambda hd, i: (hd, i, 0))
    col = wide
    row = pl.BlockSpec((1, nl, 1, DN_CHUNK), lambda hd, i: (hd, i, 0, 0))
    egl = pl.BlockSpec((1, nl, 1, LANES), lambda hd, i: (hd, i, 0, 0))
    return wide, half, col, row, egl


def _qkv_specs(rows_step):
    return [pl.BlockSpec((rows_step, LANES), lambda hd, i: (i, _head_of(hd)))] * 3


def _maybe_carrying(carry, body, name, grid, operands, in_specs, out_specs, out_shape):
    extra_scratch = []
    if carry is not None:
        kind, arrays = carry
        body, more_in, more_out, more_shape, extra_scratch = _carried(kind, arrays, body, len(operands), len(out_shape), grid)
        operands, in_specs = operands + list(arrays), in_specs + more_in
        out_specs, out_shape = out_specs + more_out, out_shape + more_shape
    sem = ("arbitrary",) * len(grid) if carry is not None else ("parallel",) * len(grid)
    return pl.pallas_call(
        body, name=name, grid=grid, in_specs=in_specs, out_specs=out_specs, out_shape=out_shape,
        scratch_shapes=extra_scratch, compiler_params=_params(sem),
    )(*operands)


def _dn_local_fwd(q, k, v, beta_c, gc_c, gc_r, carry=None):
    t = q.shape[0]
    nc = t // DN_CHUNK
    rows_step = min(LOCAL_ROWS_FWD, t)
    wide, half, col, row, egl = _dn_specs(rows_step)

    ng = rows_step // DN_GROUP
    per = DN_GROUP // DN_CHUNK
    grow = pl.BlockSpec((1, ng, 1, DN_GROUP), lambda hd, i: (hd, i, 0, 0))

    def body(q_ref, k_ref, v_ref, b_ref, gc_ref, gg_ref, u_ref, w_ref, qk_ref, qd_ref, kd_ref, egl_ref, t_ref):
        sgn = _hd_sign(pl.program_id(0))
        last = jnp.where(sgn > 0, DN_CHUNK - 1, 0)
        onehot = (lax.broadcasted_iota(jnp.int32, (DN_CHUNK, 1), 0) == last).astype(F32)
        groups = lambda a: a.reshape((ng, DN_GROUP) + a.shape[1:])
        q_all, k_all, gcc_all = q_ref[...], k_ref[...], gc_ref[0][:, :1]
        u, w, qk, t_inv = jax.vmap(functools.partial(_dn_group, sgn=sgn))(
            groups(q_all), groups(k_all), groups(v_ref[...]), groups(b_ref[0][:, :1]), groups(gcc_all), gg_ref[0])
        u_ref[0] = u.reshape(rows_step, LANES)
        w_ref[0] = w.reshape(rows_step, LANES).astype(_MXU)
        qd_ref[0] = (q_all * jnp.exp(gcc_all)).astype(_MXU)
        for gi in range(ng):
            for c in range(per):
                blk = slice(c * DN_CHUNK, (c + 1) * DN_CHUNK)
                rows = slice(gi * DN_GROUP + c * DN_CHUNK, gi * DN_GROUP + (c + 1) * DN_CHUNK)
                qk_ref[0, rows, :] = qk[gi, blk, blk].astype(_MXU)
                t_ref[0, rows, :] = t_inv[gi, blk, blk]
                gl = jnp.sum(gcc_all[rows, :] * onehot, axis=0, keepdims=True)
                kd_ref[0, rows, :] = (k_all[rows, :] * jnp.exp(gl - gcc_all[rows, :])).astype(_MXU)
                egl_ref[0, gi * per + c] = jnp.broadcast_to(jnp.exp(gl), (1, LANES))

    big = _sds((N_HD, t, LANES))
    small = _sds((N_HD, t, DN_CHUNK))
    operands = [q, k, v, beta_c, gc_c, gc_r.reshape(N_HD, t // DN_GROUP, 1, DN_GROUP)]
    return _maybe_carrying(
        carry, body, "dn_local_fwd", (N_HD, t // rows_step), operands, _qkv_specs(rows_step) + [col, col, grow],
        [wide, wide, half, wide, wide, egl, half],
        [big, _sds(big.shape, _MXU), _sds(small.shape, _MXU), _sds(big.shape, _MXU), _sds(big.shape, _MXU),
         _sds((N_HD, nc, 1, LANES)), small])


def _dn_local_bwd(q, k, v, beta_c, gc_c, gc_r, t_inv, du, dw, dqk, dqd, dkd, degl, carry=None):
    t = q.shape[0]
    nc = t // DN_CHUNK
    rows_step = min(LOCAL_ROWS_BWD, t)
    nl = rows_step // DN_CHUNK
    wide, half, col, row, egl = _dn_specs(rows_step)
    dspec = pl.BlockSpec((1, rows_step, LANES), lambda hd, i: (_dir_of(hd), i, _head_of(hd)))

    def body(q_ref, k_ref, v_ref, b_ref, gc_ref, gr_ref, t_ref, du_ref, dw_ref, dqk_ref, dqd_ref, dkd_ref, degl_ref,
             dq_ref, dk_ref, dv_ref, db_ref, dgc_ref, dgr_ref):
        sgn = _hd_sign(pl.program_id(0))

        def chunk_bwd(tinv, q, k, v, beta, gcc, gcr, du, dw, dqk, dqd, dkd, degl):
            _, vjp = jax.vjp(functools.partial(_dn_local, sgn=sgn), tinv, q, k, v, beta, gcc, gcr)
            dt, dq, dk, dv, db, dgc, dgr = vjp((du, dw, dqk, dqd, dkd, degl))
            _, vjp_a = jax.vjp(functools.partial(_dn_a, sgn=sgn), k, beta, gcc, gcr)
            dk2, db2, dgc2, dgr2 = vjp_a(_inv_unit_t(tinv, dt))
            return dq, dk + dk2, dv, db + db2, dgc + dgc2, dgr + dgr2

        chunks = lambda a: a.reshape((nl, DN_CHUNK) + a.shape[1:])
        dq, dk, dv, db, dgc, dgr = jax.vmap(chunk_bwd)(
            chunks(t_ref[0]), chunks(q_ref[...]), chunks(k_ref[...]), chunks(v_ref[...]), chunks(b_ref[0][:, :1]),
            chunks(gc_ref[0][:, :1]),
            gr_ref[0], chunks(du_ref[0]), chunks(dw_ref[0]), chunks(dqk_ref[0]), chunks(dqd_ref[0]), chunks(dkd_ref[0]),
            degl_ref[0])
        dq_ref[0] = dq.reshape(rows_step, LANES)
        dk_ref[0] = dk.reshape(rows_step, LANES)
        dv_ref[0] = dv.reshape(rows_step, LANES)
        db_ref[0] = jnp.broadcast_to(db.reshape(rows_step, 1), (rows_step, LANES))
        dgc_ref[0] = jnp.broadcast_to(dgc.reshape(rows_step, 1), (rows_step, LANES))
        dgr_ref[0] = dgr

    per_dir = _sds((2, t, DN_WIDTH))
    return _maybe_carrying(
        carry, body, "dn_local_bwd", (N_HD, t // rows_step), [q, k, v, beta_c, gc_c, gc_r, t_inv, du, dw, dqk, dqd, dkd, degl],
        _qkv_specs(rows_step) + [col, col, row, half, wide, wide, half, wide, wide, egl], [dspec, dspec, dspec, col, col, row],
        [per_dir, per_dir, per_dir, _sds((N_HD, t, LANES)), _sds((N_HD, t, LANES)), _sds((N_HD, nc, 1, DN_CHUNK))])


REC_HEADS = 8
REC_GROUPS = N_HD // REC_HEADS
REC_FWD_GROUPS = DN_HEADS // REC_HEADS


def _rec_specs(time_block):
    nr = REC_ROWS // DN_CHUNK
    wide = pl.BlockSpec((REC_HEADS, REC_ROWS, LANES), lambda g, b: (g, time_block(g, b), 0))
    half = pl.BlockSpec((REC_HEADS, REC_ROWS, DN_CHUNK), lambda g, b: (g, time_block(g, b), 0))
    egl = pl.BlockSpec((REC_HEADS, nr, 1, LANES), lambda g, b: (g, time_block(g, b), 0, 0))
    state = pl.BlockSpec((REC_HEADS, nr, DN_HEAD_DIM, DN_HEAD_DIM), lambda g, b: (g, time_block(g, b), 0, 0))
    return wide, half, egl, state


def _rec_head_cols(g):
    return jnp.where(g < REC_FWD_GROUPS, g, g - REC_FWD_GROUPS)


def _dn_rec_fwd(u, w, qk, qd, kd, egl):
    t = u.shape[1]
    nb = t // REC_ROWS
    nr = REC_ROWS // DN_CHUNK
    nc = t // DN_CHUNK

    def time_block(g, b):
        return jnp.where(g < REC_FWD_GROUPS, b, nb - 1 - b)

    wide, half, egl_spec, state = _rec_specs(time_block)
    o_spec = pl.BlockSpec((1, REC_ROWS, REC_HEADS * LANES),
                          lambda g, b: (jnp.where(g < REC_FWD_GROUPS, 0, 1), time_block(g, b), _rec_head_cols(g)))

    def body(u_ref, w_ref, qk_ref, qd_ref, kd_ref, egl_ref, o_ref, vn_ref, s_ref, s_scr):
        fwd = pl.program_id(0) < REC_FWD_GROUPS

        @pl.when(pl.program_id(1) == 0)
        def _():
            s_scr[...] = jnp.zeros_like(s_scr)

        def run(order):
            heads = range(REC_HEADS)
            s = [s_scr[j] for j in heads]
            for ce in order:
                rows = slice(ce * DN_CHUNK, (ce + 1) * DN_CHUNK)
                vn = [u_ref[j, rows, :] - _bnn(w_ref[j, rows, :], s[j]) for j in heads]
                o = [_bnn(qd_ref[j, rows, :], s[j]) + _bnn(qk_ref[j, rows, :], vn[j]) for j in heads]
                nxt = [s[j] * egl_ref[j, ce] + _btn(kd_ref[j, rows, :], vn[j]) for j in heads]
                for j in heads:
                    s_ref[j, ce] = s[j]
                    vn_ref[j, rows, :] = vn[j].astype(_MXU)
                    o_ref[0, rows, j * LANES:(j + 1) * LANES] = o[j]
                s = nxt
            for j in heads:
                s_scr[j] = s[j]

        pl.when(fwd)(lambda: run(range(nr)))
        pl.when(jnp.logical_not(fwd))(lambda: run(range(nr - 1, -1, -1)))

    return pl.pallas_call(
        body, name="dn_rec_fwd", grid=(REC_GROUPS, nb), in_specs=[wide, wide, half, wide, wide, egl_spec],
        out_specs=[o_spec, wide, state],
        out_shape=[_sds((2, t, DN_WIDTH)), _sds((N_HD, t, LANES), _MXU), _sds((N_HD, nc, DN_HEAD_DIM, DN_HEAD_DIM))],
        scratch_shapes=[pltpu.VMEM((REC_HEADS, DN_HEAD_DIM, DN_HEAD_DIM), F32)],
        compiler_params=_params(("parallel", "arbitrary")),
    )(u, w, qk, qd, kd, egl)


def _dn_rec_bwd(w, qk, qd, kd, egl, vn, states, do):
    t = w.shape[1]
    nb = t // REC_ROWS
    nr = REC_ROWS // DN_CHUNK
    nc = t // DN_CHUNK

    def time_block(g, b):
        return jnp.where(g < REC_FWD_GROUPS, nb - 1 - b, b)

    wide, half, egl_spec, state = _rec_specs(time_block)
    do_spec = pl.BlockSpec((REC_ROWS, REC_HEADS * LANES), lambda g, b: (time_block(g, b), _rec_head_cols(g)))

    def body(w_ref, qk_ref, qd_ref, kd_ref, egl_ref, vn_ref, s_ref, do_ref,
             du_ref, dw_ref, dqk_ref, dqd_ref, dkd_ref, degl_ref, ds_scr):
        fwd = pl.program_id(0) < REC_FWD_GROUPS

        @pl.when(pl.program_id(1) == 0)
        def _():
            ds_scr[...] = jnp.zeros_like(ds_scr)

        def run(order):
            heads = range(REC_HEADS)
            ds = [ds_scr[j] for j in heads]
            for ce in order:
                rows = slice(ce * DN_CHUNK, (ce + 1) * DN_CHUNK)
                s = [s_ref[j, ce] for j in heads]
                do_c = [do_ref[rows, j * LANES:(j + 1) * LANES] for j in heads]
                vn_c = [vn_ref[j, rows, :] for j in heads]
                dvn = [_btn(qk_ref[j, rows, :], do_c[j]) + _bnn(kd_ref[j, rows, :], ds[j]) for j in heads]
                nxt = [ds[j] * egl_ref[j, ce] + _btn(qd_ref[j, rows, :], do_c[j]) - _btn(w_ref[j, rows, :], dvn[j])
                       for j in heads]
                for j in heads:
                    du_ref[j, rows, :] = dvn[j]
                    dw_ref[j, rows, :] = -_bnt(dvn[j], s[j])
                for j in heads:
                    dqk_ref[j, rows, :] = _bnt(do_c[j], vn_c[j])
                    dqd_ref[j, rows, :] = _bnt(do_c[j], s[j])
                for j in heads:
                    dkd_ref[j, rows, :] = _bnt(vn_c[j], ds[j])
                    degl_ref[j, ce] = jnp.sum(s[j] * ds[j], axis=0, keepdims=True)
                ds = nxt
            for j in heads:
                ds_scr[j] = ds[j]

        pl.when(fwd)(lambda: run(range(nr - 1, -1, -1)))
        pl.when(jnp.logical_not(fwd))(lambda: run(range(nr)))

    big = _sds((N_HD, t, LANES))
    return pl.pallas_call(
        body, name="dn_rec_bwd", grid=(REC_GROUPS, nb), in_specs=[wide, half, wide, wide, egl_spec, wide, state, do_spec],
        out_specs=[wide, wide, half, wide, wide, egl_spec],
        out_shape=[big, big, _sds((N_HD, t, DN_CHUNK)), big, big, _sds((N_HD, nc, 1, LANES))],
        scratch_shapes=[pltpu.VMEM((REC_HEADS, DN_HEAD_DIM, DN_HEAD_DIM), F32)],
        compiler_params=_params(("parallel", "arbitrary")),
    )(w, qk, qd, kd, egl, vn, states, do)


def _post_fn(of, ob, z, gain):
    o = of + ob
    return o * lax.rsqrt(jnp.mean(o * o, axis=-1, keepdims=True) + RMS_EPS) * gain * _silu(z)


def _post_specs():
    o_spec = [pl.BlockSpec((1, ROW_TILE, DN_WIDTH), functools.partial(lambda i, d: (d, i, 0), d=d)) for d in (0, 1)]
    return o_spec, _cols(DN_WIDTH, C_Z), _rows(DN_WIDTH), _full((1, LANES))


def _post_fwd(o2, proj, gain):
    t = proj.shape[0]
    o_spec, z_spec, wide, gain_spec = _post_specs()

    def body(of_ref, ob_ref, z_ref, g_ref, out_ref):
        for h in range(DN_HEADS):
            cols = slice(h * LANES, (h + 1) * LANES)
            out_ref[:, cols] = _post_fn(of_ref[0, :, cols], ob_ref[0, :, cols], z_ref[:, cols], g_ref[...]).astype(_MXU)

    return pl.pallas_call(
        body, name="post_fwd", grid=(t // ROW_TILE,), in_specs=o_spec + [z_spec, gain_spec], out_specs=wide,
        out_shape=_sds((t, DN_WIDTH), _MXU), compiler_params=_params(("parallel",)),
    )(o2, o2, proj, gain)


def _post_bwd(o2, proj, gain, dout):
    t = proj.shape[0]
    o_spec, z_spec, wide, gain_spec = _post_specs()

    def body(of_ref, ob_ref, z_ref, g_ref, d_ref, do_ref, dz_ref, dg_ref):
        dg_sum = jnp.zeros((1, LANES), F32)
        for h in range(DN_HEADS):
            cols = slice(h * LANES, (h + 1) * LANES)
            _, vjp = jax.vjp(_post_fn, of_ref[0, :, cols], ob_ref[0, :, cols], z_ref[:, cols], g_ref[...])
            do, _, dz, dg = vjp(d_ref[:, cols])
            do_ref[:, cols] = do
            dz_ref[:, cols] = dz.astype(_MXU)
            dg_sum = dg_sum + dg
        _accumulate(dg_ref, dg_sum, pl.program_id(0))

    return pl.pallas_call(
        body, name="post_bwd", grid=(t // ROW_TILE,), in_specs=o_spec + [z_spec, gain_spec, wide],
        out_specs=[wide, wide, gain_spec], out_shape=[_sds((t, DN_WIDTH)), _sds((t, DN_WIDTH), _MXU), _sds((1, LANES))],
        compiler_params=_params(("arbitrary",)),
    )(o2, o2, proj, gain, dout)


def _rope(x, cos, sin):
    lane = lax.broadcasted_iota(jnp.int32, x.shape, 1)
    first = (lane & (SW_HEAD_DIM - 1)) < SW_HEAD_DIM // 2
    rot = jnp.where(first, -pltpu.roll(x, LANES - SW_HEAD_DIM // 2, 1), pltpu.roll(x, SW_HEAD_DIM // 2, 1))
    return x * cos + rot * sin


def _rope_apply(q, k, q_cols, k_cols, cos, sin, name, dtype):
    t = cos.shape[0]
    qw, kw = SW_HEADS * SW_HEAD_DIM, SW_KV_WIDTH

    def body(q_ref, k_ref, c_ref, s_ref, qo_ref, ko_ref):
        c, s = c_ref[...], s_ref[...]
        for j in range(qw // LANES):
            cols = slice(j * LANES, (j + 1) * LANES)
            qo_ref[:, cols] = _rope(q_ref[:, cols], c, s).astype(dtype)
        for j in range(kw // LANES):
            cols = slice(j * LANES, (j + 1) * LANES)
            ko_ref[:, cols] = _rope(k_ref[:, cols], c, s).astype(dtype)

    return pl.pallas_call(
        body, name=name, grid=(t // ROW_TILE,), in_specs=[_cols(qw, q_cols), _cols(kw, k_cols), _rows(LANES), _rows(LANES)],
        out_specs=[_rows(qw), _rows(kw)], out_shape=[_sds((t, qw), dtype), _sds((t, kw), dtype)],
        compiler_params=_params(("parallel",)),
    )(q, k, cos, sin)


def _attn_core(qs, kb, vb, sink, mask):
    s = _bnt(qs, kb) * (SW_HEAD_DIM ** -0.5)
    s = jnp.where(mask, s, -1e30)
    m = lax.stop_gradient(jnp.maximum(jnp.max(s, axis=1, keepdims=True), sink))
    e = jnp.exp(s - m)
    den = jnp.sum(e, axis=1, keepdims=True) + jnp.exp(sink - m)
    return _bnn(e / den, vb)


def _band_mask(n, nb):
    rows = SW_GROUP * SW_BLOCK
    i = lax.broadcasted_iota(jnp.int32, (rows, 3 * SW_BLOCK), 0) & (SW_BLOCK - 1)
    j = lax.broadcasted_iota(jnp.int32, (rows, 3 * SW_BLOCK), 1)
    near = (j - i >= 0) & (j - i <= 2 * SW_BLOCK)
    lo = jnp.where(n == 0, SW_BLOCK, 0)
    hi = jnp.where(n == nb - 1, 2 * SW_BLOCK, 3 * SW_BLOCK)
    return near & (j >= lo) & (j < hi)


def _band_specs(nb, v_cols):
    def spec(width, base, shift):
        return pl.BlockSpec((SW_BLOCK, width), lambda n: (jnp.clip(n + shift, 0, nb - 1), base // width))
    k_specs = [spec(SW_KV_WIDTH, 0, s) for s in (-1, 0, 1)]
    v_specs = [spec(SW_KV_WIDTH, v_cols, s) for s in (-1, 0, 1)]
    return k_specs, v_specs


def _head_cols(kv, g):
    h = kv * SW_GROUP + g
    return slice(h * SW_HEAD_DIM, (h + 1) * SW_HEAD_DIM)


def _kv_batches(q_ref, kb, vb, s_ref):
    kvs = range(SW_KV_HEADS)
    cols = lambda kv: slice(kv * SW_HEAD_DIM, (kv + 1) * SW_HEAD_DIM)
    qs = jnp.stack([jnp.concatenate([q_ref[:, _head_cols(kv, g)] for g in range(SW_GROUP)], axis=0) for kv in kvs])
    sinks = jnp.stack([jnp.concatenate([jnp.broadcast_to(s_ref[kv * SW_GROUP + g], (SW_BLOCK, 1)) for g in range(SW_GROUP)],
                                       axis=0) for kv in kvs])
    return qs, jnp.stack([kb[:, cols(kv)] for kv in kvs]), jnp.stack([vb[:, cols(kv)] for kv in kvs]), sinks


def _attn_fwd(qr, kr, proj, sinks):
    t = qr.shape[0]
    nb = t // SW_BLOCK
    qw = SW_HEADS * SW_HEAD_DIM
    k_specs, v_specs = _band_specs(nb, C_VSW)
    q_spec = pl.BlockSpec((SW_BLOCK, qw), lambda n: (n, 0))

    def body(q_ref, k0, k1, k2, v0, v1, v2, s_ref, o_ref):
        mask = _band_mask(pl.program_id(0), nb)
        kb = jnp.concatenate([k0[...], k1[...], k2[...]], axis=0)
        vb = jnp.concatenate([v0[...], v1[...], v2[...]], axis=0)
        qs, kbs, vbs, sinks_ = _kv_batches(q_ref, kb, vb, s_ref)
        o = jax.vmap(functools.partial(_attn_core, mask=mask))(qs, kbs, vbs, sinks_)
        for kv in range(SW_KV_HEADS):
            for g in range(SW_GROUP):
                o_ref[:, _head_cols(kv, g)] = o[kv, g * SW_BLOCK:(g + 1) * SW_BLOCK, :].astype(_MXU)

    return pl.pallas_call(
        body, name="attn_fwd", grid=(nb,), in_specs=[q_spec] + k_specs + v_specs + [_full((SW_HEADS, 1, 1))],
        out_specs=q_spec, out_shape=_sds((t, qw), _MXU), compiler_params=_params(("parallel",)),
    )(qr, kr, kr, kr, proj, proj, proj, sinks)


def _attn_bwd(qr, kr, proj, sinks, do):
    t = qr.shape[0]
    nb = t // SW_BLOCK
    qw = SW_HEADS * SW_HEAD_DIM
    k_specs, v_specs = _band_specs(nb, C_VSW)
    q_spec = pl.BlockSpec((SW_BLOCK, qw), lambda n: (n, 0))
    part = pl.BlockSpec((1, 3 * SW_BLOCK, SW_KV_WIDTH), lambda n: (n, 0, 0))

    def body(q_ref, k0, k1, k2, v0, v1, v2, s_ref, do_ref, dq_ref, dk_ref, dv_ref, ds_ref):
        mask = _band_mask(pl.program_id(0), nb)
        kb = jnp.concatenate([k0[...], k1[...], k2[...]], axis=0).astype(F32)
        vb = jnp.concatenate([v0[...], v1[...], v2[...]], axis=0)

        @pl.when(pl.program_id(0) == 0)
        def _():
            ds_ref[...] = jnp.zeros_like(ds_ref)

        qs, kbs, vbs, sinks_ = _kv_batches(q_ref, kb, vb, s_ref)
        dos = jnp.stack([jnp.concatenate([do_ref[:, _head_cols(kv, g)] for g in range(SW_GROUP)], axis=0)
                         for kv in range(SW_KV_HEADS)])

        def head_bwd(q_, k_, v_, sink_, do_):
            _, vjp = jax.vjp(functools.partial(_attn_core, mask=mask), q_, k_, v_, sink_)
            return vjp(do_)

        dqs, dkb, dvb, dsink = jax.vmap(head_bwd)(qs.astype(F32), kbs, vbs, sinks_, dos)
        for kv in range(SW_KV_HEADS):
            kvc = slice(kv * SW_HEAD_DIM, (kv + 1) * SW_HEAD_DIM)
            dk_ref[0, :, kvc] = dkb[kv]
            dv_ref[0, :, kvc] = dvb[kv]
            for g in range(SW_GROUP):
                rows = slice(g * SW_BLOCK, (g + 1) * SW_BLOCK)
                dq_ref[:, _head_cols(kv, g)] = dqs[kv, rows, :]
                ds_ref[kv * SW_GROUP + g] += jnp.sum(dsink[kv, rows, :], axis=0, keepdims=True)

    parts = _sds((nb, 3 * SW_BLOCK, SW_KV_WIDTH))
    return pl.pallas_call(
        body, name="attn_bwd", grid=(nb,), in_specs=[q_spec] + k_specs + v_specs + [_full((SW_HEADS, 1, 1)), q_spec],
        out_specs=[q_spec, part, part, _full((SW_HEADS, 1, 1))], out_shape=[_sds((t, qw)), parts, parts, _sds((SW_HEADS, 1, 1))],
        compiler_params=_params(("arbitrary",)),
    )(qr, kr, kr, kr, proj, proj, proj, sinks, do)


def _band_sum(parts, name, dtype):
    nb = parts.shape[0]
    w = parts.shape[2]

    def spec(shift, slot):
        return pl.BlockSpec((1, SW_BLOCK, w), lambda m: (jnp.clip(m + shift, 0, nb - 1), slot, 0))

    def body(prev_ref, own_ref, next_ref, o_ref):
        m = pl.program_id(0)
        total = own_ref[0] + jnp.where(m > 0, prev_ref[0], 0.0) + jnp.where(m < nb - 1, next_ref[0], 0.0)
        o_ref[...] = total.astype(dtype)

    return pl.pallas_call(
        body, name=name, grid=(nb,), in_specs=[spec(-1, 2), spec(0, 1), spec(1, 0)],
        out_specs=pl.BlockSpec((SW_BLOCK, w), lambda m: (m, 0)), out_shape=_sds((nb * SW_BLOCK, w), dtype),
        compiler_params=_params(("parallel",)),
    )(parts, parts, parts)


def _gate_rows(gbo):
    t = gbo.shape[0]
    return gbo[:, N_HD:2 * N_HD].T.reshape(N_HD, t // DN_CHUNK, 1, DN_CHUNK)


def _gate_rows_t(dgc_r):
    t = dgc_r.shape[1] * DN_CHUNK
    return jnp.pad(dgc_r.reshape(N_HD, t).T, ((0, 0), (N_HD, LANES - 2 * N_HD)))


def _layer_fwd(x, xb, w, cos, sin, carry=None):
    proj = _mm(xb, w["wm"], "nn", name="proj")
    ba = _mm(xb, w["wba"], "nn", name="proj_gates")
    qn, kn, vv = [_prep_fwd(proj, w["conv"], kind) for kind in range(3)]
    gbo, beta_c, gc_c = _gb_fwd(ba, w["alog"], w["dtb"])
    gc_r = _gate_rows(gbo)
    u, wk, qk, qd, kd, egl, tinv, *carried = _dn_local_fwd(qn, kn, vv, beta_c, gc_c, gc_r, carry=carry)
    o2, vn, states = _dn_rec_fwd(u, wk, qk, qd, kd, egl)
    o_dn = _post_fwd(o2, proj, w["dnw"])
    qr, kr = _rope_apply(proj, proj, C_QSW, C_KSW, cos, sin, "rope_fwd", _MXU)
    o_sw = _attn_fwd(qr, kr, proj, w["sinks"])
    ya = _mm(o_dn, w["wa"], "nn", name="branch_a")
    yb = _mm(o_sw, w["wb"], "nn", name="branch_b")
    merged = _merge_fwd(proj, ya, yb)
    mix = _mm(merged, w["wo"], "nn", name="mix_out")
    x1, x1b = _ln_fwd(x, mix, w["ln1g"], w["ln1b"], "ln1_fwd")
    gu = _mm(x1b, w["wgu"], "nn", name="ffn_up")
    h = _swiglu_fwd(gu)
    f = _mm(h, w["wd"], "nn", name="ffn_down")
    x2, x2b = _ln_fwd(x1, f, w["ln2g"], w["ln2b"], "ln2_fwd")
    res = dict(x=x, xb=xb, proj=proj, ba=ba, qn=qn, kn=kn, vv=vv, beta_c=beta_c, gc_c=gc_c, gc_r=gc_r, wk=wk, qk=qk, qd=qd, kd=kd, egl=egl, tinv=tinv, vn=vn,
               states=states, o2=o2, o_dn=o_dn, qr=qr, kr=kr, o_sw=o_sw, ya=ya, yb=yb, merged=merged, mix=mix, x1=x1, x1b=x1b,
               gu=gu, h=h, f=f)
    return x2, x2b, res, (carried or None)


def _layer_bwd(dx2, w, r, cos, sin, carry=None):
    dx1, df, dln2g, dln2b = _ln_bwd(r["x1"], r["f"], w["ln2g"], w["ln2b"], dx2, "ln2_bwd")
    dh = _mm(df, w["wd"], "nt", name="d_ffn_hidden")
    dwd = _mm(r["h"], df, "tn", name="dw_ffn_down", out_dtype=_MXU)
    dgu = _swiglu_bwd(r["gu"], dh)
    dwgu = _mm(r["x1b"], dgu, "tn", name="dw_ffn_up", out_dtype=_MXU)
    dx1 = _mm(dgu, w["wgu"], "nt", name="dx_ffn", add=dx1)
    dx, dmix, dln1g, dln1b = _ln_bwd(r["x"], r["mix"], w["ln1g"], w["ln1b"], dx1, "ln1_bwd")
    dmerged = _mm(dmix, w["wo"], "nt", name="d_merged")
    dwo = _mm(r["merged"], dmix, "tn", name="dw_mix_out", out_dtype=_MXU)
    dga, dgb, dya, dyb = _merge_bwd(r["proj"], r["ya"], r["yb"], dmerged)
    dwa = _mm(r["o_dn"], dya, "tn", name="dw_branch_a", out_dtype=_MXU)
    do_dn = _mm(dya, w["wa"], "nt", name="d_branch_a")
    dwb = _mm(r["o_sw"], dyb, "tn", name="dw_branch_b", out_dtype=_MXU)
    do_sw = _mm(dyb, w["wb"], "nt", name="d_branch_b")
    do, dz, ddnw = _post_bwd(r["o2"], r["proj"], w["dnw"], do_dn)
    du, dwk, dqk, dqd, dkd, degl = _dn_rec_bwd(r["wk"], r["qk"], r["qd"], r["kd"], r["egl"], r["vn"], r["states"], do)
    dq3, dk3, dv3, dbeta_c, dgc_c, dgc_r, *carried = _dn_local_bwd(r["qn"], r["kn"], r["vv"], r["beta_c"], r["gc_c"], r["gc_r"],
                                                                   r["tinv"], du, dwk, dqk, dqd, dkd, degl, carry=carry)
    dqkv, dconv = zip(*[_prep_bwd(r["proj"], w["conv"], d2, kind) for kind, d2 in enumerate((dq3, dk3, dv3))])
    dconv = jnp.concatenate(dconv, axis=1)
    dba, dalog, ddtb = _gb_bwd(r["ba"], w["alog"], w["dtb"], dbeta_c, dgc_c, _gate_rows_t(dgc_r))
    dqr, dkparts, dvparts, dsinks = _attn_bwd(r["qr"], r["kr"], r["proj"], w["sinks"], do_sw)
    dkr = _band_sum(dkparts, "attn_dk_sum", F32)
    dv = _band_sum(dvparts, "attn_dv_sum", _MXU)
    dq_sw, dk_sw = _rope_apply(dqr, dkr, 0, 0, cos, -sin, "rope_bwd", _MXU)
    dproj = jnp.concatenate([*dqkv, dz, dq_sw, dga, dgb, dk_sw, dv], axis=1)
    dwm = _mm(r["xb"], dproj, "tn", name="dw_proj", out_dtype=_MXU)
    dwba = _mm(r["xb"], dba, "tn", name="dw_proj_gates", out_dtype=_MXU)
    dx = _mm(dproj, w["wm"], "nt", name="dx_proj", add=dx)
    dx = _mm(dba, w["wba"], "nt", name="dx_proj_gates", add=dx)
    grads = dict(wm=dwm, wba=dwba, conv=dconv, alog=dalog, dtb=ddtb, dnw=ddnw, sinks=dsinks, wa=dwa, wb=dwb, wo=dwo,
                 ln1g=dln1g, ln1b=dln1b, wgu=dwgu, wd=dwd, ln2g=dln2g, ln2b=dln2b)
    return dx, grads, (carried or None)


def _rope_tables(t):
    half = SW_HEAD_DIM // 2
    inv_freq = ROPE_THETA ** (-jnp.arange(half, dtype=F32) / half)
    ang = jnp.arange(t, dtype=F32)[:, None] * inv_freq[None, :]
    return jnp.tile(jnp.cos(ang), (1, LANES // half)), jnp.tile(jnp.sin(ang), (1, LANES // half))


def _trunk(x, target, n_layers, layer_weights, fwd_carry, grads_done):
    cos, sin = _rope_tables(x.shape[0])
    xb = x.astype(_MXU)
    saved, weights, carried = [], [], None
    for i in range(n_layers):
        w = layer_weights(i, carried)
        x, xb, res, carried = _layer_fwd(x, xb, w, cos, sin, carry=fwd_carry(i))
        saved.append(res)
        weights.append(w)
    sq, dx = _loss_head(x, target)
    carry = None
    for i in reversed(range(n_layers)):
        dx, grads, carried = _layer_bwd(dx, weights[i], saved[i], cos, sin, carry=carry)
        carry = grads_done(i, grads, carried)
    return sq, dx, carry


N_CHIPS = 4


def _mesh_pos():
    return lax.axis_index("x"), lax.axis_index("y"), lax.axis_index("c")


def _other_chips(x, y):
    return [(1 - x, y), (x, 1 - y), (1 - x, 1 - y)]


def _remote_copy(src, dst, sems, k, to):
    send_sems, recv_sems, base = sems
    return pltpu.make_async_remote_copy(src_ref=src, dst_ref=dst, send_sem=send_sems.at[base + k],
                                        recv_sem=recv_sems.at[base + k], device_id=to, device_id_type=pl.DeviceIdType.MESH)


def _exchange_sems(n_arrays, per_array):
    return [pltpu.SemaphoreType.DMA((n_arrays * per_array,)), pltpu.SemaphoreType.DMA((n_arrays * per_array,)),
            pltpu.SemaphoreType.DMA((n_arrays,))]


def _comm_call(body, name, out_shapes, per_array, operands):
    n = len(operands)
    hbm = pl.BlockSpec(memory_space=pl.ANY)

    def flat_body(*refs):
        body(refs[:n], refs[n:2 * n], *refs[2 * n:])

    return pl.pallas_call(
        flat_body, name=name, in_specs=[hbm] * n, out_specs=[hbm] * n, out_shape=list(out_shapes),
        scratch_shapes=_exchange_sems(n, per_array), compiler_params=pltpu.CompilerParams(has_side_effects=True),
    )(*operands)


class _Gather:
    n_sems = N_DEV - 1

    @staticmethod
    def out_shape(block):
        return _sds((N_DEV,) + block.shape, block.dtype)

    @staticmethod
    def _own(x_ref, o_ref, sems, local_sem):
        x, y, c = _mesh_pos()
        mine = o_ref.at[4 * x + 2 * y + c]
        first = [_remote_copy(x_ref, mine, sems, 0, (x, y, 1 - c))]
        first += [_remote_copy(x_ref, mine, sems, 1 + j, (*chip, c)) for j, chip in enumerate(_other_chips(x, y))]
        return pltpu.make_async_copy(x_ref, mine, local_sem), first

    @classmethod
    def start(cls, x_ref, o_ref, sems, local_sem):
        mine, first = cls._own(x_ref, o_ref, sems, local_sem)
        mine.start()
        for cp in first:
            cp.start()

    @classmethod
    def finish(cls, x_ref, o_ref, sems, local_sem):
        x, y, c = _mesh_pos()
        sibling = (x, y, 1 - c)
        chips = _other_chips(x, y)
        slot = lambda px, py, pc: o_ref.at[4 * px + 2 * py + pc]
        mine, first = cls._own(x_ref, o_ref, sems, local_sem)
        passed = [_remote_copy(slot(*chip, c), slot(*chip, c), sems, 4 + j, sibling) for j, chip in enumerate(chips)]
        for j, chip in enumerate(chips):
            _remote_copy(x_ref, slot(*chip, c), sems, 1 + j, sibling).wait_recv()
            passed[j].start()
        _remote_copy(x_ref, slot(x, y, 1 - c), sems, 0, sibling).wait_recv()
        for j, chip in enumerate(chips):
            _remote_copy(x_ref, slot(*chip, 1 - c), sems, 4 + j, sibling).wait_recv()
        for cp in first + passed:
            cp.wait_send()
        mine.wait()


class _ChipExchange:
    n_sems = N_CHIPS - 1

    @staticmethod
    def out_shape(parts):
        return _sds(parts.shape, parts.dtype)

    @staticmethod
    def _own(x_ref, o_ref, sems, local_sem):
        x, y, c = _mesh_pos()
        me = 2 * x + y
        sent = [_remote_copy(x_ref.at[2 * cx + cy], o_ref.at[me], sems, j, (cx, cy, c))
                for j, (cx, cy) in enumerate(_other_chips(x, y))]
        return pltpu.make_async_copy(x_ref.at[me], o_ref.at[me], local_sem), sent

    @classmethod
    def start(cls, x_ref, o_ref, sems, local_sem):
        mine, sent = cls._own(x_ref, o_ref, sems, local_sem)
        mine.start()
        for cp in sent:
            cp.start()

    @classmethod
    def finish(cls, x_ref, o_ref, sems, local_sem):
        x, y, c = _mesh_pos()
        mine, sent = cls._own(x_ref, o_ref, sems, local_sem)
        for j, (cx, cy) in enumerate(_other_chips(x, y)):
            _remote_copy(x_ref.at[2 * x + y], o_ref.at[2 * cx + cy], sems, j, (cx, cy, c)).wait_recv()
        for cp in sent:
            cp.wait_send()
        mine.wait()


def _run_exchange(kind, phase, x_refs, o_refs, send_sems, recv_sems, local_sems):
    for i, (x_ref, o_ref) in enumerate(zip(x_refs, o_refs)):
        getattr(kind, phase)(x_ref, o_ref, (send_sems, recv_sems, i * kind.n_sems), local_sems.at[i])


def _exchange_alone(kind, operands, name):
    def body(x_refs, o_refs, *sems):
        _run_exchange(kind, "start", x_refs, o_refs, *sems)
        _run_exchange(kind, "finish", x_refs, o_refs, *sems)

    return _comm_call(body, name, [kind.out_shape(a) for a in operands], kind.n_sems, operands)


def _all_gather(block, name):
    return _exchange_alone(_Gather, [block], name)[0]


def _carried(kind, operands, body, n_in, n_out, grid):
    hbm = pl.BlockSpec(memory_space=pl.ANY)
    n_x = len(operands)

    def wrapped(*refs):
        ins, x_refs = refs[:n_in], refs[n_in:n_in + n_x]
        outs = refs[n_in + n_x:n_in + n_x + n_out]
        o_refs = refs[n_in + n_x + n_out:n_in + 2 * n_x + n_out]
        sems = refs[n_in + 2 * n_x + n_out:n_in + 2 * n_x + n_out + 3]
        rest = refs[n_in + 2 * n_x + n_out + 3:]
        first, last = None, None
        for axis, size in enumerate(grid):
            at0, at1 = pl.program_id(axis) == 0, pl.program_id(axis) == size - 1
            first = at0 if first is None else first & at0
            last = at1 if last is None else last & at1
        pl.when(first)(lambda: _run_exchange(kind, "start", x_refs, o_refs, *sems))
        body(*ins, *outs, *rest)
        pl.when(last)(lambda: _run_exchange(kind, "finish", x_refs, o_refs, *sems))

    return wrapped, [hbm] * n_x, [hbm] * n_x, [kind.out_shape(a) for a in operands], _exchange_sems(n_x, kind.n_sems)


def _sibling_swap(parts, name):
    def body(x_refs, o_refs, send_sems, recv_sems, local_sems):
        x, y, c = _mesh_pos()
        copies = [_remote_copy(x_ref.at[2 * q + (1 - c)], o_ref.at[q], (send_sems, recv_sems, i * N_CHIPS), q, (x, y, 1 - c))
                  for i, (x_ref, o_ref) in enumerate(zip(x_refs, o_refs)) for q in range(N_CHIPS)]
        for cp in copies:
            cp.start()
        for cp in copies:
            cp.wait()

    return _comm_call(body, name, [_sds((N_CHIPS,) + p.shape[1:], p.dtype) for p in parts], N_CHIPS, parts)


def _pair_sum(a, b, name):
    n, rows, cols = a.shape
    tr = rows if rows <= 512 else _row_tile(rows, 2048)
    blk = pl.BlockSpec((1, tr, cols), lambda q, i: (q, i, 0))

    def body(a_ref, b_ref, o_ref):
        o_ref[...] = (a_ref[...].astype(F32) + b_ref[...].astype(F32)).astype(o_ref.dtype)

    return pl.pallas_call(
        body, name=name, grid=(n, rows // tr), in_specs=[blk, blk], out_specs=blk, out_shape=_sds(a.shape, a.dtype),
        compiler_params=_params(("parallel", "parallel")),
    )(a, b)


def _chip_sums(parts, name):
    c = lax.axis_index("c")
    from_sibling = _sibling_swap(parts, "swap_" + name)
    own = [lax.dynamic_index_in_dim(p.reshape((N_CHIPS, 2) + p.shape[1:]), c, axis=1, keepdims=False) for p in parts]
    return [_pair_sum(a, b, f"pair_sum_{name}_{k}") for k, (a, b) in enumerate(zip(own, from_sibling))]


def _reduce_to_owner(parts, name):
    return _exchange_alone(_ChipExchange, _chip_sums([parts], name), "exchange_" + name)[0]


def _sum_adamw(parts, w, m, v, name):
    rows, cols = w.shape
    n_parts = parts.shape[0]
    tr = rows if rows <= 512 else _row_tile(rows)
    blk = pl.BlockSpec((tr, cols), lambda i: (i, 0))

    def body(p_ref, w_ref, m_ref, v_ref, g_ref, d_ref, nm_ref, nv_ref):
        g = p_ref[0].astype(F32)
        for i in range(1, n_parts):
            g = g + p_ref[i].astype(F32)
        nm = ADAM_B1 * m_ref[...] + (1.0 - ADAM_B1) * g
        nv = ADAM_B2 * v_ref[...] + (1.0 - ADAM_B2) * jnp.square(g)
        m_hat = nm / (1.0 - ADAM_B1 ** ADAM_STEP)
        v_hat = nv / (1.0 - ADAM_B2 ** ADAM_STEP)
        g_ref[...] = g
        d_ref[...] = -ADAM_LR * (m_hat / (jnp.sqrt(v_hat) + ADAM_EPS) + ADAM_WD * w_ref[...])
        nm_ref[...] = nm
        nv_ref[...] = nv

    return pl.pallas_call(
        body, name=name, grid=(rows // tr,), in_specs=[pl.BlockSpec((n_parts, tr, cols), lambda i: (0, i, 0)), blk, blk, blk],
        out_specs=[blk] * 4, out_shape=[_sds((rows, cols))] * 4, compiler_params=_params(("parallel",)),
    )(parts, w, m, v)


def _row_tile(rows, pref=256):
    t = pref
    while t >= 8:
        if rows % t == 0:
            return t
        t //= 2
    return rows


def _gathered_cols(g):
    g = jnp.moveaxis(g, 0, -2)
    return g.reshape(g.shape[:-2] + (g.shape[-2] * g.shape[-1],))


def _gathered_rows(g):
    g = jnp.moveaxis(g, 0, -3)
    return g.reshape(g.shape[:-3] + (g.shape[-3] * g.shape[-2], g.shape[-1]))


def _col_parts(full):
    c = full.shape[-1]
    return jnp.moveaxis(full.reshape(full.shape[:-1] + (N_DEV, c // N_DEV)), -2, 0)


def _row_parts(full):
    rows, c = full.shape[-2:]
    return jnp.moveaxis(full.reshape(full.shape[:-2] + (N_DEV, rows // N_DEV, c)), -3, 0)


def _w_in_split(w_in):
    s = lambda a, n: w_in[..., a:a + n]
    main = jnp.concatenate([s(R_QKV, 3072), s(R_Z, 1024), s(R_QSW, 1024), s(R_G, 2048), s(R_KSW, 256), s(R_VSW, 256)], axis=-1)
    gates = jnp.pad(s(R_BA, 2 * N_HD), [(0, 0)] * (w_in.ndim - 1) + [(0, LANES - 2 * N_HD)])
    return main, gates


def _w_in_join(dmain, dgates):
    s = lambda a, n: dmain[..., a:a + n]
    return jnp.concatenate([s(C_QKV, 3072), s(C_Z, 1024), dgates[..., :2 * N_HD], s(C_QSW, 1024), s(C_KSW, 256), s(C_VSW, 256),
                            s(C_GA, 2048)], axis=-1)


def _lane_row(a, offset):
    l, n = a.shape
    return jnp.pad(a, ((0, 0), (offset, LANES - offset - n)))[:, None, :]


def kernel(x, w_in, conv_w, a_log, dt_bias, dn_norm_w, sinks, w_branch_a, w_branch_b, w_out, ln1_g, ln1_b, w_gate_up, w_down, ln2_g, ln2_b, loss_target, m_w_in, m_conv_w, m_a_log, m_dt_bias, m_dn_norm_w, m_sinks, m_w_branch_a, m_w_branch_b, m_w_out, m_ln1_g, m_ln1_b, m_w_gate_up, m_w_down, m_ln2_g, m_ln2_b, v_w_in, v_conv_w, v_a_log, v_dt_bias, v_dn_norm_w, v_sinks, v_w_branch_a, v_w_branch_b, v_w_out, v_ln1_g, v_ln1_b, v_w_gate_up, v_w_down, v_ln2_g, v_ln2_b):
    l = DEPTH
    bf = lambda a: a.astype(_MXU)
    shards = [bf(w_in), bf(w_gate_up), bf(w_branch_a), bf(w_branch_b), bf(w_out), bf(w_down)]
    first = _exchange_alone(_Gather, [s[0] for s in shards], "gather_layer_0")
    conv_full = _gathered_cols(_all_gather(conv_w, "gather_conv_w"))
    row = lambda a: a[:, None, :]
    small = dict(
        conv=jnp.pad(conv_full, ((0, 0), (0, 8 - DN_CONV), (0, 0))), alog=_lane_row(a_log.reshape(l, N_HD), N_HD),
        dtb=_lane_row(dt_bias.reshape(l, N_HD), N_HD), dnw=row(dn_norm_w), sinks=sinks.reshape(l, SW_HEADS, 1, 1),
        ln1g=row(ln1_g), ln1b=row(ln1_b), ln2g=row(ln2_g), ln2b=row(ln2_b))

    def layer_weights(i, carried):
        s_in, s_gu, s_a, s_b, s_o, s_d = first if i == 0 else carried
        wm, wba = _w_in_split(_gathered_cols(s_in))
        return dict(wm=wm, wba=wba, wgu=_gathered_cols(s_gu), wa=_gathered_rows(s_a), wb=_gathered_rows(s_b),
                    wo=_gathered_rows(s_o), wd=_gathered_rows(s_d), **{k: a[i] for k, a in small.items()})

    def fwd_carry(i):
        return (_Gather, [s[i + 1] for s in shards]) if i + 1 < l else None

    layer_grads, received, waiting = [None] * l, [None] * l, []

    def grads_done(i, g_i, carried):
        if waiting:
            received[waiting.pop()] = carried
        layer_grads[i] = g_i
        parts = [_col_parts(_w_in_join(g_i["wm"], g_i["wba"])), _col_parts(g_i["wgu"]), _row_parts(g_i["wa"]),
                 _row_parts(g_i["wb"]), _row_parts(g_i["wo"]), _row_parts(g_i["wd"])]
        waiting.append(i)
        return (_ChipExchange, _chip_sums(parts, f"layer_{i}"))

    sq, dx, last = _trunk(x[0], loss_target[0], l, layer_weights, fwd_carry, grads_done)
    received[waiting.pop()] = _exchange_alone(last[0], last[1], "exchange_layer_0")
    loss = lax.psum(0.5 * sq[0, 0] / D_MODEL, ("x", "y", "c"))
    g = {k: jnp.stack([gi[k] for gi in layer_grads]) for k in small}

    def adamw(parts, w, m, v, name):
        rows = w.shape[0] * w.shape[1]
        flat = lambda a: a.reshape(rows, a.shape[-1])
        outs = _sum_adamw(parts.reshape(parts.shape[0], rows, w.shape[-1]), flat(w), flat(m), flat(v), "adamw_" + name)
        return [o.reshape(w.shape) for o in outs]

    got = [jnp.stack(per_layer, axis=1) for per_layer in zip(*received)]
    dconv = _reduce_to_owner(_col_parts(g["conv"][:, :DN_CONV, :]).reshape(N_DEV, l * DN_CONV, -1), "conv_w")
    results = {
        "w_in": adamw(got[0], w_in, m_w_in, v_w_in, "w_in"),
        "conv_w": adamw(dconv.reshape(N_CHIPS, l, DN_CONV, -1), conv_w, m_conv_w, v_conv_w, "conv_w"),
        "w_branch_a": adamw(got[2], w_branch_a, m_w_branch_a, v_w_branch_a, "w_branch_a"),
        "w_branch_b": adamw(got[3], w_branch_b, m_w_branch_b, v_w_branch_b, "w_branch_b"),
        "w_out": adamw(got[4], w_out, m_w_out, v_w_out, "w_out"),
        "w_gate_up": adamw(got[1], w_gate_up, m_w_gate_up, v_w_gate_up, "w_gate_up"),
        "w_down": adamw(got[5], w_down, m_w_down, v_w_down, "w_down"),
    }

    small_w = {"a_log": a_log.reshape(l, N_HD), "dt_bias": dt_bias.reshape(l, N_HD), "dn_norm_w": dn_norm_w, "sinks": sinks,
               "ln1_g": ln1_g, "ln1_b": ln1_b, "ln2_g": ln2_g, "ln2_b": ln2_b}
    small_m = {"a_log": m_a_log, "dt_bias": m_dt_bias, "dn_norm_w": m_dn_norm_w, "sinks": m_sinks, "ln1_g": m_ln1_g,
               "ln1_b": m_ln1_b, "ln2_g": m_ln2_g, "ln2_b": m_ln2_b}
    small_v = {"a_log": v_a_log, "dt_bias": v_dt_bias, "dn_norm_w": v_dn_norm_w, "sinks": v_sinks, "ln1_g": v_ln1_g,
               "ln1_b": v_ln1_b, "ln2_g": v_ln2_g, "ln2_b": v_ln2_b}
    small_g = {"a_log": g["alog"][:, 0, N_HD:2 * N_HD], "dt_bias": g["dtb"][:, 0, N_HD:2 * N_HD], "dn_norm_w": g["dnw"][:, 0, :],
               "sinks": g["sinks"].reshape(l, SW_HEADS), "ln1_g": g["ln1g"][:, 0, :], "ln1_b": g["ln1b"][:, 0, :],
               "ln2_g": g["ln2g"][:, 0, :], "ln2_b": g["ln2b"][:, 0, :]}
    names = list(small_w)
    cat = lambda d: jnp.concatenate([d[n].reshape(l, -1) for n in names], axis=1)
    widths = [small_w[n].shape[1] for n in names]
    total = sum(widths)
    padded = -(-total // LANES) * LANES
    pad = lambda a: jnp.pad(a, ((0, 8 - l), (0, padded - total)))
    got = _all_gather(pad(cat(small_g)), "gather_small_grads")
    outs = _sum_adamw(got, pad(cat(small_w)), pad(cat({n: small_m[n].reshape(l, -1) for n in names})),
                      pad(cat({n: small_v[n].reshape(l, -1) for n in names})), "adamw_small")
    off = 0
    for n, wd_ in zip(names, widths):
        shape = {"a_log": a_log.shape, "dt_bias": dt_bias.shape}.get(n, small_w[n].shape)
        results[n] = [o[:l, off:off + wd_].reshape(shape) for o in outs]
        off += wd_

    order = ["w_in", "conv_w", "a_log", "dt_bias", "dn_norm_w", "sinks", "w_branch_a", "w_branch_b", "w_out", "ln1_g", "ln1_b",
             "w_gate_up", "w_down", "ln2_g", "ln2_b"]
    return (loss, dx[None], *[results[n][0] for n in order], *[results[n][1] for n in order],
            *[results[n][2] for n in order], *[results[n][3] for n in order])
```

```python
import functools

import jax
import jax.numpy as jnp
from jax import lax
from jax.experimental import pallas as pl
from jax.experimental.pallas import tpu as pltpu

F32 = jnp.float32
_MXU = jnp.bfloat16
_HI = lax.Precision.HIGHEST
_MID = lax.Precision.HIGH

N_DEV = 8
D_MODEL = 1024
DEPTH = 4
DN_HEADS = 8
DN_HEAD_DIM = 128
DN_WIDTH = DN_HEADS * DN_HEAD_DIM
DN_CONV = 5
DN_CHUNK = 64
SW_HEADS = 16
SW_KV_HEADS = 4
SW_HEAD_DIM = 64
SW_GROUP = SW_HEADS // SW_KV_HEADS
SW_BLOCK = 128
SW_KV_WIDTH = SW_KV_HEADS * SW_HEAD_DIM
ROPE_THETA = 10000.0
FFN_HIDDEN = 2816
DN_ALPHA = (2.0 * DEPTH) ** 0.25
LN_EPS = 1e-5
RMS_EPS = 1e-6
ADAM_LR = 0.001
ADAM_B1 = 0.9
ADAM_B2 = 0.999
ADAM_EPS = 1e-08
ADAM_WD = 0.01
ADAM_STEP = 10

LANES = 128
N_HD = 2 * DN_HEADS
DN_GROUP = 4 * DN_CHUNK
INV_SUB = 16
LOCAL_ROWS_FWD = 2048
LOCAL_ROWS_BWD = 512
REC_ROWS = 512
ROW_TILE = 512
VMEM_LIMIT = 48 << 20

C_QKV, C_Z, C_QSW, C_GA, C_GB, C_KSW, C_VSW = 0, 3072, 4096, 5120, 6144, 7168, 7424
MAIN_COLS = 7680
R_QKV, R_Z, R_BA, R_QSW, R_KSW, R_VSW, R_G = 0, 3072, 4096, 4128, 5152, 5408, 5664
IN_COLS = 7712


_NN = ((1,), (0,))
_NT = ((1,), (1,))
_TN = ((0,), (0,))


def _dg(a, b, dims, precision):
    if precision is not None:
        return lax.dot_general(a, b, (dims, ((), ())), precision=precision, preferred_element_type=F32)
    return lax.dot_general(a.astype(_MXU), b.astype(_MXU), (dims, ((), ())), preferred_element_type=F32)


def _make_dots(hi):
    @jax.custom_vjp
    def nn(a, b):
        return _dg(a, b, _NN, hi)

    @jax.custom_vjp
    def nt(a, b):
        return _dg(a, b, _NT, hi)

    @jax.custom_vjp
    def tn(a, b):
        return _dg(a, b, _TN, hi)

    nn.defvjp(lambda a, b: (nn(a, b), (a, b)), lambda r, g: (nt(g, r[1]), tn(r[0], g)))
    nt.defvjp(lambda a, b: (nt(a, b), (a, b)), lambda r, g: (nn(g, r[1]), tn(g, r[0])))
    tn.defvjp(lambda a, b: (tn(a, b), (a, b)), lambda r, g: (nt(r[1], g), nn(r[0], g)))
    return nn, nt, tn


_bnn, _bnt, _btn = _make_dots(None)
_hnn, _hnt, _htn = _make_dots(_HI)
_mnn, _mnt, _mtn = _make_dots(_MID)


def _neumann(a, order):
    n = a.shape[0]
    eye = (lax.broadcasted_iota(jnp.int32, (n, n), 0) == lax.broadcasted_iota(jnp.int32, (n, n), 1)).astype(F32)
    inv = eye - a
    p = a
    span = 2
    while span < order:
        p = _mnn(p, p)
        inv = inv + _mnn(inv, p)
        span *= 2
    return inv


def _inv_unit(a, order):
    n = a.shape[0]
    ii = lax.broadcasted_iota(jnp.int32, (n, n), 0)
    jj = lax.broadcasted_iota(jnp.int32, (n, n), 1)
    near = (ii & -INV_SUB) == (jj & -INV_SUB)
    d_inv = _neumann(jnp.where(near, a, 0.0), INV_SUB)
    outer = _neumann(_mnn(d_inv, jnp.where(near, 0.0, a)), order // INV_SUB)
    return _mnn(outer, d_inv)


def _inv_unit_t(t, g):
    return -_mnt(_mtn(t, g), t)


def _silu(x):
    return x * jax.nn.sigmoid(x)


def _softplus(x):
    return jnp.maximum(x, 0.0) + jnp.log1p(jnp.exp(-jnp.abs(x)))


def _params(sem=None):
    kw = {"vmem_limit_bytes": VMEM_LIMIT}
    if sem is not None:
        kw["dimension_semantics"] = sem
    return pltpu.CompilerParams(**kw)


def _tile(dim, pref):
    if dim <= pref:
        return dim
    t = (pref // LANES) * LANES
    while t > LANES and dim % t:
        t -= LANES
    assert dim % t == 0, (dim, pref)
    return t


def _full(shape):
    zeros = (0,) * len(shape)
    return pl.BlockSpec(shape, lambda *_: zeros)


def _sds(shape, dtype=F32):
    return jax.ShapeDtypeStruct(shape, dtype)


def _mm(a, b, mode, *, name, add=None, tm=1536, tn=1536, tk=1536, out_dtype=F32):
    if mode == "nn":
        (m, k), (k2, n) = a.shape, b.shape
    elif mode == "nt":
        (m, k), (n, k2) = a.shape, b.shape
    else:
        (k, m), (k2, n) = a.shape, b.shape
    assert k == k2, (a.shape, b.shape, mode)
    tm, tn, tk = _tile(m, tm), _tile(n, tn), _tile(k, tk)
    nk = k // tk
    dims = {"nn": _NN, "nt": _NT, "tn": _TN}[mode]

    def body(*refs):
        if add is None:
            a_ref, b_ref, o_ref, acc = refs
        else:
            a_ref, b_ref, add_ref, o_ref, acc = refs
        kk = pl.program_id(2)

        @pl.when(kk == 0)
        def _():
            acc[...] = jnp.zeros_like(acc)

        acc[...] += _dg(a_ref[...], b_ref[...], dims, None)

        @pl.when(kk == nk - 1)
        def _():
            o_ref[...] = (acc[...] if add is None else acc[...] + add_ref[...]).astype(out_dtype)

    a_spec = pl.BlockSpec((tk, tm), lambda i, j, kk: (kk, i)) if mode == "tn" else pl.BlockSpec((tm, tk), lambda i, j, kk: (i, kk))
    b_spec = pl.BlockSpec((tn, tk), lambda i, j, kk: (j, kk)) if mode == "nt" else pl.BlockSpec((tk, tn), lambda i, j, kk: (kk, j))
    o_spec = pl.BlockSpec((tm, tn), lambda i, j, kk: (i, j))
    ins, specs = [a, b], [a_spec, b_spec]
    if add is not None:
        ins.append(add)
        specs.append(o_spec)
    return pl.pallas_call(
        body, name=name, grid=(m // tm, n // tn, nk), in_specs=specs, out_specs=o_spec,
        out_shape=_sds((m, n), out_dtype), scratch_shapes=[pltpu.VMEM((tm, tn), F32)],
        compiler_params=_params(("parallel", "parallel", "arbitrary")),
    )(*ins)


def _cols(width, start):
    assert start % width == 0
    return pl.BlockSpec((ROW_TILE, width), lambda i: (i, start // width))


def _rows(width):
    return pl.BlockSpec((ROW_TILE, width), lambda i: (i, 0))


def _accumulate(ref, value, step):
    @pl.when(step == 0)
    def _():
        ref[...] = value

    @pl.when(step != 0)
    def _():
        ref[...] += value


def _ln_fn(x, r, g, b):
    u = DN_ALPHA * x + r
    mu = jnp.mean(u, axis=-1, keepdims=True)
    var = jnp.mean(jnp.square(u - mu), axis=-1, keepdims=True)
    return (u - mu) * lax.rsqrt(var + LN_EPS) * g + b


def _ln_fwd(x, r, g, b, name):
    t, d = x.shape

    def body(x_ref, r_ref, g_ref, b_ref, o_ref, ob_ref):
        y = _ln_fn(x_ref[...], r_ref[...], g_ref[...], b_ref[...])
        o_ref[...] = y
        ob_ref[...] = y.astype(_MXU)

    return pl.pallas_call(
        body, name=name, grid=(t // ROW_TILE,), in_specs=[_rows(d), _rows(d), _full((1, d)), _full((1, d))],
        out_specs=[_rows(d), _rows(d)], out_shape=[_sds((t, d)), _sds((t, d), _MXU)], compiler_params=_params(("parallel",)),
    )(x, r, g, b)


def _ln_bwd(x, r, g, b, dy, name):
    t, d = x.shape

    def body(x_ref, r_ref, g_ref, b_ref, dy_ref, dx_ref, dr_ref, dg_ref, db_ref):
        _, vjp = jax.vjp(_ln_fn, x_ref[...], r_ref[...], g_ref[...], b_ref[...])
        dx, dr, dg, db = vjp(dy_ref[...])
        dx_ref[...] = dx
        dr_ref[...] = dr.astype(_MXU)
        _accumulate(dg_ref, dg, pl.program_id(0))
        _accumulate(db_ref, db, pl.program_id(0))

    return pl.pallas_call(
        body, name=name, grid=(t // ROW_TILE,),
        in_specs=[_rows(d), _rows(d), _full((1, d)), _full((1, d)), _rows(d)],
        out_specs=[_rows(d), _rows(d), _full((1, d)), _full((1, d))],
        out_shape=[_sds((t, d)), _sds((t, d), _MXU), _sds((1, d)), _sds((1, d))],
        compiler_params=_params(("arbitrary",)),
    )(x, r, g, b, dy)


def _merge_fn(ga, gb, ya, yb):
    return jax.nn.sigmoid(ga) * ya + jax.nn.sigmoid(gb) * yb


def _merge_fwd(proj, ya, yb):
    t, d = ya.shape

    def body(ga_ref, gb_ref, ya_ref, yb_ref, o_ref):
        o_ref[...] = _merge_fn(ga_ref[...], gb_ref[...], ya_ref[...], yb_ref[...]).astype(_MXU)

    return pl.pallas_call(
        body, name="merge_fwd", grid=(t // ROW_TILE,), in_specs=[_cols(d, C_GA), _cols(d, C_GB), _rows(d), _rows(d)],
        out_specs=_rows(d), out_shape=_sds((t, d), _MXU), compiler_params=_params(("parallel",)),
    )(proj, proj, ya, yb)


def _merge_bwd(proj, ya, yb, dm):
    t, d = ya.shape

    def body(ga_ref, gb_ref, ya_ref, yb_ref, dm_ref, dga_ref, dgb_ref, dya_ref, dyb_ref):
        _, vjp = jax.vjp(_merge_fn, ga_ref[...], gb_ref[...], ya_ref[...], yb_ref[...])
        dga_ref[...], dgb_ref[...], dya_ref[...], dyb_ref[...] = [g.astype(_MXU) for g in vjp(dm_ref[...])]

    return pl.pallas_call(
        body, name="merge_bwd", grid=(t // ROW_TILE,),
        in_specs=[_cols(d, C_GA), _cols(d, C_GB), _rows(d), _rows(d), _rows(d)],
        out_specs=[_rows(d)] * 4, out_shape=[_sds((t, d), _MXU)] * 4, compiler_params=_params(("parallel",)),
    )(proj, proj, ya, yb, dm)


def _swiglu_fn(gate, up):
    return _silu(gate) * up


def _swiglu_fwd(gu):
    t = gu.shape[0]
    f = FFN_HIDDEN
    rows = 256

    def body(gu_ref, o_ref):
        o_ref[...] = _swiglu_fn(gu_ref[:, :f], gu_ref[:, f:]).astype(_MXU)

    return pl.pallas_call(
        body, name="swiglu_fwd", grid=(t // rows,), in_specs=[pl.BlockSpec((rows, 2 * f), lambda i: (i, 0))],
        out_specs=pl.BlockSpec((rows, f), lambda i: (i, 0)), out_shape=_sds((t, f), _MXU), compiler_params=_params(("parallel",)),
    )(gu)


def _swiglu_bwd(gu, dh):
    t = gu.shape[0]
    f = FFN_HIDDEN
    rows = 256

    def body(gu_ref, dh_ref, o_ref):
        _, vjp = jax.vjp(_swiglu_fn, gu_ref[:, :f], gu_ref[:, f:])
        o_ref[:, :f], o_ref[:, f:] = [g.astype(_MXU) for g in vjp(dh_ref[...])]

    return pl.pallas_call(
        body, name="swiglu_bwd", grid=(t // rows,),
        in_specs=[pl.BlockSpec((rows, 2 * f), lambda i: (i, 0)), pl.BlockSpec((rows, f), lambda i: (i, 0))],
        out_specs=pl.BlockSpec((rows, 2 * f), lambda i: (i, 0)), out_shape=_sds((t, 2 * f), _MXU),
        compiler_params=_params(("parallel",)),
    )(gu, dh)


def _loss_head(y, target):
    t, d = y.shape

    def body(y_ref, t_ref, s_ref, dy_ref):
        err = y_ref[...] - t_ref[...]
        dy_ref[...] = err / d
        _accumulate(s_ref, jnp.broadcast_to(jnp.sum(jnp.square(err)), (1, LANES)), pl.program_id(0))

    return pl.pallas_call(
        body, name="loss_head", grid=(t // ROW_TILE,), in_specs=[_rows(d), _rows(d)],
        out_specs=[_full((1, LANES)), _rows(d)], out_shape=[_sds((1, LANES)), _sds((t, d))],
        compiler_params=_params(("arbitrary",)),
    )(y, target)


def _shift_rows(x, s):
    if s == 0:
        return x
    return pltpu.roll(x, (-s) % x.shape[0], 0)


def _conv(x, w):
    half = DN_CONV // 2
    acc = None
    for k in range(DN_CONV):
        term = _shift_rows(x, k - half) * w[k:k + 1, :]
        acc = term if acc is None else acc + term
    return acc


def _act_norm(c, do_norm, scale):
    a = _silu(c)
    if not do_norm:
        return a
    return a * lax.rsqrt(jnp.sum(a * a, axis=-1, keepdims=True) + RMS_EPS) * scale


PREP_ROWS = 512
HALO = 8
_KINDS = ((True, DN_HEAD_DIM ** -0.5), (True, 1.0), (False, 1.0))


def _halo_rows(read, i, pr, t):
    lo, hi = i * pr - HALO, (i + 1) * pr + HALO
    parts = []
    if lo < 0:
        parts.append(jnp.zeros((HALO, LANES), F32))
    parts.append(read(max(lo, 0), min(hi, t)))
    if hi > t:
        parts.append(jnp.zeros((HALO, LANES), F32))
    return jnp.concatenate(parts, axis=0) if len(parts) > 1 else parts[0]


def _prep_fwd(proj, conv_w, kind):
    t = proj.shape[0]
    pr = min(PREP_ROWS, t)
    do_norm, scale = _KINDS[kind]
    blk = pl.BlockSpec((t, LANES), lambda j: (0, kind * DN_HEADS + j))

    def body(x_ref, w_ref, o_ref):
        w = w_ref[...]
        for i in range(t // pr):
            xx = _halo_rows(lambda lo, hi: x_ref[lo:hi, :], i, pr, t)
            c = _conv(xx, w)[HALO:HALO + pr, :]
            o_ref[i * pr:(i + 1) * pr, :] = _act_norm(c, do_norm, scale)

    return pl.pallas_call(
        body, name=f"prep_fwd_{kind}", grid=(DN_HEADS,),
        in_specs=[blk, pl.BlockSpec((8, LANES), lambda j: (0, kind * DN_HEADS + j))],
        out_specs=pl.BlockSpec((t, LANES), lambda j: (0, j)), out_shape=_sds((t, DN_WIDTH)),
        compiler_params=_params(("parallel",)),
    )(proj, conv_w)


def _prep_bwd(proj, conv_w, d2, kind):
    t = proj.shape[0]
    pr = min(PREP_ROWS, t)
    do_norm, scale = _KINDS[kind]
    half = DN_CONV // 2
    blk = pl.BlockSpec((t, LANES), lambda j: (0, kind * DN_HEADS + j))
    oblk = pl.BlockSpec((t, LANES), lambda j: (0, j))

    def body(x_ref, w_ref, d_ref, dx_ref, dw_ref):
        w = w_ref[...]
        own = slice(HALO, HALO + pr)
        dw = jnp.zeros((8, LANES), F32)
        for i in range(t // pr):
            xx = _halo_rows(lambda lo, hi: x_ref[lo:hi, :], i, pr, t)
            dn = _halo_rows(lambda lo, hi: d_ref[0, lo:hi, :] + d_ref[1, lo:hi, :], i, pr, t)
            _, vjp = jax.vjp(lambda c: _act_norm(c, do_norm, scale), _conv(xx, w))
            (dc,) = vjp(dn)
            dx = None
            rows = []
            for k in range(DN_CONV):
                term = _shift_rows(dc, half - k) * w[k:k + 1, :]
                dx = term if dx is None else dx + term
                rows.append(jnp.sum(dc[own, :] * _shift_rows(xx, k - half)[own, :], axis=0, keepdims=True))
            dx_ref[i * pr:(i + 1) * pr, :] = dx[own, :].astype(_MXU)
            dw = dw + jnp.concatenate(rows + [jnp.zeros((8 - DN_CONV, LANES), F32)], axis=0)
        dw_ref[...] = dw

    return pl.pallas_call(
        body, name=f"prep_bwd_{kind}", grid=(DN_HEADS,),
        in_specs=[blk, pl.BlockSpec((8, LANES), lambda j: (0, kind * DN_HEADS + j)), pl.BlockSpec((2, t, LANES), lambda j: (0, 0, j))],
        out_specs=[oblk, pl.BlockSpec((8, LANES), lambda j: (0, j))], out_shape=[_sds((t, DN_WIDTH), _MXU), _sds((8, DN_WIDTH))],
        compiler_params=_params(("parallel",)),
    )(proj, conv_w, d2)


def _gb_fn(ba, alog_row, dtb_row):
    c = DN_CHUNK
    lane = lax.broadcasted_iota(jnp.int32, (c, LANES), 1)
    ii = lax.broadcasted_iota(jnp.int32, (c, c), 0)
    jj = lax.broadcasted_iota(jnp.int32, (c, c), 1)
    beta = jax.nn.sigmoid(ba)
    g = -jnp.exp(alog_row) * _softplus(ba + dtb_row)
    g = jnp.where((lane >= N_HD) & (lane < 2 * N_HD), g, 0.0)
    gc_fwd = _hnn((ii >= jj).astype(F32), g)
    gc_rev = _hnn((ii <= jj).astype(F32), g)
    gc = jnp.where(lane < N_HD + DN_HEADS, gc_fwd, gc_rev)
    return jnp.where(lane < N_HD, beta, jnp.where(lane < 2 * N_HD, gc, 0.0))


def _per_head_spec():
    return pl.BlockSpec((N_HD, ROW_TILE, LANES), lambda i: (0, i, 0))


def _gb_fwd(ba, alog_row, dtb_row):
    t = ba.shape[0]
    n = ROW_TILE // DN_CHUNK

    def body(ba_ref, a_ref, d_ref, o_ref, beta_ref, gc_ref):
        for c in range(n):
            rows = slice(c * DN_CHUNK, (c + 1) * DN_CHUNK)
            out = _gb_fn(ba_ref[rows, :], a_ref[...], d_ref[...])
            o_ref[rows, :] = out
            for j in range(N_HD):
                beta_ref[j, rows, :] = jnp.broadcast_to(out[:, j:j + 1], (DN_CHUNK, LANES))
                gc_ref[j, rows, :] = jnp.broadcast_to(out[:, N_HD + j:N_HD + j + 1], (DN_CHUNK, LANES))

    per_head = _sds((N_HD, t, LANES))
    return pl.pallas_call(
        body, name="gates_fwd", grid=(t // ROW_TILE,), in_specs=[_rows(LANES), _full((1, LANES)), _full((1, LANES))],
        out_specs=[_rows(LANES), _per_head_spec(), _per_head_spec()], out_shape=[_sds((t, LANES)), per_head, per_head],
        compiler_params=_params(("parallel",)),
    )(ba, alog_row, dtb_row)


def _gb_bwd(ba, alog_row, dtb_row, dbeta, dgc, d_rows):
    t = ba.shape[0]
    n = ROW_TILE // DN_CHUNK

    def body(ba_ref, a_ref, d_ref, dbeta_ref, dgc_ref, dr_ref, dba_ref, dal_ref, ddt_ref):
        dal = jnp.zeros((1, LANES), F32)
        ddt = jnp.zeros((1, LANES), F32)
        lane = lax.broadcasted_iota(jnp.int32, (DN_CHUNK, LANES), 1)
        for c in range(n):
            rows = slice(c * DN_CHUNK, (c + 1) * DN_CHUNK)
            cot = dr_ref[rows, :]
            for j in range(N_HD):
                cot = jnp.where(lane == j, dbeta_ref[j, rows, :], cot)
                cot = jnp.where(lane == N_HD + j, dgc_ref[j, rows, :] + cot, cot)
            _, vjp = jax.vjp(_gb_fn, ba_ref[rows, :], a_ref[...], d_ref[...])
            dba, da, dd = vjp(cot)
            dba_ref[rows, :] = dba.astype(_MXU)
            dal = dal + da
            ddt = ddt + dd
        _accumulate(dal_ref, dal, pl.program_id(0))
        _accumulate(ddt_ref, ddt, pl.program_id(0))

    return pl.pallas_call(
        body, name="gates_bwd", grid=(t // ROW_TILE,),
        in_specs=[_rows(LANES), _full((1, LANES)), _full((1, LANES)), _per_head_spec(), _per_head_spec(), _rows(LANES)],
        out_specs=[_rows(LANES), _full((1, LANES)), _full((1, LANES))],
        out_shape=[_sds((t, LANES), _MXU), _sds((1, LANES)), _sds((1, LANES))], compiler_params=_params(("arbitrary",)),
    )(ba, alog_row, dtb_row, dbeta, dgc, d_rows)


def _dn_decay(gcc, gcr, sgn):
    c = DN_CHUNK
    ii = lax.broadcasted_iota(jnp.int32, (c, c), 0)
    jj = lax.broadcasted_iota(jnp.int32, (c, c), 1)
    d = (ii - jj) * sgn
    lower = d >= 0
    return jnp.where(lower, jnp.exp(jnp.where(lower, gcc - gcr, 0.0)), 0.0), d > 0


def _dn_a(k, beta, gcc, gcr, sgn):
    decay, strict = _dn_decay(gcc, gcr, sgn)
    return jnp.where(strict, beta * _bnt(k, k) * decay, 0.0)


def _dn_group(q, k, v, beta, gcc, gcr, sgn):
    n = DN_GROUP
    ii = lax.broadcasted_iota(jnp.int32, (n, n), 0)
    jj = lax.broadcasted_iota(jnp.int32, (n, n), 1)
    same = (ii & -DN_CHUNK) == (jj & -DN_CHUNK)
    d = (ii - jj) * sgn
    lower = same & (d >= 0)
    decay = jnp.where(lower, jnp.exp(jnp.where(lower, gcc - gcr, 0.0)), 0.0)
    a = jnp.where(same & (d > 0), beta * _bnt(k, k) * decay, 0.0)
    t_inv = _inv_unit(a, DN_CHUNK)
    u = _bnn(t_inv, v * beta)
    w = _bnn(t_inv, k * (beta * jnp.exp(gcc)))
    return u, w, _bnt(q, k) * decay, t_inv


def _dn_local(t_inv, q, k, v, beta, gcc, gcr, sgn):
    c = DN_CHUNK
    decay, _ = _dn_decay(gcc, gcr, sgn)
    eg = jnp.exp(gcc)
    u = _bnn(t_inv, v * beta)
    w = _bnn(t_inv, k * (beta * eg))
    qk = _bnt(q, k) * decay
    qd = q * eg
    last = jnp.where(sgn > 0, c - 1, 0)
    onehot = (lax.broadcasted_iota(jnp.int32, (c, 1), 0) == last).astype(F32)
    gl = jnp.sum(gcc * onehot, axis=0, keepdims=True)
    kd = k * jnp.exp(gl - gcc)
    egl = jnp.broadcast_to(jnp.exp(gl), (1, LANES))
    return u, w, qk, qd, kd, egl


def _hd_sign(hd):
    return jnp.where(hd < DN_HEADS, 1, -1).astype(jnp.int32)


def _head_of(hd):
    return jnp.where(hd < DN_HEADS, hd, hd - DN_HEADS)


def _dir_of(hd):
    return jnp.where(hd < DN_HEADS, 0, 1)


def _dn_specs(rows_step):
    nl = rows_step // DN_CHUNK
    wide = pl.BlockSpec((1, rows_step, LANES), lambda hd, i: (hd, i, 0))
    half = pl.BlockSpec((1, rows_step, DN_CHUNK), lambda hd, i: (hd, i, 0))
    col = wide
    row = pl.BlockSpec((1, nl, 1, DN_CHUNK), lambda hd, i: (hd, i, 0, 0))
    egl = pl.BlockSpec((1, nl, 1, LANES), lambda hd, i: (hd, i, 0, 0))
    return wide, half, col, row, egl


def _qkv_specs(rows_step):
    return [pl.BlockSpec((rows_step, LANES), lambda hd, i: (i, _head_of(hd)))] * 3


def _maybe_carrying(carry, body, name, grid, operands, in_specs, out_specs, out_shape):
    extra_scratch = []
    if carry is not None:
        kind, arrays = carry
        body, more_in, more_out, more_shape, extra_scratch = _carried(kind, arrays, body, len(operands), len(out_shape), grid)
        operands, in_specs = operands + list(arrays), in_specs + more_in
        out_specs, out_shape = out_specs + more_out, out_shape + more_shape
    sem = ("arbitrary",) * len(grid) if carry is not None else ("parallel",) * len(grid)
    return pl.pallas_call(
        body, name=name, grid=grid, in_specs=in_specs, out_specs=out_specs, out_shape=out_shape,
        scratch_shapes=extra_scratch, compiler_params=_params(sem),
    )(*operands)


def _dn_local_fwd(q, k, v, beta_c, gc_c, gc_r, carry=None):
    t = q.shape[0]
    nc = t // DN_CHUNK
    rows_step = min(LOCAL_ROWS_FWD, t)
    wide, half, col, row, egl = _dn_specs(rows_step)

    ng = rows_step // DN_GROUP
    per = DN_GROUP // DN_CHUNK
    grow = pl.BlockSpec((1, ng, 1, DN_GROUP), lambda hd, i: (hd, i, 0, 0))

    def body(q_ref, k_ref, v_ref, b_ref, gc_ref, gg_ref, u_ref, w_ref, qk_ref, qd_ref, kd_ref, egl_ref, t_ref):
        sgn = _hd_sign(pl.program_id(0))
        last = jnp.where(sgn > 0, DN_CHUNK - 1, 0)
        onehot = (lax.broadcasted_iota(jnp.int32, (DN_CHUNK, 1), 0) == last).astype(F32)
        groups = lambda a: a.reshape((ng, DN_GROUP) + a.shape[1:])
        q_all, k_all, gcc_all = q_ref[...], k_ref[...], gc_ref[0][:, :1]
        u, w, qk, t_inv = jax.vmap(functools.partial(_dn_group, sgn=sgn))(
            groups(q_all), groups(k_all), groups(v_ref[...]), groups(b_ref[0][:, :1]), groups(gcc_all), gg_ref[0])
        u_ref[0] = u.reshape(rows_step, LANES)
        w_ref[0] = w.reshape(rows_step, LANES).astype(_MXU)
        qd_ref[0] = (q_all * jnp.exp(gcc_all)).astype(_MXU)
        for gi in range(ng):
            for c in range(per):
                blk = slice(c * DN_CHUNK, (c + 1) * DN_CHUNK)
                rows = slice(gi * DN_GROUP + c * DN_CHUNK, gi * DN_GROUP + (c + 1) * DN_CHUNK)
                qk_ref[0, rows, :] = qk[gi, blk, blk].astype(_MXU)
                t_ref[0, rows, :] = t_inv[gi, blk, blk]
                gl = jnp.sum(gcc_all[rows, :] * onehot, axis=0, keepdims=True)
                kd_ref[0, rows, :] = (k_all[rows, :] * jnp.exp(gl - gcc_all[rows, :])).astype(_MXU)
                egl_ref[0, gi * per + c] = jnp.broadcast_to(jnp.exp(gl), (1, LANES))

    big = _sds((N_HD, t, LANES))
    small = _sds((N_HD, t, DN_CHUNK))
    operands = [q, k, v, beta_c, gc_c, gc_r.reshape(N_HD, t // DN_GROUP, 1, DN_GROUP)]
    return _maybe_carrying(
        carry, body, "dn_local_fwd", (N_HD, t // rows_step), operands, _qkv_specs(rows_step) + [col, col, grow],
        [wide, wide, half, wide, wide, egl, half],
        [big, _sds(big.shape, _MXU), _sds(small.shape, _MXU), _sds(big.shape, _MXU), _sds(big.shape, _MXU),
         _sds((N_HD, nc, 1, LANES)), small])


def _dn_local_bwd(q, k, v, beta_c, gc_c, gc_r, t_inv, du, dw, dqk, dqd, dkd, degl, carry=None):
    t = q.shape[0]
    nc = t // DN_CHUNK
    rows_step = min(LOCAL_ROWS_BWD, t)
    nl = rows_step // DN_CHUNK
    wide, half, col, row, egl = _dn_specs(rows_step)
    dspec = pl.BlockSpec((1, rows_step, LANES), lambda hd, i: (_dir_of(hd), i, _head_of(hd)))

    def body(q_ref, k_ref, v_ref, b_ref, gc_ref, gr_ref, t_ref, du_ref, dw_ref, dqk_ref, dqd_ref, dkd_ref, degl_ref,
             dq_ref, dk_ref, dv_ref, db_ref, dgc_ref, dgr_ref):
        sgn = _hd_sign(pl.program_id(0))

        def chunk_bwd(tinv, q, k, v, beta, gcc, gcr, du, dw, dqk, dqd, dkd, degl):
            _, vjp = jax.vjp(functools.partial(_dn_local, sgn=sgn), tinv, q, k, v, beta, gcc, gcr)
            dt, dq, dk, dv, db, dgc, dgr = vjp((du, dw, dqk, dqd, dkd, degl))
            _, vjp_a = jax.vjp(functools.partial(_dn_a, sgn=sgn), k, beta, gcc, gcr)
            dk2, db2, dgc2, dgr2 = vjp_a(_inv_unit_t(tinv, dt))
            return dq, dk + dk2, dv, db + db2, dgc + dgc2, dgr + dgr2

        chunks = lambda a: a.reshape((nl, DN_CHUNK) + a.shape[1:])
        dq, dk, dv, db, dgc, dgr = jax.vmap(chunk_bwd)(
            chunks(t_ref[0]), chunks(q_ref[...]), chunks(k_ref[...]), chunks(v_ref[...]), chunks(b_ref[0][:, :1]),
            chunks(gc_ref[0][:, :1]),
            gr_ref[0], chunks(du_ref[0]), chunks(dw_ref[0]), chunks(dqk_ref[0]), chunks(dqd_ref[0]), chunks(dkd_ref[0]),
            degl_ref[0])
        dq_ref[0] = dq.reshape(rows_step, LANES)
        dk_ref[0] = dk.reshape(rows_step, LANES)
        dv_ref[0] = dv.reshape(rows_step, LANES)
        db_ref[0] = jnp.broadcast_to(db.reshape(rows_step, 1), (rows_step, LANES))
        dgc_ref[0] = jnp.broadcast_to(dgc.reshape(rows_step, 1), (rows_step, LANES))
        dgr_ref[0] = dgr

    per_dir = _sds((2, t, DN_WIDTH))
    return _maybe_carrying(
        carry, body, "dn_local_bwd", (N_HD, t // rows_step), [q, k, v, beta_c, gc_c, gc_r, t_inv, du, dw, dqk, dqd, dkd, degl],
        _qkv_specs(rows_step) + [col, col, row, half, wide, wide, half, wide, wide, egl], [dspec, dspec, dspec, col, col, row],
        [per_dir, per_dir, per_dir, _sds((N_HD, t, LANES)), _sds((N_HD, t, LANES)), _sds((N_HD, nc, 1, DN_CHUNK))])


REC_HEADS = 8
REC_GROUPS = N_HD // REC_HEADS
REC_FWD_GROUPS = DN_HEADS // REC_HEADS


def _rec_specs(time_block):
    nr = REC_ROWS // DN_CHUNK
    wide = pl.BlockSpec((REC_HEADS, REC_ROWS, LANES), lambda g, b: (g, time_block(g, b), 0))
    half = pl.BlockSpec((REC_HEADS, REC_ROWS, DN_CHUNK), lambda g, b: (g, time_block(g, b), 0))
    egl = pl.BlockSpec((REC_HEADS, nr, 1, LANES), lambda g, b: (g, time_block(g, b), 0, 0))
    state = pl.BlockSpec((REC_HEADS, nr, DN_HEAD_DIM, DN_HEAD_DIM), lambda g, b: (g, time_block(g, b), 0, 0))
    return wide, half, egl, state


def _rec_head_cols(g):
    return jnp.where(g < REC_FWD_GROUPS, g, g - REC_FWD_GROUPS)


def _dn_rec_fwd(u, w, qk, qd, kd, egl):
    t = u.shape[1]
    nb = t // REC_ROWS
    nr = REC_ROWS // DN_CHUNK
    nc = t // DN_CHUNK

    def time_block(g, b):
        return jnp.where(g < REC_FWD_GROUPS, b, nb - 1 - b)

    wide, half, egl_spec, state = _rec_specs(time_block)
    o_spec = pl.BlockSpec((1, REC_ROWS, REC_HEADS * LANES),
                          lambda g, b: (jnp.where(g < REC_FWD_GROUPS, 0, 1), time_block(g, b), _rec_head_cols(g)))

    def body(u_ref, w_ref, qk_ref, qd_ref, kd_ref, egl_ref, o_ref, vn_ref, s_ref, s_scr):
        fwd = pl.program_id(0) < REC_FWD_GROUPS

        @pl.when(pl.program_id(1) == 0)
        def _():
            s_scr[...] = jnp.zeros_like(s_scr)

        def run(order):
            heads = range(REC_HEADS)
            s = [s_scr[j] for j in heads]
            for ce in order:
                rows = slice(ce * DN_CHUNK, (ce + 1) * DN_CHUNK)
                vn = [u_ref[j, rows, :] - _bnn(w_ref[j, rows, :], s[j]) for j in heads]
                o = [_bnn(qd_ref[j, rows, :], s[j]) + _bnn(qk_ref[j, rows, :], vn[j]) for j in heads]
                nxt = [s[j] * egl_ref[j, ce] + _btn(kd_ref[j, rows, :], vn[j]) for j in heads]
                for j in heads:
                    s_ref[j, ce] = s[j]
                    vn_ref[j, rows, :] = vn[j].astype(_MXU)
                    o_ref[0, rows, j * LANES:(j + 1) * LANES] = o[j]
                s = nxt
            for j in heads:
                s_scr[j] = s[j]

        pl.when(fwd)(lambda: run(range(nr)))
        pl.when(jnp.logical_not(fwd))(lambda: run(range(nr - 1, -1, -1)))

    return pl.pallas_call(
        body, name="dn_rec_fwd", grid=(REC_GROUPS, nb), in_specs=[wide, wide, half, wide, wide, egl_spec],
        out_specs=[o_spec, wide, state],
        out_shape=[_sds((2, t, DN_WIDTH)), _sds((N_HD, t, LANES), _MXU), _sds((N_HD, nc, DN_HEAD_DIM, DN_HEAD_DIM))],
        scratch_shapes=[pltpu.VMEM((REC_HEADS, DN_HEAD_DIM, DN_HEAD_DIM), F32)],
        compiler_params=_params(("parallel", "arbitrary")),
    )(u, w, qk, qd, kd, egl)


def _dn_rec_bwd(w, qk, qd, kd, egl, vn, states, do):
    t = w.shape[1]
    nb = t // REC_ROWS
    nr = REC_ROWS // DN_CHUNK
    nc = t // DN_CHUNK

    def time_block(g, b):
        return jnp.where(g < REC_FWD_GROUPS, nb - 1 - b, b)

    wide, half, egl_spec, state = _rec_specs(time_block)
    do_spec = pl.BlockSpec((REC_ROWS, REC_HEADS * LANES), lambda g, b: (time_block(g, b), _rec_head_cols(g)))

    def body(w_ref, qk_ref, qd_ref, kd_ref, egl_ref, vn_ref, s_ref, do_ref,
             du_ref, dw_ref, dqk_ref, dqd_ref, dkd_ref, degl_ref, ds_scr):
        fwd = pl.program_id(0) < REC_FWD_GROUPS

        @pl.when(pl.program_id(1) == 0)
        def _():
            ds_scr[...] = jnp.zeros_like(ds_scr)

        def run(order):
            heads = range(REC_HEADS)
            ds = [ds_scr[j] for j in heads]
            for ce in order:
                rows = slice(ce * DN_CHUNK, (ce + 1) * DN_CHUNK)
                s = [s_ref[j, ce] for j in heads]
                do_c = [do_ref[rows, j * LANES:(j + 1) * LANES] for j in heads]
                vn_c = [vn_ref[j, rows, :] for j in heads]
                dvn = [_btn(qk_ref[j, rows, :], do_c[j]) + _bnn(kd_ref[j, rows, :], ds[j]) for j in heads]
                nxt = [ds[j] * egl_ref[j, ce] + _btn(qd_ref[j, rows, :], do_c[j]) - _btn(w_ref[j, rows, :], dvn[j])
                       for j in heads]
                for j in heads:
                    du_ref[j, rows, :] = dvn[j]
                    dw_ref[j, rows, :] = -_bnt(dvn[j], s[j])
                for j in heads:
                    dqk_ref[j, rows, :] = _bnt(do_c[j], vn_c[j])
                    dqd_ref[j, rows, :] = _bnt(do_c[j], s[j])
                for j in heads:
                    dkd_ref[j, rows, :] = _bnt(vn_c[j], ds[j])
                    degl_ref[j, ce] = jnp.sum(s[j] * ds[j], axis=0, keepdims=True)
                ds = nxt
            for j in heads:
                ds_scr[j] = ds[j]

        pl.when(fwd)(lambda: run(range(nr - 1, -1, -1)))
        pl.when(jnp.logical_not(fwd))(lambda: run(range(nr)))

    big = _sds((N_HD, t, LANES))
    return pl.pallas_call(
        body, name="dn_rec_bwd", grid=(REC_GROUPS, nb), in_specs=[wide, half, wide, wide, egl_spec, wide, state, do_spec],
        out_specs=[wide, wide, half, wide, wide, egl_spec],
        out_shape=[big, big, _sds((N_HD, t, DN_CHUNK)), big, big, _sds((N_HD, nc, 1, LANES))],
        scratch_shapes=[pltpu.VMEM((REC_HEADS, DN_HEAD_DIM, DN_HEAD_DIM), F32)],
        compiler_params=_params(("parallel", "arbitrary")),
    )(w, qk, qd, kd, egl, vn, states, do)


def _post_fn(of, ob, z, gain):
    o = of + ob
    return o * lax.rsqrt(jnp.mean(o * o, axis=-1, keepdims=True) + RMS_EPS) * gain * _silu(z)


def _post_specs():
    o_spec = [pl.BlockSpec((1, ROW_TILE, DN_WIDTH), functools.partial(lambda i, d: (d, i, 0), d=d)) for d in (0, 1)]
    return o_spec, _cols(DN_WIDTH, C_Z), _rows(DN_WIDTH), _full((1, LANES))


def _post_fwd(o2, proj, gain):
    t = proj.shape[0]
    o_spec, z_spec, wide, gain_spec = _post_specs()

    def body(of_ref, ob_ref, z_ref, g_ref, out_ref):
        for h in range(DN_HEADS):
            cols = slice(h * LANES, (h + 1) * LANES)
            out_ref[:, cols] = _post_fn(of_ref[0, :, cols], ob_ref[0, :, cols], z_ref[:, cols], g_ref[...]).astype(_MXU)

    return pl.pallas_call(
        body, name="post_fwd", grid=(t // ROW_TILE,), in_specs=o_spec + [z_spec, gain_spec], out_specs=wide,
        out_shape=_sds((t, DN_WIDTH), _MXU), compiler_params=_params(("parallel",)),
    )(o2, o2, proj, gain)


def _post_bwd(o2, proj, gain, dout):
    t = proj.shape[0]
    o_spec, z_spec, wide, gain_spec = _post_specs()

    def body(of_ref, ob_ref, z_ref, g_ref, d_ref, do_ref, dz_ref, dg_ref):
        dg_sum = jnp.zeros((1, LANES), F32)
        for h in range(DN_HEADS):
            cols = slice(h * LANES, (h + 1) * LANES)
            _, vjp = jax.vjp(_post_fn, of_ref[0, :, cols], ob_ref[0, :, cols], z_ref[:, cols], g_ref[...])
            do, _, dz, dg = vjp(d_ref[:, cols])
            do_ref[:, cols] = do
            dz_ref[:, cols] = dz.astype(_MXU)
            dg_sum = dg_sum + dg
        _accumulate(dg_ref, dg_sum, pl.program_id(0))

    return pl.pallas_call(
        body, name="post_bwd", grid=(t // ROW_TILE,), in_specs=o_spec + [z_spec, gain_spec, wide],
        out_specs=[wide, wide, gain_spec], out_shape=[_sds((t, DN_WIDTH)), _sds((t, DN_WIDTH), _MXU), _sds((1, LANES))],
        compiler_params=_params(("arbitrary",)),
    )(o2, o2, proj, gain, dout)


def _rope(x, cos, sin):
    lane = lax.broadcasted_iota(jnp.int32, x.shape, 1)
    first = (lane & (SW_HEAD_DIM - 1)) < SW_HEAD_DIM // 2
    rot = jnp.where(first, -pltpu.roll(x, LANES - SW_HEAD_DIM // 2, 1), pltpu.roll(x, SW_HEAD_DIM // 2, 1))
    return x * cos + rot * sin


def _rope_apply(q, k, q_cols, k_cols, cos, sin, name, dtype):
    t = cos.shape[0]
    qw, kw = SW_HEADS * SW_HEAD_DIM, SW_KV_WIDTH

    def body(q_ref, k_ref, c_ref, s_ref, qo_ref, ko_ref):
        c, s = c_ref[...], s_ref[...]
        for j in range(qw // LANES):
            cols = slice(j * LANES, (j + 1) * LANES)
            qo_ref[:, cols] = _rope(q_ref[:, cols], c, s).astype(dtype)
        for j in range(kw // LANES):
            cols = slice(j * LANES, (j + 1) * LANES)
            ko_ref[:, cols] = _rope(k_ref[:, cols], c, s).astype(dtype)

    return pl.pallas_call(
        body, name=name, grid=(t // ROW_TILE,), in_specs=[_cols(qw, q_cols), _cols(kw, k_cols), _rows(LANES), _rows(LANES)],
        out_specs=[_rows(qw), _rows(kw)], out_shape=[_sds((t, qw), dtype), _sds((t, kw), dtype)],
        compiler_params=_params(("parallel",)),
    )(q, k, cos, sin)


def _attn_core(qs, kb, vb, sink, mask):
    s = _bnt(qs, kb) * (SW_HEAD_DIM ** -0.5)
    s = jnp.where(mask, s, -1e30)
    m = lax.stop_gradient(jnp.maximum(jnp.max(s, axis=1, keepdims=True), sink))
    e = jnp.exp(s - m)
    den = jnp.sum(e, axis=1, keepdims=True) + jnp.exp(sink - m)
    return _bnn(e / den, vb)


def _band_mask(n, nb):
    rows = SW_GROUP * SW_BLOCK
    i = lax.broadcasted_iota(jnp.int32, (rows, 3 * SW_BLOCK), 0) & (SW_BLOCK - 1)
    j = lax.broadcasted_iota(jnp.int32, (rows, 3 * SW_BLOCK), 1)
    near = (j - i >= 0) & (j - i <= 2 * SW_BLOCK)
    lo = jnp.where(n == 0, SW_BLOCK, 0)
    hi = jnp.where(n == nb - 1, 2 * SW_BLOCK, 3 * SW_BLOCK)
    return near & (j >= lo) & (j < hi)


def _band_specs(nb, v_cols):
    def spec(width, base, shift):
        return pl.BlockSpec((SW_BLOCK, width), lambda n: (jnp.clip(n + shift, 0, nb - 1), base // width))
    k_specs = [spec(SW_KV_WIDTH, 0, s) for s in (-1, 0, 1)]
    v_specs = [spec(SW_KV_WIDTH, v_cols, s) for s in (-1, 0, 1)]
    return k_specs, v_specs


def _head_cols(kv, g):
    h = kv * SW_GROUP + g
    return slice(h * SW_HEAD_DIM, (h + 1) * SW_HEAD_DIM)


def _kv_batches(q_ref, kb, vb, s_ref):
    kvs = range(SW_KV_HEADS)
    cols = lambda kv: slice(kv * SW_HEAD_DIM, (kv + 1) * SW_HEAD_DIM)
    qs = jnp.stack([jnp.concatenate([q_ref[:, _head_cols(kv, g)] for g in range(SW_GROUP)], axis=0) for kv in kvs])
    sinks = jnp.stack([jnp.concatenate([jnp.broadcast_to(s_ref[kv * SW_GROUP + g], (SW_BLOCK, 1)) for g in range(SW_GROUP)],
                                       axis=0) for kv in kvs])
    return qs, jnp.stack([kb[:, cols(kv)] for kv in kvs]), jnp.stack([vb[:, cols(kv)] for kv in kvs]), sinks


def _attn_fwd(qr, kr, proj, sinks):
    t = qr.shape[0]
    nb = t // SW_BLOCK
    qw = SW_HEADS * SW_HEAD_DIM
    k_specs, v_specs = _band_specs(nb, C_VSW)
    q_spec = pl.BlockSpec((SW_BLOCK, qw), lambda n: (n, 0))

    def body(q_ref, k0, k1, k2, v0, v1, v2, s_ref, o_ref):
        mask = _band_mask(pl.program_id(0), nb)
        kb = jnp.concatenate([k0[...], k1[...], k2[...]], axis=0)
        vb = jnp.concatenate([v0[...], v1[...], v2[...]], axis=0)
        qs, kbs, vbs, sinks_ = _kv_batches(q_ref, kb, vb, s_ref)
        o = jax.vmap(functools.partial(_attn_core, mask=mask))(qs, kbs, vbs, sinks_)
        for kv in range(SW_KV_HEADS):
            for g in range(SW_GROUP):
                o_ref[:, _head_cols(kv, g)] = o[kv, g * SW_BLOCK:(g + 1) * SW_BLOCK, :].astype(_MXU)

    return pl.pallas_call(
        body, name="attn_fwd", grid=(nb,), in_specs=[q_spec] + k_specs + v_specs + [_full((SW_HEADS, 1, 1))],
        out_specs=q_spec, out_shape=_sds((t, qw), _MXU), compiler_params=_params(("parallel",)),
    )(qr, kr, kr, kr, proj, proj, proj, sinks)


def _attn_bwd(qr, kr, proj, sinks, do):
    t = qr.shape[0]
    nb = t // SW_BLOCK
    qw = SW_HEADS * SW_HEAD_DIM
    k_specs, v_specs = _band_specs(nb, C_VSW)
    q_spec = pl.BlockSpec((SW_BLOCK, qw), lambda n: (n, 0))
    part = pl.BlockSpec((1, 3 * SW_BLOCK, SW_KV_WIDTH), lambda n: (n, 0, 0))

    def body(q_ref, k0, k1, k2, v0, v1, v2, s_ref, do_ref, dq_ref, dk_ref, dv_ref, ds_ref):
        mask = _band_mask(pl.program_id(0), nb)
        kb = jnp.concatenate([k0[...], k1[...], k2[...]], axis=0).astype(F32)
        vb = jnp.concatenate([v0[...], v1[...], v2[...]], axis=0)

        @pl.when(pl.program_id(0) == 0)
        def _():
            ds_ref[...] = jnp.zeros_like(ds_ref)

        qs, kbs, vbs, sinks_ = _kv_batches(q_ref, kb, vb, s_ref)
        dos = jnp.stack([jnp.concatenate([do_ref[:, _head_cols(kv, g)] for g in range(SW_GROUP)], axis=0)
                         for kv in range(SW_KV_HEADS)])

        def head_bwd(q_, k_, v_, sink_, do_):
            _, vjp = jax.vjp(functools.partial(_attn_core, mask=mask), q_, k_, v_, sink_)
            return vjp(do_)

        dqs, dkb, dvb, dsink = jax.vmap(head_bwd)(qs.astype(F32), kbs, vbs, sinks_, dos)
        for kv in range(SW_KV_HEADS):
            kvc = slice(kv * SW_HEAD_DIM, (kv + 1) * SW_HEAD_DIM)
            dk_ref[0, :, kvc] = dkb[kv]
            dv_ref[0, :, kvc] = dvb[kv]
            for g in range(SW_GROUP):
                rows = slice(g * SW_BLOCK, (g + 1) * SW_BLOCK)
                dq_ref[:, _head_cols(kv, g)] = dqs[kv, rows, :]
                ds_ref[kv * SW_GROUP + g] += jnp.sum(dsink[kv, rows, :], axis=0, keepdims=True)

    parts = _sds((nb, 3 * SW_BLOCK, SW_KV_WIDTH))
    return pl.pallas_call(
        body, name="attn_bwd", grid=(nb,), in_specs=[q_spec] + k_specs + v_specs + [_full((SW_HEADS, 1, 1)), q_spec],
        out_specs=[q_spec, part, part, _full((SW_HEADS, 1, 1))], out_shape=[_sds((t, qw)), parts, parts, _sds((SW_HEADS, 1, 1))],
        compiler_params=_params(("arbitrary",)),
    )(qr, kr, kr, kr, proj, proj, proj, sinks, do)


def _band_sum(parts, name, dtype):
    nb = parts.shape[0]
    w = parts.shape[2]

    def spec(shift, slot):
        return pl.BlockSpec((1, SW_BLOCK, w), lambda m: (jnp.clip(m + shift, 0, nb - 1), slot, 0))

    def body(prev_ref, own_ref, next_ref, o_ref):
        m = pl.program_id(0)
        total = own_ref[0] + jnp.where(m > 0, prev_ref[0], 0.0) + jnp.where(m < nb - 1, next_ref[0], 0.0)
        o_ref[...] = total.astype(dtype)

    return pl.pallas_call(
        body, name=name, grid=(nb,), in_specs=[spec(-1, 2), spec(0, 1), spec(1, 0)],
        out_specs=pl.BlockSpec((SW_BLOCK, w), lambda m: (m, 0)), out_shape=_sds((nb * SW_BLOCK, w), dtype),
        compiler_params=_params(("parallel",)),
    )(parts, parts, parts)


def _gate_rows(gbo):
    t = gbo.shape[0]
    return gbo[:, N_HD:2 * N_HD].T.reshape(N_HD, t // DN_CHUNK, 1, DN_CHUNK)


def _gate_rows_t(dgc_r):
    t = dgc_r.shape[1] * DN_CHUNK
    return jnp.pad(dgc_r.reshape(N_HD, t).T, ((0, 0), (N_HD, LANES - 2 * N_HD)))


def _layer_fwd(x, xb, w, cos, sin, carry=None):
    proj = _mm(xb, w["wm"], "nn", name="proj")
    ba = _mm(xb, w["wba"], "nn", name="proj_gates")
    qn, kn, vv = [_prep_fwd(proj, w["conv"], kind) for kind in range(3)]
    gbo, beta_c, gc_c = _gb_fwd(ba, w["alog"], w["dtb"])
    gc_r = _gate_rows(gbo)
    u, wk, qk, qd, kd, egl, tinv, *carried = _dn_local_fwd(qn, kn, vv, beta_c, gc_c, gc_r, carry=carry)
    o2, vn, states = _dn_rec_fwd(u, wk, qk, qd, kd, egl)
    o_dn = _post_fwd(o2, proj, w["dnw"])
    qr, kr = _rope_apply(proj, proj, C_QSW, C_KSW, cos, sin, "rope_fwd", _MXU)
    o_sw = _attn_fwd(qr, kr, proj, w["sinks"])
    ya = _mm(o_dn, w["wa"], "nn", name="branch_a")
    yb = _mm(o_sw, w["wb"], "nn", name="branch_b")
    merged = _merge_fwd(proj, ya, yb)
    mix = _mm(merged, w["wo"], "nn", name="mix_out")
    x1, x1b = _ln_fwd(x, mix, w["ln1g"], w["ln1b"], "ln1_fwd")
    gu = _mm(x1b, w["wgu"], "nn", name="ffn_up")
    h = _swiglu_fwd(gu)
    f = _mm(h, w["wd"], "nn", name="ffn_down")
    x2, x2b = _ln_fwd(x1, f, w["ln2g"], w["ln2b"], "ln2_fwd")
    res = dict(x=x, xb=xb, proj=proj, ba=ba, qn=qn, kn=kn, vv=vv, beta_c=beta_c, gc_c=gc_c, gc_r=gc_r, wk=wk, qk=qk, qd=qd, kd=kd, egl=egl, tinv=tinv, vn=vn,
               states=states, o2=o2, o_dn=o_dn, qr=qr, kr=kr, o_sw=o_sw, ya=ya, yb=yb, merged=merged, mix=mix, x1=x1, x1b=x1b,
               gu=gu, h=h, f=f)
    return x2, x2b, res, (carried or None)


def _layer_bwd(dx2, w, r, cos, sin, carry=None):
    dx1, df, dln2g, dln2b = _ln_bwd(r["x1"], r["f"], w["ln2g"], w["ln2b"], dx2, "ln2_bwd")
    dh = _mm(df, w["wd"], "nt", name="d_ffn_hidden")
    dwd = _mm(r["h"], df, "tn", name="dw_ffn_down", out_dtype=_MXU)
    dgu = _swiglu_bwd(r["gu"], dh)
    dwgu = _mm(r["x1b"], dgu, "tn", name="dw_ffn_up", out_dtype=_MXU)
    dx1 = _mm(dgu, w["wgu"], "nt", name="dx_ffn", add=dx1)
    dx, dmix, dln1g, dln1b = _ln_bwd(r["x"], r["mix"], w["ln1g"], w["ln1b"], dx1, "ln1_bwd")
    dmerged = _mm(dmix, w["wo"], "nt", name="d_merged")
    dwo = _mm(r["merged"], dmix, "tn", name="dw_mix_out", out_dtype=_MXU)
    dga, dgb, dya, dyb = _merge_bwd(r["proj"], r["ya"], r["yb"], dmerged)
    dwa = _mm(r["o_dn"], dya, "tn", name="dw_branch_a", out_dtype=_MXU)
    do_dn = _mm(dya, w["wa"], "nt", name="d_branch_a")
    dwb = _mm(r["o_sw"], dyb, "tn", name="dw_branch_b", out_dtype=_MXU)
    do_sw = _mm(dyb, w["wb"], "nt", name="d_branch_b")
    do, dz, ddnw = _post_bwd(r["o2"], r["proj"], w["dnw"], do_dn)
    du, dwk, dqk, dqd, dkd, degl = _dn_rec_bwd(r["wk"], r["qk"], r["qd"], r["kd"], r["egl"], r["vn"], r["states"], do)
    dq3, dk3, dv3, dbeta_c, dgc_c, dgc_r, *carried = _dn_local_bwd(r["qn"], r["kn"], r["vv"], r["beta_c"], r["gc_c"], r["gc_r"],
                                                                   r["tinv"], du, dwk, dqk, dqd, dkd, degl, carry=carry)
    dqkv, dconv = zip(*[_prep_bwd(r["proj"], w["conv"], d2, kind) for kind, d2 in enumerate((dq3, dk3, dv3))])
    dconv = jnp.concatenate(dconv, axis=1)
    dba, dalog, ddtb = _gb_bwd(r["ba"], w["alog"], w["dtb"], dbeta_c, dgc_c, _gate_rows_t(dgc_r))
    dqr, dkparts, dvparts, dsinks = _attn_bwd(r["qr"], r["kr"], r["proj"], w["sinks"], do_sw)
    dkr = _band_sum(dkparts, "attn_dk_sum", F32)
    dv = _band_sum(dvparts, "attn_dv_sum", _MXU)
    dq_sw, dk_sw = _rope_apply(dqr, dkr, 0, 0, cos, -sin, "rope_bwd", _MXU)
    dproj = jnp.concatenate([*dqkv, dz, dq_sw, dga, dgb, dk_sw, dv], axis=1)
    dwm = _mm(r["xb"], dproj, "tn", name="dw_proj", out_dtype=_MXU)
    dwba = _mm(r["xb"], dba, "tn", name="dw_proj_gates", out_dtype=_MXU)
    dx = _mm(dproj, w["wm"], "nt", name="dx_proj", add=dx)
    dx = _mm(dba, w["wba"], "nt", name="dx_proj_gates", add=dx)
    grads = dict(wm=dwm, wba=dwba, conv=dconv, alog=dalog, dtb=ddtb, dnw=ddnw, sinks=dsinks, wa=dwa, wb=dwb, wo=dwo,
                 ln1g=dln1g, ln1b=dln1b, wgu=dwgu, wd=dwd, ln2g=dln2g, ln2b=dln2b)
    return dx, grads, (carried or None)


def _rope_tables(t):
    half = SW_HEAD_DIM // 2
    inv_freq = ROPE_THETA ** (-jnp.arange(half, dtype=F32) / half)
    ang = jnp.arange(t, dtype=F32)[:, None] * inv_freq[None, :]
    return jnp.tile(jnp.cos(ang), (1, LANES // half)), jnp.tile(jnp.sin(ang), (1, LANES // half))


def _trunk(x, target, n_layers, layer_weights, fwd_carry, grads_done):
    cos, sin = _rope_tables(x.shape[0])
    xb = x.astype(_MXU)
    saved, weights, carried = [], [], None
    for i in range(n_layers):
        w = layer_weights(i, carried)
        x, xb, res, carried = _layer_fwd(x, xb, w, cos, sin, carry=fwd_carry(i))
        saved.append(res)
        weights.append(w)
    sq, dx = _loss_head(x, target)
    carry = None
    for i in reversed(range(n_layers)):
        dx, grads, carried = _layer_bwd(dx, weights[i], saved[i], cos, sin, carry=carry)
        carry = grads_done(i, grads, carried)
    return sq, dx, carry


N_CHIPS = 4


def _mesh_pos():
    return lax.axis_index("x"), lax.axis_index("y"), lax.axis_index("c")


def _other_chips(x, y):
    return [(1 - x, y), (x, 1 - y), (1 - x, 1 - y)]


def _remote_copy(src, dst, sems, k, to):
    send_sems, recv_sems, base = sems
    return pltpu.make_async_remote_copy(src_ref=src, dst_ref=dst, send_sem=send_sems.at[base + k],
                                        recv_sem=recv_sems.at[base + k], device_id=to, device_id_type=pl.DeviceIdType.MESH)


def _exchange_sems(n_arrays, per_array):
    return [pltpu.SemaphoreType.DMA((n_arrays * per_array,)), pltpu.SemaphoreType.DMA((n_arrays * per_array,)),
            pltpu.SemaphoreType.DMA((n_arrays,))]


def _comm_call(body, name, out_shapes, per_array, operands):
    n = len(operands)
    hbm = pl.BlockSpec(memory_space=pl.ANY)

    def flat_body(*refs):
        body(refs[:n], refs[n:2 * n], *refs[2 * n:])

    return pl.pallas_call(
        flat_body, name=name, in_specs=[hbm] * n, out_specs=[hbm] * n, out_shape=list(out_shapes),
        scratch_shapes=_exchange_sems(n, per_array), compiler_params=pltpu.CompilerParams(has_side_effects=True),
    )(*operands)


class _Gather:
    n_sems = N_DEV - 1

    @staticmethod
    def out_shape(block):
        return _sds((N_DEV,) + block.shape, block.dtype)

    @staticmethod
    def _own(x_ref, o_ref, sems, local_sem):
        x, y, c = _mesh_pos()
        mine = o_ref.at[4 * x + 2 * y + c]
        first = [_remote_copy(x_ref, mine, sems, 0, (x, y, 1 - c))]
        first += [_remote_copy(x_ref, mine, sems, 1 + j, (*chip, c)) for j, chip in enumerate(_other_chips(x, y))]
        return pltpu.make_async_copy(x_ref, mine, local_sem), first

    @classmethod
    def start(cls, x_ref, o_ref, sems, local_sem):
        mine, first = cls._own(x_ref, o_ref, sems, local_sem)
        mine.start()
        for cp in first:
            cp.start()

    @classmethod
    def finish(cls, x_ref, o_ref, sems, local_sem):
        x, y, c = _mesh_pos()
        sibling = (x, y, 1 - c)
        chips = _other_chips(x, y)
        slot = lambda px, py, pc: o_ref.at[4 * px + 2 * py + pc]
        mine, first = cls._own(x_ref, o_ref, sems, local_sem)
        passed = [_remote_copy(slot(*chip, c), slot(*chip, c), sems, 4 + j, sibling) for j, chip in enumerate(chips)]
        for j, chip in enumerate(chips):
            _remote_copy(x_ref, slot(*chip, c), sems, 1 + j, sibling).wait_recv()
            passed[j].start()
        _remote_copy(x_ref, slot(x, y, 1 - c), sems, 0, sibling).wait_recv()
        for j, chip in enumerate(chips):
            _remote_copy(x_ref, slot(*chip, 1 - c), sems, 4 + j, sibling).wait_recv()
        for cp in first + passed:
            cp.wait_send()
        mine.wait()


class _ChipExchange:
    n_sems = N_CHIPS - 1

    @staticmethod
    def out_shape(parts):
        return _sds(parts.shape, parts.dtype)

    @staticmethod
    def _own(x_ref, o_ref, sems, local_sem):
        x, y, c = _mesh_pos()
        me = 2 * x + y
        sent = [_remote_copy(x_ref.at[2 * cx + cy], o_ref.at[me], sems, j, (cx, cy, c))
                for j, (cx, cy) in enumerate(_other_chips(x, y))]
        return pltpu.make_async_copy(x_ref.at[me], o_ref.at[me], local_sem), sent

    @classmethod
    def start(cls, x_ref, o_ref, sems, local_sem):
        mine, sent = cls._own(x_ref, o_ref, sems, local_sem)
        mine.start()
        for cp in sent:
            cp.start()

    @classmethod
    def finish(cls, x_ref, o_ref, sems, local_sem):
        x, y, c = _mesh_pos()
        mine, sent = cls._own(x_ref, o_ref, sems, local_sem)
        for j, (cx, cy) in enumerate(_other_chips(x, y)):
            _remote_copy(x_ref.at[2 * x + y], o_ref.at[2 * cx + cy], sems, j, (cx, cy, c)).wait_recv()
        for cp in sent:
            cp.wait_send()
        mine.wait()


def _run_exchange(kind, phase, x_refs, o_refs, send_sems, recv_sems, local_sems):
    for i, (x_ref, o_ref) in enumerate(zip(x_refs, o_refs)):
        getattr(kind, phase)(x_ref, o_ref, (send_sems, recv_sems, i * kind.n_sems), local_sems.at[i])


def _exchange_alone(kind, operands, name):
    def body(x_refs, o_refs, *sems):
        _run_exchange(kind, "start", x_refs, o_refs, *sems)
        _run_exchange(kind, "finish", x_refs, o_refs, *sems)

    return _comm_call(body, name, [kind.out_shape(a) for a in operands], kind.n_sems, operands)


def _all_gather(block, name):
    return _exchange_alone(_Gather, [block], name)[0]


def _carried(kind, operands, body, n_in, n_out, grid):
    hbm = pl.BlockSpec(memory_space=pl.ANY)
    n_x = len(operands)

    def wrapped(*refs):
        ins, x_refs = refs[:n_in], refs[n_in:n_in + n_x]
        outs = refs[n_in + n_x:n_in + n_x + n_out]
        o_refs = refs[n_in + n_x + n_out:n_in + 2 * n_x + n_out]
        sems = refs[n_in + 2 * n_x + n_out:n_in + 2 * n_x + n_out + 3]
        rest = refs[n_in + 2 * n_x + n_out + 3:]
        first, last = None, None
        for axis, size in enumerate(grid):
            at0, at1 = pl.program_id(axis) == 0, pl.program_id(axis) == size - 1
            first = at0 if first is None else first & at0
            last = at1 if last is None else last & at1
        pl.when(first)(lambda: _run_exchange(kind, "start", x_refs, o_refs, *sems))
        body(*ins, *outs, *rest)
        pl.when(last)(lambda: _run_exchange(kind, "finish", x_refs, o_refs, *sems))

    return wrapped, [hbm] * n_x, [hbm] * n_x, [kind.out_shape(a) for a in operands], _exchange_sems(n_x, kind.n_sems)


def _sibling_swap(parts, name):
    def body(x_refs, o_refs, send_sems, recv_sems, local_sems):
        x, y, c = _mesh_pos()
        copies = [_remote_copy(x_ref.at[2 * q + (1 - c)], o_ref.at[q], (send_sems, recv_sems, i * N_CHIPS), q, (x, y, 1 - c))
                  for i, (x_ref, o_ref) in enumerate(zip(x_refs, o_refs)) for q in range(N_CHIPS)]
        for cp in copies:
            cp.start()
        for cp in copies:
            cp.wait()

    return _comm_call(body, name, [_sds((N_CHIPS,) + p.shape[1:], p.dtype) for p in parts], N_CHIPS, parts)


def _pair_sum(a, b, name):
    n, rows, cols = a.shape
    tr = rows if rows <= 512 else _row_tile(rows, 2048)
    blk = pl.BlockSpec((1, tr, cols), lambda q, i: (q, i, 0))

    def body(a_ref, b_ref, o_ref):
        o_ref[...] = (a_ref[...].astype(F32) + b_ref[...].astype(F32)).astype(o_ref.dtype)

    return pl.pallas_call(
        body, name=name, grid=(n, rows // tr), in_specs=[blk, blk], out_specs=blk, out_shape=_sds(a.shape, a.dtype),
        compiler_params=_params(("parallel", "parallel")),
    )(a, b)


def _chip_sums(parts, name):
    c = lax.axis_index("c")
    from_sibling = _sibling_swap(parts, "swap_" + name)
    own = [lax.dynamic_index_in_dim(p.reshape((N_CHIPS, 2) + p.shape[1:]), c, axis=1, keepdims=False) for p in parts]
    return [_pair_sum(a, b, f"pair_sum_{name}_{k}") for k, (a, b) in enumerate(zip(own, from_sibling))]


def _reduce_to_owner(parts, name):
    return _exchange_alone(_ChipExchange, _chip_sums([parts], name), "exchange_" + name)[0]


def _sum_adamw(parts, w, m, v, name):
    rows, cols = w.shape
    n_parts = parts.shape[0]
    tr = rows if rows <= 512 else _row_tile(rows)
    blk = pl.BlockSpec((tr, cols), lambda i: (i, 0))

    def body(p_ref, w_ref, m_ref, v_ref, g_ref, d_ref, nm_ref, nv_ref):
        g = p_ref[0].astype(F32)
        for i in range(1, n_parts):
            g = g + p_ref[i].astype(F32)
        nm = ADAM_B1 * m_ref[...] + (1.0 - ADAM_B1) * g
        nv = ADAM_B2 * v_ref[...] + (1.0 - ADAM_B2) * jnp.square(g)
        m_hat = nm / (1.0 - ADAM_B1 ** ADAM_STEP)
        v_hat = nv / (1.0 - ADAM_B2 ** ADAM_STEP)
        g_ref[...] = g
        d_ref[...] = -ADAM_LR * (m_hat / (jnp.sqrt(v_hat) + ADAM_EPS) + ADAM_WD * w_ref[...])
        nm_ref[...] = nm
        nv_ref[...] = nv

    return pl.pallas_call(
        body, name=name, grid=(rows // tr,), in_specs=[pl.BlockSpec((n_parts, tr, cols), lambda i: (0, i, 0)), blk, blk, blk],
        out_specs=[blk] * 4, out_shape=[_sds((rows, cols))] * 4, compiler_params=_params(("parallel",)),
    )(parts, w, m, v)


def _row_tile(rows, pref=256):
    t = pref
    while t >= 8:
        if rows % t == 0:
            return t
        t //= 2
    return rows


def _gathered_cols(g):
    g = jnp.moveaxis(g, 0, -2)
    return g.reshape(g.shape[:-2] + (g.shape[-2] * g.shape[-1],))


def _gathered_rows(g):
    g = jnp.moveaxis(g, 0, -3)
    return g.reshape(g.shape[:-3] + (g.shape[-3] * g.shape[-2], g.shape[-1]))


def _col_parts(full):
    c = full.shape[-1]
    return jnp.moveaxis(full.reshape(full.shape[:-1] + (N_DEV, c // N_DEV)), -2, 0)


def _row_parts(full):
    rows, c = full.shape[-2:]
    return jnp.moveaxis(full.reshape(full.shape[:-2] + (N_DEV, rows // N_DEV, c)), -3, 0)


def _w_in_split(w_in):
    s = lambda a, n: w_in[..., a:a + n]
    main = jnp.concatenate([s(R_QKV, 3072), s(R_Z, 1024), s(R_QSW, 1024), s(R_G, 2048), s(R_KSW, 256), s(R_VSW, 256)], axis=-1)
    gates = jnp.pad(s(R_BA, 2 * N_HD), [(0, 0)] * (w_in.ndim - 1) + [(0, LANES - 2 * N_HD)])
    return main, gates


def _w_in_join(dmain, dgates):
    s = lambda a, n: dmain[..., a:a + n]
    return jnp.concatenate([s(C_QKV, 3072), s(C_Z, 1024), dgates[..., :2 * N_HD], s(C_QSW, 1024), s(C_KSW, 256), s(C_VSW, 256),
                            s(C_GA, 2048)], axis=-1)


def _lane_row(a, offset):
    l, n = a.shape
    return jnp.pad(a, ((0, 0), (offset, LANES - offset - n)))[:, None, :]


def kernel(x, w_in, conv_w, a_log, dt_bias, dn_norm_w, sinks, w_branch_a, w_branch_b, w_out, ln1_g, ln1_b, w_gate_up, w_down, ln2_g, ln2_b, loss_target, m_w_in, m_conv_w, m_a_log, m_dt_bias, m_dn_norm_w, m_sinks, m_w_branch_a, m_w_branch_b, m_w_out, m_ln1_g, m_ln1_b, m_w_gate_up, m_w_down, m_ln2_g, m_ln2_b, v_w_in, v_conv_w, v_a_log, v_dt_bias, v_dn_norm_w, v_sinks, v_w_branch_a, v_w_branch_b, v_w_out, v_ln1_g, v_ln1_b, v_w_gate_up, v_w_down, v_ln2_g, v_ln2_b):
    l = DEPTH
    bf = lambda a: a.astype(_MXU)
    shards = [bf(w_in), bf(w_gate_up), bf(w_branch_a), bf(w_branch_b), bf(w_out), bf(w_down)]
    first = _exchange_alone(_Gather, [s[0] for s in shards], "gather_layer_0")
    conv_full = _gathered_cols(_all_gather(conv_w, "gather_conv_w"))
    row = lambda a: a[:, None, :]
    small = dict(
        conv=jnp.pad(conv_full, ((0, 0), (0, 8 - DN_CONV), (0, 0))), alog=_lane_row(a_log.reshape(l, N_HD), N_HD),
        dtb=_lane_row(dt_bias.reshape(l, N_HD), N_HD), dnw=row(dn_norm_w), sinks=sinks.reshape(l, SW_HEADS, 1, 1),
        ln1g=row(ln1_g), ln1b=row(ln1_b), ln2g=row(ln2_g), ln2b=row(ln2_b))

    def layer_weights(i, carried):
        s_in, s_gu, s_a, s_b, s_o, s_d = first if i == 0 else carried
        wm, wba = _w_in_split(_gathered_cols(s_in))
        return dict(wm=wm, wba=wba, wgu=_gathered_cols(s_gu), wa=_gathered_rows(s_a), wb=_gathered_rows(s_b),
                    wo=_gathered_rows(s_o), wd=_gathered_rows(s_d), **{k: a[i] for k, a in small.items()})

    def fwd_carry(i):
        return (_Gather, [s[i + 1] for s in shards]) if i + 1 < l else None

    layer_grads, received, waiting = [None] * l, [None] * l, []

    def grads_done(i, g_i, carried):
        if waiting:
            received[waiting.pop()] = carried
        layer_grads[i] = g_i
        parts = [_col_parts(_w_in_join(g_i["wm"], g_i["wba"])), _col_parts(g_i["wgu"]), _row_parts(g_i["wa"]),
                 _row_parts(g_i["wb"]), _row_parts(g_i["wo"]), _row_parts(g_i["wd"])]
        waiting.append(i)
        return (_ChipExchange, _chip_sums(parts, f"layer_{i}"))

    sq, dx, last = _trunk(x[0], loss_target[0], l, layer_weights, fwd_carry, grads_done)
    received[waiting.pop()] = _exchange_alone(last[0], last[1], "exchange_layer_0")
    loss = lax.psum(0.5 * sq[0, 0] / D_MODEL, ("x", "y", "c"))
    g = {k: jnp.stack([gi[k] for gi in layer_grads]) for k in small}

    def adamw(parts, w, m, v, name):
        rows = w.shape[0] * w.shape[1]
        flat = lambda a: a.reshape(rows, a.shape[-1])
        outs = _sum_adamw(parts.reshape(parts.shape[0], rows, w.shape[-1]), flat(w), flat(m), flat(v), "adamw_" + name)
        return [o.reshape(w.shape) for o in outs]

    got = [jnp.stack(per_layer, axis=1) for per_layer in zip(*received)]
    dconv = _reduce_to_owner(_col_parts(g["conv"][:, :DN_CONV, :]).reshape(N_DEV, l * DN_CONV, -1), "conv_w")
    results = {
        "w_in": adamw(got[0], w_in, m_w_in, v_w_in, "w_in"),
        "conv_w": adamw(dconv.reshape(N_CHIPS, l, DN_CONV, -1), conv_w, m_conv_w, v_conv_w, "conv_w"),
        "w_branch_a": adamw(got[2], w_branch_a, m_w_branch_a, v_w_branch_a, "w_branch_a"),
        "w_branch_b": adamw(got[3], w_branch_b, m_w_branch_b, v_w_branch_b, "w_branch_b"),
        "w_out": adamw(got[4], w_out, m_w_out, v_w_out, "w_out"),
        "w_gate_up": adamw(got[1], w_gate_up, m_w_gate_up, v_w_gate_up, "w_gate_up"),
        "w_down": adamw(got[5], w_down, m_w_down, v_w_down, "w_down"),
    }

    small_w = {"a_log": a_log.reshape(l, N_HD), "dt_bias": dt_bias.reshape(l, N_HD), "dn_norm_w": dn_norm_w, "sinks": sinks,
               "ln1_g": ln1_g, "ln1_b": ln1_b, "ln2_g": ln2_g, "ln2_b": ln2_b}
    small_m = {"a_log": m_a_log, "dt_bias": m_dt_bias, "dn_norm_w": m_dn_norm_w, "sinks": m_sinks, "ln1_g": m_ln1_g,
               "ln1_b": m_ln1_b, "ln2_g": m_ln2_g, "ln2_b": m_ln2_b}
    small_v = {"a_log": v_a_log, "dt_bias": v_dt_bias, "dn_norm_w": v_dn_norm_w, "sinks": v_sinks, "ln1_g": v_ln1_g,
               "ln1_b": v_ln1_b, "ln2_g": v_ln2_g, "ln2_b": v_ln2_b}
    small_g = {"a_log": g["alog"][:, 0, N_HD:2 * N_HD], "dt_bias": g["dtb"][:, 0, N_HD:2 * N_HD], "dn_norm_w": g["dnw"][:, 0, :],
               "sinks": g["sinks"].reshape(l, SW_HEADS), "ln1_g": g["ln1g"][:, 0, :], "ln1_b": g["ln1b"][:, 0, :],
               "ln2_g": g["ln2g"][:, 0, :], "ln2_b": g["ln2b"][:, 0, :]}
    names = list(small_w)
    cat = lambda d: jnp.concatenate([d[n].reshape(l, -1) for n in names], axis=1)
    widths = [small_w[n].shape[1] for n in names]
    total = sum(widths)
    padded = -(-total // LANES) * LANES
    pad = lambda a: jnp.pad(a, ((0, 8 - l), (0, padded - total)))
    got = _all_gather(pad(cat(small_g)), "gather_small_grads")
    outs = _sum_adamw(got, pad(cat(small_w)), pad(cat({n: small_m[n].reshape(l, -1) for n in names})),
                      pad(cat({n: small_v[n].reshape(l, -1) for n in names})), "adamw_small")
    off = 0
    for n, wd_ in zip(names, widths):
        shape = {"a_log": a_log.shape, "dt_bias": dt_bias.shape}.get(n, small_w[n].shape)
        results[n] = [o[:l, off:off + wd_].reshape(shape) for o in outs]
        off += wd_

    order = ["w_in", "conv_w", "a_log", "dt_bias", "dn_norm_w", "sinks", "w_branch_a", "w_branch_b", "w_out", "ln1_g", "ln1_b",
             "w_gate_up", "w_down", "ln2_g", "ln2_b"]
    return (loss, dx[None], *[results[n][0] for n in order], *[results[n][1] for n in order],
            *[results[n][2] for n in order], *[results[n][3] for n in order])
```

```python
import functools

import jax
import jax.numpy as jnp
from jax import lax
from jax.experimental import pallas as pl
from jax.experimental.pallas import tpu as pltpu

F32 = jnp.float32
_MXU = jnp.bfloat16
_HI = lax.Precision.HIGHEST
_MID = lax.Precision.HIGH

N_DEV = 8
D_MODEL = 1024
DEPTH = 4
DN_HEADS = 8
DN_HEAD_DIM = 128
DN_WIDTH = DN_HEADS * DN_HEAD_DIM
DN_CONV = 5
DN_CHUNK = 64
SW_HEADS = 16
SW_KV_HEADS = 4
SW_HEAD_DIM = 64
SW_GROUP = SW_HEADS // SW_KV_HEADS
SW_BLOCK = 128
SW_KV_WIDTH = SW_KV_HEADS * SW_HEAD_DIM
ROPE_THETA = 10000.0
FFN_HIDDEN = 2816
DN_ALPHA = (2.0 * DEPTH) ** 0.25
LN_EPS = 1e-5
RMS_EPS = 1e-6
ADAM_LR = 0.001
ADAM_B1 = 0.9
ADAM_B2 = 0.999
ADAM_EPS = 1e-08
ADAM_WD = 0.01
ADAM_STEP = 10

LANES = 128
N_HD = 2 * DN_HEADS
DN_GROUP = 4 * DN_CHUNK
INV_SUB = 16
LOCAL_ROWS_FWD = 2048
LOCAL_ROWS_BWD = 512
REC_ROWS = 512
ROW_TILE = 512
VMEM_LIMIT = 48 << 20

C_QKV, C_Z, C_QSW, C_GA, C_GB, C_KSW, C_VSW = 0, 3072, 4096, 5120, 6144, 7168, 7424
MAIN_COLS = 7680
R_QKV, R_Z, R_BA, R_QSW, R_KSW, R_VSW, R_G = 0, 3072, 4096, 4128, 5152, 5408, 5664
IN_COLS = 7712


_NN = ((1,), (0,))
_NT = ((1,), (1,))
_TN = ((0,), (0,))


def _dg(a, b, dims, precision):
    if precision is not None:
        return lax.dot_general(a, b, (dims, ((), ())), precision=precision, preferred_element_type=F32)
    return lax.dot_general(a.astype(_MXU), b.astype(_MXU), (dims, ((), ())), preferred_element_type=F32)


def _make_dots(hi):
    @jax.custom_vjp
    def nn(a, b):
        return _dg(a, b, _NN, hi)

    @jax.custom_vjp
    def nt(a, b):
        return _dg(a, b, _NT, hi)

    @jax.custom_vjp
    def tn(a, b):
        return _dg(a, b, _TN, hi)

    nn.defvjp(lambda a, b: (nn(a, b), (a, b)), lambda r, g: (nt(g, r[1]), tn(r[0], g)))
    nt.defvjp(lambda a, b: (nt(a, b), (a, b)), lambda r, g: (nn(g, r[1]), tn(g, r[0])))
    tn.defvjp(lambda a, b: (tn(a, b), (a, b)), lambda r, g: (nt(r[1], g), nn(r[0], g)))
    return nn, nt, tn


_bnn, _bnt, _btn = _make_dots(None)
_hnn, _hnt, _htn = _make_dots(_HI)
_mnn, _mnt, _mtn = _make_dots(_MID)


def _neumann(a, order):
    n = a.shape[0]
    eye = (lax.broadcasted_iota(jnp.int32, (n, n), 0) == lax.broadcasted_iota(jnp.int32, (n, n), 1)).astype(F32)
    inv = eye - a
    p = a
    span = 2
    while span < order:
        p = _mnn(p, p)
        inv = inv + _mnn(inv, p)
        span *= 2
    return inv


def _inv_unit(a, order):
    n = a.shape[0]
    ii = lax.broadcasted_iota(jnp.int32, (n, n), 0)
    jj = lax.broadcasted_iota(jnp.int32, (n, n), 1)
    near = (ii & -INV_SUB) == (jj & -INV_SUB)
    d_inv = _neumann(jnp.where(near, a, 0.0), INV_SUB)
    outer = _neumann(_mnn(d_inv, jnp.where(near, 0.0, a)), order // INV_SUB)
    return _mnn(outer, d_inv)


def _inv_unit_t(t, g):
    return -_mnt(_mtn(t, g), t)


def _silu(x):
    return x * jax.nn.sigmoid(x)


def _softplus(x):
    return jnp.maximum(x, 0.0) + jnp.log1p(jnp.exp(-jnp.abs(x)))


def _params(sem=None):
    kw = {"vmem_limit_bytes": VMEM_LIMIT}
    if sem is not None:
        kw["dimension_semantics"] = sem
    return pltpu.CompilerParams(**kw)


def _tile(dim, pref):
    if dim <= pref:
        return dim
    t = (pref // LANES) * LANES
    while t > LANES and dim % t:
        t -= LANES
    assert dim % t == 0, (dim, pref)
    return t


def _full(shape):
    zeros = (0,) * len(shape)
    return pl.BlockSpec(shape, lambda *_: zeros)


def _sds(shape, dtype=F32):
    return jax.ShapeDtypeStruct(shape, dtype)


def _mm(a, b, mode, *, name, add=None, tm=1536, tn=1536, tk=1536, out_dtype=F32):
    if mode == "nn":
        (m, k), (k2, n) = a.shape, b.shape
    elif mode == "nt":
        (m, k), (n, k2) = a.shape, b.shape
    else:
        (k, m), (k2, n) = a.shape, b.shape
    assert k == k2, (a.shape, b.shape, mode)
    tm, tn, tk = _tile(m, tm), _tile(n, tn), _tile(k, tk)
    nk = k // tk
    dims = {"nn": _NN, "nt": _NT, "tn": _TN}[mode]

    def body(*refs):
        if add is None:
            a_ref, b_ref, o_ref, acc = refs
        else:
            a_ref, b_ref, add_ref, o_ref, acc = refs
        kk = pl.program_id(2)

        @pl.when(kk == 0)
        def _():
            acc[...] = jnp.zeros_like(acc)

        acc[...] += _dg(a_ref[...], b_ref[...], dims, None)

        @pl.when(kk == nk - 1)
        def _():
            o_ref[...] = (acc[...] if add is None else acc[...] + add_ref[...]).astype(out_dtype)

    a_spec = pl.BlockSpec((tk, tm), lambda i, j, kk: (kk, i)) if mode == "tn" else pl.BlockSpec((tm, tk), lambda i, j, kk: (i, kk))
    b_spec = pl.BlockSpec((tn, tk), lambda i, j, kk: (j, kk)) if mode == "nt" else pl.BlockSpec((tk, tn), lambda i, j, kk: (kk, j))
    o_spec = pl.BlockSpec((tm, tn), lambda i, j, kk: (i, j))
    ins, specs = [a, b], [a_spec, b_spec]
    if add is not None:
        ins.append(add)
        specs.append(o_spec)
    return pl.pallas_call(
        body, name=name, grid=(m // tm, n // tn, nk), in_specs=specs, out_specs=o_spec,
        out_shape=_sds((m, n), out_dtype), scratch_shapes=[pltpu.VMEM((tm, tn), F32)],
        compiler_params=_params(("parallel", "parallel", "arbitrary")),
    )(*ins)


def _cols(width, start):
    assert start % width == 0
    return pl.BlockSpec((ROW_TILE, width), lambda i: (i, start // width))


def _rows(width):
    return pl.BlockSpec((ROW_TILE, width), lambda i: (i, 0))


def _accumulate(ref, value, step):
    @pl.when(step == 0)
    def _():
        ref[...] = value

    @pl.when(step != 0)
    def _():
        ref[...] += value


def _ln_fn(x, r, g, b):
    u = DN_ALPHA * x + r
    mu = jnp.mean(u, axis=-1, keepdims=True)
    var = jnp.mean(jnp.square(u - mu), axis=-1, keepdims=True)
    return (u - mu) * lax.rsqrt(var + LN_EPS) * g + b


def _ln_fwd(x, r, g, b, name):
    t, d = x.shape

    def body(x_ref, r_ref, g_ref, b_ref, o_ref, ob_ref):
        y = _ln_fn(x_ref[...], r_ref[...], g_ref[...], b_ref[...])
        o_ref[...] = y
        ob_ref[...] = y.astype(_MXU)

    return pl.pallas_call(
        body, name=name, grid=(t // ROW_TILE,), in_specs=[_rows(d), _rows(d), _full((1, d)), _full((1, d))],
        out_specs=[_rows(d), _rows(d)], out_shape=[_sds((t, d)), _sds((t, d), _MXU)], compiler_params=_params(("parallel",)),
    )(x, r, g, b)


def _ln_bwd(x, r, g, b, dy, name):
    t, d = x.shape

    def body(x_ref, r_ref, g_ref, b_ref, dy_ref, dx_ref, dr_ref, dg_ref, db_ref):
        _, vjp = jax.vjp(_ln_fn, x_ref[...], r_ref[...], g_ref[...], b_ref[...])
        dx, dr, dg, db = vjp(dy_ref[...])
        dx_ref[...] = dx
        dr_ref[...] = dr.astype(_MXU)
        _accumulate(dg_ref, dg, pl.program_id(0))
        _accumulate(db_ref, db, pl.program_id(0))

    return pl.pallas_call(
        body, name=name, grid=(t // ROW_TILE,),
        in_specs=[_rows(d), _rows(d), _full((1, d)), _full((1, d)), _rows(d)],
        out_specs=[_rows(d), _rows(d), _full((1, d)), _full((1, d))],
        out_shape=[_sds((t, d)), _sds((t, d), _MXU), _sds((1, d)), _sds((1, d))],
        compiler_params=_params(("arbitrary",)),
    )(x, r, g, b, dy)


def _merge_fn(ga, gb, ya, yb):
    return jax.nn.sigmoid(ga) * ya + jax.nn.sigmoid(gb) * yb


def _merge_fwd(proj, ya, yb):
    t, d = ya.shape

    def body(ga_ref, gb_ref, ya_ref, yb_ref, o_ref):
        o_ref[...] = _merge_fn(ga_ref[...], gb_ref[...], ya_ref[...], yb_ref[...]).astype(_MXU)

    return pl.pallas_call(
        body, name="merge_fwd", grid=(t // ROW_TILE,), in_specs=[_cols(d, C_GA), _cols(d, C_GB), _rows(d), _rows(d)],
        out_specs=_rows(d), out_shape=_sds((t, d), _MXU), compiler_params=_params(("parallel",)),
    )(proj, proj, ya, yb)


def _merge_bwd(proj, ya, yb, dm):
    t, d = ya.shape

    def body(ga_ref, gb_ref, ya_ref, yb_ref, dm_ref, dga_ref, dgb_ref, dya_ref, dyb_ref):
        _, vjp = jax.vjp(_merge_fn, ga_ref[...], gb_ref[...], ya_ref[...], yb_ref[...])
        dga_ref[...], dgb_ref[...], dya_ref[...], dyb_ref[...] = [g.astype(_MXU) for g in vjp(dm_ref[...])]

    return pl.pallas_call(
        body, name="merge_bwd", grid=(t // ROW_TILE,),
        in_specs=[_cols(d, C_GA), _cols(d, C_GB), _rows(d), _rows(d), _rows(d)],
        out_specs=[_rows(d)] * 4, out_shape=[_sds((t, d), _MXU)] * 4, compiler_params=_params(("parallel",)),
    )(proj, proj, ya, yb, dm)


def _swiglu_fn(gate, up):
    return _silu(gate) * up


def _swiglu_fwd(gu):
    t = gu.shape[0]
    f = FFN_HIDDEN
    rows = 256

    def body(gu_ref, o_ref):
        o_ref[...] = _swiglu_fn(gu_ref[:, :f], gu_ref[:, f:]).astype(_MXU)

    return pl.pallas_call(
        body, name="swiglu_fwd", grid=(t // rows,), in_specs=[pl.BlockSpec((rows, 2 * f), lambda i: (i, 0))],
        out_specs=pl.BlockSpec((rows, f), lambda i: (i, 0)), out_shape=_sds((t, f), _MXU), compiler_params=_params(("parallel",)),
    )(gu)


def _swiglu_bwd(gu, dh, carry=None):
    t = gu.shape[0]
    f = FFN_HIDDEN
    rows = 256

    def body(gu_ref, dh_ref, o_ref):
        _, vjp = jax.vjp(_swiglu_fn, gu_ref[:, :f], gu_ref[:, f:])
        o_ref[:, :f], o_ref[:, f:] = [g.astype(_MXU) for g in vjp(dh_ref[...])]

    return _maybe_carrying(
        carry, body, "swiglu_bwd", (t // rows,), [gu, dh],
        [pl.BlockSpec((rows, 2 * f), lambda i: (i, 0)), pl.BlockSpec((rows, f), lambda i: (i, 0))],
        [pl.BlockSpec((rows, 2 * f), lambda i: (i, 0))], [_sds((t, 2 * f), _MXU)])


def _loss_head(y, target):
    t, d = y.shape

    def body(y_ref, t_ref, s_ref, dy_ref):
        err = y_ref[...] - t_ref[...]
        dy_ref[...] = err / d
        _accumulate(s_ref, jnp.broadcast_to(jnp.sum(jnp.square(err)), (1, LANES)), pl.program_id(0))

    return pl.pallas_call(
        body, name="loss_head", grid=(t // ROW_TILE,), in_specs=[_rows(d), _rows(d)],
        out_specs=[_full((1, LANES)), _rows(d)], out_shape=[_sds((1, LANES)), _sds((t, d))],
        compiler_params=_params(("arbitrary",)),
    )(y, target)


def _shift_rows(x, s):
    if s == 0:
        return x
    return pltpu.roll(x, (-s) % x.shape[0], 0)


def _conv(x, w):
    half = DN_CONV // 2
    acc = None
    for k in range(DN_CONV):
        term = _shift_rows(x, k - half) * w[k:k + 1, :]
        acc = term if acc is None else acc + term
    return acc


def _act_norm(c, do_norm, scale):
    a = _silu(c)
    if not do_norm:
        return a
    return a * lax.rsqrt(jnp.sum(a * a, axis=-1, keepdims=True) + RMS_EPS) * scale


PREP_ROWS = 512
HALO = 8
_KINDS = ((True, DN_HEAD_DIM ** -0.5), (True, 1.0), (False, 1.0))


def _halo_rows(read, i, pr, t):
    lo, hi = i * pr - HALO, (i + 1) * pr + HALO
    parts = []
    if lo < 0:
        parts.append(jnp.zeros((HALO, LANES), F32))
    parts.append(read(max(lo, 0), min(hi, t)))
    if hi > t:
        parts.append(jnp.zeros((HALO, LANES), F32))
    return jnp.concatenate(parts, axis=0) if len(parts) > 1 else parts[0]


def _prep_fwd(proj, conv_w, kind):
    t = proj.shape[0]
    pr = min(PREP_ROWS, t)
    do_norm, scale = _KINDS[kind]
    blk = pl.BlockSpec((t, LANES), lambda j: (0, kind * DN_HEADS + j))

    def body(x_ref, w_ref, o_ref):
        w = w_ref[...]
        for i in range(t // pr):
            xx = _halo_rows(lambda lo, hi: x_ref[lo:hi, :], i, pr, t)
            c = _conv(xx, w)[HALO:HALO + pr, :]
            o_ref[i * pr:(i + 1) * pr, :] = _act_norm(c, do_norm, scale)

    return pl.pallas_call(
        body, name=f"prep_fwd_{kind}", grid=(DN_HEADS,),
        in_specs=[blk, pl.BlockSpec((8, LANES), lambda j: (0, kind * DN_HEADS + j))],
        out_specs=pl.BlockSpec((t, LANES), lambda j: (0, j)), out_shape=_sds((t, DN_WIDTH)),
        compiler_params=_params(("parallel",)),
    )(proj, conv_w)


def _prep_bwd(proj, conv_w, d2, kind):
    t = proj.shape[0]
    pr = min(PREP_ROWS, t)
    do_norm, scale = _KINDS[kind]
    half = DN_CONV // 2
    blk = pl.BlockSpec((t, LANES), lambda j: (0, kind * DN_HEADS + j))
    oblk = pl.BlockSpec((t, LANES), lambda j: (0, j))

    def body(x_ref, w_ref, d_ref, dx_ref, dw_ref):
        w = w_ref[...]
        own = slice(HALO, HALO + pr)
        dw = jnp.zeros((8, LANES), F32)
        for i in range(t // pr):
            xx = _halo_rows(lambda lo, hi: x_ref[lo:hi, :], i, pr, t)
            dn = _halo_rows(lambda lo, hi: d_ref[0, lo:hi, :] + d_ref[1, lo:hi, :], i, pr, t)
            _, vjp = jax.vjp(lambda c: _act_norm(c, do_norm, scale), _conv(xx, w))
            (dc,) = vjp(dn)
            dx = None
            rows = []
            for k in range(DN_CONV):
                term = _shift_rows(dc, half - k) * w[k:k + 1, :]
                dx = term if dx is None else dx + term
                rows.append(jnp.sum(dc[own, :] * _shift_rows(xx, k - half)[own, :], axis=0, keepdims=True))
            dx_ref[i * pr:(i + 1) * pr, :] = dx[own, :].astype(_MXU)
            dw = dw + jnp.concatenate(rows + [jnp.zeros((8 - DN_CONV, LANES), F32)], axis=0)
        dw_ref[...] = dw

    return pl.pallas_call(
        body, name=f"prep_bwd_{kind}", grid=(DN_HEADS,),
        in_specs=[blk, pl.BlockSpec((8, LANES), lambda j: (0, kind * DN_HEADS + j)), pl.BlockSpec((2, t, LANES), lambda j: (0, 0, j))],
        out_specs=[oblk, pl.BlockSpec((8, LANES), lambda j: (0, j))], out_shape=[_sds((t, DN_WIDTH), _MXU), _sds((8, DN_WIDTH))],
        compiler_params=_params(("parallel",)),
    )(proj, conv_w, d2)


def _gb_fn(ba, alog_row, dtb_row):
    c = DN_CHUNK
    lane = lax.broadcasted_iota(jnp.int32, (c, LANES), 1)
    ii = lax.broadcasted_iota(jnp.int32, (c, c), 0)
    jj = lax.broadcasted_iota(jnp.int32, (c, c), 1)
    beta = jax.nn.sigmoid(ba)
    g = -jnp.exp(alog_row) * _softplus(ba + dtb_row)
    g = jnp.where((lane >= N_HD) & (lane < 2 * N_HD), g, 0.0)
    gc_fwd = _hnn((ii >= jj).astype(F32), g)
    gc_rev = _hnn((ii <= jj).astype(F32), g)
    gc = jnp.where(lane < N_HD + DN_HEADS, gc_fwd, gc_rev)
    return jnp.where(lane < N_HD, beta, jnp.where(lane < 2 * N_HD, gc, 0.0))


def _per_head_spec():
    return pl.BlockSpec((N_HD, ROW_TILE, LANES), lambda i: (0, i, 0))


def _gb_fwd(ba, alog_row, dtb_row):
    t = ba.shape[0]
    n = ROW_TILE // DN_CHUNK

    def body(ba_ref, a_ref, d_ref, o_ref, beta_ref, gc_ref):
        for c in range(n):
            rows = slice(c * DN_CHUNK, (c + 1) * DN_CHUNK)
            out = _gb_fn(ba_ref[rows, :], a_ref[...], d_ref[...])
            o_ref[rows, :] = out
            for j in range(N_HD):
                beta_ref[j, rows, :] = jnp.broadcast_to(out[:, j:j + 1], (DN_CHUNK, LANES))
                gc_ref[j, rows, :] = jnp.broadcast_to(out[:, N_HD + j:N_HD + j + 1], (DN_CHUNK, LANES))

    per_head = _sds((N_HD, t, LANES))
    return pl.pallas_call(
        body, name="gates_fwd", grid=(t // ROW_TILE,), in_specs=[_rows(LANES), _full((1, LANES)), _full((1, LANES))],
        out_specs=[_rows(LANES), _per_head_spec(), _per_head_spec()], out_shape=[_sds((t, LANES)), per_head, per_head],
        compiler_params=_params(("parallel",)),
    )(ba, alog_row, dtb_row)


def _gb_bwd(ba, alog_row, dtb_row, dbeta, dgc, d_rows):
    t = ba.shape[0]
    n = ROW_TILE // DN_CHUNK

    def body(ba_ref, a_ref, d_ref, dbeta_ref, dgc_ref, dr_ref, dba_ref, dal_ref, ddt_ref):
        dal = jnp.zeros((1, LANES), F32)
        ddt = jnp.zeros((1, LANES), F32)
        lane = lax.broadcasted_iota(jnp.int32, (DN_CHUNK, LANES), 1)
        for c in range(n):
            rows = slice(c * DN_CHUNK, (c + 1) * DN_CHUNK)
            cot = dr_ref[rows, :]
            for j in range(N_HD):
                cot = jnp.where(lane == j, dbeta_ref[j, rows, :], cot)
                cot = jnp.where(lane == N_HD + j, dgc_ref[j, rows, :] + cot, cot)
            _, vjp = jax.vjp(_gb_fn, ba_ref[rows, :], a_ref[...], d_ref[...])
            dba, da, dd = vjp(cot)
            dba_ref[rows, :] = dba.astype(_MXU)
            dal = dal + da
            ddt = ddt + dd
        _accumulate(dal_ref, dal, pl.program_id(0))
        _accumulate(ddt_ref, ddt, pl.program_id(0))

    return pl.pallas_call(
        body, name="gates_bwd", grid=(t // ROW_TILE,),
        in_specs=[_rows(LANES), _full((1, LANES)), _full((1, LANES)), _per_head_spec(), _per_head_spec(), _rows(LANES)],
        out_specs=[_rows(LANES), _full((1, LANES)), _full((1, LANES))],
        out_shape=[_sds((t, LANES), _MXU), _sds((1, LANES)), _sds((1, LANES))], compiler_params=_params(("arbitrary",)),
    )(ba, alog_row, dtb_row, dbeta, dgc, d_rows)


def _dn_decay(gcc, gcr, sgn):
    c = DN_CHUNK
    ii = lax.broadcasted_iota(jnp.int32, (c, c), 0)
    jj = lax.broadcasted_iota(jnp.int32, (c, c), 1)
    d = (ii - jj) * sgn
    lower = d >= 0
    return jnp.where(lower, jnp.exp(jnp.where(lower, gcc - gcr, 0.0)), 0.0), d > 0


def _dn_a(k, beta, gcc, gcr, sgn):
    decay, strict = _dn_decay(gcc, gcr, sgn)
    return jnp.where(strict, beta * _bnt(k, k) * decay, 0.0)


def _dn_group(q, k, v, beta, gcc, gcr, sgn):
    n = DN_GROUP
    ii = lax.broadcasted_iota(jnp.int32, (n, n), 0)
    jj = lax.broadcasted_iota(jnp.int32, (n, n), 1)
    same = (ii & -DN_CHUNK) == (jj & -DN_CHUNK)
    d = (ii - jj) * sgn
    lower = same & (d >= 0)
    decay = jnp.where(lower, jnp.exp(jnp.where(lower, gcc - gcr, 0.0)), 0.0)
    a = jnp.where(same & (d > 0), beta * _bnt(k, k) * decay, 0.0)
    t_inv = _inv_unit(a, DN_CHUNK)
    u = _bnn(t_inv, v * beta)
    w = _bnn(t_inv, k * (beta * jnp.exp(gcc)))
    return u, w, _bnt(q, k) * decay, t_inv


def _dn_local(t_inv, q, k, v, beta, gcc, gcr, sgn):
    c = DN_CHUNK
    decay, _ = _dn_decay(gcc, gcr, sgn)
    eg = jnp.exp(gcc)
    u = _bnn(t_inv, v * beta)
    w = _bnn(t_inv, k * (beta * eg))
    qk = _bnt(q, k) * decay
    qd = q * eg
    last = jnp.where(sgn > 0, c - 1, 0)
    onehot = (lax.broadcasted_iota(jnp.int32, (c, 1), 0) == last).astype(F32)
    gl = jnp.sum(gcc * onehot, axis=0, keepdims=True)
    kd = k * jnp.exp(gl - gcc)
    egl = jnp.broadcast_to(jnp.exp(gl), (1, LANES))
    return u, w, qk, qd, kd, egl


def _hd_sign(hd):
    return jnp.where(hd < DN_HEADS, 1, -1).astype(jnp.int32)


def _head_of(hd):
    return jnp.where(hd < DN_HEADS, hd, hd - DN_HEADS)


def _dir_of(hd):
    return jnp.where(hd < DN_HEADS, 0, 1)


def _dn_specs(rows_step):
    nl = rows_step // DN_CHUNK
    wide = pl.BlockSpec((1, rows_step, LANES), lambda hd, i: (hd, i, 0))
    half = pl.BlockSpec((1, rows_step, DN_CHUNK), lambda hd, i: (hd, i, 0))
    col = wide
    row = pl.BlockSpec((1, nl, 1, DN_CHUNK), lambda hd, i: (hd, i, 0, 0))
    egl = pl.BlockSpec((1, nl, 1, LANES), lambda hd, i: (hd, i, 0, 0))
    return wide, half, col, row, egl


def _qkv_specs(rows_step):
    return [pl.BlockSpec((rows_step, LANES), lambda hd, i: (i, _head_of(hd)))] * 3


def _maybe_carrying(carry, body, name, grid, operands, in_specs, out_specs, out_shape):
    extra_scratch = []
    if carry is not None:
        kind, arrays = carry
        body, more_in, more_out, more_shape, extra_scratch = _carried(kind, arrays, body, len(operands), len(out_shape), grid)
        operands, in_specs = operands + list(arrays), in_specs + more_in
        out_specs, out_shape = out_specs + more_out, out_shape + more_shape
    sem = ("arbitrary",) * len(grid) if carry is not None else ("parallel",) * len(grid)
    return pl.pallas_call(
        body, name=name, grid=grid, in_specs=in_specs, out_specs=out_specs, out_shape=out_shape,
        scratch_shapes=extra_scratch, compiler_params=_params(sem),
    )(*operands)


def _dn_local_fwd(q, k, v, beta_c, gc_c, gc_r, carry=None):
    t = q.shape[0]
    nc = t // DN_CHUNK
    rows_step = min(LOCAL_ROWS_FWD, t)
    wide, half, col, row, egl = _dn_specs(rows_step)

    ng = rows_step // DN_GROUP
    per = DN_GROUP // DN_CHUNK
    grow = pl.BlockSpec((1, ng, 1, DN_GROUP), lambda hd, i: (hd, i, 0, 0))

    def body(q_ref, k_ref, v_ref, b_ref, gc_ref, gg_ref, u_ref, w_ref, qk_ref, qd_ref, kd_ref, egl_ref, t_ref):
        sgn = _hd_sign(pl.program_id(0))
        last = jnp.where(sgn > 0, DN_CHUNK - 1, 0)
        onehot = (lax.broadcasted_iota(jnp.int32, (DN_CHUNK, 1), 0) == last).astype(F32)
        groups = lambda a: a.reshape((ng, DN_GROUP) + a.shape[1:])
        q_all, k_all, gcc_all = q_ref[...], k_ref[...], gc_ref[0][:, :1]
        u, w, qk, t_inv = jax.vmap(functools.partial(_dn_group, sgn=sgn))(
            groups(q_all), groups(k_all), groups(v_ref[...]), groups(b_ref[0][:, :1]), groups(gcc_all), gg_ref[0])
        u_ref[0] = u.reshape(rows_step, LANES)
        w_ref[0] = w.reshape(rows_step, LANES).astype(_MXU)
        qd_ref[0] = (q_all * jnp.exp(gcc_all)).astype(_MXU)
        for gi in range(ng):
            for c in range(per):
                blk = slice(c * DN_CHUNK, (c + 1) * DN_CHUNK)
                rows = slice(gi * DN_GROUP + c * DN_CHUNK, gi * DN_GROUP + (c + 1) * DN_CHUNK)
                qk_ref[0, rows, :] = qk[gi, blk, blk].astype(_MXU)
                t_ref[0, rows, :] = t_inv[gi, blk, blk]
                gl = jnp.sum(gcc_all[rows, :] * onehot, axis=0, keepdims=True)
                kd_ref[0, rows, :] = (k_all[rows, :] * jnp.exp(gl - gcc_all[rows, :])).astype(_MXU)
                egl_ref[0, gi * per + c] = jnp.broadcast_to(jnp.exp(gl), (1, LANES))

    big = _sds((N_HD, t, LANES))
    small = _sds((N_HD, t, DN_CHUNK))
    operands = [q, k, v, beta_c, gc_c, gc_r.reshape(N_HD, t // DN_GROUP, 1, DN_GROUP)]
    return _maybe_carrying(
        carry, body, "dn_local_fwd", (N_HD, t // rows_step), operands, _qkv_specs(rows_step) + [col, col, grow],
        [wide, wide, half, wide, wide, egl, half],
        [big, _sds(big.shape, _MXU), _sds(small.shape, _MXU), _sds(big.shape, _MXU), _sds(big.shape, _MXU),
         _sds((N_HD, nc, 1, LANES)), small])


def _dn_local_bwd(q, k, v, beta_c, gc_c, gc_r, t_inv, du, dw, dqk, dqd, dkd, degl, carry=None):
    t = q.shape[0]
    nc = t // DN_CHUNK
    rows_step = min(LOCAL_ROWS_BWD, t)
    nl = rows_step // DN_CHUNK
    wide, half, col, row, egl = _dn_specs(rows_step)
    dspec = pl.BlockSpec((1, rows_step, LANES), lambda hd, i: (_dir_of(hd), i, _head_of(hd)))

    def body(q_ref, k_ref, v_ref, b_ref, gc_ref, gr_ref, t_ref, du_ref, dw_ref, dqk_ref, dqd_ref, dkd_ref, degl_ref,
             dq_ref, dk_ref, dv_ref, db_ref, dgc_ref, dgr_ref):
        sgn = _hd_sign(pl.program_id(0))

        def chunk_bwd(tinv, q, k, v, beta, gcc, gcr, du, dw, dqk, dqd, dkd, degl):
            _, vjp = jax.vjp(functools.partial(_dn_local, sgn=sgn), tinv, q, k, v, beta, gcc, gcr)
            dt, dq, dk, dv, db, dgc, dgr = vjp((du, dw, dqk, dqd, dkd, degl))
            _, vjp_a = jax.vjp(functools.partial(_dn_a, sgn=sgn), k, beta, gcc, gcr)
            dk2, db2, dgc2, dgr2 = vjp_a(_inv_unit_t(tinv, dt))
            return dq, dk + dk2, dv, db + db2, dgc + dgc2, dgr + dgr2

        chunks = lambda a: a.reshape((nl, DN_CHUNK) + a.shape[1:])
        dq, dk, dv, db, dgc, dgr = jax.vmap(chunk_bwd)(
            chunks(t_ref[0]), chunks(q_ref[...]), chunks(k_ref[...]), chunks(v_ref[...]), chunks(b_ref[0][:, :1]),
            chunks(gc_ref[0][:, :1]),
            gr_ref[0], chunks(du_ref[0]), chunks(dw_ref[0]), chunks(dqk_ref[0]), chunks(dqd_ref[0]), chunks(dkd_ref[0]),
            degl_ref[0])
        dq_ref[0] = dq.reshape(rows_step, LANES)
        dk_ref[0] = dk.reshape(rows_step, LANES)
        dv_ref[0] = dv.reshape(rows_step, LANES)
        db_ref[0] = jnp.broadcast_to(db.reshape(rows_step, 1), (rows_step, LANES))
        dgc_ref[0] = jnp.broadcast_to(dgc.reshape(rows_step, 1), (rows_step, LANES))
        dgr_ref[0] = dgr

    per_dir = _sds((2, t, DN_WIDTH))
    return _maybe_carrying(
        carry, body, "dn_local_bwd", (N_HD, t // rows_step), [q, k, v, beta_c, gc_c, gc_r, t_inv, du, dw, dqk, dqd, dkd, degl],
        _qkv_specs(rows_step) + [col, col, row, half, wide, wide, half, wide, wide, egl], [dspec, dspec, dspec, col, col, row],
        [per_dir, per_dir, per_dir, _sds((N_HD, t, LANES)), _sds((N_HD, t, LANES)), _sds((N_HD, nc, 1, DN_CHUNK))])


REC_HEADS = 8
REC_GROUPS = N_HD // REC_HEADS
REC_FWD_GROUPS = DN_HEADS // REC_HEADS


def _rec_specs(time_block):
    nr = REC_ROWS // DN_CHUNK
    wide = pl.BlockSpec((REC_HEADS, REC_ROWS, LANES), lambda g, b: (g, time_block(g, b), 0))
    half = pl.BlockSpec((REC_HEADS, REC_ROWS, DN_CHUNK), lambda g, b: (g, time_block(g, b), 0))
    egl = pl.BlockSpec((REC_HEADS, nr, 1, LANES), lambda g, b: (g, time_block(g, b), 0, 0))
    state = pl.BlockSpec((REC_HEADS, nr, DN_HEAD_DIM, DN_HEAD_DIM), lambda g, b: (g, time_block(g, b), 0, 0))
    return wide, half, egl, state


def _rec_head_cols(g):
    return jnp.where(g < REC_FWD_GROUPS, g, g - REC_FWD_GROUPS)


def _dn_rec_fwd(u, w, qk, qd, kd, egl):
    t = u.shape[1]
    nb = t // REC_ROWS
    nr = REC_ROWS // DN_CHUNK
    nc = t // DN_CHUNK

    def time_block(g, b):
        return jnp.where(g < REC_FWD_GROUPS, b, nb - 1 - b)

    wide, half, egl_spec, state = _rec_specs(time_block)
    o_spec = pl.BlockSpec((1, REC_ROWS, REC_HEADS * LANES),
                          lambda g, b: (jnp.where(g < REC_FWD_GROUPS, 0, 1), time_block(g, b), _rec_head_cols(g)))

    def body(u_ref, w_ref, qk_ref, qd_ref, kd_ref, egl_ref, o_ref, vn_ref, s_ref, s_scr):
        fwd = pl.program_id(0) < REC_FWD_GROUPS

        @pl.when(pl.program_id(1) == 0)
        def _():
            s_scr[...] = jnp.zeros_like(s_scr)

        def run(order):
            heads = range(REC_HEADS)
            s = [s_scr[j] for j in heads]
            for ce in order:
                rows = slice(ce * DN_CHUNK, (ce + 1) * DN_CHUNK)
                vn = [u_ref[j, rows, :] - _bnn(w_ref[j, rows, :], s[j]) for j in heads]
                o = [_bnn(qd_ref[j, rows, :], s[j]) + _bnn(qk_ref[j, rows, :], vn[j]) for j in heads]
                nxt = [s[j] * egl_ref[j, ce] + _btn(kd_ref[j, rows, :], vn[j]) for j in heads]
                for j in heads:
                    s_ref[j, ce] = s[j]
                    vn_ref[j, rows, :] = vn[j].astype(_MXU)
                    o_ref[0, rows, j * LANES:(j + 1) * LANES] = o[j]
                s = nxt
            for j in heads:
                s_scr[j] = s[j]

        pl.when(fwd)(lambda: run(range(nr)))
        pl.when(jnp.logical_not(fwd))(lambda: run(range(nr - 1, -1, -1)))

    return pl.pallas_call(
        body, name="dn_rec_fwd", grid=(REC_GROUPS, nb), in_specs=[wide, wide, half, wide, wide, egl_spec],
        out_specs=[o_spec, wide, state],
        out_shape=[_sds((2, t, DN_WIDTH)), _sds((N_HD, t, LANES), _MXU), _sds((N_HD, nc, DN_HEAD_DIM, DN_HEAD_DIM))],
        scratch_shapes=[pltpu.VMEM((REC_HEADS, DN_HEAD_DIM, DN_HEAD_DIM), F32)],
        compiler_params=_params(("parallel", "arbitrary")),
    )(u, w, qk, qd, kd, egl)


def _dn_rec_bwd(w, qk, qd, kd, egl, vn, states, do):
    t = w.shape[1]
    nb = t // REC_ROWS
    nr = REC_ROWS // DN_CHUNK
    nc = t // DN_CHUNK

    def time_block(g, b):
        return jnp.where(g < REC_FWD_GROUPS, nb - 1 - b, b)

    wide, half, egl_spec, state = _rec_specs(time_block)
    do_spec = pl.BlockSpec((REC_ROWS, REC_HEADS * LANES), lambda g, b: (time_block(g, b), _rec_head_cols(g)))

    def body(w_ref, qk_ref, qd_ref, kd_ref, egl_ref, vn_ref, s_ref, do_ref,
             du_ref, dw_ref, dqk_ref, dqd_ref, dkd_ref, degl_ref, ds_scr):
        fwd = pl.program_id(0) < REC_FWD_GROUPS

        @pl.when(pl.program_id(1) == 0)
        def _():
            ds_scr[...] = jnp.zeros_like(ds_scr)

        def run(order):
            heads = range(REC_HEADS)
            ds = [ds_scr[j] for j in heads]
            for ce in order:
                rows = slice(ce * DN_CHUNK, (ce + 1) * DN_CHUNK)
                s = [s_ref[j, ce] for j in heads]
                do_c = [do_ref[rows, j * LANES:(j + 1) * LANES] for j in heads]
                vn_c = [vn_ref[j, rows, :] for j in heads]
                dvn = [_btn(qk_ref[j, rows, :], do_c[j]) + _bnn(kd_ref[j, rows, :], ds[j]) for j in heads]
                nxt = [ds[j] * egl_ref[j, ce] + _btn(qd_ref[j, rows, :], do_c[j]) - _btn(w_ref[j, rows, :], dvn[j])
                       for j in heads]
                for j in heads:
                    du_ref[j, rows, :] = dvn[j]
                    dw_ref[j, rows, :] = -_bnt(dvn[j], s[j])
                for j in heads:
                    dqk_ref[j, rows, :] = _bnt(do_c[j], vn_c[j])
                    dqd_ref[j, rows, :] = _bnt(do_c[j], s[j])
                for j in heads:
                    dkd_ref[j, rows, :] = _bnt(vn_c[j], ds[j])
                    degl_ref[j, ce] = jnp.sum(s[j] * ds[j], axis=0, keepdims=True)
                ds = nxt
            for j in heads:
                ds_scr[j] = ds[j]

        pl.when(fwd)(lambda: run(range(nr - 1, -1, -1)))
        pl.when(jnp.logical_not(fwd))(lambda: run(range(nr)))

    big = _sds((N_HD, t, LANES))
    return pl.pallas_call(
        body, name="dn_rec_bwd", grid=(REC_GROUPS, nb), in_specs=[wide, half, wide, wide, egl_spec, wide, state, do_spec],
        out_specs=[wide, wide, half, wide, wide, egl_spec],
        out_shape=[big, big, _sds((N_HD, t, DN_CHUNK)), big, big, _sds((N_HD, nc, 1, LANES))],
        scratch_shapes=[pltpu.VMEM((REC_HEADS, DN_HEAD_DIM, DN_HEAD_DIM), F32)],
        compiler_params=_params(("parallel", "arbitrary")),
    )(w, qk, qd, kd, egl, vn, states, do)


def _post_fn(of, ob, z, gain):
    o = of + ob
    return o * lax.rsqrt(jnp.mean(o * o, axis=-1, keepdims=True) + RMS_EPS) * gain * _silu(z)


def _post_specs():
    o_spec = [pl.BlockSpec((1, ROW_TILE, DN_WIDTH), functools.partial(lambda i, d: (d, i, 0), d=d)) for d in (0, 1)]
    return o_spec, _cols(DN_WIDTH, C_Z), _rows(DN_WIDTH), _full((1, LANES))


def _post_fwd(o2, proj, gain):
    t = proj.shape[0]
    o_spec, z_spec, wide, gain_spec = _post_specs()

    def body(of_ref, ob_ref, z_ref, g_ref, out_ref):
        for h in range(DN_HEADS):
            cols = slice(h * LANES, (h + 1) * LANES)
            out_ref[:, cols] = _post_fn(of_ref[0, :, cols], ob_ref[0, :, cols], z_ref[:, cols], g_ref[...]).astype(_MXU)

    return pl.pallas_call(
        body, name="post_fwd", grid=(t // ROW_TILE,), in_specs=o_spec + [z_spec, gain_spec], out_specs=wide,
        out_shape=_sds((t, DN_WIDTH), _MXU), compiler_params=_params(("parallel",)),
    )(o2, o2, proj, gain)


def _post_bwd(o2, proj, gain, dout):
    t = proj.shape[0]
    o_spec, z_spec, wide, gain_spec = _post_specs()

    def body(of_ref, ob_ref, z_ref, g_ref, d_ref, do_ref, dz_ref, dg_ref):
        dg_sum = jnp.zeros((1, LANES), F32)
        for h in range(DN_HEADS):
            cols = slice(h * LANES, (h + 1) * LANES)
            _, vjp = jax.vjp(_post_fn, of_ref[0, :, cols], ob_ref[0, :, cols], z_ref[:, cols], g_ref[...])
            do, _, dz, dg = vjp(d_ref[:, cols])
            do_ref[:, cols] = do
            dz_ref[:, cols] = dz.astype(_MXU)
            dg_sum = dg_sum + dg
        _accumulate(dg_ref, dg_sum, pl.program_id(0))

    return pl.pallas_call(
        body, name="post_bwd", grid=(t // ROW_TILE,), in_specs=o_spec + [z_spec, gain_spec, wide],
        out_specs=[wide, wide, gain_spec], out_shape=[_sds((t, DN_WIDTH)), _sds((t, DN_WIDTH), _MXU), _sds((1, LANES))],
        compiler_params=_params(("arbitrary",)),
    )(o2, o2, proj, gain, dout)


def _rope(x, cos, sin):
    lane = lax.broadcasted_iota(jnp.int32, x.shape, 1)
    first = (lane & (SW_HEAD_DIM - 1)) < SW_HEAD_DIM // 2
    rot = jnp.where(first, -pltpu.roll(x, LANES - SW_HEAD_DIM // 2, 1), pltpu.roll(x, SW_HEAD_DIM // 2, 1))
    return x * cos + rot * sin


def _rope_apply(q, k, q_cols, k_cols, cos, sin, name, dtype):
    t = cos.shape[0]
    qw, kw = SW_HEADS * SW_HEAD_DIM, SW_KV_WIDTH

    def body(q_ref, k_ref, c_ref, s_ref, qo_ref, ko_ref):
        c, s = c_ref[...], s_ref[...]
        for j in range(qw // LANES):
            cols = slice(j * LANES, (j + 1) * LANES)
            qo_ref[:, cols] = _rope(q_ref[:, cols], c, s).astype(dtype)
        for j in range(kw // LANES):
            cols = slice(j * LANES, (j + 1) * LANES)
            ko_ref[:, cols] = _rope(k_ref[:, cols], c, s).astype(dtype)

    return pl.pallas_call(
        body, name=name, grid=(t // ROW_TILE,), in_specs=[_cols(qw, q_cols), _cols(kw, k_cols), _rows(LANES), _rows(LANES)],
        out_specs=[_rows(qw), _rows(kw)], out_shape=[_sds((t, qw), dtype), _sds((t, kw), dtype)],
        compiler_params=_params(("parallel",)),
    )(q, k, cos, sin)


def _attn_core(qs, kb, vb, sink, mask):
    s = _bnt(qs, kb) * (SW_HEAD_DIM ** -0.5)
    s = jnp.where(mask, s, -1e30)
    m = lax.stop_gradient(jnp.maximum(jnp.max(s, axis=1, keepdims=True), sink))
    e = jnp.exp(s - m)
    den = jnp.sum(e, axis=1, keepdims=True) + jnp.exp(sink - m)
    return _bnn(e / den, vb)


def _band_mask(n, nb):
    rows = SW_GROUP * SW_BLOCK
    i = lax.broadcasted_iota(jnp.int32, (rows, 3 * SW_BLOCK), 0) & (SW_BLOCK - 1)
    j = lax.broadcasted_iota(jnp.int32, (rows, 3 * SW_BLOCK), 1)
    near = (j - i >= 0) & (j - i <= 2 * SW_BLOCK)
    lo = jnp.where(n == 0, SW_BLOCK, 0)
    hi = jnp.where(n == nb - 1, 2 * SW_BLOCK, 3 * SW_BLOCK)
    return near & (j >= lo) & (j < hi)


def _band_specs(nb, v_cols):
    def spec(width, base, shift):
        return pl.BlockSpec((SW_BLOCK, width), lambda n: (jnp.clip(n + shift, 0, nb - 1), base // width))
    k_specs = [spec(SW_KV_WIDTH, 0, s) for s in (-1, 0, 1)]
    v_specs = [spec(SW_KV_WIDTH, v_cols, s) for s in (-1, 0, 1)]
    return k_specs, v_specs


def _head_cols(kv, g):
    h = kv * SW_GROUP + g
    return slice(h * SW_HEAD_DIM, (h + 1) * SW_HEAD_DIM)


def _kv_batches(q_ref, kb, vb, s_ref):
    kvs = range(SW_KV_HEADS)
    cols = lambda kv: slice(kv * SW_HEAD_DIM, (kv + 1) * SW_HEAD_DIM)
    qs = jnp.stack([jnp.concatenate([q_ref[:, _head_cols(kv, g)] for g in range(SW_GROUP)], axis=0) for kv in kvs])
    sinks = jnp.stack([jnp.concatenate([jnp.broadcast_to(s_ref[kv * SW_GROUP + g], (SW_BLOCK, 1)) for g in range(SW_GROUP)],
                                       axis=0) for kv in kvs])
    return qs, jnp.stack([kb[:, cols(kv)] for kv in kvs]), jnp.stack([vb[:, cols(kv)] for kv in kvs]), sinks


def _attn_fwd(qr, kr, proj, sinks):
    t = qr.shape[0]
    nb = t // SW_BLOCK
    qw = SW_HEADS * SW_HEAD_DIM
    k_specs, v_specs = _band_specs(nb, C_VSW)
    q_spec = pl.BlockSpec((SW_BLOCK, qw), lambda n: (n, 0))

    def body(q_ref, k0, k1, k2, v0, v1, v2, s_ref, o_ref):
        mask = _band_mask(pl.program_id(0), nb)
        kb = jnp.concatenate([k0[...], k1[...], k2[...]], axis=0)
        vb = jnp.concatenate([v0[...], v1[...], v2[...]], axis=0)
        qs, kbs, vbs, sinks_ = _kv_batches(q_ref, kb, vb, s_ref)
        o = jax.vmap(functools.partial(_attn_core, mask=mask))(qs, kbs, vbs, sinks_)
        for kv in range(SW_KV_HEADS):
            for g in range(SW_GROUP):
                o_ref[:, _head_cols(kv, g)] = o[kv, g * SW_BLOCK:(g + 1) * SW_BLOCK, :].astype(_MXU)

    return pl.pallas_call(
        body, name="attn_fwd", grid=(nb,), in_specs=[q_spec] + k_specs + v_specs + [_full((SW_HEADS, 1, 1))],
        out_specs=q_spec, out_shape=_sds((t, qw), _MXU), compiler_params=_params(("parallel",)),
    )(qr, kr, kr, kr, proj, proj, proj, sinks)


def _attn_bwd(qr, kr, proj, sinks, do):
    t = qr.shape[0]
    nb = t // SW_BLOCK
    qw = SW_HEADS * SW_HEAD_DIM
    k_specs, v_specs = _band_specs(nb, C_VSW)
    q_spec = pl.BlockSpec((SW_BLOCK, qw), lambda n: (n, 0))
    part = pl.BlockSpec((1, 3 * SW_BLOCK, SW_KV_WIDTH), lambda n: (n, 0, 0))

    def body(q_ref, k0, k1, k2, v0, v1, v2, s_ref, do_ref, dq_ref, dk_ref, dv_ref, ds_ref):
        mask = _band_mask(pl.program_id(0), nb)
        kb = jnp.concatenate([k0[...], k1[...], k2[...]], axis=0).astype(F32)
        vb = jnp.concatenate([v0[...], v1[...], v2[...]], axis=0)

        @pl.when(pl.program_id(0) == 0)
        def _():
            ds_ref[...] = jnp.zeros_like(ds_ref)

        qs, kbs, vbs, sinks_ = _kv_batches(q_ref, kb, vb, s_ref)
        dos = jnp.stack([jnp.concatenate([do_ref[:, _head_cols(kv, g)] for g in range(SW_GROUP)], axis=0)
                         for kv in range(SW_KV_HEADS)])

        def head_bwd(q_, k_, v_, sink_, do_):
            _, vjp = jax.vjp(functools.partial(_attn_core, mask=mask), q_, k_, v_, sink_)
            return vjp(do_)

        dqs, dkb, dvb, dsink = jax.vmap(head_bwd)(qs.astype(F32), kbs, vbs, sinks_, dos)
        for kv in range(SW_KV_HEADS):
            kvc = slice(kv * SW_HEAD_DIM, (kv + 1) * SW_HEAD_DIM)
            dk_ref[0, :, kvc] = dkb[kv]
            dv_ref[0, :, kvc] = dvb[kv]
            for g in range(SW_GROUP):
                rows = slice(g * SW_BLOCK, (g + 1) * SW_BLOCK)
                dq_ref[:, _head_cols(kv, g)] = dqs[kv, rows, :]
                ds_ref[kv * SW_GROUP + g] += jnp.sum(dsink[kv, rows, :], axis=0, keepdims=True)

    parts = _sds((nb, 3 * SW_BLOCK, SW_KV_WIDTH))
    return pl.pallas_call(
        body, name="attn_bwd", grid=(nb,), in_specs=[q_spec] + k_specs + v_specs + [_full((SW_HEADS, 1, 1)), q_spec],
        out_specs=[q_spec, part, part, _full((SW_HEADS, 1, 1))], out_shape=[_sds((t, qw)), parts, parts, _sds((SW_HEADS, 1, 1))],
        compiler_params=_params(("arbitrary",)),
    )(qr, kr, kr, kr, proj, proj, proj, sinks, do)


def _band_sum(parts, name, dtype):
    nb = parts.shape[0]
    w = parts.shape[2]

    def spec(shift, slot):
        return pl.BlockSpec((1, SW_BLOCK, w), lambda m: (jnp.clip(m + shift, 0, nb - 1), slot, 0))

    def body(prev_ref, own_ref, next_ref, o_ref):
        m = pl.program_id(0)
        total = own_ref[0] + jnp.where(m > 0, prev_ref[0], 0.0) + jnp.where(m < nb - 1, next_ref[0], 0.0)
        o_ref[...] = total.astype(dtype)

    return pl.pallas_call(
        body, name=name, grid=(nb,), in_specs=[spec(-1, 2), spec(0, 1), spec(1, 0)],
        out_specs=pl.BlockSpec((SW_BLOCK, w), lambda m: (m, 0)), out_shape=_sds((nb * SW_BLOCK, w), dtype),
        compiler_params=_params(("parallel",)),
    )(parts, parts, parts)


def _gate_rows(gbo):
    t = gbo.shape[0]
    return gbo[:, N_HD:2 * N_HD].T.reshape(N_HD, t // DN_CHUNK, 1, DN_CHUNK)


def _gate_rows_t(dgc_r):
    t = dgc_r.shape[1] * DN_CHUNK
    return jnp.pad(dgc_r.reshape(N_HD, t).T, ((0, 0), (N_HD, LANES - 2 * N_HD)))


def _layer_fwd(x, xb, w, cos, sin, carry=None):
    proj = _mm(xb, w["wm"], "nn", name="proj")
    ba = _mm(xb, w["wba"], "nn", name="proj_gates")
    qn, kn, vv = [_prep_fwd(proj, w["conv"], kind) for kind in range(3)]
    gbo, beta_c, gc_c = _gb_fwd(ba, w["alog"], w["dtb"])
    gc_r = _gate_rows(gbo)
    u, wk, qk, qd, kd, egl, tinv, *carried = _dn_local_fwd(qn, kn, vv, beta_c, gc_c, gc_r, carry=carry)
    o2, vn, states = _dn_rec_fwd(u, wk, qk, qd, kd, egl)
    o_dn = _post_fwd(o2, proj, w["dnw"])
    qr, kr = _rope_apply(proj, proj, C_QSW, C_KSW, cos, sin, "rope_fwd", _MXU)
    o_sw = _attn_fwd(qr, kr, proj, w["sinks"])
    ya = _mm(o_dn, w["wa"], "nn", name="branch_a")
    yb = _mm(o_sw, w["wb"], "nn", name="branch_b")
    merged = _merge_fwd(proj, ya, yb)
    mix = _mm(merged, w["wo"], "nn", name="mix_out")
    x1, x1b = _ln_fwd(x, mix, w["ln1g"], w["ln1b"], "ln1_fwd")
    gu = _mm(x1b, w["wgu"], "nn", name="ffn_up")
    h = _swiglu_fwd(gu)
    f = _mm(h, w["wd"], "nn", name="ffn_down")
    x2, x2b = _ln_fwd(x1, f, w["ln2g"], w["ln2b"], "ln2_fwd")
    res = dict(x=x, xb=xb, proj=proj, ba=ba, qn=qn, kn=kn, vv=vv, beta_c=beta_c, gc_c=gc_c, gc_r=gc_r, wk=wk, qk=qk, qd=qd, kd=kd, egl=egl, tinv=tinv, vn=vn,
               states=states, o2=o2, o_dn=o_dn, qr=qr, kr=kr, o_sw=o_sw, ya=ya, yb=yb, merged=merged, mix=mix, x1=x1, x1b=x1b,
               gu=gu, h=h, f=f)
    return x2, x2b, res, (carried or None)


def _layer_bwd(dx2, w, r, cos, sin, carry=None):
    dx1, df, dln2g, dln2b = _ln_bwd(r["x1"], r["f"], w["ln2g"], w["ln2b"], dx2, "ln2_bwd")
    dh = _mm(df, w["wd"], "nt", name="d_ffn_hidden")
    dwd = _mm(r["h"], df, "tn", name="dw_ffn_down", out_dtype=_MXU)
    dgu, *swapped = _swiglu_bwd(r["gu"], dh, carry=(_SiblingSwap, carry[0]) if carry else None)
    carry = (_ChipExchange, _pair_sums(carry[0], swapped, carry[1])) if carry else None
    dwgu = _mm(r["x1b"], dgu, "tn", name="dw_ffn_up", out_dtype=_MXU)
    dx1 = _mm(dgu, w["wgu"], "nt", name="dx_ffn", add=dx1)
    dx, dmix, dln1g, dln1b = _ln_bwd(r["x"], r["mix"], w["ln1g"], w["ln1b"], dx1, "ln1_bwd")
    dmerged = _mm(dmix, w["wo"], "nt", name="d_merged")
    dwo = _mm(r["merged"], dmix, "tn", name="dw_mix_out", out_dtype=_MXU)
    dga, dgb, dya, dyb = _merge_bwd(r["proj"], r["ya"], r["yb"], dmerged)
    dwa = _mm(r["o_dn"], dya, "tn", name="dw_branch_a", out_dtype=_MXU)
    do_dn = _mm(dya, w["wa"], "nt", name="d_branch_a")
    dwb = _mm(r["o_sw"], dyb, "tn", name="dw_branch_b", out_dtype=_MXU)
    do_sw = _mm(dyb, w["wb"], "nt", name="d_branch_b")
    do, dz, ddnw = _post_bwd(r["o2"], r["proj"], w["dnw"], do_dn)
    du, dwk, dqk, dqd, dkd, degl = _dn_rec_bwd(r["wk"], r["qk"], r["qd"], r["kd"], r["egl"], r["vn"], r["states"], do)
    dq3, dk3, dv3, dbeta_c, dgc_c, dgc_r, *carried = _dn_local_bwd(r["qn"], r["kn"], r["vv"], r["beta_c"], r["gc_c"], r["gc_r"],
                                                                   r["tinv"], du, dwk, dqk, dqd, dkd, degl, carry=carry)
    dqkv, dconv = zip(*[_prep_bwd(r["proj"], w["conv"], d2, kind) for kind, d2 in enumerate((dq3, dk3, dv3))])
    dconv = jnp.concatenate(dconv, axis=1)
    dba, dalog, ddtb = _gb_bwd(r["ba"], w["alog"], w["dtb"], dbeta_c, dgc_c, _gate_rows_t(dgc_r))
    dqr, dkparts, dvparts, dsinks = _attn_bwd(r["qr"], r["kr"], r["proj"], w["sinks"], do_sw)
    dkr = _band_sum(dkparts, "attn_dk_sum", F32)
    dv = _band_sum(dvparts, "attn_dv_sum", _MXU)
    dq_sw, dk_sw = _rope_apply(dqr, dkr, 0, 0, cos, -sin, "rope_bwd", _MXU)
    dproj = jnp.concatenate([*dqkv, dz, dq_sw, dga, dgb, dk_sw, dv], axis=1)
    dwm = _mm(r["xb"], dproj, "tn", name="dw_proj", out_dtype=_MXU)
    dwba = _mm(r["xb"], dba, "tn", name="dw_proj_gates", out_dtype=_MXU)
    dx = _mm(dproj, w["wm"], "nt", name="dx_proj", add=dx)
    dx = _mm(dba, w["wba"], "nt", name="dx_proj_gates", add=dx)
    grads = dict(wm=dwm, wba=dwba, conv=dconv, alog=dalog, dtb=ddtb, dnw=ddnw, sinks=dsinks, wa=dwa, wb=dwb, wo=dwo,
                 ln1g=dln1g, ln1b=dln1b, wgu=dwgu, wd=dwd, ln2g=dln2g, ln2b=dln2b)
    return dx, grads, (carried or None)


def _rope_tables(t):
    half = SW_HEAD_DIM // 2
    inv_freq = ROPE_THETA ** (-jnp.arange(half, dtype=F32) / half)
    ang = jnp.arange(t, dtype=F32)[:, None] * inv_freq[None, :]
    return jnp.tile(jnp.cos(ang), (1, LANES // half)), jnp.tile(jnp.sin(ang), (1, LANES // half))


def _trunk(x, target, n_layers, layer_weights, fwd_carry, grads_done):
    cos, sin = _rope_tables(x.shape[0])
    xb = x.astype(_MXU)
    saved, weights, carried = [], [], None
    for i in range(n_layers):
        w = layer_weights(i, carried)
        x, xb, res, carried = _layer_fwd(x, xb, w, cos, sin, carry=fwd_carry(i))
        saved.append(res)
        weights.append(w)
    sq, dx = _loss_head(x, target)
    carry = None
    for i in reversed(range(n_layers)):
        dx, grads, carried = _layer_bwd(dx, weights[i], saved[i], cos, sin, carry=carry)
        carry = grads_done(i, grads, carried)
    return sq, dx, carry


N_CHIPS = 4


def _mesh_pos():
    return lax.axis_index("x"), lax.axis_index("y"), lax.axis_index("c")


def _other_chips(x, y):
    return [(1 - x, y), (x, 1 - y), (1 - x, 1 - y)]


def _remote_copy(src, dst, sems, k, to):
    send_sems, recv_sems, base = sems
    return pltpu.make_async_remote_copy(src_ref=src, dst_ref=dst, send_sem=send_sems.at[base + k],
                                        recv_sem=recv_sems.at[base + k], device_id=to, device_id_type=pl.DeviceIdType.MESH)


def _exchange_sems(n_arrays, per_array):
    return [pltpu.SemaphoreType.DMA((n_arrays * per_array,)), pltpu.SemaphoreType.DMA((n_arrays * per_array,)),
            pltpu.SemaphoreType.DMA((n_arrays,))]


def _comm_call(body, name, out_shapes, per_array, operands):
    n = len(operands)
    hbm = pl.BlockSpec(memory_space=pl.ANY)

    def flat_body(*refs):
        body(refs[:n], refs[n:2 * n], *refs[2 * n:])

    return pl.pallas_call(
        flat_body, name=name, in_specs=[hbm] * n, out_specs=[hbm] * n, out_shape=list(out_shapes),
        scratch_shapes=_exchange_sems(n, per_array), compiler_params=pltpu.CompilerParams(has_side_effects=True),
    )(*operands)


class _Gather:
    n_sems = N_DEV - 1

    @staticmethod
    def out_shape(block):
        return _sds((N_DEV,) + block.shape, block.dtype)

    @staticmethod
    def _own(x_ref, o_ref, sems, local_sem):
        x, y, c = _mesh_pos()
        mine = o_ref.at[4 * x + 2 * y + c]
        first = [_remote_copy(x_ref, mine, sems, 0, (x, y, 1 - c))]
        first += [_remote_copy(x_ref, mine, sems, 1 + j, (*chip, c)) for j, chip in enumerate(_other_chips(x, y))]
        return pltpu.make_async_copy(x_ref, mine, local_sem), first

    @classmethod
    def start(cls, x_ref, o_ref, sems, local_sem):
        mine, first = cls._own(x_ref, o_ref, sems, local_sem)
        mine.start()
        for cp in first:
            cp.start()

    @classmethod
    def finish(cls, x_ref, o_ref, sems, local_sem):
        x, y, c = _mesh_pos()
        sibling = (x, y, 1 - c)
        chips = _other_chips(x, y)
        slot = lambda px, py, pc: o_ref.at[4 * px + 2 * py + pc]
        mine, first = cls._own(x_ref, o_ref, sems, local_sem)
        passed = [_remote_copy(slot(*chip, c), slot(*chip, c), sems, 4 + j, sibling) for j, chip in enumerate(chips)]
        for j, chip in enumerate(chips):
            _remote_copy(x_ref, slot(*chip, c), sems, 1 + j, sibling).wait_recv()
            passed[j].start()
        _remote_copy(x_ref, slot(x, y, 1 - c), sems, 0, sibling).wait_recv()
        for j, chip in enumerate(chips):
            _remote_copy(x_ref, slot(*chip, 1 - c), sems, 4 + j, sibling).wait_recv()
        for cp in first + passed:
            cp.wait_send()
        mine.wait()


class _ChipExchange:
    n_sems = N_CHIPS - 1

    @staticmethod
    def out_shape(parts):
        return _sds(parts.shape, parts.dtype)

    @staticmethod
    def _own(x_ref, o_ref, sems, local_sem):
        x, y, c = _mesh_pos()
        me = 2 * x + y
        sent = [_remote_copy(x_ref.at[2 * cx + cy], o_ref.at[me], sems, j, (cx, cy, c))
                for j, (cx, cy) in enumerate(_other_chips(x, y))]
        return pltpu.make_async_copy(x_ref.at[me], o_ref.at[me], local_sem), sent

    @classmethod
    def start(cls, x_ref, o_ref, sems, local_sem):
        mine, sent = cls._own(x_ref, o_ref, sems, local_sem)
        mine.start()
        for cp in sent:
            cp.start()

    @classmethod
    def finish(cls, x_ref, o_ref, sems, local_sem):
        x, y, c = _mesh_pos()
        mine, sent = cls._own(x_ref, o_ref, sems, local_sem)
        for j, (cx, cy) in enumerate(_other_chips(x, y)):
            _remote_copy(x_ref.at[2 * x + y], o_ref.at[2 * cx + cy], sems, j, (cx, cy, c)).wait_recv()
        for cp in sent:
            cp.wait_send()
        mine.wait()


def _run_exchange(kind, phase, x_refs, o_refs, send_sems, recv_sems, local_sems):
    for i, (x_ref, o_ref) in enumerate(zip(x_refs, o_refs)):
        getattr(kind, phase)(x_ref, o_ref, (send_sems, recv_sems, i * kind.n_sems), local_sems.at[i])


def _exchange_alone(kind, operands, name):
    def body(x_refs, o_refs, *sems):
        _run_exchange(kind, "start", x_refs, o_refs, *sems)
        _run_exchange(kind, "finish", x_refs, o_refs, *sems)

    return _comm_call(body, name, [kind.out_shape(a) for a in operands], kind.n_sems, operands)


def _all_gather(block, name):
    return _exchange_alone(_Gather, [block], name)[0]


def _carried(kind, operands, body, n_in, n_out, grid):
    hbm = pl.BlockSpec(memory_space=pl.ANY)
    n_x = len(operands)

    def wrapped(*refs):
        ins, x_refs = refs[:n_in], refs[n_in:n_in + n_x]
        outs = refs[n_in + n_x:n_in + n_x + n_out]
        o_refs = refs[n_in + n_x + n_out:n_in + 2 * n_x + n_out]
        sems = refs[n_in + 2 * n_x + n_out:n_in + 2 * n_x + n_out + 3]
        rest = refs[n_in + 2 * n_x + n_out + 3:]
        first, last = None, None
        for axis, size in enumerate(grid):
            at0, at1 = pl.program_id(axis) == 0, pl.program_id(axis) == size - 1
            first = at0 if first is None else first & at0
            last = at1 if last is None else last & at1
        pl.when(first)(lambda: _run_exchange(kind, "start", x_refs, o_refs, *sems))
        body(*ins, *outs, *rest)
        pl.when(last)(lambda: _run_exchange(kind, "finish", x_refs, o_refs, *sems))

    return wrapped, [hbm] * n_x, [hbm] * n_x, [kind.out_shape(a) for a in operands], _exchange_sems(n_x, kind.n_sems)


class _SiblingSwap:
    n_sems = N_CHIPS

    @staticmethod
    def out_shape(parts):
        return _sds((N_CHIPS,) + parts.shape[1:], parts.dtype)

    @staticmethod
    def _copies(x_ref, o_ref, sems):
        x, y, c = _mesh_pos()
        return [_remote_copy(x_ref.at[2 * q + (1 - c)], o_ref.at[q], sems, q, (x, y, 1 - c)) for q in range(N_CHIPS)]

    @classmethod
    def start(cls, x_ref, o_ref, sems, local_sem):
        for cp in cls._copies(x_ref, o_ref, sems):
            cp.start()

    @classmethod
    def finish(cls, x_ref, o_ref, sems, local_sem):
        for cp in cls._copies(x_ref, o_ref, sems):
            cp.wait()


def _sibling_swap(parts, name):
    return _exchange_alone(_SiblingSwap, parts, name)


def _pair_sum(a, b, name):
    n, rows, cols = a.shape
    tr = rows if rows <= 512 else _row_tile(rows, 2048)
    blk = pl.BlockSpec((1, tr, cols), lambda q, i: (q, i, 0))

    def body(a_ref, b_ref, o_ref):
        o_ref[...] = (a_ref[...].astype(F32) + b_ref[...].astype(F32)).astype(o_ref.dtype)

    return pl.pallas_call(
        body, name=name, grid=(n, rows // tr), in_specs=[blk, blk], out_specs=blk, out_shape=_sds(a.shape, a.dtype),
        compiler_params=_params(("parallel", "parallel")),
    )(a, b)


def _chip_sums(parts, name):
    return _pair_sums(parts, _sibling_swap(parts, "swap_" + name), name)


def _pair_sums(parts, from_sibling, name):
    c = lax.axis_index("c")
    own = [lax.dynamic_index_in_dim(p.reshape((N_CHIPS, 2) + p.shape[1:]), c, axis=1, keepdims=False) for p in parts]
    return [_pair_sum(a, b, f"pair_sum_{name}_{k}") for k, (a, b) in enumerate(zip(own, from_sibling))]


def _reduce_to_owner(parts, name):
    return _exchange_alone(_ChipExchange, _chip_sums([parts], name), "exchange_" + name)[0]


def _sum_adamw(parts, w, m, v, name):
    rows, cols = w.shape
    n_parts = parts.shape[0]
    tr = rows if rows <= 512 else _row_tile(rows)
    blk = pl.BlockSpec((tr, cols), lambda i: (i, 0))

    def body(p_ref, w_ref, m_ref, v_ref, g_ref, d_ref, nm_ref, nv_ref):
        g = p_ref[0].astype(F32)
        for i in range(1, n_parts):
            g = g + p_ref[i].astype(F32)
        nm = ADAM_B1 * m_ref[...] + (1.0 - ADAM_B1) * g
        nv = ADAM_B2 * v_ref[...] + (1.0 - ADAM_B2) * jnp.square(g)
        m_hat = nm / (1.0 - ADAM_B1 ** ADAM_STEP)
        v_hat = nv / (1.0 - ADAM_B2 ** ADAM_STEP)
        g_ref[...] = g
        d_ref[...] = -ADAM_LR * (m_hat / (jnp.sqrt(v_hat) + ADAM_EPS) + ADAM_WD * w_ref[...])
        nm_ref[...] = nm
        nv_ref[...] = nv

    return pl.pallas_call(
        body, name=name, grid=(rows // tr,), in_specs=[pl.BlockSpec((n_parts, tr, cols), lambda i: (0, i, 0)), blk, blk, blk],
        out_specs=[blk] * 4, out_shape=[_sds((rows, cols))] * 4, compiler_params=_params(("parallel",)),
    )(parts, w, m, v)


def _row_tile(rows, pref=256):
    t = pref
    while t >= 8:
        if rows % t == 0:
            return t
        t //= 2
    return rows


def _gathered_cols(g):
    g = jnp.moveaxis(g, 0, -2)
    return g.reshape(g.shape[:-2] + (g.shape[-2] * g.shape[-1],))


def _gathered_rows(g):
    g = jnp.moveaxis(g, 0, -3)
    return g.reshape(g.shape[:-3] + (g.shape[-3] * g.shape[-2], g.shape[-1]))


def _col_parts(full):
    c = full.shape[-1]
    return jnp.moveaxis(full.reshape(full.shape[:-1] + (N_DEV, c // N_DEV)), -2, 0)


def _row_parts(full):
    rows, c = full.shape[-2:]
    return jnp.moveaxis(full.reshape(full.shape[:-2] + (N_DEV, rows // N_DEV, c)), -3, 0)


def _w_in_split(w_in):
    s = lambda a, n: w_in[..., a:a + n]
    main = jnp.concatenate([s(R_QKV, 3072), s(R_Z, 1024), s(R_QSW, 1024), s(R_G, 2048), s(R_KSW, 256), s(R_VSW, 256)], axis=-1)
    gates = jnp.pad(s(R_BA, 2 * N_HD), [(0, 0)] * (w_in.ndim - 1) + [(0, LANES - 2 * N_HD)])
    return main, gates


def _w_in_join(dmain, dgates):
    s = lambda a, n: dmain[..., a:a + n]
    return jnp.concatenate([s(C_QKV, 3072), s(C_Z, 1024), dgates[..., :2 * N_HD], s(C_QSW, 1024), s(C_KSW, 256), s(C_VSW, 256),
                            s(C_GA, 2048)], axis=-1)


def _lane_row(a, offset):
    l, n = a.shape
    return jnp.pad(a, ((0, 0), (offset, LANES - offset - n)))[:, None, :]


def kernel(x, w_in, conv_w, a_log, dt_bias, dn_norm_w, sinks, w_branch_a, w_branch_b, w_out, ln1_g, ln1_b, w_gate_up, w_down, ln2_g, ln2_b, loss_target, m_w_in, m_conv_w, m_a_log, m_dt_bias, m_dn_norm_w, m_sinks, m_w_branch_a, m_w_branch_b, m_w_out, m_ln1_g, m_ln1_b, m_w_gate_up, m_w_down, m_ln2_g, m_ln2_b, v_w_in, v_conv_w, v_a_log, v_dt_bias, v_dn_norm_w, v_sinks, v_w_branch_a, v_w_branch_b, v_w_out, v_ln1_g, v_ln1_b, v_w_gate_up, v_w_down, v_ln2_g, v_ln2_b):
    l = DEPTH
    bf = lambda a: a.astype(_MXU)
    shards = [bf(w_in), bf(w_gate_up), bf(w_branch_a), bf(w_branch_b), bf(w_out), bf(w_down)]
    first = _exchange_alone(_Gather, [s[0] for s in shards], "gather_layer_0")
    conv_full = _gathered_cols(_all_gather(conv_w, "gather_conv_w"))
    row = lambda a: a[:, None, :]
    small = dict(
        conv=jnp.pad(conv_full, ((0, 0), (0, 8 - DN_CONV), (0, 0))), alog=_lane_row(a_log.reshape(l, N_HD), N_HD),
        dtb=_lane_row(dt_bias.reshape(l, N_HD), N_HD), dnw=row(dn_norm_w), sinks=sinks.reshape(l, SW_HEADS, 1, 1),
        ln1g=row(ln1_g), ln1b=row(ln1_b), ln2g=row(ln2_g), ln2b=row(ln2_b))

    def layer_weights(i, carried):
        s_in, s_gu, s_a, s_b, s_o, s_d = first if i == 0 else carried
        wm, wba = _w_in_split(_gathered_cols(s_in))
        return dict(wm=wm, wba=wba, wgu=_gathered_cols(s_gu), wa=_gathered_rows(s_a), wb=_gathered_rows(s_b),
                    wo=_gathered_rows(s_o), wd=_gathered_rows(s_d), **{k: a[i] for k, a in small.items()})

    def fwd_carry(i):
        return (_Gather, [s[i + 1] for s in shards]) if i + 1 < l else None

    layer_grads, received, waiting = [None] * l, [None] * l, []

    def grads_done(i, g_i, carried):
        if waiting:
            received[waiting.pop()] = carried
        layer_grads[i] = g_i
        parts = [_col_parts(_w_in_join(g_i["wm"], g_i["wba"])), _col_parts(g_i["wgu"]), _row_parts(g_i["wa"]),
                 _row_parts(g_i["wb"]), _row_parts(g_i["wo"]), _row_parts(g_i["wd"])]
        waiting.append(i)
        return parts, f"layer_{i}"

    sq, dx, last = _trunk(x[0], loss_target[0], l, layer_weights, fwd_carry, grads_done)
    received[waiting.pop()] = _exchange_alone(_ChipExchange, _chip_sums(*last), "exchange_layer_0")
    loss = lax.psum(0.5 * sq[0, 0] / D_MODEL, ("x", "y", "c"))
    g = {k: jnp.stack([gi[k] for gi in layer_grads]) for k in small}

    def adamw(parts, w, m, v, name):
        rows = w.shape[0] * w.shape[1]
        flat = lambda a: a.reshape(rows, a.shape[-1])
        outs = _sum_adamw(parts.reshape(parts.shape[0], rows, w.shape[-1]), flat(w), flat(m), flat(v), "adamw_" + name)
        return [o.reshape(w.shape) for o in outs]

    got = [jnp.stack(per_layer, axis=1) for per_layer in zip(*received)]
    dconv = _reduce_to_owner(_col_parts(g["conv"][:, :DN_CONV, :]).reshape(N_DEV, l * DN_CONV, -1), "conv_w")
    results = {
        "w_in": adamw(got[0], w_in, m_w_in, v_w_in, "w_in"),
        "conv_w": adamw(dconv.reshape(N_CHIPS, l, DN_CONV, -1), conv_w, m_conv_w, v_conv_w, "conv_w"),
        "w_branch_a": adamw(got[2], w_branch_a, m_w_branch_a, v_w_branch_a, "w_branch_a"),
        "w_branch_b": adamw(got[3], w_branch_b, m_w_branch_b, v_w_branch_b, "w_branch_b"),
        "w_out": adamw(got[4], w_out, m_w_out, v_w_out, "w_out"),
        "w_gate_up": adamw(got[1], w_gate_up, m_w_gate_up, v_w_gate_up, "w_gate_up"),
        "w_down": adamw(got[5], w_down, m_w_down, v_w_down, "w_down"),
    }

    small_w = {"a_log": a_log.reshape(l, N_HD), "dt_bias": dt_bias.reshape(l, N_HD), "dn_norm_w": dn_norm_w, "sinks": sinks,
               "ln1_g": ln1_g, "ln1_b": ln1_b, "ln2_g": ln2_g, "ln2_b": ln2_b}
    small_m = {"a_log": m_a_log, "dt_bias": m_dt_bias, "dn_norm_w": m_dn_norm_w, "sinks": m_sinks, "ln1_g": m_ln1_g,
               "ln1_b": m_ln1_b, "ln2_g": m_ln2_g, "ln2_b": m_ln2_b}
    small_v = {"a_log": v_a_log, "dt_bias": v_dt_bias, "dn_norm_w": v_dn_norm_w, "sinks": v_sinks, "ln1_g": v_ln1_g,
               "ln1_b": v_ln1_b, "ln2_g": v_ln2_g, "ln2_b": v_ln2_b}
    small_g = {"a_log": g["alog"][:, 0, N_HD:2 * N_HD], "dt_bias": g["dtb"][:, 0, N_HD:2 * N_HD], "dn_norm_w": g["dnw"][:, 0, :],
               "sinks": g["sinks"].reshape(l, SW_HEADS), "ln1_g": g["ln1g"][:, 0, :], "ln1_b": g["ln1b"][:, 0, :],
               "ln2_g": g["ln2g"][:, 0, :], "ln2_b": g["ln2b"][:, 0, :]}
    names = list(small_w)
    cat = lambda d: jnp.concatenate([d[n].reshape(l, -1) for n in names], axis=1)
    widths = [small_w[n].shape[1] for n in names]
    total = sum(widths)
    padded = -(-total // LANES) * LANES
    pad = lambda a: jnp.pad(a, ((0, 8 - l), (0, padded - total)))
    got = _all_gather(pad(cat(small_g)), "gather_small_grads")
    outs = _sum_adamw(got, pad(cat(small_w)), pad(cat({n: small_m[n].reshape(l, -1) for n in names})),
                      pad(cat({n: small_v[n].reshape(l, -1) for n in names})), "adamw_small")
    off = 0
    for n, wd_ in zip(names, widths):
        shape = {"a_log": a_log.shape, "dt_bias": dt_bias.shape}.get(n, small_w[n].shape)
        results[n] = [o[:l, off:off + wd_].reshape(shape) for o in outs]
        off += wd_

    order = ["w_in", "conv_w", "a_log", "dt_bias", "dn_norm_w", "sinks", "w_branch_a", "w_branch_b", "w_out", "ln1_g", "ln1_b",
             "w_gate_up", "w_down", "ln2_g", "ln2_b"]
    return (loss, dx[None], *[results[n][0] for n in order], *[results[n][1] for n in order],
            *[results[n][2] for n in order], *[results[n][3] for n in order])
```
